```python
import jax, jax.numpy as jnp
from jax import lax
import numpy as np

D_MODEL = 1024
BATCH = 8
SEQ = 2048
DEPTH = 1
DEC_BATCH = 128
DEC_SEQ = 1
PAST_LEN = 16384
PAGE_SIZE = 128

N_MEM = 256
SSM_HEADS = 16
SSM_HEAD_DIM = 64
SSM_INNER = SSM_HEADS * SSM_HEAD_DIM
SSM_STATE = 128
SSM_GROUPS = 4
SSM_CONV = 4
SSM_CONV_DIM = SSM_INNER + 2 * SSM_GROUPS * SSM_STATE
SSD_CHUNK = 128
SC_WIDTH = D_MODEL
SC_CONV = 3
IN_PROJ_DIM = SSM_INNER + SSM_CONV_DIM + SSM_HEADS + 3 * SC_WIDTH + 2 * D_MODEL
XA_HEADS = 4
XA_HEAD_DIM = D_MODEL // XA_HEADS
N_EXPERTS = 32
TOP_K = 4
D_FF = D_MODEL
SWIGLU_LIMIT = 7.0
SWIGLU_ALPHA = 1.702
MOE_BLOCK = 128
EPS = 1e-6

kernel_name = 'hybrid_ssd_shortconv_memxattn_moe_step'


def rmsnorm(x, g):
    xf = x.astype(jnp.float32)
    y = xf * lax.rsqrt(jnp.mean(xf * xf, axis=-1, keepdims=True) + EPS)
    return (y * g.astype(jnp.float32)).astype(x.dtype)


def causal_dwconv(buf, u, w, bias):
    k_width = w.shape[0]
    t_len = u.shape[1]
    full = jnp.concatenate([buf.astype(u.dtype), u], axis=1)
    out = sum(full[:, k:k + t_len, :] * w[k] for k in range(k_width))
    if bias is not None:
        out = out + bias
    return out, full[:, full.shape[1] - (k_width - 1):, :]


def ssd_chunked(xs, dt, a, bs, cs):
    b, seq_len = xs.shape[:2]
    q = SSD_CHUNK
    pad = (-seq_len) % q
    if pad:
        xs = jnp.pad(xs, ((0, 0), (0, pad), (0, 0), (0, 0)))
        dt = jnp.pad(dt, ((0, 0), (0, pad), (0, 0)))
        bs = jnp.pad(bs, ((0, 0), (0, pad), (0, 0), (0, 0)))
        cs = jnp.pad(cs, ((0, 0), (0, pad), (0, 0), (0, 0)))
    n_chunks = (seq_len + pad) // q
    rep = SSM_HEADS // SSM_GROUPS
    bh = jnp.repeat(bs, rep, axis=2).reshape(b, n_chunks, q, SSM_HEADS, SSM_STATE)
    ch = jnp.repeat(cs, rep, axis=2).reshape(b, n_chunks, q, SSM_HEADS, SSM_STATE)
    xdt = (xs * dt[..., None]).reshape(b, n_chunks, q, SSM_HEADS, SSM_HEAD_DIM)
    a_cum = jnp.cumsum((dt * a).reshape(b, n_chunks, q, SSM_HEADS), axis=2)
    seg = a_cum[:, :, :, None, :] - a_cum[:, :, None, :, :]
    mask = jnp.tril(jnp.ones((q, q), dtype=bool))[None, None, :, :, None]
    decay = jnp.exp(jnp.where(mask, seg, -jnp.inf))
    scores = jnp.einsum('bclhn,bcshn->bclsh', ch, bh)
    y_diag = jnp.einsum('bclsh,bcshp->bclhp', scores * decay, xdt)
    decay_to_end = jnp.exp(a_cum[:, :, -1:, :] - a_cum)
    chunk_states = jnp.einsum('bclhn,bclh,bclhp->bchpn', bh, decay_to_end, xdt)
    chunk_decay = jnp.exp(a_cum[:, :, -1, :])

    def step(s, inp):
        st, dc = inp
        return dc[:, :, None, None] * s + st, s

    s0 = jnp.zeros((b, SSM_HEADS, SSM_HEAD_DIM, SSM_STATE), jnp.float32)
    s_final, s_in = lax.scan(step, s0, (jnp.moveaxis(chunk_states, 1, 0), jnp.moveaxis(chunk_decay, 1, 0)))
    s_in = jnp.moveaxis(s_in, 0, 1)
    y_off = jnp.einsum('bclhn,bchpn,bclh->bclhp', ch, s_in, jnp.exp(a_cum))
    y = (y_diag + y_off).reshape(b, n_chunks * q, SSM_HEADS, SSM_HEAD_DIM)[:, :seq_len]
    return y, s_final


def ssd_recurrent(xs, dt, a, bs, cs, s0):
    rep = SSM_HEADS // SSM_GROUPS

    def step(s, inp):
        xt, dtt, bt, ct = inp
        bt = jnp.repeat(bt, rep, axis=1)
        ct = jnp.repeat(ct, rep, axis=1)
        s = jnp.exp(dtt * a)[:, :, None, None] * s + jnp.einsum('bh,bhp,bhn->bhpn', dtt, xt, bt)
        return s, jnp.einsum('bhpn,bhn->bhp', s, ct)

    s_final, ys = lax.scan(step, s0, (jnp.moveaxis(xs, 1, 0), jnp.moveaxis(dt, 1, 0),
                                      jnp.moveaxis(bs, 1, 0), jnp.moveaxis(cs, 1, 0)))
    return jnp.moveaxis(ys, 0, 1), s_final


def gated_group_rmsnorm(y, z, g):
    b, t_len, _ = y.shape
    u = (y * jax.nn.silu(z.astype(jnp.float32))).reshape(b, t_len, SSM_GROUPS, SSM_INNER // SSM_GROUPS)
    u = u * lax.rsqrt(jnp.mean(u * u, axis=-1, keepdims=True) + EPS)
    return u.reshape(b, t_len, SSM_INNER) * g.astype(jnp.float32)


def mixer_block(hn, mconv_buf, sconv_buf, ssm_state, lw, prompt):
    b, t_len, _ = hn.shape
    proj = hn @ lw['w_in']
    sizes = (SSM_INNER, SSM_CONV_DIM, SSM_HEADS, SC_WIDTH, SC_WIDTH, SC_WIDTH, D_MODEL)
    cuts = [int(c) for c in np.cumsum(sizes)]
    z, xbc, dt_raw, sc_b, sc_c, sc_v, g_a, g_b = jnp.split(proj, cuts, axis=-1)
    if mconv_buf is None:
        mconv_buf = jnp.zeros((b, SSM_CONV - 1, SSM_CONV_DIM), hn.dtype)
    xbc, new_mbuf = causal_dwconv(mconv_buf, xbc, lw['w_mconv'], lw['b_mconv'])
    xbc = jax.nn.silu(xbc)
    xs, bs, cs = jnp.split(xbc, [SSM_INNER, SSM_INNER + SSM_GROUPS * SSM_STATE], axis=-1)
    xs = xs.reshape(b, t_len, SSM_HEADS, SSM_HEAD_DIM).astype(jnp.float32)
    bs = bs.reshape(b, t_len, SSM_GROUPS, SSM_STATE).astype(jnp.float32)
    cs = cs.reshape(b, t_len, SSM_GROUPS, SSM_STATE).astype(jnp.float32)
    dt = jax.nn.softplus(dt_raw.astype(jnp.float32) + lw['dt_bias'].astype(jnp.float32))
    a = -jnp.exp(lw['a_log'].astype(jnp.float32))
    if prompt:
        y, s_new = ssd_chunked(xs, dt, a, bs, cs)
    else:
        y, s_new = ssd_recurrent(xs, dt, a, bs, cs, ssm_state.astype(jnp.float32))
    y = y + lw['d_skip'].astype(jnp.float32)[:, None] * xs
    y_a = gated_group_rmsnorm(y.reshape(b, t_len, SSM_INNER), z, lw['norm_ssm']).astype(hn.dtype)
    if sconv_buf is None:
        sconv_buf = jnp.zeros((b, SC_CONV - 1, SC_WIDTH), hn.dtype)
    u, new_sbuf = causal_dwconv(sconv_buf, sc_c * sc_v, lw['w_sconv'], None)
    y_b = sc_b * u
    merged = jax.nn.sigmoid(g_a) * y_a + jax.nn.sigmoid(g_b) * y_b
    return merged @ lw['w_out'], new_mbuf, new_sbuf, s_new.astype(hn.dtype)


def mem_kv(mem, lw):
    b, m, _ = mem.shape
    mn = rmsnorm(mem, lw['norm_mem'])
    k = (mn @ lw['w_xk']).reshape(b, m, XA_HEADS, XA_HEAD_DIM)
    v = (mn @ lw['w_xv']).reshape(b, m, XA_HEADS, XA_HEAD_DIM)
    return k, v


def cross_attn(hn, k, v, w_q, w_o):
    b, t_len, _ = hn.shape
    q = (hn @ w_q).reshape(b, t_len, XA_HEADS, XA_HEAD_DIM)
    s = jnp.einsum('bthd,bmhd->bhtm', q, k.astype(q.dtype)).astype(jnp.float32) * (XA_HEAD_DIM ** -0.5)
    p = jax.nn.softmax(s, axis=-1).astype(hn.dtype)
    o = jnp.einsum('bhtm,bmhd->bthd', p, v.astype(hn.dtype)).reshape(b, t_len, D_MODEL)
    return o @ w_o


def expert_ffn(xb, w_gu, b_gu, w_dn, b_dn):
    gu = xb @ w_gu + b_gu
    gate, up = jnp.split(gu, 2, axis=-1)
    gate = jnp.minimum(gate, SWIGLU_LIMIT)
    up = jnp.clip(up, -SWIGLU_LIMIT, SWIGLU_LIMIT)
    glu = gate * jax.nn.sigmoid(SWIGLU_ALPHA * gate)
    return ((up + 1.0) * glu) @ w_dn + b_dn


def moe(hn, lw):
    b, t_len, _ = hn.shape
    x = hn.reshape(-1, D_MODEL)
    n_tok = x.shape[0]
    logits = (x @ lw['w_router']).astype(jnp.float32) + lw['b_router'].astype(jnp.float32)
    top_v, top_i = lax.top_k(logits, TOP_K)
    gates = jax.nn.softmax(top_v, axis=-1)
    n_assign = n_tok * TOP_K
    flat_e = top_i.reshape(-1).astype(jnp.int32)
    flat_g = gates.reshape(-1)
    flat_t = jnp.repeat(jnp.arange(n_tok, dtype=jnp.int32), TOP_K)
    order = jnp.argsort(flat_e, stable=True)
    se, st, sg = flat_e[order], flat_t[order], flat_g[order]
    counts = jnp.zeros((N_EXPERTS,), jnp.int32).at[flat_e].add(1)
    padded = (counts + MOE_BLOCK - 1) // MOE_BLOCK * MOE_BLOCK
    start_sorted = jnp.cumsum(counts) - counts
    pad_end = jnp.cumsum(padded)
    start_padded = pad_end - padded
    dest = start_padded[se] + jnp.arange(n_assign, dtype=jnp.int32) - start_sorted[se]
    n_blocks = -(-(n_assign + N_EXPERTS * (MOE_BLOCK - 1)) // MOE_BLOCK)
    n_rows = n_blocks * MOE_BLOCK
    row_tok = jnp.zeros((n_rows,), jnp.int32).at[dest].set(st)
    row_gate = jnp.zeros((n_rows,), jnp.float32).at[dest].set(sg)
    block_start = jnp.arange(n_blocks, dtype=jnp.int32) * MOE_BLOCK
    block_e = jnp.minimum(jnp.searchsorted(pad_end, block_start, side='right'), N_EXPERTS - 1).astype(jnp.int32)
    xs = x[row_tok].reshape(n_blocks, MOE_BLOCK, D_MODEL)

    def run_block(args):
        xb, e = args
        return expert_ffn(xb, lw['w_gate_up'][e], lw['b_gate_up'][e], lw['w_down'][e], lw['b_down'][e])

    ys = lax.map(run_block, (xs, block_e)).reshape(n_rows, D_MODEL)
    out = jnp.zeros_like(x).at[row_tok].add(ys * row_gate[:, None].astype(ys.dtype))
    return out.reshape(b, t_len, D_MODEL)


def decoder_layer(h, mem_k, mem_v, mconv_buf, sconv_buf, ssm_state, lw, prompt):
    a, new_mbuf, new_sbuf, s_new = mixer_block(rmsnorm(h, lw['norm_mix']), mconv_buf, sconv_buf, ssm_state, lw, prompt)
    h = h + a
    h = h + cross_attn(rmsnorm(h, lw['norm_xattn']), mem_k, mem_v, lw['w_xq'], lw['w_xo'])
    h = h + moe(rmsnorm(h, lw['norm_moe']), lw)
    return h, new_mbuf, new_sbuf, s_new


def setup_inputs(seed: int = 0) -> dict:
    key = jax.random.key(seed)
    ks = iter(jax.random.split(key, 40))
    f32 = jnp.float32

    def nrm(shape, scale):
        return jax.random.normal(next(ks), shape, f32) * scale

    def gain(shape):
        return 1.0 + nrm(shape, 0.02)

    dt0 = jnp.exp(jax.random.uniform(next(ks), (DEPTH, SSM_HEADS), f32, minval=np.log(1e-3), maxval=np.log(1e-1)))
    return {
        'x_prompt': nrm((BATCH, SEQ, D_MODEL), 1.0),
        'x_sample': nrm((DEC_BATCH, DEC_SEQ, D_MODEL), 1.0),
        'mem_prompt': nrm((BATCH, N_MEM, D_MODEL), 1.0),
        'state_ssm': nrm((DEPTH, DEC_BATCH, SSM_HEADS, SSM_HEAD_DIM, SSM_STATE), 0.5),
        'state_mamba_conv': nrm((DEPTH, DEC_BATCH, SSM_CONV - 1, SSM_CONV_DIM), 1.0),
        'state_short_conv': nrm((DEPTH, DEC_BATCH, SC_CONV - 1, SC_WIDTH), 1.0),
        'cache_mem_k': nrm((DEPTH, DEC_BATCH, N_MEM, XA_HEADS, XA_HEAD_DIM), 1.0),
        'cache_mem_v': nrm((DEPTH, DEC_BATCH, N_MEM, XA_HEADS, XA_HEAD_DIM), 1.0),
        'norm_mix': gain((DEPTH, D_MODEL)),
        'w_in': nrm((DEPTH, D_MODEL, IN_PROJ_DIM), D_MODEL ** -0.5),
        'w_mconv': nrm((DEPTH, SSM_CONV, SSM_CONV_DIM), SSM_CONV ** -0.5),
        'b_mconv': nrm((DEPTH, SSM_CONV_DIM), 0.02),
        'dt_bias': dt0 + jnp.log(-jnp.expm1(-dt0)),
        'a_log': jnp.log(jax.random.uniform(next(ks), (DEPTH, SSM_HEADS), f32, minval=1.0, maxval=16.0)),
        'd_skip': gain((DEPTH, SSM_HEADS)),
        'norm_ssm': gain((DEPTH, SSM_INNER)),
        'w_sconv': nrm((DEPTH, SC_CONV, SC_WIDTH), SC_CONV ** -0.5),
        'w_out': nrm((DEPTH, D_MODEL, D_MODEL), D_MODEL ** -0.5),
        'norm_xattn': gain((DEPTH, D_MODEL)),
        'norm_mem': gain((DEPTH, D_MODEL)),
        'w_xq': nrm((DEPTH, D_MODEL, D_MODEL), D_MODEL ** -0.5),
        'w_xk': nrm((DEPTH, D_MODEL, D_MODEL), D_MODEL ** -0.5),
        'w_xv': nrm((DEPTH, D_MODEL, D_MODEL), D_MODEL ** -0.5),
        'w_xo': nrm((DEPTH, D_MODEL, D_MODEL), D_MODEL ** -0.5),
        'norm_moe': gain((DEPTH, D_MODEL)),
        'w_router': nrm((DEPTH, D_MODEL, N_EXPERTS), D_MODEL ** -0.5),
        'b_router': nrm((DEPTH, N_EXPERTS), 0.01),
        'w_gate_up': nrm((DEPTH, N_EXPERTS, D_MODEL, 2 * D_FF), D_MODEL ** -0.5),
        'b_gate_up': nrm((DEPTH, N_EXPERTS, 2 * D_FF), 0.02),
        'w_down': nrm((DEPTH, N_EXPERTS, D_FF, D_MODEL), D_FF ** -0.5),
        'b_down': nrm((DEPTH, N_EXPERTS, D_MODEL), 0.02),
        'norm_final': gain((D_MODEL,)),
    }


def reference(x_prompt, x_sample, mem_prompt, state_ssm, state_mamba_conv, state_short_conv,
              cache_mem_k, cache_mem_v, norm_mix, w_in, w_mconv, b_mconv, dt_bias, a_log, d_skip,
              norm_ssm, w_sconv, w_out, norm_xattn, norm_mem, w_xq, w_xk, w_xv, w_xo, norm_moe,
              w_router, b_router, w_gate_up, b_gate_up, w_down, b_down, norm_final):
    hp, hs = x_prompt, x_sample
    ssm_p, mconv_p, sconv_p, mk_p_all, mv_p_all = [], [], [], [], []
    ssm_s, mconv_s, sconv_s = [], [], []
    for l in range(DEPTH):
        lw = {
            'norm_mix': norm_mix[l], 'w_in': w_in[l], 'w_mconv': w_mconv[l], 'b_mconv': b_mconv[l],
            'dt_bias': dt_bias[l], 'a_log': a_log[l], 'd_skip': d_skip[l], 'norm_ssm': norm_ssm[l],
            'w_sconv': w_sconv[l], 'w_out': w_out[l], 'norm_xattn': norm_xattn[l], 'norm_mem': norm_mem[l],
            'w_xq': w_xq[l], 'w_xk': w_xk[l], 'w_xv': w_xv[l], 'w_xo': w_xo[l], 'norm_moe': norm_moe[l],
            'w_router': w_router[l], 'b_router': b_router[l], 'w_gate_up': w_gate_up[l],
            'b_gate_up': b_gate_up[l], 'w_down': w_down[l], 'b_down': b_down[l],
        }
        mk_p, mv_p = mem_kv(mem_prompt, lw)
        hp, mb_p, sb_p, st_p = decoder_layer(hp, mk_p, mv_p, None, None, None, lw, True)
        hs, mb_s, sb_s, st_s = decoder_layer(hs, cache_mem_k[l], cache_mem_v[l], state_mamba_conv[l],
                                             state_short_conv[l], state_ssm[l], lw, False)
        ssm_p.append(st_p); mconv_p.append(mb_p); sconv_p.append(sb_p)
        mk_p_all.append(mk_p); mv_p_all.append(mv_p)
        ssm_s.append(st_s); mconv_s.append(mb_s); sconv_s.append(sb_s)
    y_prompt = rmsnorm(hp, norm_final)
    y_sample = rmsnorm(hs, norm_final)
    new_ssm_prompt = jnp.stack(ssm_p)
    new_mconv_prompt = jnp.stack(mconv_p)
    new_sconv_prompt = jnp.stack(sconv_p)
    mem_k_prompt = jnp.stack(mk_p_all)
    mem_v_prompt = jnp.stack(mv_p_all)
    new_ssm_sample = jnp.stack(ssm_s)
    new_mconv_sample = jnp.stack(mconv_s)
    new_sconv_sample = jnp.stack(sconv_s)
    return (y_prompt, y_sample, new_ssm_prompt, new_mconv_prompt, new_sconv_prompt, mem_k_prompt,
            mem_v_prompt, new_ssm_sample, new_mconv_sample, new_sconv_sample)
```

```python
import functools

import jax
import jax.numpy as jnp
from jax import lax
from jax.experimental import pallas as pl
from jax.experimental.pallas import tpu as pltpu

F32 = jnp.float32
BF16 = jnp.bfloat16
I32 = jnp.int32

D_MODEL = 1024
N_MEM = 256
SSM_HEADS = 16
SSM_HEAD_DIM = 64
SSM_INNER = SSM_HEADS * SSM_HEAD_DIM
SSM_STATE = 128
SSM_GROUPS = 4
HEADS_PER_GROUP = SSM_HEADS // SSM_GROUPS
GROUP_WIDTH = SSM_INNER // SSM_GROUPS
SSM_CONV = 4
SSM_CONV_DIM = SSM_INNER + 2 * SSM_GROUPS * SSM_STATE
SC_CONV = 3
XA_HEADS = 4
XA_HEAD_DIM = D_MODEL // XA_HEADS
N_EXPERTS = 32
TOP_K = 4
D_FF = D_MODEL
SWIGLU_LIMIT = 7.0
SWIGLU_ALPHA = 1.702
EPS = 1e-6

LANES = 128
SUBLANES = 8
VMEM_LIMIT = 56 * 1024 * 1024

MIX_TILE = 256
MOE_ROW_TILE = 256
STATE_BB = 8
ATTN_BB = 4

NT_DIMS = (((1,), (1,)), ((), ()))


def _cparams(*sem):
    return pltpu.CompilerParams(dimension_semantics=sem, vmem_limit_bytes=VMEM_LIMIT)


def _const_spec(shape):
    nd = len(shape)
    return pl.BlockSpec(shape, lambda *_: (0,) * nd, pipeline_mode=pl.Buffered(1))


def _sigmoid(x):
    return 1.0 / (1.0 + jnp.exp(-x))


def _silu(x):
    return x * _sigmoid(x)


def _softplus(x):
    return jnp.maximum(x, 0.0) + jnp.log(1.0 + jnp.exp(-jnp.abs(x)))


def _rms(x, g):
    ms = jnp.mean(x * x, axis=-1, keepdims=True)
    return x * lax.rsqrt(ms + EPS) * g


def _dot(a, b):
    return jnp.dot(a, b, preferred_element_type=F32)


def _dot_nt(a, b):
    return lax.dot_general(a, b, NT_DIMS, preferred_element_type=F32)


def _expand_heads(v):
    rows = v.shape[0]
    lane = lax.broadcasted_iota(I32, (rows, LANES), 1)
    pieces = []
    for j in range(SSM_HEADS // 2):
        a = jnp.broadcast_to(v[:, 2 * j:2 * j + 1], (rows, LANES))
        b = jnp.broadcast_to(v[:, 2 * j + 1:2 * j + 2], (rows, LANES))
        pieces.append(jnp.where(lane < SSM_HEAD_DIM, a, b))
    return jnp.concatenate(pieces, axis=1)


def _group_rmsnorm(u, g):
    outs = []
    for k in range(SSM_GROUPS):
        ug = u[:, k * GROUP_WIDTH:(k + 1) * GROUP_WIDTH]
        ms = jnp.mean(ug * ug, axis=-1, keepdims=True)
        outs.append(ug * lax.rsqrt(ms + EPS))
    return jnp.concatenate(outs, axis=1) * g


def _memkv_body(mem_ref, g_ref, wk_ref, wv_ref, k_ref, v_ref, kb_ref, vb_ref):
    mn = _rms(mem_ref[...], g_ref[...]).astype(BF16)
    k = _dot(mn, wk_ref[...])
    v = _dot(mn, wv_ref[...])
    k_ref[...] = k
    v_ref[...] = v
    kb_ref[...] = k.astype(BF16)
    vb_ref[...] = v.astype(BF16)


def _mem_kv(mem2d, norm_mem, wk, wv):
    rows = mem2d.shape[0]
    nb = rows // N_MEM
    blk = pl.BlockSpec((N_MEM, D_MODEL), lambda b: (b, 0))
    return pl.pallas_call(
        _memkv_body,
        grid=(nb,),
        in_specs=[blk, _const_spec((1, D_MODEL)), _const_spec((D_MODEL, D_MODEL)),
                  _const_spec((D_MODEL, D_MODEL))],
        out_specs=[blk, blk, blk, blk],
        out_shape=[jax.ShapeDtypeStruct((rows, D_MODEL), F32)] * 2
        + [jax.ShapeDtypeStruct((rows, D_MODEL), BF16)] * 2,
        compiler_params=_cparams("arbitrary"),
        name="mem_kv",
    )(mem2d, norm_mem, wk, wv)


def _mix_body(x_ref, gmix_ref, wa_ref, wdtc_ref, wdtr_ref, wb_ref, wmc_ref, bmc_ref,
              dtb_ref, dtbt_ref, alog_ref, alogt_ref, dskip_ref, gssm_ref, wsc_ref, wout_ref,
              h_ref, ssm_ref, mbuf_ref, sbuf_ref,
              st_ref, cbuf_ref, scbuf_ref):
    tq = MIX_TILE
    c = pl.program_id(1)

    @pl.when(c == 0)
    def _():
        st_ref[...] = jnp.zeros_like(st_ref)
        cbuf_ref[0:SUBLANES, :] = jnp.zeros((SUBLANES, SSM_CONV_DIM), F32)
        scbuf_ref[0:SUBLANES, :] = jnp.zeros((SUBLANES, D_MODEL), F32)

    x = x_ref[...]
    xn = _rms(x, gmix_ref[...]).astype(BF16)

    pa = _dot(xn, wa_ref[...])
    z = pa[:, :SSM_INNER]
    u = pa[:, SSM_INNER:]
    cbuf_ref[SUBLANES:SUBLANES + tq, :] = u
    wm = wmc_ref[...]
    conv = u * wm[SSM_CONV - 1:SSM_CONV, :] + bmc_ref[...]
    for k in range(SSM_CONV - 1):
        off = SUBLANES - (SSM_CONV - 1) + k
        conv = conv + cbuf_ref[off:off + tq, :] * wm[k:k + 1, :]
    tail = cbuf_ref[tq + SUBLANES - (SSM_CONV - 1):tq + SUBLANES, :]
    mbuf_ref[...] = tail
    cbuf_ref[SUBLANES - (SSM_CONV - 1):SUBLANES, :] = tail
    xbc = _silu(conv)
    xs = xbc[:, :SSM_INNER]
    bm = xbc[:, SSM_INNER:SSM_INNER + SSM_GROUPS * SSM_STATE]
    cm = xbc[:, SSM_INNER + SSM_GROUPS * SSM_STATE:]

    dt = _softplus(_dot(xn, wdtc_ref[...]) + dtb_ref[...])
    dtt = _softplus(_dot_nt(wdtr_ref[...], xn) + dtbt_ref[...])
    a_row = -jnp.exp(alog_ref[...])
    a_col = -jnp.exp(alogt_ref[...])
    row_i = lax.broadcasted_iota(I32, (tq, tq), 0)
    col_i = lax.broadcasted_iota(I32, (tq, tq), 1)
    causal = row_i >= col_i
    tril = causal.astype(F32)
    triu = (row_i <= col_i).astype(F32)
    a_cum = jnp.dot(tril, dt * a_row, precision=lax.Precision.HIGHEST,
                    preferred_element_type=F32)
    a_cumt = jnp.dot(dtt * a_col, triu, precision=lax.Precision.HIGHEST,
                     preferred_element_type=F32)
    a_last = a_cum[tq - 1:tq, :]

    xdt = xs * _expand_heads(dt)
    in_decay = _expand_heads(jnp.exp(a_cum))
    to_end = _expand_heads(jnp.exp(a_last - a_cum))
    chunk_decay = _expand_heads(jnp.exp(a_last))
    xdt_b = xdt.astype(BF16)
    xend_b = (xdt * to_end).astype(BF16)
    lane = lax.broadcasted_iota(I32, (tq, LANES), 1)

    y_groups = []
    for g in range(SSM_GROUPS):
        cg = cm[:, g * SSM_STATE:(g + 1) * SSM_STATE].astype(BF16)
        bg_f = bm[:, g * SSM_STATE:(g + 1) * SSM_STATE]
        bg = bg_f.astype(BF16)
        scores = _dot_nt(cg, bg)
        gs = slice(g * GROUP_WIDTH, (g + 1) * GROUP_WIDTH)
        st_g = st_ref[:, gs]
        y_off = _dot(cg, st_g.astype(BF16)) * in_decay[:, gs]
        pair_out = []
        for pr in range(HEADS_PER_GROUP // 2):
            h0 = g * HEADS_PER_GROUP + 2 * pr
            xp = xdt_b[:, h0 * SSM_HEAD_DIM:(h0 + 2) * SSM_HEAD_DIM]
            ys = []
            for h in (h0, h0 + 1):
                seg = a_cum[:, h:h + 1] - a_cumt[h:h + 1, :]
                decay = jnp.where(causal, jnp.exp(jnp.minimum(seg, 0.0)), 0.0)
                ys.append(_dot((scores * decay).astype(BF16), xp))
            pair_out.append(jnp.where(lane < SSM_HEAD_DIM, ys[0], ys[1]))
        y_groups.append(jnp.concatenate(pair_out, axis=1) + y_off)
        st_ref[:, gs] = st_g * chunk_decay[:, gs] + _dot(bg_f.T.astype(BF16), xend_b[:, gs])
    y = jnp.concatenate(y_groups, axis=1) + dskip_ref[...] * xs
    y_a = _group_rmsnorm(y * _silu(z), gssm_ref[...])

    @pl.when(c == pl.num_programs(1) - 1)
    def _():
        ssm_ref[...] = st_ref[...].T

    pb = _dot(xn, wb_ref[...])
    sc_b = pb[:, 0:D_MODEL]
    cv = pb[:, D_MODEL:2 * D_MODEL] * pb[:, 2 * D_MODEL:3 * D_MODEL]
    g_a = pb[:, 3 * D_MODEL:4 * D_MODEL]
    g_b = pb[:, 4 * D_MODEL:5 * D_MODEL]
    scbuf_ref[SUBLANES:SUBLANES + tq, :] = cv
    ws = wsc_ref[...]
    uc = cv * ws[SC_CONV - 1:SC_CONV, :]
    for k in range(SC_CONV - 1):
        off = SUBLANES - (SC_CONV - 1) + k
        uc = uc + scbuf_ref[off:off + tq, :] * ws[k:k + 1, :]
    stail = scbuf_ref[tq + SUBLANES - (SC_CONV - 1):tq + SUBLANES, :]
    sbuf_ref[...] = stail
    scbuf_ref[SUBLANES - (SC_CONV - 1):SUBLANES, :] = stail
    merged = _sigmoid(g_a) * y_a + _sigmoid(g_b) * (sc_b * uc)
    h_ref[...] = x + _dot(merged.astype(BF16), wout_ref[...])


def _prompt_mixer(x2d, nb, w):
    t = x2d.shape[0]
    nc = t // nb // MIX_TILE
    tok = pl.BlockSpec((MIX_TILE, D_MODEL), lambda b, c: (b * nc + c, 0))
    return pl.pallas_call(
        _mix_body,
        grid=(nb, nc),
        in_specs=[tok, _const_spec((1, D_MODEL)),
                  _const_spec((D_MODEL, SSM_INNER + SSM_CONV_DIM)),
                  _const_spec((D_MODEL, SSM_HEADS)), _const_spec((SSM_HEADS, D_MODEL)),
                  _const_spec((D_MODEL, 5 * D_MODEL)),
                  _const_spec((SSM_CONV, SSM_CONV_DIM)), _const_spec((1, SSM_CONV_DIM)),
                  _const_spec((1, SSM_HEADS)), _const_spec((SSM_HEADS, 1)),
                  _const_spec((1, SSM_HEADS)), _const_spec((SSM_HEADS, 1)),
                  _const_spec((1, SSM_INNER)), _const_spec((1, SSM_INNER)),
                  _const_spec((SC_CONV, D_MODEL)), _const_spec((D_MODEL, D_MODEL))],
        out_specs=[tok,
                   pl.BlockSpec((None, SSM_INNER, SSM_STATE), lambda b, c: (b, 0, 0)),
                   pl.BlockSpec((None, SSM_CONV - 1, SSM_CONV_DIM), lambda b, c: (b, 0, 0)),
                   pl.BlockSpec((None, SC_CONV - 1, D_MODEL), lambda b, c: (b, 0, 0))],
        out_shape=[jax.ShapeDtypeStruct((t, D_MODEL), F32),
                   jax.ShapeDtypeStruct((nb, SSM_INNER, SSM_STATE), F32),
                   jax.ShapeDtypeStruct((nb, SSM_CONV - 1, SSM_CONV_DIM), F32),
                   jax.ShapeDtypeStruct((nb, SC_CONV - 1, D_MODEL), F32)],
        scratch_shapes=[pltpu.VMEM((SSM_STATE, SSM_INNER), F32),
                        pltpu.VMEM((MIX_TILE + SUBLANES, SSM_CONV_DIM), F32),
                        pltpu.VMEM((MIX_TILE + SUBLANES, D_MODEL), F32)],
        compiler_params=_cparams("arbitrary", "arbitrary"),
        name="prompt_mixer",
    )(x2d, w["norm_mix"], w["w_a"], w["w_dt"], w["w_dt_t"], w["w_b"], w["w_mconv"], w["b_mconv"],
      w["dt_bias"], w["dt_bias_t"], w["a_log"], w["a_log_t"], w["d_skip"], w["norm_ssm"],
      w["w_sconv"], w["w_out"])


def _router_tail(h2, gmoe_ref, wr_ref, br_ref, h2_ref, hn_ref, lg_ref):
    h2_ref[...] = h2
    hn = _rms(h2, gmoe_ref[...])
    hn_ref[...] = hn
    lg_ref[...] = _dot(hn.astype(BF16), wr_ref[...]) + br_ref[...]


def _attn_body(h_ref, gx_ref, wq_ref, k_ref, v_ref, wo_ref, gmoe_ref, wr_ref, br_ref,
               h2_ref, hn_ref, lg_ref):
    h = h_ref[...]
    hn = _rms(h, gx_ref[...]).astype(BF16)
    q = _dot(hn, wq_ref[...]).astype(BF16)
    outs = []
    for hd in range(XA_HEADS):
        sl = slice(hd * XA_HEAD_DIM, (hd + 1) * XA_HEAD_DIM)
        s = _dot_nt(q[:, sl], k_ref[:, sl]) * (XA_HEAD_DIM ** -0.5)
        e = jnp.exp(s - jnp.max(s, axis=-1, keepdims=True))
        p = e / jnp.sum(e, axis=-1, keepdims=True)
        outs.append(_dot(p.astype(BF16), v_ref[:, sl]))
    o = jnp.concatenate(outs, axis=1).astype(BF16)
    h2 = h + _dot(o, wo_ref[...])
    _router_tail(h2, gmoe_ref, wr_ref, br_ref, h2_ref, hn_ref, lg_ref)


def _prompt_attn(h1, kb, vb, nb, w):
    t = h1.shape[0]
    nc = t // nb // MIX_TILE
    tok = pl.BlockSpec((MIX_TILE, D_MODEL), lambda b, c: (b * nc + c, 0))
    kv = pl.BlockSpec((N_MEM, D_MODEL), lambda b, c: (b, 0))
    return pl.pallas_call(
        _attn_body,
        grid=(nb, nc),
        in_specs=[tok, _const_spec((1, D_MODEL)), _const_spec((D_MODEL, D_MODEL)), kv, kv,
                  _const_spec((D_MODEL, D_MODEL)), _const_spec((1, D_MODEL)),
                  _const_spec((D_MODEL, N_EXPERTS)), _const_spec((1, N_EXPERTS))],
        out_specs=[tok, tok, pl.BlockSpec((MIX_TILE, N_EXPERTS), lambda b, c: (b * nc + c, 0))],
        out_shape=[jax.ShapeDtypeStruct((t, D_MODEL), F32),
                   jax.ShapeDtypeStruct((t, D_MODEL), F32),
                   jax.ShapeDtypeStruct((t, N_EXPERTS), F32)],
        compiler_params=_cparams("arbitrary", "arbitrary"),
        name="prompt_attn",
    )(h1, w["norm_xattn"], w["w_xq"], kb, vb, w["w_xo"], w["norm_moe"], w["w_router"], w["b_router"])


def _sproj_body(x_ref, gmix_ref, wa_ref, wdtc_ref, wb_ref, wmc_ref, bmc_ref, dtb_ref, alog_ref,
                wsc_ref, mst_ref, sst_ref,
                z_ref, xs_ref, dtx_ref, dec_ref, bm_ref, cm_ref, yb_ref, sga_ref, mnew_ref, snew_ref):
    x = x_ref[...]
    xn = _rms(x, gmix_ref[...]).astype(BF16)
    pa = _dot(xn, wa_ref[...])
    z_ref[...] = pa[:, :SSM_INNER]
    u = pa[:, SSM_INNER:]
    wm = wmc_ref[...]
    conv = u * wm[SSM_CONV - 1:SSM_CONV, :] + bmc_ref[...]
    for k in range(SSM_CONV - 1):
        conv = conv + mst_ref[k] * wm[k:k + 1, :]
    for k in range(SSM_CONV - 2):
        mnew_ref[k] = mst_ref[k + 1]
    mnew_ref[SSM_CONV - 2] = u
    xbc = _silu(conv)
    xs = xbc[:, :SSM_INNER]
    xs_ref[...] = xs
    bm_ref[...] = xbc[:, SSM_INNER:SSM_INNER + SSM_GROUPS * SSM_STATE]
    cm_ref[...] = xbc[:, SSM_INNER + SSM_GROUPS * SSM_STATE:]
    dt = _softplus(_dot(xn, wdtc_ref[...]) + dtb_ref[...])
    dec_ref[...] = jnp.exp(dt * (-jnp.exp(alog_ref[...])))
    dtx_ref[...] = xs * _expand_heads(dt)
    pb = _dot(xn, wb_ref[...])
    cv = pb[:, D_MODEL:2 * D_MODEL] * pb[:, 2 * D_MODEL:3 * D_MODEL]
    ws = wsc_ref[...]
    uc = cv * ws[SC_CONV - 1:SC_CONV, :]
    for k in range(SC_CONV - 1):
        uc = uc + sst_ref[k] * ws[k:k + 1, :]
    for k in range(SC_CONV - 2):
        snew_ref[k] = sst_ref[k + 1]
    snew_ref[SC_CONV - 2] = cv
    yb_ref[...] = _sigmoid(pb[:, 4 * D_MODEL:5 * D_MODEL]) * (pb[:, 0:D_MODEL] * uc)
    sga_ref[...] = _sigmoid(pb[:, 3 * D_MODEL:4 * D_MODEL])


def _sample_proj(x, mstate_t, sstate_t, w):
    nb = x.shape[0]
    f = lambda *s: jax.ShapeDtypeStruct(s, F32)
    return pl.pallas_call(
        _sproj_body,
        out_shape=[f(nb, SSM_INNER), f(nb, SSM_INNER), f(nb, SSM_INNER), f(nb, SSM_HEADS),
                   f(nb, SSM_GROUPS * SSM_STATE), f(nb, SSM_GROUPS * SSM_STATE),
                   f(nb, D_MODEL), f(nb, D_MODEL),
                   f(SSM_CONV - 1, nb, SSM_CONV_DIM), f(SC_CONV - 1, nb, D_MODEL)],
        compiler_params=pltpu.CompilerParams(vmem_limit_bytes=VMEM_LIMIT),
        name="sample_proj",
    )(x, w["norm_mix"], w["w_a"], w["w_dt"], w["w_b"], w["w_mconv"], w["b_mconv"], w["dt_bias"],
      w["a_log"], w["w_sconv"], mstate_t, sstate_t)


def _sstate_body(dec_ref, s_ref, dtx_ref, bm_ref, cm_ref, snew_ref, y_ref):
    i = pl.program_id(0)
    rows_per_blk = LANES
    for j in range(STATE_BB):
        b = i * STATE_BB + j
        dtx_row = dtx_ref[j:j + 1, :]
        y_parts = []
        for g in range(SSM_GROUPS):
            b_row = bm_ref[j:j + 1, g * SSM_STATE:(g + 1) * SSM_STATE]
            c_row = cm_ref[j:j + 1, g * SSM_STATE:(g + 1) * SSM_STATE].astype(BF16)
            new_blocks = []
            for q in range(GROUP_WIDTH // rows_per_blk):
                r0 = g * GROUP_WIDTH + q * rows_per_blk
                dcol = jnp.broadcast_to(dtx_row[:, r0:r0 + rows_per_blk], (rows_per_blk, LANES)).T
                sub = []
                for hh in range(rows_per_blk // SSM_HEAD_DIM):
                    h = r0 // SSM_HEAD_DIM + hh
                    lo = hh * SSM_HEAD_DIM
                    s_old = s_ref[j, r0 + lo:r0 + lo + SSM_HEAD_DIM, :]
                    sub.append(s_old * dec_ref[b, h] + dcol[lo:lo + SSM_HEAD_DIM, :] * b_row)
                blk = jnp.concatenate(sub, axis=0)
                snew_ref[j, r0:r0 + rows_per_blk, :] = blk
                new_blocks.append(blk.astype(BF16))
            s_g = jnp.concatenate(new_blocks, axis=0)
            y_parts.append(_dot_nt(c_row, s_g))
        y_ref[j:j + 1, :] = jnp.concatenate(y_parts, axis=1)


def _sample_state(dec, state, dtx, bm, cm):
    nb = state.shape[0]
    row = lambda wdt: pl.BlockSpec((STATE_BB, wdt), lambda i, dec: (i, 0))
    st = pl.BlockSpec((STATE_BB, SSM_INNER, SSM_STATE), lambda i, dec: (i, 0, 0))
    return pl.pallas_call(
        _sstate_body,
        grid_spec=pltpu.PrefetchScalarGridSpec(
            num_scalar_prefetch=1, grid=(nb // STATE_BB,),
            in_specs=[st, row(SSM_INNER), row(SSM_GROUPS * SSM_STATE), row(SSM_GROUPS * SSM_STATE)],
            out_specs=[st, row(SSM_INNER)]),
        out_shape=[jax.ShapeDtypeStruct(state.shape, F32), jax.ShapeDtypeStruct((nb, SSM_INNER), F32)],
        compiler_params=_cparams("arbitrary"),
        name="sample_state",
    )(dec, state, dtx, bm, cm)


def _sfin1_body(x_ref, y_ref, xs_ref, z_ref, yb_ref, sga_ref, dskip_ref, gssm_ref, wout_ref,
                gx_ref, wq_ref, h_ref, q_ref):
    y = y_ref[...] + dskip_ref[...] * xs_ref[...]
    y_a = _group_rmsnorm(y * _silu(z_ref[...]), gssm_ref[...])
    merged = sga_ref[...] * y_a + yb_ref[...]
    h = x_ref[...] + _dot(merged.astype(BF16), wout_ref[...])
    h_ref[...] = h
    q_ref[...] = _dot(_rms(h, gx_ref[...]).astype(BF16), wq_ref[...])


def _sample_fin1(x, y, xs, z, yb, sga, w):
    nb = x.shape[0]
    return pl.pallas_call(
        _sfin1_body,
        out_shape=[jax.ShapeDtypeStruct((nb, D_MODEL), F32)] * 2,
        compiler_params=pltpu.CompilerParams(vmem_limit_bytes=VMEM_LIMIT),
        name="sample_fin1",
    )(x, y, xs, z, yb, sga, w["d_skip"], w["norm_ssm"], w["w_out"], w["norm_xattn"], w["w_xq"])


def _sattn_body(q_ref, k_ref, v_ref, o_ref):
    for j in range(ATTN_BB):
        parts = []
        for hd in range(XA_HEADS):
            sl = slice(hd * XA_HEAD_DIM, (hd + 1) * XA_HEAD_DIM)
            qh = q_ref[j, :, sl].astype(BF16)
            s = _dot_nt(qh, k_ref[j, :, sl].astype(BF16)) * (XA_HEAD_DIM ** -0.5)
            e = jnp.exp(s - jnp.max(s, axis=-1, keepdims=True))
            p = e / jnp.sum(e, axis=-1, keepdims=True)
            parts.append(_dot(p.astype(BF16), v_ref[j, :, sl].astype(BF16)))
        o_ref[j] = jnp.concatenate(parts, axis=1)


def _sample_attn(q3, k3, v3):
    nb = q3.shape[0]
    qs = pl.BlockSpec((ATTN_BB, 1, D_MODEL), lambda i: (i, 0, 0))
    kv = pl.BlockSpec((ATTN_BB, N_MEM, D_MODEL), lambda i: (i, 0, 0))
    return pl.pallas_call(
        _sattn_body,
        grid=(nb // ATTN_BB,),
        in_specs=[qs, kv, kv],
        out_specs=qs,
        out_shape=jax.ShapeDtypeStruct((nb, 1, D_MODEL), F32),
        compiler_params=_cparams("arbitrary"),
        name="sample_attn",
    )(q3, k3, v3)


def _sfin2_body(h_ref, o_ref, wo_ref, gmoe_ref, wr_ref, br_ref, h2_ref, hn_ref, lg_ref):
    h2 = h_ref[...] + _dot(o_ref[...].astype(BF16), wo_ref[...])
    _router_tail(h2, gmoe_ref, wr_ref, br_ref, h2_ref, hn_ref, lg_ref)


def _sample_fin2(h1, o, w):
    nb = h1.shape[0]
    return pl.pallas_call(
        _sfin2_body,
        out_shape=[jax.ShapeDtypeStruct((nb, D_MODEL), F32)] * 2
        + [jax.ShapeDtypeStruct((nb, N_EXPERTS), F32)],
        compiler_params=pltpu.CompilerParams(vmem_limit_bytes=VMEM_LIMIT),
        name="sample_fin2",
    )(h1, o, w["w_xo"], w["norm_moe"], w["w_router"], w["b_router"])


def _route_body(lg_ref, e_ref, g_ref, r_ref, cnt_ref, carry_ref):
    i = pl.program_id(0)
    tt = lg_ref.shape[0]

    @pl.when(i == 0)
    def _():
        carry_ref[...] = jnp.zeros_like(carry_ref)

    work = lg_ref[...]
    lane = lax.broadcasted_iota(I32, (tt, N_EXPERTS), 1).astype(F32)
    vals, idxs, hots = [], [], []
    for _ in range(TOP_K):
        m = jnp.max(work, axis=-1, keepdims=True)
        idx = jnp.min(jnp.where(work == m, lane, float(N_EXPERTS)), axis=-1, keepdims=True)
        hot = lane == idx
        vals.append(m)
        idxs.append(idx)
        hots.append(hot)
        work = jnp.where(hot, -jnp.inf, work)
    exps = [jnp.exp(v - vals[0]) for v in vals]
    tot = exps[0]
    for e in exps[1:]:
        tot = tot + e
    assigned = hots[0]
    for hot in hots[1:]:
        assigned = assigned | hot
    a = assigned.astype(F32)
    r_i = lax.broadcasted_iota(I32, (tt, tt), 0)
    c_i = lax.broadcasted_iota(I32, (tt, tt), 1)
    before = (r_i > c_i).astype(BF16)
    rank_all = _dot(before, a.astype(BF16)) + carry_ref[...]
    carry = carry_ref[...] + jnp.sum(a, axis=0, keepdims=True)
    carry_ref[...] = carry
    cnt_ref[...] = carry
    k_lane = lax.broadcasted_iota(I32, (tt, TOP_K), 1)
    e_out = jnp.zeros((tt, TOP_K), F32)
    g_out = jnp.zeros((tt, TOP_K), F32)
    r_out = jnp.zeros((tt, TOP_K), F32)
    for k in range(TOP_K):
        rk = jnp.sum(jnp.where(hots[k], rank_all, 0.0), axis=-1, keepdims=True)
        e_out = jnp.where(k_lane == k, idxs[k], e_out)
        g_out = jnp.where(k_lane == k, exps[k] / tot, g_out)
        r_out = jnp.where(k_lane == k, rk, r_out)
    e_ref[...] = e_out.astype(I32)
    g_ref[...] = g_out
    r_ref[...] = r_out.astype(I32)


def _route(logits, tt):
    t = logits.shape[0]
    tk = pl.BlockSpec((tt, TOP_K), lambda i: (i, 0))
    return pl.pallas_call(
        _route_body,
        grid=(t // tt,),
        in_specs=[pl.BlockSpec((tt, N_EXPERTS), lambda i: (i, 0))],
        out_specs=[tk, tk, tk, pl.BlockSpec((1, N_EXPERTS), lambda i: (0, 0))],
        out_shape=[jax.ShapeDtypeStruct((t, TOP_K), I32), jax.ShapeDtypeStruct((t, TOP_K), F32),
                   jax.ShapeDtypeStruct((t, TOP_K), I32), jax.ShapeDtypeStruct((1, N_EXPERTS), F32)],
        scratch_shapes=[pltpu.VMEM((1, N_EXPERTS), F32)],
        compiler_params=_cparams("arbitrary"),
        name="moe_route",
    )(logits)


def _dispatch_body(tm, tt, pos_ref, zstart_ref, x_hbm, o_hbm, zero_ref, sem):
    i = pl.program_id(0)

    def zero_copy(j):
        return pltpu.make_async_copy(zero_ref, o_hbm.at[pl.ds(pl.multiple_of(zstart_ref[j], tm), tm), :], sem)

    @pl.when(i == 0)
    def _():
        zero_ref[...] = jnp.zeros_like(zero_ref)

        def start(j, carry):
            @pl.when(zstart_ref[j] >= 0)
            def _():
                zero_copy(j).start()
            return carry

        def wait(j, carry):
            @pl.when(zstart_ref[j] >= 0)
            def _():
                zero_copy(j).wait()
            return carry

        lax.fori_loop(0, zstart_ref.shape[0], start, 0)
        lax.fori_loop(0, zstart_ref.shape[0], wait, 0)

    def row_copy(t, k):
        src = x_hbm.at[pl.ds(i * tt + t, 1), :]
        dst = o_hbm.at[pl.ds(pos_ref[0, 0, t * TOP_K + k], 1), :]
        return pltpu.make_async_copy(src, dst, sem)

    def issue(t, carry):
        for k in range(TOP_K):
            row_copy(t, k).start()
        return carry

    lax.fori_loop(0, tt, issue, 0)
    n = tt * TOP_K
    pltpu.make_async_copy(o_hbm.at[pl.ds(0, n), :], o_hbm.at[pl.ds(0, n), :], sem).wait()


def _dispatch(pos3, zero_starts, hn, n_rows, tm, tt):
    t = hn.shape[0]
    return pl.pallas_call(
        functools.partial(_dispatch_body, tm, tt),
        grid_spec=pltpu.PrefetchScalarGridSpec(
            num_scalar_prefetch=0, grid=(t // tt,),
            in_specs=[pl.BlockSpec((1, 1, tt * TOP_K), lambda i: (i, 0, 0), memory_space=pltpu.SMEM),
                      pl.BlockSpec(memory_space=pltpu.SMEM),
                      pl.BlockSpec(memory_space=pl.ANY)],
            out_specs=pl.BlockSpec(memory_space=pl.ANY),
            scratch_shapes=[pltpu.VMEM((tm, D_MODEL), F32), pltpu.SemaphoreType.DMA]),
        out_shape=jax.ShapeDtypeStruct((n_rows, D_MODEL), F32),
        compiler_params=_cparams("arbitrary"),
        name="moe_dispatch",
    )(pos3, zero_starts, hn)


def _expert_body(be_ref, nu_ref, x_ref, wgu_ref, bgu_ref, wdn_ref, bdn_ref, y_ref):
    i = pl.program_id(0)

    @pl.when(i < nu_ref[0])
    def _():
        gu = _dot(x_ref[...].astype(BF16), wgu_ref[...]) + bgu_ref[...]
        gate = jnp.minimum(gu[:, :D_FF], SWIGLU_LIMIT)
        up = jnp.clip(gu[:, D_FF:], -SWIGLU_LIMIT, SWIGLU_LIMIT)
        act = (up + 1.0) * (gate * _sigmoid(SWIGLU_ALPHA * gate))
        y_ref[...] = _dot(act.astype(BF16), wdn_ref[...]) + bdn_ref[...]

    @pl.when(i >= nu_ref[0])
    def _():
        y_ref[...] = jnp.zeros_like(y_ref)


def _experts(block_e, n_used, xs, wgu, bgu, wdn, bdn, tm):
    n_rows = xs.shape[0]
    return pl.pallas_call(
        _expert_body,
        grid_spec=pltpu.PrefetchScalarGridSpec(
            num_scalar_prefetch=2, grid=(n_rows // tm,),
            in_specs=[pl.BlockSpec((tm, D_MODEL), lambda i, be, nu: (jnp.minimum(i, nu[0] - 1), 0)),
                      pl.BlockSpec((None, D_MODEL, 2 * D_FF), lambda i, be, nu: (be[i], 0, 0)),
                      pl.BlockSpec((None, 1, 2 * D_FF), lambda i, be, nu: (be[i], 0, 0)),
                      pl.BlockSpec((None, D_FF, D_MODEL), lambda i, be, nu: (be[i], 0, 0)),
                      pl.BlockSpec((None, 1, D_MODEL), lambda i, be, nu: (be[i], 0, 0))],
            out_specs=pl.BlockSpec((tm, D_MODEL), lambda i, be, nu: (i, 0))),
        out_shape=jax.ShapeDtypeStruct((n_rows, D_MODEL), F32),
        compiler_params=_cparams("arbitrary"),
        name="moe_experts",
    )(block_e, n_used, xs, wgu, bgu, wdn, bdn)


def _combine_body(tt, pos_ref, g_ref, h_ref, gfin_ref, ys_hbm, y_ref, buf_ref, sem):
    def row_copy(t, k):
        src = ys_hbm.at[pl.ds(pos_ref[0, 0, t * TOP_K + k], 1), :]
        return pltpu.make_async_copy(src, buf_ref.at[k, pl.ds(t, 1), :], sem)

    def issue(t, carry):
        for k in range(TOP_K):
            row_copy(t, k).start()
        return carry

    lax.fori_loop(0, tt, issue, 0)
    for k in range(TOP_K):
        pltpu.make_async_copy(ys_hbm.at[pl.ds(0, tt), :], buf_ref.at[k], sem).wait()
    gates = g_ref[...]
    out = h_ref[...]
    for k in range(TOP_K):
        out = out + gates[:, k:k + 1] * buf_ref[k]
    y_ref[...] = _rms(out, gfin_ref[...])


def _combine(pos3, gates, h2, norm_final, ys, tt):
    t = h2.shape[0]
    tok = pl.BlockSpec((tt, D_MODEL), lambda i: (i, 0))
    return pl.pallas_call(
        functools.partial(_combine_body, tt),
        grid_spec=pltpu.PrefetchScalarGridSpec(
            num_scalar_prefetch=0, grid=(t // tt,),
            in_specs=[pl.BlockSpec((1, 1, tt * TOP_K), lambda i: (i, 0, 0), memory_space=pltpu.SMEM),
                      pl.BlockSpec((tt, TOP_K), lambda i: (i, 0)), tok,
                      pl.BlockSpec((1, D_MODEL), lambda i: (0, 0)),
                      pl.BlockSpec(memory_space=pl.ANY)],
            out_specs=tok,
            scratch_shapes=[pltpu.VMEM((TOP_K, tt, D_MODEL), F32), pltpu.SemaphoreType.DMA]),
        out_shape=jax.ShapeDtypeStruct((t, D_MODEL), F32),
        compiler_params=_cparams("arbitrary"),
        name="moe_combine",
    )(pos3, gates, h2, norm_final, ys)


def _moe_and_final_norm(hn, logits, h2, w, tt, tm):
    t = hn.shape[0]
    eidx, gates, rank, counts_f = _route(logits, tt)
    counts = counts_f[0].astype(I32)
    padded = (counts + tm - 1) // tm * tm
    pad_end = jnp.cumsum(padded)
    start = pad_end - padded
    onehot = eidx[..., None] == jnp.arange(N_EXPERTS, dtype=I32)
    pos = rank + jnp.sum(jnp.where(onehot, start, 0), axis=-1)
    pos3 = pos.reshape(t // tt, 1, tt * TOP_K)
    n_blocks = (t * TOP_K + N_EXPERTS * (tm - 1)) // tm
    n_rows = n_blocks * tm
    block_start = jnp.arange(n_blocks, dtype=I32) * tm
    block_e = jnp.minimum(jnp.sum(block_start[:, None] >= pad_end[None, :], axis=-1), N_EXPERTS - 1).astype(I32)
    n_used = (pad_end[-1:] // tm).astype(I32)
    zero_starts = jnp.concatenate([jnp.where(padded > 0, pad_end - tm, -1),
                                   jnp.where(block_start >= pad_end[-1], block_start, -1)]).astype(I32)
    xs = _dispatch(pos3, zero_starts, hn, n_rows, tm, tt)
    ys = _experts(block_e, n_used, xs, w["w_gate_up"], w["b_gate_up"], w["w_down"], w["b_down"], tm)
    return _combine(pos3, gates, h2, w["norm_final"], ys, tt)


def kernel(x_prompt, x_sample, mem_prompt, state_ssm, state_mamba_conv, state_short_conv, cache_mem_k, cache_mem_v, norm_mix, w_in, w_mconv, b_mconv, dt_bias, a_log, d_skip, norm_ssm, w_sconv, w_out, norm_xattn, norm_mem, w_xq, w_xk, w_xv, w_xo, norm_moe, w_router, b_router, w_gate_up, b_gate_up, w_down, b_down, norm_final):
    nbp, seq, _ = x_prompt.shape
    nbs = x_sample.shape[0]
    dt_lo = SSM_INNER + SSM_CONV_DIM
    w_in0 = w_in[0]
    w_dt = w_in0[:, dt_lo:dt_lo + SSM_HEADS]
    w = {
        "norm_mix": norm_mix, "norm_ssm": norm_ssm, "norm_xattn": norm_xattn, "norm_moe": norm_moe,
        "norm_final": norm_final.reshape(1, D_MODEL),
        "w_a": w_in0[:, :dt_lo].astype(BF16),
        "w_dt": w_dt.astype(BF16), "w_dt_t": w_dt.T.astype(BF16),
        "w_b": w_in0[:, dt_lo + SSM_HEADS:].astype(BF16),
        "w_mconv": w_mconv[0], "b_mconv": b_mconv,
        "dt_bias": dt_bias, "dt_bias_t": dt_bias.reshape(SSM_HEADS, 1),
        "a_log": a_log, "a_log_t": a_log.reshape(SSM_HEADS, 1),
        "d_skip": jnp.repeat(d_skip, SSM_HEAD_DIM, axis=1),
        "w_sconv": w_sconv[0], "w_out": w_out[0].astype(BF16),
        "w_xq": w_xq[0].astype(BF16), "w_xo": w_xo[0].astype(BF16),
        "w_router": w_router[0].astype(BF16), "b_router": b_router,
        "w_gate_up": w_gate_up[0].astype(BF16), "b_gate_up": b_gate_up[0].reshape(N_EXPERTS, 1, 2 * D_FF),
        "w_down": w_down[0].astype(BF16), "b_down": b_down[0].reshape(N_EXPERTS, 1, D_MODEL),
    }

    k_p, v_p, kb, vb = _mem_kv(mem_prompt.reshape(nbp * N_MEM, D_MODEL), norm_mem,
                               w_xk[0].astype(BF16), w_xv[0].astype(BF16))
    h1, ssm_p, mconv_p, sconv_p = _prompt_mixer(x_prompt.reshape(nbp * seq, D_MODEL), nbp, w)
    h2, hn, logits = _prompt_attn(h1, kb, vb, nbp, w)
    y_prompt = _moe_and_final_norm(hn, logits, h2, w, MIX_TILE, MOE_ROW_TILE)

    xs2 = x_sample.reshape(nbs, D_MODEL)
    mstate_t = jnp.transpose(state_mamba_conv[0], (1, 0, 2))
    sstate_t = jnp.transpose(state_short_conv[0], (1, 0, 2))
    z, xs_, dtx, dec, bm, cm, yb, sga, mnew_t, snew_t = _sample_proj(xs2, mstate_t, sstate_t, w)
    ssm_s, y_s = _sample_state(dec, state_ssm[0].reshape(nbs, SSM_INNER, SSM_STATE), dtx, bm, cm)
    h1s, q_s = _sample_fin1(xs2, y_s, xs_, z, yb, sga, w)
    o_s = _sample_attn(q_s.reshape(nbs, 1, D_MODEL),
                       cache_mem_k[0].reshape(nbs, N_MEM, D_MODEL),
                       cache_mem_v[0].reshape(nbs, N_MEM, D_MODEL))
    h2s, hns, logits_s = _sample_fin2(h1s, o_s.reshape(nbs, D_MODEL), w)
    y_sample = _moe_and_final_norm(hns, logits_s, h2s, w, nbs, LANES)

    return (y_prompt.reshape(nbp, seq, D_MODEL),
            y_sample.reshape(nbs, 1, D_MODEL),
            ssm_p.reshape(1, nbp, SSM_HEADS, SSM_HEAD_DIM, SSM_STATE),
            mconv_p[None], sconv_p[None],
            k_p.reshape(1, nbp, N_MEM, XA_HEADS, XA_HEAD_DIM),
            v_p.reshape(1, nbp, N_MEM, XA_HEADS, XA_HEAD_DIM),
            ssm_s.reshape(1, nbs, SSM_HEADS, SSM_HEAD_DIM, SSM_STATE),
            jnp.transpose(mnew_t, (1, 0, 2))[None],
            jnp.transpose(snew_t, (1, 0, 2))[None])
```

```python
import functools

import jax
import jax.numpy as jnp
from jax import lax
from jax.experimental import pallas as pl
from jax.experimental.pallas import tpu as pltpu

F32 = jnp.float32
BF16 = jnp.bfloat16
I32 = jnp.int32

D_MODEL = 1024
N_MEM = 256
SSM_HEADS = 16
SSM_HEAD_DIM = 64
SSM_INNER = SSM_HEADS * SSM_HEAD_DIM
SSM_STATE = 128
SSM_GROUPS = 4
HEADS_PER_GROUP = SSM_HEADS // SSM_GROUPS
GROUP_WIDTH = SSM_INNER // SSM_GROUPS
SSM_CONV = 4
SSM_CONV_DIM = SSM_INNER + 2 * SSM_GROUPS * SSM_STATE
SC_CONV = 3
XA_HEADS = 4
XA_HEAD_DIM = D_MODEL // XA_HEADS
N_EXPERTS = 32
TOP_K = 4
D_FF = D_MODEL
SWIGLU_LIMIT = 7.0
SWIGLU_ALPHA = 1.702
EPS = 1e-6

LANES = 128
SUBLANES = 8
VMEM_LIMIT = 56 * 1024 * 1024

MIX_TILE = 256
MOE_ROW_TILE = 256
STATE_BB = 8
ATTN_BB = 4

NT_DIMS = (((1,), (1,)), ((), ()))


def _cparams(*sem):
    return pltpu.CompilerParams(dimension_semantics=sem, vmem_limit_bytes=VMEM_LIMIT)


def _const_spec(shape):
    nd = len(shape)
    return pl.BlockSpec(shape, lambda *_: (0,) * nd, pipeline_mode=pl.Buffered(1))


def _sigmoid(x):
    return 1.0 / (1.0 + jnp.exp(-x))


def _silu(x):
    return x * _sigmoid(x)


def _softplus(x):
    return jnp.maximum(x, 0.0) + jnp.log(1.0 + jnp.exp(-jnp.abs(x)))


def _rms(x, g):
    ms = jnp.mean(x * x, axis=-1, keepdims=True)
    return x * lax.rsqrt(ms + EPS) * g


def _dot(a, b):
    return jnp.dot(a, b, preferred_element_type=F32)


def _dot_nt(a, b):
    return lax.dot_general(a, b, NT_DIMS, preferred_element_type=F32)


def _expand_heads(v):
    rows = v.shape[0]
    lane = lax.broadcasted_iota(I32, (rows, LANES), 1)
    pieces = []
    for j in range(SSM_HEADS // 2):
        a = jnp.broadcast_to(v[:, 2 * j:2 * j + 1], (rows, LANES))
        b = jnp.broadcast_to(v[:, 2 * j + 1:2 * j + 2], (rows, LANES))
        pieces.append(jnp.where(lane < SSM_HEAD_DIM, a, b))
    return jnp.concatenate(pieces, axis=1)


def _group_rmsnorm(u, g):
    outs = []
    for k in range(SSM_GROUPS):
        ug = u[:, k * GROUP_WIDTH:(k + 1) * GROUP_WIDTH]
        ms = jnp.mean(ug * ug, axis=-1, keepdims=True)
        outs.append(ug * lax.rsqrt(ms + EPS))
    return jnp.concatenate(outs, axis=1) * g


def _memkv_body(mem_ref, g_ref, wk_ref, wv_ref, k_ref, v_ref, kb_ref, vb_ref):
    mn = _rms(mem_ref[...], g_ref[...]).astype(BF16)
    k = _dot(mn, wk_ref[...])
    v = _dot(mn, wv_ref[...])
    k_ref[...] = k
    v_ref[...] = v
    kb_ref[...] = k.astype(BF16)
    vb_ref[...] = v.astype(BF16)


def _mem_kv(mem2d, norm_mem, wk, wv):
    rows = mem2d.shape[0]
    nb = rows // N_MEM
    blk = pl.BlockSpec((N_MEM, D_MODEL), lambda b: (b, 0))
    return pl.pallas_call(
        _memkv_body,
        grid=(nb,),
        in_specs=[blk, _const_spec((1, D_MODEL)), _const_spec((D_MODEL, D_MODEL)),
                  _const_spec((D_MODEL, D_MODEL))],
        out_specs=[blk, blk, blk, blk],
        out_shape=[jax.ShapeDtypeStruct((rows, D_MODEL), F32)] * 2
        + [jax.ShapeDtypeStruct((rows, D_MODEL), BF16)] * 2,
        compiler_params=_cparams("arbitrary"),
        name="mem_kv",
    )(mem2d, norm_mem, wk, wv)


def _mix_body(x_ref, gmix_ref, wa_ref, wdtc_ref, wdtr_ref, wb_ref, wmc_ref, bmc_ref,
              dtb_ref, dtbt_ref, alog_ref, alogt_ref, dskip_ref, gssm_ref, wsc_ref, wout_ref,
              h_ref, ssm_ref, mbuf_ref, sbuf_ref,
              st_ref, cbuf_ref, scbuf_ref):
    tq = MIX_TILE
    c = pl.program_id(1)

    @pl.when(c == 0)
    def _():
        st_ref[...] = jnp.zeros_like(st_ref)
        cbuf_ref[0:SUBLANES, :] = jnp.zeros((SUBLANES, SSM_CONV_DIM), F32)
        scbuf_ref[0:SUBLANES, :] = jnp.zeros((SUBLANES, D_MODEL), F32)

    x = x_ref[...]
    xn = _rms(x, gmix_ref[...]).astype(BF16)

    pa = _dot(xn, wa_ref[...])
    z = pa[:, :SSM_INNER]
    u = pa[:, SSM_INNER:]
    cbuf_ref[SUBLANES:SUBLANES + tq, :] = u
    wm = wmc_ref[...]
    conv = u * wm[SSM_CONV - 1:SSM_CONV, :] + bmc_ref[...]
    for k in range(SSM_CONV - 1):
        off = SUBLANES - (SSM_CONV - 1) + k
        conv = conv + cbuf_ref[off:off + tq, :] * wm[k:k + 1, :]
    tail = cbuf_ref[tq + SUBLANES - (SSM_CONV - 1):tq + SUBLANES, :]
    mbuf_ref[...] = tail
    cbuf_ref[SUBLANES - (SSM_CONV - 1):SUBLANES, :] = tail
    xbc = _silu(conv)
    xs = xbc[:, :SSM_INNER]
    bm = xbc[:, SSM_INNER:SSM_INNER + SSM_GROUPS * SSM_STATE]
    cm = xbc[:, SSM_INNER + SSM_GROUPS * SSM_STATE:]

    dt = _softplus(_dot(xn, wdtc_ref[...]) + dtb_ref[...])
    dtt = _softplus(_dot_nt(wdtr_ref[...], xn) + dtbt_ref[...])
    a_row = -jnp.exp(alog_ref[...])
    a_col = -jnp.exp(alogt_ref[...])
    row_i = lax.broadcasted_iota(I32, (tq, tq), 0)
    col_i = lax.broadcasted_iota(I32, (tq, tq), 1)
    causal = row_i >= col_i
    tril = causal.astype(F32)
    triu = (row_i <= col_i).astype(F32)
    a_cum = jnp.dot(tril, dt * a_row, precision=lax.Precision.HIGHEST,
                    preferred_element_type=F32)
    a_cumt = jnp.dot(dtt * a_col, triu, precision=lax.Precision.HIGHEST,
                     preferred_element_type=F32)
    a_last = a_cum[tq - 1:tq, :]

    xdt = xs * _expand_heads(dt)
    in_decay = _expand_heads(jnp.exp(a_cum))
    to_end = _expand_heads(jnp.exp(a_last - a_cum))
    chunk_decay = _expand_heads(jnp.exp(a_last))
    xdt_b = xdt.astype(BF16)
    xend_b = (xdt * to_end).astype(BF16)
    lane = lax.broadcasted_iota(I32, (tq, LANES), 1)

    y_groups = []
    for g in range(SSM_GROUPS):
        cg = cm[:, g * SSM_STATE:(g + 1) * SSM_STATE].astype(BF16)
        bg_f = bm[:, g * SSM_STATE:(g + 1) * SSM_STATE]
        bg = bg_f.astype(BF16)
        scores = _dot_nt(cg, bg)
        gs = slice(g * GROUP_WIDTH, (g + 1) * GROUP_WIDTH)
        st_g = st_ref[:, gs]
        y_off = _dot(cg, st_g.astype(BF16)) * in_decay[:, gs]
        pair_out = []
        for pr in range(HEADS_PER_GROUP // 2):
            h0 = g * HEADS_PER_GROUP + 2 * pr
            xp = xdt_b[:, h0 * SSM_HEAD_DIM:(h0 + 2) * SSM_HEAD_DIM]
            ys = []
            for h in (h0, h0 + 1):
                seg = a_cum[:, h:h + 1] - a_cumt[h:h + 1, :]
                decay = jnp.where(causal, jnp.exp(jnp.minimum(seg, 0.0)), 0.0)
                ys.append(_dot((scores * decay).astype(BF16), xp))
            pair_out.append(jnp.where(lane < SSM_HEAD_DIM, ys[0], ys[1]))
        y_groups.append(jnp.concatenate(pair_out, axis=1) + y_off)
        st_ref[:, gs] = st_g * chunk_decay[:, gs] + _dot(bg_f.T.astype(BF16), xend_b[:, gs])
    y = jnp.concatenate(y_groups, axis=1) + dskip_ref[...] * xs
    y_a = _group_rmsnorm(y * _silu(z), gssm_ref[...])

    @pl.when(c == pl.num_programs(1) - 1)
    def _():
        ssm_ref[...] = st_ref[...].T

    pb = _dot(xn, wb_ref[...])
    sc_b = pb[:, 0:D_MODEL]
    cv = pb[:, D_MODEL:2 * D_MODEL] * pb[:, 2 * D_MODEL:3 * D_MODEL]
    g_a = pb[:, 3 * D_MODEL:4 * D_MODEL]
    g_b = pb[:, 4 * D_MODEL:5 * D_MODEL]
    scbuf_ref[SUBLANES:SUBLANES + tq, :] = cv
    ws = wsc_ref[...]
    uc = cv * ws[SC_CONV - 1:SC_CONV, :]
    for k in range(SC_CONV - 1):
        off = SUBLANES - (SC_CONV - 1) + k
        uc = uc + scbuf_ref[off:off + tq, :] * ws[k:k + 1, :]
    stail = scbuf_ref[tq + SUBLANES - (SC_CONV - 1):tq + SUBLANES, :]
    sbuf_ref[...] = stail
    scbuf_ref[SUBLANES - (SC_CONV - 1):SUBLANES, :] = stail
    merged = _sigmoid(g_a) * y_a + _sigmoid(g_b) * (sc_b * uc)
    h_ref[...] = x + _dot(merged.astype(BF16), wout_ref[...])


def _prompt_mixer(x2d, nb, w):
    t = x2d.shape[0]
    nc = t // nb // MIX_TILE
    tok = pl.BlockSpec((MIX_TILE, D_MODEL), lambda b, c: (b * nc + c, 0))
    return pl.pallas_call(
        _mix_body,
        grid=(nb, nc),
        in_specs=[tok, _const_spec((1, D_MODEL)),
                  _const_spec((D_MODEL, SSM_INNER + SSM_CONV_DIM)),
                  _const_spec((D_MODEL, SSM_HEADS)), _const_spec((SSM_HEADS, D_MODEL)),
                  _const_spec((D_MODEL, 5 * D_MODEL)),
                  _const_spec((SSM_CONV, SSM_CONV_DIM)), _const_spec((1, SSM_CONV_DIM)),
                  _const_spec((1, SSM_HEADS)), _const_spec((SSM_HEADS, 1)),
                  _const_spec((1, SSM_HEADS)), _const_spec((SSM_HEADS, 1)),
                  _const_spec((1, SSM_INNER)), _const_spec((1, SSM_INNER)),
                  _const_spec((SC_CONV, D_MODEL)), _const_spec((D_MODEL, D_MODEL))],
        out_specs=[tok,
                   pl.BlockSpec((None, SSM_INNER, SSM_STATE), lambda b, c: (b, 0, 0)),
                   pl.BlockSpec((None, SSM_CONV - 1, SSM_CONV_DIM), lambda b, c: (b, 0, 0)),
                   pl.BlockSpec((None, SC_CONV - 1, D_MODEL), lambda b, c: (b, 0, 0))],
        out_shape=[jax.ShapeDtypeStruct((t, D_MODEL), F32),
                   jax.ShapeDtypeStruct((nb, SSM_INNER, SSM_STATE), F32),
                   jax.ShapeDtypeStruct((nb, SSM_CONV - 1, SSM_CONV_DIM), F32),
                   jax.ShapeDtypeStruct((nb, SC_CONV - 1, D_MODEL), F32)],
        scratch_shapes=[pltpu.VMEM((SSM_STATE, SSM_INNER), F32),
                        pltpu.VMEM((MIX_TILE + SUBLANES, SSM_CONV_DIM), F32),
                        pltpu.VMEM((MIX_TILE + SUBLANES, D_MODEL), F32)],
        compiler_params=_cparams("arbitrary", "arbitrary"),
        name="prompt_mixer",
    )(x2d, w["norm_mix"], w["w_a"], w["w_dt"], w["w_dt_t"], w["w_b"], w["w_mconv"], w["b_mconv"],
      w["dt_bias"], w["dt_bias_t"], w["a_log"], w["a_log_t"], w["d_skip"], w["norm_ssm"],
      w["w_sconv"], w["w_out"])


def _router_tail(h2, gmoe_ref, wr_ref, br_ref, h2_ref, hn_ref, lg_ref):
    h2_ref[...] = h2
    hn = _rms(h2, gmoe_ref[...])
    hn_ref[...] = hn
    lg_ref[...] = _dot(hn.astype(BF16), wr_ref[...]) + br_ref[...]


def _attn_body(h_ref, gx_ref, wq_ref, k_ref, v_ref, wo_ref, gmoe_ref, wr_ref, br_ref,
               h2_ref, hn_ref, lg_ref):
    h = h_ref[...]
    hn = _rms(h, gx_ref[...]).astype(BF16)
    q = _dot(hn, wq_ref[...]).astype(BF16)
    outs = []
    for hd in range(XA_HEADS):
        sl = slice(hd * XA_HEAD_DIM, (hd + 1) * XA_HEAD_DIM)
        s = _dot_nt(q[:, sl], k_ref[:, sl]) * (XA_HEAD_DIM ** -0.5)
        e = jnp.exp(s - jnp.max(s, axis=-1, keepdims=True))
        p = e / jnp.sum(e, axis=-1, keepdims=True)
        outs.append(_dot(p.astype(BF16), v_ref[:, sl]))
    o = jnp.concatenate(outs, axis=1).astype(BF16)
    h2 = h + _dot(o, wo_ref[...])
    _router_tail(h2, gmoe_ref, wr_ref, br_ref, h2_ref, hn_ref, lg_ref)


def _prompt_attn(h1, kb, vb, nb, w):
    t = h1.shape[0]
    nc = t // nb // MIX_TILE
    tok = pl.BlockSpec((MIX_TILE, D_MODEL), lambda b, c: (b * nc + c, 0))
    kv = pl.BlockSpec((N_MEM, D_MODEL), lambda b, c: (b, 0))
    return pl.pallas_call(
        _attn_body,
        grid=(nb, nc),
        in_specs=[tok, _const_spec((1, D_MODEL)), _const_spec((D_MODEL, D_MODEL)), kv, kv,
                  _const_spec((D_MODEL, D_MODEL)), _const_spec((1, D_MODEL)),
                  _const_spec((D_MODEL, N_EXPERTS)), _const_spec((1, N_EXPERTS))],
        out_specs=[tok, tok, pl.BlockSpec((MIX_TILE, N_EXPERTS), lambda b, c: (b * nc + c, 0))],
        out_shape=[jax.ShapeDtypeStruct((t, D_MODEL), F32),
                   jax.ShapeDtypeStruct((t, D_MODEL), F32),
                   jax.ShapeDtypeStruct((t, N_EXPERTS), F32)],
        compiler_params=_cparams("arbitrary", "arbitrary"),
        name="prompt_attn",
    )(h1, w["norm_xattn"], w["w_xq"], kb, vb, w["w_xo"], w["norm_moe"], w["w_router"], w["b_router"])


def _sproj_body(x_ref, gmix_ref, wa_ref, wdtc_ref, wb_ref, wmc_ref, bmc_ref, dtb_ref, alog_ref,
                wsc_ref, mst_ref, sst_ref,
                z_ref, xs_ref, dtx_ref, dec_ref, bm_ref, cm_ref, yb_ref, sga_ref, mnew_ref, snew_ref):
    x = x_ref[...]
    xn = _rms(x, gmix_ref[...]).astype(BF16)
    pa = _dot(xn, wa_ref[...])
    z_ref[...] = pa[:, :SSM_INNER]
    u = pa[:, SSM_INNER:]
    wm = wmc_ref[...]
    conv = u * wm[SSM_CONV - 1:SSM_CONV, :] + bmc_ref[...]
    for k in range(SSM_CONV - 1):
        conv = conv + mst_ref[k] * wm[k:k + 1, :]
    for k in range(SSM_CONV - 2):
        mnew_ref[k] = mst_ref[k + 1]
    mnew_ref[SSM_CONV - 2] = u
    xbc = _silu(conv)
    xs = xbc[:, :SSM_INNER]
    xs_ref[...] = xs
    bm_ref[...] = xbc[:, SSM_INNER:SSM_INNER + SSM_GROUPS * SSM_STATE]
    cm_ref[...] = xbc[:, SSM_INNER + SSM_GROUPS * SSM_STATE:]
    dt = _softplus(_dot(xn, wdtc_ref[...]) + dtb_ref[...])
    dec_ref[...] = jnp.exp(dt * (-jnp.exp(alog_ref[...])))
    dtx_ref[...] = xs * _expand_heads(dt)
    pb = _dot(xn, wb_ref[...])
    cv = pb[:, D_MODEL:2 * D_MODEL] * pb[:, 2 * D_MODEL:3 * D_MODEL]
    ws = wsc_ref[...]
    uc = cv * ws[SC_CONV - 1:SC_CONV, :]
    for k in range(SC_CONV - 1):
        uc = uc + sst_ref[k] * ws[k:k + 1, :]
    for k in range(SC_CONV - 2):
        snew_ref[k] = sst_ref[k + 1]
    snew_ref[SC_CONV - 2] = cv
    yb_ref[...] = _sigmoid(pb[:, 4 * D_MODEL:5 * D_MODEL]) * (pb[:, 0:D_MODEL] * uc)
    sga_ref[...] = _sigmoid(pb[:, 3 * D_MODEL:4 * D_MODEL])


def _sample_proj(x, mstate_t, sstate_t, w):
    nb = x.shape[0]
    f = lambda *s: jax.ShapeDtypeStruct(s, F32)
    return pl.pallas_call(
        _sproj_body,
        out_shape=[f(nb, SSM_INNER), f(nb, SSM_INNER), f(nb, SSM_INNER), f(nb, SSM_HEADS),
                   f(nb, SSM_GROUPS * SSM_STATE), f(nb, SSM_GROUPS * SSM_STATE),
                   f(nb, D_MODEL), f(nb, D_MODEL),
                   f(SSM_CONV - 1, nb, SSM_CONV_DIM), f(SC_CONV - 1, nb, D_MODEL)],
        compiler_params=pltpu.CompilerParams(vmem_limit_bytes=VMEM_LIMIT),
        name="sample_proj",
    )(x, w["norm_mix"], w["w_a"], w["w_dt"], w["w_b"], w["w_mconv"], w["b_mconv"], w["dt_bias"],
      w["a_log"], w["w_sconv"], mstate_t, sstate_t)


def _sstate_body(dec_ref, s_ref, dtx_ref, bm_ref, cm_ref, snew_ref, y_ref):
    i = pl.program_id(0)
    rows_per_blk = LANES
    for j in range(STATE_BB):
        b = i * STATE_BB + j
        dtx_row = dtx_ref[j:j + 1, :]
        y_parts = []
        for g in range(SSM_GROUPS):
            b_row = bm_ref[j:j + 1, g * SSM_STATE:(g + 1) * SSM_STATE]
            c_row = cm_ref[j:j + 1, g * SSM_STATE:(g + 1) * SSM_STATE].astype(BF16)
            new_blocks = []
            for q in range(GROUP_WIDTH // rows_per_blk):
                r0 = g * GROUP_WIDTH + q * rows_per_blk
                dcol = jnp.broadcast_to(dtx_row[:, r0:r0 + rows_per_blk], (rows_per_blk, LANES)).T
                sub = []
                for hh in range(rows_per_blk // SSM_HEAD_DIM):
                    h = r0 // SSM_HEAD_DIM + hh
                    lo = hh * SSM_HEAD_DIM
                    s_old = s_ref[j, r0 + lo:r0 + lo + SSM_HEAD_DIM, :]
                    sub.append(s_old * dec_ref[b, h] + dcol[lo:lo + SSM_HEAD_DIM, :] * b_row)
                blk = jnp.concatenate(sub, axis=0)
                snew_ref[j, r0:r0 + rows_per_blk, :] = blk
                new_blocks.append(blk.astype(BF16))
            s_g = jnp.concatenate(new_blocks, axis=0)
            y_parts.append(_dot_nt(c_row, s_g))
        y_ref[j:j + 1, :] = jnp.concatenate(y_parts, axis=1)


def _sample_state(dec, state, dtx, bm, cm):
    nb = state.shape[0]
    row = lambda wdt: pl.BlockSpec((STATE_BB, wdt), lambda i, dec: (i, 0))
    st = pl.BlockSpec((STATE_BB, SSM_INNER, SSM_STATE), lambda i, dec: (i, 0, 0))
    return pl.pallas_call(
        _sstate_body,
        grid_spec=pltpu.PrefetchScalarGridSpec(
            num_scalar_prefetch=1, grid=(nb // STATE_BB,),
            in_specs=[st, row(SSM_INNER), row(SSM_GROUPS * SSM_STATE), row(SSM_GROUPS * SSM_STATE)],
            out_specs=[st, row(SSM_INNER)]),
        out_shape=[jax.ShapeDtypeStruct(state.shape, F32), jax.ShapeDtypeStruct((nb, SSM_INNER), F32)],
        compiler_params=_cparams("arbitrary"),
        name="sample_state",
    )(dec, state, dtx, bm, cm)


def _sfin1_body(x_ref, y_ref, xs_ref, z_ref, yb_ref, sga_ref, dskip_ref, gssm_ref, wout_ref,
                gx_ref, wq_ref, h_ref, q_ref):
    y = y_ref[...] + dskip_ref[...] * xs_ref[...]
    y_a = _group_rmsnorm(y * _silu(z_ref[...]), gssm_ref[...])
    merged = sga_ref[...] * y_a + yb_ref[...]
    h = x_ref[...] + _dot(merged.astype(BF16), wout_ref[...])
    h_ref[...] = h
    q_ref[...] = _dot(_rms(h, gx_ref[...]).astype(BF16), wq_ref[...])


def _sample_fin1(x, y, xs, z, yb, sga, w):
    nb = x.shape[0]
    return pl.pallas_call(
        _sfin1_body,
        out_shape=[jax.ShapeDtypeStruct((nb, D_MODEL), F32)] * 2,
        compiler_params=pltpu.CompilerParams(vmem_limit_bytes=VMEM_LIMIT),
        name="sample_fin1",
    )(x, y, xs, z, yb, sga, w["d_skip"], w["norm_ssm"], w["w_out"], w["norm_xattn"], w["w_xq"])


def _sattn_body(q_ref, k_ref, v_ref, o_ref):
    for j in range(ATTN_BB):
        q_row = q_ref[j]
        q4 = jnp.concatenate([q_row[:, h * XA_HEAD_DIM:(h + 1) * XA_HEAD_DIM]
                              for h in range(XA_HEADS)], axis=0)
        s = jnp.sum(k_ref[j] * q4[None], axis=-1, keepdims=True) * (XA_HEAD_DIM ** -0.5)
        e = jnp.exp(s - jnp.max(s, axis=0, keepdims=True))
        p = e / jnp.sum(e, axis=0, keepdims=True)
        o4 = jnp.sum(p * v_ref[j], axis=0)
        o_ref[j] = jnp.concatenate([o4[h:h + 1, :] for h in range(XA_HEADS)], axis=1)


def _sample_attn(q3, k3, v3):
    nb = q3.shape[0]
    qs = pl.BlockSpec((ATTN_BB, 1, D_MODEL), lambda i: (i, 0, 0))
    kv = pl.BlockSpec((ATTN_BB, N_MEM, XA_HEADS, XA_HEAD_DIM), lambda i: (i, 0, 0, 0))
    return pl.pallas_call(
        _sattn_body,
        grid=(nb // ATTN_BB,),
        in_specs=[qs, kv, kv],
        out_specs=qs,
        out_shape=jax.ShapeDtypeStruct((nb, 1, D_MODEL), F32),
        compiler_params=_cparams("arbitrary"),
        name="sample_attn",
    )(q3, k3, v3)


def _sfin2_body(h_ref, o_ref, wo_ref, gmoe_ref, wr_ref, br_ref, h2_ref, hn_ref, lg_ref):
    h2 = h_ref[...] + _dot(o_ref[...].astype(BF16), wo_ref[...])
    _router_tail(h2, gmoe_ref, wr_ref, br_ref, h2_ref, hn_ref, lg_ref)


def _sample_fin2(h1, o, w):
    nb = h1.shape[0]
    return pl.pallas_call(
        _sfin2_body,
        out_shape=[jax.ShapeDtypeStruct((nb, D_MODEL), F32)] * 2
        + [jax.ShapeDtypeStruct((nb, N_EXPERTS), F32)],
        compiler_params=pltpu.CompilerParams(vmem_limit_bytes=VMEM_LIMIT),
        name="sample_fin2",
    )(h1, o, w["w_xo"], w["norm_moe"], w["w_router"], w["b_router"])


def _route_body(lg_ref, e_ref, g_ref, r_ref, cnt_ref, carry_ref):
    i = pl.program_id(0)
    tt = lg_ref.shape[0]

    @pl.when(i == 0)
    def _():
        carry_ref[...] = jnp.zeros_like(carry_ref)

    work = lg_ref[...]
    lane = lax.broadcasted_iota(I32, (tt, N_EXPERTS), 1).astype(F32)
    vals, idxs, hots = [], [], []
    for _ in range(TOP_K):
        m = jnp.max(work, axis=-1, keepdims=True)
        idx = jnp.min(jnp.where(work == m, lane, float(N_EXPERTS)), axis=-1, keepdims=True)
        hot = lane == idx
        vals.append(m)
        idxs.append(idx)
        hots.append(hot)
        work = jnp.where(hot, -jnp.inf, work)
    exps = [jnp.exp(v - vals[0]) for v in vals]
    tot = exps[0]
    for e in exps[1:]:
        tot = tot + e
    assigned = hots[0]
    for hot in hots[1:]:
        assigned = assigned | hot
    a = assigned.astype(F32)
    r_i = lax.broadcasted_iota(I32, (tt, tt), 0)
    c_i = lax.broadcasted_iota(I32, (tt, tt), 1)
    before = (r_i > c_i).astype(BF16)
    rank_all = _dot(before, a.astype(BF16)) + carry_ref[...]
    carry = carry_ref[...] + jnp.sum(a, axis=0, keepdims=True)
    carry_ref[...] = carry
    cnt_ref[...] = carry
    k_lane = lax.broadcasted_iota(I32, (tt, TOP_K), 1)
    e_out = jnp.zeros((tt, TOP_K), F32)
    g_out = jnp.zeros((tt, TOP_K), F32)
    r_out = jnp.zeros((tt, TOP_K), F32)
    for k in range(TOP_K):
        rk = jnp.sum(jnp.where(hots[k], rank_all, 0.0), axis=-1, keepdims=True)
        e_out = jnp.where(k_lane == k, idxs[k], e_out)
        g_out = jnp.where(k_lane == k, exps[k] / tot, g_out)
        r_out = jnp.where(k_lane == k, rk, r_out)
    e_ref[...] = e_out.astype(I32)
    g_ref[...] = g_out
    r_ref[...] = r_out.astype(I32)


def _route(logits, tt):
    t = logits.shape[0]
    tk = pl.BlockSpec((tt, TOP_K), lambda i: (i, 0))
    return pl.pallas_call(
        _route_body,
        grid=(t // tt,),
        in_specs=[pl.BlockSpec((tt, N_EXPERTS), lambda i: (i, 0))],
        out_specs=[tk, tk, tk, pl.BlockSpec((1, N_EXPERTS), lambda i: (0, 0))],
        out_shape=[jax.ShapeDtypeStruct((t, TOP_K), I32), jax.ShapeDtypeStruct((t, TOP_K), F32),
                   jax.ShapeDtypeStruct((t, TOP_K), I32), jax.ShapeDtypeStruct((1, N_EXPERTS), F32)],
        scratch_shapes=[pltpu.VMEM((1, N_EXPERTS), F32)],
        compiler_params=_cparams("arbitrary"),
        name="moe_route",
    )(logits)


def _dispatch_body(tm, tt, pos_ref, zstart_ref, x_ref, o_hbm, zero_ref, sem):
    i = pl.program_id(0)

    def zero_copy(j):
        return pltpu.make_async_copy(zero_ref, o_hbm.at[pl.ds(pl.multiple_of(zstart_ref[j], tm), tm), :], sem)

    @pl.when(i == 0)
    def _():
        zero_ref[...] = jnp.zeros_like(zero_ref)

        def start(j, carry):
            @pl.when(zstart_ref[j] >= 0)
            def _():
                zero_copy(j).start()
            return carry

        def wait(j, carry):
            @pl.when(zstart_ref[j] >= 0)
            def _():
                zero_copy(j).wait()
            return carry

        lax.fori_loop(0, zstart_ref.shape[0], start, 0)
        lax.fori_loop(0, zstart_ref.shape[0], wait, 0)

    def row_copy(t, k):
        src = x_ref.at[pl.ds(t, 1), :]
        dst = o_hbm.at[pl.ds(pos_ref[0, 0, t * TOP_K + k], 1), :]
        return pltpu.make_async_copy(src, dst, sem)

    def issue(t, carry):
        for k in range(TOP_K):
            row_copy(t, k).start()
        return carry

    lax.fori_loop(0, tt, issue, 0)
    n = tt * TOP_K
    pltpu.make_async_copy(o_hbm.at[pl.ds(0, n), :], o_hbm.at[pl.ds(0, n), :], sem).wait()


def _dispatch(pos3, zero_starts, hn, n_rows, tm, tt):
    t = hn.shape[0]
    return pl.pallas_call(
        functools.partial(_dispatch_body, tm, tt),
        grid_spec=pltpu.PrefetchScalarGridSpec(
            num_scalar_prefetch=0, grid=(t // tt,),
            in_specs=[pl.BlockSpec((1, 1, tt * TOP_K), lambda i: (i, 0, 0), memory_space=pltpu.SMEM),
                      pl.BlockSpec(memory_space=pltpu.SMEM),
                      pl.BlockSpec((tt, D_MODEL), lambda i: (i, 0))],
            out_specs=pl.BlockSpec(memory_space=pl.ANY),
            scratch_shapes=[pltpu.VMEM((tm, D_MODEL), F32), pltpu.SemaphoreType.DMA]),
        out_shape=jax.ShapeDtypeStruct((n_rows, D_MODEL), F32),
        compiler_params=_cparams("arbitrary"),
        name="moe_dispatch",
    )(pos3, zero_starts, hn)


def _expert_body(be_ref, nu_ref, x_ref, wgu_ref, bgu_ref, wdn_ref, bdn_ref, y_ref):
    i = pl.program_id(0)

    @pl.when(i < nu_ref[0])
    def _():
        gu = _dot(x_ref[...].astype(BF16), wgu_ref[...]) + bgu_ref[...]
        gate = jnp.minimum(gu[:, :D_FF], SWIGLU_LIMIT)
        up = jnp.clip(gu[:, D_FF:], -SWIGLU_LIMIT, SWIGLU_LIMIT)
        act = (up + 1.0) * (gate * _sigmoid(SWIGLU_ALPHA * gate))
        y_ref[...] = _dot(act.astype(BF16), wdn_ref[...]) + bdn_ref[...]

    @pl.when(i >= nu_ref[0])
    def _():
        y_ref[...] = jnp.zeros_like(y_ref)


def _experts(block_e, n_used, xs, wgu, bgu, wdn, bdn, tm):
    n_rows = xs.shape[0]
    return pl.pallas_call(
        _expert_body,
        grid_spec=pltpu.PrefetchScalarGridSpec(
            num_scalar_prefetch=2, grid=(n_rows // tm,),
            in_specs=[pl.BlockSpec((tm, D_MODEL), lambda i, be, nu: (jnp.minimum(i, nu[0] - 1), 0)),
                      pl.BlockSpec((None, D_MODEL, 2 * D_FF), lambda i, be, nu: (be[i], 0, 0)),
                      pl.BlockSpec((None, 1, 2 * D_FF), lambda i, be, nu: (be[i], 0, 0)),
                      pl.BlockSpec((None, D_FF, D_MODEL), lambda i, be, nu: (be[i], 0, 0)),
                      pl.BlockSpec((None, 1, D_MODEL), lambda i, be, nu: (be[i], 0, 0))],
            out_specs=pl.BlockSpec((tm, D_MODEL), lambda i, be, nu: (i, 0))),
        out_shape=jax.ShapeDtypeStruct((n_rows, D_MODEL), F32),
        compiler_params=_cparams("arbitrary"),
        name="moe_experts",
    )(block_e, n_used, xs, wgu, bgu, wdn, bdn)


def _combine_body(tt, pos_ref, g_ref, h_ref, gfin_ref, ys_hbm, y_ref, buf_ref, sem):
    def row_copy(t, k):
        src = ys_hbm.at[pl.ds(pos_ref[0, 0, t * TOP_K + k], 1), :]
        return pltpu.make_async_copy(src, buf_ref.at[k, pl.ds(t, 1), :], sem)

    def issue(t, carry):
        for k in range(TOP_K):
            row_copy(t, k).start()
        return carry

    lax.fori_loop(0, tt, issue, 0)
    for k in range(TOP_K):
        pltpu.make_async_copy(ys_hbm.at[pl.ds(0, tt), :], buf_ref.at[k], sem).wait()
    gates = g_ref[...]
    out = h_ref[...]
    for k in range(TOP_K):
        out = out + gates[:, k:k + 1] * buf_ref[k]
    y_ref[...] = _rms(out, gfin_ref[...])


def _combine(pos3, gates, h2, norm_final, ys, tt):
    t = h2.shape[0]
    tok = pl.BlockSpec((tt, D_MODEL), lambda i: (i, 0))
    return pl.pallas_call(
        functools.partial(_combine_body, tt),
        grid_spec=pltpu.PrefetchScalarGridSpec(
            num_scalar_prefetch=0, grid=(t // tt,),
            in_specs=[pl.BlockSpec((1, 1, tt * TOP_K), lambda i: (i, 0, 0), memory_space=pltpu.SMEM),
                      pl.BlockSpec((tt, TOP_K), lambda i: (i, 0)), tok,
                      pl.BlockSpec((1, D_MODEL), lambda i: (0, 0)),
                      pl.BlockSpec(memory_space=pl.ANY)],
            out_specs=tok,
            scratch_shapes=[pltpu.VMEM((TOP_K, tt, D_MODEL), F32), pltpu.SemaphoreType.DMA]),
        out_shape=jax.ShapeDtypeStruct((t, D_MODEL), F32),
        compiler_params=_cparams("arbitrary"),
        name="moe_combine",
    )(pos3, gates, h2, norm_final, ys)


def _moe_and_final_norm(hn, logits, h2, w, tt, tm):
    t = hn.shape[0]
    eidx, gates, rank, counts_f = _route(logits, tt)
    counts = counts_f[0].astype(I32)
    padded = (counts + tm - 1) // tm * tm
    pad_end = jnp.cumsum(padded)
    start = pad_end - padded
    onehot = eidx[..., None] == jnp.arange(N_EXPERTS, dtype=I32)
    pos = rank + jnp.sum(jnp.where(onehot, start, 0), axis=-1)
    pos3 = pos.reshape(t // tt, 1, tt * TOP_K)
    n_blocks = (t * TOP_K + N_EXPERTS * (tm - 1)) // tm
    n_rows = n_blocks * tm
    block_start = jnp.arange(n_blocks, dtype=I32) * tm
    block_e = jnp.minimum(jnp.sum(block_start[:, None] >= pad_end[None, :], axis=-1), N_EXPERTS - 1).astype(I32)
    n_used = (pad_end[-1:] // tm).astype(I32)
    zero_starts = jnp.concatenate([jnp.where(padded > 0, pad_end - tm, -1),
                                   jnp.where(block_start >= pad_end[-1], block_start, -1)]).astype(I32)
    xs = _dispatch(pos3, zero_starts, hn, n_rows, tm, tt)
    ys = _experts(block_e, n_used, xs, w["w_gate_up"], w["b_gate_up"], w["w_down"], w["b_down"], tm)
    return _combine(pos3, gates, h2, w["norm_final"], ys, tt)


def kernel(x_prompt, x_sample, mem_prompt, state_ssm, state_mamba_conv, state_short_conv, cache_mem_k, cache_mem_v, norm_mix, w_in, w_mconv, b_mconv, dt_bias, a_log, d_skip, norm_ssm, w_sconv, w_out, norm_xattn, norm_mem, w_xq, w_xk, w_xv, w_xo, norm_moe, w_router, b_router, w_gate_up, b_gate_up, w_down, b_down, norm_final):
    nbp, seq, _ = x_prompt.shape
    nbs = x_sample.shape[0]
    dt_lo = SSM_INNER + SSM_CONV_DIM
    w_in0 = w_in[0]
    w_dt = w_in0[:, dt_lo:dt_lo + SSM_HEADS]
    w = {
        "norm_mix": norm_mix, "norm_ssm": norm_ssm, "norm_xattn": norm_xattn, "norm_moe": norm_moe,
        "norm_final": norm_final.reshape(1, D_MODEL),
        "w_a": w_in0[:, :dt_lo].astype(BF16),
        "w_dt": w_dt.astype(BF16), "w_dt_t": w_dt.T.astype(BF16),
        "w_b": w_in0[:, dt_lo + SSM_HEADS:].astype(BF16),
        "w_mconv": w_mconv[0], "b_mconv": b_mconv,
        "dt_bias": dt_bias, "dt_bias_t": dt_bias.reshape(SSM_HEADS, 1),
        "a_log": a_log, "a_log_t": a_log.reshape(SSM_HEADS, 1),
        "d_skip": jnp.repeat(d_skip, SSM_HEAD_DIM, axis=1),
        "w_sconv": w_sconv[0], "w_out": w_out[0].astype(BF16),
        "w_xq": w_xq[0].astype(BF16), "w_xo": w_xo[0].astype(BF16),
        "w_router": w_router[0].astype(BF16), "b_router": b_router,
        "w_gate_up": w_gate_up[0].astype(BF16), "b_gate_up": b_gate_up[0].reshape(N_EXPERTS, 1, 2 * D_FF),
        "w_down": w_down[0].astype(BF16), "b_down": b_down[0].reshape(N_EXPERTS, 1, D_MODEL),
    }

    k_p, v_p, kb, vb = _mem_kv(mem_prompt.reshape(nbp * N_MEM, D_MODEL), norm_mem,
                               w_xk[0].astype(BF16), w_xv[0].astype(BF16))
    h1, ssm_p, mconv_p, sconv_p = _prompt_mixer(x_prompt.reshape(nbp * seq, D_MODEL), nbp, w)
    h2, hn, logits = _prompt_attn(h1, kb, vb, nbp, w)
    y_prompt = _moe_and_final_norm(hn, logits, h2, w, MIX_TILE, MOE_ROW_TILE)

    xs2 = x_sample.reshape(nbs, D_MODEL)
    mstate_t = jnp.transpose(state_mamba_conv[0], (1, 0, 2))
    sstate_t = jnp.transpose(state_short_conv[0], (1, 0, 2))
    z, xs_, dtx, dec, bm, cm, yb, sga, mnew_t, snew_t = _sample_proj(xs2, mstate_t, sstate_t, w)
    ssm_s, y_s = _sample_state(dec, state_ssm[0].reshape(nbs, SSM_INNER, SSM_STATE), dtx, bm, cm)
    h1s, q_s = _sample_fin1(xs2, y_s, xs_, z, yb, sga, w)
    o_s = _sample_attn(q_s.reshape(nbs, 1, D_MODEL),
                       cache_mem_k[0], cache_mem_v[0])
    h2s, hns, logits_s = _sample_fin2(h1s, o_s.reshape(nbs, D_MODEL), w)
    y_sample = _moe_and_final_norm(hns, logits_s, h2s, w, nbs, LANES)

    return (y_prompt.reshape(nbp, seq, D_MODEL),
            y_sample.reshape(nbs, 1, D_MODEL),
            ssm_p.reshape(1, nbp, SSM_HEADS, SSM_HEAD_DIM, SSM_STATE),
            mconv_p[None], sconv_p[None],
            k_p.reshape(1, nbp, N_MEM, XA_HEADS, XA_HEAD_DIM),
            v_p.reshape(1, nbp, N_MEM, XA_HEADS, XA_HEAD_DIM),
            ssm_s.reshape(1, nbs, SSM_HEADS, SSM_HEAD_DIM, SSM_STATE),
            jnp.transpose(mnew_t, (1, 0, 2))[None],
            jnp.transpose(snew_t, (1, 0, 2))[None])
```

```python
import functools

import jax
import jax.numpy as jnp
from jax import lax
from jax.experimental import pallas as pl
from jax.experimental.pallas import tpu as pltpu

F32 = jnp.float32
BF16 = jnp.bfloat16
I32 = jnp.int32

D_MODEL = 1024
N_MEM = 256
SSM_HEADS = 16
SSM_HEAD_DIM = 64
SSM_INNER = SSM_HEADS * SSM_HEAD_DIM
SSM_STATE = 128
SSM_GROUPS = 4
HEADS_PER_GROUP = SSM_HEADS // SSM_GROUPS
GROUP_WIDTH = SSM_INNER // SSM_GROUPS
SSM_CONV = 4
SSM_CONV_DIM = SSM_INNER + 2 * SSM_GROUPS * SSM_STATE
SC_CONV = 3
XA_HEADS = 4
XA_HEAD_DIM = D_MODEL // XA_HEADS
N_EXPERTS = 32
TOP_K = 4
D_FF = D_MODEL
SWIGLU_LIMIT = 7.0
SWIGLU_ALPHA = 1.702
EPS = 1e-6

LANES = 128
SUBLANES = 8
VMEM_LIMIT = 56 * 1024 * 1024

MIX_TILE = 256
MOE_ROW_TILE = 256
STATE_BB = 8
ATTN_BB = 4

NT_DIMS = (((1,), (1,)), ((), ()))
TN_DIMS = (((0,), (0,)), ((), ()))


def _cparams(*sem):
    return pltpu.CompilerParams(dimension_semantics=sem, vmem_limit_bytes=VMEM_LIMIT)


def _const_spec(shape):
    nd = len(shape)
    return pl.BlockSpec(shape, lambda *_: (0,) * nd, pipeline_mode=pl.Buffered(1))


def _sigmoid(x):
    return 1.0 / (1.0 + jnp.exp(-x))


def _silu(x):
    return x * _sigmoid(x)


def _softplus(x):
    return jnp.maximum(x, 0.0) + jnp.log(1.0 + jnp.exp(-jnp.abs(x)))


def _rms(x, g):
    ms = jnp.mean(x * x, axis=-1, keepdims=True)
    return x * lax.rsqrt(ms + EPS) * g


def _dot(a, b):
    return jnp.dot(a, b, preferred_element_type=F32)


def _dot_nt(a, b):
    return lax.dot_general(a, b, NT_DIMS, preferred_element_type=F32)


def _expand_heads(v):
    rows = v.shape[0]
    lane = lax.broadcasted_iota(I32, (rows, LANES), 1)
    pieces = []
    for j in range(SSM_HEADS // 2):
        a = jnp.broadcast_to(v[:, 2 * j:2 * j + 1], (rows, LANES))
        b = jnp.broadcast_to(v[:, 2 * j + 1:2 * j + 2], (rows, LANES))
        pieces.append(jnp.where(lane < SSM_HEAD_DIM, a, b))
    return jnp.concatenate(pieces, axis=1)


def _group_rmsnorm(u, g):
    outs = []
    for k in range(SSM_GROUPS):
        ug = u[:, k * GROUP_WIDTH:(k + 1) * GROUP_WIDTH]
        ms = jnp.mean(ug * ug, axis=-1, keepdims=True)
        outs.append(ug * lax.rsqrt(ms + EPS))
    return jnp.concatenate(outs, axis=1) * g


def _memkv_body(mem_ref, g_ref, wk_ref, wv_ref, k_ref, v_ref, kb_ref, vb_ref):
    mn = _rms(mem_ref[...], g_ref[...]).astype(BF16)
    k = _dot(mn, wk_ref[...])
    v = _dot(mn, wv_ref[...])
    k_ref[...] = k
    v_ref[...] = v
    kb_ref[...] = k.astype(BF16)
    vb_ref[...] = v.astype(BF16)


def _mem_kv(mem2d, norm_mem, wk, wv):
    rows = mem2d.shape[0]
    nb = rows // N_MEM
    blk = pl.BlockSpec((N_MEM, D_MODEL), lambda b: (b, 0))
    return pl.pallas_call(
        _memkv_body,
        grid=(nb,),
        in_specs=[blk, _const_spec((1, D_MODEL)), _const_spec((D_MODEL, D_MODEL)),
                  _const_spec((D_MODEL, D_MODEL))],
        out_specs=[blk, blk, blk, blk],
        out_shape=[jax.ShapeDtypeStruct((rows, D_MODEL), F32)] * 2
        + [jax.ShapeDtypeStruct((rows, D_MODEL), BF16)] * 2,
        compiler_params=_cparams("arbitrary"),
        name="mem_kv",
    )(mem2d, norm_mem, wk, wv)


def _mix_body(x_ref, gmix_ref, wa_ref, wdtc_ref, wdtr_ref, wb_ref, wmc_ref, bmc_ref,
              dtb_ref, dtbt_ref, alog_ref, alogt_ref, dskip_ref, gssm_ref, wsc_ref, wout_ref,
              h_ref, ssm_ref, mbuf_ref, sbuf_ref,
              st_ref, cbuf_ref, scbuf_ref):
    tq = MIX_TILE
    c = pl.program_id(1)

    @pl.when(c == 0)
    def _():
        st_ref[...] = jnp.zeros_like(st_ref)
        cbuf_ref[0:SUBLANES, :] = jnp.zeros((SUBLANES, SSM_CONV_DIM), F32)
        scbuf_ref[0:SUBLANES, :] = jnp.zeros((SUBLANES, D_MODEL), F32)

    x = x_ref[...]
    xn = _rms(x, gmix_ref[...]).astype(BF16)

    pa = _dot(xn, wa_ref[...])
    z = pa[:, :SSM_INNER]
    u = pa[:, SSM_INNER:]
    cbuf_ref[SUBLANES:SUBLANES + tq, :] = u
    wm = wmc_ref[...]
    conv = u * wm[SSM_CONV - 1:SSM_CONV, :] + bmc_ref[...]
    for k in range(SSM_CONV - 1):
        off = SUBLANES - (SSM_CONV - 1) + k
        conv = conv + cbuf_ref[off:off + tq, :] * wm[k:k + 1, :]
    tail = cbuf_ref[tq + SUBLANES - (SSM_CONV - 1):tq + SUBLANES, :]
    mbuf_ref[...] = tail
    cbuf_ref[SUBLANES - (SSM_CONV - 1):SUBLANES, :] = tail
    xbc = _silu(conv)
    xs = xbc[:, :SSM_INNER]
    bm = xbc[:, SSM_INNER:SSM_INNER + SSM_GROUPS * SSM_STATE]
    cm = xbc[:, SSM_INNER + SSM_GROUPS * SSM_STATE:]

    dt = _softplus(_dot(xn, wdtc_ref[...]) + dtb_ref[...])
    dtt = _softplus(_dot_nt(wdtr_ref[...], xn) + dtbt_ref[...])
    a_row = -jnp.exp(alog_ref[...])
    a_col = -jnp.exp(alogt_ref[...])
    row_i = lax.broadcasted_iota(I32, (tq, tq), 0)
    col_i = lax.broadcasted_iota(I32, (tq, tq), 1)
    causal = row_i >= col_i
    tril = causal.astype(F32)
    triu = (row_i <= col_i).astype(F32)
    a_cum = jnp.dot(tril, dt * a_row, precision=lax.Precision.HIGHEST,
                    preferred_element_type=F32)
    a_cumt = jnp.dot(dtt * a_col, triu, precision=lax.Precision.HIGHEST,
                     preferred_element_type=F32)
    a_last = a_cum[tq - 1:tq, :]

    xdt = xs * _expand_heads(dt)
    in_decay = _expand_heads(jnp.exp(a_cum))
    to_end = _expand_heads(jnp.exp(a_last - a_cum))
    chunk_decay = _expand_heads(jnp.exp(a_last))
    xdt_b = xdt.astype(BF16)
    xend_b = (xdt * to_end).astype(BF16)
    lane = lax.broadcasted_iota(I32, (tq, LANES), 1)

    y_groups = []
    for g in range(SSM_GROUPS):
        cg = cm[:, g * SSM_STATE:(g + 1) * SSM_STATE].astype(BF16)
        bg_f = bm[:, g * SSM_STATE:(g + 1) * SSM_STATE]
        bg = bg_f.astype(BF16)
        scores = _dot_nt(cg, bg)
        gs = slice(g * GROUP_WIDTH, (g + 1) * GROUP_WIDTH)
        st_g = st_ref[:, gs]
        y_off = _dot(cg, st_g.astype(BF16)) * in_decay[:, gs]
        pair_out = []
        for pr in range(HEADS_PER_GROUP // 2):
            h0 = g * HEADS_PER_GROUP + 2 * pr
            xp = xdt_b[:, h0 * SSM_HEAD_DIM:(h0 + 2) * SSM_HEAD_DIM]
            ys = []
            for h in (h0, h0 + 1):
                seg = a_cum[:, h:h + 1] - a_cumt[h:h + 1, :]
                decay = jnp.where(causal, jnp.exp(jnp.minimum(seg, 0.0)), 0.0)
                ys.append(_dot((scores * decay).astype(BF16), xp))
            pair_out.append(jnp.where(lane < SSM_HEAD_DIM, ys[0], ys[1]))
        y_groups.append(jnp.concatenate(pair_out, axis=1) + y_off)
        st_ref[:, gs] = st_g * chunk_decay[:, gs] + _dot(bg_f.T.astype(BF16), xend_b[:, gs])
    y = jnp.concatenate(y_groups, axis=1) + dskip_ref[...] * xs
    y_a = _group_rmsnorm(y * _silu(z), gssm_ref[...])

    @pl.when(c == pl.num_programs(1) - 1)
    def _():
        ssm_ref[...] = st_ref[...].T

    pb = _dot(xn, wb_ref[...])
    sc_b = pb[:, 0:D_MODEL]
    cv = pb[:, D_MODEL:2 * D_MODEL] * pb[:, 2 * D_MODEL:3 * D_MODEL]
    g_a = pb[:, 3 * D_MODEL:4 * D_MODEL]
    g_b = pb[:, 4 * D_MODEL:5 * D_MODEL]
    scbuf_ref[SUBLANES:SUBLANES + tq, :] = cv
    ws = wsc_ref[...]
    uc = cv * ws[SC_CONV - 1:SC_CONV, :]
    for k in range(SC_CONV - 1):
        off = SUBLANES - (SC_CONV - 1) + k
        uc = uc + scbuf_ref[off:off + tq, :] * ws[k:k + 1, :]
    stail = scbuf_ref[tq + SUBLANES - (SC_CONV - 1):tq + SUBLANES, :]
    sbuf_ref[...] = stail
    scbuf_ref[SUBLANES - (SC_CONV - 1):SUBLANES, :] = stail
    merged = _sigmoid(g_a) * y_a + _sigmoid(g_b) * (sc_b * uc)
    h_ref[...] = x + _dot(merged.astype(BF16), wout_ref[...])


def _prompt_mixer(x2d, nb, w):
    t = x2d.shape[0]
    nc = t // nb // MIX_TILE
    tok = pl.BlockSpec((MIX_TILE, D_MODEL), lambda b, c: (b * nc + c, 0))
    return pl.pallas_call(
        _mix_body,
        grid=(nb, nc),
        in_specs=[tok, _const_spec((1, D_MODEL)),
                  _const_spec((D_MODEL, SSM_INNER + SSM_CONV_DIM)),
                  _const_spec((D_MODEL, SSM_HEADS)), _const_spec((SSM_HEADS, D_MODEL)),
                  _const_spec((D_MODEL, 5 * D_MODEL)),
                  _const_spec((SSM_CONV, SSM_CONV_DIM)), _const_spec((1, SSM_CONV_DIM)),
                  _const_spec((1, SSM_HEADS)), _const_spec((SSM_HEADS, 1)),
                  _const_spec((1, SSM_HEADS)), _const_spec((SSM_HEADS, 1)),
                  _const_spec((1, SSM_INNER)), _const_spec((1, SSM_INNER)),
                  _const_spec((SC_CONV, D_MODEL)), _const_spec((D_MODEL, D_MODEL))],
        out_specs=[tok,
                   pl.BlockSpec((None, SSM_INNER, SSM_STATE), lambda b, c: (b, 0, 0)),
                   pl.BlockSpec((None, SSM_CONV - 1, SSM_CONV_DIM), lambda b, c: (b, 0, 0)),
                   pl.BlockSpec((None, SC_CONV - 1, D_MODEL), lambda b, c: (b, 0, 0))],
        out_shape=[jax.ShapeDtypeStruct((t, D_MODEL), F32),
                   jax.ShapeDtypeStruct((nb, SSM_INNER, SSM_STATE), F32),
                   jax.ShapeDtypeStruct((nb, SSM_CONV - 1, SSM_CONV_DIM), F32),
                   jax.ShapeDtypeStruct((nb, SC_CONV - 1, D_MODEL), F32)],
        scratch_shapes=[pltpu.VMEM((SSM_STATE, SSM_INNER), F32),
                        pltpu.VMEM((MIX_TILE + SUBLANES, SSM_CONV_DIM), F32),
                        pltpu.VMEM((MIX_TILE + SUBLANES, D_MODEL), F32)],
        compiler_params=_cparams("arbitrary", "arbitrary"),
        name="prompt_mixer",
    )(x2d, w["norm_mix"], w["w_a"], w["w_dt"], w["w_dt_t"], w["w_b"], w["w_mconv"], w["b_mconv"],
      w["dt_bias"], w["dt_bias_t"], w["a_log"], w["a_log_t"], w["d_skip"], w["norm_ssm"],
      w["w_sconv"], w["w_out"])


def _router_tail(h2, gmoe_ref, wr_ref, br_ref, h2_ref, hn_ref, lg_ref):
    h2_ref[...] = h2
    hn = _rms(h2, gmoe_ref[...])
    hn_ref[...] = hn
    lg_ref[...] = _dot(hn.astype(BF16), wr_ref[...]) + br_ref[...]


def _attn_body(h_ref, gx_ref, wq_ref, k_ref, v_ref, wo_ref, gmoe_ref, wr_ref, br_ref,
               h2_ref, hn_ref, lg_ref):
    h = h_ref[...]
    hn = _rms(h, gx_ref[...]).astype(BF16)
    q = _dot(hn, wq_ref[...]).astype(BF16)
    outs = []
    for hd in range(XA_HEADS):
        sl = slice(hd * XA_HEAD_DIM, (hd + 1) * XA_HEAD_DIM)
        s = _dot_nt(q[:, sl], k_ref[:, sl]) * (XA_HEAD_DIM ** -0.5)
        e = jnp.exp(s - jnp.max(s, axis=-1, keepdims=True))
        p = e / jnp.sum(e, axis=-1, keepdims=True)
        outs.append(_dot(p.astype(BF16), v_ref[:, sl]))
    o = jnp.concatenate(outs, axis=1).astype(BF16)
    h2 = h + _dot(o, wo_ref[...])
    _router_tail(h2, gmoe_ref, wr_ref, br_ref, h2_ref, hn_ref, lg_ref)


def _prompt_attn(h1, kb, vb, nb, w):
    t = h1.shape[0]
    nc = t // nb // MIX_TILE
    tok = pl.BlockSpec((MIX_TILE, D_MODEL), lambda b, c: (b * nc + c, 0))
    kv = pl.BlockSpec((N_MEM, D_MODEL), lambda b, c: (b, 0))
    return pl.pallas_call(
        _attn_body,
        grid=(nb, nc),
        in_specs=[tok, _const_spec((1, D_MODEL)), _const_spec((D_MODEL, D_MODEL)), kv, kv,
                  _const_spec((D_MODEL, D_MODEL)), _const_spec((1, D_MODEL)),
                  _const_spec((D_MODEL, N_EXPERTS)), _const_spec((1, N_EXPERTS))],
        out_specs=[tok, tok, pl.BlockSpec((MIX_TILE, N_EXPERTS), lambda b, c: (b * nc + c, 0))],
        out_shape=[jax.ShapeDtypeStruct((t, D_MODEL), F32),
                   jax.ShapeDtypeStruct((t, D_MODEL), F32),
                   jax.ShapeDtypeStruct((t, N_EXPERTS), F32)],
        compiler_params=_cparams("arbitrary", "arbitrary"),
        name="prompt_attn",
    )(h1, w["norm_xattn"], w["w_xq"], kb, vb, w["w_xo"], w["norm_moe"], w["w_router"], w["b_router"])


def _sproj_body(x_ref, gmix_ref, wa_ref, wdtc_ref, wb_ref, wmc_ref, bmc_ref, dtb_ref, alog_ref,
                wsc_ref, mst_ref, sst_ref,
                z_ref, xs_ref, dtx_ref, dec_ref, bm_ref, cm_ref, yb_ref, sga_ref, mnew_ref, snew_ref):
    x = x_ref[...]
    xn = _rms(x, gmix_ref[...]).astype(BF16)
    pa = _dot(xn, wa_ref[...])
    z_ref[...] = pa[:, :SSM_INNER]
    u = pa[:, SSM_INNER:]
    wm = wmc_ref[...]
    conv = u * wm[SSM_CONV - 1:SSM_CONV, :] + bmc_ref[...]
    for k in range(SSM_CONV - 1):
        conv = conv + mst_ref[k] * wm[k:k + 1, :]
    for k in range(SSM_CONV - 2):
        mnew_ref[k] = mst_ref[k + 1]
    mnew_ref[SSM_CONV - 2] = u
    xbc = _silu(conv)
    xs = xbc[:, :SSM_INNER]
    xs_ref[...] = xs
    bm_ref[...] = xbc[:, SSM_INNER:SSM_INNER + SSM_GROUPS * SSM_STATE]
    cm_ref[...] = xbc[:, SSM_INNER + SSM_GROUPS * SSM_STATE:]
    dt = _softplus(_dot(xn, wdtc_ref[...]) + dtb_ref[...])
    dec_ref[...] = jnp.exp(dt * (-jnp.exp(alog_ref[...])))
    dtx_ref[...] = xs * _expand_heads(dt)
    pb = _dot(xn, wb_ref[...])
    cv = pb[:, D_MODEL:2 * D_MODEL] * pb[:, 2 * D_MODEL:3 * D_MODEL]
    ws = wsc_ref[...]
    uc = cv * ws[SC_CONV - 1:SC_CONV, :]
    for k in range(SC_CONV - 1):
        uc = uc + sst_ref[k] * ws[k:k + 1, :]
    for k in range(SC_CONV - 2):
        snew_ref[k] = sst_ref[k + 1]
    snew_ref[SC_CONV - 2] = cv
    yb_ref[...] = _sigmoid(pb[:, 4 * D_MODEL:5 * D_MODEL]) * (pb[:, 0:D_MODEL] * uc)
    sga_ref[...] = _sigmoid(pb[:, 3 * D_MODEL:4 * D_MODEL])


def _sample_proj(x, mstate_t, sstate_t, w):
    nb = x.shape[0]
    f = lambda *s: jax.ShapeDtypeStruct(s, F32)
    return pl.pallas_call(
        _sproj_body,
        out_shape=[f(nb, SSM_INNER), f(nb, SSM_INNER), f(nb, SSM_INNER), f(nb, SSM_HEADS),
                   f(nb, SSM_GROUPS * SSM_STATE), f(nb, SSM_GROUPS * SSM_STATE),
                   f(nb, D_MODEL), f(nb, D_MODEL),
                   f(SSM_CONV - 1, nb, SSM_CONV_DIM), f(SC_CONV - 1, nb, D_MODEL)],
        compiler_params=pltpu.CompilerParams(vmem_limit_bytes=VMEM_LIMIT),
        name="sample_proj",
    )(x, w["norm_mix"], w["w_a"], w["w_dt"], w["w_b"], w["w_mconv"], w["b_mconv"], w["dt_bias"],
      w["a_log"], w["w_sconv"], mstate_t, sstate_t)


def _sstate_body(dec_ref, s_ref, dtx_ref, bm_ref, cm_ref, snew_ref, y_ref):
    i = pl.program_id(0)
    rows_per_blk = LANES
    for j in range(STATE_BB):
        b = i * STATE_BB + j
        dtx_row = dtx_ref[j:j + 1, :]
        y_parts = []
        for g in range(SSM_GROUPS):
            b_row = bm_ref[j:j + 1, g * SSM_STATE:(g + 1) * SSM_STATE]
            c_row = cm_ref[j:j + 1, g * SSM_STATE:(g + 1) * SSM_STATE].astype(BF16)
            new_blocks = []
            for q in range(GROUP_WIDTH // rows_per_blk):
                r0 = g * GROUP_WIDTH + q * rows_per_blk
                dcol = jnp.broadcast_to(dtx_row[:, r0:r0 + rows_per_blk], (rows_per_blk, LANES)).T
                sub = []
                for hh in range(rows_per_blk // SSM_HEAD_DIM):
                    h = r0 // SSM_HEAD_DIM + hh
                    lo = hh * SSM_HEAD_DIM
                    s_old = s_ref[j, r0 + lo:r0 + lo + SSM_HEAD_DIM, :]
                    sub.append(s_old * dec_ref[b, h] + dcol[lo:lo + SSM_HEAD_DIM, :] * b_row)
                blk = jnp.concatenate(sub, axis=0)
                snew_ref[j, r0:r0 + rows_per_blk, :] = blk
                new_blocks.append(blk.astype(BF16))
            s_g = jnp.concatenate(new_blocks, axis=0)
            y_parts.append(_dot_nt(c_row, s_g))
        y_ref[j:j + 1, :] = jnp.concatenate(y_parts, axis=1)


def _sample_state(dec, state, dtx, bm, cm):
    nb = state.shape[0]
    row = lambda wdt: pl.BlockSpec((STATE_BB, wdt), lambda i, dec: (i, 0))
    st = pl.BlockSpec((STATE_BB, SSM_INNER, SSM_STATE), lambda i, dec: (i, 0, 0))
    return pl.pallas_call(
        _sstate_body,
        grid_spec=pltpu.PrefetchScalarGridSpec(
            num_scalar_prefetch=1, grid=(nb // STATE_BB,),
            in_specs=[st, row(SSM_INNER), row(SSM_GROUPS * SSM_STATE), row(SSM_GROUPS * SSM_STATE)],
            out_specs=[st, row(SSM_INNER)]),
        out_shape=[jax.ShapeDtypeStruct(state.shape, F32), jax.ShapeDtypeStruct((nb, SSM_INNER), F32)],
        compiler_params=_cparams("arbitrary"),
        name="sample_state",
    )(dec, state, dtx, bm, cm)


def _sfin1_body(x_ref, y_ref, xs_ref, z_ref, yb_ref, sga_ref, dskip_ref, gssm_ref, wout_ref,
                gx_ref, wq_ref, h_ref, q_ref):
    y = y_ref[...] + dskip_ref[...] * xs_ref[...]
    y_a = _group_rmsnorm(y * _silu(z_ref[...]), gssm_ref[...])
    merged = sga_ref[...] * y_a + yb_ref[...]
    h = x_ref[...] + _dot(merged.astype(BF16), wout_ref[...])
    h_ref[...] = h
    q_ref[...] = _dot(_rms(h, gx_ref[...]).astype(BF16), wq_ref[...])


def _sample_fin1(x, y, xs, z, yb, sga, w):
    nb = x.shape[0]
    return pl.pallas_call(
        _sfin1_body,
        out_shape=[jax.ShapeDtypeStruct((nb, D_MODEL), F32)] * 2,
        compiler_params=pltpu.CompilerParams(vmem_limit_bytes=VMEM_LIMIT),
        name="sample_fin1",
    )(x, y, xs, z, yb, sga, w["d_skip"], w["norm_ssm"], w["w_out"], w["norm_xattn"], w["w_xq"])


def _sattn_body(q_ref, k_ref, v_ref, o_ref):
    for j in range(ATTN_BB):
        q_row = q_ref[j]
        q4 = jnp.concatenate([q_row[:, h * XA_HEAD_DIM:(h + 1) * XA_HEAD_DIM]
                              for h in range(XA_HEADS)], axis=0)
        s = jnp.sum(k_ref[j] * q4[None], axis=-1, keepdims=True) * (XA_HEAD_DIM ** -0.5)
        e = jnp.exp(s - jnp.max(s, axis=0, keepdims=True))
        p = e / jnp.sum(e, axis=0, keepdims=True)
        o4 = jnp.sum(p * v_ref[j], axis=0)
        o_ref[j] = jnp.concatenate([o4[h:h + 1, :] for h in range(XA_HEADS)], axis=1)


def _sample_attn(q3, k3, v3):
    nb = q3.shape[0]
    qs = pl.BlockSpec((ATTN_BB, 1, D_MODEL), lambda i: (i, 0, 0))
    kv = pl.BlockSpec((ATTN_BB, N_MEM, XA_HEADS, XA_HEAD_DIM), lambda i: (i, 0, 0, 0))
    return pl.pallas_call(
        _sattn_body,
        grid=(nb // ATTN_BB,),
        in_specs=[qs, kv, kv],
        out_specs=qs,
        out_shape=jax.ShapeDtypeStruct((nb, 1, D_MODEL), F32),
        compiler_params=_cparams("arbitrary"),
        name="sample_attn",
    )(q3, k3, v3)


def _sfin2_body(h_ref, o_ref, wo_ref, gmoe_ref, wr_ref, br_ref, h2_ref, hn_ref, lg_ref):
    h2 = h_ref[...] + _dot(o_ref[...].astype(BF16), wo_ref[...])
    _router_tail(h2, gmoe_ref, wr_ref, br_ref, h2_ref, hn_ref, lg_ref)


def _sample_fin2(h1, o, w):
    nb = h1.shape[0]
    return pl.pallas_call(
        _sfin2_body,
        out_shape=[jax.ShapeDtypeStruct((nb, D_MODEL), F32)] * 2
        + [jax.ShapeDtypeStruct((nb, N_EXPERTS), F32)],
        compiler_params=pltpu.CompilerParams(vmem_limit_bytes=VMEM_LIMIT),
        name="sample_fin2",
    )(h1, o, w["w_xo"], w["norm_moe"], w["w_router"], w["b_router"])


def _route_body(lg_ref, g_ref, loc_ref, cnt_ref, off_ref):
    tt = lg_ref.shape[0]
    work = lg_ref[...]
    lane = lax.broadcasted_iota(I32, (tt, N_EXPERTS), 1).astype(F32)
    vals, hots = [], []
    for _ in range(TOP_K):
        m = jnp.max(work, axis=-1, keepdims=True)
        idx = jnp.min(jnp.where(work == m, lane, float(N_EXPERTS)), axis=-1, keepdims=True)
        hot = lane == idx
        vals.append(m)
        hots.append(hot)
        work = jnp.where(hot, -jnp.inf, work)
    exps = [jnp.exp(v - vals[0]) for v in vals]
    tot = exps[0]
    for e in exps[1:]:
        tot = tot + e
    assigned = hots[0]
    for hot in hots[1:]:
        assigned = assigned | hot
    a = assigned.astype(BF16)
    r_i = lax.broadcasted_iota(I32, (tt, tt), 0)
    c_i = lax.broadcasted_iota(I32, (tt, tt), 1)
    rank = _dot((r_i > c_i).astype(BF16), a)
    e_r = lax.broadcasted_iota(I32, (N_EXPERTS, N_EXPERTS), 0)
    e_c = lax.broadcasted_iota(I32, (N_EXPERTS, N_EXPERTS), 1)
    cnt = jnp.sum(a.astype(F32), axis=0, keepdims=True)
    cnt = jnp.floor((cnt + (SUBLANES - 1)) * (1.0 / SUBLANES)) * SUBLANES
    cnt_rows = jnp.broadcast_to(cnt, (SUBLANES, N_EXPERTS)).astype(BF16)
    off = _dot(cnt_rows, (e_r < e_c).astype(BF16))[0:1, :]
    slot = rank + off
    k_lane = lax.broadcasted_iota(I32, (tt, TOP_K), 1)
    g_out = jnp.zeros((tt, TOP_K), F32)
    l_out = jnp.zeros((tt, TOP_K), F32)
    for k in range(TOP_K):
        lk = jnp.sum(jnp.where(hots[k], slot, 0.0), axis=-1, keepdims=True)
        g_out = jnp.where(k_lane == k, exps[k] / tot, g_out)
        l_out = jnp.where(k_lane == k, lk, l_out)
    g_ref[...] = g_out
    loc_ref[...] = l_out.astype(I32)
    cnt_ref[...] = cnt.astype(I32)
    off_ref[...] = off.astype(I32)


def _route(logits, tt):
    t = logits.shape[0]
    nt = t // tt
    tk = pl.BlockSpec((tt, TOP_K), lambda i: (i, 0))
    per_tile = pl.BlockSpec((None, 1, N_EXPERTS), lambda i: (i, 0, 0))
    return pl.pallas_call(
        _route_body,
        grid=(nt,),
        in_specs=[pl.BlockSpec((tt, N_EXPERTS), lambda i: (i, 0))],
        out_specs=[tk, tk, per_tile, per_tile],
        out_shape=[jax.ShapeDtypeStruct((t, TOP_K), F32), jax.ShapeDtypeStruct((t, TOP_K), I32),
                   jax.ShapeDtypeStruct((nt, 1, N_EXPERTS), I32), jax.ShapeDtypeStruct((nt, 1, N_EXPERTS), I32)],
        compiler_params=_cparams("arbitrary"),
        name="moe_route",
    )(logits)


def _sorted_rows(tt):
    return tt * TOP_K + N_EXPERTS * SUBLANES


def _run_copies(tt, cnt_ref, off_ref, base_ref, make_copy, wait):
    def per_expert(e, carry):
        n = cnt_ref[0, 0, e]
        lo = off_ref[0, 0, e]
        go = base_ref[0, 0, e]
        done = 0
        size = tt
        while size >= SUBLANES:
            @pl.when((n & size) != 0)
            def _(size=size, done=done):
                cp = make_copy(pl.multiple_of(lo + done, SUBLANES), pl.multiple_of(go + done, SUBLANES), size)
                if wait:
                    cp.wait()
                else:
                    cp.start()
            done = done + (n & size)
            size //= 2
        return carry

    lax.fori_loop(0, N_EXPERTS, per_expert, 0)


def _dispatch_body(tm, tt, cnt_ref, off_ref, base_ref, zstart_ref, loc_ref, x_ref, o_hbm,
                   zero_ref, srt_ref, sem):
    i = pl.program_id(0)
    r = _sorted_rows(tt)

    def zero_copy(j):
        return pltpu.make_async_copy(zero_ref, o_hbm.at[pl.ds(pl.multiple_of(zstart_ref[j], tm), tm), :], sem)

    @pl.when(i == 0)
    def _():
        zero_ref[...] = jnp.zeros_like(zero_ref)

        def start(j, carry):
            @pl.when(zstart_ref[j] >= 0)
            def _():
                zero_copy(j).start()
            return carry

        def wait(j, carry):
            @pl.when(zstart_ref[j] >= 0)
            def _():
                zero_copy(j).wait()
            return carry

        lax.fori_loop(0, zstart_ref.shape[0], start, 0)
        lax.fori_loop(0, zstart_ref.shape[0], wait, 0)

    loc = loc_ref[...]
    slot_i = lax.broadcasted_iota(I32, (tt, r), 1)
    hit = slot_i == loc[:, 0:1]
    for k in range(1, TOP_K):
        hit = hit | (slot_i == loc[:, k:k + 1])
    srt_ref[...] = lax.dot_general(hit.astype(BF16), x_ref[...].astype(BF16), TN_DIMS,
                                   preferred_element_type=F32)

    def make_copy(lo, go, size):
        return pltpu.make_async_copy(srt_ref.at[pl.ds(lo, size), :], o_hbm.at[pl.ds(go, size), :], sem)

    _run_copies(tt, cnt_ref, off_ref, base_ref, make_copy, wait=False)
    _run_copies(tt, cnt_ref, off_ref, base_ref, make_copy, wait=True)


def _dispatch(cnt3, off3, base3, zero_starts, loc, hn, n_rows, tm, tt):
    t = hn.shape[0]
    smem_tile = pl.BlockSpec((1, 1, N_EXPERTS), lambda i: (i, 0, 0), memory_space=pltpu.SMEM)
    return pl.pallas_call(
        functools.partial(_dispatch_body, tm, tt),
        grid_spec=pltpu.PrefetchScalarGridSpec(
            num_scalar_prefetch=0, grid=(t // tt,),
            in_specs=[smem_tile, smem_tile, smem_tile,
                      pl.BlockSpec(memory_space=pltpu.SMEM),
                      pl.BlockSpec((tt, TOP_K), lambda i: (i, 0)),
                      pl.BlockSpec((tt, D_MODEL), lambda i: (i, 0))],
            out_specs=pl.BlockSpec(memory_space=pl.ANY),
            scratch_shapes=[pltpu.VMEM((tm, D_MODEL), F32), pltpu.VMEM((_sorted_rows(tt), D_MODEL), F32),
                            pltpu.SemaphoreType.DMA]),
        out_shape=jax.ShapeDtypeStruct((n_rows, D_MODEL), F32),
        compiler_params=_cparams("arbitrary"),
        name="moe_dispatch",
    )(cnt3, off3, base3, zero_starts, loc, hn)


def _expert_body(be_ref, nu_ref, x_ref, wgu_ref, bgu_ref, wdn_ref, bdn_ref, y_ref):
    i = pl.program_id(0)

    @pl.when(i < nu_ref[0])
    def _():
        gu = _dot(x_ref[...].astype(BF16), wgu_ref[...]) + bgu_ref[...]
        gate = jnp.minimum(gu[:, :D_FF], SWIGLU_LIMIT)
        up = jnp.clip(gu[:, D_FF:], -SWIGLU_LIMIT, SWIGLU_LIMIT)
        act = (up + 1.0) * (gate * _sigmoid(SWIGLU_ALPHA * gate))
        y_ref[...] = _dot(act.astype(BF16), wdn_ref[...]) + bdn_ref[...]

    @pl.when(i >= nu_ref[0])
    def _():
        y_ref[...] = jnp.zeros_like(y_ref)


def _experts(block_e, n_used, xs, wgu, bgu, wdn, bdn, tm):
    n_rows = xs.shape[0]
    return pl.pallas_call(
        _expert_body,
        grid_spec=pltpu.PrefetchScalarGridSpec(
            num_scalar_prefetch=2, grid=(n_rows // tm,),
            in_specs=[pl.BlockSpec((tm, D_MODEL), lambda i, be, nu: (jnp.minimum(i, nu[0] - 1), 0)),
                      pl.BlockSpec((None, D_MODEL, 2 * D_FF), lambda i, be, nu: (be[i], 0, 0)),
                      pl.BlockSpec((None, 1, 2 * D_FF), lambda i, be, nu: (be[i], 0, 0)),
                      pl.BlockSpec((None, D_FF, D_MODEL), lambda i, be, nu: (be[i], 0, 0)),
                      pl.BlockSpec((None, 1, D_MODEL), lambda i, be, nu: (be[i], 0, 0))],
            out_specs=pl.BlockSpec((tm, D_MODEL), lambda i, be, nu: (i, 0))),
        out_shape=jax.ShapeDtypeStruct((n_rows, D_MODEL), F32),
        compiler_params=_cparams("arbitrary"),
        name="moe_experts",
    )(block_e, n_used, xs, wgu, bgu, wdn, bdn)


def _combine_body(tt, cnt_ref, off_ref, base_ref, loc_ref, g_ref, h_ref, gfin_ref, ys_hbm, y_ref,
                  buf_ref, sem):
    r = _sorted_rows(tt)

    def make_copy(lo, go, size):
        return pltpu.make_async_copy(ys_hbm.at[pl.ds(go, size), :], buf_ref.at[pl.ds(lo, size), :], sem)

    @pl.when(pl.program_id(0) == 0)
    def _():
        buf_ref[...] = jnp.zeros_like(buf_ref)

    _run_copies(tt, cnt_ref, off_ref, base_ref, make_copy, wait=False)
    _run_copies(tt, cnt_ref, off_ref, base_ref, make_copy, wait=True)
    loc = loc_ref[...]
    gates = g_ref[...]
    slot_i = lax.broadcasted_iota(I32, (tt, r), 1)
    gmat = jnp.where(slot_i == loc[:, 0:1], gates[:, 0:1], 0.0)
    for k in range(1, TOP_K):
        gmat = gmat + jnp.where(slot_i == loc[:, k:k + 1], gates[:, k:k + 1], 0.0)
    out = h_ref[...] + _dot(gmat.astype(BF16), buf_ref[...].astype(BF16))
    y_ref[...] = _rms(out, gfin_ref[...])


def _combine(cnt3, off3, base3, loc, gates, h2, norm_final, ys, tt):
    t = h2.shape[0]
    tok = pl.BlockSpec((tt, D_MODEL), lambda i: (i, 0))
    tk = pl.BlockSpec((tt, TOP_K), lambda i: (i, 0))
    smem_tile = pl.BlockSpec((1, 1, N_EXPERTS), lambda i: (i, 0, 0), memory_space=pltpu.SMEM)
    return pl.pallas_call(
        functools.partial(_combine_body, tt),
        grid_spec=pltpu.PrefetchScalarGridSpec(
            num_scalar_prefetch=0, grid=(t // tt,),
            in_specs=[smem_tile, smem_tile, smem_tile, tk, tk, tok,
                      pl.BlockSpec((1, D_MODEL), lambda i: (0, 0)),
                      pl.BlockSpec(memory_space=pl.ANY)],
            out_specs=tok,
            scratch_shapes=[pltpu.VMEM((_sorted_rows(tt), D_MODEL), F32), pltpu.SemaphoreType.DMA]),
        out_shape=jax.ShapeDtypeStruct((t, D_MODEL), F32),
        compiler_params=_cparams("arbitrary"),
        name="moe_combine",
    )(cnt3, off3, base3, loc, gates, h2, norm_final, ys)


def _moe_and_final_norm(hn, logits, h2, w, tt, tm):
    t = hn.shape[0]
    gates, loc, cnt3, off3 = _route(logits, tt)
    cnt = cnt3[:, 0, :]
    counts = jnp.sum(cnt, axis=0)
    padded = (counts + tm - 1) // tm * tm
    pad_end = jnp.cumsum(padded)
    start = pad_end - padded
    base3 = (start[None, :] + jnp.cumsum(cnt, axis=0) - cnt)[:, None, :].astype(I32)
    n_blocks = (t * TOP_K + (t // tt) * N_EXPERTS * (SUBLANES - 1) + N_EXPERTS * (tm - 1)) // tm
    n_rows = n_blocks * tm
    block_start = jnp.arange(n_blocks, dtype=I32) * tm
    block_e = jnp.minimum(jnp.sum(block_start[:, None] >= pad_end[None, :], axis=-1), N_EXPERTS - 1).astype(I32)
    n_used = (pad_end[-1:] // tm).astype(I32)
    zero_starts = jnp.concatenate([jnp.where(padded > 0, pad_end - tm, -1),
                                   jnp.where(block_start >= pad_end[-1], block_start, -1)]).astype(I32)
    xs = _dispatch(cnt3, off3, base3, zero_starts, loc, hn, n_rows, tm, tt)
    ys = _experts(block_e, n_used, xs, w["w_gate_up"], w["b_gate_up"], w["w_down"], w["b_down"], tm)
    return _combine(cnt3, off3, base3, loc, gates, h2, w["norm_final"], ys, tt)


def kernel(x_prompt, x_sample, mem_prompt, state_ssm, state_mamba_conv, state_short_conv, cache_mem_k, cache_mem_v, norm_mix, w_in, w_mconv, b_mconv, dt_bias, a_log, d_skip, norm_ssm, w_sconv, w_out, norm_xattn, norm_mem, w_xq, w_xk, w_xv, w_xo, norm_moe, w_router, b_router, w_gate_up, b_gate_up, w_down, b_down, norm_final):
    nbp, seq, _ = x_prompt.shape
    nbs = x_sample.shape[0]
    dt_lo = SSM_INNER + SSM_CONV_DIM
    w_in0 = w_in[0]
    w_dt = w_in0[:, dt_lo:dt_lo + SSM_HEADS]
    w = {
        "norm_mix": norm_mix, "norm_ssm": norm_ssm, "norm_xattn": norm_xattn, "norm_moe": norm_moe,
        "norm_final": norm_final.reshape(1, D_MODEL),
        "w_a": w_in0[:, :dt_lo].astype(BF16),
        "w_dt": w_dt.astype(BF16), "w_dt_t": w_dt.T.astype(BF16),
        "w_b": w_in0[:, dt_lo + SSM_HEADS:].astype(BF16),
        "w_mconv": w_mconv[0], "b_mconv": b_mconv,
        "dt_bias": dt_bias, "dt_bias_t": dt_bias.reshape(SSM_HEADS, 1),
        "a_log": a_log, "a_log_t": a_log.reshape(SSM_HEADS, 1),
        "d_skip": jnp.repeat(d_skip, SSM_HEAD_DIM, axis=1),
        "w_sconv": w_sconv[0], "w_out": w_out[0].astype(BF16),
        "w_xq": w_xq[0].astype(BF16), "w_xo": w_xo[0].astype(BF16),
        "w_router": w_router[0].astype(BF16), "b_router": b_router,
        "w_gate_up": w_gate_up[0].astype(BF16), "b_gate_up": b_gate_up[0].reshape(N_EXPERTS, 1, 2 * D_FF),
        "w_down": w_down[0].astype(BF16), "b_down": b_down[0].reshape(N_EXPERTS, 1, D_MODEL),
    }

    k_p, v_p, kb, vb = _mem_kv(mem_prompt.reshape(nbp * N_MEM, D_MODEL), norm_mem,
                               w_xk[0].astype(BF16), w_xv[0].astype(BF16))
    h1, ssm_p, mconv_p, sconv_p = _prompt_mixer(x_prompt.reshape(nbp * seq, D_MODEL), nbp, w)
    h2, hn, logits = _prompt_attn(h1, kb, vb, nbp, w)
    y_prompt = _moe_and_final_norm(hn, logits, h2, w, MIX_TILE, MOE_ROW_TILE)

    xs2 = x_sample.reshape(nbs, D_MODEL)
    mstate_t = jnp.transpose(state_mamba_conv[0], (1, 0, 2))
    sstate_t = jnp.transpose(state_short_conv[0], (1, 0, 2))
    z, xs_, dtx, dec, bm, cm, yb, sga, mnew_t, snew_t = _sample_proj(xs2, mstate_t, sstate_t, w)
    ssm_s, y_s = _sample_state(dec, state_ssm[0].reshape(nbs, SSM_INNER, SSM_STATE), dtx, bm, cm)
    h1s, q_s = _sample_fin1(xs2, y_s, xs_, z, yb, sga, w)
    o_s = _sample_attn(q_s.reshape(nbs, 1, D_MODEL),
                       cache_mem_k[0], cache_mem_v[0])
    h2s, hns, logits_s = _sample_fin2(h1s, o_s.reshape(nbs, D_MODEL), w)
    y_sample = _moe_and_final_norm(hns, logits_s, h2s, w, nbs, LANES)

    return (y_prompt.reshape(nbp, seq, D_MODEL),
            y_sample.reshape(nbs, 1, D_MODEL),
            ssm_p.reshape(1, nbp, SSM_HEADS, SSM_HEAD_DIM, SSM_STATE),
            mconv_p[None], sconv_p[None],
            k_p.reshape(1, nbp, N_MEM, XA_HEADS, XA_HEAD_DIM),
            v_p.reshape(1, nbp, N_MEM, XA_HEADS, XA_HEAD_DIM),
            ssm_s.reshape(1, nbs, SSM_HEADS, SSM_HEAD_DIM, SSM_STATE),
            jnp.transpose(mnew_t, (1, 0, 2))[None],
            jnp.transpose(snew_t, (1, 0, 2))[None])
```

```python
import functools

import jax
import jax.numpy as jnp
from jax import lax
from jax.experimental import pallas as pl
from jax.experimental.pallas import tpu as pltpu

F32 = jnp.float32
BF16 = jnp.bfloat16
I32 = jnp.int32

D_MODEL = 1024
N_MEM = 256
SSM_HEADS = 16
SSM_HEAD_DIM = 64
SSM_INNER = SSM_HEADS * SSM_HEAD_DIM
SSM_STATE = 128
SSM_GROUPS = 4
HEADS_PER_GROUP = SSM_HEADS // SSM_GROUPS
GROUP_WIDTH = SSM_INNER // SSM_GROUPS
SSM_CONV = 4
SSM_CONV_DIM = SSM_INNER + 2 * SSM_GROUPS * SSM_STATE
SC_CONV = 3
XA_HEADS = 4
XA_HEAD_DIM = D_MODEL // XA_HEADS
N_EXPERTS = 32
TOP_K = 4
D_FF = D_MODEL
SWIGLU_LIMIT = 7.0
SWIGLU_ALPHA = 1.702
EPS = 1e-6

LANES = 128
SUBLANES = 8
VMEM_LIMIT = 56 * 1024 * 1024

MIX_TILE = 256
MOE_ROW_TILE = 512
STATE_BB = 8
ATTN_BB = 4

NT_DIMS = (((1,), (1,)), ((), ()))
TN_DIMS = (((0,), (0,)), ((), ()))


def _cparams(*sem):
    return pltpu.CompilerParams(dimension_semantics=sem, vmem_limit_bytes=VMEM_LIMIT)


def _const_spec(shape):
    nd = len(shape)
    return pl.BlockSpec(shape, lambda *_: (0,) * nd, pipeline_mode=pl.Buffered(1))


def _sigmoid(x):
    return 1.0 / (1.0 + jnp.exp(-x))


def _silu(x):
    return x * _sigmoid(x)


def _softplus(x):
    return jnp.maximum(x, 0.0) + jnp.log(1.0 + jnp.exp(-jnp.abs(x)))


def _rms(x, g):
    ms = jnp.mean(x * x, axis=-1, keepdims=True)
    return x * lax.rsqrt(ms + EPS) * g


def _dot(a, b):
    return jnp.dot(a, b, preferred_element_type=F32)


def _dot_nt(a, b):
    return lax.dot_general(a, b, NT_DIMS, preferred_element_type=F32)


def _expand_heads(v):
    rows = v.shape[0]
    lane = lax.broadcasted_iota(I32, (rows, LANES), 1)
    pieces = []
    for j in range(SSM_HEADS // 2):
        a = jnp.broadcast_to(v[:, 2 * j:2 * j + 1], (rows, LANES))
        b = jnp.broadcast_to(v[:, 2 * j + 1:2 * j + 2], (rows, LANES))
        pieces.append(jnp.where(lane < SSM_HEAD_DIM, a, b))
    return jnp.concatenate(pieces, axis=1)


def _group_rmsnorm(u, g):
    outs = []
    for k in range(SSM_GROUPS):
        ug = u[:, k * GROUP_WIDTH:(k + 1) * GROUP_WIDTH]
        ms = jnp.mean(ug * ug, axis=-1, keepdims=True)
        outs.append(ug * lax.rsqrt(ms + EPS))
    return jnp.concatenate(outs, axis=1) * g


def _memkv_body(mem_ref, g_ref, wk_ref, wv_ref, k_ref, v_ref, kb_ref, vb_ref):
    mn = _rms(mem_ref[...], g_ref[...]).astype(BF16)
    k = _dot(mn, wk_ref[...])
    v = _dot(mn, wv_ref[...])
    k_ref[...] = k
    v_ref[...] = v
    kb_ref[...] = k.astype(BF16)
    vb_ref[...] = v.astype(BF16)


def _mem_kv(mem2d, norm_mem, wk, wv):
    rows = mem2d.shape[0]
    nb = rows // N_MEM
    blk = pl.BlockSpec((N_MEM, D_MODEL), lambda b: (b, 0))
    return pl.pallas_call(
        _memkv_body,
        grid=(nb,),
        in_specs=[blk, _const_spec((1, D_MODEL)), _const_spec((D_MODEL, D_MODEL)),
                  _const_spec((D_MODEL, D_MODEL))],
        out_specs=[blk, blk, blk, blk],
        out_shape=[jax.ShapeDtypeStruct((rows, D_MODEL), F32)] * 2
        + [jax.ShapeDtypeStruct((rows, D_MODEL), BF16)] * 2,
        compiler_params=_cparams("arbitrary"),
        name="mem_kv",
    )(mem2d, norm_mem, wk, wv)


def _mix_body(x_ref, gmix_ref, wa_ref, wdtc_ref, wdtr_ref, wb_ref, wmc_ref, bmc_ref,
              dtb_ref, dtbt_ref, alog_ref, alogt_ref, dskip_ref, gssm_ref, wsc_ref, wout_ref,
              h_ref, ssm_ref, mbuf_ref, sbuf_ref,
              st_ref, cbuf_ref, scbuf_ref):
    tq = MIX_TILE
    c = pl.program_id(1)

    @pl.when(c == 0)
    def _():
        st_ref[...] = jnp.zeros_like(st_ref)
        cbuf_ref[0:SUBLANES, :] = jnp.zeros((SUBLANES, SSM_CONV_DIM), F32)
        scbuf_ref[0:SUBLANES, :] = jnp.zeros((SUBLANES, D_MODEL), F32)

    x = x_ref[...]
    xn = _rms(x, gmix_ref[...]).astype(BF16)

    pa = _dot(xn, wa_ref[...])
    z = pa[:, :SSM_INNER]
    u = pa[:, SSM_INNER:]
    cbuf_ref[SUBLANES:SUBLANES + tq, :] = u
    wm = wmc_ref[...]
    conv = u * wm[SSM_CONV - 1:SSM_CONV, :] + bmc_ref[...]
    for k in range(SSM_CONV - 1):
        off = SUBLANES - (SSM_CONV - 1) + k
        conv = conv + cbuf_ref[off:off + tq, :] * wm[k:k + 1, :]
    tail = cbuf_ref[tq + SUBLANES - (SSM_CONV - 1):tq + SUBLANES, :]
    mbuf_ref[...] = tail
    cbuf_ref[SUBLANES - (SSM_CONV - 1):SUBLANES, :] = tail
    xbc = _silu(conv)
    xs = xbc[:, :SSM_INNER]
    bm = xbc[:, SSM_INNER:SSM_INNER + SSM_GROUPS * SSM_STATE]
    cm = xbc[:, SSM_INNER + SSM_GROUPS * SSM_STATE:]

    dt = _softplus(_dot(xn, wdtc_ref[...]) + dtb_ref[...])
    dtt = _softplus(_dot_nt(wdtr_ref[...], xn) + dtbt_ref[...])
    a_row = -jnp.exp(alog_ref[...])
    a_col = -jnp.exp(alogt_ref[...])
    row_i = lax.broadcasted_iota(I32, (tq, tq), 0)
    col_i = lax.broadcasted_iota(I32, (tq, tq), 1)
    causal = row_i >= col_i
    tril = causal.astype(F32)
    triu = (row_i <= col_i).astype(F32)
    a_cum = jnp.dot(tril, dt * a_row, precision=lax.Precision.HIGHEST,
                    preferred_element_type=F32)
    a_cumt = jnp.dot(dtt * a_col, triu, precision=lax.Precision.HIGHEST,
                     preferred_element_type=F32)
    a_last = a_cum[tq - 1:tq, :]

    xdt = xs * _expand_heads(dt)
    in_decay = _expand_heads(jnp.exp(a_cum))
    to_end = _expand_heads(jnp.exp(a_last - a_cum))
    chunk_decay = _expand_heads(jnp.exp(a_last))
    xdt_b = xdt.astype(BF16)
    xend_b = (xdt * to_end).astype(BF16)
    lane = lax.broadcasted_iota(I32, (tq, LANES), 1)

    y_groups = []
    for g in range(SSM_GROUPS):
        cg = cm[:, g * SSM_STATE:(g + 1) * SSM_STATE].astype(BF16)
        bg_f = bm[:, g * SSM_STATE:(g + 1) * SSM_STATE]
        bg = bg_f.astype(BF16)
        scores = _dot_nt(cg, bg)
        gs = slice(g * GROUP_WIDTH, (g + 1) * GROUP_WIDTH)
        st_g = st_ref[:, gs]
        y_off = _dot(cg, st_g.astype(BF16)) * in_decay[:, gs]
        pair_out = []
        for pr in range(HEADS_PER_GROUP // 2):
            h0 = g * HEADS_PER_GROUP + 2 * pr
            xp = xdt_b[:, h0 * SSM_HEAD_DIM:(h0 + 2) * SSM_HEAD_DIM]
            ys = []
            for h in (h0, h0 + 1):
                seg = a_cum[:, h:h + 1] - a_cumt[h:h + 1, :]
                decay = jnp.where(causal, jnp.exp(jnp.minimum(seg, 0.0)), 0.0)
                ys.append(_dot((scores * decay).astype(BF16), xp))
            pair_out.append(jnp.where(lane < SSM_HEAD_DIM, ys[0], ys[1]))
        y_groups.append(jnp.concatenate(pair_out, axis=1) + y_off)
        st_ref[:, gs] = st_g * chunk_decay[:, gs] + _dot(bg_f.T.astype(BF16), xend_b[:, gs])
    y = jnp.concatenate(y_groups, axis=1) + dskip_ref[...] * xs
    y_a = _group_rmsnorm(y * _silu(z), gssm_ref[...])

    @pl.when(c == pl.num_programs(1) - 1)
    def _():
        ssm_ref[...] = st_ref[...].T

    pb = _dot(xn, wb_ref[...])
    sc_b = pb[:, 0:D_MODEL]
    cv = pb[:, D_MODEL:2 * D_MODEL] * pb[:, 2 * D_MODEL:3 * D_MODEL]
    g_a = pb[:, 3 * D_MODEL:4 * D_MODEL]
    g_b = pb[:, 4 * D_MODEL:5 * D_MODEL]
    scbuf_ref[SUBLANES:SUBLANES + tq, :] = cv
    ws = wsc_ref[...]
    uc = cv * ws[SC_CONV - 1:SC_CONV, :]
    for k in range(SC_CONV - 1):
        off = SUBLANES - (SC_CONV - 1) + k
        uc = uc + scbuf_ref[off:off + tq, :] * ws[k:k + 1, :]
    stail = scbuf_ref[tq + SUBLANES - (SC_CONV - 1):tq + SUBLANES, :]
    sbuf_ref[...] = stail
    scbuf_ref[SUBLANES - (SC_CONV - 1):SUBLANES, :] = stail
    merged = _sigmoid(g_a) * y_a + _sigmoid(g_b) * (sc_b * uc)
    h_ref[...] = x + _dot(merged.astype(BF16), wout_ref[...])


def _prompt_mixer(x2d, nb, w):
    t = x2d.shape[0]
    nc = t // nb // MIX_TILE
    tok = pl.BlockSpec((MIX_TILE, D_MODEL), lambda b, c: (b * nc + c, 0))
    return pl.pallas_call(
        _mix_body,
        grid=(nb, nc),
        in_specs=[tok, _const_spec((1, D_MODEL)),
                  _const_spec((D_MODEL, SSM_INNER + SSM_CONV_DIM)),
                  _const_spec((D_MODEL, SSM_HEADS)), _const_spec((SSM_HEADS, D_MODEL)),
                  _const_spec((D_MODEL, 5 * D_MODEL)),
                  _const_spec((SSM_CONV, SSM_CONV_DIM)), _const_spec((1, SSM_CONV_DIM)),
                  _const_spec((1, SSM_HEADS)), _const_spec((SSM_HEADS, 1)),
                  _const_spec((1, SSM_HEADS)), _const_spec((SSM_HEADS, 1)),
                  _const_spec((1, SSM_INNER)), _const_spec((1, SSM_INNER)),
                  _const_spec((SC_CONV, D_MODEL)), _const_spec((D_MODEL, D_MODEL))],
        out_specs=[tok,
                   pl.BlockSpec((None, SSM_INNER, SSM_STATE), lambda b, c: (b, 0, 0)),
                   pl.BlockSpec((None, SSM_CONV - 1, SSM_CONV_DIM), lambda b, c: (b, 0, 0)),
                   pl.BlockSpec((None, SC_CONV - 1, D_MODEL), lambda b, c: (b, 0, 0))],
        out_shape=[jax.ShapeDtypeStruct((t, D_MODEL), F32),
                   jax.ShapeDtypeStruct((nb, SSM_INNER, SSM_STATE), F32),
                   jax.ShapeDtypeStruct((nb, SSM_CONV - 1, SSM_CONV_DIM), F32),
                   jax.ShapeDtypeStruct((nb, SC_CONV - 1, D_MODEL), F32)],
        scratch_shapes=[pltpu.VMEM((SSM_STATE, SSM_INNER), F32),
                        pltpu.VMEM((MIX_TILE + SUBLANES, SSM_CONV_DIM), F32),
                        pltpu.VMEM((MIX_TILE + SUBLANES, D_MODEL), F32)],
        compiler_params=_cparams("arbitrary", "arbitrary"),
        name="prompt_mixer",
    )(x2d, w["norm_mix"], w["w_a"], w["w_dt"], w["w_dt_t"], w["w_b"], w["w_mconv"], w["b_mconv"],
      w["dt_bias"], w["dt_bias_t"], w["a_log"], w["a_log_t"], w["d_skip"], w["norm_ssm"],
      w["w_sconv"], w["w_out"])


def _router_tail(h2, gmoe_ref, wr_ref, br_ref, h2_ref, hn_ref, lg_ref):
    h2_ref[...] = h2
    hn = _rms(h2, gmoe_ref[...])
    hn_ref[...] = hn
    lg_ref[...] = _dot(hn.astype(BF16), wr_ref[...]) + br_ref[...]


def _attn_body(h_ref, gx_ref, wq_ref, k_ref, v_ref, wo_ref, gmoe_ref, wr_ref, br_ref,
               h2_ref, hn_ref, lg_ref):
    h = h_ref[...]
    hn = _rms(h, gx_ref[...]).astype(BF16)
    q = _dot(hn, wq_ref[...]).astype(BF16)
    outs = []
    for hd in range(XA_HEADS):
        sl = slice(hd * XA_HEAD_DIM, (hd + 1) * XA_HEAD_DIM)
        s = _dot_nt(q[:, sl], k_ref[:, sl]) * (XA_HEAD_DIM ** -0.5)
        e = jnp.exp(s - jnp.max(s, axis=-1, keepdims=True))
        p = e / jnp.sum(e, axis=-1, keepdims=True)
        outs.append(_dot(p.astype(BF16), v_ref[:, sl]))
    o = jnp.concatenate(outs, axis=1).astype(BF16)
    h2 = h + _dot(o, wo_ref[...])
    _router_tail(h2, gmoe_ref, wr_ref, br_ref, h2_ref, hn_ref, lg_ref)


def _prompt_attn(h1, kb, vb, nb, w):
    t = h1.shape[0]
    nc = t // nb // MIX_TILE
    tok = pl.BlockSpec((MIX_TILE, D_MODEL), lambda b, c: (b * nc + c, 0))
    kv = pl.BlockSpec((N_MEM, D_MODEL), lambda b, c: (b, 0))
    return pl.pallas_call(
        _attn_body,
        grid=(nb, nc),
        in_specs=[tok, _const_spec((1, D_MODEL)), _const_spec((D_MODEL, D_MODEL)), kv, kv,
                  _const_spec((D_MODEL, D_MODEL)), _const_spec((1, D_MODEL)),
                  _const_spec((D_MODEL, N_EXPERTS)), _const_spec((1, N_EXPERTS))],
        out_specs=[tok, tok, pl.BlockSpec((MIX_TILE, N_EXPERTS), lambda b, c: (b * nc + c, 0))],
        out_shape=[jax.ShapeDtypeStruct((t, D_MODEL), F32),
                   jax.ShapeDtypeStruct((t, D_MODEL), F32),
                   jax.ShapeDtypeStruct((t, N_EXPERTS), F32)],
        compiler_params=_cparams("arbitrary", "arbitrary"),
        name="prompt_attn",
    )(h1, w["norm_xattn"], w["w_xq"], kb, vb, w["w_xo"], w["norm_moe"], w["w_router"], w["b_router"])


def _sproj_body(x_ref, gmix_ref, wa_ref, wdtc_ref, wb_ref, wmc_ref, bmc_ref, dtb_ref, alog_ref,
                wsc_ref, mst_ref, sst_ref,
                z_ref, xs_ref, dtx_ref, dec_ref, bm_ref, cm_ref, yb_ref, sga_ref, mnew_ref, snew_ref):
    x = x_ref[...]
    xn = _rms(x, gmix_ref[...]).astype(BF16)
    pa = _dot(xn, wa_ref[...])
    z_ref[...] = pa[:, :SSM_INNER]
    u = pa[:, SSM_INNER:]
    wm = wmc_ref[...]
    conv = u * wm[SSM_CONV - 1:SSM_CONV, :] + bmc_ref[...]
    for k in range(SSM_CONV - 1):
        conv = conv + mst_ref[k] * wm[k:k + 1, :]
    for k in range(SSM_CONV - 2):
        mnew_ref[k] = mst_ref[k + 1]
    mnew_ref[SSM_CONV - 2] = u
    xbc = _silu(conv)
    xs = xbc[:, :SSM_INNER]
    xs_ref[...] = xs
    bm_ref[...] = xbc[:, SSM_INNER:SSM_INNER + SSM_GROUPS * SSM_STATE]
    cm_ref[...] = xbc[:, SSM_INNER + SSM_GROUPS * SSM_STATE:]
    dt = _softplus(_dot(xn, wdtc_ref[...]) + dtb_ref[...])
    dec_ref[...] = jnp.exp(dt * (-jnp.exp(alog_ref[...])))
    dtx_ref[...] = xs * _expand_heads(dt)
    pb = _dot(xn, wb_ref[...])
    cv = pb[:, D_MODEL:2 * D_MODEL] * pb[:, 2 * D_MODEL:3 * D_MODEL]
    ws = wsc_ref[...]
    uc = cv * ws[SC_CONV - 1:SC_CONV, :]
    for k in range(SC_CONV - 1):
        uc = uc + sst_ref[k] * ws[k:k + 1, :]
    for k in range(SC_CONV - 2):
        snew_ref[k] = sst_ref[k + 1]
    snew_ref[SC_CONV - 2] = cv
    yb_ref[...] = _sigmoid(pb[:, 4 * D_MODEL:5 * D_MODEL]) * (pb[:, 0:D_MODEL] * uc)
    sga_ref[...] = _sigmoid(pb[:, 3 * D_MODEL:4 * D_MODEL])


def _sample_proj(x, mstate_t, sstate_t, w):
    nb = x.shape[0]
    f = lambda *s: jax.ShapeDtypeStruct(s, F32)
    return pl.pallas_call(
        _sproj_body,
        out_shape=[f(nb, SSM_INNER), f(nb, SSM_INNER), f(nb, SSM_INNER), f(nb, SSM_HEADS),
                   f(nb, SSM_GROUPS * SSM_STATE), f(nb, SSM_GROUPS * SSM_STATE),
                   f(nb, D_MODEL), f(nb, D_MODEL),
                   f(SSM_CONV - 1, nb, SSM_CONV_DIM), f(SC_CONV - 1, nb, D_MODEL)],
        compiler_params=pltpu.CompilerParams(vmem_limit_bytes=VMEM_LIMIT),
        name="sample_proj",
    )(x, w["norm_mix"], w["w_a"], w["w_dt"], w["w_b"], w["w_mconv"], w["b_mconv"], w["dt_bias"],
      w["a_log"], w["w_sconv"], mstate_t, sstate_t)


def _sstate_body(dec_ref, s_ref, dtx_ref, bm_ref, cm_ref, snew_ref, y_ref):
    i = pl.program_id(0)
    rows_per_blk = LANES
    for j in range(STATE_BB):
        b = i * STATE_BB + j
        dtx_row = dtx_ref[j:j + 1, :]
        y_parts = []
        for g in range(SSM_GROUPS):
            b_row = bm_ref[j:j + 1, g * SSM_STATE:(g + 1) * SSM_STATE]
            c_row = cm_ref[j:j + 1, g * SSM_STATE:(g + 1) * SSM_STATE].astype(BF16)
            new_blocks = []
            for q in range(GROUP_WIDTH // rows_per_blk):
                r0 = g * GROUP_WIDTH + q * rows_per_blk
                dcol = jnp.broadcast_to(dtx_row[:, r0:r0 + rows_per_blk], (rows_per_blk, LANES)).T
                sub = []
                for hh in range(rows_per_blk // SSM_HEAD_DIM):
                    h = r0 // SSM_HEAD_DIM + hh
                    lo = hh * SSM_HEAD_DIM
                    s_old = s_ref[j, r0 + lo:r0 + lo + SSM_HEAD_DIM, :]
                    sub.append(s_old * dec_ref[b, h] + dcol[lo:lo + SSM_HEAD_DIM, :] * b_row)
                blk = jnp.concatenate(sub, axis=0)
                snew_ref[j, r0:r0 + rows_per_blk, :] = blk
                new_blocks.append(blk.astype(BF16))
            s_g = jnp.concatenate(new_blocks, axis=0)
            y_parts.append(_dot_nt(c_row, s_g))
        y_ref[j:j + 1, :] = jnp.concatenate(y_parts, axis=1)


def _sample_state(dec, state, dtx, bm, cm):
    nb = state.shape[0]
    row = lambda wdt: pl.BlockSpec((STATE_BB, wdt), lambda i, dec: (i, 0))
    st = pl.BlockSpec((STATE_BB, SSM_INNER, SSM_STATE), lambda i, dec: (i, 0, 0))
    return pl.pallas_call(
        _sstate_body,
        grid_spec=pltpu.PrefetchScalarGridSpec(
            num_scalar_prefetch=1, grid=(nb // STATE_BB,),
            in_specs=[st, row(SSM_INNER), row(SSM_GROUPS * SSM_STATE), row(SSM_GROUPS * SSM_STATE)],
            out_specs=[st, row(SSM_INNER)]),
        out_shape=[jax.ShapeDtypeStruct(state.shape, F32), jax.ShapeDtypeStruct((nb, SSM_INNER), F32)],
        compiler_params=_cparams("arbitrary"),
        name="sample_state",
    )(dec, state, dtx, bm, cm)


def _sfin1_body(x_ref, y_ref, xs_ref, z_ref, yb_ref, sga_ref, dskip_ref, gssm_ref, wout_ref,
                gx_ref, wq_ref, h_ref, q_ref):
    y = y_ref[...] + dskip_ref[...] * xs_ref[...]
    y_a = _group_rmsnorm(y * _silu(z_ref[...]), gssm_ref[...])
    merged = sga_ref[...] * y_a + yb_ref[...]
    h = x_ref[...] + _dot(merged.astype(BF16), wout_ref[...])
    h_ref[...] = h
    q_ref[...] = _dot(_rms(h, gx_ref[...]).astype(BF16), wq_ref[...])


def _sample_fin1(x, y, xs, z, yb, sga, w):
    nb = x.shape[0]
    return pl.pallas_call(
        _sfin1_body,
        out_shape=[jax.ShapeDtypeStruct((nb, D_MODEL), F32)] * 2,
        compiler_params=pltpu.CompilerParams(vmem_limit_bytes=VMEM_LIMIT),
        name="sample_fin1",
    )(x, y, xs, z, yb, sga, w["d_skip"], w["norm_ssm"], w["w_out"], w["norm_xattn"], w["w_xq"])


def _sattn_body(q_ref, k_ref, v_ref, o_ref):
    for j in range(ATTN_BB):
        q_row = q_ref[j]
        q4 = jnp.concatenate([q_row[:, h * XA_HEAD_DIM:(h + 1) * XA_HEAD_DIM]
                              for h in range(XA_HEADS)], axis=0)
        s = jnp.sum(k_ref[j] * q4[None], axis=-1, keepdims=True) * (XA_HEAD_DIM ** -0.5)
        e = jnp.exp(s - jnp.max(s, axis=0, keepdims=True))
        p = e / jnp.sum(e, axis=0, keepdims=True)
        o4 = jnp.sum(p * v_ref[j], axis=0)
        o_ref[j] = jnp.concatenate([o4[h:h + 1, :] for h in range(XA_HEADS)], axis=1)


def _sample_attn(q3, k3, v3):
    nb = q3.shape[0]
    qs = pl.BlockSpec((ATTN_BB, 1, D_MODEL), lambda i: (i, 0, 0))
    kv = pl.BlockSpec((ATTN_BB, N_MEM, XA_HEADS, XA_HEAD_DIM), lambda i: (i, 0, 0, 0))
    return pl.pallas_call(
        _sattn_body,
        grid=(nb // ATTN_BB,),
        in_specs=[qs, kv, kv],
        out_specs=qs,
        out_shape=jax.ShapeDtypeStruct((nb, 1, D_MODEL), F32),
        compiler_params=_cparams("arbitrary"),
        name="sample_attn",
    )(q3, k3, v3)


def _sfin2_body(h_ref, o_ref, wo_ref, gmoe_ref, wr_ref, br_ref, h2_ref, hn_ref, lg_ref):
    h2 = h_ref[...] + _dot(o_ref[...].astype(BF16), wo_ref[...])
    _router_tail(h2, gmoe_ref, wr_ref, br_ref, h2_ref, hn_ref, lg_ref)


def _sample_fin2(h1, o, w):
    nb = h1.shape[0]
    return pl.pallas_call(
        _sfin2_body,
        out_shape=[jax.ShapeDtypeStruct((nb, D_MODEL), F32)] * 2
        + [jax.ShapeDtypeStruct((nb, N_EXPERTS), F32)],
        compiler_params=pltpu.CompilerParams(vmem_limit_bytes=VMEM_LIMIT),
        name="sample_fin2",
    )(h1, o, w["w_xo"], w["norm_moe"], w["w_router"], w["b_router"])


def _route_body(lg_ref, g_ref, loc_ref, cnt_ref, off_ref):
    tt = lg_ref.shape[0]
    work = lg_ref[...]
    lane = lax.broadcasted_iota(I32, (tt, N_EXPERTS), 1).astype(F32)
    vals, hots = [], []
    for _ in range(TOP_K):
        m = jnp.max(work, axis=-1, keepdims=True)
        idx = jnp.min(jnp.where(work == m, lane, float(N_EXPERTS)), axis=-1, keepdims=True)
        hot = lane == idx
        vals.append(m)
        hots.append(hot)
        work = jnp.where(hot, -jnp.inf, work)
    exps = [jnp.exp(v - vals[0]) for v in vals]
    tot = exps[0]
    for e in exps[1:]:
        tot = tot + e
    assigned = hots[0]
    for hot in hots[1:]:
        assigned = assigned | hot
    a = assigned.astype(BF16)
    r_i = lax.broadcasted_iota(I32, (tt, tt), 0)
    c_i = lax.broadcasted_iota(I32, (tt, tt), 1)
    rank = _dot((r_i > c_i).astype(BF16), a)
    e_r = lax.broadcasted_iota(I32, (N_EXPERTS, N_EXPERTS), 0)
    e_c = lax.broadcasted_iota(I32, (N_EXPERTS, N_EXPERTS), 1)
    cnt = jnp.sum(a.astype(F32), axis=0, keepdims=True)
    cnt = jnp.floor((cnt + (SUBLANES - 1)) * (1.0 / SUBLANES)) * SUBLANES
    cnt_rows = jnp.broadcast_to(cnt, (SUBLANES, N_EXPERTS)).astype(BF16)
    off = _dot(cnt_rows, (e_r < e_c).astype(BF16))[0:1, :]
    slot = rank + off
    k_lane = lax.broadcasted_iota(I32, (tt, TOP_K), 1)
    g_out = jnp.zeros((tt, TOP_K), F32)
    l_out = jnp.zeros((tt, TOP_K), F32)
    for k in range(TOP_K):
        lk = jnp.sum(jnp.where(hots[k], slot, 0.0), axis=-1, keepdims=True)
        g_out = jnp.where(k_lane == k, exps[k] / tot, g_out)
        l_out = jnp.where(k_lane == k, lk, l_out)
    g_ref[...] = g_out
    loc_ref[...] = l_out.astype(I32)
    cnt_ref[...] = cnt.astype(I32)
    off_ref[...] = off.astype(I32)


def _route(logits, tt):
    t = logits.shape[0]
    nt = t // tt
    tk = pl.BlockSpec((tt, TOP_K), lambda i: (i, 0))
    per_tile = pl.BlockSpec((None, 1, N_EXPERTS), lambda i: (i, 0, 0))
    return pl.pallas_call(
        _route_body,
        grid=(nt,),
        in_specs=[pl.BlockSpec((tt, N_EXPERTS), lambda i: (i, 0))],
        out_specs=[tk, tk, per_tile, per_tile],
        out_shape=[jax.ShapeDtypeStruct((t, TOP_K), F32), jax.ShapeDtypeStruct((t, TOP_K), I32),
                   jax.ShapeDtypeStruct((nt, 1, N_EXPERTS), I32), jax.ShapeDtypeStruct((nt, 1, N_EXPERTS), I32)],
        compiler_params=_cparams("arbitrary"),
        name="moe_route",
    )(logits)


def _sorted_rows(tt):
    return tt * TOP_K + N_EXPERTS * SUBLANES


def _run_copies(tt, tile, cnt_ref, off_ref, base_ref, make_copy, wait):
    def per_expert(e, carry):
        n = cnt_ref[tile, e]
        lo = off_ref[tile, e]
        go = base_ref[tile, e]
        done = 0
        size = tt
        while size >= SUBLANES:
            @pl.when((n & size) != 0)
            def _(size=size, done=done):
                cp = make_copy(pl.multiple_of(lo + done, SUBLANES), pl.multiple_of(go + done, SUBLANES), size)
                if wait:
                    cp.wait()
                else:
                    cp.start()
            done = done + (n & size)
            size //= 2
        return carry

    lax.fori_loop(0, N_EXPERTS, per_expert, 0)


def _dispatch_body(tm, tt, cnt_ref, off_ref, base_ref, zstart_ref, loc_ref, x_ref, o_hbm,
                   zero_ref, srt_ref, zsem, sems):
    i = pl.program_id(0)
    last = pl.num_programs(0) - 1
    r = _sorted_rows(tt)

    def zero_copy(j):
        return pltpu.make_async_copy(zero_ref, o_hbm.at[pl.ds(pl.multiple_of(zstart_ref[j], tm), tm), :], zsem)

    @pl.when(i == 0)
    def _():
        zero_ref[...] = jnp.zeros_like(zero_ref)

        def start(j, carry):
            @pl.when(zstart_ref[j] >= 0)
            def _():
                zero_copy(j).start()
            return carry

        def wait(j, carry):
            @pl.when(zstart_ref[j] >= 0)
            def _():
                zero_copy(j).wait()
            return carry

        lax.fori_loop(0, zstart_ref.shape[0], start, 0)
        lax.fori_loop(0, zstart_ref.shape[0], wait, 0)

    loc = loc_ref[...]
    slot_i = lax.broadcasted_iota(I32, (tt, r), 1)
    hit = slot_i == loc[:, 0:1]
    for k in range(1, TOP_K):
        hit = hit | (slot_i == loc[:, k:k + 1])
    buf = i % 2
    srt_ref[buf] = lax.dot_general(hit.astype(BF16), x_ref[...].astype(BF16), TN_DIMS,
                                   preferred_element_type=F32)

    def copies(tile, wait):
        b = tile % 2

        def make_copy(lo, go, size):
            return pltpu.make_async_copy(srt_ref.at[b, pl.ds(lo, size), :], o_hbm.at[pl.ds(go, size), :],
                                         sems.at[b])

        _run_copies(tt, tile, cnt_ref, off_ref, base_ref, make_copy, wait)

    copies(i, False)

    @pl.when(i > 0)
    def _():
        copies(i - 1, True)

    @pl.when(i == last)
    def _():
        copies(i, True)


def _dispatch(cnt, off, base, zero_starts, loc, hn, n_rows, tm, tt):
    t = hn.shape[0]
    smem = pl.BlockSpec(memory_space=pltpu.SMEM)
    return pl.pallas_call(
        functools.partial(_dispatch_body, tm, tt),
        grid_spec=pltpu.PrefetchScalarGridSpec(
            num_scalar_prefetch=0, grid=(t // tt,),
            in_specs=[smem, smem, smem, smem,
                      pl.BlockSpec((tt, TOP_K), lambda i: (i, 0)),
                      pl.BlockSpec((tt, D_MODEL), lambda i: (i, 0))],
            out_specs=pl.BlockSpec(memory_space=pl.ANY),
            scratch_shapes=[pltpu.VMEM((tm, D_MODEL), F32), pltpu.VMEM((2, _sorted_rows(tt), D_MODEL), F32),
                            pltpu.SemaphoreType.DMA, pltpu.SemaphoreType.DMA((2,))]),
        out_shape=jax.ShapeDtypeStruct((n_rows, D_MODEL), F32),
        compiler_params=_cparams("arbitrary"),
        name="moe_dispatch",
    )(cnt, off, base, zero_starts, loc, hn)


def _expert_body(be_ref, nu_ref, x_ref, wgu_ref, bgu_ref, wdn_ref, bdn_ref, y_ref):
    i = pl.program_id(0)

    @pl.when(i < nu_ref[0])
    def _():
        gu = _dot(x_ref[...].astype(BF16), wgu_ref[...]) + bgu_ref[...]
        gate = jnp.minimum(gu[:, :D_FF], SWIGLU_LIMIT)
        up = jnp.clip(gu[:, D_FF:], -SWIGLU_LIMIT, SWIGLU_LIMIT)
        act = (up + 1.0) * (gate * _sigmoid(SWIGLU_ALPHA * gate))
        y_ref[...] = _dot(act.astype(BF16), wdn_ref[...]) + bdn_ref[...]

    @pl.when(i >= nu_ref[0])
    def _():
        y_ref[...] = jnp.zeros_like(y_ref)


def _experts(block_e, n_used, xs, wgu, bgu, wdn, bdn, tm):
    n_rows = xs.shape[0]
    return pl.pallas_call(
        _expert_body,
        grid_spec=pltpu.PrefetchScalarGridSpec(
            num_scalar_prefetch=2, grid=(n_rows // tm,),
            in_specs=[pl.BlockSpec((tm, D_MODEL), lambda i, be, nu: (jnp.minimum(i, nu[0] - 1), 0)),
                      pl.BlockSpec((None, D_MODEL, 2 * D_FF), lambda i, be, nu: (be[i], 0, 0)),
                      pl.BlockSpec((None, 1, 2 * D_FF), lambda i, be, nu: (be[i], 0, 0)),
                      pl.BlockSpec((None, D_FF, D_MODEL), lambda i, be, nu: (be[i], 0, 0)),
                      pl.BlockSpec((None, 1, D_MODEL), lambda i, be, nu: (be[i], 0, 0))],
            out_specs=pl.BlockSpec((tm, D_MODEL), lambda i, be, nu: (i, 0))),
        out_shape=jax.ShapeDtypeStruct((n_rows, D_MODEL), F32),
        compiler_params=_cparams("arbitrary"),
        name="moe_experts",
    )(block_e, n_used, xs, wgu, bgu, wdn, bdn)


def _combine_body(tt, cnt_ref, off_ref, base_ref, loc_ref, g_ref, h_ref, gfin_ref, ys_hbm, y_ref,
                  buf_ref, sems):
    i = pl.program_id(0)
    last = pl.num_programs(0) - 1
    r = _sorted_rows(tt)

    def copies(tile, wait):
        b = tile % 2

        def make_copy(lo, go, size):
            return pltpu.make_async_copy(ys_hbm.at[pl.ds(go, size), :], buf_ref.at[b, pl.ds(lo, size), :],
                                         sems.at[b])

        _run_copies(tt, tile, cnt_ref, off_ref, base_ref, make_copy, wait)

    @pl.when(i == 0)
    def _():
        buf_ref[...] = jnp.zeros_like(buf_ref)
        copies(0, False)

    @pl.when(i < last)
    def _():
        copies(i + 1, False)

    copies(i, True)
    loc = loc_ref[...]
    gates = g_ref[...]
    slot_i = lax.broadcasted_iota(I32, (tt, r), 1)
    gmat = jnp.where(slot_i == loc[:, 0:1], gates[:, 0:1], 0.0)
    for k in range(1, TOP_K):
        gmat = gmat + jnp.where(slot_i == loc[:, k:k + 1], gates[:, k:k + 1], 0.0)
    out = h_ref[...] + _dot(gmat.astype(BF16), buf_ref[i % 2].astype(BF16))
    y_ref[...] = _rms(out, gfin_ref[...])


def _combine(cnt, off, base, loc, gates, h2, norm_final, ys, tt):
    t = h2.shape[0]
    tok = pl.BlockSpec((tt, D_MODEL), lambda i: (i, 0))
    tk = pl.BlockSpec((tt, TOP_K), lambda i: (i, 0))
    smem = pl.BlockSpec(memory_space=pltpu.SMEM)
    return pl.pallas_call(
        functools.partial(_combine_body, tt),
        grid_spec=pltpu.PrefetchScalarGridSpec(
            num_scalar_prefetch=0, grid=(t // tt,),
            in_specs=[smem, smem, smem, tk, tk, tok,
                      pl.BlockSpec((1, D_MODEL), lambda i: (0, 0)),
                      pl.BlockSpec(memory_space=pl.ANY)],
            out_specs=tok,
            scratch_shapes=[pltpu.VMEM((2, _sorted_rows(tt), D_MODEL), F32), pltpu.SemaphoreType.DMA((2,))]),
        out_shape=jax.ShapeDtypeStruct((t, D_MODEL), F32),
        compiler_params=_cparams("arbitrary"),
        name="moe_combine",
    )(cnt, off, base, loc, gates, h2, norm_final, ys)


def _moe_and_final_norm(hn, logits, h2, w, tt, tm):
    t = hn.shape[0]
    gates, loc, cnt3, off3 = _route(logits, tt)
    cnt = cnt3[:, 0, :]
    counts = jnp.sum(cnt, axis=0)
    padded = (counts + tm - 1) // tm * tm
    pad_end = jnp.cumsum(padded)
    start = pad_end - padded
    off = off3[:, 0, :]
    base = (start[None, :] + jnp.cumsum(cnt, axis=0) - cnt).astype(I32)
    n_blocks = (t * TOP_K + (t // tt) * N_EXPERTS * (SUBLANES - 1) + N_EXPERTS * (tm - 1)) // tm
    n_rows = n_blocks * tm
    block_start = jnp.arange(n_blocks, dtype=I32) * tm
    block_e = jnp.minimum(jnp.sum(block_start[:, None] >= pad_end[None, :], axis=-1), N_EXPERTS - 1).astype(I32)
    n_used = (pad_end[-1:] // tm).astype(I32)
    zero_starts = jnp.concatenate([jnp.where(padded > 0, pad_end - tm, -1),
                                   jnp.where(block_start >= pad_end[-1], block_start, -1)]).astype(I32)
    xs = _dispatch(cnt, off, base, zero_starts, loc, hn, n_rows, tm, tt)
    ys = _experts(block_e, n_used, xs, w["w_gate_up"], w["b_gate_up"], w["w_down"], w["b_down"], tm)
    return _combine(cnt, off, base, loc, gates, h2, w["norm_final"], ys, tt)


def kernel(x_prompt, x_sample, mem_prompt, state_ssm, state_mamba_conv, state_short_conv, cache_mem_k, cache_mem_v, norm_mix, w_in, w_mconv, b_mconv, dt_bias, a_log, d_skip, norm_ssm, w_sconv, w_out, norm_xattn, norm_mem, w_xq, w_xk, w_xv, w_xo, norm_moe, w_router, b_router, w_gate_up, b_gate_up, w_down, b_down, norm_final):
    nbp, seq, _ = x_prompt.shape
    nbs = x_sample.shape[0]
    dt_lo = SSM_INNER + SSM_CONV_DIM
    w_in0 = w_in[0]
    w_dt = w_in0[:, dt_lo:dt_lo + SSM_HEADS]
    w = {
        "norm_mix": norm_mix, "norm_ssm": norm_ssm, "norm_xattn": norm_xattn, "norm_moe": norm_moe,
        "norm_final": norm_final.reshape(1, D_MODEL),
        "w_a": w_in0[:, :dt_lo].astype(BF16),
        "w_dt": w_dt.astype(BF16), "w_dt_t": w_dt.T.astype(BF16),
        "w_b": w_in0[:, dt_lo + SSM_HEADS:].astype(BF16),
        "w_mconv": w_mconv[0], "b_mconv": b_mconv,
        "dt_bias": dt_bias, "dt_bias_t": dt_bias.reshape(SSM_HEADS, 1),
        "a_log": a_log, "a_log_t": a_log.reshape(SSM_HEADS, 1),
        "d_skip": jnp.repeat(d_skip, SSM_HEAD_DIM, axis=1),
        "w_sconv": w_sconv[0], "w_out": w_out[0].astype(BF16),
        "w_xq": w_xq[0].astype(BF16), "w_xo": w_xo[0].astype(BF16),
        "w_router": w_router[0].astype(BF16), "b_router": b_router,
        "w_gate_up": w_gate_up[0].astype(BF16), "b_gate_up": b_gate_up[0].reshape(N_EXPERTS, 1, 2 * D_FF),
        "w_down": w_down[0].astype(BF16), "b_down": b_down[0].reshape(N_EXPERTS, 1, D_MODEL),
    }

    k_p, v_p, kb, vb = _mem_kv(mem_prompt.reshape(nbp * N_MEM, D_MODEL), norm_mem,
                               w_xk[0].astype(BF16), w_xv[0].astype(BF16))
    h1, ssm_p, mconv_p, sconv_p = _prompt_mixer(x_prompt.reshape(nbp * seq, D_MODEL), nbp, w)
    h2, hn, logits = _prompt_attn(h1, kb, vb, nbp, w)
    y_prompt = _moe_and_final_norm(hn, logits, h2, w, MIX_TILE, MOE_ROW_TILE)

    xs2 = x_sample.reshape(nbs, D_MODEL)
    mstate_t = jnp.transpose(state_mamba_conv[0], (1, 0, 2))
    sstate_t = jnp.transpose(state_short_conv[0], (1, 0, 2))
    z, xs_, dtx, dec, bm, cm, yb, sga, mnew_t, snew_t = _sample_proj(xs2, mstate_t, sstate_t, w)
    ssm_s, y_s = _sample_state(dec, state_ssm[0].reshape(nbs, SSM_INNER, SSM_STATE), dtx, bm, cm)
    h1s, q_s = _sample_fin1(xs2, y_s, xs_, z, yb, sga, w)
    o_s = _sample_attn(q_s.reshape(nbs, 1, D_MODEL),
                       cache_mem_k[0], cache_mem_v[0])
    h2s, hns, logits_s = _sample_fin2(h1s, o_s.reshape(nbs, D_MODEL), w)
    y_sample = _moe_and_final_norm(hns, logits_s, h2s, w, nbs, LANES)

    return (y_prompt.reshape(nbp, seq, D_MODEL),
            y_sample.reshape(nbs, 1, D_MODEL),
            ssm_p.reshape(1, nbp, SSM_HEADS, SSM_HEAD_DIM, SSM_STATE),
            mconv_p[None], sconv_p[None],
            k_p.reshape(1, nbp, N_MEM, XA_HEADS, XA_HEAD_DIM),
            v_p.reshape(1, nbp, N_MEM, XA_HEADS, XA_HEAD_DIM),
            ssm_s.reshape(1, nbs, SSM_HEADS, SSM_HEAD_DIM, SSM_STATE),
            jnp.transpose(mnew_t, (1, 0, 2))[None],
            jnp.transpose(snew_t, (1, 0, 2))[None])
```

```python
import functools

import jax
import jax.numpy as jnp
from jax import lax
from jax.experimental import pallas as pl
from jax.experimental.pallas import tpu as pltpu

F32 = jnp.float32
BF16 = jnp.bfloat16
I32 = jnp.int32

D_MODEL = 1024
N_MEM = 256
SSM_HEADS = 16
SSM_HEAD_DIM = 64
SSM_INNER = SSM_HEADS * SSM_HEAD_DIM
SSM_STATE = 128
SSM_GROUPS = 4
HEADS_PER_GROUP = SSM_HEADS // SSM_GROUPS
GROUP_WIDTH = SSM_INNER // SSM_GROUPS
SSM_CONV = 4
SSM_CONV_DIM = SSM_INNER + 2 * SSM_GROUPS * SSM_STATE
SC_CONV = 3
XA_HEADS = 4
XA_HEAD_DIM = D_MODEL // XA_HEADS
N_EXPERTS = 32
TOP_K = 4
D_FF = D_MODEL
SWIGLU_LIMIT = 7.0
SWIGLU_ALPHA = 1.702
EPS = 1e-6

LANES = 128
SUBLANES = 8
VMEM_LIMIT = 56 * 1024 * 1024

MIX_TILE = 256
MOE_ROW_TILE = 512
STATE_BB = 8
ATTN_BB = 4

NT_DIMS = (((1,), (1,)), ((), ()))
TN_DIMS = (((0,), (0,)), ((), ()))


def _cparams(*sem):
    return pltpu.CompilerParams(dimension_semantics=sem, vmem_limit_bytes=VMEM_LIMIT)


def _const_spec(shape):
    nd = len(shape)
    return pl.BlockSpec(shape, lambda *_: (0,) * nd, pipeline_mode=pl.Buffered(1))


def _sigmoid(x):
    return 1.0 / (1.0 + jnp.exp(-x))


def _silu(x):
    return x * _sigmoid(x)


def _softplus(x):
    return jnp.maximum(x, 0.0) + jnp.log(1.0 + jnp.exp(-jnp.abs(x)))


def _rms(x, g):
    ms = jnp.mean(x * x, axis=-1, keepdims=True)
    return x * lax.rsqrt(ms + EPS) * g


def _dot(a, b):
    return jnp.dot(a, b, preferred_element_type=F32)


def _dot_nt(a, b):
    return lax.dot_general(a, b, NT_DIMS, preferred_element_type=F32)


def _expand_heads(v):
    rows = v.shape[0]
    lane = lax.broadcasted_iota(I32, (rows, LANES), 1)
    pieces = []
    for j in range(SSM_HEADS // 2):
        a = jnp.broadcast_to(v[:, 2 * j:2 * j + 1], (rows, LANES))
        b = jnp.broadcast_to(v[:, 2 * j + 1:2 * j + 2], (rows, LANES))
        pieces.append(jnp.where(lane < SSM_HEAD_DIM, a, b))
    return jnp.concatenate(pieces, axis=1)


def _group_rmsnorm(u, g):
    outs = []
    for k in range(SSM_GROUPS):
        ug = u[:, k * GROUP_WIDTH:(k + 1) * GROUP_WIDTH]
        ms = jnp.mean(ug * ug, axis=-1, keepdims=True)
        outs.append(ug * lax.rsqrt(ms + EPS))
    return jnp.concatenate(outs, axis=1) * g


def _memkv_body(mem_ref, g_ref, wk_ref, wv_ref, k_ref, v_ref, kb_ref, vb_ref):
    mn = _rms(mem_ref[...], g_ref[...]).astype(BF16)
    k = _dot(mn, wk_ref[...])
    v = _dot(mn, wv_ref[...])
    k_ref[...] = k
    v_ref[...] = v
    kb_ref[...] = k.astype(BF16)
    vb_ref[...] = v.astype(BF16)


def _mem_kv(mem2d, norm_mem, wk, wv):
    rows = mem2d.shape[0]
    nb = rows // N_MEM
    blk = pl.BlockSpec((N_MEM, D_MODEL), lambda b: (b, 0))
    return pl.pallas_call(
        _memkv_body,
        grid=(nb,),
        in_specs=[blk, _const_spec((1, D_MODEL)), _const_spec((D_MODEL, D_MODEL)),
                  _const_spec((D_MODEL, D_MODEL))],
        out_specs=[blk, blk, blk, blk],
        out_shape=[jax.ShapeDtypeStruct((rows, D_MODEL), F32)] * 2
        + [jax.ShapeDtypeStruct((rows, D_MODEL), BF16)] * 2,
        compiler_params=_cparams("arbitrary"),
        name="mem_kv",
    )(mem2d, norm_mem, wk, wv)


def _mix_body(x_ref, gmix_ref, wa_ref, wdtc_ref, wdtr_ref, wb_ref, wmc_ref, bmc_ref,
              dtb_ref, dtbt_ref, alog_ref, alogt_ref, dskip_ref, gssm_ref, wsc_ref, wout_ref,
              h_ref, ssm_ref, mbuf_ref, sbuf_ref,
              st_ref, cbuf_ref, scbuf_ref):
    tq = MIX_TILE
    c = pl.program_id(1)

    @pl.when(c == 0)
    def _():
        st_ref[...] = jnp.zeros_like(st_ref)
        cbuf_ref[0:SUBLANES, :] = jnp.zeros((SUBLANES, SSM_CONV_DIM), F32)
        scbuf_ref[0:SUBLANES, :] = jnp.zeros((SUBLANES, D_MODEL), F32)

    x = x_ref[...]
    xn = _rms(x, gmix_ref[...]).astype(BF16)

    pa = _dot(xn, wa_ref[...])
    z = pa[:, :SSM_INNER]
    u = pa[:, SSM_INNER:]
    cbuf_ref[SUBLANES:SUBLANES + tq, :] = u
    wm = wmc_ref[...]
    conv = u * wm[SSM_CONV - 1:SSM_CONV, :] + bmc_ref[...]
    for k in range(SSM_CONV - 1):
        off = SUBLANES - (SSM_CONV - 1) + k
        conv = conv + cbuf_ref[off:off + tq, :] * wm[k:k + 1, :]
    tail = cbuf_ref[tq + SUBLANES - (SSM_CONV - 1):tq + SUBLANES, :]
    mbuf_ref[...] = tail
    cbuf_ref[SUBLANES - (SSM_CONV - 1):SUBLANES, :] = tail
    xbc = _silu(conv)
    xs = xbc[:, :SSM_INNER]
    bm = xbc[:, SSM_INNER:SSM_INNER + SSM_GROUPS * SSM_STATE]
    cm = xbc[:, SSM_INNER + SSM_GROUPS * SSM_STATE:]

    dt = _softplus(_dot(xn, wdtc_ref[...]) + dtb_ref[...])
    dtt = _softplus(_dot_nt(wdtr_ref[...], xn) + dtbt_ref[...])
    a_row = -jnp.exp(alog_ref[...])
    a_col = -jnp.exp(alogt_ref[...])
    row_i = lax.broadcasted_iota(I32, (tq, tq), 0)
    col_i = lax.broadcasted_iota(I32, (tq, tq), 1)
    causal = row_i >= col_i
    tril = causal.astype(F32)
    triu = (row_i <= col_i).astype(F32)
    a_cum = jnp.dot(tril, dt * a_row, precision=lax.Precision.HIGHEST,
                    preferred_element_type=F32)
    a_cumt = jnp.dot(dtt * a_col, triu, precision=lax.Precision.HIGHEST,
                     preferred_element_type=F32)
    a_last = a_cum[tq - 1:tq, :]

    xdt = xs * _expand_heads(dt)
    in_decay = _expand_heads(jnp.exp(a_cum))
    to_end = _expand_heads(jnp.exp(a_last - a_cum))
    chunk_decay = _expand_heads(jnp.exp(a_last))
    xdt_b = xdt.astype(BF16)
    xend_b = (xdt * to_end).astype(BF16)
    lane = lax.broadcasted_iota(I32, (tq, LANES), 1)

    y_groups = []
    for g in range(SSM_GROUPS):
        cg = cm[:, g * SSM_STATE:(g + 1) * SSM_STATE].astype(BF16)
        bg_f = bm[:, g * SSM_STATE:(g + 1) * SSM_STATE]
        bg = bg_f.astype(BF16)
        scores = _dot_nt(cg, bg)
        gs = slice(g * GROUP_WIDTH, (g + 1) * GROUP_WIDTH)
        st_g = st_ref[:, gs]
        y_off = _dot(cg, st_g.astype(BF16)) * in_decay[:, gs]
        pair_out = []
        for pr in range(HEADS_PER_GROUP // 2):
            h0 = g * HEADS_PER_GROUP + 2 * pr
            xp = xdt_b[:, h0 * SSM_HEAD_DIM:(h0 + 2) * SSM_HEAD_DIM]
            ys = []
            for h in (h0, h0 + 1):
                seg = a_cum[:, h:h + 1] - a_cumt[h:h + 1, :]
                decay = jnp.where(causal, jnp.exp(jnp.minimum(seg, 0.0)), 0.0)
                ys.append(_dot((scores * decay).astype(BF16), xp))
            pair_out.append(jnp.where(lane < SSM_HEAD_DIM, ys[0], ys[1]))
        y_groups.append(jnp.concatenate(pair_out, axis=1) + y_off)
        st_ref[:, gs] = st_g * chunk_decay[:, gs] + _dot(bg_f.T.astype(BF16), xend_b[:, gs])
    y = jnp.concatenate(y_groups, axis=1) + dskip_ref[...] * xs
    y_a = _group_rmsnorm(y * _silu(z), gssm_ref[...])

    @pl.when(c == pl.num_programs(1) - 1)
    def _():
        ssm_ref[...] = st_ref[...].T

    pb = _dot(xn, wb_ref[...])
    sc_b = pb[:, 0:D_MODEL]
    cv = pb[:, D_MODEL:2 * D_MODEL] * pb[:, 2 * D_MODEL:3 * D_MODEL]
    g_a = pb[:, 3 * D_MODEL:4 * D_MODEL]
    g_b = pb[:, 4 * D_MODEL:5 * D_MODEL]
    scbuf_ref[SUBLANES:SUBLANES + tq, :] = cv
    ws = wsc_ref[...]
    uc = cv * ws[SC_CONV - 1:SC_CONV, :]
    for k in range(SC_CONV - 1):
        off = SUBLANES - (SC_CONV - 1) + k
        uc = uc + scbuf_ref[off:off + tq, :] * ws[k:k + 1, :]
    stail = scbuf_ref[tq + SUBLANES - (SC_CONV - 1):tq + SUBLANES, :]
    sbuf_ref[...] = stail
    scbuf_ref[SUBLANES - (SC_CONV - 1):SUBLANES, :] = stail
    merged = _sigmoid(g_a) * y_a + _sigmoid(g_b) * (sc_b * uc)
    h_ref[...] = x + _dot(merged.astype(BF16), wout_ref[...])


def _prompt_mixer(x2d, nb, w):
    t = x2d.shape[0]
    nc = t // nb // MIX_TILE
    tok = pl.BlockSpec((MIX_TILE, D_MODEL), lambda b, c: (b * nc + c, 0))
    return pl.pallas_call(
        _mix_body,
        grid=(nb, nc),
        in_specs=[tok, _const_spec((1, D_MODEL)),
                  _const_spec((D_MODEL, SSM_INNER + SSM_CONV_DIM)),
                  _const_spec((D_MODEL, SSM_HEADS)), _const_spec((SSM_HEADS, D_MODEL)),
                  _const_spec((D_MODEL, 5 * D_MODEL)),
                  _const_spec((SSM_CONV, SSM_CONV_DIM)), _const_spec((1, SSM_CONV_DIM)),
                  _const_spec((1, SSM_HEADS)), _const_spec((SSM_HEADS, 1)),
                  _const_spec((1, SSM_HEADS)), _const_spec((SSM_HEADS, 1)),
                  _const_spec((1, SSM_INNER)), _const_spec((1, SSM_INNER)),
                  _const_spec((SC_CONV, D_MODEL)), _const_spec((D_MODEL, D_MODEL))],
        out_specs=[tok,
                   pl.BlockSpec((None, SSM_INNER, SSM_STATE), lambda b, c: (b, 0, 0)),
                   pl.BlockSpec((None, SSM_CONV - 1, SSM_CONV_DIM), lambda b, c: (b, 0, 0)),
                   pl.BlockSpec((None, SC_CONV - 1, D_MODEL), lambda b, c: (b, 0, 0))],
        out_shape=[jax.ShapeDtypeStruct((t, D_MODEL), F32),
                   jax.ShapeDtypeStruct((nb, SSM_INNER, SSM_STATE), F32),
                   jax.ShapeDtypeStruct((nb, SSM_CONV - 1, SSM_CONV_DIM), F32),
                   jax.ShapeDtypeStruct((nb, SC_CONV - 1, D_MODEL), F32)],
        scratch_shapes=[pltpu.VMEM((SSM_STATE, SSM_INNER), F32),
                        pltpu.VMEM((MIX_TILE + SUBLANES, SSM_CONV_DIM), F32),
                        pltpu.VMEM((MIX_TILE + SUBLANES, D_MODEL), F32)],
        compiler_params=_cparams("arbitrary", "arbitrary"),
        name="prompt_mixer",
    )(x2d, w["norm_mix"], w["w_a"], w["w_dt"], w["w_dt_t"], w["w_b"], w["w_mconv"], w["b_mconv"],
      w["dt_bias"], w["dt_bias_t"], w["a_log"], w["a_log_t"], w["d_skip"], w["norm_ssm"],
      w["w_sconv"], w["w_out"])


def _router_tail(h2, gmoe_ref, wr_ref, br_ref, h2_ref, hn_ref, lg_ref):
    h2_ref[...] = h2
    hn = _rms(h2, gmoe_ref[...])
    hn_ref[...] = hn
    lg_ref[...] = _dot(hn.astype(BF16), wr_ref[...]) + br_ref[...]


def _attn_body(h_ref, gx_ref, wq_ref, k_ref, v_ref, wo_ref, gmoe_ref, wr_ref, br_ref,
               h2_ref, hn_ref, lg_ref):
    h = h_ref[...]
    hn = _rms(h, gx_ref[...]).astype(BF16)
    q = _dot(hn, wq_ref[...]).astype(BF16)
    outs = []
    for hd in range(XA_HEADS):
        sl = slice(hd * XA_HEAD_DIM, (hd + 1) * XA_HEAD_DIM)
        s = _dot_nt(q[:, sl], k_ref[:, sl]) * (XA_HEAD_DIM ** -0.5)
        e = jnp.exp(s - jnp.max(s, axis=-1, keepdims=True))
        p = e / jnp.sum(e, axis=-1, keepdims=True)
        outs.append(_dot(p.astype(BF16), v_ref[:, sl]))
    o = jnp.concatenate(outs, axis=1).astype(BF16)
    h2 = h + _dot(o, wo_ref[...])
    _router_tail(h2, gmoe_ref, wr_ref, br_ref, h2_ref, hn_ref, lg_ref)


def _prompt_attn(h1, kb, vb, nb, w):
    t = h1.shape[0]
    nc = t // nb // MIX_TILE
    tok = pl.BlockSpec((MIX_TILE, D_MODEL), lambda b, c: (b * nc + c, 0))
    kv = pl.BlockSpec((N_MEM, D_MODEL), lambda b, c: (b, 0))
    return pl.pallas_call(
        _attn_body,
        grid=(nb, nc),
        in_specs=[tok, _const_spec((1, D_MODEL)), _const_spec((D_MODEL, D_MODEL)), kv, kv,
                  _const_spec((D_MODEL, D_MODEL)), _const_spec((1, D_MODEL)),
                  _const_spec((D_MODEL, N_EXPERTS)), _const_spec((1, N_EXPERTS))],
        out_specs=[tok, tok, pl.BlockSpec((MIX_TILE, N_EXPERTS), lambda b, c: (b * nc + c, 0))],
        out_shape=[jax.ShapeDtypeStruct((t, D_MODEL), F32),
                   jax.ShapeDtypeStruct((t, D_MODEL), F32),
                   jax.ShapeDtypeStruct((t, N_EXPERTS), F32)],
        compiler_params=_cparams("arbitrary", "arbitrary"),
        name="prompt_attn",
    )(h1, w["norm_xattn"], w["w_xq"], kb, vb, w["w_xo"], w["norm_moe"], w["w_router"], w["b_router"])


def _sproj_body(x_ref, gmix_ref, wa_ref, wdtc_ref, wb_ref, wmc_ref, bmc_ref, dtb_ref, alog_ref,
                wsc_ref, mst_ref, sst_ref,
                z_ref, xs_ref, dtx_ref, dec_ref, bm_ref, cm_ref, yb_ref, sga_ref, mnew_ref, snew_ref):
    x = x_ref[...]
    xn = _rms(x, gmix_ref[...]).astype(BF16)
    pa = _dot(xn, wa_ref[...])
    z_ref[...] = pa[:, :SSM_INNER]
    u = pa[:, SSM_INNER:]
    wm = wmc_ref[...]
    conv = u * wm[SSM_CONV - 1:SSM_CONV, :] + bmc_ref[...]
    for k in range(SSM_CONV - 1):
        conv = conv + mst_ref[k] * wm[k:k + 1, :]
    for k in range(SSM_CONV - 2):
        mnew_ref[k] = mst_ref[k + 1]
    mnew_ref[SSM_CONV - 2] = u
    xbc = _silu(conv)
    xs = xbc[:, :SSM_INNER]
    xs_ref[...] = xs
    bm_ref[...] = xbc[:, SSM_INNER:SSM_INNER + SSM_GROUPS * SSM_STATE]
    cm_ref[...] = xbc[:, SSM_INNER + SSM_GROUPS * SSM_STATE:]
    dt = _softplus(_dot(xn, wdtc_ref[...]) + dtb_ref[...])
    dec_ref[...] = jnp.exp(dt * (-jnp.exp(alog_ref[...])))
    dtx_ref[...] = xs * _expand_heads(dt)
    pb = _dot(xn, wb_ref[...])
    cv = pb[:, D_MODEL:2 * D_MODEL] * pb[:, 2 * D_MODEL:3 * D_MODEL]
    ws = wsc_ref[...]
    uc = cv * ws[SC_CONV - 1:SC_CONV, :]
    for k in range(SC_CONV - 1):
        uc = uc + sst_ref[k] * ws[k:k + 1, :]
    for k in range(SC_CONV - 2):
        snew_ref[k] = sst_ref[k + 1]
    snew_ref[SC_CONV - 2] = cv
    yb_ref[...] = _sigmoid(pb[:, 4 * D_MODEL:5 * D_MODEL]) * (pb[:, 0:D_MODEL] * uc)
    sga_ref[...] = _sigmoid(pb[:, 3 * D_MODEL:4 * D_MODEL])


def _sample_proj(x, mstate_t, sstate_t, w):
    nb = x.shape[0]
    f = lambda *s: jax.ShapeDtypeStruct(s, F32)
    return pl.pallas_call(
        _sproj_body,
        out_shape=[f(nb, SSM_INNER), f(nb, SSM_INNER), f(nb, SSM_INNER), f(nb, SSM_HEADS),
                   f(nb, SSM_GROUPS * SSM_STATE), f(nb, SSM_GROUPS * SSM_STATE),
                   f(nb, D_MODEL), f(nb, D_MODEL),
                   f(SSM_CONV - 1, nb, SSM_CONV_DIM), f(SC_CONV - 1, nb, D_MODEL)],
        compiler_params=pltpu.CompilerParams(vmem_limit_bytes=VMEM_LIMIT),
        name="sample_proj",
    )(x, w["norm_mix"], w["w_a"], w["w_dt"], w["w_b"], w["w_mconv"], w["b_mconv"], w["dt_bias"],
      w["a_log"], w["w_sconv"], mstate_t, sstate_t)


def _sstate_body(dec_ref, s_ref, dtx_ref, bm_ref, cm_ref, snew_ref, y_ref):
    i = pl.program_id(0)
    rows_per_blk = LANES
    for j in range(STATE_BB):
        b = i * STATE_BB + j
        dtx_row = dtx_ref[j:j + 1, :]
        y_parts = []
        for g in range(SSM_GROUPS):
            b_row = bm_ref[j:j + 1, g * SSM_STATE:(g + 1) * SSM_STATE]
            c_row = cm_ref[j:j + 1, g * SSM_STATE:(g + 1) * SSM_STATE].astype(BF16)
            new_blocks = []
            for q in range(GROUP_WIDTH // rows_per_blk):
                r0 = g * GROUP_WIDTH + q * rows_per_blk
                dcol = jnp.broadcast_to(dtx_row[:, r0:r0 + rows_per_blk], (rows_per_blk, LANES)).T
                sub = []
                for hh in range(rows_per_blk // SSM_HEAD_DIM):
                    h = r0 // SSM_HEAD_DIM + hh
                    lo = hh * SSM_HEAD_DIM
                    s_old = s_ref[j, r0 + lo:r0 + lo + SSM_HEAD_DIM, :]
                    sub.append(s_old * dec_ref[b, h] + dcol[lo:lo + SSM_HEAD_DIM, :] * b_row)
                blk = jnp.concatenate(sub, axis=0)
                snew_ref[j, r0:r0 + rows_per_blk, :] = blk
                new_blocks.append(blk.astype(BF16))
            s_g = jnp.concatenate(new_blocks, axis=0)
            y_parts.append(_dot_nt(c_row, s_g))
        y_ref[j:j + 1, :] = jnp.concatenate(y_parts, axis=1)


def _sample_state(dec, state, dtx, bm, cm):
    nb = state.shape[0]
    row = lambda wdt: pl.BlockSpec((STATE_BB, wdt), lambda i, dec: (i, 0))
    st = pl.BlockSpec((STATE_BB, SSM_INNER, SSM_STATE), lambda i, dec: (i, 0, 0))
    return pl.pallas_call(
        _sstate_body,
        grid_spec=pltpu.PrefetchScalarGridSpec(
            num_scalar_prefetch=1, grid=(nb // STATE_BB,),
            in_specs=[st, row(SSM_INNER), row(SSM_GROUPS * SSM_STATE), row(SSM_GROUPS * SSM_STATE)],
            out_specs=[st, row(SSM_INNER)]),
        out_shape=[jax.ShapeDtypeStruct(state.shape, F32), jax.ShapeDtypeStruct((nb, SSM_INNER), F32)],
        compiler_params=_cparams("arbitrary"),
        name="sample_state",
    )(dec, state, dtx, bm, cm)


def _sfin1_body(x_ref, y_ref, xs_ref, z_ref, yb_ref, sga_ref, dskip_ref, gssm_ref, wout_ref,
                gx_ref, wq_ref, h_ref, q_ref):
    y = y_ref[...] + dskip_ref[...] * xs_ref[...]
    y_a = _group_rmsnorm(y * _silu(z_ref[...]), gssm_ref[...])
    merged = sga_ref[...] * y_a + yb_ref[...]
    h = x_ref[...] + _dot(merged.astype(BF16), wout_ref[...])
    h_ref[...] = h
    q_ref[...] = _dot(_rms(h, gx_ref[...]).astype(BF16), wq_ref[...])


def _sample_fin1(x, y, xs, z, yb, sga, w):
    nb = x.shape[0]
    return pl.pallas_call(
        _sfin1_body,
        out_shape=[jax.ShapeDtypeStruct((nb, D_MODEL), F32)] * 2,
        compiler_params=pltpu.CompilerParams(vmem_limit_bytes=VMEM_LIMIT),
        name="sample_fin1",
    )(x, y, xs, z, yb, sga, w["d_skip"], w["norm_ssm"], w["w_out"], w["norm_xattn"], w["w_xq"])


def _sattn_body(q_ref, k_ref, v_ref, o_ref):
    for j in range(ATTN_BB):
        q_row = q_ref[j]
        q4 = jnp.concatenate([q_row[:, h * XA_HEAD_DIM:(h + 1) * XA_HEAD_DIM]
                              for h in range(XA_HEADS)], axis=0)
        s = jnp.sum(k_ref[j] * q4[None], axis=-1, keepdims=True) * (XA_HEAD_DIM ** -0.5)
        e = jnp.exp(s - jnp.max(s, axis=0, keepdims=True))
        p = e / jnp.sum(e, axis=0, keepdims=True)
        o4 = jnp.sum(p * v_ref[j], axis=0)
        o_ref[j] = jnp.concatenate([o4[h:h + 1, :] for h in range(XA_HEADS)], axis=1)


def _sample_attn(q3, k3, v3):
    nb = q3.shape[0]
    qs = pl.BlockSpec((ATTN_BB, 1, D_MODEL), lambda i: (i, 0, 0))
    kv = pl.BlockSpec((ATTN_BB, N_MEM, XA_HEADS, XA_HEAD_DIM), lambda i: (i, 0, 0, 0))
    return pl.pallas_call(
        _sattn_body,
        grid=(nb // ATTN_BB,),
        in_specs=[qs, kv, kv],
        out_specs=qs,
        out_shape=jax.ShapeDtypeStruct((nb, 1, D_MODEL), F32),
        compiler_params=_cparams("arbitrary"),
        name="sample_attn",
    )(q3, k3, v3)


def _sfin2_body(h_ref, o_ref, wo_ref, gmoe_ref, wr_ref, br_ref, h2_ref, hn_ref, lg_ref):
    h2 = h_ref[...] + _dot(o_ref[...].astype(BF16), wo_ref[...])
    _router_tail(h2, gmoe_ref, wr_ref, br_ref, h2_ref, hn_ref, lg_ref)


def _sample_fin2(h1, o, w):
    nb = h1.shape[0]
    return pl.pallas_call(
        _sfin2_body,
        out_shape=[jax.ShapeDtypeStruct((nb, D_MODEL), F32)] * 2
        + [jax.ShapeDtypeStruct((nb, N_EXPERTS), F32)],
        compiler_params=pltpu.CompilerParams(vmem_limit_bytes=VMEM_LIMIT),
        name="sample_fin2",
    )(h1, o, w["w_xo"], w["norm_moe"], w["w_router"], w["b_router"])


def _route_body(lg_ref, g_ref, loc_ref, cnt_ref, off_ref):
    tt = lg_ref.shape[0]
    work = lg_ref[...]
    lane = lax.broadcasted_iota(I32, (tt, N_EXPERTS), 1).astype(F32)
    vals, hots = [], []
    for _ in range(TOP_K):
        m = jnp.max(work, axis=-1, keepdims=True)
        idx = jnp.min(jnp.where(work == m, lane, float(N_EXPERTS)), axis=-1, keepdims=True)
        hot = lane == idx
        vals.append(m)
        hots.append(hot)
        work = jnp.where(hot, -jnp.inf, work)
    exps = [jnp.exp(v - vals[0]) for v in vals]
    tot = exps[0]
    for e in exps[1:]:
        tot = tot + e
    assigned = hots[0]
    for hot in hots[1:]:
        assigned = assigned | hot
    a = assigned.astype(BF16)
    r_i = lax.broadcasted_iota(I32, (tt, tt), 0)
    c_i = lax.broadcasted_iota(I32, (tt, tt), 1)
    rank = _dot((r_i > c_i).astype(BF16), a)
    e_r = lax.broadcasted_iota(I32, (N_EXPERTS, N_EXPERTS), 0)
    e_c = lax.broadcasted_iota(I32, (N_EXPERTS, N_EXPERTS), 1)
    cnt = jnp.sum(a.astype(F32), axis=0, keepdims=True)
    cnt = jnp.floor((cnt + (SUBLANES - 1)) * (1.0 / SUBLANES)) * SUBLANES
    cnt_rows = jnp.broadcast_to(cnt, (SUBLANES, N_EXPERTS)).astype(BF16)
    off = _dot(cnt_rows, (e_r < e_c).astype(BF16))[0:1, :]
    slot = rank + off
    k_lane = lax.broadcasted_iota(I32, (tt, TOP_K), 1)
    g_out = jnp.zeros((tt, TOP_K), F32)
    l_out = jnp.zeros((tt, TOP_K), F32)
    for k in range(TOP_K):
        lk = jnp.sum(jnp.where(hots[k], slot, 0.0), axis=-1, keepdims=True)
        g_out = jnp.where(k_lane == k, exps[k] / tot, g_out)
        l_out = jnp.where(k_lane == k, lk, l_out)
    g_ref[...] = g_out
    loc_ref[...] = l_out.astype(I32)
    cnt_ref[...] = cnt.astype(I32)
    off_ref[...] = off.astype(I32)


def _route(logits, tt):
    t = logits.shape[0]
    nt = t // tt
    tk = pl.BlockSpec((tt, TOP_K), lambda i: (i, 0))
    per_tile = pl.BlockSpec((None, 1, N_EXPERTS), lambda i: (i, 0, 0))
    return pl.pallas_call(
        _route_body,
        grid=(nt,),
        in_specs=[pl.BlockSpec((tt, N_EXPERTS), lambda i: (i, 0))],
        out_specs=[tk, tk, per_tile, per_tile],
        out_shape=[jax.ShapeDtypeStruct((t, TOP_K), F32), jax.ShapeDtypeStruct((t, TOP_K), I32),
                   jax.ShapeDtypeStruct((nt, 1, N_EXPERTS), I32), jax.ShapeDtypeStruct((nt, 1, N_EXPERTS), I32)],
        compiler_params=_cparams("arbitrary"),
        name="moe_route",
    )(logits)


def _sorted_rows(tt):
    return tt * TOP_K + N_EXPERTS * SUBLANES


def _run_copies(tt, tile, cnt_ref, off_ref, base_ref, make_copy, wait):
    if wait:
        total = off_ref[tile, N_EXPERTS - 1] + cnt_ref[tile, N_EXPERTS - 1]

        @pl.when(total > 0)
        def _():
            make_copy(0, 0, pl.multiple_of(total, SUBLANES)).wait()
        return

    def per_expert(e, carry):
        n = cnt_ref[tile, e]

        @pl.when(n > 0)
        def _():
            make_copy(pl.multiple_of(off_ref[tile, e], SUBLANES),
                      pl.multiple_of(base_ref[tile, e], SUBLANES), pl.multiple_of(n, SUBLANES)).start()
        return carry

    lax.fori_loop(0, N_EXPERTS, per_expert, 0)


def _dispatch_body(tm, tt, cnt_ref, off_ref, base_ref, zstart_ref, loc_ref, x_ref, o_hbm,
                   zero_ref, srt_ref, zsem, sems):
    i = pl.program_id(0)
    last = pl.num_programs(0) - 1
    r = _sorted_rows(tt)

    def zero_copy(j):
        return pltpu.make_async_copy(zero_ref, o_hbm.at[pl.ds(pl.multiple_of(zstart_ref[j], tm), tm), :], zsem)

    @pl.when(i == 0)
    def _():
        zero_ref[...] = jnp.zeros_like(zero_ref)

        def start(j, carry):
            @pl.when(zstart_ref[j] >= 0)
            def _():
                zero_copy(j).start()
            return carry

        def wait(j, carry):
            @pl.when(zstart_ref[j] >= 0)
            def _():
                zero_copy(j).wait()
            return carry

        lax.fori_loop(0, zstart_ref.shape[0], start, 0)
        lax.fori_loop(0, zstart_ref.shape[0], wait, 0)

    loc = loc_ref[...]
    slot_i = lax.broadcasted_iota(I32, (tt, r), 1)
    hit = slot_i == loc[:, 0:1]
    for k in range(1, TOP_K):
        hit = hit | (slot_i == loc[:, k:k + 1])
    buf = i % 2
    srt_ref[buf] = lax.dot_general(hit.astype(BF16), x_ref[...].astype(BF16), TN_DIMS,
                                   preferred_element_type=F32)

    def copies(tile, wait):
        b = tile % 2

        def make_copy(lo, go, size):
            return pltpu.make_async_copy(srt_ref.at[b, pl.ds(lo, size), :], o_hbm.at[pl.ds(go, size), :],
                                         sems.at[b])

        _run_copies(tt, tile, cnt_ref, off_ref, base_ref, make_copy, wait)

    copies(i, False)

    @pl.when(i > 0)
    def _():
        copies(i - 1, True)

    @pl.when(i == last)
    def _():
        copies(i, True)


def _dispatch(cnt, off, base, zero_starts, loc, hn, n_rows, tm, tt):
    t = hn.shape[0]
    smem = pl.BlockSpec(memory_space=pltpu.SMEM)
    return pl.pallas_call(
        functools.partial(_dispatch_body, tm, tt),
        grid_spec=pltpu.PrefetchScalarGridSpec(
            num_scalar_prefetch=0, grid=(t // tt,),
            in_specs=[smem, smem, smem, smem,
                      pl.BlockSpec((tt, TOP_K), lambda i: (i, 0)),
                      pl.BlockSpec((tt, D_MODEL), lambda i: (i, 0))],
            out_specs=pl.BlockSpec(memory_space=pl.ANY),
            scratch_shapes=[pltpu.VMEM((tm, D_MODEL), F32), pltpu.VMEM((2, _sorted_rows(tt), D_MODEL), F32),
                            pltpu.SemaphoreType.DMA, pltpu.SemaphoreType.DMA((2,))]),
        out_shape=jax.ShapeDtypeStruct((n_rows, D_MODEL), F32),
        compiler_params=_cparams("arbitrary"),
        name="moe_dispatch",
    )(cnt, off, base, zero_starts, loc, hn)


def _expert_body(be_ref, nu_ref, x_ref, wgu_ref, bgu_ref, wdn_ref, bdn_ref, y_ref):
    i = pl.program_id(0)

    @pl.when(i < nu_ref[0])
    def _():
        gu = _dot(x_ref[...].astype(BF16), wgu_ref[...]) + bgu_ref[...]
        gate = jnp.minimum(gu[:, :D_FF], SWIGLU_LIMIT)
        up = jnp.clip(gu[:, D_FF:], -SWIGLU_LIMIT, SWIGLU_LIMIT)
        act = (up + 1.0) * (gate * _sigmoid(SWIGLU_ALPHA * gate))
        y_ref[...] = _dot(act.astype(BF16), wdn_ref[...]) + bdn_ref[...]

    @pl.when(i >= nu_ref[0])
    def _():
        y_ref[...] = jnp.zeros_like(y_ref)


def _experts(block_e, n_used, xs, wgu, bgu, wdn, bdn, tm):
    n_rows = xs.shape[0]
    return pl.pallas_call(
        _expert_body,
        grid_spec=pltpu.PrefetchScalarGridSpec(
            num_scalar_prefetch=2, grid=(n_rows // tm,),
            in_specs=[pl.BlockSpec((tm, D_MODEL), lambda i, be, nu: (jnp.minimum(i, nu[0] - 1), 0)),
                      pl.BlockSpec((None, D_MODEL, 2 * D_FF), lambda i, be, nu: (be[i], 0, 0)),
                      pl.BlockSpec((None, 1, 2 * D_FF), lambda i, be, nu: (be[i], 0, 0)),
                      pl.BlockSpec((None, D_FF, D_MODEL), lambda i, be, nu: (be[i], 0, 0)),
                      pl.BlockSpec((None, 1, D_MODEL), lambda i, be, nu: (be[i], 0, 0))],
            out_specs=pl.BlockSpec((tm, D_MODEL), lambda i, be, nu: (i, 0))),
        out_shape=jax.ShapeDtypeStruct((n_rows, D_MODEL), F32),
        compiler_params=_cparams("arbitrary"),
        name="moe_experts",
    )(block_e, n_used, xs, wgu, bgu, wdn, bdn)


def _combine_body(tt, cnt_ref, off_ref, base_ref, loc_ref, g_ref, h_ref, gfin_ref, ys_hbm, y_ref,
                  buf_ref, sems):
    i = pl.program_id(0)
    last = pl.num_programs(0) - 1
    r = _sorted_rows(tt)

    def copies(tile, wait):
        b = tile % 2

        def make_copy(lo, go, size):
            return pltpu.make_async_copy(ys_hbm.at[pl.ds(go, size), :], buf_ref.at[b, pl.ds(lo, size), :],
                                         sems.at[b])

        _run_copies(tt, tile, cnt_ref, off_ref, base_ref, make_copy, wait)

    @pl.when(i == 0)
    def _():
        buf_ref[...] = jnp.zeros_like(buf_ref)
        copies(0, False)

    @pl.when(i < last)
    def _():
        copies(i + 1, False)

    copies(i, True)
    loc = loc_ref[...]
    gates = g_ref[...]
    slot_i = lax.broadcasted_iota(I32, (tt, r), 1)
    gmat = jnp.where(slot_i == loc[:, 0:1], gates[:, 0:1], 0.0)
    for k in range(1, TOP_K):
        gmat = gmat + jnp.where(slot_i == loc[:, k:k + 1], gates[:, k:k + 1], 0.0)
    out = h_ref[...] + _dot(gmat.astype(BF16), buf_ref[i % 2].astype(BF16))
    y_ref[...] = _rms(out, gfin_ref[...])


def _combine(cnt, off, base, loc, gates, h2, norm_final, ys, tt):
    t = h2.shape[0]
    tok = pl.BlockSpec((tt, D_MODEL), lambda i: (i, 0))
    tk = pl.BlockSpec((tt, TOP_K), lambda i: (i, 0))
    smem = pl.BlockSpec(memory_space=pltpu.SMEM)
    return pl.pallas_call(
        functools.partial(_combine_body, tt),
        grid_spec=pltpu.PrefetchScalarGridSpec(
            num_scalar_prefetch=0, grid=(t // tt,),
            in_specs=[smem, smem, smem, tk, tk, tok,
                      pl.BlockSpec((1, D_MODEL), lambda i: (0, 0)),
                      pl.BlockSpec(memory_space=pl.ANY)],
            out_specs=tok,
            scratch_shapes=[pltpu.VMEM((2, _sorted_rows(tt), D_MODEL), F32), pltpu.SemaphoreType.DMA((2,))]),
        out_shape=jax.ShapeDtypeStruct((t, D_MODEL), F32),
        compiler_params=_cparams("arbitrary"),
        name="moe_combine",
    )(cnt, off, base, loc, gates, h2, norm_final, ys)


def _moe_and_final_norm(hn, logits, h2, w, tt, tm):
    t = hn.shape[0]
    gates, loc, cnt3, off3 = _route(logits, tt)
    cnt = cnt3[:, 0, :]
    counts = jnp.sum(cnt, axis=0)
    padded = (counts + tm - 1) // tm * tm
    pad_end = jnp.cumsum(padded)
    start = pad_end - padded
    off = off3[:, 0, :]
    base = (start[None, :] + jnp.cumsum(cnt, axis=0) - cnt).astype(I32)
    n_blocks = (t * TOP_K + (t // tt) * N_EXPERTS * (SUBLANES - 1) + N_EXPERTS * (tm - 1)) // tm
    n_rows = n_blocks * tm
    block_start = jnp.arange(n_blocks, dtype=I32) * tm
    block_e = jnp.minimum(jnp.sum(block_start[:, None] >= pad_end[None, :], axis=-1), N_EXPERTS - 1).astype(I32)
    n_used = (pad_end[-1:] // tm).astype(I32)
    zero_starts = jnp.concatenate([jnp.where(padded > 0, pad_end - tm, -1),
                                   jnp.where(block_start >= pad_end[-1], block_start, -1)]).astype(I32)
    xs = _dispatch(cnt, off, base, zero_starts, loc, hn, n_rows, tm, tt)
    ys = _experts(block_e, n_used, xs, w["w_gate_up"], w["b_gate_up"], w["w_down"], w["b_down"], tm)
    return _combine(cnt, off, base, loc, gates, h2, w["norm_final"], ys, tt)


def kernel(x_prompt, x_sample, mem_prompt, state_ssm, state_mamba_conv, state_short_conv, cache_mem_k, cache_mem_v, norm_mix, w_in, w_mconv, b_mconv, dt_bias, a_log, d_skip, norm_ssm, w_sconv, w_out, norm_xattn, norm_mem, w_xq, w_xk, w_xv, w_xo, norm_moe, w_router, b_router, w_gate_up, b_gate_up, w_down, b_down, norm_final):
    nbp, seq, _ = x_prompt.shape
    nbs = x_sample.shape[0]
    dt_lo = SSM_INNER + SSM_CONV_DIM
    w_in0 = w_in[0]
    w_dt = w_in0[:, dt_lo:dt_lo + SSM_HEADS]
    w = {
        "norm_mix": norm_mix, "norm_ssm": norm_ssm, "norm_xattn": norm_xattn, "norm_moe": norm_moe,
        "norm_final": norm_final.reshape(1, D_MODEL),
        "w_a": w_in0[:, :dt_lo].astype(BF16),
        "w_dt": w_dt.astype(BF16), "w_dt_t": w_dt.T.astype(BF16),
        "w_b": w_in0[:, dt_lo + SSM_HEADS:].astype(BF16),
        "w_mconv": w_mconv[0], "b_mconv": b_mconv,
        "dt_bias": dt_bias, "dt_bias_t": dt_bias.reshape(SSM_HEADS, 1),
        "a_log": a_log, "a_log_t": a_log.reshape(SSM_HEADS, 1),
        "d_skip": jnp.repeat(d_skip, SSM_HEAD_DIM, axis=1),
        "w_sconv": w_sconv[0], "w_out": w_out[0].astype(BF16),
        "w_xq": w_xq[0].astype(BF16), "w_xo": w_xo[0].astype(BF16),
        "w_router": w_router[0].astype(BF16), "b_router": b_router,
        "w_gate_up": w_gate_up[0].astype(BF16), "b_gate_up": b_gate_up[0].reshape(N_EXPERTS, 1, 2 * D_FF),
        "w_down": w_down[0].astype(BF16), "b_down": b_down[0].reshape(N_EXPERTS, 1, D_MODEL),
    }

    k_p, v_p, kb, vb = _mem_kv(mem_prompt.reshape(nbp * N_MEM, D_MODEL), norm_mem,
                               w_xk[0].astype(BF16), w_xv[0].astype(BF16))
    h1, ssm_p, mconv_p, sconv_p = _prompt_mixer(x_prompt.reshape(nbp * seq, D_MODEL), nbp, w)
    h2, hn, logits = _prompt_attn(h1, kb, vb, nbp, w)
    y_prompt = _moe_and_final_norm(hn, logits, h2, w, MIX_TILE, MOE_ROW_TILE)

    xs2 = x_sample.reshape(nbs, D_MODEL)
    mstate_t = jnp.transpose(state_mamba_conv[0], (1, 0, 2))
    sstate_t = jnp.transpose(state_short_conv[0], (1, 0, 2))
    z, xs_, dtx, dec, bm, cm, yb, sga, mnew_t, snew_t = _sample_proj(xs2, mstate_t, sstate_t, w)
    ssm_s, y_s = _sample_state(dec, state_ssm[0].reshape(nbs, SSM_INNER, SSM_STATE), dtx, bm, cm)
    h1s, q_s = _sample_fin1(xs2, y_s, xs_, z, yb, sga, w)
    o_s = _sample_attn(q_s.reshape(nbs, 1, D_MODEL),
                       cache_mem_k[0], cache_mem_v[0])
    h2s, hns, logits_s = _sample_fin2(h1s, o_s.reshape(nbs, D_MODEL), w)
    y_sample = _moe_and_final_norm(hns, logits_s, h2s, w, nbs, LANES)

    return (y_prompt.reshape(nbp, seq, D_MODEL),
            y_sample.reshape(nbs, 1, D_MODEL),
            ssm_p.reshape(1, nbp, SSM_HEADS, SSM_HEAD_DIM, SSM_STATE),
            mconv_p[None], sconv_p[None],
            k_p.reshape(1, nbp, N_MEM, XA_HEADS, XA_HEAD_DIM),
            v_p.reshape(1, nbp, N_MEM, XA_HEADS, XA_HEAD_DIM),
            ssm_s.reshape(1, nbs, SSM_HEADS, SSM_HEAD_DIM, SSM_STATE),
            jnp.transpose(mnew_t, (1, 0, 2))[None],
            jnp.transpose(snew_t, (1, 0, 2))[None])
```

```python
import functools

import jax
import jax.numpy as jnp
from jax import lax
from jax.experimental import pallas as pl
from jax.experimental.pallas import tpu as pltpu

F32 = jnp.float32
BF16 = jnp.bfloat16
I32 = jnp.int32

D_MODEL = 1024
N_MEM = 256
SSM_HEADS = 16
SSM_HEAD_DIM = 64
SSM_INNER = SSM_HEADS * SSM_HEAD_DIM
SSM_STATE = 128
SSM_GROUPS = 4
HEADS_PER_GROUP = SSM_HEADS // SSM_GROUPS
GROUP_WIDTH = SSM_INNER // SSM_GROUPS
SSM_CONV = 4
SSM_CONV_DIM = SSM_INNER + 2 * SSM_GROUPS * SSM_STATE
SC_CONV = 3
XA_HEADS = 4
XA_HEAD_DIM = D_MODEL // XA_HEADS
N_EXPERTS = 32
TOP_K = 4
D_FF = D_MODEL
SWIGLU_LIMIT = 7.0
SWIGLU_ALPHA = 1.702
EPS = 1e-6

LANES = 128
SUBLANES = 8
VMEM_LIMIT = 56 * 1024 * 1024

MIX_TILE = 256
MOE_ROW_TILE = 512
STATE_BB = 8
ATTN_BB = 4

NT_DIMS = (((1,), (1,)), ((), ()))
TN_DIMS = (((0,), (0,)), ((), ()))


def _cparams(*sem):
    return pltpu.CompilerParams(dimension_semantics=sem, vmem_limit_bytes=VMEM_LIMIT)


def _const_spec(shape):
    nd = len(shape)
    return pl.BlockSpec(shape, lambda *_: (0,) * nd, pipeline_mode=pl.Buffered(1))


def _sigmoid(x):
    return 1.0 / (1.0 + jnp.exp(-x))


def _silu(x):
    return x * _sigmoid(x)


def _softplus(x):
    return jnp.maximum(x, 0.0) + jnp.log(1.0 + jnp.exp(-jnp.abs(x)))


def _rms(x, g):
    ms = jnp.mean(x * x, axis=-1, keepdims=True)
    return x * lax.rsqrt(ms + EPS) * g


def _dot(a, b):
    return jnp.dot(a, b, preferred_element_type=F32)


def _dot_nt(a, b):
    return lax.dot_general(a, b, NT_DIMS, preferred_element_type=F32)


def _expand_heads(v):
    rows = v.shape[0]
    lane = lax.broadcasted_iota(I32, (rows, LANES), 1)
    pieces = []
    for j in range(SSM_HEADS // 2):
        a = jnp.broadcast_to(v[:, 2 * j:2 * j + 1], (rows, LANES))
        b = jnp.broadcast_to(v[:, 2 * j + 1:2 * j + 2], (rows, LANES))
        pieces.append(jnp.where(lane < SSM_HEAD_DIM, a, b))
    return jnp.concatenate(pieces, axis=1)


def _pad_rows(x, rows):
    return jnp.concatenate([x, jnp.zeros((rows - x.shape[0], x.shape[1]), x.dtype)], axis=0)


def _group_rmsnorm(u, g):
    outs = []
    for k in range(SSM_GROUPS):
        ug = u[:, k * GROUP_WIDTH:(k + 1) * GROUP_WIDTH]
        ms = jnp.mean(ug * ug, axis=-1, keepdims=True)
        outs.append(ug * lax.rsqrt(ms + EPS))
    return jnp.concatenate(outs, axis=1) * g


def _memkv_body(mem_ref, g_ref, wk_ref, wv_ref, k_ref, v_ref, kb_ref, vb_ref):
    mn = _rms(mem_ref[...], g_ref[...]).astype(BF16)
    k = _dot(mn, wk_ref[...])
    v = _dot(mn, wv_ref[...])
    k_ref[...] = k
    v_ref[...] = v
    kb_ref[...] = k.astype(BF16)
    vb_ref[...] = v.astype(BF16)


def _mem_kv(mem2d, norm_mem, wk, wv):
    rows = mem2d.shape[0]
    nb = rows // N_MEM
    blk = pl.BlockSpec((N_MEM, D_MODEL), lambda b: (b, 0))
    return pl.pallas_call(
        _memkv_body,
        grid=(nb,),
        in_specs=[blk, _const_spec((1, D_MODEL)), _const_spec((D_MODEL, D_MODEL)),
                  _const_spec((D_MODEL, D_MODEL))],
        out_specs=[blk, blk, blk, blk],
        out_shape=[jax.ShapeDtypeStruct((rows, D_MODEL), F32)] * 2
        + [jax.ShapeDtypeStruct((rows, D_MODEL), BF16)] * 2,
        compiler_params=_cparams("arbitrary"),
        name="mem_kv",
    )(mem2d, norm_mem, wk, wv)


def _mix_body(x_ref, gmix_ref, wa_ref, wdtc_ref, wdtr_ref, wb_ref, wmc_ref, bmc_ref,
              dtb_ref, dtbt_ref, alog_ref, alogt_ref, dskip_ref, gssm_ref, wsc_ref, wout_ref,
              h_ref, ssm_ref, mbuf_ref, sbuf_ref,
              st_ref, cbuf_ref, scbuf_ref):
    tq = MIX_TILE
    c = pl.program_id(1)

    @pl.when(c == 0)
    def _():
        st_ref[...] = jnp.zeros_like(st_ref)
        cbuf_ref[0:SUBLANES, :] = jnp.zeros((SUBLANES, SSM_CONV_DIM), F32)
        scbuf_ref[0:SUBLANES, :] = jnp.zeros((SUBLANES, D_MODEL), F32)

    x = x_ref[...]
    xn = _rms(x, gmix_ref[...]).astype(BF16)

    pa = _dot(xn, wa_ref[...])
    z = pa[:, :SSM_INNER]
    u = pa[:, SSM_INNER:]
    cbuf_ref[SUBLANES:SUBLANES + tq, :] = u
    wm = wmc_ref[...]
    conv = u * wm[SSM_CONV - 1:SSM_CONV, :] + bmc_ref[...]
    for k in range(SSM_CONV - 1):
        off = SUBLANES - (SSM_CONV - 1) + k
        conv = conv + cbuf_ref[off:off + tq, :] * wm[k:k + 1, :]
    tail = cbuf_ref[tq + SUBLANES - (SSM_CONV - 1):tq + SUBLANES, :]
    mbuf_ref[...] = tail
    cbuf_ref[SUBLANES - (SSM_CONV - 1):SUBLANES, :] = tail
    xbc = _silu(conv)
    xs = xbc[:, :SSM_INNER]
    bm = xbc[:, SSM_INNER:SSM_INNER + SSM_GROUPS * SSM_STATE]
    cm = xbc[:, SSM_INNER + SSM_GROUPS * SSM_STATE:]

    dt = _softplus(_dot(xn, wdtc_ref[...]) + dtb_ref[...])
    dtt = _softplus(_dot_nt(wdtr_ref[...], xn) + dtbt_ref[...])
    a_row = -jnp.exp(alog_ref[...])
    a_col = -jnp.exp(alogt_ref[...])
    row_i = lax.broadcasted_iota(I32, (tq, tq), 0)
    col_i = lax.broadcasted_iota(I32, (tq, tq), 1)
    causal = row_i >= col_i
    tril = causal.astype(F32)
    triu = (row_i <= col_i).astype(F32)
    a_cum = jnp.dot(tril, dt * a_row, precision=lax.Precision.HIGHEST,
                    preferred_element_type=F32)
    a_cumt = jnp.dot(dtt * a_col, triu, precision=lax.Precision.HIGHEST,
                     preferred_element_type=F32)
    a_last = a_cum[tq - 1:tq, :]

    xdt = xs * _expand_heads(dt)
    in_decay = _expand_heads(jnp.exp(a_cum))
    to_end = _expand_heads(jnp.exp(a_last - a_cum))
    chunk_decay = _expand_heads(jnp.exp(a_last))
    xdt_b = xdt.astype(BF16)
    xend_b = (xdt * to_end).astype(BF16)
    lane = lax.broadcasted_iota(I32, (tq, LANES), 1)

    y_groups = []
    for g in range(SSM_GROUPS):
        cg = cm[:, g * SSM_STATE:(g + 1) * SSM_STATE].astype(BF16)
        bg_f = bm[:, g * SSM_STATE:(g + 1) * SSM_STATE]
        bg = bg_f.astype(BF16)
        scores = _dot_nt(cg, bg)
        gs = slice(g * GROUP_WIDTH, (g + 1) * GROUP_WIDTH)
        st_g = st_ref[:, gs]
        y_off = _dot(cg, st_g.astype(BF16)) * in_decay[:, gs]
        pair_out = []
        for pr in range(HEADS_PER_GROUP // 2):
            h0 = g * HEADS_PER_GROUP + 2 * pr
            xp = xdt_b[:, h0 * SSM_HEAD_DIM:(h0 + 2) * SSM_HEAD_DIM]
            ys = []
            for h in (h0, h0 + 1):
                seg = a_cum[:, h:h + 1] - a_cumt[h:h + 1, :]
                decay = jnp.where(causal, jnp.exp(jnp.minimum(seg, 0.0)), 0.0)
                ys.append(_dot((scores * decay).astype(BF16), xp))
            pair_out.append(jnp.where(lane < SSM_HEAD_DIM, ys[0], ys[1]))
        y_groups.append(jnp.concatenate(pair_out, axis=1) + y_off)
        st_ref[:, gs] = st_g * chunk_decay[:, gs] + _dot(bg_f.T.astype(BF16), xend_b[:, gs])
    y = jnp.concatenate(y_groups, axis=1) + dskip_ref[...] * xs
    y_a = _group_rmsnorm(y * _silu(z), gssm_ref[...])

    @pl.when(c == pl.num_programs(1) - 1)
    def _():
        ssm_ref[...] = st_ref[...].T

    pb = _dot(xn, wb_ref[...])
    sc_b = pb[:, 0:D_MODEL]
    cv = pb[:, D_MODEL:2 * D_MODEL] * pb[:, 2 * D_MODEL:3 * D_MODEL]
    g_a = pb[:, 3 * D_MODEL:4 * D_MODEL]
    g_b = pb[:, 4 * D_MODEL:5 * D_MODEL]
    scbuf_ref[SUBLANES:SUBLANES + tq, :] = cv
    ws = wsc_ref[...]
    uc = cv * ws[SC_CONV - 1:SC_CONV, :]
    for k in range(SC_CONV - 1):
        off = SUBLANES - (SC_CONV - 1) + k
        uc = uc + scbuf_ref[off:off + tq, :] * ws[k:k + 1, :]
    stail = scbuf_ref[tq + SUBLANES - (SC_CONV - 1):tq + SUBLANES, :]
    sbuf_ref[...] = stail
    scbuf_ref[SUBLANES - (SC_CONV - 1):SUBLANES, :] = stail
    merged = _sigmoid(g_a) * y_a + _sigmoid(g_b) * (sc_b * uc)
    h_ref[...] = x + _dot(merged.astype(BF16), wout_ref[...])


def _prompt_mixer(x2d, nb, w):
    t = x2d.shape[0]
    nc = t // nb // MIX_TILE
    tok = pl.BlockSpec((MIX_TILE, D_MODEL), lambda b, c: (b * nc + c, 0))
    return pl.pallas_call(
        _mix_body,
        grid=(nb, nc),
        in_specs=[tok, _const_spec((1, D_MODEL)),
                  _const_spec((D_MODEL, SSM_INNER + SSM_CONV_DIM)),
                  _const_spec((D_MODEL, SSM_HEADS)), _const_spec((SSM_HEADS, D_MODEL)),
                  _const_spec((D_MODEL, 5 * D_MODEL)),
                  _const_spec((SSM_CONV, SSM_CONV_DIM)), _const_spec((1, SSM_CONV_DIM)),
                  _const_spec((1, SSM_HEADS)), _const_spec((SSM_HEADS, 1)),
                  _const_spec((1, SSM_HEADS)), _const_spec((SSM_HEADS, 1)),
                  _const_spec((1, SSM_INNER)), _const_spec((1, SSM_INNER)),
                  _const_spec((SC_CONV, D_MODEL)), _const_spec((D_MODEL, D_MODEL))],
        out_specs=[tok,
                   pl.BlockSpec((None, SSM_INNER, SSM_STATE), lambda b, c: (b, 0, 0)),
                   pl.BlockSpec((None, SSM_CONV - 1, SSM_CONV_DIM), lambda b, c: (b, 0, 0)),
                   pl.BlockSpec((None, SC_CONV - 1, D_MODEL), lambda b, c: (b, 0, 0))],
        out_shape=[jax.ShapeDtypeStruct((t, D_MODEL), F32),
                   jax.ShapeDtypeStruct((nb, SSM_INNER, SSM_STATE), F32),
                   jax.ShapeDtypeStruct((nb, SSM_CONV - 1, SSM_CONV_DIM), F32),
                   jax.ShapeDtypeStruct((nb, SC_CONV - 1, D_MODEL), F32)],
        scratch_shapes=[pltpu.VMEM((SSM_STATE, SSM_INNER), F32),
                        pltpu.VMEM((MIX_TILE + SUBLANES, SSM_CONV_DIM), F32),
                        pltpu.VMEM((MIX_TILE + SUBLANES, D_MODEL), F32)],
        compiler_params=_cparams("arbitrary", "arbitrary"),
        name="prompt_mixer",
    )(x2d, w["norm_mix"], w["w_a"], w["w_dt"], w["w_dt_t"], w["w_b"], w["w_mconv"], w["b_mconv"],
      w["dt_bias"], w["dt_bias_t"], w["a_log"], w["a_log_t"], w["d_skip"], w["norm_ssm"],
      w["w_sconv"], w["w_out"])


def _router_tail(h2, gmoe_ref, wr_ref, br_ref, h2_ref, hn_ref, lg_ref):
    h2_ref[...] = h2
    hn = _rms(h2, gmoe_ref[...])
    hn_ref[...] = hn
    lg_ref[...] = _dot(hn.astype(BF16), wr_ref[...]) + br_ref[...]


def _attn_body(h_ref, gx_ref, wq_ref, k_ref, v_ref, wo_ref, gmoe_ref, wr_ref, br_ref,
               h2_ref, hn_ref, lg_ref):
    h = h_ref[...]
    hn = _rms(h, gx_ref[...]).astype(BF16)
    q = _dot(hn, wq_ref[...]).astype(BF16)
    outs = []
    for hd in range(XA_HEADS):
        sl = slice(hd * XA_HEAD_DIM, (hd + 1) * XA_HEAD_DIM)
        s = _dot_nt(q[:, sl], k_ref[:, sl]) * (XA_HEAD_DIM ** -0.5)
        e = jnp.exp(s - jnp.max(s, axis=-1, keepdims=True))
        p = e / jnp.sum(e, axis=-1, keepdims=True)
        outs.append(_dot(p.astype(BF16), v_ref[:, sl]))
    o = jnp.concatenate(outs, axis=1).astype(BF16)
    h2 = h + _dot(o, wo_ref[...])
    _router_tail(h2, gmoe_ref, wr_ref, br_ref, h2_ref, hn_ref, lg_ref)


def _prompt_attn(h1, kb, vb, nb, w):
    t = h1.shape[0]
    nc = t // nb // MIX_TILE
    tok = pl.BlockSpec((MIX_TILE, D_MODEL), lambda b, c: (b * nc + c, 0))
    kv = pl.BlockSpec((N_MEM, D_MODEL), lambda b, c: (b, 0))
    return pl.pallas_call(
        _attn_body,
        grid=(nb, nc),
        in_specs=[tok, _const_spec((1, D_MODEL)), _const_spec((D_MODEL, D_MODEL)), kv, kv,
                  _const_spec((D_MODEL, D_MODEL)), _const_spec((1, D_MODEL)),
                  _const_spec((D_MODEL, N_EXPERTS)), _const_spec((1, N_EXPERTS))],
        out_specs=[tok, tok, pl.BlockSpec((MIX_TILE, N_EXPERTS), lambda b, c: (b * nc + c, 0))],
        out_shape=[jax.ShapeDtypeStruct((t, D_MODEL), F32),
                   jax.ShapeDtypeStruct((t, D_MODEL), F32),
                   jax.ShapeDtypeStruct((t, N_EXPERTS), F32)],
        compiler_params=_cparams("arbitrary", "arbitrary"),
        name="prompt_attn",
    )(h1, w["norm_xattn"], w["w_xq"], kb, vb, w["w_xo"], w["norm_moe"], w["w_router"], w["b_router"])


def _sproj_body(x_ref, gmix_ref, wa_ref, wdtc_ref, wb_ref, wmc_ref, bmc_ref, dtb_ref, alog_ref,
                wsc_ref, mst_ref, sst_ref,
                z_ref, xs_ref, dtx_ref, dec_ref, bm_ref, cm_ref, yb_ref, sga_ref, mnew_ref, snew_ref):
    x = x_ref[...]
    xn = _rms(x, gmix_ref[...]).astype(BF16)
    pa = _dot(xn, wa_ref[...])
    z_ref[...] = pa[:, :SSM_INNER]
    u = pa[:, SSM_INNER:]
    wm = wmc_ref[...]
    conv = u * wm[SSM_CONV - 1:SSM_CONV, :] + bmc_ref[...]
    for k in range(SSM_CONV - 1):
        conv = conv + mst_ref[k] * wm[k:k + 1, :]
    for k in range(SSM_CONV - 2):
        mnew_ref[k] = mst_ref[k + 1]
    mnew_ref[SSM_CONV - 2] = u
    xbc = _silu(conv)
    xs = xbc[:, :SSM_INNER]
    xs_ref[...] = xs
    bm_ref[...] = xbc[:, SSM_INNER:SSM_INNER + SSM_GROUPS * SSM_STATE]
    cm_ref[...] = xbc[:, SSM_INNER + SSM_GROUPS * SSM_STATE:]
    dt = _softplus(_dot(xn, wdtc_ref[...]) + dtb_ref[...])
    dec_ref[...] = jnp.exp(dt * (-jnp.exp(alog_ref[...])))
    dtx_ref[...] = xs * _expand_heads(dt)
    pb = _dot(xn, wb_ref[...])
    cv = pb[:, D_MODEL:2 * D_MODEL] * pb[:, 2 * D_MODEL:3 * D_MODEL]
    ws = wsc_ref[...]
    uc = cv * ws[SC_CONV - 1:SC_CONV, :]
    for k in range(SC_CONV - 1):
        uc = uc + sst_ref[k] * ws[k:k + 1, :]
    for k in range(SC_CONV - 2):
        snew_ref[k] = sst_ref[k + 1]
    snew_ref[SC_CONV - 2] = cv
    yb_ref[...] = _sigmoid(pb[:, 4 * D_MODEL:5 * D_MODEL]) * (pb[:, 0:D_MODEL] * uc)
    sga_ref[...] = _sigmoid(pb[:, 3 * D_MODEL:4 * D_MODEL])


def _sample_proj(x, mstate_t, sstate_t, w):
    nb = x.shape[0]
    f = lambda *s: jax.ShapeDtypeStruct(s, F32)
    return pl.pallas_call(
        _sproj_body,
        out_shape=[f(nb, SSM_INNER), f(nb, SSM_INNER), f(nb, SSM_INNER), f(nb, SSM_HEADS),
                   f(nb, SSM_GROUPS * SSM_STATE), f(nb, SSM_GROUPS * SSM_STATE),
                   f(nb, D_MODEL), f(nb, D_MODEL),
                   f(SSM_CONV - 1, nb, SSM_CONV_DIM), f(SC_CONV - 1, nb, D_MODEL)],
        compiler_params=pltpu.CompilerParams(vmem_limit_bytes=VMEM_LIMIT),
        name="sample_proj",
    )(x, w["norm_mix"], w["w_a"], w["w_dt"], w["w_b"], w["w_mconv"], w["b_mconv"], w["dt_bias"],
      w["a_log"], w["w_sconv"], mstate_t, sstate_t)


def _sstate_body(dec_ref, s_ref, dtx_ref, bm_ref, cm_ref, snew_ref, y_ref):
    i = pl.program_id(0)
    rows_per_blk = LANES
    for j in range(STATE_BB):
        b = i * STATE_BB + j
        dtx_row = dtx_ref[j:j + 1, :]
        y_parts = []
        for g in range(SSM_GROUPS):
            b_row = bm_ref[j:j + 1, g * SSM_STATE:(g + 1) * SSM_STATE]
            c_row = cm_ref[j:j + 1, g * SSM_STATE:(g + 1) * SSM_STATE].astype(BF16)
            new_blocks = []
            for q in range(GROUP_WIDTH // rows_per_blk):
                r0 = g * GROUP_WIDTH + q * rows_per_blk
                dcol = jnp.broadcast_to(dtx_row[:, r0:r0 + rows_per_blk], (rows_per_blk, LANES)).T
                sub = []
                for hh in range(rows_per_blk // SSM_HEAD_DIM):
                    h = r0 // SSM_HEAD_DIM + hh
                    lo = hh * SSM_HEAD_DIM
                    s_old = s_ref[j, r0 + lo:r0 + lo + SSM_HEAD_DIM, :]
                    sub.append(s_old * dec_ref[b, h] + dcol[lo:lo + SSM_HEAD_DIM, :] * b_row)
                blk = jnp.concatenate(sub, axis=0)
                snew_ref[j, r0:r0 + rows_per_blk, :] = blk
                new_blocks.append(blk.astype(BF16))
            s_g = jnp.concatenate(new_blocks, axis=0)
            y_parts.append(_dot_nt(c_row, s_g))
        y_ref[j:j + 1, :] = jnp.concatenate(y_parts, axis=1)


def _sample_state(dec, state, dtx, bm, cm):
    nb = state.shape[0]
    row = lambda wdt: pl.BlockSpec((STATE_BB, wdt), lambda i, dec: (i, 0))
    st = pl.BlockSpec((STATE_BB, SSM_INNER, SSM_STATE), lambda i, dec: (i, 0, 0))
    return pl.pallas_call(
        _sstate_body,
        grid_spec=pltpu.PrefetchScalarGridSpec(
            num_scalar_prefetch=1, grid=(nb // STATE_BB,),
            in_specs=[st, row(SSM_INNER), row(SSM_GROUPS * SSM_STATE), row(SSM_GROUPS * SSM_STATE)],
            out_specs=[st, row(SSM_INNER)]),
        out_shape=[jax.ShapeDtypeStruct(state.shape, F32), jax.ShapeDtypeStruct((nb, SSM_INNER), F32)],
        compiler_params=_cparams("arbitrary"),
        name="sample_state",
    )(dec, state, dtx, bm, cm)


def _sfin1_body(x_ref, y_ref, xs_ref, z_ref, yb_ref, sga_ref, dskip_ref, gssm_ref, wout_ref,
                gx_ref, wq_ref, h_ref, q_ref):
    y = y_ref[...] + dskip_ref[...] * xs_ref[...]
    y_a = _group_rmsnorm(y * _silu(z_ref[...]), gssm_ref[...])
    merged = sga_ref[...] * y_a + yb_ref[...]
    h = x_ref[...] + _dot(merged.astype(BF16), wout_ref[...])
    h_ref[...] = h
    q_ref[...] = _dot(_rms(h, gx_ref[...]).astype(BF16), wq_ref[...])


def _sample_fin1(x, y, xs, z, yb, sga, w):
    nb = x.shape[0]
    return pl.pallas_call(
        _sfin1_body,
        out_shape=[jax.ShapeDtypeStruct((nb, D_MODEL), F32)] * 2,
        compiler_params=pltpu.CompilerParams(vmem_limit_bytes=VMEM_LIMIT),
        name="sample_fin1",
    )(x, y, xs, z, yb, sga, w["d_skip"], w["norm_ssm"], w["w_out"], w["norm_xattn"], w["w_xq"])


def _sattn_body(q_ref, k_ref, v_ref, o_ref):
    for j in range(ATTN_BB):
        q_row = q_ref[j]
        q4 = jnp.concatenate([q_row[:, h * XA_HEAD_DIM:(h + 1) * XA_HEAD_DIM]
                              for h in range(XA_HEADS)], axis=0)
        s = jnp.sum(k_ref[j] * q4[None], axis=-1, keepdims=True) * (XA_HEAD_DIM ** -0.5)
        e = jnp.exp(s - jnp.max(s, axis=0, keepdims=True))
        p = e / jnp.sum(e, axis=0, keepdims=True)
        o4 = jnp.sum(p * v_ref[j], axis=0)
        o_ref[j] = jnp.concatenate([o4[h:h + 1, :] for h in range(XA_HEADS)], axis=1)


def _sample_attn(q3, k3, v3):
    nb = q3.shape[0]
    qs = pl.BlockSpec((ATTN_BB, 1, D_MODEL), lambda i: (i, 0, 0))
    kv = pl.BlockSpec((ATTN_BB, N_MEM, XA_HEADS, XA_HEAD_DIM), lambda i: (i, 0, 0, 0))
    return pl.pallas_call(
        _sattn_body,
        grid=(nb // ATTN_BB,),
        in_specs=[qs, kv, kv],
        out_specs=qs,
        out_shape=jax.ShapeDtypeStruct((nb, 1, D_MODEL), F32),
        compiler_params=_cparams("arbitrary"),
        name="sample_attn",
    )(q3, k3, v3)


def _sfin2_body(h_ref, o_ref, wo_ref, gmoe_ref, wr_ref, br_ref, h2_ref, hn_ref, lg_ref):
    h2 = h_ref[...] + _dot(o_ref[...].astype(BF16), wo_ref[...])
    _router_tail(h2, gmoe_ref, wr_ref, br_ref, h2_ref, hn_ref, lg_ref)


def _sample_fin2(h1, o, w):
    nb = h1.shape[0]
    return pl.pallas_call(
        _sfin2_body,
        out_shape=[jax.ShapeDtypeStruct((nb, D_MODEL), F32)] * 2
        + [jax.ShapeDtypeStruct((nb, N_EXPERTS), F32)],
        compiler_params=pltpu.CompilerParams(vmem_limit_bytes=VMEM_LIMIT),
        name="sample_fin2",
    )(h1, o, w["w_xo"], w["norm_moe"], w["w_router"], w["b_router"])


def _route_body(lgp_ref, lgs_ref, g_ref, loc_ref, cnt_ref, off_ref):
    tt = lgp_ref.shape[0]
    is_sample = pl.program_id(0) == pl.num_programs(0) - 1
    row = lax.broadcasted_iota(I32, (tt, 1), 0)
    valid = jnp.logical_or(jnp.logical_not(is_sample), row < lgs_ref.shape[0])
    work = jnp.where(is_sample, _pad_rows(lgs_ref[...], tt), lgp_ref[...])
    lane = lax.broadcasted_iota(I32, (tt, N_EXPERTS), 1).astype(F32)
    vals, hots = [], []
    for _ in range(TOP_K):
        m = jnp.max(work, axis=-1, keepdims=True)
        idx = jnp.min(jnp.where(work == m, lane, float(N_EXPERTS)), axis=-1, keepdims=True)
        hot = (lane == idx) & valid
        vals.append(m)
        hots.append(hot)
        work = jnp.where(hot, -jnp.inf, work)
    exps = [jnp.exp(v - vals[0]) for v in vals]
    tot = exps[0]
    for e in exps[1:]:
        tot = tot + e
    assigned = hots[0]
    for hot in hots[1:]:
        assigned = assigned | hot
    a = assigned.astype(BF16)
    r_i = lax.broadcasted_iota(I32, (tt, tt), 0)
    c_i = lax.broadcasted_iota(I32, (tt, tt), 1)
    rank = _dot((r_i > c_i).astype(BF16), a)
    e_r = lax.broadcasted_iota(I32, (N_EXPERTS, N_EXPERTS), 0)
    e_c = lax.broadcasted_iota(I32, (N_EXPERTS, N_EXPERTS), 1)
    cnt = jnp.sum(a.astype(F32), axis=0, keepdims=True)
    cnt = jnp.floor((cnt + (SUBLANES - 1)) * (1.0 / SUBLANES)) * SUBLANES
    cnt_rows = jnp.broadcast_to(cnt, (SUBLANES, N_EXPERTS)).astype(BF16)
    off = _dot(cnt_rows, (e_r < e_c).astype(BF16))[0:1, :]
    slot = rank + off
    k_lane = lax.broadcasted_iota(I32, (tt, TOP_K), 1)
    g_out = jnp.zeros((tt, TOP_K), F32)
    l_out = jnp.zeros((tt, TOP_K), F32)
    for k in range(TOP_K):
        lk = jnp.sum(jnp.where(hots[k], slot, 0.0), axis=-1, keepdims=True)
        g_out = jnp.where(k_lane == k, exps[k] / tot, g_out)
        l_out = jnp.where(k_lane == k, lk, l_out)
    g_ref[...] = jnp.where(valid, g_out, 0.0)
    loc_ref[...] = jnp.where(valid, l_out, -1.0).astype(I32)
    cnt_ref[...] = cnt.astype(I32)
    off_ref[...] = off.astype(I32)


def _route(logits_p, logits_s, tt):
    ntp = logits_p.shape[0] // tt
    nt = ntp + 1
    t = nt * tt
    tk = pl.BlockSpec((tt, TOP_K), lambda i: (i, 0))
    per_tile = pl.BlockSpec((None, 1, N_EXPERTS), lambda i: (i, 0, 0))
    return pl.pallas_call(
        _route_body,
        grid=(nt,),
        in_specs=[pl.BlockSpec((tt, N_EXPERTS), lambda i: (jnp.minimum(i, ntp - 1), 0)),
                  pl.BlockSpec(logits_s.shape, lambda i: (0, 0))],
        out_specs=[tk, tk, per_tile, per_tile],
        out_shape=[jax.ShapeDtypeStruct((t, TOP_K), F32), jax.ShapeDtypeStruct((t, TOP_K), I32),
                   jax.ShapeDtypeStruct((nt, 1, N_EXPERTS), I32), jax.ShapeDtypeStruct((nt, 1, N_EXPERTS), I32)],
        compiler_params=_cparams("arbitrary"),
        name="moe_route",
    )(logits_p, logits_s)


def _sorted_rows(tt):
    return tt * TOP_K + N_EXPERTS * SUBLANES


def _run_copies(tt, tile, cnt_ref, off_ref, base_ref, make_copy, wait):
    if wait:
        total = off_ref[tile, N_EXPERTS - 1] + cnt_ref[tile, N_EXPERTS - 1]

        @pl.when(total > 0)
        def _():
            make_copy(0, 0, pl.multiple_of(total, SUBLANES)).wait()
        return

    def per_expert(e, carry):
        n = cnt_ref[tile, e]

        @pl.when(n > 0)
        def _():
            make_copy(pl.multiple_of(off_ref[tile, e], SUBLANES),
                      pl.multiple_of(base_ref[tile, e], SUBLANES), pl.multiple_of(n, SUBLANES)).start()
        return carry

    lax.fori_loop(0, N_EXPERTS, per_expert, 0)


def _dispatch_body(tm, tt, cnt_ref, off_ref, base_ref, zstart_ref, loc_ref, x_ref, xs_ref, o_hbm,
                   zero_ref, srt_ref, zsem, sems):
    i = pl.program_id(0)
    last = pl.num_programs(0) - 1
    r = _sorted_rows(tt)

    def zero_copy(j):
        return pltpu.make_async_copy(zero_ref, o_hbm.at[pl.ds(pl.multiple_of(zstart_ref[j], tm), tm), :], zsem)

    @pl.when(i == 0)
    def _():
        zero_ref[...] = jnp.zeros_like(zero_ref)

        def start(j, carry):
            @pl.when(zstart_ref[j] >= 0)
            def _():
                zero_copy(j).start()
            return carry

        def wait(j, carry):
            @pl.when(zstart_ref[j] >= 0)
            def _():
                zero_copy(j).wait()
            return carry

        lax.fori_loop(0, zstart_ref.shape[0], start, 0)
        lax.fori_loop(0, zstart_ref.shape[0], wait, 0)

    loc = loc_ref[...]
    slot_i = lax.broadcasted_iota(I32, (tt, r), 1)
    hit = slot_i == loc[:, 0:1]
    for k in range(1, TOP_K):
        hit = hit | (slot_i == loc[:, k:k + 1])
    buf = i % 2
    x = jnp.where(i == last, _pad_rows(xs_ref[...], tt), x_ref[...])
    srt_ref[buf] = lax.dot_general(hit.astype(BF16), x.astype(BF16), TN_DIMS, preferred_element_type=F32)

    def copies(tile, wait):
        b = tile % 2

        def make_copy(lo, go, size):
            return pltpu.make_async_copy(srt_ref.at[b, pl.ds(lo, size), :], o_hbm.at[pl.ds(go, size), :],
                                         sems.at[b])

        _run_copies(tt, tile, cnt_ref, off_ref, base_ref, make_copy, wait)

    copies(i, False)

    @pl.when(i > 0)
    def _():
        copies(i - 1, True)

    @pl.when(i == last)
    def _():
        copies(i, True)


def _dispatch(cnt, off, base, zero_starts, loc, hn_p, hn_s, n_rows, tm, tt):
    ntp = hn_p.shape[0] // tt
    smem = pl.BlockSpec(memory_space=pltpu.SMEM)
    return pl.pallas_call(
        functools.partial(_dispatch_body, tm, tt),
        grid_spec=pltpu.PrefetchScalarGridSpec(
            num_scalar_prefetch=0, grid=(ntp + 1,),
            in_specs=[smem, smem, smem, smem,
                      pl.BlockSpec((tt, TOP_K), lambda i: (i, 0)),
                      pl.BlockSpec((tt, D_MODEL), lambda i: (jnp.minimum(i, ntp - 1), 0)),
                      pl.BlockSpec(hn_s.shape, lambda i: (0, 0))],
            out_specs=pl.BlockSpec(memory_space=pl.ANY),
            scratch_shapes=[pltpu.VMEM((tm, D_MODEL), F32), pltpu.VMEM((2, _sorted_rows(tt), D_MODEL), F32),
                            pltpu.SemaphoreType.DMA, pltpu.SemaphoreType.DMA((2,))]),
        out_shape=jax.ShapeDtypeStruct((n_rows, D_MODEL), F32),
        compiler_params=_cparams("arbitrary"),
        name="moe_dispatch",
    )(cnt, off, base, zero_starts, loc, hn_p, hn_s)


def _expert_body(be_ref, nu_ref, x_ref, wgu_ref, bgu_ref, wdn_ref, bdn_ref, y_ref, wgu_b, wdn_b):
    i = pl.program_id(0)

    @pl.when(jnp.logical_or(i == 0, be_ref[i] != be_ref[jnp.maximum(i - 1, 0)]))
    def _():
        wgu_b[...] = wgu_ref[...].astype(BF16)
        wdn_b[...] = wdn_ref[...].astype(BF16)

    @pl.when(i < nu_ref[0])
    def _():
        gu = _dot(x_ref[...].astype(BF16), wgu_b[...]) + bgu_ref[...]
        gate = jnp.minimum(gu[:, :D_FF], SWIGLU_LIMIT)
        up = jnp.clip(gu[:, D_FF:], -SWIGLU_LIMIT, SWIGLU_LIMIT)
        act = (up + 1.0) * (gate * _sigmoid(SWIGLU_ALPHA * gate))
        y_ref[...] = _dot(act.astype(BF16), wdn_b[...]) + bdn_ref[...]

    @pl.when(i >= nu_ref[0])
    def _():
        y_ref[...] = jnp.zeros_like(y_ref)


def _experts(block_e, n_used, xs, wgu, bgu, wdn, bdn, tm):
    n_rows = xs.shape[0]
    return pl.pallas_call(
        _expert_body,
        grid_spec=pltpu.PrefetchScalarGridSpec(
            num_scalar_prefetch=2, grid=(n_rows // tm,),
            in_specs=[pl.BlockSpec((tm, D_MODEL), lambda i, be, nu: (jnp.minimum(i, nu[0] - 1), 0)),
                      pl.BlockSpec((None, D_MODEL, 2 * D_FF), lambda i, be, nu: (be[i], 0, 0)),
                      pl.BlockSpec((None, 1, 2 * D_FF), lambda i, be, nu: (be[i], 0, 0)),
                      pl.BlockSpec((None, D_FF, D_MODEL), lambda i, be, nu: (be[i], 0, 0)),
                      pl.BlockSpec((None, 1, D_MODEL), lambda i, be, nu: (be[i], 0, 0))],
            out_specs=pl.BlockSpec((tm, D_MODEL), lambda i, be, nu: (i, 0)),
            scratch_shapes=[pltpu.VMEM((D_MODEL, 2 * D_FF), BF16), pltpu.VMEM((D_FF, D_MODEL), BF16)]),
        out_shape=jax.ShapeDtypeStruct((n_rows, D_MODEL), F32),
        compiler_params=_cparams("arbitrary"),
        name="moe_experts",
    )(block_e, n_used, xs, wgu, bgu, wdn, bdn)


def _combine_body(tt, cnt_ref, off_ref, base_ref, loc_ref, g_ref, h_ref, hs_ref, gfin_ref, ys_hbm,
                  y_ref, ysmp_ref, buf_ref, sems):
    i = pl.program_id(0)
    last = pl.num_programs(0) - 1
    r = _sorted_rows(tt)

    def copies(tile, wait):
        b = tile % 2

        def make_copy(lo, go, size):
            return pltpu.make_async_copy(ys_hbm.at[pl.ds(go, size), :], buf_ref.at[b, pl.ds(lo, size), :],
                                         sems.at[b])

        _run_copies(tt, tile, cnt_ref, off_ref, base_ref, make_copy, wait)

    @pl.when(i == 0)
    def _():
        buf_ref[...] = jnp.zeros_like(buf_ref)
        copies(0, False)

    @pl.when(i < last)
    def _():
        copies(i + 1, False)

    copies(i, True)
    loc = loc_ref[...]
    gates = g_ref[...]
    slot_i = lax.broadcasted_iota(I32, (tt, r), 1)
    gmat = jnp.where(slot_i == loc[:, 0:1], gates[:, 0:1], 0.0)
    for k in range(1, TOP_K):
        gmat = gmat + jnp.where(slot_i == loc[:, k:k + 1], gates[:, k:k + 1], 0.0)
    h = jnp.where(i == last, _pad_rows(hs_ref[...], tt), h_ref[...])
    y = _rms(h + _dot(gmat.astype(BF16), buf_ref[i % 2].astype(BF16)), gfin_ref[...])

    @pl.when(i < last)
    def _():
        y_ref[...] = y

    @pl.when(i == last)
    def _():
        ysmp_ref[...] = y[0:ysmp_ref.shape[0], :]


def _combine(cnt, off, base, loc, gates, h2_p, h2_s, norm_final, ys, tt):
    ntp = h2_p.shape[0] // tt
    tok = pl.BlockSpec((tt, D_MODEL), lambda i: (jnp.minimum(i, ntp - 1), 0))
    smp = pl.BlockSpec(h2_s.shape, lambda i: (0, 0))
    tk = pl.BlockSpec((tt, TOP_K), lambda i: (i, 0))
    smem = pl.BlockSpec(memory_space=pltpu.SMEM)
    return pl.pallas_call(
        functools.partial(_combine_body, tt),
        grid_spec=pltpu.PrefetchScalarGridSpec(
            num_scalar_prefetch=0, grid=(ntp + 1,),
            in_specs=[smem, smem, smem, tk, tk, tok, smp,
                      pl.BlockSpec((1, D_MODEL), lambda i: (0, 0)),
                      pl.BlockSpec(memory_space=pl.ANY)],
            out_specs=[tok, smp],
            scratch_shapes=[pltpu.VMEM((2, _sorted_rows(tt), D_MODEL), F32), pltpu.SemaphoreType.DMA((2,))]),
        out_shape=[jax.ShapeDtypeStruct(h2_p.shape, F32), jax.ShapeDtypeStruct(h2_s.shape, F32)],
        compiler_params=_cparams("arbitrary"),
        name="moe_combine",
    )(cnt, off, base, loc, gates, h2_p, h2_s, norm_final, ys)


def _moe_and_final_norm(hn_p, logits_p, h2_p, hn_s, logits_s, h2_s, w, tt, tm):
    nt = hn_p.shape[0] // tt + 1
    t = hn_p.shape[0] + hn_s.shape[0]
    gates, loc, cnt3, off3 = _route(logits_p, logits_s, tt)
    cnt = cnt3[:, 0, :]
    counts = jnp.sum(cnt, axis=0)
    padded = (counts + tm - 1) // tm * tm
    pad_end = jnp.cumsum(padded)
    start = pad_end - padded
    off = off3[:, 0, :]
    base = (start[None, :] + jnp.cumsum(cnt, axis=0) - cnt).astype(I32)
    n_blocks = (t * TOP_K + nt * N_EXPERTS * (SUBLANES - 1) + N_EXPERTS * (tm - 1)) // tm
    n_rows = n_blocks * tm
    block_start = jnp.arange(n_blocks, dtype=I32) * tm
    block_e = jnp.minimum(jnp.sum(block_start[:, None] >= pad_end[None, :], axis=-1), N_EXPERTS - 1).astype(I32)
    n_used = (pad_end[-1:] // tm).astype(I32)
    zero_starts = jnp.concatenate([jnp.where(padded > 0, pad_end - tm, -1),
                                   jnp.where(block_start >= pad_end[-1], block_start, -1)]).astype(I32)
    xs = _dispatch(cnt, off, base, zero_starts, loc, hn_p, hn_s, n_rows, tm, tt)
    ys = _experts(block_e, n_used, xs, w["w_gate_up"], w["b_gate_up"], w["w_down"], w["b_down"], tm)
    return _combine(cnt, off, base, loc, gates, h2_p, h2_s, w["norm_final"], ys, tt)


def kernel(x_prompt, x_sample, mem_prompt, state_ssm, state_mamba_conv, state_short_conv, cache_mem_k, cache_mem_v, norm_mix, w_in, w_mconv, b_mconv, dt_bias, a_log, d_skip, norm_ssm, w_sconv, w_out, norm_xattn, norm_mem, w_xq, w_xk, w_xv, w_xo, norm_moe, w_router, b_router, w_gate_up, b_gate_up, w_down, b_down, norm_final):
    nbp, seq, _ = x_prompt.shape
    nbs = x_sample.shape[0]
    dt_lo = SSM_INNER + SSM_CONV_DIM
    w_in0 = w_in[0]
    w_dt = w_in0[:, dt_lo:dt_lo + SSM_HEADS]
    w = {
        "norm_mix": norm_mix, "norm_ssm": norm_ssm, "norm_xattn": norm_xattn, "norm_moe": norm_moe,
        "norm_final": norm_final.reshape(1, D_MODEL),
        "w_a": w_in0[:, :dt_lo].astype(BF16),
        "w_dt": w_dt.astype(BF16), "w_dt_t": w_dt.T.astype(BF16),
        "w_b": w_in0[:, dt_lo + SSM_HEADS:].astype(BF16),
        "w_mconv": w_mconv[0], "b_mconv": b_mconv,
        "dt_bias": dt_bias, "dt_bias_t": dt_bias.reshape(SSM_HEADS, 1),
        "a_log": a_log, "a_log_t": a_log.reshape(SSM_HEADS, 1),
        "d_skip": jnp.repeat(d_skip, SSM_HEAD_DIM, axis=1),
        "w_sconv": w_sconv[0], "w_out": w_out[0].astype(BF16),
        "w_xq": w_xq[0].astype(BF16), "w_xo": w_xo[0].astype(BF16),
        "w_router": w_router[0].astype(BF16), "b_router": b_router,
        "w_gate_up": w_gate_up[0], "b_gate_up": b_gate_up[0].reshape(N_EXPERTS, 1, 2 * D_FF),
        "w_down": w_down[0], "b_down": b_down[0].reshape(N_EXPERTS, 1, D_MODEL),
    }

    k_p, v_p, kb, vb = _mem_kv(mem_prompt.reshape(nbp * N_MEM, D_MODEL), norm_mem,
                               w_xk[0].astype(BF16), w_xv[0].astype(BF16))
    h1, ssm_p, mconv_p, sconv_p = _prompt_mixer(x_prompt.reshape(nbp * seq, D_MODEL), nbp, w)
    h2, hn, logits = _prompt_attn(h1, kb, vb, nbp, w)

    xs2 = x_sample.reshape(nbs, D_MODEL)
    mstate_t = jnp.transpose(state_mamba_conv[0], (1, 0, 2))
    sstate_t = jnp.transpose(state_short_conv[0], (1, 0, 2))
    z, xs_, dtx, dec, bm, cm, yb, sga, mnew_t, snew_t = _sample_proj(xs2, mstate_t, sstate_t, w)
    ssm_s, y_s = _sample_state(dec, state_ssm[0].reshape(nbs, SSM_INNER, SSM_STATE), dtx, bm, cm)
    h1s, q_s = _sample_fin1(xs2, y_s, xs_, z, yb, sga, w)
    o_s = _sample_attn(q_s.reshape(nbs, 1, D_MODEL),
                       cache_mem_k[0], cache_mem_v[0])
    h2s, hns, logits_s = _sample_fin2(h1s, o_s.reshape(nbs, D_MODEL), w)
    y_prompt, y_sample = _moe_and_final_norm(hn, logits, h2, hns, logits_s, h2s, w, MIX_TILE, MOE_ROW_TILE)

    return (y_prompt.reshape(nbp, seq, D_MODEL),
            y_sample.reshape(nbs, 1, D_MODEL),
            ssm_p.reshape(1, nbp, SSM_HEADS, SSM_HEAD_DIM, SSM_STATE),
            mconv_p[None], sconv_p[None],
            k_p.reshape(1, nbp, N_MEM, XA_HEADS, XA_HEAD_DIM),
            v_p.reshape(1, nbp, N_MEM, XA_HEADS, XA_HEAD_DIM),
            ssm_s.reshape(1, nbs, SSM_HEADS, SSM_HEAD_DIM, SSM_STATE),
            jnp.transpose(mnew_t, (1, 0, 2))[None],
            jnp.transpose(snew_t, (1, 0, 2))[None])
```

```python
import functools

import jax
import jax.numpy as jnp
from jax import lax
from jax.experimental import pallas as pl
from jax.experimental.pallas import tpu as pltpu

F32 = jnp.float32
BF16 = jnp.bfloat16
I32 = jnp.int32

D_MODEL = 1024
N_MEM = 256
SSM_HEADS = 16
SSM_HEAD_DIM = 64
SSM_INNER = SSM_HEADS * SSM_HEAD_DIM
SSM_STATE = 128
SSM_GROUPS = 4
HEADS_PER_GROUP = SSM_HEADS // SSM_GROUPS
GROUP_WIDTH = SSM_INNER // SSM_GROUPS
SSM_CONV = 4
SSM_CONV_DIM = SSM_INNER + 2 * SSM_GROUPS * SSM_STATE
SC_CONV = 3
XA_HEADS = 4
XA_HEAD_DIM = D_MODEL // XA_HEADS
N_EXPERTS = 32
TOP_K = 4
D_FF = D_MODEL
SWIGLU_LIMIT = 7.0
SWIGLU_ALPHA = 1.702
EPS = 1e-6

LANES = 128
SUBLANES = 8
VMEM_LIMIT = 56 * 1024 * 1024

MIX_TILE = 256
MOE_ROW_TILE = 512
EXPERT_ROW_SPLITS = 4
STATE_BB = 8
ATTN_BB = 4

NT_DIMS = (((1,), (1,)), ((), ()))
TN_DIMS = (((0,), (0,)), ((), ()))


def _cparams(*sem):
    return pltpu.CompilerParams(dimension_semantics=sem, vmem_limit_bytes=VMEM_LIMIT)


def _const_spec(shape):
    nd = len(shape)
    return pl.BlockSpec(shape, lambda *_: (0,) * nd, pipeline_mode=pl.Buffered(1))


def _sigmoid(x):
    return 1.0 / (1.0 + jnp.exp(-x))


def _silu(x):
    return x * _sigmoid(x)


def _softplus(x):
    return jnp.maximum(x, 0.0) + jnp.log(1.0 + jnp.exp(-jnp.abs(x)))


def _rms(x, g):
    ms = jnp.mean(x * x, axis=-1, keepdims=True)
    return x * lax.rsqrt(ms + EPS) * g


def _dot(a, b):
    return jnp.dot(a, b, preferred_element_type=F32)


def _dot_nt(a, b):
    return lax.dot_general(a, b, NT_DIMS, preferred_element_type=F32)


def _expand_heads(v):
    rows = v.shape[0]
    lane = lax.broadcasted_iota(I32, (rows, LANES), 1)
    pieces = []
    for j in range(SSM_HEADS // 2):
        a = jnp.broadcast_to(v[:, 2 * j:2 * j + 1], (rows, LANES))
        b = jnp.broadcast_to(v[:, 2 * j + 1:2 * j + 2], (rows, LANES))
        pieces.append(jnp.where(lane < SSM_HEAD_DIM, a, b))
    return jnp.concatenate(pieces, axis=1)


def _pad_rows(x, rows):
    return jnp.concatenate([x, jnp.zeros((rows - x.shape[0], x.shape[1]), x.dtype)], axis=0)


def _group_rmsnorm(u, g):
    outs = []
    for k in range(SSM_GROUPS):
        ug = u[:, k * GROUP_WIDTH:(k + 1) * GROUP_WIDTH]
        ms = jnp.mean(ug * ug, axis=-1, keepdims=True)
        outs.append(ug * lax.rsqrt(ms + EPS))
    return jnp.concatenate(outs, axis=1) * g


def _memkv_body(mem_ref, g_ref, wk_ref, wv_ref, k_ref, v_ref, kb_ref, vb_ref):
    mn = _rms(mem_ref[...], g_ref[...]).astype(BF16)
    k = _dot(mn, wk_ref[...])
    v = _dot(mn, wv_ref[...])
    k_ref[...] = k
    v_ref[...] = v
    kb_ref[...] = k.astype(BF16)
    vb_ref[...] = v.astype(BF16)


def _mem_kv(mem2d, norm_mem, wk, wv):
    rows = mem2d.shape[0]
    nb = rows // N_MEM
    blk = pl.BlockSpec((N_MEM, D_MODEL), lambda b: (b, 0))
    return pl.pallas_call(
        _memkv_body,
        grid=(nb,),
        in_specs=[blk, _const_spec((1, D_MODEL)), _const_spec((D_MODEL, D_MODEL)),
                  _const_spec((D_MODEL, D_MODEL))],
        out_specs=[blk, blk, blk, blk],
        out_shape=[jax.ShapeDtypeStruct((rows, D_MODEL), F32)] * 2
        + [jax.ShapeDtypeStruct((rows, D_MODEL), BF16)] * 2,
        compiler_params=_cparams("arbitrary"),
        name="mem_kv",
    )(mem2d, norm_mem, wk, wv)


def _mix_body(x_ref, gmix_ref, wa_ref, wdtc_ref, wdtr_ref, wb_ref, wmc_ref, bmc_ref,
              dtb_ref, dtbt_ref, alog_ref, alogt_ref, dskip_ref, gssm_ref, wsc_ref, wout_ref,
              h_ref, ssm_ref, mbuf_ref, sbuf_ref,
              st_ref, cbuf_ref, scbuf_ref):
    tq = MIX_TILE
    c = pl.program_id(1)

    @pl.when(c == 0)
    def _():
        st_ref[...] = jnp.zeros_like(st_ref)
        cbuf_ref[0:SUBLANES, :] = jnp.zeros((SUBLANES, SSM_CONV_DIM), F32)
        scbuf_ref[0:SUBLANES, :] = jnp.zeros((SUBLANES, D_MODEL), F32)

    x = x_ref[...]
    xn = _rms(x, gmix_ref[...]).astype(BF16)

    pa = _dot(xn, wa_ref[...])
    z = pa[:, :SSM_INNER]
    u = pa[:, SSM_INNER:]
    cbuf_ref[SUBLANES:SUBLANES + tq, :] = u
    wm = wmc_ref[...]
    conv = u * wm[SSM_CONV - 1:SSM_CONV, :] + bmc_ref[...]
    for k in range(SSM_CONV - 1):
        off = SUBLANES - (SSM_CONV - 1) + k
        conv = conv + cbuf_ref[off:off + tq, :] * wm[k:k + 1, :]
    tail = cbuf_ref[tq + SUBLANES - (SSM_CONV - 1):tq + SUBLANES, :]
    mbuf_ref[...] = tail
    cbuf_ref[SUBLANES - (SSM_CONV - 1):SUBLANES, :] = tail
    xbc = _silu(conv)
    xs = xbc[:, :SSM_INNER]
    bm = xbc[:, SSM_INNER:SSM_INNER + SSM_GROUPS * SSM_STATE]
    cm = xbc[:, SSM_INNER + SSM_GROUPS * SSM_STATE:]

    dt = _softplus(_dot(xn, wdtc_ref[...]) + dtb_ref[...])
    dtt = _softplus(_dot_nt(wdtr_ref[...], xn) + dtbt_ref[...])
    a_row = -jnp.exp(alog_ref[...])
    a_col = -jnp.exp(alogt_ref[...])
    row_i = lax.broadcasted_iota(I32, (tq, tq), 0)
    col_i = lax.broadcasted_iota(I32, (tq, tq), 1)
    causal = row_i >= col_i
    tril = causal.astype(F32)
    triu = (row_i <= col_i).astype(F32)
    a_cum = jnp.dot(tril, dt * a_row, precision=lax.Precision.HIGHEST,
                    preferred_element_type=F32)
    a_cumt = jnp.dot(dtt * a_col, triu, precision=lax.Precision.HIGHEST,
                     preferred_element_type=F32)
    a_last = a_cum[tq - 1:tq, :]

    xdt = xs * _expand_heads(dt)
    in_decay = _expand_heads(jnp.exp(a_cum))
    to_end = _expand_heads(jnp.exp(a_last - a_cum))
    chunk_decay = _expand_heads(jnp.exp(a_last))
    xdt_b = xdt.astype(BF16)
    xend_b = (xdt * to_end).astype(BF16)
    lane = lax.broadcasted_iota(I32, (tq, LANES), 1)

    y_groups = []
    for g in range(SSM_GROUPS):
        cg = cm[:, g * SSM_STATE:(g + 1) * SSM_STATE].astype(BF16)
        bg_f = bm[:, g * SSM_STATE:(g + 1) * SSM_STATE]
        bg = bg_f.astype(BF16)
        scores = _dot_nt(cg, bg)
        gs = slice(g * GROUP_WIDTH, (g + 1) * GROUP_WIDTH)
        st_g = st_ref[:, gs]
        y_off = _dot(cg, st_g.astype(BF16)) * in_decay[:, gs]
        pair_out = []
        for pr in range(HEADS_PER_GROUP // 2):
            h0 = g * HEADS_PER_GROUP + 2 * pr
            xp = xdt_b[:, h0 * SSM_HEAD_DIM:(h0 + 2) * SSM_HEAD_DIM]
            ys = []
            for h in (h0, h0 + 1):
                seg = a_cum[:, h:h + 1] - a_cumt[h:h + 1, :]
                decay = jnp.where(causal, jnp.exp(jnp.minimum(seg, 0.0)), 0.0)
                ys.append(_dot((scores * decay).astype(BF16), xp))
            pair_out.append(jnp.where(lane < SSM_HEAD_DIM, ys[0], ys[1]))
        y_groups.append(jnp.concatenate(pair_out, axis=1) + y_off)
        st_ref[:, gs] = st_g * chunk_decay[:, gs] + _dot(bg_f.T.astype(BF16), xend_b[:, gs])
    y = jnp.concatenate(y_groups, axis=1) + dskip_ref[...] * xs
    y_a = _group_rmsnorm(y * _silu(z), gssm_ref[...])

    @pl.when(c == pl.num_programs(1) - 1)
    def _():
        ssm_ref[...] = st_ref[...].T

    pb = _dot(xn, wb_ref[...])
    sc_b = pb[:, 0:D_MODEL]
    cv = pb[:, D_MODEL:2 * D_MODEL] * pb[:, 2 * D_MODEL:3 * D_MODEL]
    g_a = pb[:, 3 * D_MODEL:4 * D_MODEL]
    g_b = pb[:, 4 * D_MODEL:5 * D_MODEL]
    scbuf_ref[SUBLANES:SUBLANES + tq, :] = cv
    ws = wsc_ref[...]
    uc = cv * ws[SC_CONV - 1:SC_CONV, :]
    for k in range(SC_CONV - 1):
        off = SUBLANES - (SC_CONV - 1) + k
        uc = uc + scbuf_ref[off:off + tq, :] * ws[k:k + 1, :]
    stail = scbuf_ref[tq + SUBLANES - (SC_CONV - 1):tq + SUBLANES, :]
    sbuf_ref[...] = stail
    scbuf_ref[SUBLANES - (SC_CONV - 1):SUBLANES, :] = stail
    merged = _sigmoid(g_a) * y_a + _sigmoid(g_b) * (sc_b * uc)
    h_ref[...] = x + _dot(merged.astype(BF16), wout_ref[...])


def _prompt_mixer(x2d, nb, w):
    t = x2d.shape[0]
    nc = t // nb // MIX_TILE
    tok = pl.BlockSpec((MIX_TILE, D_MODEL), lambda b, c: (b * nc + c, 0))
    return pl.pallas_call(
        _mix_body,
        grid=(nb, nc),
        in_specs=[tok, _const_spec((1, D_MODEL)),
                  _const_spec((D_MODEL, SSM_INNER + SSM_CONV_DIM)),
                  _const_spec((D_MODEL, SSM_HEADS)), _const_spec((SSM_HEADS, D_MODEL)),
                  _const_spec((D_MODEL, 5 * D_MODEL)),
                  _const_spec((SSM_CONV, SSM_CONV_DIM)), _const_spec((1, SSM_CONV_DIM)),
                  _const_spec((1, SSM_HEADS)), _const_spec((SSM_HEADS, 1)),
                  _const_spec((1, SSM_HEADS)), _const_spec((SSM_HEADS, 1)),
                  _const_spec((1, SSM_INNER)), _const_spec((1, SSM_INNER)),
                  _const_spec((SC_CONV, D_MODEL)), _const_spec((D_MODEL, D_MODEL))],
        out_specs=[tok,
                   pl.BlockSpec((None, SSM_INNER, SSM_STATE), lambda b, c: (b, 0, 0)),
                   pl.BlockSpec((None, SSM_CONV - 1, SSM_CONV_DIM), lambda b, c: (b, 0, 0)),
                   pl.BlockSpec((None, SC_CONV - 1, D_MODEL), lambda b, c: (b, 0, 0))],
        out_shape=[jax.ShapeDtypeStruct((t, D_MODEL), F32),
                   jax.ShapeDtypeStruct((nb, SSM_INNER, SSM_STATE), F32),
                   jax.ShapeDtypeStruct((nb, SSM_CONV - 1, SSM_CONV_DIM), F32),
                   jax.ShapeDtypeStruct((nb, SC_CONV - 1, D_MODEL), F32)],
        scratch_shapes=[pltpu.VMEM((SSM_STATE, SSM_INNER), F32),
                        pltpu.VMEM((MIX_TILE + SUBLANES, SSM_CONV_DIM), F32),
                        pltpu.VMEM((MIX_TILE + SUBLANES, D_MODEL), F32)],
        compiler_params=_cparams("arbitrary", "arbitrary"),
        name="prompt_mixer",
    )(x2d, w["norm_mix"], w["w_a"], w["w_dt"], w["w_dt_t"], w["w_b"], w["w_mconv"], w["b_mconv"],
      w["dt_bias"], w["dt_bias_t"], w["a_log"], w["a_log_t"], w["d_skip"], w["norm_ssm"],
      w["w_sconv"], w["w_out"])


def _router_tail(h2, gmoe_ref, wr_ref, br_ref, h2_ref, hn_ref, lg_ref):
    h2_ref[...] = h2
    hn = _rms(h2, gmoe_ref[...])
    hn_ref[...] = hn
    lg_ref[...] = _dot(hn.astype(BF16), wr_ref[...]) + br_ref[...]


def _attn_body(h_ref, gx_ref, wq_ref, k_ref, v_ref, wo_ref, gmoe_ref, wr_ref, br_ref,
               h2_ref, hn_ref, lg_ref):
    h = h_ref[...]
    hn = _rms(h, gx_ref[...]).astype(BF16)
    q = _dot(hn, wq_ref[...]).astype(BF16)
    outs = []
    for hd in range(XA_HEADS):
        sl = slice(hd * XA_HEAD_DIM, (hd + 1) * XA_HEAD_DIM)
        s = _dot_nt(q[:, sl], k_ref[:, sl]) * (XA_HEAD_DIM ** -0.5)
        e = jnp.exp(s - jnp.max(s, axis=-1, keepdims=True))
        p = e / jnp.sum(e, axis=-1, keepdims=True)
        outs.append(_dot(p.astype(BF16), v_ref[:, sl]))
    o = jnp.concatenate(outs, axis=1).astype(BF16)
    h2 = h + _dot(o, wo_ref[...])
    _router_tail(h2, gmoe_ref, wr_ref, br_ref, h2_ref, hn_ref, lg_ref)


def _prompt_attn(h1, kb, vb, nb, w):
    t = h1.shape[0]
    nc = t // nb // MIX_TILE
    tok = pl.BlockSpec((MIX_TILE, D_MODEL), lambda b, c: (b * nc + c, 0))
    kv = pl.BlockSpec((N_MEM, D_MODEL), lambda b, c: (b, 0))
    return pl.pallas_call(
        _attn_body,
        grid=(nb, nc),
        in_specs=[tok, _const_spec((1, D_MODEL)), _const_spec((D_MODEL, D_MODEL)), kv, kv,
                  _const_spec((D_MODEL, D_MODEL)), _const_spec((1, D_MODEL)),
                  _const_spec((D_MODEL, N_EXPERTS)), _const_spec((1, N_EXPERTS))],
        out_specs=[tok, tok, pl.BlockSpec((MIX_TILE, N_EXPERTS), lambda b, c: (b * nc + c, 0))],
        out_shape=[jax.ShapeDtypeStruct((t, D_MODEL), F32),
                   jax.ShapeDtypeStruct((t, D_MODEL), F32),
                   jax.ShapeDtypeStruct((t, N_EXPERTS), F32)],
        compiler_params=_cparams("arbitrary", "arbitrary"),
        name="prompt_attn",
    )(h1, w["norm_xattn"], w["w_xq"], kb, vb, w["w_xo"], w["norm_moe"], w["w_router"], w["b_router"])


def _sproj_body(x_ref, gmix_ref, wa_ref, wdtc_ref, wb_ref, wmc_ref, bmc_ref, dtb_ref, alog_ref,
                wsc_ref, mst_ref, sst_ref,
                z_ref, xs_ref, dtx_ref, dec_ref, bm_ref, cm_ref, yb_ref, sga_ref, mnew_ref, snew_ref):
    x = x_ref[...]
    xn = _rms(x, gmix_ref[...]).astype(BF16)
    pa = _dot(xn, wa_ref[...])
    z_ref[...] = pa[:, :SSM_INNER]
    u = pa[:, SSM_INNER:]
    wm = wmc_ref[...]
    conv = u * wm[SSM_CONV - 1:SSM_CONV, :] + bmc_ref[...]
    for k in range(SSM_CONV - 1):
        conv = conv + mst_ref[k] * wm[k:k + 1, :]
    for k in range(SSM_CONV - 2):
        mnew_ref[k] = mst_ref[k + 1]
    mnew_ref[SSM_CONV - 2] = u
    xbc = _silu(conv)
    xs = xbc[:, :SSM_INNER]
    xs_ref[...] = xs
    bm_ref[...] = xbc[:, SSM_INNER:SSM_INNER + SSM_GROUPS * SSM_STATE]
    cm_ref[...] = xbc[:, SSM_INNER + SSM_GROUPS * SSM_STATE:]
    dt = _softplus(_dot(xn, wdtc_ref[...]) + dtb_ref[...])
    dec_ref[...] = jnp.exp(dt * (-jnp.exp(alog_ref[...])))
    dtx_ref[...] = xs * _expand_heads(dt)
    pb = _dot(xn, wb_ref[...])
    cv = pb[:, D_MODEL:2 * D_MODEL] * pb[:, 2 * D_MODEL:3 * D_MODEL]
    ws = wsc_ref[...]
    uc = cv * ws[SC_CONV - 1:SC_CONV, :]
    for k in range(SC_CONV - 1):
        uc = uc + sst_ref[k] * ws[k:k + 1, :]
    for k in range(SC_CONV - 2):
        snew_ref[k] = sst_ref[k + 1]
    snew_ref[SC_CONV - 2] = cv
    yb_ref[...] = _sigmoid(pb[:, 4 * D_MODEL:5 * D_MODEL]) * (pb[:, 0:D_MODEL] * uc)
    sga_ref[...] = _sigmoid(pb[:, 3 * D_MODEL:4 * D_MODEL])


def _sample_proj(x, mstate_t, sstate_t, w):
    nb = x.shape[0]
    f = lambda *s: jax.ShapeDtypeStruct(s, F32)
    return pl.pallas_call(
        _sproj_body,
        out_shape=[f(nb, SSM_INNER), f(nb, SSM_INNER), f(nb, SSM_INNER), f(nb, SSM_HEADS),
                   f(nb, SSM_GROUPS * SSM_STATE), f(nb, SSM_GROUPS * SSM_STATE),
                   f(nb, D_MODEL), f(nb, D_MODEL),
                   f(SSM_CONV - 1, nb, SSM_CONV_DIM), f(SC_CONV - 1, nb, D_MODEL)],
        compiler_params=pltpu.CompilerParams(vmem_limit_bytes=VMEM_LIMIT),
        name="sample_proj",
    )(x, w["norm_mix"], w["w_a"], w["w_dt"], w["w_b"], w["w_mconv"], w["b_mconv"], w["dt_bias"],
      w["a_log"], w["w_sconv"], mstate_t, sstate_t)


def _sstate_body(dec_ref, s_ref, dtx_ref, bm_ref, cm_ref, snew_ref, y_ref):
    i = pl.program_id(0)
    rows_per_blk = LANES
    for j in range(STATE_BB):
        b = i * STATE_BB + j
        dtx_row = dtx_ref[j:j + 1, :]
        y_parts = []
        for g in range(SSM_GROUPS):
            b_row = bm_ref[j:j + 1, g * SSM_STATE:(g + 1) * SSM_STATE]
            c_row = cm_ref[j:j + 1, g * SSM_STATE:(g + 1) * SSM_STATE].astype(BF16)
            new_blocks = []
            for q in range(GROUP_WIDTH // rows_per_blk):
                r0 = g * GROUP_WIDTH + q * rows_per_blk
                dcol = jnp.broadcast_to(dtx_row[:, r0:r0 + rows_per_blk], (rows_per_blk, LANES)).T
                sub = []
                for hh in range(rows_per_blk // SSM_HEAD_DIM):
                    h = r0 // SSM_HEAD_DIM + hh
                    lo = hh * SSM_HEAD_DIM
                    s_old = s_ref[j, r0 + lo:r0 + lo + SSM_HEAD_DIM, :]
                    sub.append(s_old * dec_ref[b, h] + dcol[lo:lo + SSM_HEAD_DIM, :] * b_row)
                blk = jnp.concatenate(sub, axis=0)
                snew_ref[j, r0:r0 + rows_per_blk, :] = blk
                new_blocks.append(blk.astype(BF16))
            s_g = jnp.concatenate(new_blocks, axis=0)
            y_parts.append(_dot_nt(c_row, s_g))
        y_ref[j:j + 1, :] = jnp.concatenate(y_parts, axis=1)


def _sample_state(dec, state, dtx, bm, cm):
    nb = state.shape[0]
    row = lambda wdt: pl.BlockSpec((STATE_BB, wdt), lambda i, dec: (i, 0))
    st = pl.BlockSpec((STATE_BB, SSM_INNER, SSM_STATE), lambda i, dec: (i, 0, 0))
    return pl.pallas_call(
        _sstate_body,
        grid_spec=pltpu.PrefetchScalarGridSpec(
            num_scalar_prefetch=1, grid=(nb // STATE_BB,),
            in_specs=[st, row(SSM_INNER), row(SSM_GROUPS * SSM_STATE), row(SSM_GROUPS * SSM_STATE)],
            out_specs=[st, row(SSM_INNER)]),
        out_shape=[jax.ShapeDtypeStruct(state.shape, F32), jax.ShapeDtypeStruct((nb, SSM_INNER), F32)],
        compiler_params=_cparams("arbitrary"),
        name="sample_state",
    )(dec, state, dtx, bm, cm)


def _sfin1_body(x_ref, y_ref, xs_ref, z_ref, yb_ref, sga_ref, dskip_ref, gssm_ref, wout_ref,
                gx_ref, wq_ref, h_ref, q_ref):
    y = y_ref[...] + dskip_ref[...] * xs_ref[...]
    y_a = _group_rmsnorm(y * _silu(z_ref[...]), gssm_ref[...])
    merged = sga_ref[...] * y_a + yb_ref[...]
    h = x_ref[...] + _dot(merged.astype(BF16), wout_ref[...])
    h_ref[...] = h
    q_ref[...] = _dot(_rms(h, gx_ref[...]).astype(BF16), wq_ref[...])


def _sample_fin1(x, y, xs, z, yb, sga, w):
    nb = x.shape[0]
    return pl.pallas_call(
        _sfin1_body,
        out_shape=[jax.ShapeDtypeStruct((nb, D_MODEL), F32)] * 2,
        compiler_params=pltpu.CompilerParams(vmem_limit_bytes=VMEM_LIMIT),
        name="sample_fin1",
    )(x, y, xs, z, yb, sga, w["d_skip"], w["norm_ssm"], w["w_out"], w["norm_xattn"], w["w_xq"])


def _sattn_body(q_ref, k_ref, v_ref, o_ref):
    for j in range(ATTN_BB):
        q_row = q_ref[j]
        q4 = jnp.concatenate([q_row[:, h * XA_HEAD_DIM:(h + 1) * XA_HEAD_DIM]
                              for h in range(XA_HEADS)], axis=0)
        s = jnp.sum(k_ref[j] * q4[None], axis=-1, keepdims=True) * (XA_HEAD_DIM ** -0.5)
        e = jnp.exp(s - jnp.max(s, axis=0, keepdims=True))
        p = e / jnp.sum(e, axis=0, keepdims=True)
        o4 = jnp.sum(p * v_ref[j], axis=0)
        o_ref[j] = jnp.concatenate([o4[h:h + 1, :] for h in range(XA_HEADS)], axis=1)


def _sample_attn(q3, k3, v3):
    nb = q3.shape[0]
    qs = pl.BlockSpec((ATTN_BB, 1, D_MODEL), lambda i: (i, 0, 0))
    kv = pl.BlockSpec((ATTN_BB, N_MEM, XA_HEADS, XA_HEAD_DIM), lambda i: (i, 0, 0, 0))
    return pl.pallas_call(
        _sattn_body,
        grid=(nb // ATTN_BB,),
        in_specs=[qs, kv, kv],
        out_specs=qs,
        out_shape=jax.ShapeDtypeStruct((nb, 1, D_MODEL), F32),
        compiler_params=_cparams("arbitrary"),
        name="sample_attn",
    )(q3, k3, v3)


def _sfin2_body(h_ref, o_ref, wo_ref, gmoe_ref, wr_ref, br_ref, h2_ref, hn_ref, lg_ref):
    h2 = h_ref[...] + _dot(o_ref[...].astype(BF16), wo_ref[...])
    _router_tail(h2, gmoe_ref, wr_ref, br_ref, h2_ref, hn_ref, lg_ref)


def _sample_fin2(h1, o, w):
    nb = h1.shape[0]
    return pl.pallas_call(
        _sfin2_body,
        out_shape=[jax.ShapeDtypeStruct((nb, D_MODEL), F32)] * 2
        + [jax.ShapeDtypeStruct((nb, N_EXPERTS), F32)],
        compiler_params=pltpu.CompilerParams(vmem_limit_bytes=VMEM_LIMIT),
        name="sample_fin2",
    )(h1, o, w["w_xo"], w["norm_moe"], w["w_router"], w["b_router"])


def _route_body(lgp_ref, lgs_ref, g_ref, loc_ref, cnt_ref, off_ref):
    tt = lgp_ref.shape[0]
    is_sample = pl.program_id(0) == pl.num_programs(0) - 1
    row = lax.broadcasted_iota(I32, (tt, 1), 0)
    valid = jnp.logical_or(jnp.logical_not(is_sample), row < lgs_ref.shape[0])
    work = jnp.where(is_sample, _pad_rows(lgs_ref[...], tt), lgp_ref[...])
    lane = lax.broadcasted_iota(I32, (tt, N_EXPERTS), 1).astype(F32)
    vals, hots = [], []
    for _ in range(TOP_K):
        m = jnp.max(work, axis=-1, keepdims=True)
        idx = jnp.min(jnp.where(work == m, lane, float(N_EXPERTS)), axis=-1, keepdims=True)
        hot = (lane == idx) & valid
        vals.append(m)
        hots.append(hot)
        work = jnp.where(hot, -jnp.inf, work)
    exps = [jnp.exp(v - vals[0]) for v in vals]
    tot = exps[0]
    for e in exps[1:]:
        tot = tot + e
    assigned = hots[0]
    for hot in hots[1:]:
        assigned = assigned | hot
    a = assigned.astype(BF16)
    r_i = lax.broadcasted_iota(I32, (tt, tt), 0)
    c_i = lax.broadcasted_iota(I32, (tt, tt), 1)
    rank = _dot((r_i > c_i).astype(BF16), a)
    e_r = lax.broadcasted_iota(I32, (N_EXPERTS, N_EXPERTS), 0)
    e_c = lax.broadcasted_iota(I32, (N_EXPERTS, N_EXPERTS), 1)
    cnt = jnp.sum(a.astype(F32), axis=0, keepdims=True)
    cnt = jnp.floor((cnt + (SUBLANES - 1)) * (1.0 / SUBLANES)) * SUBLANES
    cnt_rows = jnp.broadcast_to(cnt, (SUBLANES, N_EXPERTS)).astype(BF16)
    off = _dot(cnt_rows, (e_r < e_c).astype(BF16))[0:1, :]
    slot = rank + off
    k_lane = lax.broadcasted_iota(I32, (tt, TOP_K), 1)
    g_out = jnp.zeros((tt, TOP_K), F32)
    l_out = jnp.zeros((tt, TOP_K), F32)
    for k in range(TOP_K):
        lk = jnp.sum(jnp.where(hots[k], slot, 0.0), axis=-1, keepdims=True)
        g_out = jnp.where(k_lane == k, exps[k] / tot, g_out)
        l_out = jnp.where(k_lane == k, lk, l_out)
    g_ref[...] = jnp.where(valid, g_out, 0.0)
    loc_ref[...] = jnp.where(valid, l_out, -1.0).astype(I32)
    cnt_ref[...] = cnt.astype(I32)
    off_ref[...] = off.astype(I32)


def _route(logits_p, logits_s, tt):
    ntp = logits_p.shape[0] // tt
    nt = ntp + 1
    t = nt * tt
    tk = pl.BlockSpec((tt, TOP_K), lambda i: (i, 0))
    per_tile = pl.BlockSpec((None, 1, N_EXPERTS), lambda i: (i, 0, 0))
    return pl.pallas_call(
        _route_body,
        grid=(nt,),
        in_specs=[pl.BlockSpec((tt, N_EXPERTS), lambda i: (jnp.minimum(i, ntp - 1), 0)),
                  pl.BlockSpec(logits_s.shape, lambda i: (0, 0))],
        out_specs=[tk, tk, per_tile, per_tile],
        out_shape=[jax.ShapeDtypeStruct((t, TOP_K), F32), jax.ShapeDtypeStruct((t, TOP_K), I32),
                   jax.ShapeDtypeStruct((nt, 1, N_EXPERTS), I32), jax.ShapeDtypeStruct((nt, 1, N_EXPERTS), I32)],
        compiler_params=_cparams("arbitrary"),
        name="moe_route",
    )(logits_p, logits_s)


def _sorted_rows(tt):
    return tt * TOP_K + N_EXPERTS * SUBLANES


def _run_copies(tt, tile, cnt_ref, off_ref, base_ref, make_copy, wait):
    if wait:
        total = off_ref[tile, N_EXPERTS - 1] + cnt_ref[tile, N_EXPERTS - 1]

        @pl.when(total > 0)
        def _():
            make_copy(0, 0, pl.multiple_of(total, SUBLANES)).wait()
        return

    def per_expert(e, carry):
        n = cnt_ref[tile, e]

        @pl.when(n > 0)
        def _():
            make_copy(pl.multiple_of(off_ref[tile, e], SUBLANES),
                      pl.multiple_of(base_ref[tile, e], SUBLANES), pl.multiple_of(n, SUBLANES)).start()
        return carry

    lax.fori_loop(0, N_EXPERTS, per_expert, 0)


def _dispatch_body(tm, tt, cnt_ref, off_ref, base_ref, zstart_ref, loc_ref, x_ref, xs_ref, o_hbm,
                   zero_ref, srt_ref, zsem, sems):
    i = pl.program_id(0)
    last = pl.num_programs(0) - 1
    r = _sorted_rows(tt)

    def zero_copy(j):
        return pltpu.make_async_copy(zero_ref, o_hbm.at[pl.ds(pl.multiple_of(zstart_ref[j], tm), tm), :], zsem)

    @pl.when(i == 0)
    def _():
        zero_ref[...] = jnp.zeros_like(zero_ref)

        def start(j, carry):
            @pl.when(zstart_ref[j] >= 0)
            def _():
                zero_copy(j).start()
            return carry

        def wait(j, carry):
            @pl.when(zstart_ref[j] >= 0)
            def _():
                zero_copy(j).wait()
            return carry

        lax.fori_loop(0, zstart_ref.shape[0], start, 0)
        lax.fori_loop(0, zstart_ref.shape[0], wait, 0)

    loc = loc_ref[...]
    slot_i = lax.broadcasted_iota(I32, (tt, r), 1)
    hit = slot_i == loc[:, 0:1]
    for k in range(1, TOP_K):
        hit = hit | (slot_i == loc[:, k:k + 1])
    buf = i % 2
    x = jnp.where(i == last, _pad_rows(xs_ref[...], tt), x_ref[...])
    srt_ref[buf] = lax.dot_general(hit.astype(BF16), x.astype(BF16), TN_DIMS, preferred_element_type=F32)

    def copies(tile, wait):
        b = tile % 2

        def make_copy(lo, go, size):
            return pltpu.make_async_copy(srt_ref.at[b, pl.ds(lo, size), :], o_hbm.at[pl.ds(go, size), :],
                                         sems.at[b])

        _run_copies(tt, tile, cnt_ref, off_ref, base_ref, make_copy, wait)

    copies(i, False)

    @pl.when(i > 0)
    def _():
        copies(i - 1, True)

    @pl.when(i == last)
    def _():
        copies(i, True)


def _dispatch(cnt, off, base, zero_starts, loc, hn_p, hn_s, n_rows, tm, tt):
    ntp = hn_p.shape[0] // tt
    smem = pl.BlockSpec(memory_space=pltpu.SMEM)
    return pl.pallas_call(
        functools.partial(_dispatch_body, tm, tt),
        grid_spec=pltpu.PrefetchScalarGridSpec(
            num_scalar_prefetch=0, grid=(ntp + 1,),
            in_specs=[smem, smem, smem, smem,
                      pl.BlockSpec((tt, TOP_K), lambda i: (i, 0)),
                      pl.BlockSpec((tt, D_MODEL), lambda i: (jnp.minimum(i, ntp - 1), 0)),
                      pl.BlockSpec(hn_s.shape, lambda i: (0, 0))],
            out_specs=pl.BlockSpec(memory_space=pl.ANY),
            scratch_shapes=[pltpu.VMEM((tm, D_MODEL), F32), pltpu.VMEM((2, _sorted_rows(tt), D_MODEL), F32),
                            pltpu.SemaphoreType.DMA, pltpu.SemaphoreType.DMA((2,))]),
        out_shape=jax.ShapeDtypeStruct((n_rows, D_MODEL), F32),
        compiler_params=_cparams("arbitrary"),
        name="moe_dispatch",
    )(cnt, off, base, zero_starts, loc, hn_p, hn_s)


def _expert_body(be_ref, nu_ref, bv_ref, x_ref, wgu_ref, bgu_ref, wdn_ref, bdn_ref, y_ref, wgu_b, wdn_b):
    i = pl.program_id(0)
    tm = x_ref.shape[0]
    valid = bv_ref[i]

    @pl.when(jnp.logical_or(i == 0, be_ref[i] != be_ref[jnp.maximum(i - 1, 0)]))
    def _():
        wgu_b[...] = wgu_ref[...].astype(BF16)
        wdn_b[...] = wdn_ref[...].astype(BF16)

    def ffn(rows):
        gu = _dot(x_ref[0:rows, :].astype(BF16), wgu_b[...]) + bgu_ref[...]
        gate = jnp.minimum(gu[:, :D_FF], SWIGLU_LIMIT)
        up = jnp.clip(gu[:, D_FF:], -SWIGLU_LIMIT, SWIGLU_LIMIT)
        act = (up + 1.0) * (gate * _sigmoid(SWIGLU_ALPHA * gate))
        y_ref[0:rows, :] = _dot(act.astype(BF16), wdn_b[...]) + bdn_ref[...]
        if rows < tm:
            y_ref[rows:tm, :] = jnp.zeros((tm - rows, D_MODEL), F32)

    quarter = tm // EXPERT_ROW_SPLITS
    for q in range(1, EXPERT_ROW_SPLITS + 1):
        @pl.when(jnp.logical_and(valid > (q - 1) * quarter, valid <= q * quarter))
        def _(q=q):
            ffn(q * quarter)

    @pl.when(valid == 0)
    def _():
        y_ref[...] = jnp.zeros_like(y_ref)


def _experts(block_e, n_used, block_valid, xs, wgu, bgu, wdn, bdn, tm):
    n_rows = xs.shape[0]
    return pl.pallas_call(
        _expert_body,
        grid_spec=pltpu.PrefetchScalarGridSpec(
            num_scalar_prefetch=3, grid=(n_rows // tm,),
            in_specs=[pl.BlockSpec((tm, D_MODEL), lambda i, be, nu, bv: (jnp.minimum(i, nu[0] - 1), 0)),
                      pl.BlockSpec((None, D_MODEL, 2 * D_FF), lambda i, be, nu, bv: (be[i], 0, 0)),
                      pl.BlockSpec((None, 1, 2 * D_FF), lambda i, be, nu, bv: (be[i], 0, 0)),
                      pl.BlockSpec((None, D_FF, D_MODEL), lambda i, be, nu, bv: (be[i], 0, 0)),
                      pl.BlockSpec((None, 1, D_MODEL), lambda i, be, nu, bv: (be[i], 0, 0))],
            out_specs=pl.BlockSpec((tm, D_MODEL), lambda i, be, nu, bv: (i, 0)),
            scratch_shapes=[pltpu.VMEM((D_MODEL, 2 * D_FF), BF16), pltpu.VMEM((D_FF, D_MODEL), BF16)]),
        out_shape=jax.ShapeDtypeStruct((n_rows, D_MODEL), F32),
        compiler_params=_cparams("arbitrary"),
        name="moe_experts",
    )(block_e, n_used, block_valid, xs, wgu, bgu, wdn, bdn)


def _combine_body(tt, cnt_ref, off_ref, base_ref, loc_ref, g_ref, h_ref, hs_ref, gfin_ref, ys_hbm,
                  y_ref, ysmp_ref, buf_ref, sems):
    i = pl.program_id(0)
    last = pl.num_programs(0) - 1
    r = _sorted_rows(tt)

    def copies(tile, wait):
        b = tile % 2

        def make_copy(lo, go, size):
            return pltpu.make_async_copy(ys_hbm.at[pl.ds(go, size), :], buf_ref.at[b, pl.ds(lo, size), :],
                                         sems.at[b])

        _run_copies(tt, tile, cnt_ref, off_ref, base_ref, make_copy, wait)

    @pl.when(i == 0)
    def _():
        buf_ref[...] = jnp.zeros_like(buf_ref)
        copies(0, False)

    @pl.when(i < last)
    def _():
        copies(i + 1, False)

    copies(i, True)
    loc = loc_ref[...]
    gates = g_ref[...]
    slot_i = lax.broadcasted_iota(I32, (tt, r), 1)
    gmat = jnp.where(slot_i == loc[:, 0:1], gates[:, 0:1], 0.0)
    for k in range(1, TOP_K):
        gmat = gmat + jnp.where(slot_i == loc[:, k:k + 1], gates[:, k:k + 1], 0.0)
    h = jnp.where(i == last, _pad_rows(hs_ref[...], tt), h_ref[...])
    y = _rms(h + _dot(gmat.astype(BF16), buf_ref[i % 2].astype(BF16)), gfin_ref[...])

    @pl.when(i < last)
    def _():
        y_ref[...] = y

    @pl.when(i == last)
    def _():
        ysmp_ref[...] = y[0:ysmp_ref.shape[0], :]


def _combine(cnt, off, base, loc, gates, h2_p, h2_s, norm_final, ys, tt):
    ntp = h2_p.shape[0] // tt
    tok = pl.BlockSpec((tt, D_MODEL), lambda i: (jnp.minimum(i, ntp - 1), 0))
    smp = pl.BlockSpec(h2_s.shape, lambda i: (0, 0))
    tk = pl.BlockSpec((tt, TOP_K), lambda i: (i, 0))
    smem = pl.BlockSpec(memory_space=pltpu.SMEM)
    return pl.pallas_call(
        functools.partial(_combine_body, tt),
        grid_spec=pltpu.PrefetchScalarGridSpec(
            num_scalar_prefetch=0, grid=(ntp + 1,),
            in_specs=[smem, smem, smem, tk, tk, tok, smp,
                      pl.BlockSpec((1, D_MODEL), lambda i: (0, 0)),
                      pl.BlockSpec(memory_space=pl.ANY)],
            out_specs=[tok, smp],
            scratch_shapes=[pltpu.VMEM((2, _sorted_rows(tt), D_MODEL), F32), pltpu.SemaphoreType.DMA((2,))]),
        out_shape=[jax.ShapeDtypeStruct(h2_p.shape, F32), jax.ShapeDtypeStruct(h2_s.shape, F32)],
        compiler_params=_cparams("arbitrary"),
        name="moe_combine",
    )(cnt, off, base, loc, gates, h2_p, h2_s, norm_final, ys)


def _moe_and_final_norm(hn_p, logits_p, h2_p, hn_s, logits_s, h2_s, w, tt, tm):
    nt = hn_p.shape[0] // tt + 1
    t = hn_p.shape[0] + hn_s.shape[0]
    gates, loc, cnt3, off3 = _route(logits_p, logits_s, tt)
    cnt = cnt3[:, 0, :]
    counts = jnp.sum(cnt, axis=0)
    padded = (counts + tm - 1) // tm * tm
    pad_end = jnp.cumsum(padded)
    start = pad_end - padded
    off = off3[:, 0, :]
    base = (start[None, :] + jnp.cumsum(cnt, axis=0) - cnt).astype(I32)
    n_blocks = (t * TOP_K + nt * N_EXPERTS * (SUBLANES - 1) + N_EXPERTS * (tm - 1)) // tm
    n_rows = n_blocks * tm
    block_start = jnp.arange(n_blocks, dtype=I32) * tm
    block_e = jnp.minimum(jnp.sum(block_start[:, None] >= pad_end[None, :], axis=-1), N_EXPERTS - 1).astype(I32)
    n_used = (pad_end[-1:] // tm).astype(I32)
    zero_starts = jnp.concatenate([jnp.where(padded > 0, pad_end - tm, -1),
                                   jnp.where(block_start >= pad_end[-1], block_start, -1)]).astype(I32)
    xs = _dispatch(cnt, off, base, zero_starts, loc, hn_p, hn_s, n_rows, tm, tt)
    block_valid = jnp.clip((start + counts)[block_e] - block_start, 0, tm)
    block_valid = jnp.where(block_start < pad_end[-1], block_valid, 0).astype(I32)
    ys = _experts(block_e, n_used, block_valid, xs, w["w_gate_up"], w["b_gate_up"], w["w_down"], w["b_down"], tm)
    return _combine(cnt, off, base, loc, gates, h2_p, h2_s, w["norm_final"], ys, tt)


def kernel(x_prompt, x_sample, mem_prompt, state_ssm, state_mamba_conv, state_short_conv, cache_mem_k, cache_mem_v, norm_mix, w_in, w_mconv, b_mconv, dt_bias, a_log, d_skip, norm_ssm, w_sconv, w_out, norm_xattn, norm_mem, w_xq, w_xk, w_xv, w_xo, norm_moe, w_router, b_router, w_gate_up, b_gate_up, w_down, b_down, norm_final):
    nbp, seq, _ = x_prompt.shape
    nbs = x_sample.shape[0]
    dt_lo = SSM_INNER + SSM_CONV_DIM
    w_in0 = w_in[0]
    w_dt = w_in0[:, dt_lo:dt_lo + SSM_HEADS]
    w = {
        "norm_mix": norm_mix, "norm_ssm": norm_ssm, "norm_xattn": norm_xattn, "norm_moe": norm_moe,
        "norm_final": norm_final.reshape(1, D_MODEL),
        "w_a": w_in0[:, :dt_lo].astype(BF16),
        "w_dt": w_dt.astype(BF16), "w_dt_t": w_dt.T.astype(BF16),
        "w_b": w_in0[:, dt_lo + SSM_HEADS:].astype(BF16),
        "w_mconv": w_mconv[0], "b_mconv": b_mconv,
        "dt_bias": dt_bias, "dt_bias_t": dt_bias.reshape(SSM_HEADS, 1),
        "a_log": a_log, "a_log_t": a_log.reshape(SSM_HEADS, 1),
        "d_skip": jnp.repeat(d_skip, SSM_HEAD_DIM, axis=1),
        "w_sconv": w_sconv[0], "w_out": w_out[0].astype(BF16),
        "w_xq": w_xq[0].astype(BF16), "w_xo": w_xo[0].astype(BF16),
        "w_router": w_router[0].astype(BF16), "b_router": b_router,
        "w_gate_up": w_gate_up[0], "b_gate_up": b_gate_up[0].reshape(N_EXPERTS, 1, 2 * D_FF),
        "w_down": w_down[0], "b_down": b_down[0].reshape(N_EXPERTS, 1, D_MODEL),
    }

    k_p, v_p, kb, vb = _mem_kv(mem_prompt.reshape(nbp * N_MEM, D_MODEL), norm_mem,
                               w_xk[0].astype(BF16), w_xv[0].astype(BF16))
    h1, ssm_p, mconv_p, sconv_p = _prompt_mixer(x_prompt.reshape(nbp * seq, D_MODEL), nbp, w)
    h2, hn, logits = _prompt_attn(h1, kb, vb, nbp, w)

    xs2 = x_sample.reshape(nbs, D_MODEL)
    mstate_t = jnp.transpose(state_mamba_conv[0], (1, 0, 2))
    sstate_t = jnp.transpose(state_short_conv[0], (1, 0, 2))
    z, xs_, dtx, dec, bm, cm, yb, sga, mnew_t, snew_t = _sample_proj(xs2, mstate_t, sstate_t, w)
    ssm_s, y_s = _sample_state(dec, state_ssm[0].reshape(nbs, SSM_INNER, SSM_STATE), dtx, bm, cm)
    h1s, q_s = _sample_fin1(xs2, y_s, xs_, z, yb, sga, w)
    o_s = _sample_attn(q_s.reshape(nbs, 1, D_MODEL),
                       cache_mem_k[0], cache_mem_v[0])
    h2s, hns, logits_s = _sample_fin2(h1s, o_s.reshape(nbs, D_MODEL), w)
    y_prompt, y_sample = _moe_and_final_norm(hn, logits, h2, hns, logits_s, h2s, w, MIX_TILE, MOE_ROW_TILE)

    return (y_prompt.reshape(nbp, seq, D_MODEL),
            y_sample.reshape(nbs, 1, D_MODEL),
            ssm_p.reshape(1, nbp, SSM_HEADS, SSM_HEAD_DIM, SSM_STATE),
            mconv_p[None], sconv_p[None],
            k_p.reshape(1, nbp, N_MEM, XA_HEADS, XA_HEAD_DIM),
            v_p.reshape(1, nbp, N_MEM, XA_HEADS, XA_HEAD_DIM),
            ssm_s.reshape(1, nbs, SSM_HEADS, SSM_HEAD_DIM, SSM_STATE),
            jnp.transpose(mnew_t, (1, 0, 2))[None],
            jnp.transpose(snew_t, (1, 0, 2))[None])
```

```python
import functools

import jax
import jax.numpy as jnp
from jax import lax
from jax.experimental import pallas as pl
from jax.experimental.pallas import tpu as pltpu

F32 = jnp.float32
BF16 = jnp.bfloat16
I32 = jnp.int32

D_MODEL = 1024
N_MEM = 256
SSM_HEADS = 16
SSM_HEAD_DIM = 64
SSM_INNER = SSM_HEADS * SSM_HEAD_DIM
SSM_STATE = 128
SSM_GROUPS = 4
HEADS_PER_GROUP = SSM_HEADS // SSM_GROUPS
GROUP_WIDTH = SSM_INNER // SSM_GROUPS
SSM_CONV = 4
SSM_CONV_DIM = SSM_INNER + 2 * SSM_GROUPS * SSM_STATE
SC_CONV = 3
XA_HEADS = 4
XA_HEAD_DIM = D_MODEL // XA_HEADS
N_EXPERTS = 32
TOP_K = 4
D_FF = D_MODEL
SWIGLU_LIMIT = 7.0
SWIGLU_ALPHA = 1.702
EPS = 1e-6

LANES = 128
SUBLANES = 8
VMEM_LIMIT = 56 * 1024 * 1024

MIX_TILE = 256
MOE_ROW_TILE = 512
EXPERT_ROW_SPLITS = 4
STATE_BB = 8
ATTN_BB = 4

NT_DIMS = (((1,), (1,)), ((), ()))
TN_DIMS = (((0,), (0,)), ((), ()))


def _cparams(*sem):
    return pltpu.CompilerParams(dimension_semantics=sem, vmem_limit_bytes=VMEM_LIMIT)


def _const_spec(shape):
    nd = len(shape)
    return pl.BlockSpec(shape, lambda *_: (0,) * nd, pipeline_mode=pl.Buffered(1))


def _sigmoid(x):
    return 1.0 / (1.0 + jnp.exp(-x))


def _silu(x):
    return x * _sigmoid(x)


def _softplus(x):
    return jnp.maximum(x, 0.0) + jnp.log(1.0 + jnp.exp(-jnp.abs(x)))


def _rms(x, g):
    ms = jnp.mean(x * x, axis=-1, keepdims=True)
    return x * lax.rsqrt(ms + EPS) * g


def _dot(a, b):
    return jnp.dot(a, b, preferred_element_type=F32)


def _dot_nt(a, b):
    return lax.dot_general(a, b, NT_DIMS, preferred_element_type=F32)


def _expand_heads(v):
    rows = v.shape[0]
    lane = lax.broadcasted_iota(I32, (rows, LANES), 1)
    pieces = []
    for j in range(SSM_HEADS // 2):
        a = jnp.broadcast_to(v[:, 2 * j:2 * j + 1], (rows, LANES))
        b = jnp.broadcast_to(v[:, 2 * j + 1:2 * j + 2], (rows, LANES))
        pieces.append(jnp.where(lane < SSM_HEAD_DIM, a, b))
    return jnp.concatenate(pieces, axis=1)


def _pad_rows(x, rows):
    return jnp.concatenate([x, jnp.zeros((rows - x.shape[0], x.shape[1]), x.dtype)], axis=0)


def _group_rmsnorm(u, g):
    outs = []
    for k in range(SSM_GROUPS):
        ug = u[:, k * GROUP_WIDTH:(k + 1) * GROUP_WIDTH]
        ms = jnp.mean(ug * ug, axis=-1, keepdims=True)
        outs.append(ug * lax.rsqrt(ms + EPS))
    return jnp.concatenate(outs, axis=1) * g


def _memkv_body(mem_ref, g_ref, wk_ref, wv_ref, k_ref, v_ref, kb_ref, vb_ref):
    mn = _rms(mem_ref[...], g_ref[...]).astype(BF16)
    k = _dot(mn, wk_ref[...])
    v = _dot(mn, wv_ref[...])
    k_ref[...] = k
    v_ref[...] = v
    kb_ref[...] = k.astype(BF16)
    vb_ref[...] = v.astype(BF16)


def _mem_kv(mem2d, norm_mem, wk, wv):
    rows = mem2d.shape[0]
    nb = rows // N_MEM
    blk = pl.BlockSpec((N_MEM, D_MODEL), lambda b: (b, 0))
    return pl.pallas_call(
        _memkv_body,
        grid=(nb,),
        in_specs=[blk, _const_spec((1, D_MODEL)), _const_spec((D_MODEL, D_MODEL)),
                  _const_spec((D_MODEL, D_MODEL))],
        out_specs=[blk, blk, blk, blk],
        out_shape=[jax.ShapeDtypeStruct((rows, D_MODEL), F32)] * 2
        + [jax.ShapeDtypeStruct((rows, D_MODEL), BF16)] * 2,
        compiler_params=_cparams("arbitrary"),
        name="mem_kv",
    )(mem2d, norm_mem, wk, wv)


def _mix_body(x_ref, gmix_ref, wa_ref, wdtc_ref, wdtr_ref, wb_ref, wmc_ref, bmc_ref,
              dtb_ref, dtbt_ref, alog_ref, alogt_ref, dskip_ref, gssm_ref, wsc_ref, wout_ref,
              h_ref, ssm_ref, mbuf_ref, sbuf_ref,
              st_ref, cbuf_ref, scbuf_ref):
    tq = MIX_TILE
    c = pl.program_id(1)

    @pl.when(c == 0)
    def _():
        st_ref[...] = jnp.zeros_like(st_ref)
        cbuf_ref[0:SUBLANES, :] = jnp.zeros((SUBLANES, SSM_CONV_DIM), F32)
        scbuf_ref[0:SUBLANES, :] = jnp.zeros((SUBLANES, D_MODEL), F32)

    x = x_ref[...]
    xn = _rms(x, gmix_ref[...]).astype(BF16)

    pa = _dot(xn, wa_ref[...])
    z = pa[:, :SSM_INNER]
    u = pa[:, SSM_INNER:]
    cbuf_ref[SUBLANES:SUBLANES + tq, :] = u
    wm = wmc_ref[...]
    conv = u * wm[SSM_CONV - 1:SSM_CONV, :] + bmc_ref[...]
    for k in range(SSM_CONV - 1):
        off = SUBLANES - (SSM_CONV - 1) + k
        conv = conv + cbuf_ref[off:off + tq, :] * wm[k:k + 1, :]
    tail = cbuf_ref[tq + SUBLANES - (SSM_CONV - 1):tq + SUBLANES, :]
    mbuf_ref[...] = tail
    cbuf_ref[SUBLANES - (SSM_CONV - 1):SUBLANES, :] = tail
    xbc = _silu(conv)
    xs = xbc[:, :SSM_INNER]
    bm = xbc[:, SSM_INNER:SSM_INNER + SSM_GROUPS * SSM_STATE]
    cm = xbc[:, SSM_INNER + SSM_GROUPS * SSM_STATE:]

    dt = _softplus(_dot(xn, wdtc_ref[...]) + dtb_ref[...])
    dtt = _softplus(_dot_nt(wdtr_ref[...], xn) + dtbt_ref[...])
    a_row = -jnp.exp(alog_ref[...])
    a_col = -jnp.exp(alogt_ref[...])
    row_i = lax.broadcasted_iota(I32, (tq, tq), 0)
    col_i = lax.broadcasted_iota(I32, (tq, tq), 1)
    causal = row_i >= col_i
    tril = causal.astype(F32)
    triu = (row_i <= col_i).astype(F32)
    a_cum = jnp.dot(tril, dt * a_row, precision=lax.Precision.HIGHEST,
                    preferred_element_type=F32)
    a_cumt = jnp.dot(dtt * a_col, triu, precision=lax.Precision.HIGHEST,
                     preferred_element_type=F32)
    a_last = a_cum[tq - 1:tq, :]

    xdt = xs * _expand_heads(dt)
    in_decay = _expand_heads(jnp.exp(a_cum))
    to_end = _expand_heads(jnp.exp(a_last - a_cum))
    chunk_decay = _expand_heads(jnp.exp(a_last))
    xdt_b = xdt.astype(BF16)
    xend_b = (xdt * to_end).astype(BF16)
    lane = lax.broadcasted_iota(I32, (tq, LANES), 1)

    y_groups = []
    for g in range(SSM_GROUPS):
        cg = cm[:, g * SSM_STATE:(g + 1) * SSM_STATE].astype(BF16)
        bg_f = bm[:, g * SSM_STATE:(g + 1) * SSM_STATE]
        bg = bg_f.astype(BF16)
        scores = _dot_nt(cg, bg)
        gs = slice(g * GROUP_WIDTH, (g + 1) * GROUP_WIDTH)
        st_g = st_ref[:, gs]
        y_off = _dot(cg, st_g.astype(BF16)) * in_decay[:, gs]
        pair_out = []
        for pr in range(HEADS_PER_GROUP // 2):
            h0 = g * HEADS_PER_GROUP + 2 * pr
            xp = xdt_b[:, h0 * SSM_HEAD_DIM:(h0 + 2) * SSM_HEAD_DIM]
            ys = []
            for h in (h0, h0 + 1):
                seg = a_cum[:, h:h + 1] - a_cumt[h:h + 1, :]
                decay = jnp.where(causal, jnp.exp(jnp.minimum(seg, 0.0)), 0.0)
                ys.append(_dot((scores * decay).astype(BF16), xp))
            pair_out.append(jnp.where(lane < SSM_HEAD_DIM, ys[0], ys[1]))
        y_groups.append(jnp.concatenate(pair_out, axis=1) + y_off)
        st_ref[:, gs] = st_g * chunk_decay[:, gs] + _dot(bg_f.T.astype(BF16), xend_b[:, gs])
    y = jnp.concatenate(y_groups, axis=1) + dskip_ref[...] * xs
    y_a = _group_rmsnorm(y * _silu(z), gssm_ref[...])

    @pl.when(c == pl.num_programs(1) - 1)
    def _():
        ssm_ref[...] = st_ref[...].T

    pb = _dot(xn, wb_ref[...])
    sc_b = pb[:, 0:D_MODEL]
    cv = pb[:, D_MODEL:2 * D_MODEL] * pb[:, 2 * D_MODEL:3 * D_MODEL]
    g_a = pb[:, 3 * D_MODEL:4 * D_MODEL]
    g_b = pb[:, 4 * D_MODEL:5 * D_MODEL]
    scbuf_ref[SUBLANES:SUBLANES + tq, :] = cv
    ws = wsc_ref[...]
    uc = cv * ws[SC_CONV - 1:SC_CONV, :]
    for k in range(SC_CONV - 1):
        off = SUBLANES - (SC_CONV - 1) + k
        uc = uc + scbuf_ref[off:off + tq, :] * ws[k:k + 1, :]
    stail = scbuf_ref[tq + SUBLANES - (SC_CONV - 1):tq + SUBLANES, :]
    sbuf_ref[...] = stail
    scbuf_ref[SUBLANES - (SC_CONV - 1):SUBLANES, :] = stail
    merged = _sigmoid(g_a) * y_a + _sigmoid(g_b) * (sc_b * uc)
    h_ref[...] = x + _dot(merged.astype(BF16), wout_ref[...])


def _prompt_mixer(x2d, nb, w):
    t = x2d.shape[0]
    nc = t // nb // MIX_TILE
    tok = pl.BlockSpec((MIX_TILE, D_MODEL), lambda b, c: (b * nc + c, 0))
    return pl.pallas_call(
        _mix_body,
        grid=(nb, nc),
        in_specs=[tok, _const_spec((1, D_MODEL)),
                  _const_spec((D_MODEL, SSM_INNER + SSM_CONV_DIM)),
                  _const_spec((D_MODEL, SSM_HEADS)), _const_spec((SSM_HEADS, D_MODEL)),
                  _const_spec((D_MODEL, 5 * D_MODEL)),
                  _const_spec((SSM_CONV, SSM_CONV_DIM)), _const_spec((1, SSM_CONV_DIM)),
                  _const_spec((1, SSM_HEADS)), _const_spec((SSM_HEADS, 1)),
                  _const_spec((1, SSM_HEADS)), _const_spec((SSM_HEADS, 1)),
                  _const_spec((1, SSM_INNER)), _const_spec((1, SSM_INNER)),
                  _const_spec((SC_CONV, D_MODEL)), _const_spec((D_MODEL, D_MODEL))],
        out_specs=[tok,
                   pl.BlockSpec((None, SSM_INNER, SSM_STATE), lambda b, c: (b, 0, 0)),
                   pl.BlockSpec((None, SSM_CONV - 1, SSM_CONV_DIM), lambda b, c: (b, 0, 0)),
                   pl.BlockSpec((None, SC_CONV - 1, D_MODEL), lambda b, c: (b, 0, 0))],
        out_shape=[jax.ShapeDtypeStruct((t, D_MODEL), F32),
                   jax.ShapeDtypeStruct((nb, SSM_INNER, SSM_STATE), F32),
                   jax.ShapeDtypeStruct((nb, SSM_CONV - 1, SSM_CONV_DIM), F32),
                   jax.ShapeDtypeStruct((nb, SC_CONV - 1, D_MODEL), F32)],
        scratch_shapes=[pltpu.VMEM((SSM_STATE, SSM_INNER), F32),
                        pltpu.VMEM((MIX_TILE + SUBLANES, SSM_CONV_DIM), F32),
                        pltpu.VMEM((MIX_TILE + SUBLANES, D_MODEL), F32)],
        compiler_params=_cparams("arbitrary", "arbitrary"),
        name="prompt_mixer",
    )(x2d, w["norm_mix"], w["w_a"], w["w_dt"], w["w_dt_t"], w["w_b"], w["w_mconv"], w["b_mconv"],
      w["dt_bias"], w["dt_bias_t"], w["a_log"], w["a_log_t"], w["d_skip"], w["norm_ssm"],
      w["w_sconv"], w["w_out"])


def _router_tail(h2, gmoe_ref, wr_ref, br_ref, h2_ref, hn_ref, lg_ref):
    h2_ref[...] = h2
    hn = _rms(h2, gmoe_ref[...])
    hn_ref[...] = hn
    lg_ref[...] = _dot(hn.astype(BF16), wr_ref[...]) + br_ref[...]


def _attn_body(h_ref, gx_ref, wq_ref, k_ref, v_ref, wo_ref, gmoe_ref, wr_ref, br_ref,
               h2_ref, hn_ref, lg_ref):
    h = h_ref[...]
    hn = _rms(h, gx_ref[...]).astype(BF16)
    q = _dot(hn, wq_ref[...]).astype(BF16)
    outs = []
    for hd in range(XA_HEADS):
        sl = slice(hd * XA_HEAD_DIM, (hd + 1) * XA_HEAD_DIM)
        s = _dot_nt(q[:, sl], k_ref[:, sl]) * (XA_HEAD_DIM ** -0.5)
        e = jnp.exp(s - jnp.max(s, axis=-1, keepdims=True))
        p = e / jnp.sum(e, axis=-1, keepdims=True)
        outs.append(_dot(p.astype(BF16), v_ref[:, sl]))
    o = jnp.concatenate(outs, axis=1).astype(BF16)
    h2 = h + _dot(o, wo_ref[...])
    _router_tail(h2, gmoe_ref, wr_ref, br_ref, h2_ref, hn_ref, lg_ref)


def _prompt_attn(h1, kb, vb, nb, w):
    t = h1.shape[0]
    nc = t // nb // MIX_TILE
    tok = pl.BlockSpec((MIX_TILE, D_MODEL), lambda b, c: (b * nc + c, 0))
    kv = pl.BlockSpec((N_MEM, D_MODEL), lambda b, c: (b, 0))
    return pl.pallas_call(
        _attn_body,
        grid=(nb, nc),
        in_specs=[tok, _const_spec((1, D_MODEL)), _const_spec((D_MODEL, D_MODEL)), kv, kv,
                  _const_spec((D_MODEL, D_MODEL)), _const_spec((1, D_MODEL)),
                  _const_spec((D_MODEL, N_EXPERTS)), _const_spec((1, N_EXPERTS))],
        out_specs=[tok, tok, pl.BlockSpec((MIX_TILE, N_EXPERTS), lambda b, c: (b * nc + c, 0))],
        out_shape=[jax.ShapeDtypeStruct((t, D_MODEL), F32),
                   jax.ShapeDtypeStruct((t, D_MODEL), F32),
                   jax.ShapeDtypeStruct((t, N_EXPERTS), F32)],
        compiler_params=_cparams("arbitrary", "arbitrary"),
        name="prompt_attn",
    )(h1, w["norm_xattn"], w["w_xq"], kb, vb, w["w_xo"], w["norm_moe"], w["w_router"], w["b_router"])


def _sproj_body(x_ref, gmix_ref, wa_ref, wdtc_ref, wb_ref, wmc_ref, bmc_ref, dtb_ref, alog_ref,
                wsc_ref, mst_ref, sst_ref,
                z_ref, xs_ref, dtx_ref, dec_ref, bm_ref, cm_ref, yb_ref, sga_ref, mnew_ref, snew_ref):
    x = x_ref[...]
    xn = _rms(x, gmix_ref[...]).astype(BF16)
    pa = _dot(xn, wa_ref[...])
    z_ref[...] = pa[:, :SSM_INNER]
    u = pa[:, SSM_INNER:]
    wm = wmc_ref[...]
    conv = u * wm[SSM_CONV - 1:SSM_CONV, :] + bmc_ref[...]
    for k in range(SSM_CONV - 1):
        conv = conv + mst_ref[k] * wm[k:k + 1, :]
    for k in range(SSM_CONV - 2):
        mnew_ref[k] = mst_ref[k + 1]
    mnew_ref[SSM_CONV - 2] = u
    xbc = _silu(conv)
    xs = xbc[:, :SSM_INNER]
    xs_ref[...] = xs
    bm_ref[...] = xbc[:, SSM_INNER:SSM_INNER + SSM_GROUPS * SSM_STATE]
    cm_ref[...] = xbc[:, SSM_INNER + SSM_GROUPS * SSM_STATE:]
    dt = _softplus(_dot(xn, wdtc_ref[...]) + dtb_ref[...])
    dec_ref[...] = jnp.exp(dt * (-jnp.exp(alog_ref[...])))
    dtx_ref[...] = xs * _expand_heads(dt)
    pb = _dot(xn, wb_ref[...])
    cv = pb[:, D_MODEL:2 * D_MODEL] * pb[:, 2 * D_MODEL:3 * D_MODEL]
    ws = wsc_ref[...]
    uc = cv * ws[SC_CONV - 1:SC_CONV, :]
    for k in range(SC_CONV - 1):
        uc = uc + sst_ref[k] * ws[k:k + 1, :]
    for k in range(SC_CONV - 2):
        snew_ref[k] = sst_ref[k + 1]
    snew_ref[SC_CONV - 2] = cv
    yb_ref[...] = _sigmoid(pb[:, 4 * D_MODEL:5 * D_MODEL]) * (pb[:, 0:D_MODEL] * uc)
    sga_ref[...] = _sigmoid(pb[:, 3 * D_MODEL:4 * D_MODEL])


def _sample_proj(x, mstate_t, sstate_t, w):
    nb = x.shape[0]
    f = lambda *s: jax.ShapeDtypeStruct(s, F32)
    return pl.pallas_call(
        _sproj_body,
        out_shape=[f(nb, SSM_INNER), f(nb, SSM_INNER), f(nb, SSM_INNER), f(nb, SSM_HEADS),
                   f(nb, SSM_GROUPS * SSM_STATE), f(nb, SSM_GROUPS * SSM_STATE),
                   f(nb, D_MODEL), f(nb, D_MODEL),
                   f(SSM_CONV - 1, nb, SSM_CONV_DIM), f(SC_CONV - 1, nb, D_MODEL)],
        compiler_params=pltpu.CompilerParams(vmem_limit_bytes=VMEM_LIMIT),
        name="sample_proj",
    )(x, w["norm_mix"], w["w_a"], w["w_dt"], w["w_b"], w["w_mconv"], w["b_mconv"], w["dt_bias"],
      w["a_log"], w["w_sconv"], mstate_t, sstate_t)


def _sstate_body(dec_ref, s_ref, dtx_ref, bm_ref, cm_ref, snew_ref, y_ref):
    i = pl.program_id(0)
    rows_per_blk = LANES
    for j in range(STATE_BB):
        b = i * STATE_BB + j
        dtx_row = dtx_ref[j:j + 1, :]
        y_parts = []
        for g in range(SSM_GROUPS):
            b_row = bm_ref[j:j + 1, g * SSM_STATE:(g + 1) * SSM_STATE]
            c_row = cm_ref[j:j + 1, g * SSM_STATE:(g + 1) * SSM_STATE].astype(BF16)
            new_blocks = []
            for q in range(GROUP_WIDTH // rows_per_blk):
                r0 = g * GROUP_WIDTH + q * rows_per_blk
                dcol = jnp.broadcast_to(dtx_row[:, r0:r0 + rows_per_blk], (rows_per_blk, LANES)).T
                sub = []
                for hh in range(rows_per_blk // SSM_HEAD_DIM):
                    h = r0 // SSM_HEAD_DIM + hh
                    lo = hh * SSM_HEAD_DIM
                    s_old = s_ref[j, r0 + lo:r0 + lo + SSM_HEAD_DIM, :]
                    sub.append(s_old * dec_ref[b, h] + dcol[lo:lo + SSM_HEAD_DIM, :] * b_row)
                blk = jnp.concatenate(sub, axis=0)
                snew_ref[j, r0:r0 + rows_per_blk, :] = blk
                new_blocks.append(blk.astype(BF16))
            s_g = jnp.concatenate(new_blocks, axis=0)
            y_parts.append(_dot_nt(c_row, s_g))
        y_ref[j:j + 1, :] = jnp.concatenate(y_parts, axis=1)


def _sample_state(dec, state, dtx, bm, cm):
    nb = state.shape[0]
    row = lambda wdt: pl.BlockSpec((STATE_BB, wdt), lambda i, dec: (i, 0))
    st = pl.BlockSpec((STATE_BB, SSM_INNER, SSM_STATE), lambda i, dec: (i, 0, 0))
    return pl.pallas_call(
        _sstate_body,
        grid_spec=pltpu.PrefetchScalarGridSpec(
            num_scalar_prefetch=1, grid=(nb // STATE_BB,),
            in_specs=[st, row(SSM_INNER), row(SSM_GROUPS * SSM_STATE), row(SSM_GROUPS * SSM_STATE)],
            out_specs=[st, row(SSM_INNER)]),
        out_shape=[jax.ShapeDtypeStruct(state.shape, F32), jax.ShapeDtypeStruct((nb, SSM_INNER), F32)],
        compiler_params=_cparams("arbitrary"),
        name="sample_state",
    )(dec, state, dtx, bm, cm)


def _sfin1_body(x_ref, y_ref, xs_ref, z_ref, yb_ref, sga_ref, dskip_ref, gssm_ref, wout_ref,
                gx_ref, wq_ref, h_ref, q_ref):
    y = y_ref[...] + dskip_ref[...] * xs_ref[...]
    y_a = _group_rmsnorm(y * _silu(z_ref[...]), gssm_ref[...])
    merged = sga_ref[...] * y_a + yb_ref[...]
    h = x_ref[...] + _dot(merged.astype(BF16), wout_ref[...])
    h_ref[...] = h
    q_ref[...] = _dot(_rms(h, gx_ref[...]).astype(BF16), wq_ref[...])


def _sample_fin1(x, y, xs, z, yb, sga, w):
    nb = x.shape[0]
    return pl.pallas_call(
        _sfin1_body,
        out_shape=[jax.ShapeDtypeStruct((nb, D_MODEL), F32)] * 2,
        compiler_params=pltpu.CompilerParams(vmem_limit_bytes=VMEM_LIMIT),
        name="sample_fin1",
    )(x, y, xs, z, yb, sga, w["d_skip"], w["norm_ssm"], w["w_out"], w["norm_xattn"], w["w_xq"])


def _sattn_body(q_ref, k_ref, v_ref, o_ref):
    for j in range(ATTN_BB):
        q_row = q_ref[j]
        q4 = jnp.concatenate([q_row[:, h * XA_HEAD_DIM:(h + 1) * XA_HEAD_DIM]
                              for h in range(XA_HEADS)], axis=0)
        s = jnp.sum(k_ref[j] * q4[None], axis=-1, keepdims=True) * (XA_HEAD_DIM ** -0.5)
        e = jnp.exp(s - jnp.max(s, axis=0, keepdims=True))
        p = e / jnp.sum(e, axis=0, keepdims=True)
        o4 = jnp.sum(p * v_ref[j], axis=0)
        o_ref[j] = jnp.concatenate([o4[h:h + 1, :] for h in range(XA_HEADS)], axis=1)


def _sample_attn(q3, k3, v3):
    nb = q3.shape[0]
    qs = pl.BlockSpec((ATTN_BB, 1, D_MODEL), lambda i: (i, 0, 0))
    kv = pl.BlockSpec((ATTN_BB, N_MEM, XA_HEADS, XA_HEAD_DIM), lambda i: (i, 0, 0, 0))
    return pl.pallas_call(
        _sattn_body,
        grid=(nb // ATTN_BB,),
        in_specs=[qs, kv, kv],
        out_specs=qs,
        out_shape=jax.ShapeDtypeStruct((nb, 1, D_MODEL), F32),
        compiler_params=_cparams("arbitrary"),
        name="sample_attn",
    )(q3, k3, v3)


def _sfin2_body(h_ref, o_ref, wo_ref, gmoe_ref, wr_ref, br_ref, h2_ref, hn_ref, lg_ref):
    h2 = h_ref[...] + _dot(o_ref[...].astype(BF16), wo_ref[...])
    _router_tail(h2, gmoe_ref, wr_ref, br_ref, h2_ref, hn_ref, lg_ref)


def _sample_fin2(h1, o, w):
    nb = h1.shape[0]
    return pl.pallas_call(
        _sfin2_body,
        out_shape=[jax.ShapeDtypeStruct((nb, D_MODEL), F32)] * 2
        + [jax.ShapeDtypeStruct((nb, N_EXPERTS), F32)],
        compiler_params=pltpu.CompilerParams(vmem_limit_bytes=VMEM_LIMIT),
        name="sample_fin2",
    )(h1, o, w["w_xo"], w["norm_moe"], w["w_router"], w["b_router"])


def _route_body(lgp_ref, lgs_ref, g_ref, loc_ref, cnt_ref, off_ref):
    tt = lgp_ref.shape[0]
    is_sample = pl.program_id(0) == pl.num_programs(0) - 1
    row = lax.broadcasted_iota(I32, (tt, 1), 0)
    valid = jnp.logical_or(jnp.logical_not(is_sample), row < lgs_ref.shape[0])
    work = jnp.where(is_sample, _pad_rows(lgs_ref[...], tt), lgp_ref[...])
    lane = lax.broadcasted_iota(I32, (tt, N_EXPERTS), 1).astype(F32)
    vals, hots = [], []
    for _ in range(TOP_K):
        m = jnp.max(work, axis=-1, keepdims=True)
        idx = jnp.min(jnp.where(work == m, lane, float(N_EXPERTS)), axis=-1, keepdims=True)
        hot = (lane == idx) & valid
        vals.append(m)
        hots.append(hot)
        work = jnp.where(hot, -jnp.inf, work)
    exps = [jnp.exp(v - vals[0]) for v in vals]
    tot = exps[0]
    for e in exps[1:]:
        tot = tot + e
    assigned = hots[0]
    for hot in hots[1:]:
        assigned = assigned | hot
    a = assigned.astype(BF16)
    r_i = lax.broadcasted_iota(I32, (tt, tt), 0)
    c_i = lax.broadcasted_iota(I32, (tt, tt), 1)
    rank = _dot((r_i > c_i).astype(BF16), a)
    e_r = lax.broadcasted_iota(I32, (N_EXPERTS, N_EXPERTS), 0)
    e_c = lax.broadcasted_iota(I32, (N_EXPERTS, N_EXPERTS), 1)
    cnt = jnp.sum(a.astype(F32), axis=0, keepdims=True)
    cnt = jnp.floor((cnt + (SUBLANES - 1)) * (1.0 / SUBLANES)) * SUBLANES
    cnt_rows = jnp.broadcast_to(cnt, (SUBLANES, N_EXPERTS)).astype(BF16)
    off = _dot(cnt_rows, (e_r < e_c).astype(BF16))[0:1, :]
    slot = rank + off
    k_lane = lax.broadcasted_iota(I32, (tt, TOP_K), 1)
    g_out = jnp.zeros((tt, TOP_K), F32)
    l_out = jnp.zeros((tt, TOP_K), F32)
    for k in range(TOP_K):
        lk = jnp.sum(jnp.where(hots[k], slot, 0.0), axis=-1, keepdims=True)
        g_out = jnp.where(k_lane == k, exps[k] / tot, g_out)
        l_out = jnp.where(k_lane == k, lk, l_out)
    g_ref[...] = jnp.where(valid, g_out, 0.0)
    loc_ref[...] = jnp.where(valid, l_out, -1.0).astype(I32)
    cnt_ref[...] = cnt.astype(I32)
    off_ref[...] = off.astype(I32)


def _route(logits_p, logits_s, tt):
    ntp = logits_p.shape[0] // tt
    nt = ntp + 1
    t = nt * tt
    tk = pl.BlockSpec((tt, TOP_K), lambda i: (i, 0))
    per_tile = pl.BlockSpec((None, 1, N_EXPERTS), lambda i: (i, 0, 0))
    return pl.pallas_call(
        _route_body,
        grid=(nt,),
        in_specs=[pl.BlockSpec((tt, N_EXPERTS), lambda i: (jnp.minimum(i, ntp - 1), 0)),
                  pl.BlockSpec(logits_s.shape, lambda i: (0, 0))],
        out_specs=[tk, tk, per_tile, per_tile],
        out_shape=[jax.ShapeDtypeStruct((t, TOP_K), F32), jax.ShapeDtypeStruct((t, TOP_K), I32),
                   jax.ShapeDtypeStruct((nt, 1, N_EXPERTS), I32), jax.ShapeDtypeStruct((nt, 1, N_EXPERTS), I32)],
        compiler_params=_cparams("arbitrary"),
        name="moe_route",
    )(logits_p, logits_s)


def _sorted_rows(tt):
    return tt * TOP_K + N_EXPERTS * SUBLANES


def _run_copies(tt, tile, cnt_ref, off_ref, base_ref, make_copy, wait):
    if wait:
        total = off_ref[tile, N_EXPERTS - 1] + cnt_ref[tile, N_EXPERTS - 1]

        @pl.when(total > 0)
        def _():
            make_copy(0, 0, pl.multiple_of(total, SUBLANES)).wait()
        return

    def per_expert(e, carry):
        n = cnt_ref[tile, e]

        @pl.when(n > 0)
        def _():
            make_copy(pl.multiple_of(off_ref[tile, e], SUBLANES),
                      pl.multiple_of(base_ref[tile, e], SUBLANES), pl.multiple_of(n, SUBLANES)).start()
        return carry

    lax.fori_loop(0, N_EXPERTS, per_expert, 0)


def _dispatch_body(tm, tt, cnt_ref, off_ref, base_ref, zstart_ref, loc_ref, x_ref, xs_ref, o_hbm,
                   zero_ref, srt_ref, zsem, sems):
    i = pl.program_id(0)
    last = pl.num_programs(0) - 1
    r = _sorted_rows(tt)

    def zero_copy(j):
        return pltpu.make_async_copy(zero_ref, o_hbm.at[pl.ds(pl.multiple_of(zstart_ref[j], tm), tm), :], zsem)

    @pl.when(i == 0)
    def _():
        zero_ref[...] = jnp.zeros_like(zero_ref)

        def start(j, carry):
            @pl.when(zstart_ref[j] >= 0)
            def _():
                zero_copy(j).start()
            return carry

        def wait(j, carry):
            @pl.when(zstart_ref[j] >= 0)
            def _():
                zero_copy(j).wait()
            return carry

        lax.fori_loop(0, zstart_ref.shape[0], start, 0)
        lax.fori_loop(0, zstart_ref.shape[0], wait, 0)

    loc = loc_ref[...]
    slot_i = lax.broadcasted_iota(I32, (tt, r), 1)
    hit = slot_i == loc[:, 0:1]
    for k in range(1, TOP_K):
        hit = hit | (slot_i == loc[:, k:k + 1])
    buf = i % 2
    x = jnp.where(i == last, _pad_rows(xs_ref[...], tt), x_ref[...])
    srt_ref[buf] = lax.dot_general(hit.astype(BF16), x.astype(BF16), TN_DIMS, preferred_element_type=F32)

    def copies(tile, wait):
        b = tile % 2

        def make_copy(lo, go, size):
            return pltpu.make_async_copy(srt_ref.at[b, pl.ds(lo, size), :], o_hbm.at[pl.ds(go, size), :],
                                         sems.at[b])

        _run_copies(tt, tile, cnt_ref, off_ref, base_ref, make_copy, wait)

    copies(i, False)

    @pl.when(i > 0)
    def _():
        copies(i - 1, True)

    @pl.when(i == last)
    def _():
        copies(i, True)


def _dispatch(cnt, off, base, zero_starts, loc, hn_p, hn_s, n_rows, tm, tt):
    ntp = hn_p.shape[0] // tt
    smem = pl.BlockSpec(memory_space=pltpu.SMEM)
    return pl.pallas_call(
        functools.partial(_dispatch_body, tm, tt),
        grid_spec=pltpu.PrefetchScalarGridSpec(
            num_scalar_prefetch=0, grid=(ntp + 1,),
            in_specs=[smem, smem, smem, smem,
                      pl.BlockSpec((tt, TOP_K), lambda i: (i, 0)),
                      pl.BlockSpec((tt, D_MODEL), lambda i: (jnp.minimum(i, ntp - 1), 0)),
                      pl.BlockSpec(hn_s.shape, lambda i: (0, 0))],
            out_specs=pl.BlockSpec(memory_space=pl.ANY),
            scratch_shapes=[pltpu.VMEM((tm, D_MODEL), F32), pltpu.VMEM((2, _sorted_rows(tt), D_MODEL), F32),
                            pltpu.SemaphoreType.DMA, pltpu.SemaphoreType.DMA((2,))]),
        out_shape=jax.ShapeDtypeStruct((n_rows, D_MODEL), F32),
        compiler_params=_cparams("arbitrary"),
        name="moe_dispatch",
    )(cnt, off, base, zero_starts, loc, hn_p, hn_s)


def _expert_body(be_ref, nu_ref, bv_ref, slot_ref, nxt_ref, x_ref, wgu_hbm, bgu_ref, wdn_hbm, bdn_ref,
                 y_ref, wgu_f, wdn_f, wgu_b, wdn_b, sems):
    i = pl.program_id(0)
    tm = x_ref.shape[0]
    valid = bv_ref[i]
    expert = be_ref[i]
    slot = slot_ref[i]

    def weight_copies(e, s):
        return (pltpu.make_async_copy(wgu_hbm.at[e], wgu_f.at[s], sems.at[0, s]),
                pltpu.make_async_copy(wdn_hbm.at[e], wdn_f.at[s], sems.at[1, s]))

    @pl.when(i == 0)
    def _():
        for cp in weight_copies(expert, slot):
            cp.start()

    @pl.when(jnp.logical_and(i < nu_ref[0], jnp.logical_or(i == 0, expert != be_ref[jnp.maximum(i - 1, 0)])))
    def _():
        for cp in weight_copies(expert, slot):
            cp.wait()
        wgu_b[...] = wgu_f[slot].astype(BF16)
        wdn_b[...] = wdn_f[slot].astype(BF16)

        @pl.when(nxt_ref[i] >= 0)
        def _():
            for cp in weight_copies(nxt_ref[i], 1 - slot):
                cp.start()

    def ffn(rows):
        gu = _dot(x_ref[0:rows, :].astype(BF16), wgu_b[...]) + bgu_ref[...]
        gate = jnp.minimum(gu[:, :D_FF], SWIGLU_LIMIT)
        up = jnp.clip(gu[:, D_FF:], -SWIGLU_LIMIT, SWIGLU_LIMIT)
        act = (up + 1.0) * (gate * _sigmoid(SWIGLU_ALPHA * gate))
        y_ref[0:rows, :] = _dot(act.astype(BF16), wdn_b[...]) + bdn_ref[...]
        if rows < tm:
            y_ref[rows:tm, :] = jnp.zeros((tm - rows, D_MODEL), F32)

    quarter = tm // EXPERT_ROW_SPLITS
    for q in range(1, EXPERT_ROW_SPLITS + 1):
        @pl.when(jnp.logical_and(valid > (q - 1) * quarter, valid <= q * quarter))
        def _(q=q):
            ffn(q * quarter)

    @pl.when(valid == 0)
    def _():
        y_ref[...] = jnp.zeros_like(y_ref)


def _experts(block_e, n_used, block_valid, block_slot, block_next, xs, wgu, bgu, wdn, bdn, tm):
    n_rows = xs.shape[0]
    return pl.pallas_call(
        _expert_body,
        grid_spec=pltpu.PrefetchScalarGridSpec(
            num_scalar_prefetch=5, grid=(n_rows // tm,),
            in_specs=[pl.BlockSpec((tm, D_MODEL), lambda i, be, nu, *_: (jnp.minimum(i, nu[0] - 1), 0)),
                      pl.BlockSpec(memory_space=pl.ANY),
                      pl.BlockSpec((None, 1, 2 * D_FF), lambda i, be, *_: (be[i], 0, 0)),
                      pl.BlockSpec(memory_space=pl.ANY),
                      pl.BlockSpec((None, 1, D_MODEL), lambda i, be, *_: (be[i], 0, 0))],
            out_specs=pl.BlockSpec((tm, D_MODEL), lambda i, *_: (i, 0)),
            scratch_shapes=[pltpu.VMEM((2, D_MODEL, 2 * D_FF), F32), pltpu.VMEM((2, D_FF, D_MODEL), F32),
                            pltpu.VMEM((D_MODEL, 2 * D_FF), BF16), pltpu.VMEM((D_FF, D_MODEL), BF16),
                            pltpu.SemaphoreType.DMA((2, 2))]),
        out_shape=jax.ShapeDtypeStruct((n_rows, D_MODEL), F32),
        compiler_params=_cparams("arbitrary"),
        name="moe_experts",
    )(block_e, n_used, block_valid, block_slot, block_next, xs, wgu, bgu, wdn, bdn)


def _combine_body(tt, cnt_ref, off_ref, base_ref, loc_ref, g_ref, h_ref, hs_ref, gfin_ref, ys_hbm,
                  y_ref, ysmp_ref, buf_ref, sems):
    i = pl.program_id(0)
    last = pl.num_programs(0) - 1
    r = _sorted_rows(tt)

    def copies(tile, wait):
        b = tile % 2

        def make_copy(lo, go, size):
            return pltpu.make_async_copy(ys_hbm.at[pl.ds(go, size), :], buf_ref.at[b, pl.ds(lo, size), :],
                                         sems.at[b])

        _run_copies(tt, tile, cnt_ref, off_ref, base_ref, make_copy, wait)

    @pl.when(i == 0)
    def _():
        buf_ref[...] = jnp.zeros_like(buf_ref)
        copies(0, False)

    @pl.when(i < last)
    def _():
        copies(i + 1, False)

    copies(i, True)
    loc = loc_ref[...]
    gates = g_ref[...]
    slot_i = lax.broadcasted_iota(I32, (tt, r), 1)
    gmat = jnp.where(slot_i == loc[:, 0:1], gates[:, 0:1], 0.0)
    for k in range(1, TOP_K):
        gmat = gmat + jnp.where(slot_i == loc[:, k:k + 1], gates[:, k:k + 1], 0.0)
    h = jnp.where(i == last, _pad_rows(hs_ref[...], tt), h_ref[...])
    y = _rms(h + _dot(gmat.astype(BF16), buf_ref[i % 2].astype(BF16)), gfin_ref[...])

    @pl.when(i < last)
    def _():
        y_ref[...] = y

    @pl.when(i == last)
    def _():
        ysmp_ref[...] = y[0:ysmp_ref.shape[0], :]


def _combine(cnt, off, base, loc, gates, h2_p, h2_s, norm_final, ys, tt):
    ntp = h2_p.shape[0] // tt
    tok = pl.BlockSpec((tt, D_MODEL), lambda i: (jnp.minimum(i, ntp - 1), 0))
    smp = pl.BlockSpec(h2_s.shape, lambda i: (0, 0))
    tk = pl.BlockSpec((tt, TOP_K), lambda i: (i, 0))
    smem = pl.BlockSpec(memory_space=pltpu.SMEM)
    return pl.pallas_call(
        functools.partial(_combine_body, tt),
        grid_spec=pltpu.PrefetchScalarGridSpec(
            num_scalar_prefetch=0, grid=(ntp + 1,),
            in_specs=[smem, smem, smem, tk, tk, tok, smp,
                      pl.BlockSpec((1, D_MODEL), lambda i: (0, 0)),
                      pl.BlockSpec(memory_space=pl.ANY)],
            out_specs=[tok, smp],
            scratch_shapes=[pltpu.VMEM((2, _sorted_rows(tt), D_MODEL), F32), pltpu.SemaphoreType.DMA((2,))]),
        out_shape=[jax.ShapeDtypeStruct(h2_p.shape, F32), jax.ShapeDtypeStruct(h2_s.shape, F32)],
        compiler_params=_cparams("arbitrary"),
        name="moe_combine",
    )(cnt, off, base, loc, gates, h2_p, h2_s, norm_final, ys)


def _moe_and_final_norm(hn_p, logits_p, h2_p, hn_s, logits_s, h2_s, w, tt, tm):
    nt = hn_p.shape[0] // tt + 1
    t = hn_p.shape[0] + hn_s.shape[0]
    gates, loc, cnt3, off3 = _route(logits_p, logits_s, tt)
    cnt = cnt3[:, 0, :]
    counts = jnp.sum(cnt, axis=0)
    padded = (counts + tm - 1) // tm * tm
    pad_end = jnp.cumsum(padded)
    start = pad_end - padded
    off = off3[:, 0, :]
    base = (start[None, :] + jnp.cumsum(cnt, axis=0) - cnt).astype(I32)
    n_blocks = (t * TOP_K + nt * N_EXPERTS * (SUBLANES - 1) + N_EXPERTS * (tm - 1)) // tm
    n_rows = n_blocks * tm
    block_start = jnp.arange(n_blocks, dtype=I32) * tm
    block_e = jnp.minimum(jnp.sum(block_start[:, None] >= pad_end[None, :], axis=-1), N_EXPERTS - 1).astype(I32)
    n_used = (pad_end[-1:] // tm).astype(I32)
    zero_starts = jnp.concatenate([jnp.where(padded > 0, pad_end - tm, -1),
                                   jnp.where(block_start >= pad_end[-1], block_start, -1)]).astype(I32)
    xs = _dispatch(cnt, off, base, zero_starts, loc, hn_p, hn_s, n_rows, tm, tt)
    block_valid = jnp.clip((start + counts)[block_e] - block_start, 0, tm)
    block_valid = jnp.where(block_start < pad_end[-1], block_valid, 0).astype(I32)
    present = padded > 0
    e_ids = jnp.arange(N_EXPERTS, dtype=I32)
    later = present[None, :] & (e_ids[None, :] > e_ids[:, None])
    next_e = jnp.where(jnp.any(later, axis=1), jnp.argmax(later, axis=1), -1).astype(I32)
    run_slot = ((jnp.cumsum(present.astype(I32)) - 1) % 2).astype(I32)
    ys = _experts(block_e, n_used, block_valid, run_slot[block_e], next_e[block_e], xs, w["w_gate_up"], w["b_gate_up"], w["w_down"], w["b_down"], tm)
    return _combine(cnt, off, base, loc, gates, h2_p, h2_s, w["norm_final"], ys, tt)


def kernel(x_prompt, x_sample, mem_prompt, state_ssm, state_mamba_conv, state_short_conv, cache_mem_k, cache_mem_v, norm_mix, w_in, w_mconv, b_mconv, dt_bias, a_log, d_skip, norm_ssm, w_sconv, w_out, norm_xattn, norm_mem, w_xq, w_xk, w_xv, w_xo, norm_moe, w_router, b_router, w_gate_up, b_gate_up, w_down, b_down, norm_final):
    nbp, seq, _ = x_prompt.shape
    nbs = x_sample.shape[0]
    dt_lo = SSM_INNER + SSM_CONV_DIM
    w_in0 = w_in[0]
    w_dt = w_in0[:, dt_lo:dt_lo + SSM_HEADS]
    w = {
        "norm_mix": norm_mix, "norm_ssm": norm_ssm, "norm_xattn": norm_xattn, "norm_moe": norm_moe,
        "norm_final": norm_final.reshape(1, D_MODEL),
        "w_a": w_in0[:, :dt_lo].astype(BF16),
        "w_dt": w_dt.astype(BF16), "w_dt_t": w_dt.T.astype(BF16),
        "w_b": w_in0[:, dt_lo + SSM_HEADS:].astype(BF16),
        "w_mconv": w_mconv[0], "b_mconv": b_mconv,
        "dt_bias": dt_bias, "dt_bias_t": dt_bias.reshape(SSM_HEADS, 1),
        "a_log": a_log, "a_log_t": a_log.reshape(SSM_HEADS, 1),
        "d_skip": jnp.repeat(d_skip, SSM_HEAD_DIM, axis=1),
        "w_sconv": w_sconv[0], "w_out": w_out[0].astype(BF16),
        "w_xq": w_xq[0].astype(BF16), "w_xo": w_xo[0].astype(BF16),
        "w_router": w_router[0].astype(BF16), "b_router": b_router,
        "w_gate_up": w_gate_up[0], "b_gate_up": b_gate_up[0].reshape(N_EXPERTS, 1, 2 * D_FF),
        "w_down": w_down[0], "b_down": b_down[0].reshape(N_EXPERTS, 1, D_MODEL),
    }

    k_p, v_p, kb, vb = _mem_kv(mem_prompt.reshape(nbp * N_MEM, D_MODEL), norm_mem,
                               w_xk[0].astype(BF16), w_xv[0].astype(BF16))
    h1, ssm_p, mconv_p, sconv_p = _prompt_mixer(x_prompt.reshape(nbp * seq, D_MODEL), nbp, w)
    h2, hn, logits = _prompt_attn(h1, kb, vb, nbp, w)

    xs2 = x_sample.reshape(nbs, D_MODEL)
    mstate_t = jnp.transpose(state_mamba_conv[0], (1, 0, 2))
    sstate_t = jnp.transpose(state_short_conv[0], (1, 0, 2))
    z, xs_, dtx, dec, bm, cm, yb, sga, mnew_t, snew_t = _sample_proj(xs2, mstate_t, sstate_t, w)
    ssm_s, y_s = _sample_state(dec, state_ssm[0].reshape(nbs, SSM_INNER, SSM_STATE), dtx, bm, cm)
    h1s, q_s = _sample_fin1(xs2, y_s, xs_, z, yb, sga, w)
    o_s = _sample_attn(q_s.reshape(nbs, 1, D_MODEL),
                       cache_mem_k[0], cache_mem_v[0])
    h2s, hns, logits_s = _sample_fin2(h1s, o_s.reshape(nbs, D_MODEL), w)
    y_prompt, y_sample = _moe_and_final_norm(hn, logits, h2, hns, logits_s, h2s, w, MIX_TILE, MOE_ROW_TILE)

    return (y_prompt.reshape(nbp, seq, D_MODEL),
            y_sample.reshape(nbs, 1, D_MODEL),
            ssm_p.reshape(1, nbp, SSM_HEADS, SSM_HEAD_DIM, SSM_STATE),
            mconv_p[None], sconv_p[None],
            k_p.reshape(1, nbp, N_MEM, XA_HEADS, XA_HEAD_DIM),
            v_p.reshape(1, nbp, N_MEM, XA_HEADS, XA_HEAD_DIM),
            ssm_s.reshape(1, nbs, SSM_HEADS, SSM_HEAD_DIM, SSM_STATE),
            jnp.transpose(mnew_t, (1, 0, 2))[None],
            jnp.transpose(snew_t, (1, 0, 2))[None])
```

```python
import functools

import jax
import jax.numpy as jnp
from jax import lax
from jax.experimental import pallas as pl
from jax.experimental.pallas import tpu as pltpu

F32 = jnp.float32
BF16 = jnp.bfloat16
I32 = jnp.int32

D_MODEL = 1024
N_MEM = 256
SSM_HEADS = 16
SSM_HEAD_DIM = 64
SSM_INNER = SSM_HEADS * SSM_HEAD_DIM
SSM_STATE = 128
SSM_GROUPS = 4
HEADS_PER_GROUP = SSM_HEADS // SSM_GROUPS
GROUP_WIDTH = SSM_INNER // SSM_GROUPS
SSM_CONV = 4
SSM_CONV_DIM = SSM_INNER + 2 * SSM_GROUPS * SSM_STATE
SC_CONV = 3
XA_HEADS = 4
XA_HEAD_DIM = D_MODEL // XA_HEADS
N_EXPERTS = 32
TOP_K = 4
D_FF = D_MODEL
SWIGLU_LIMIT = 7.0
SWIGLU_ALPHA = 1.702
EPS = 1e-6

LANES = 128
SUBLANES = 8
VMEM_LIMIT = 56 * 1024 * 1024

MIX_TILE = 256
MOE_ROW_TILE = 512
EXPERT_ROW_SPLITS = 4
STATE_BB = 8
ATTN_BB = 4

NT_DIMS = (((1,), (1,)), ((), ()))
TN_DIMS = (((0,), (0,)), ((), ()))


def _cparams(*sem):
    return pltpu.CompilerParams(dimension_semantics=sem, vmem_limit_bytes=VMEM_LIMIT)


def _const_spec(shape):
    nd = len(shape)
    return pl.BlockSpec(shape, lambda *_: (0,) * nd, pipeline_mode=pl.Buffered(1))


def _sigmoid(x):
    return 1.0 / (1.0 + jnp.exp(-x))


def _silu(x):
    return x * _sigmoid(x)


def _softplus(x):
    return jnp.maximum(x, 0.0) + jnp.log(1.0 + jnp.exp(-jnp.abs(x)))


def _rms(x, g):
    ms = jnp.mean(x * x, axis=-1, keepdims=True)
    return x * lax.rsqrt(ms + EPS) * g


def _dot(a, b):
    return jnp.dot(a, b, preferred_element_type=F32)


def _dot_nt(a, b):
    return lax.dot_general(a, b, NT_DIMS, preferred_element_type=F32)


def _expand_heads(v):
    rows = v.shape[0]
    lane = lax.broadcasted_iota(I32, (rows, LANES), 1)
    pieces = []
    for j in range(SSM_HEADS // 2):
        a = jnp.broadcast_to(v[:, 2 * j:2 * j + 1], (rows, LANES))
        b = jnp.broadcast_to(v[:, 2 * j + 1:2 * j + 2], (rows, LANES))
        pieces.append(jnp.where(lane < SSM_HEAD_DIM, a, b))
    return jnp.concatenate(pieces, axis=1)


def _cumsum(x, axis):
    idx = lax.broadcasted_iota(I32, x.shape, axis)
    shift = 1
    while shift < x.shape[axis]:
        x = x + jnp.where(idx >= shift, pltpu.roll(x, shift, axis), 0.0)
        shift *= 2
    return x


def _pad_rows(x, rows):
    return jnp.concatenate([x, jnp.zeros((rows - x.shape[0], x.shape[1]), x.dtype)], axis=0)


def _group_rmsnorm(u, g):
    outs = []
    for k in range(SSM_GROUPS):
        ug = u[:, k * GROUP_WIDTH:(k + 1) * GROUP_WIDTH]
        ms = jnp.mean(ug * ug, axis=-1, keepdims=True)
        outs.append(ug * lax.rsqrt(ms + EPS))
    return jnp.concatenate(outs, axis=1) * g


def _memkv_body(mem_ref, g_ref, wk_ref, wv_ref, k_ref, v_ref, kb_ref, vb_ref):
    mn = _rms(mem_ref[...], g_ref[...]).astype(BF16)
    k = _dot(mn, wk_ref[...])
    v = _dot(mn, wv_ref[...])
    k_ref[...] = k
    v_ref[...] = v
    kb_ref[...] = k.astype(BF16)
    vb_ref[...] = v.astype(BF16)


def _mem_kv(mem2d, norm_mem, wk, wv):
    rows = mem2d.shape[0]
    nb = rows // N_MEM
    blk = pl.BlockSpec((N_MEM, D_MODEL), lambda b: (b, 0))
    return pl.pallas_call(
        _memkv_body,
        grid=(nb,),
        in_specs=[blk, _const_spec((1, D_MODEL)), _const_spec((D_MODEL, D_MODEL)),
                  _const_spec((D_MODEL, D_MODEL))],
        out_specs=[blk, blk, blk, blk],
        out_shape=[jax.ShapeDtypeStruct((rows, D_MODEL), F32)] * 2
        + [jax.ShapeDtypeStruct((rows, D_MODEL), BF16)] * 2,
        compiler_params=_cparams("arbitrary"),
        name="mem_kv",
    )(mem2d, norm_mem, wk, wv)


def _mix_body(x_ref, gmix_ref, wa_ref, wdtc_ref, wdtr_ref, wb_ref, wmc_ref, bmc_ref,
              dtb_ref, dtbt_ref, alog_ref, alogt_ref, dskip_ref, gssm_ref, wsc_ref, wout_ref,
              h_ref, ssm_ref, mbuf_ref, sbuf_ref,
              st_ref, cbuf_ref, scbuf_ref):
    tq = MIX_TILE
    c = pl.program_id(1)

    @pl.when(c == 0)
    def _():
        st_ref[...] = jnp.zeros_like(st_ref)
        cbuf_ref[0:SUBLANES, :] = jnp.zeros((SUBLANES, SSM_CONV_DIM), F32)
        scbuf_ref[0:SUBLANES, :] = jnp.zeros((SUBLANES, D_MODEL), F32)

    x = x_ref[...]
    xn = _rms(x, gmix_ref[...]).astype(BF16)

    u = _dot(xn, wa_ref[:, SSM_INNER:])
    cbuf_ref[SUBLANES:SUBLANES + tq, :] = u
    wm = wmc_ref[...]
    conv = u * wm[SSM_CONV - 1:SSM_CONV, :] + bmc_ref[...]
    for k in range(SSM_CONV - 1):
        off = SUBLANES - (SSM_CONV - 1) + k
        conv = conv + cbuf_ref[off:off + tq, :] * wm[k:k + 1, :]
    tail = cbuf_ref[tq + SUBLANES - (SSM_CONV - 1):tq + SUBLANES, :]
    mbuf_ref[...] = tail
    cbuf_ref[SUBLANES - (SSM_CONV - 1):SUBLANES, :] = tail
    xbc = _silu(conv)
    xs = xbc[:, :SSM_INNER]
    bm = xbc[:, SSM_INNER:SSM_INNER + SSM_GROUPS * SSM_STATE]
    cm = xbc[:, SSM_INNER + SSM_GROUPS * SSM_STATE:]

    dt = _softplus(_dot(xn, wdtc_ref[...]) + dtb_ref[...])
    dtt = _softplus(_dot_nt(wdtr_ref[...], xn) + dtbt_ref[...])
    a_row = -jnp.exp(alog_ref[...])
    a_col = -jnp.exp(alogt_ref[...])
    row_i = lax.broadcasted_iota(I32, (tq, tq), 0)
    col_i = lax.broadcasted_iota(I32, (tq, tq), 1)
    causal = row_i >= col_i
    a_cum = _cumsum(dt * a_row, 0)
    a_cumt = _cumsum(dtt * a_col, 1)
    a_last = a_cum[tq - 1:tq, :]

    xdt = xs * _expand_heads(dt)
    in_decay = _expand_heads(jnp.exp(a_cum))
    to_end = _expand_heads(jnp.exp(a_last - a_cum))
    chunk_decay = _expand_heads(jnp.exp(a_last))
    xdt_b = xdt.astype(BF16)
    xend_b = (xdt * to_end).astype(BF16)
    lane = lax.broadcasted_iota(I32, (tq, LANES), 1)

    def proj_b(k):
        return _dot(xn, wb_ref[:, k * D_MODEL:(k + 1) * D_MODEL])

    pb = []
    y_groups = []
    for g in range(SSM_GROUPS):
        pb.append(proj_b(g))
        if g == 0:
            z = _dot(xn, wa_ref[:, :SSM_INNER])
        if g == 2:
            g_b = proj_b(SSM_GROUPS)
        cg = cm[:, g * SSM_STATE:(g + 1) * SSM_STATE].astype(BF16)
        bg_f = bm[:, g * SSM_STATE:(g + 1) * SSM_STATE]
        bg = bg_f.astype(BF16)
        scores = _dot_nt(cg, bg)
        gs = slice(g * GROUP_WIDTH, (g + 1) * GROUP_WIDTH)
        st_g = st_ref[:, gs]
        y_off = _dot(cg, st_g.astype(BF16)) * in_decay[:, gs]
        pair_out = []
        for pr in range(HEADS_PER_GROUP // 2):
            h0 = g * HEADS_PER_GROUP + 2 * pr
            xp = xdt_b[:, h0 * SSM_HEAD_DIM:(h0 + 2) * SSM_HEAD_DIM]
            ys = []
            for h in (h0, h0 + 1):
                seg = a_cum[:, h:h + 1] - a_cumt[h:h + 1, :]
                decay = jnp.where(causal, jnp.exp(jnp.minimum(seg, 0.0)), 0.0)
                ys.append(_dot((scores * decay).astype(BF16), xp))
            pair_out.append(jnp.where(lane < SSM_HEAD_DIM, ys[0], ys[1]))
        y_groups.append(jnp.concatenate(pair_out, axis=1) + y_off)
        st_ref[:, gs] = st_g * chunk_decay[:, gs] + _dot(bg_f.T.astype(BF16), xend_b[:, gs])
    y = jnp.concatenate(y_groups, axis=1) + dskip_ref[...] * xs
    y_a = _group_rmsnorm(y * _silu(z), gssm_ref[...])

    sc_b, sc_c, sc_v, g_a = pb
    cv = sc_c * sc_v
    scbuf_ref[SUBLANES:SUBLANES + tq, :] = cv
    ws = wsc_ref[...]
    uc = cv * ws[SC_CONV - 1:SC_CONV, :]
    for k in range(SC_CONV - 1):
        off = SUBLANES - (SC_CONV - 1) + k
        uc = uc + scbuf_ref[off:off + tq, :] * ws[k:k + 1, :]
    stail = scbuf_ref[tq + SUBLANES - (SC_CONV - 1):tq + SUBLANES, :]
    sbuf_ref[...] = stail
    scbuf_ref[SUBLANES - (SC_CONV - 1):SUBLANES, :] = stail
    merged = _sigmoid(g_a) * y_a + _sigmoid(g_b) * (sc_b * uc)
    h_ref[...] = x + _dot(merged.astype(BF16), wout_ref[...])

    @pl.when(c == pl.num_programs(1) - 1)
    def _():
        ssm_ref[...] = st_ref[...].T


def _prompt_mixer(x2d, nb, w):
    t = x2d.shape[0]
    nc = t // nb // MIX_TILE
    tok = pl.BlockSpec((MIX_TILE, D_MODEL), lambda b, c: (b * nc + c, 0))
    return pl.pallas_call(
        _mix_body,
        grid=(nb, nc),
        in_specs=[tok, _const_spec((1, D_MODEL)),
                  _const_spec((D_MODEL, SSM_INNER + SSM_CONV_DIM)),
                  _const_spec((D_MODEL, SSM_HEADS)), _const_spec((SSM_HEADS, D_MODEL)),
                  _const_spec((D_MODEL, 5 * D_MODEL)),
                  _const_spec((SSM_CONV, SSM_CONV_DIM)), _const_spec((1, SSM_CONV_DIM)),
                  _const_spec((1, SSM_HEADS)), _const_spec((SSM_HEADS, 1)),
                  _const_spec((1, SSM_HEADS)), _const_spec((SSM_HEADS, 1)),
                  _const_spec((1, SSM_INNER)), _const_spec((1, SSM_INNER)),
                  _const_spec((SC_CONV, D_MODEL)), _const_spec((D_MODEL, D_MODEL))],
        out_specs=[tok,
                   pl.BlockSpec((None, SSM_INNER, SSM_STATE), lambda b, c: (b, 0, 0)),
                   pl.BlockSpec((None, SSM_CONV - 1, SSM_CONV_DIM), lambda b, c: (b, 0, 0)),
                   pl.BlockSpec((None, SC_CONV - 1, D_MODEL), lambda b, c: (b, 0, 0))],
        out_shape=[jax.ShapeDtypeStruct((t, D_MODEL), F32),
                   jax.ShapeDtypeStruct((nb, SSM_INNER, SSM_STATE), F32),
                   jax.ShapeDtypeStruct((nb, SSM_CONV - 1, SSM_CONV_DIM), F32),
                   jax.ShapeDtypeStruct((nb, SC_CONV - 1, D_MODEL), F32)],
        scratch_shapes=[pltpu.VMEM((SSM_STATE, SSM_INNER), F32),
                        pltpu.VMEM((MIX_TILE + SUBLANES, SSM_CONV_DIM), F32),
                        pltpu.VMEM((MIX_TILE + SUBLANES, D_MODEL), F32)],
        compiler_params=_cparams("arbitrary", "arbitrary"),
        name="prompt_mixer",
    )(x2d, w["norm_mix"], w["w_a"], w["w_dt"], w["w_dt_t"], w["w_b"], w["w_mconv"], w["b_mconv"],
      w["dt_bias"], w["dt_bias_t"], w["a_log"], w["a_log_t"], w["d_skip"], w["norm_ssm"],
      w["w_sconv"], w["w_out"])


def _router_tail(h2, gmoe_ref, wr_ref, br_ref, h2_ref, hn_ref, lg_ref):
    h2_ref[...] = h2
    hn = _rms(h2, gmoe_ref[...])
    hn_ref[...] = hn
    lg_ref[...] = _dot(hn.astype(BF16), wr_ref[...]) + br_ref[...]


def _attn_body(h_ref, gx_ref, wq_ref, k_ref, v_ref, wo_ref, gmoe_ref, wr_ref, br_ref,
               h2_ref, hn_ref, lg_ref):
    h = h_ref[...]
    hn = _rms(h, gx_ref[...]).astype(BF16)
    q = _dot(hn, wq_ref[...]).astype(BF16)
    outs = []
    for hd in range(XA_HEADS):
        sl = slice(hd * XA_HEAD_DIM, (hd + 1) * XA_HEAD_DIM)
        s = _dot_nt(q[:, sl], k_ref[:, sl]) * (XA_HEAD_DIM ** -0.5)
        e = jnp.exp(s - jnp.max(s, axis=-1, keepdims=True))
        p = e / jnp.sum(e, axis=-1, keepdims=True)
        outs.append(_dot(p.astype(BF16), v_ref[:, sl]))
    o = jnp.concatenate(outs, axis=1).astype(BF16)
    h2 = h + _dot(o, wo_ref[...])
    _router_tail(h2, gmoe_ref, wr_ref, br_ref, h2_ref, hn_ref, lg_ref)


def _prompt_attn(h1, kb, vb, nb, w):
    t = h1.shape[0]
    nc = t // nb // MIX_TILE
    tok = pl.BlockSpec((MIX_TILE, D_MODEL), lambda b, c: (b * nc + c, 0))
    kv = pl.BlockSpec((N_MEM, D_MODEL), lambda b, c: (b, 0))
    return pl.pallas_call(
        _attn_body,
        grid=(nb, nc),
        in_specs=[tok, _const_spec((1, D_MODEL)), _const_spec((D_MODEL, D_MODEL)), kv, kv,
                  _const_spec((D_MODEL, D_MODEL)), _const_spec((1, D_MODEL)),
                  _const_spec((D_MODEL, N_EXPERTS)), _const_spec((1, N_EXPERTS))],
        out_specs=[tok, tok, pl.BlockSpec((MIX_TILE, N_EXPERTS), lambda b, c: (b * nc + c, 0))],
        out_shape=[jax.ShapeDtypeStruct((t, D_MODEL), F32),
                   jax.ShapeDtypeStruct((t, D_MODEL), F32),
                   jax.ShapeDtypeStruct((t, N_EXPERTS), F32)],
        compiler_params=_cparams("arbitrary", "arbitrary"),
        name="prompt_attn",
    )(h1, w["norm_xattn"], w["w_xq"], kb, vb, w["w_xo"], w["norm_moe"], w["w_router"], w["b_router"])


def _sproj_body(x_ref, gmix_ref, wa_ref, wdtc_ref, wb_ref, wmc_ref, bmc_ref, dtb_ref, alog_ref,
                wsc_ref, mst_ref, sst_ref,
                z_ref, xs_ref, dtx_ref, dec_ref, bm_ref, cm_ref, yb_ref, sga_ref, mnew_ref, snew_ref):
    x = x_ref[...]
    xn = _rms(x, gmix_ref[...]).astype(BF16)
    pa = _dot(xn, wa_ref[...])
    z_ref[...] = pa[:, :SSM_INNER]
    u = pa[:, SSM_INNER:]
    wm = wmc_ref[...]
    conv = u * wm[SSM_CONV - 1:SSM_CONV, :] + bmc_ref[...]
    for k in range(SSM_CONV - 1):
        conv = conv + mst_ref[k] * wm[k:k + 1, :]
    for k in range(SSM_CONV - 2):
        mnew_ref[k] = mst_ref[k + 1]
    mnew_ref[SSM_CONV - 2] = u
    xbc = _silu(conv)
    xs = xbc[:, :SSM_INNER]
    xs_ref[...] = xs
    bm_ref[...] = xbc[:, SSM_INNER:SSM_INNER + SSM_GROUPS * SSM_STATE]
    cm_ref[...] = xbc[:, SSM_INNER + SSM_GROUPS * SSM_STATE:]
    dt = _softplus(_dot(xn, wdtc_ref[...]) + dtb_ref[...])
    dec_ref[...] = jnp.exp(dt * (-jnp.exp(alog_ref[...])))
    dtx_ref[...] = xs * _expand_heads(dt)
    pb = _dot(xn, wb_ref[...])
    cv = pb[:, D_MODEL:2 * D_MODEL] * pb[:, 2 * D_MODEL:3 * D_MODEL]
    ws = wsc_ref[...]
    uc = cv * ws[SC_CONV - 1:SC_CONV, :]
    for k in range(SC_CONV - 1):
        uc = uc + sst_ref[k] * ws[k:k + 1, :]
    for k in range(SC_CONV - 2):
        snew_ref[k] = sst_ref[k + 1]
    snew_ref[SC_CONV - 2] = cv
    yb_ref[...] = _sigmoid(pb[:, 4 * D_MODEL:5 * D_MODEL]) * (pb[:, 0:D_MODEL] * uc)
    sga_ref[...] = _sigmoid(pb[:, 3 * D_MODEL:4 * D_MODEL])


def _sample_proj(x, mstate_t, sstate_t, w):
    nb = x.shape[0]
    f = lambda *s: jax.ShapeDtypeStruct(s, F32)
    return pl.pallas_call(
        _sproj_body,
        out_shape=[f(nb, SSM_INNER), f(nb, SSM_INNER), f(nb, SSM_INNER), f(nb, SSM_HEADS),
                   f(nb, SSM_GROUPS * SSM_STATE), f(nb, SSM_GROUPS * SSM_STATE),
                   f(nb, D_MODEL), f(nb, D_MODEL),
                   f(SSM_CONV - 1, nb, SSM_CONV_DIM), f(SC_CONV - 1, nb, D_MODEL)],
        compiler_params=pltpu.CompilerParams(vmem_limit_bytes=VMEM_LIMIT),
        name="sample_proj",
    )(x, w["norm_mix"], w["w_a"], w["w_dt"], w["w_b"], w["w_mconv"], w["b_mconv"], w["dt_bias"],
      w["a_log"], w["w_sconv"], mstate_t, sstate_t)


def _sstate_body(dec_ref, s_ref, dtx_ref, bm_ref, cm_ref, snew_ref, y_ref):
    i = pl.program_id(0)
    rows_per_blk = LANES
    for j in range(STATE_BB):
        b = i * STATE_BB + j
        dtx_row = dtx_ref[j:j + 1, :]
        y_parts = []
        for g in range(SSM_GROUPS):
            b_row = bm_ref[j:j + 1, g * SSM_STATE:(g + 1) * SSM_STATE]
            c_row = cm_ref[j:j + 1, g * SSM_STATE:(g + 1) * SSM_STATE].astype(BF16)
            new_blocks = []
            for q in range(GROUP_WIDTH // rows_per_blk):
                r0 = g * GROUP_WIDTH + q * rows_per_blk
                dcol = jnp.broadcast_to(dtx_row[:, r0:r0 + rows_per_blk], (rows_per_blk, LANES)).T
                sub = []
                for hh in range(rows_per_blk // SSM_HEAD_DIM):
                    h = r0 // SSM_HEAD_DIM + hh
                    lo = hh * SSM_HEAD_DIM
                    s_old = s_ref[j, r0 + lo:r0 + lo + SSM_HEAD_DIM, :]
                    sub.append(s_old * dec_ref[b, h] + dcol[lo:lo + SSM_HEAD_DIM, :] * b_row)
                blk = jnp.concatenate(sub, axis=0)
                snew_ref[j, r0:r0 + rows_per_blk, :] = blk
                new_blocks.append(blk.astype(BF16))
            s_g = jnp.concatenate(new_blocks, axis=0)
            y_parts.append(_dot_nt(c_row, s_g))
        y_ref[j:j + 1, :] = jnp.concatenate(y_parts, axis=1)


def _sample_state(dec, state, dtx, bm, cm):
    nb = state.shape[0]
    row = lambda wdt: pl.BlockSpec((STATE_BB, wdt), lambda i, dec: (i, 0))
    st = pl.BlockSpec((STATE_BB, SSM_INNER, SSM_STATE), lambda i, dec: (i, 0, 0))
    return pl.pallas_call(
        _sstate_body,
        grid_spec=pltpu.PrefetchScalarGridSpec(
            num_scalar_prefetch=1, grid=(nb // STATE_BB,),
            in_specs=[st, row(SSM_INNER), row(SSM_GROUPS * SSM_STATE), row(SSM_GROUPS * SSM_STATE)],
            out_specs=[st, row(SSM_INNER)]),
        out_shape=[jax.ShapeDtypeStruct(state.shape, F32), jax.ShapeDtypeStruct((nb, SSM_INNER), F32)],
        compiler_params=_cparams("arbitrary"),
        name="sample_state",
    )(dec, state, dtx, bm, cm)


def _sfin1_body(x_ref, y_ref, xs_ref, z_ref, yb_ref, sga_ref, dskip_ref, gssm_ref, wout_ref,
                gx_ref, wq_ref, h_ref, q_ref):
    y = y_ref[...] + dskip_ref[...] * xs_ref[...]
    y_a = _group_rmsnorm(y * _silu(z_ref[...]), gssm_ref[...])
    merged = sga_ref[...] * y_a + yb_ref[...]
    h = x_ref[...] + _dot(merged.astype(BF16), wout_ref[...])
    h_ref[...] = h
    q_ref[...] = _dot(_rms(h, gx_ref[...]).astype(BF16), wq_ref[...])


def _sample_fin1(x, y, xs, z, yb, sga, w):
    nb = x.shape[0]
    return pl.pallas_call(
        _sfin1_body,
        out_shape=[jax.ShapeDtypeStruct((nb, D_MODEL), F32)] * 2,
        compiler_params=pltpu.CompilerParams(vmem_limit_bytes=VMEM_LIMIT),
        name="sample_fin1",
    )(x, y, xs, z, yb, sga, w["d_skip"], w["norm_ssm"], w["w_out"], w["norm_xattn"], w["w_xq"])


def _sattn_body(q_ref, k_ref, v_ref, o_ref):
    for j in range(ATTN_BB):
        q_row = q_ref[j]
        q4 = jnp.concatenate([q_row[:, h * XA_HEAD_DIM:(h + 1) * XA_HEAD_DIM]
                              for h in range(XA_HEADS)], axis=0)
        s = jnp.sum(k_ref[j] * q4[None], axis=-1, keepdims=True) * (XA_HEAD_DIM ** -0.5)
        e = jnp.exp(s - jnp.max(s, axis=0, keepdims=True))
        p = e / jnp.sum(e, axis=0, keepdims=True)
        o4 = jnp.sum(p * v_ref[j], axis=0)
        o_ref[j] = jnp.concatenate([o4[h:h + 1, :] for h in range(XA_HEADS)], axis=1)


def _sample_attn(q3, k3, v3):
    nb = q3.shape[0]
    qs = pl.BlockSpec((ATTN_BB, 1, D_MODEL), lambda i: (i, 0, 0))
    kv = pl.BlockSpec((ATTN_BB, N_MEM, XA_HEADS, XA_HEAD_DIM), lambda i: (i, 0, 0, 0))
    return pl.pallas_call(
        _sattn_body,
        grid=(nb // ATTN_BB,),
        in_specs=[qs, kv, kv],
        out_specs=qs,
        out_shape=jax.ShapeDtypeStruct((nb, 1, D_MODEL), F32),
        compiler_params=_cparams("arbitrary"),
        name="sample_attn",
    )(q3, k3, v3)


def _sfin2_body(h_ref, o_ref, wo_ref, gmoe_ref, wr_ref, br_ref, h2_ref, hn_ref, lg_ref):
    h2 = h_ref[...] + _dot(o_ref[...].astype(BF16), wo_ref[...])
    _router_tail(h2, gmoe_ref, wr_ref, br_ref, h2_ref, hn_ref, lg_ref)


def _sample_fin2(h1, o, w):
    nb = h1.shape[0]
    return pl.pallas_call(
        _sfin2_body,
        out_shape=[jax.ShapeDtypeStruct((nb, D_MODEL), F32)] * 2
        + [jax.ShapeDtypeStruct((nb, N_EXPERTS), F32)],
        compiler_params=pltpu.CompilerParams(vmem_limit_bytes=VMEM_LIMIT),
        name="sample_fin2",
    )(h1, o, w["w_xo"], w["norm_moe"], w["w_router"], w["b_router"])


def _route_body(lgp_ref, lgs_ref, g_ref, loc_ref, cnt_ref, off_ref):
    tt = lgp_ref.shape[0]
    is_sample = pl.program_id(0) == pl.num_programs(0) - 1
    row = lax.broadcasted_iota(I32, (tt, 1), 0)
    valid = jnp.logical_or(jnp.logical_not(is_sample), row < lgs_ref.shape[0])
    work = jnp.where(is_sample, _pad_rows(lgs_ref[...], tt), lgp_ref[...])
    lane = lax.broadcasted_iota(I32, (tt, N_EXPERTS), 1).astype(F32)
    vals, hots = [], []
    for _ in range(TOP_K):
        m = jnp.max(work, axis=-1, keepdims=True)
        idx = jnp.min(jnp.where(work == m, lane, float(N_EXPERTS)), axis=-1, keepdims=True)
        hot = (lane == idx) & valid
        vals.append(m)
        hots.append(hot)
        work = jnp.where(hot, -jnp.inf, work)
    exps = [jnp.exp(v - vals[0]) for v in vals]
    tot = exps[0]
    for e in exps[1:]:
        tot = tot + e
    assigned = hots[0]
    for hot in hots[1:]:
        assigned = assigned | hot
    a = assigned.astype(BF16)
    r_i = lax.broadcasted_iota(I32, (tt, tt), 0)
    c_i = lax.broadcasted_iota(I32, (tt, tt), 1)
    rank = _dot((r_i > c_i).astype(BF16), a)
    e_r = lax.broadcasted_iota(I32, (N_EXPERTS, N_EXPERTS), 0)
    e_c = lax.broadcasted_iota(I32, (N_EXPERTS, N_EXPERTS), 1)
    cnt = jnp.sum(a.astype(F32), axis=0, keepdims=True)
    cnt = jnp.floor((cnt + (SUBLANES - 1)) * (1.0 / SUBLANES)) * SUBLANES
    cnt_rows = jnp.broadcast_to(cnt, (SUBLANES, N_EXPERTS)).astype(BF16)
    off = _dot(cnt_rows, (e_r < e_c).astype(BF16))[0:1, :]
    slot = rank + off
    k_lane = lax.broadcasted_iota(I32, (tt, TOP_K), 1)
    g_out = jnp.zeros((tt, TOP_K), F32)
    l_out = jnp.zeros((tt, TOP_K), F32)
    for k in range(TOP_K):
        lk = jnp.sum(jnp.where(hots[k], slot, 0.0), axis=-1, keepdims=True)
        g_out = jnp.where(k_lane == k, exps[k] / tot, g_out)
        l_out = jnp.where(k_lane == k, lk, l_out)
    g_ref[...] = jnp.where(valid, g_out, 0.0)
    loc_ref[...] = jnp.where(valid, l_out, -1.0).astype(I32)
    cnt_ref[...] = cnt.astype(I32)
    off_ref[...] = off.astype(I32)


def _route(logits_p, logits_s, tt):
    ntp = logits_p.shape[0] // tt
    nt = ntp + 1
    t = nt * tt
    tk = pl.BlockSpec((tt, TOP_K), lambda i: (i, 0))
    per_tile = pl.BlockSpec((None, 1, N_EXPERTS), lambda i: (i, 0, 0))
    return pl.pallas_call(
        _route_body,
        grid=(nt,),
        in_specs=[pl.BlockSpec((tt, N_EXPERTS), lambda i: (jnp.minimum(i, ntp - 1), 0)),
                  pl.BlockSpec(logits_s.shape, lambda i: (0, 0))],
        out_specs=[tk, tk, per_tile, per_tile],
        out_shape=[jax.ShapeDtypeStruct((t, TOP_K), F32), jax.ShapeDtypeStruct((t, TOP_K), I32),
                   jax.ShapeDtypeStruct((nt, 1, N_EXPERTS), I32), jax.ShapeDtypeStruct((nt, 1, N_EXPERTS), I32)],
        compiler_params=_cparams("arbitrary"),
        name="moe_route",
    )(logits_p, logits_s)


def _sorted_rows(tt):
    return tt * TOP_K + N_EXPERTS * SUBLANES


def _run_copies(tt, tile, cnt_ref, off_ref, base_ref, make_copy, wait):
    if wait:
        total = off_ref[tile, N_EXPERTS - 1] + cnt_ref[tile, N_EXPERTS - 1]

        @pl.when(total > 0)
        def _():
            make_copy(0, 0, pl.multiple_of(total, SUBLANES)).wait()
        return

    def per_expert(e, carry):
        n = cnt_ref[tile, e]

        @pl.when(n > 0)
        def _():
            make_copy(pl.multiple_of(off_ref[tile, e], SUBLANES),
                      pl.multiple_of(base_ref[tile, e], SUBLANES), pl.multiple_of(n, SUBLANES)).start()
        return carry

    lax.fori_loop(0, N_EXPERTS, per_expert, 0)


def _dispatch_body(tm, tt, cnt_ref, off_ref, base_ref, zstart_ref, loc_ref, x_ref, xs_ref, o_hbm,
                   zero_ref, srt_ref, zsem, sems):
    i = pl.program_id(0)
    last = pl.num_programs(0) - 1
    r = _sorted_rows(tt)

    def zero_copy(j):
        return pltpu.make_async_copy(zero_ref, o_hbm.at[pl.ds(pl.multiple_of(zstart_ref[j], tm), tm), :], zsem)

    @pl.when(i == 0)
    def _():
        zero_ref[...] = jnp.zeros_like(zero_ref)

        def start(j, carry):
            @pl.when(zstart_ref[j] >= 0)
            def _():
                zero_copy(j).start()
            return carry

        def wait(j, carry):
            @pl.when(zstart_ref[j] >= 0)
            def _():
                zero_copy(j).wait()
            return carry

        lax.fori_loop(0, zstart_ref.shape[0], start, 0)
        lax.fori_loop(0, zstart_ref.shape[0], wait, 0)

    loc = loc_ref[...]
    slot_i = lax.broadcasted_iota(I32, (tt, r), 1)
    hit = slot_i == loc[:, 0:1]
    for k in range(1, TOP_K):
        hit = hit | (slot_i == loc[:, k:k + 1])
    buf = i % 2
    x = jnp.where(i == last, _pad_rows(xs_ref[...], tt), x_ref[...])
    srt_ref[buf] = lax.dot_general(hit.astype(BF16), x.astype(BF16), TN_DIMS, preferred_element_type=F32)

    def copies(tile, wait):
        b = tile % 2

        def make_copy(lo, go, size):
            return pltpu.make_async_copy(srt_ref.at[b, pl.ds(lo, size), :], o_hbm.at[pl.ds(go, size), :],
                                         sems.at[b])

        _run_copies(tt, tile, cnt_ref, off_ref, base_ref, make_copy, wait)

    copies(i, False)

    @pl.when(i > 0)
    def _():
        copies(i - 1, True)

    @pl.when(i == last)
    def _():
        copies(i, True)


def _dispatch(cnt, off, base, zero_starts, loc, hn_p, hn_s, n_rows, tm, tt):
    ntp = hn_p.shape[0] // tt
    smem = pl.BlockSpec(memory_space=pltpu.SMEM)
    return pl.pallas_call(
        functools.partial(_dispatch_body, tm, tt),
        grid_spec=pltpu.PrefetchScalarGridSpec(
            num_scalar_prefetch=0, grid=(ntp + 1,),
            in_specs=[smem, smem, smem, smem,
                      pl.BlockSpec((tt, TOP_K), lambda i: (i, 0)),
                      pl.BlockSpec((tt, D_MODEL), lambda i: (jnp.minimum(i, ntp - 1), 0)),
                      pl.BlockSpec(hn_s.shape, lambda i: (0, 0))],
            out_specs=pl.BlockSpec(memory_space=pl.ANY),
            scratch_shapes=[pltpu.VMEM((tm, D_MODEL), F32), pltpu.VMEM((2, _sorted_rows(tt), D_MODEL), F32),
                            pltpu.SemaphoreType.DMA, pltpu.SemaphoreType.DMA((2,))]),
        out_shape=jax.ShapeDtypeStruct((n_rows, D_MODEL), F32),
        compiler_params=_cparams("arbitrary"),
        name="moe_dispatch",
    )(cnt, off, base, zero_starts, loc, hn_p, hn_s)


def _expert_body(be_ref, nu_ref, bv_ref, slot_ref, nxt_ref, x_ref, wgu_hbm, bgu_ref, wdn_hbm, bdn_ref,
                 y_ref, wgu_f, wdn_f, wgu_b, wdn_b, sems):
    i = pl.program_id(0)
    tm = x_ref.shape[0]
    valid = bv_ref[i]
    expert = be_ref[i]
    slot = slot_ref[i]

    def weight_copies(e, s):
        return (pltpu.make_async_copy(wgu_hbm.at[e], wgu_f.at[s], sems.at[0, s]),
                pltpu.make_async_copy(wdn_hbm.at[e], wdn_f.at[s], sems.at[1, s]))

    @pl.when(i == 0)
    def _():
        for cp in weight_copies(expert, slot):
            cp.start()

    @pl.when(jnp.logical_and(i < nu_ref[0], jnp.logical_or(i == 0, expert != be_ref[jnp.maximum(i - 1, 0)])))
    def _():
        for cp in weight_copies(expert, slot):
            cp.wait()
        wgu_b[...] = wgu_f[slot].astype(BF16)
        wdn_b[...] = wdn_f[slot].astype(BF16)

        @pl.when(nxt_ref[i] >= 0)
        def _():
            for cp in weight_copies(nxt_ref[i], 1 - slot):
                cp.start()

    def ffn(rows):
        gu = _dot(x_ref[0:rows, :].astype(BF16), wgu_b[...]) + bgu_ref[...]
        gate = jnp.minimum(gu[:, :D_FF], SWIGLU_LIMIT)
        up = jnp.clip(gu[:, D_FF:], -SWIGLU_LIMIT, SWIGLU_LIMIT)
        act = (up + 1.0) * (gate * _sigmoid(SWIGLU_ALPHA * gate))
        y_ref[0:rows, :] = _dot(act.astype(BF16), wdn_b[...]) + bdn_ref[...]
        if rows < tm:
            y_ref[rows:tm, :] = jnp.zeros((tm - rows, D_MODEL), F32)

    quarter = tm // EXPERT_ROW_SPLITS
    for q in range(1, EXPERT_ROW_SPLITS + 1):
        @pl.when(jnp.logical_and(valid > (q - 1) * quarter, valid <= q * quarter))
        def _(q=q):
            ffn(q * quarter)

    @pl.when(valid == 0)
    def _():
        y_ref[...] = jnp.zeros_like(y_ref)


def _experts(block_e, n_used, block_valid, block_slot, block_next, xs, wgu, bgu, wdn, bdn, tm):
    n_rows = xs.shape[0]
    return pl.pallas_call(
        _expert_body,
        grid_spec=pltpu.PrefetchScalarGridSpec(
            num_scalar_prefetch=5, grid=(n_rows // tm,),
            in_specs=[pl.BlockSpec((tm, D_MODEL), lambda i, be, nu, *_: (jnp.minimum(i, nu[0] - 1), 0)),
                      pl.BlockSpec(memory_space=pl.ANY),
                      pl.BlockSpec((None, 1, 2 * D_FF), lambda i, be, *_: (be[i], 0, 0)),
                      pl.BlockSpec(memory_space=pl.ANY),
                      pl.BlockSpec((None, 1, D_MODEL), lambda i, be, *_: (be[i], 0, 0))],
            out_specs=pl.BlockSpec((tm, D_MODEL), lambda i, *_: (i, 0)),
            scratch_shapes=[pltpu.VMEM((2, D_MODEL, 2 * D_FF), F32), pltpu.VMEM((2, D_FF, D_MODEL), F32),
                            pltpu.VMEM((D_MODEL, 2 * D_FF), BF16), pltpu.VMEM((D_FF, D_MODEL), BF16),
                            pltpu.SemaphoreType.DMA((2, 2))]),
        out_shape=jax.ShapeDtypeStruct((n_rows, D_MODEL), F32),
        compiler_params=_cparams("arbitrary"),
        name="moe_experts",
    )(block_e, n_used, block_valid, block_slot, block_next, xs, wgu, bgu, wdn, bdn)


def _combine_body(tt, cnt_ref, off_ref, base_ref, loc_ref, g_ref, h_ref, hs_ref, gfin_ref, ys_hbm,
                  y_ref, ysmp_ref, buf_ref, sems):
    i = pl.program_id(0)
    last = pl.num_programs(0) - 1
    r = _sorted_rows(tt)

    def copies(tile, wait):
        b = tile % 2

        def make_copy(lo, go, size):
            return pltpu.make_async_copy(ys_hbm.at[pl.ds(go, size), :], buf_ref.at[b, pl.ds(lo, size), :],
                                         sems.at[b])

        _run_copies(tt, tile, cnt_ref, off_ref, base_ref, make_copy, wait)

    @pl.when(i == 0)
    def _():
        buf_ref[...] = jnp.zeros_like(buf_ref)
        copies(0, False)

    @pl.when(i < last)
    def _():
        copies(i + 1, False)

    copies(i, True)
    loc = loc_ref[...]
    gates = g_ref[...]
    slot_i = lax.broadcasted_iota(I32, (tt, r), 1)
    gmat = jnp.where(slot_i == loc[:, 0:1], gates[:, 0:1], 0.0)
    for k in range(1, TOP_K):
        gmat = gmat + jnp.where(slot_i == loc[:, k:k + 1], gates[:, k:k + 1], 0.0)
    h = jnp.where(i == last, _pad_rows(hs_ref[...], tt), h_ref[...])
    y = _rms(h + _dot(gmat.astype(BF16), buf_ref[i % 2].astype(BF16)), gfin_ref[...])

    @pl.when(i < last)
    def _():
        y_ref[...] = y

    @pl.when(i == last)
    def _():
        ysmp_ref[...] = y[0:ysmp_ref.shape[0], :]


def _combine(cnt, off, base, loc, gates, h2_p, h2_s, norm_final, ys, tt):
    ntp = h2_p.shape[0] // tt
    tok = pl.BlockSpec((tt, D_MODEL), lambda i: (jnp.minimum(i, ntp - 1), 0))
    smp = pl.BlockSpec(h2_s.shape, lambda i: (0, 0))
    tk = pl.BlockSpec((tt, TOP_K), lambda i: (i, 0))
    smem = pl.BlockSpec(memory_space=pltpu.SMEM)
    return pl.pallas_call(
        functools.partial(_combine_body, tt),
        grid_spec=pltpu.PrefetchScalarGridSpec(
            num_scalar_prefetch=0, grid=(ntp + 1,),
            in_specs=[smem, smem, smem, tk, tk, tok, smp,
                      pl.BlockSpec((1, D_MODEL), lambda i: (0, 0)),
                      pl.BlockSpec(memory_space=pl.ANY)],
            out_specs=[tok, smp],
            scratch_shapes=[pltpu.VMEM((2, _sorted_rows(tt), D_MODEL), F32), pltpu.SemaphoreType.DMA((2,))]),
        out_shape=[jax.ShapeDtypeStruct(h2_p.shape, F32), jax.ShapeDtypeStruct(h2_s.shape, F32)],
        compiler_params=_cparams("arbitrary"),
        name="moe_combine",
    )(cnt, off, base, loc, gates, h2_p, h2_s, norm_final, ys)


def _moe_and_final_norm(hn_p, logits_p, h2_p, hn_s, logits_s, h2_s, w, tt, tm):
    nt = hn_p.shape[0] // tt + 1
    t = hn_p.shape[0] + hn_s.shape[0]
    gates, loc, cnt3, off3 = _route(logits_p, logits_s, tt)
    cnt = cnt3[:, 0, :]
    counts = jnp.sum(cnt, axis=0)
    padded = (counts + tm - 1) // tm * tm
    pad_end = jnp.cumsum(padded)
    start = pad_end - padded
    off = off3[:, 0, :]
    base = (start[None, :] + jnp.cumsum(cnt, axis=0) - cnt).astype(I32)
    n_blocks = (t * TOP_K + nt * N_EXPERTS * (SUBLANES - 1) + N_EXPERTS * (tm - 1)) // tm
    n_rows = n_blocks * tm
    block_start = jnp.arange(n_blocks, dtype=I32) * tm
    block_e = jnp.minimum(jnp.sum(block_start[:, None] >= pad_end[None, :], axis=-1), N_EXPERTS - 1).astype(I32)
    n_used = (pad_end[-1:] // tm).astype(I32)
    zero_starts = jnp.concatenate([jnp.where(padded > 0, pad_end - tm, -1),
                                   jnp.where(block_start >= pad_end[-1], block_start, -1)]).astype(I32)
    xs = _dispatch(cnt, off, base, zero_starts, loc, hn_p, hn_s, n_rows, tm, tt)
    block_valid = jnp.clip((start + counts)[block_e] - block_start, 0, tm)
    block_valid = jnp.where(block_start < pad_end[-1], block_valid, 0).astype(I32)
    present = padded > 0
    e_ids = jnp.arange(N_EXPERTS, dtype=I32)
    later = present[None, :] & (e_ids[None, :] > e_ids[:, None])
    next_e = jnp.where(jnp.any(later, axis=1), jnp.argmax(later, axis=1), -1).astype(I32)
    run_slot = ((jnp.cumsum(present.astype(I32)) - 1) % 2).astype(I32)
    ys = _experts(block_e, n_used, block_valid, run_slot[block_e], next_e[block_e], xs, w["w_gate_up"], w["b_gate_up"], w["w_down"], w["b_down"], tm)
    return _combine(cnt, off, base, loc, gates, h2_p, h2_s, w["norm_final"], ys, tt)


def kernel(x_prompt, x_sample, mem_prompt, state_ssm, state_mamba_conv, state_short_conv, cache_mem_k, cache_mem_v, norm_mix, w_in, w_mconv, b_mconv, dt_bias, a_log, d_skip, norm_ssm, w_sconv, w_out, norm_xattn, norm_mem, w_xq, w_xk, w_xv, w_xo, norm_moe, w_router, b_router, w_gate_up, b_gate_up, w_down, b_down, norm_final):
    nbp, seq, _ = x_prompt.shape
    nbs = x_sample.shape[0]
    dt_lo = SSM_INNER + SSM_CONV_DIM
    w_in0 = w_in[0]
    w_dt = w_in0[:, dt_lo:dt_lo + SSM_HEADS]
    w = {
        "norm_mix": norm_mix, "norm_ssm": norm_ssm, "norm_xattn": norm_xattn, "norm_moe": norm_moe,
        "norm_final": norm_final.reshape(1, D_MODEL),
        "w_a": w_in0[:, :dt_lo].astype(BF16),
        "w_dt": w_dt.astype(BF16), "w_dt_t": w_dt.T.astype(BF16),
        "w_b": w_in0[:, dt_lo + SSM_HEADS:].astype(BF16),
        "w_mconv": w_mconv[0], "b_mconv": b_mconv,
        "dt_bias": dt_bias, "dt_bias_t": dt_bias.reshape(SSM_HEADS, 1),
        "a_log": a_log, "a_log_t": a_log.reshape(SSM_HEADS, 1),
        "d_skip": jnp.repeat(d_skip, SSM_HEAD_DIM, axis=1),
        "w_sconv": w_sconv[0], "w_out": w_out[0].astype(BF16),
        "w_xq": w_xq[0].astype(BF16), "w_xo": w_xo[0].astype(BF16),
        "w_router": w_router[0].astype(BF16), "b_router": b_router,
        "w_gate_up": w_gate_up[0], "b_gate_up": b_gate_up[0].reshape(N_EXPERTS, 1, 2 * D_FF),
        "w_down": w_down[0], "b_down": b_down[0].reshape(N_EXPERTS, 1, D_MODEL),
    }

    k_p, v_p, kb, vb = _mem_kv(mem_prompt.reshape(nbp * N_MEM, D_MODEL), norm_mem,
                               w_xk[0].astype(BF16), w_xv[0].astype(BF16))
    h1, ssm_p, mconv_p, sconv_p = _prompt_mixer(x_prompt.reshape(nbp * seq, D_MODEL), nbp, w)
    h2, hn, logits = _prompt_attn(h1, kb, vb, nbp, w)

    xs2 = x_sample.reshape(nbs, D_MODEL)
    mstate_t = jnp.transpose(state_mamba_conv[0], (1, 0, 2))
    sstate_t = jnp.transpose(state_short_conv[0], (1, 0, 2))
    z, xs_, dtx, dec, bm, cm, yb, sga, mnew_t, snew_t = _sample_proj(xs2, mstate_t, sstate_t, w)
    ssm_s, y_s = _sample_state(dec, state_ssm[0].reshape(nbs, SSM_INNER, SSM_STATE), dtx, bm, cm)
    h1s, q_s = _sample_fin1(xs2, y_s, xs_, z, yb, sga, w)
    o_s = _sample_attn(q_s.reshape(nbs, 1, D_MODEL),
                       cache_mem_k[0], cache_mem_v[0])
    h2s, hns, logits_s = _sample_fin2(h1s, o_s.reshape(nbs, D_MODEL), w)
    y_prompt, y_sample = _moe_and_final_norm(hn, logits, h2, hns, logits_s, h2s, w, MIX_TILE, MOE_ROW_TILE)

    return (y_prompt.reshape(nbp, seq, D_MODEL),
            y_sample.reshape(nbs, 1, D_MODEL),
            ssm_p.reshape(1, nbp, SSM_HEADS, SSM_HEAD_DIM, SSM_STATE),
            mconv_p[None], sconv_p[None],
            k_p.reshape(1, nbp, N_MEM, XA_HEADS, XA_HEAD_DIM),
            v_p.reshape(1, nbp, N_MEM, XA_HEADS, XA_HEAD_DIM),
            ssm_s.reshape(1, nbs, SSM_HEADS, SSM_HEAD_DIM, SSM_STATE),
            jnp.transpose(mnew_t, (1, 0, 2))[None],
            jnp.transpose(snew_t, (1, 0, 2))[None])
```

```python
import functools

import jax
import jax.numpy as jnp
from jax import lax
from jax.experimental import pallas as pl
from jax.experimental.pallas import tpu as pltpu

F32 = jnp.float32
BF16 = jnp.bfloat16
I32 = jnp.int32

D_MODEL = 1024
N_MEM = 256
SSM_HEADS = 16
SSM_HEAD_DIM = 64
SSM_INNER = SSM_HEADS * SSM_HEAD_DIM
SSM_STATE = 128
SSM_GROUPS = 4
HEADS_PER_GROUP = SSM_HEADS // SSM_GROUPS
GROUP_WIDTH = SSM_INNER // SSM_GROUPS
SSM_CONV = 4
SSM_CONV_DIM = SSM_INNER + 2 * SSM_GROUPS * SSM_STATE
SC_CONV = 3
XA_HEADS = 4
XA_HEAD_DIM = D_MODEL // XA_HEADS
N_EXPERTS = 32
TOP_K = 4
D_FF = D_MODEL
SWIGLU_LIMIT = 7.0
SWIGLU_ALPHA = 1.702
EPS = 1e-6

LANES = 128
SUBLANES = 8
VMEM_LIMIT = 56 * 1024 * 1024

MIX_TILE = 256
MOE_ROW_TILE = 512
EXPERT_ROW_SPLITS = 4
STATE_BB = 8
ATTN_BB = 4

NT_DIMS = (((1,), (1,)), ((), ()))
TN_DIMS = (((0,), (0,)), ((), ()))


def _cparams(*sem):
    return pltpu.CompilerParams(dimension_semantics=sem, vmem_limit_bytes=VMEM_LIMIT)


def _const_spec(shape):
    nd = len(shape)
    return pl.BlockSpec(shape, lambda *_: (0,) * nd, pipeline_mode=pl.Buffered(1))


def _sigmoid(x):
    return 1.0 / (1.0 + jnp.exp(-x))


def _silu(x):
    return x * _sigmoid(x)


def _softplus(x):
    return jnp.maximum(x, 0.0) + jnp.log(1.0 + jnp.exp(-jnp.abs(x)))


def _rms(x, g):
    ms = jnp.mean(x * x, axis=-1, keepdims=True)
    return x * lax.rsqrt(ms + EPS) * g


def _dot(a, b):
    return jnp.dot(a, b, preferred_element_type=F32)


def _dot_nt(a, b):
    return lax.dot_general(a, b, NT_DIMS, preferred_element_type=F32)


def _expand_heads(v):
    rows = v.shape[0]
    lane = lax.broadcasted_iota(I32, (rows, LANES), 1)
    pieces = []
    for j in range(SSM_HEADS // 2):
        a = jnp.broadcast_to(v[:, 2 * j:2 * j + 1], (rows, LANES))
        b = jnp.broadcast_to(v[:, 2 * j + 1:2 * j + 2], (rows, LANES))
        pieces.append(jnp.where(lane < SSM_HEAD_DIM, a, b))
    return jnp.concatenate(pieces, axis=1)


def _cumsum(x, axis):
    idx = lax.broadcasted_iota(I32, x.shape, axis)
    shift = 1
    while shift < x.shape[axis]:
        x = x + jnp.where(idx >= shift, pltpu.roll(x, shift, axis), 0.0)
        shift *= 2
    return x


def _pad_rows(x, rows):
    return jnp.concatenate([x, jnp.zeros((rows - x.shape[0], x.shape[1]), x.dtype)], axis=0)


def _group_rmsnorm(u, g):
    outs = []
    for k in range(SSM_GROUPS):
        ug = u[:, k * GROUP_WIDTH:(k + 1) * GROUP_WIDTH]
        ms = jnp.mean(ug * ug, axis=-1, keepdims=True)
        outs.append(ug * lax.rsqrt(ms + EPS))
    return jnp.concatenate(outs, axis=1) * g


def _memkv_body(mem_ref, g_ref, wk_ref, wv_ref, k_ref, v_ref, kb_ref, vb_ref):
    mn = _rms(mem_ref[...], g_ref[...]).astype(BF16)
    k = _dot(mn, wk_ref[...])
    v = _dot(mn, wv_ref[...])
    k_ref[...] = k
    v_ref[...] = v
    kb_ref[...] = k.astype(BF16)
    vb_ref[...] = v.astype(BF16)


def _mem_kv(mem2d, norm_mem, wk, wv):
    rows = mem2d.shape[0]
    nb = rows // N_MEM
    blk = pl.BlockSpec((N_MEM, D_MODEL), lambda b: (b, 0))
    return pl.pallas_call(
        _memkv_body,
        grid=(nb,),
        in_specs=[blk, _const_spec((1, D_MODEL)), _const_spec((D_MODEL, D_MODEL)),
                  _const_spec((D_MODEL, D_MODEL))],
        out_specs=[blk, blk, blk, blk],
        out_shape=[jax.ShapeDtypeStruct((rows, D_MODEL), F32)] * 2
        + [jax.ShapeDtypeStruct((rows, D_MODEL), BF16)] * 2,
        compiler_params=_cparams("arbitrary"),
        name="mem_kv",
    )(mem2d, norm_mem, wk, wv)


def _mix_body(x_ref, gmix_ref, wa_ref, wdtc_ref, wdtr_ref, wb_ref, wmc_ref, bmc_ref,
              dtb_ref, dtbt_ref, alog_ref, alogt_ref, dskip_ref, gssm_ref, wsc_ref, wout_ref,
              h_ref, ssm_ref, mbuf_ref, sbuf_ref,
              st_ref, cbuf_ref, scbuf_ref):
    tq = MIX_TILE
    c = pl.program_id(1)

    @pl.when(c == 0)
    def _():
        st_ref[...] = jnp.zeros_like(st_ref)
        cbuf_ref[0:SUBLANES, :] = jnp.zeros((SUBLANES, SSM_CONV_DIM), F32)
        scbuf_ref[0:SUBLANES, :] = jnp.zeros((SUBLANES, D_MODEL), F32)

    x = x_ref[...]
    xn = _rms(x, gmix_ref[...]).astype(BF16)

    u = _dot(xn, wa_ref[:, SSM_INNER:])
    cbuf_ref[SUBLANES:SUBLANES + tq, :] = u
    wm = wmc_ref[...]
    conv = u * wm[SSM_CONV - 1:SSM_CONV, :] + bmc_ref[...]
    for k in range(SSM_CONV - 1):
        off = SUBLANES - (SSM_CONV - 1) + k
        conv = conv + cbuf_ref[off:off + tq, :] * wm[k:k + 1, :]
    tail = cbuf_ref[tq + SUBLANES - (SSM_CONV - 1):tq + SUBLANES, :]
    mbuf_ref[...] = tail
    cbuf_ref[SUBLANES - (SSM_CONV - 1):SUBLANES, :] = tail
    xbc = _silu(conv)
    xs = xbc[:, :SSM_INNER]
    bm = xbc[:, SSM_INNER:SSM_INNER + SSM_GROUPS * SSM_STATE]
    cm = xbc[:, SSM_INNER + SSM_GROUPS * SSM_STATE:]

    dt = _softplus(_dot(xn, wdtc_ref[...]) + dtb_ref[...])
    dtt = _softplus(_dot_nt(wdtr_ref[...], xn) + dtbt_ref[...])
    a_row = -jnp.exp(alog_ref[...])
    a_col = -jnp.exp(alogt_ref[...])
    row_i = lax.broadcasted_iota(I32, (tq, tq), 0)
    col_i = lax.broadcasted_iota(I32, (tq, tq), 1)
    causal = row_i >= col_i
    a_cum = _cumsum(dt * a_row, 0)
    a_cumt = _cumsum(dtt * a_col, 1)
    a_last = a_cum[tq - 1:tq, :]

    xdt = xs * _expand_heads(dt)
    in_decay = _expand_heads(jnp.exp(a_cum))
    to_end = _expand_heads(jnp.exp(a_last - a_cum))
    chunk_decay = _expand_heads(jnp.exp(a_last))
    xdt_b = xdt.astype(BF16)
    xend_b = (xdt * to_end).astype(BF16)
    lane = lax.broadcasted_iota(I32, (tq, LANES), 1)

    def proj_b(k):
        return _dot(xn, wb_ref[:, k * D_MODEL:(k + 1) * D_MODEL])

    pb = []
    y_groups = []
    for g in range(SSM_GROUPS):
        pb.append(proj_b(g))
        if g == 0:
            z = _dot(xn, wa_ref[:, :SSM_INNER])
        if g == 2:
            g_b = proj_b(SSM_GROUPS)
        cg = cm[:, g * SSM_STATE:(g + 1) * SSM_STATE].astype(BF16)
        bg_f = bm[:, g * SSM_STATE:(g + 1) * SSM_STATE]
        bg = bg_f.astype(BF16)
        scores = _dot_nt(cg, bg)
        gs = slice(g * GROUP_WIDTH, (g + 1) * GROUP_WIDTH)
        st_g = st_ref[:, gs]
        y_off = _dot(cg, st_g.astype(BF16)) * in_decay[:, gs]
        pair_out = []
        for pr in range(HEADS_PER_GROUP // 2):
            h0 = g * HEADS_PER_GROUP + 2 * pr
            xp = xdt_b[:, h0 * SSM_HEAD_DIM:(h0 + 2) * SSM_HEAD_DIM]
            ys = []
            for h in (h0, h0 + 1):
                seg = a_cum[:, h:h + 1] - a_cumt[h:h + 1, :]
                decay = jnp.where(causal, jnp.exp(jnp.minimum(seg, 0.0)), 0.0)
                ys.append(_dot((scores * decay).astype(BF16), xp))
            pair_out.append(jnp.where(lane < SSM_HEAD_DIM, ys[0], ys[1]))
        y_groups.append(jnp.concatenate(pair_out, axis=1) + y_off)
        st_ref[:, gs] = st_g * chunk_decay[:, gs] + _dot(bg_f.T.astype(BF16), xend_b[:, gs])
    y = jnp.concatenate(y_groups, axis=1) + dskip_ref[...] * xs
    y_a = _group_rmsnorm(y * _silu(z), gssm_ref[...])

    sc_b, sc_c, sc_v, g_a = pb
    cv = sc_c * sc_v
    scbuf_ref[SUBLANES:SUBLANES + tq, :] = cv
    ws = wsc_ref[...]
    uc = cv * ws[SC_CONV - 1:SC_CONV, :]
    for k in range(SC_CONV - 1):
        off = SUBLANES - (SC_CONV - 1) + k
        uc = uc + scbuf_ref[off:off + tq, :] * ws[k:k + 1, :]
    stail = scbuf_ref[tq + SUBLANES - (SC_CONV - 1):tq + SUBLANES, :]
    sbuf_ref[...] = stail
    scbuf_ref[SUBLANES - (SC_CONV - 1):SUBLANES, :] = stail
    merged = _sigmoid(g_a) * y_a + _sigmoid(g_b) * (sc_b * uc)
    h_ref[...] = x + _dot(merged.astype(BF16), wout_ref[...])

    @pl.when(c == pl.num_programs(1) - 1)
    def _():
        ssm_ref[...] = st_ref[...].T


def _prompt_mixer(x2d, nb, w):
    t = x2d.shape[0]
    nc = t // nb // MIX_TILE
    tok = pl.BlockSpec((MIX_TILE, D_MODEL), lambda b, c: (b * nc + c, 0))
    return pl.pallas_call(
        _mix_body,
        grid=(nb, nc),
        in_specs=[tok, _const_spec((1, D_MODEL)),
                  _const_spec((D_MODEL, SSM_INNER + SSM_CONV_DIM)),
                  _const_spec((D_MODEL, SSM_HEADS)), _const_spec((SSM_HEADS, D_MODEL)),
                  _const_spec((D_MODEL, 5 * D_MODEL)),
                  _const_spec((SSM_CONV, SSM_CONV_DIM)), _const_spec((1, SSM_CONV_DIM)),
                  _const_spec((1, SSM_HEADS)), _const_spec((SSM_HEADS, 1)),
                  _const_spec((1, SSM_HEADS)), _const_spec((SSM_HEADS, 1)),
                  _const_spec((1, SSM_INNER)), _const_spec((1, SSM_INNER)),
                  _const_spec((SC_CONV, D_MODEL)), _const_spec((D_MODEL, D_MODEL))],
        out_specs=[tok,
                   pl.BlockSpec((None, SSM_INNER, SSM_STATE), lambda b, c: (b, 0, 0)),
                   pl.BlockSpec((None, SSM_CONV - 1, SSM_CONV_DIM), lambda b, c: (b, 0, 0)),
                   pl.BlockSpec((None, SC_CONV - 1, D_MODEL), lambda b, c: (b, 0, 0))],
        out_shape=[jax.ShapeDtypeStruct((t, D_MODEL), F32),
                   jax.ShapeDtypeStruct((nb, SSM_INNER, SSM_STATE), F32),
                   jax.ShapeDtypeStruct((nb, SSM_CONV - 1, SSM_CONV_DIM), F32),
                   jax.ShapeDtypeStruct((nb, SC_CONV - 1, D_MODEL), F32)],
        scratch_shapes=[pltpu.VMEM((SSM_STATE, SSM_INNER), F32),
                        pltpu.VMEM((MIX_TILE + SUBLANES, SSM_CONV_DIM), F32),
                        pltpu.VMEM((MIX_TILE + SUBLANES, D_MODEL), F32)],
        compiler_params=_cparams("arbitrary", "arbitrary"),
        name="prompt_mixer",
    )(x2d, w["norm_mix"], w["w_a"], w["w_dt"], w["w_dt_t"], w["w_b"], w["w_mconv"], w["b_mconv"],
      w["dt_bias"], w["dt_bias_t"], w["a_log"], w["a_log_t"], w["d_skip"], w["norm_ssm"],
      w["w_sconv"], w["w_out"])


def _router_tail(h2, gmoe_ref, wr_ref, br_ref, h2_ref, hn_ref, lg_ref):
    h2_ref[...] = h2
    hn = _rms(h2, gmoe_ref[...])
    hn_ref[...] = hn
    lg_ref[...] = _dot(hn.astype(BF16), wr_ref[...]) + br_ref[...]


def _attn_body(h_ref, gx_ref, wq_ref, k_ref, v_ref, wo_ref, gmoe_ref, wr_ref, br_ref,
               h2_ref, hn_ref, lg_ref):
    h = h_ref[...]
    hn = _rms(h, gx_ref[...]).astype(BF16)
    q = _dot(hn, wq_ref[...]).astype(BF16)
    outs = []
    for hd in range(XA_HEADS):
        sl = slice(hd * XA_HEAD_DIM, (hd + 1) * XA_HEAD_DIM)
        s = _dot_nt(q[:, sl], k_ref[:, sl]) * (XA_HEAD_DIM ** -0.5)
        e = jnp.exp(s - jnp.max(s, axis=-1, keepdims=True))
        p = e / jnp.sum(e, axis=-1, keepdims=True)
        outs.append(_dot(p.astype(BF16), v_ref[:, sl]))
    o = jnp.concatenate(outs, axis=1).astype(BF16)
    h2 = h + _dot(o, wo_ref[...])
    _router_tail(h2, gmoe_ref, wr_ref, br_ref, h2_ref, hn_ref, lg_ref)


def _prompt_attn(h1, kb, vb, nb, w):
    t = h1.shape[0]
    nc = t // nb // MIX_TILE
    tok = pl.BlockSpec((MIX_TILE, D_MODEL), lambda b, c: (b * nc + c, 0))
    kv = pl.BlockSpec((N_MEM, D_MODEL), lambda b, c: (b, 0))
    return pl.pallas_call(
        _attn_body,
        grid=(nb, nc),
        in_specs=[tok, _const_spec((1, D_MODEL)), _const_spec((D_MODEL, D_MODEL)), kv, kv,
                  _const_spec((D_MODEL, D_MODEL)), _const_spec((1, D_MODEL)),
                  _const_spec((D_MODEL, N_EXPERTS)), _const_spec((1, N_EXPERTS))],
        out_specs=[tok, tok, pl.BlockSpec((MIX_TILE, N_EXPERTS), lambda b, c: (b * nc + c, 0))],
        out_shape=[jax.ShapeDtypeStruct((t, D_MODEL), F32),
                   jax.ShapeDtypeStruct((t, D_MODEL), F32),
                   jax.ShapeDtypeStruct((t, N_EXPERTS), F32)],
        compiler_params=_cparams("arbitrary", "arbitrary"),
        name="prompt_attn",
    )(h1, w["norm_xattn"], w["w_xq"], kb, vb, w["w_xo"], w["norm_moe"], w["w_router"], w["b_router"])


def _sproj_body(x_ref, gmix_ref, wa_ref, wdtc_ref, wb_ref, wmc_ref, bmc_ref, dtb_ref, alog_ref,
                wsc_ref, mst_ref, sst_ref,
                z_ref, xs_ref, dtx_ref, dec_ref, bm_ref, cm_ref, yb_ref, sga_ref, mnew_ref, snew_ref):
    x = x_ref[...]
    xn = _rms(x, gmix_ref[...]).astype(BF16)
    pa = _dot(xn, wa_ref[...])
    z_ref[...] = pa[:, :SSM_INNER]
    u = pa[:, SSM_INNER:]
    wm = wmc_ref[...]
    conv = u * wm[SSM_CONV - 1:SSM_CONV, :] + bmc_ref[...]
    for k in range(SSM_CONV - 1):
        conv = conv + mst_ref[k] * wm[k:k + 1, :]
    for k in range(SSM_CONV - 2):
        mnew_ref[k] = mst_ref[k + 1]
    mnew_ref[SSM_CONV - 2] = u
    xbc = _silu(conv)
    xs = xbc[:, :SSM_INNER]
    xs_ref[...] = xs
    bm_ref[...] = xbc[:, SSM_INNER:SSM_INNER + SSM_GROUPS * SSM_STATE]
    cm_ref[...] = xbc[:, SSM_INNER + SSM_GROUPS * SSM_STATE:]
    dt = _softplus(_dot(xn, wdtc_ref[...]) + dtb_ref[...])
    dec_ref[...] = jnp.exp(dt * (-jnp.exp(alog_ref[...])))
    dtx_ref[...] = xs * _expand_heads(dt)
    pb = _dot(xn, wb_ref[...])
    cv = pb[:, D_MODEL:2 * D_MODEL] * pb[:, 2 * D_MODEL:3 * D_MODEL]
    ws = wsc_ref[...]
    uc = cv * ws[SC_CONV - 1:SC_CONV, :]
    for k in range(SC_CONV - 1):
        uc = uc + sst_ref[k] * ws[k:k + 1, :]
    for k in range(SC_CONV - 2):
        snew_ref[k] = sst_ref[k + 1]
    snew_ref[SC_CONV - 2] = cv
    yb_ref[...] = _sigmoid(pb[:, 4 * D_MODEL:5 * D_MODEL]) * (pb[:, 0:D_MODEL] * uc)
    sga_ref[...] = _sigmoid(pb[:, 3 * D_MODEL:4 * D_MODEL])


def _sample_proj(x, mstate_t, sstate_t, w):
    nb = x.shape[0]
    f = lambda *s: jax.ShapeDtypeStruct(s, F32)
    return pl.pallas_call(
        _sproj_body,
        out_shape=[f(nb, SSM_INNER), f(nb, SSM_INNER), f(nb, SSM_INNER), f(nb, SSM_HEADS),
                   f(nb, SSM_GROUPS * SSM_STATE), f(nb, SSM_GROUPS * SSM_STATE),
                   f(nb, D_MODEL), f(nb, D_MODEL),
                   f(SSM_CONV - 1, nb, SSM_CONV_DIM), f(SC_CONV - 1, nb, D_MODEL)],
        compiler_params=pltpu.CompilerParams(vmem_limit_bytes=VMEM_LIMIT),
        name="sample_proj",
    )(x, w["norm_mix"], w["w_a"], w["w_dt"], w["w_b"], w["w_mconv"], w["b_mconv"], w["dt_bias"],
      w["a_log"], w["w_sconv"], mstate_t, sstate_t)


def _sstate_body(dec_ref, s_ref, dtx_ref, bm_ref, cm_ref, snew_ref, y_ref):
    i = pl.program_id(0)
    rows_per_blk = LANES
    for j in range(STATE_BB):
        b = i * STATE_BB + j
        dtx_row = dtx_ref[j:j + 1, :]
        y_parts = []
        for g in range(SSM_GROUPS):
            b_row = bm_ref[j:j + 1, g * SSM_STATE:(g + 1) * SSM_STATE]
            c_row = cm_ref[j:j + 1, g * SSM_STATE:(g + 1) * SSM_STATE].astype(BF16)
            new_blocks = []
            for q in range(GROUP_WIDTH // rows_per_blk):
                r0 = g * GROUP_WIDTH + q * rows_per_blk
                dcol = jnp.broadcast_to(dtx_row[:, r0:r0 + rows_per_blk], (rows_per_blk, LANES)).T
                sub = []
                for hh in range(rows_per_blk // SSM_HEAD_DIM):
                    h = r0 // SSM_HEAD_DIM + hh
                    lo = hh * SSM_HEAD_DIM
                    s_old = s_ref[j, r0 + lo:r0 + lo + SSM_HEAD_DIM, :]
                    sub.append(s_old * dec_ref[b, h] + dcol[lo:lo + SSM_HEAD_DIM, :] * b_row)
                blk = jnp.concatenate(sub, axis=0)
                snew_ref[j, r0:r0 + rows_per_blk, :] = blk
                new_blocks.append(blk.astype(BF16))
            s_g = jnp.concatenate(new_blocks, axis=0)
            y_parts.append(_dot_nt(c_row, s_g))
        y_ref[j:j + 1, :] = jnp.concatenate(y_parts, axis=1)


def _sample_state(dec, state, dtx, bm, cm):
    nb = state.shape[0]
    row = lambda wdt: pl.BlockSpec((STATE_BB, wdt), lambda i, dec: (i, 0))
    st = pl.BlockSpec((STATE_BB, SSM_INNER, SSM_STATE), lambda i, dec: (i, 0, 0))
    return pl.pallas_call(
        _sstate_body,
        grid_spec=pltpu.PrefetchScalarGridSpec(
            num_scalar_prefetch=1, grid=(nb // STATE_BB,),
            in_specs=[st, row(SSM_INNER), row(SSM_GROUPS * SSM_STATE), row(SSM_GROUPS * SSM_STATE)],
            out_specs=[st, row(SSM_INNER)]),
        out_shape=[jax.ShapeDtypeStruct(state.shape, F32), jax.ShapeDtypeStruct((nb, SSM_INNER), F32)],
        compiler_params=_cparams("arbitrary"),
        name="sample_state",
    )(dec, state, dtx, bm, cm)


def _sfin1_body(x_ref, y_ref, xs_ref, z_ref, yb_ref, sga_ref, dskip_ref, gssm_ref, wout_ref,
                gx_ref, wq_ref, h_ref, q_ref):
    y = y_ref[...] + dskip_ref[...] * xs_ref[...]
    y_a = _group_rmsnorm(y * _silu(z_ref[...]), gssm_ref[...])
    merged = sga_ref[...] * y_a + yb_ref[...]
    h = x_ref[...] + _dot(merged.astype(BF16), wout_ref[...])
    h_ref[...] = h
    q_ref[...] = _dot(_rms(h, gx_ref[...]).astype(BF16), wq_ref[...])


def _sample_fin1(x, y, xs, z, yb, sga, w):
    nb = x.shape[0]
    return pl.pallas_call(
        _sfin1_body,
        out_shape=[jax.ShapeDtypeStruct((nb, D_MODEL), F32)] * 2,
        compiler_params=pltpu.CompilerParams(vmem_limit_bytes=VMEM_LIMIT),
        name="sample_fin1",
    )(x, y, xs, z, yb, sga, w["d_skip"], w["norm_ssm"], w["w_out"], w["norm_xattn"], w["w_xq"])


def _sattn_body(q_ref, k_ref, v_ref, o_ref):
    for j in range(ATTN_BB):
        q_row = q_ref[j]
        q4 = jnp.concatenate([q_row[:, h * XA_HEAD_DIM:(h + 1) * XA_HEAD_DIM]
                              for h in range(XA_HEADS)], axis=0)
        s = jnp.sum(k_ref[j] * q4[None], axis=-1, keepdims=True) * (XA_HEAD_DIM ** -0.5)
        e = jnp.exp(s - jnp.max(s, axis=0, keepdims=True))
        p = e / jnp.sum(e, axis=0, keepdims=True)
        o4 = jnp.sum(p * v_ref[j], axis=0)
        o_ref[j] = jnp.concatenate([o4[h:h + 1, :] for h in range(XA_HEADS)], axis=1)


def _sample_attn(q3, k3, v3):
    nb = q3.shape[0]
    qs = pl.BlockSpec((ATTN_BB, 1, D_MODEL), lambda i: (i, 0, 0))
    kv = pl.BlockSpec((ATTN_BB, N_MEM, XA_HEADS, XA_HEAD_DIM), lambda i: (i, 0, 0, 0))
    return pl.pallas_call(
        _sattn_body,
        grid=(nb // ATTN_BB,),
        in_specs=[qs, kv, kv],
        out_specs=qs,
        out_shape=jax.ShapeDtypeStruct((nb, 1, D_MODEL), F32),
        compiler_params=_cparams("arbitrary"),
        name="sample_attn",
    )(q3, k3, v3)


def _sfin2_body(h_ref, o_ref, wo_ref, gmoe_ref, wr_ref, br_ref, h2_ref, hn_ref, lg_ref):
    h2 = h_ref[...] + _dot(o_ref[...].astype(BF16), wo_ref[...])
    _router_tail(h2, gmoe_ref, wr_ref, br_ref, h2_ref, hn_ref, lg_ref)


def _sample_fin2(h1, o, w):
    nb = h1.shape[0]
    return pl.pallas_call(
        _sfin2_body,
        out_shape=[jax.ShapeDtypeStruct((nb, D_MODEL), F32)] * 2
        + [jax.ShapeDtypeStruct((nb, N_EXPERTS), F32)],
        compiler_params=pltpu.CompilerParams(vmem_limit_bytes=VMEM_LIMIT),
        name="sample_fin2",
    )(h1, o, w["w_xo"], w["norm_moe"], w["w_router"], w["b_router"])


def _route_body(lgp_ref, lgs_ref, g_ref, loc_ref, cnt_ref, off_ref):
    tt = lgp_ref.shape[0]
    is_sample = pl.program_id(0) == pl.num_programs(0) - 1
    row = lax.broadcasted_iota(I32, (tt, 1), 0)
    valid = jnp.logical_or(jnp.logical_not(is_sample), row < lgs_ref.shape[0])
    work = jnp.where(is_sample, _pad_rows(lgs_ref[...], tt), lgp_ref[...])
    lane = lax.broadcasted_iota(I32, (tt, N_EXPERTS), 1).astype(F32)
    vals, hots = [], []
    for _ in range(TOP_K):
        m = jnp.max(work, axis=-1, keepdims=True)
        idx = jnp.min(jnp.where(work == m, lane, float(N_EXPERTS)), axis=-1, keepdims=True)
        hot = (lane == idx) & valid
        vals.append(m)
        hots.append(hot)
        work = jnp.where(hot, -jnp.inf, work)
    exps = [jnp.exp(v - vals[0]) for v in vals]
    tot = exps[0]
    for e in exps[1:]:
        tot = tot + e
    assigned = hots[0]
    for hot in hots[1:]:
        assigned = assigned | hot
    a = assigned.astype(BF16)
    r_i = lax.broadcasted_iota(I32, (tt, tt), 0)
    c_i = lax.broadcasted_iota(I32, (tt, tt), 1)
    rank = _dot((r_i > c_i).astype(BF16), a)
    e_r = lax.broadcasted_iota(I32, (N_EXPERTS, N_EXPERTS), 0)
    e_c = lax.broadcasted_iota(I32, (N_EXPERTS, N_EXPERTS), 1)
    cnt = jnp.sum(a.astype(F32), axis=0, keepdims=True)
    cnt = jnp.floor((cnt + (SUBLANES - 1)) * (1.0 / SUBLANES)) * SUBLANES
    cnt_rows = jnp.broadcast_to(cnt, (SUBLANES, N_EXPERTS)).astype(BF16)
    off = _dot(cnt_rows, (e_r < e_c).astype(BF16))[0:1, :]
    slot = rank + off
    k_lane = lax.broadcasted_iota(I32, (tt, TOP_K), 1)
    g_out = jnp.zeros((tt, TOP_K), F32)
    l_out = jnp.zeros((tt, TOP_K), F32)
    for k in range(TOP_K):
        lk = jnp.sum(jnp.where(hots[k], slot, 0.0), axis=-1, keepdims=True)
        g_out = jnp.where(k_lane == k, exps[k] / tot, g_out)
        l_out = jnp.where(k_lane == k, lk, l_out)
    g_ref[...] = jnp.where(valid, g_out, 0.0)
    loc_ref[...] = jnp.where(valid, l_out, -1.0).astype(I32)
    cnt_ref[...] = cnt.astype(I32)
    off_ref[...] = off.astype(I32)


def _route(logits_p, logits_s, tt):
    ntp = logits_p.shape[0] // tt
    nt = ntp + 1
    t = nt * tt
    tk = pl.BlockSpec((tt, TOP_K), lambda i: (i, 0))
    per_tile = pl.BlockSpec((None, 1, N_EXPERTS), lambda i: (i, 0, 0))
    return pl.pallas_call(
        _route_body,
        grid=(nt,),
        in_specs=[pl.BlockSpec((tt, N_EXPERTS), lambda i: (jnp.minimum(i, ntp - 1), 0)),
                  pl.BlockSpec(logits_s.shape, lambda i: (0, 0))],
        out_specs=[tk, tk, per_tile, per_tile],
        out_shape=[jax.ShapeDtypeStruct((t, TOP_K), F32), jax.ShapeDtypeStruct((t, TOP_K), I32),
                   jax.ShapeDtypeStruct((nt, 1, N_EXPERTS), I32), jax.ShapeDtypeStruct((nt, 1, N_EXPERTS), I32)],
        compiler_params=_cparams("arbitrary"),
        name="moe_route",
    )(logits_p, logits_s)


def _sorted_rows(tt):
    return tt * TOP_K + N_EXPERTS * SUBLANES


def _run_copies(tt, tile, cnt_ref, off_ref, base_ref, make_copy, wait):
    if wait:
        total = off_ref[tile, N_EXPERTS - 1] + cnt_ref[tile, N_EXPERTS - 1]

        @pl.when(total > 0)
        def _():
            make_copy(0, 0, pl.multiple_of(total, SUBLANES)).wait()
        return

    def per_expert(e, carry):
        n = cnt_ref[tile, e]

        @pl.when(n > 0)
        def _():
            make_copy(pl.multiple_of(off_ref[tile, e], SUBLANES),
                      pl.multiple_of(base_ref[tile, e], SUBLANES), pl.multiple_of(n, SUBLANES)).start()
        return carry

    lax.fori_loop(0, N_EXPERTS, per_expert, 0)


def _dispatch_body(tm, tt, cnt_ref, off_ref, base_ref, zstart_ref, loc_ref, x_ref, xs_ref, o_hbm,
                   zero_ref, srt_ref, zsem, sems):
    i = pl.program_id(0)
    last = pl.num_programs(0) - 1
    r = _sorted_rows(tt)

    def zero_copy(j):
        return pltpu.make_async_copy(zero_ref, o_hbm.at[pl.ds(pl.multiple_of(zstart_ref[j], tm), tm), :], zsem)

    @pl.when(i == 0)
    def _():
        zero_ref[...] = jnp.zeros_like(zero_ref)

        def start(j, carry):
            @pl.when(zstart_ref[j] >= 0)
            def _():
                zero_copy(j).start()
            return carry

        def wait(j, carry):
            @pl.when(zstart_ref[j] >= 0)
            def _():
                zero_copy(j).wait()
            return carry

        lax.fori_loop(0, zstart_ref.shape[0], start, 0)
        lax.fori_loop(0, zstart_ref.shape[0], wait, 0)

    loc = loc_ref[...]
    slot_i = lax.broadcasted_iota(I32, (tt, r), 1)
    hit = slot_i == loc[:, 0:1]
    for k in range(1, TOP_K):
        hit = hit | (slot_i == loc[:, k:k + 1])
    buf = i % 2
    x = jnp.where(i == last, _pad_rows(xs_ref[...], tt), x_ref[...])
    srt_ref[buf] = lax.dot_general(hit.astype(BF16), x.astype(BF16), TN_DIMS, preferred_element_type=F32)

    def copies(tile, wait):
        b = tile % 2

        def make_copy(lo, go, size):
            return pltpu.make_async_copy(srt_ref.at[b, pl.ds(lo, size), :], o_hbm.at[pl.ds(go, size), :],
                                         sems.at[b])

        _run_copies(tt, tile, cnt_ref, off_ref, base_ref, make_copy, wait)

    copies(i, False)

    @pl.when(i > 0)
    def _():
        copies(i - 1, True)

    @pl.when(i == last)
    def _():
        copies(i, True)


def _dispatch(cnt, off, base, zero_starts, loc, hn_p, hn_s, n_rows, tm, tt):
    ntp = hn_p.shape[0] // tt
    smem = pl.BlockSpec(memory_space=pltpu.SMEM)
    return pl.pallas_call(
        functools.partial(_dispatch_body, tm, tt),
        grid_spec=pltpu.PrefetchScalarGridSpec(
            num_scalar_prefetch=0, grid=(ntp + 1,),
            in_specs=[smem, smem, smem, smem,
                      pl.BlockSpec((tt, TOP_K), lambda i: (i, 0)),
                      pl.BlockSpec((tt, D_MODEL), lambda i: (jnp.minimum(i, ntp - 1), 0)),
                      pl.BlockSpec(hn_s.shape, lambda i: (0, 0))],
            out_specs=pl.BlockSpec(memory_space=pl.ANY),
            scratch_shapes=[pltpu.VMEM((tm, D_MODEL), F32), pltpu.VMEM((2, _sorted_rows(tt), D_MODEL), F32),
                            pltpu.SemaphoreType.DMA, pltpu.SemaphoreType.DMA((2,))]),
        out_shape=jax.ShapeDtypeStruct((n_rows, D_MODEL), F32),
        compiler_params=_cparams("arbitrary"),
        name="moe_dispatch",
    )(cnt, off, base, zero_starts, loc, hn_p, hn_s)


def _expert_body(be_ref, nu_ref, bv_ref, slot_ref, nxt_ref, x_ref, wgu_hbm, bgu_ref, wdn_hbm, bdn_ref,
                 y_ref, wgu_f, wdn_f, wgu_b, wdn_b, sems):
    i = pl.program_id(0)
    tm = x_ref.shape[0]
    valid = bv_ref[i]
    expert = be_ref[i]
    slot = slot_ref[i]

    def weight_copies(e, s):
        return (pltpu.make_async_copy(wgu_hbm.at[e], wgu_f.at[s], sems.at[0, s]),
                pltpu.make_async_copy(wdn_hbm.at[e], wdn_f.at[s], sems.at[1, s]))

    @pl.when(i == 0)
    def _():
        for cp in weight_copies(expert, slot):
            cp.start()

    @pl.when(jnp.logical_and(i < nu_ref[0], jnp.logical_or(i == 0, expert != be_ref[jnp.maximum(i - 1, 0)])))
    def _():
        for cp in weight_copies(expert, slot):
            cp.wait()
        wgu_b[...] = wgu_f[slot].astype(BF16)
        wdn_b[...] = wdn_f[slot].astype(BF16)

        @pl.when(nxt_ref[i] >= 0)
        def _():
            for cp in weight_copies(nxt_ref[i], 1 - slot):
                cp.start()

    def ffn(rows):
        gu = _dot(x_ref[0:rows, :].astype(BF16), wgu_b[...]) + bgu_ref[...]
        gate = jnp.minimum(gu[:, :D_FF], SWIGLU_LIMIT)
        up = jnp.clip(gu[:, D_FF:], -SWIGLU_LIMIT, SWIGLU_LIMIT)
        act = (up + 1.0) * (gate * _sigmoid(SWIGLU_ALPHA * gate))
        y_ref[0:rows, :] = _dot(act.astype(BF16), wdn_b[...]) + bdn_ref[...]
        if rows < tm:
            y_ref[rows:tm, :] = jnp.zeros((tm - rows, D_MODEL), F32)

    quarter = tm // EXPERT_ROW_SPLITS
    for q in range(1, EXPERT_ROW_SPLITS + 1):
        @pl.when(jnp.logical_and(valid > (q - 1) * quarter, valid <= q * quarter))
        def _(q=q):
            ffn(q * quarter)

    @pl.when(valid == 0)
    def _():
        y_ref[...] = jnp.zeros_like(y_ref)


def _experts(block_e, n_used, block_valid, block_slot, block_next, xs, wgu, bgu, wdn, bdn, tm):
    n_rows = xs.shape[0]
    return pl.pallas_call(
        _expert_body,
        grid_spec=pltpu.PrefetchScalarGridSpec(
            num_scalar_prefetch=5, grid=(n_rows // tm,),
            in_specs=[pl.BlockSpec((tm, D_MODEL), lambda i, be, nu, *_: (jnp.minimum(i, nu[0] - 1), 0)),
                      pl.BlockSpec(memory_space=pl.ANY),
                      pl.BlockSpec((None, 1, 2 * D_FF), lambda i, be, *_: (be[i], 0, 0)),
                      pl.BlockSpec(memory_space=pl.ANY),
                      pl.BlockSpec((None, 1, D_MODEL), lambda i, be, *_: (be[i], 0, 0))],
            out_specs=pl.BlockSpec((tm, D_MODEL), lambda i, *_: (i, 0)),
            scratch_shapes=[pltpu.VMEM((2, D_MODEL, 2 * D_FF), F32), pltpu.VMEM((2, D_FF, D_MODEL), F32),
                            pltpu.VMEM((D_MODEL, 2 * D_FF), BF16), pltpu.VMEM((D_FF, D_MODEL), BF16),
                            pltpu.SemaphoreType.DMA((2, 2))]),
        out_shape=jax.ShapeDtypeStruct((n_rows, D_MODEL), F32),
        compiler_params=_cparams("arbitrary"),
        name="moe_experts",
    )(block_e, n_used, block_valid, block_slot, block_next, xs, wgu, bgu, wdn, bdn)


def _combine_body(tt, cnt_ref, off_ref, base_ref, loc_ref, g_ref, h_ref, hs_ref, gfin_ref, ys_hbm,
                  y_ref, ysmp_ref, buf_ref, sems):
    i = pl.program_id(0)
    last = pl.num_programs(0) - 1
    r = _sorted_rows(tt)

    def copies(tile, wait):
        b = tile % 2

        def make_copy(lo, go, size):
            return pltpu.make_async_copy(ys_hbm.at[pl.ds(go, size), :], buf_ref.at[b, pl.ds(lo, size), :],
                                         sems.at[b])

        _run_copies(tt, tile, cnt_ref, off_ref, base_ref, make_copy, wait)

    @pl.when(i == 0)
    def _():
        buf_ref[...] = jnp.zeros_like(buf_ref)
        copies(0, False)

    @pl.when(i < last)
    def _():
        copies(i + 1, False)

    copies(i, True)
    loc = loc_ref[...]
    gates = g_ref[...]
    slot_i = lax.broadcasted_iota(I32, (tt, r), 1)
    gmat = jnp.where(slot_i == loc[:, 0:1], gates[:, 0:1], 0.0)
    for k in range(1, TOP_K):
        gmat = gmat + jnp.where(slot_i == loc[:, k:k + 1], gates[:, k:k + 1], 0.0)
    h = jnp.where(i == last, _pad_rows(hs_ref[...], tt), h_ref[...])
    y = _rms(h + _dot(gmat.astype(BF16), buf_ref[i % 2].astype(BF16)), gfin_ref[...])

    @pl.when(i < last)
    def _():
        y_ref[...] = y

    @pl.when(i == last)
    def _():
        ysmp_ref[...] = y[0:ysmp_ref.shape[0], :]


def _combine(cnt, off, base, loc, gates, h2_p, h2_s, norm_final, ys, tt):
    ntp = h2_p.shape[0] // tt
    tok = pl.BlockSpec((tt, D_MODEL), lambda i: (jnp.minimum(i, ntp - 1), 0))
    smp = pl.BlockSpec(h2_s.shape, lambda i: (0, 0))
    tk = pl.BlockSpec((tt, TOP_K), lambda i: (i, 0))
    smem = pl.BlockSpec(memory_space=pltpu.SMEM)
    return pl.pallas_call(
        functools.partial(_combine_body, tt),
        grid_spec=pltpu.PrefetchScalarGridSpec(
            num_scalar_prefetch=0, grid=(ntp + 1,),
            in_specs=[smem, smem, smem, tk, tk, tok, smp,
                      pl.BlockSpec((1, D_MODEL), lambda i: (0, 0)),
                      pl.BlockSpec(memory_space=pl.ANY)],
            out_specs=[tok, smp],
            scratch_shapes=[pltpu.VMEM((2, _sorted_rows(tt), D_MODEL), F32), pltpu.SemaphoreType.DMA((2,))]),
        out_shape=[jax.ShapeDtypeStruct(h2_p.shape, F32), jax.ShapeDtypeStruct(h2_s.shape, F32)],
        compiler_params=_cparams("arbitrary"),
        name="moe_combine",
    )(cnt, off, base, loc, gates, h2_p, h2_s, norm_final, ys)


def _moe_and_final_norm(hn_p, logits_p, h2_p, hn_s, logits_s, h2_s, w, tt, tm):
    nt = hn_p.shape[0] // tt + 1
    t = hn_p.shape[0] + hn_s.shape[0]
    gates, loc, cnt3, off3 = _route(logits_p, logits_s, tt)
    cnt = cnt3[:, 0, :]
    counts = jnp.sum(cnt, axis=0)
    padded = (counts + tm - 1) // tm * tm
    pad_end = jnp.cumsum(padded)
    start = pad_end - padded
    off = off3[:, 0, :]
    base = (start[None, :] + jnp.cumsum(cnt, axis=0) - cnt).astype(I32)
    n_blocks = (t * TOP_K + nt * N_EXPERTS * (SUBLANES - 1) + N_EXPERTS * (tm - 1)) // tm
    n_rows = n_blocks * tm
    block_start = jnp.arange(n_blocks, dtype=I32) * tm
    block_e = jnp.minimum(jnp.sum(block_start[:, None] >= pad_end[None, :], axis=-1), N_EXPERTS - 1).astype(I32)
    n_used = (pad_end[-1:] // tm).astype(I32)
    zero_starts = jnp.concatenate([jnp.where(padded > 0, pad_end - tm, -1),
                                   jnp.where(block_start >= pad_end[-1], block_start, -1)]).astype(I32)
    xs = _dispatch(cnt, off, base, zero_starts, loc, hn_p, hn_s, n_rows, tm, tt)
    e_ids = jnp.arange(N_EXPERTS, dtype=I32)
    block_hot = block_e[:, None] == e_ids[None, :]

    def per_block(table):
        return jnp.sum(jnp.where(block_hot, table[None, :], 0), axis=1).astype(I32)

    block_valid = jnp.clip(per_block(start + counts) - block_start, 0, tm)
    block_valid = jnp.where(block_start < pad_end[-1], block_valid, 0).astype(I32)
    present = padded > 0
    later = present[None, :] & (e_ids[None, :] > e_ids[:, None])
    next_e = jnp.min(jnp.where(later, e_ids[None, :], N_EXPERTS), axis=1)
    next_e = jnp.where(next_e < N_EXPERTS, next_e, -1).astype(I32)
    before = present[None, :] & (e_ids[None, :] < e_ids[:, None])
    run_slot = (jnp.sum(before.astype(I32), axis=1) % 2).astype(I32)
    ys = _experts(block_e, n_used, block_valid, per_block(run_slot), per_block(next_e), xs, w["w_gate_up"], w["b_gate_up"], w["w_down"], w["b_down"], tm)
    return _combine(cnt, off, base, loc, gates, h2_p, h2_s, w["norm_final"], ys, tt)


def kernel(x_prompt, x_sample, mem_prompt, state_ssm, state_mamba_conv, state_short_conv, cache_mem_k, cache_mem_v, norm_mix, w_in, w_mconv, b_mconv, dt_bias, a_log, d_skip, norm_ssm, w_sconv, w_out, norm_xattn, norm_mem, w_xq, w_xk, w_xv, w_xo, norm_moe, w_router, b_router, w_gate_up, b_gate_up, w_down, b_down, norm_final):
    nbp, seq, _ = x_prompt.shape
    nbs = x_sample.shape[0]
    dt_lo = SSM_INNER + SSM_CONV_DIM
    w_in0 = w_in[0]
    w_dt = w_in0[:, dt_lo:dt_lo + SSM_HEADS]
    w = {
        "norm_mix": norm_mix, "norm_ssm": norm_ssm, "norm_xattn": norm_xattn, "norm_moe": norm_moe,
        "norm_final": norm_final.reshape(1, D_MODEL),
        "w_a": w_in0[:, :dt_lo].astype(BF16),
        "w_dt": w_dt.astype(BF16), "w_dt_t": w_dt.T.astype(BF16),
        "w_b": w_in0[:, dt_lo + SSM_HEADS:].astype(BF16),
        "w_mconv": w_mconv[0], "b_mconv": b_mconv,
        "dt_bias": dt_bias, "dt_bias_t": dt_bias.reshape(SSM_HEADS, 1),
        "a_log": a_log, "a_log_t": a_log.reshape(SSM_HEADS, 1),
        "d_skip": jnp.repeat(d_skip, SSM_HEAD_DIM, axis=1),
        "w_sconv": w_sconv[0], "w_out": w_out[0].astype(BF16),
        "w_xq": w_xq[0].astype(BF16), "w_xo": w_xo[0].astype(BF16),
        "w_router": w_router[0].astype(BF16), "b_router": b_router,
        "w_gate_up": w_gate_up[0], "b_gate_up": b_gate_up[0].reshape(N_EXPERTS, 1, 2 * D_FF),
        "w_down": w_down[0], "b_down": b_down[0].reshape(N_EXPERTS, 1, D_MODEL),
    }

    k_p, v_p, kb, vb = _mem_kv(mem_prompt.reshape(nbp * N_MEM, D_MODEL), norm_mem,
                               w_xk[0].astype(BF16), w_xv[0].astype(BF16))
    h1, ssm_p, mconv_p, sconv_p = _prompt_mixer(x_prompt.reshape(nbp * seq, D_MODEL), nbp, w)
    h2, hn, logits = _prompt_attn(h1, kb, vb, nbp, w)

    xs2 = x_sample.reshape(nbs, D_MODEL)
    mstate_t = jnp.transpose(state_mamba_conv[0], (1, 0, 2))
    sstate_t = jnp.transpose(state_short_conv[0], (1, 0, 2))
    z, xs_, dtx, dec, bm, cm, yb, sga, mnew_t, snew_t = _sample_proj(xs2, mstate_t, sstate_t, w)
    ssm_s, y_s = _sample_state(dec, state_ssm[0].reshape(nbs, SSM_INNER, SSM_STATE), dtx, bm, cm)
    h1s, q_s = _sample_fin1(xs2, y_s, xs_, z, yb, sga, w)
    o_s = _sample_attn(q_s.reshape(nbs, 1, D_MODEL),
                       cache_mem_k[0], cache_mem_v[0])
    h2s, hns, logits_s = _sample_fin2(h1s, o_s.reshape(nbs, D_MODEL), w)
    y_prompt, y_sample = _moe_and_final_norm(hn, logits, h2, hns, logits_s, h2s, w, MIX_TILE, MOE_ROW_TILE)

    return (y_prompt.reshape(nbp, seq, D_MODEL),
            y_sample.reshape(nbs, 1, D_MODEL),
            ssm_p.reshape(1, nbp, SSM_HEADS, SSM_HEAD_DIM, SSM_STATE),
            mconv_p[None], sconv_p[None],
            k_p.reshape(1, nbp, N_MEM, XA_HEADS, XA_HEAD_DIM),
            v_p.reshape(1, nbp, N_MEM, XA_HEADS, XA_HEAD_DIM),
            ssm_s.reshape(1, nbs, SSM_HEADS, SSM_HEAD_DIM, SSM_STATE),
            jnp.transpose(mnew_t, (1, 0, 2))[None],
            jnp.transpose(snew_t, (1, 0, 2))[None])
```

```python
import functools

import jax
import jax.numpy as jnp
from jax import lax
from jax.experimental import pallas as pl
from jax.experimental.pallas import tpu as pltpu

F32 = jnp.float32
BF16 = jnp.bfloat16
I32 = jnp.int32

D_MODEL = 1024
N_MEM = 256
SSM_HEADS = 16
SSM_HEAD_DIM = 64
SSM_INNER = SSM_HEADS * SSM_HEAD_DIM
SSM_STATE = 128
SSM_GROUPS = 4
HEADS_PER_GROUP = SSM_HEADS // SSM_GROUPS
GROUP_WIDTH = SSM_INNER // SSM_GROUPS
SSM_CONV = 4
SSM_CONV_DIM = SSM_INNER + 2 * SSM_GROUPS * SSM_STATE
SC_CONV = 3
XA_HEADS = 4
XA_HEAD_DIM = D_MODEL // XA_HEADS
N_EXPERTS = 32
TOP_K = 4
D_FF = D_MODEL
SWIGLU_LIMIT = 7.0
SWIGLU_ALPHA = 1.702
EPS = 1e-6

LANES = 128
SUBLANES = 8
VMEM_LIMIT = 56 * 1024 * 1024

MIX_TILE = 256
MOE_ROW_TILE = 512
EXPERT_ROW_SPLITS = 4
STATE_BB = 8
ATTN_BB = 4

NT_DIMS = (((1,), (1,)), ((), ()))
TN_DIMS = (((0,), (0,)), ((), ()))


def _cparams(*sem):
    return pltpu.CompilerParams(dimension_semantics=sem, vmem_limit_bytes=VMEM_LIMIT)


def _const_spec(shape):
    nd = len(shape)
    return pl.BlockSpec(shape, lambda *_: (0,) * nd, pipeline_mode=pl.Buffered(1))


def _sigmoid(x):
    return 1.0 / (1.0 + jnp.exp(-x))


def _silu(x):
    return x * _sigmoid(x)


def _softplus(x):
    return jnp.maximum(x, 0.0) + jnp.log(1.0 + jnp.exp(-jnp.abs(x)))


def _rms(x, g):
    ms = jnp.mean(x * x, axis=-1, keepdims=True)
    return x * lax.rsqrt(ms + EPS) * g


def _dot(a, b):
    return jnp.dot(a, b, preferred_element_type=F32)


def _dot_nt(a, b):
    return lax.dot_general(a, b, NT_DIMS, preferred_element_type=F32)


def _expand_heads(v):
    rows = v.shape[0]
    lane = lax.broadcasted_iota(I32, (rows, LANES), 1)
    pieces = []
    for j in range(SSM_HEADS // 2):
        a = jnp.broadcast_to(v[:, 2 * j:2 * j + 1], (rows, LANES))
        b = jnp.broadcast_to(v[:, 2 * j + 1:2 * j + 2], (rows, LANES))
        pieces.append(jnp.where(lane < SSM_HEAD_DIM, a, b))
    return jnp.concatenate(pieces, axis=1)


def _cumsum(x, axis):
    idx = lax.broadcasted_iota(I32, x.shape, axis)
    shift = 1
    while shift < x.shape[axis]:
        x = x + jnp.where(idx >= shift, pltpu.roll(x, shift, axis), 0.0)
        shift *= 2
    return x


def _pad_rows(x, rows):
    return jnp.concatenate([x, jnp.zeros((rows - x.shape[0], x.shape[1]), x.dtype)], axis=0)


def _group_rmsnorm(u, g):
    outs = []
    for k in range(SSM_GROUPS):
        ug = u[:, k * GROUP_WIDTH:(k + 1) * GROUP_WIDTH]
        ms = jnp.mean(ug * ug, axis=-1, keepdims=True)
        outs.append(ug * lax.rsqrt(ms + EPS))
    return jnp.concatenate(outs, axis=1) * g


def _memkv_body(mem_ref, g_ref, wk_ref, wv_ref, k_ref, v_ref, kb_ref, vb_ref):
    mn = _rms(mem_ref[...], g_ref[...]).astype(BF16)
    k = _dot(mn, wk_ref[...])
    v = _dot(mn, wv_ref[...])
    k_ref[...] = k
    v_ref[...] = v
    kb_ref[...] = k.astype(BF16)
    vb_ref[...] = v.astype(BF16)


def _mem_kv(mem2d, norm_mem, wk, wv):
    rows = mem2d.shape[0]
    nb = rows // N_MEM
    blk = pl.BlockSpec((N_MEM, D_MODEL), lambda b: (b, 0))
    return pl.pallas_call(
        _memkv_body,
        grid=(nb,),
        in_specs=[blk, _const_spec((1, D_MODEL)), _const_spec((D_MODEL, D_MODEL)),
                  _const_spec((D_MODEL, D_MODEL))],
        out_specs=[blk, blk, blk, blk],
        out_shape=[jax.ShapeDtypeStruct((rows, D_MODEL), F32)] * 2
        + [jax.ShapeDtypeStruct((rows, D_MODEL), BF16)] * 2,
        compiler_params=_cparams("arbitrary"),
        name="mem_kv",
    )(mem2d, norm_mem, wk, wv)


def _mix_body(x_ref, gmix_ref, wa_ref, wdtc_ref, wdtr_ref, wb_ref, wmc_ref, bmc_ref,
              dtb_ref, dtbt_ref, alog_ref, alogt_ref, dskip_ref, gssm_ref, wsc_ref, wout_ref,
              h_ref, ssm_ref, mbuf_ref, sbuf_ref,
              st_ref, cbuf_ref, scbuf_ref):
    tq = MIX_TILE
    c = pl.program_id(1)

    @pl.when(c == 0)
    def _():
        st_ref[...] = jnp.zeros_like(st_ref)
        cbuf_ref[0:SUBLANES, :] = jnp.zeros((SUBLANES, SSM_CONV_DIM), F32)
        scbuf_ref[0:SUBLANES, :] = jnp.zeros((SUBLANES, D_MODEL), F32)

    x = x_ref[...]
    xn = _rms(x, gmix_ref[...]).astype(BF16)

    u = _dot(xn, wa_ref[:, SSM_INNER:])
    cbuf_ref[SUBLANES:SUBLANES + tq, :] = u
    wm = wmc_ref[...]
    conv = u * wm[SSM_CONV - 1:SSM_CONV, :] + bmc_ref[...]
    for k in range(SSM_CONV - 1):
        off = SUBLANES - (SSM_CONV - 1) + k
        conv = conv + cbuf_ref[off:off + tq, :] * wm[k:k + 1, :]
    tail = cbuf_ref[tq + SUBLANES - (SSM_CONV - 1):tq + SUBLANES, :]
    mbuf_ref[...] = tail
    cbuf_ref[SUBLANES - (SSM_CONV - 1):SUBLANES, :] = tail
    xbc = _silu(conv)
    xs = xbc[:, :SSM_INNER]
    bm = xbc[:, SSM_INNER:SSM_INNER + SSM_GROUPS * SSM_STATE]
    cm = xbc[:, SSM_INNER + SSM_GROUPS * SSM_STATE:]

    dt = _softplus(_dot(xn, wdtc_ref[...]) + dtb_ref[...])
    dtt = _softplus(_dot_nt(wdtr_ref[...], xn) + dtbt_ref[...])
    a_row = -jnp.exp(alog_ref[...])
    a_col = -jnp.exp(alogt_ref[...])
    row_i = lax.broadcasted_iota(I32, (tq, tq), 0)
    col_i = lax.broadcasted_iota(I32, (tq, tq), 1)
    causal = row_i >= col_i
    a_cum = _cumsum(dt * a_row, 0)
    a_cumt = _cumsum(dtt * a_col, 1)
    a_last = a_cum[tq - 1:tq, :]

    xdt = xs * _expand_heads(dt)
    in_decay = _expand_heads(jnp.exp(a_cum))
    to_end = _expand_heads(jnp.exp(a_last - a_cum))
    chunk_decay = _expand_heads(jnp.exp(a_last))
    xdt_b = xdt.astype(BF16)
    xend_b = (xdt * to_end).astype(BF16)
    lane = lax.broadcasted_iota(I32, (tq, LANES), 1)

    def proj_b(k):
        return _dot(xn, wb_ref[:, k * D_MODEL:(k + 1) * D_MODEL])

    pb = []
    y_groups = []
    for g in range(SSM_GROUPS):
        pb.append(proj_b(g))
        if g == 0:
            z = _dot(xn, wa_ref[:, :SSM_INNER])
        if g == 2:
            g_b = proj_b(SSM_GROUPS)
        cg = cm[:, g * SSM_STATE:(g + 1) * SSM_STATE].astype(BF16)
        bg_f = bm[:, g * SSM_STATE:(g + 1) * SSM_STATE]
        bg = bg_f.astype(BF16)
        scores = _dot_nt(cg, bg)
        gs = slice(g * GROUP_WIDTH, (g + 1) * GROUP_WIDTH)
        st_g = st_ref[:, gs]
        y_off = _dot(cg, st_g.astype(BF16)) * in_decay[:, gs]
        pair_out = []
        for pr in range(HEADS_PER_GROUP // 2):
            h0 = g * HEADS_PER_GROUP + 2 * pr
            xp = xdt_b[:, h0 * SSM_HEAD_DIM:(h0 + 2) * SSM_HEAD_DIM]
            ys = []
            for h in (h0, h0 + 1):
                seg = a_cum[:, h:h + 1] - a_cumt[h:h + 1, :]
                decay = jnp.where(causal, jnp.exp(jnp.minimum(seg, 0.0)), 0.0)
                ys.append(_dot((scores * decay).astype(BF16), xp))
            pair_out.append(jnp.where(lane < SSM_HEAD_DIM, ys[0], ys[1]))
        y_groups.append(jnp.concatenate(pair_out, axis=1) + y_off)
        st_ref[:, gs] = st_g * chunk_decay[:, gs] + _dot(bg_f.T.astype(BF16), xend_b[:, gs])
    y = jnp.concatenate(y_groups, axis=1) + dskip_ref[...] * xs
    y_a = _group_rmsnorm(y * _silu(z), gssm_ref[...])

    sc_b, sc_c, sc_v, g_a = pb
    cv = sc_c * sc_v
    scbuf_ref[SUBLANES:SUBLANES + tq, :] = cv
    ws = wsc_ref[...]
    uc = cv * ws[SC_CONV - 1:SC_CONV, :]
    for k in range(SC_CONV - 1):
        off = SUBLANES - (SC_CONV - 1) + k
        uc = uc + scbuf_ref[off:off + tq, :] * ws[k:k + 1, :]
    stail = scbuf_ref[tq + SUBLANES - (SC_CONV - 1):tq + SUBLANES, :]
    sbuf_ref[...] = stail
    scbuf_ref[SUBLANES - (SC_CONV - 1):SUBLANES, :] = stail
    merged = _sigmoid(g_a) * y_a + _sigmoid(g_b) * (sc_b * uc)
    h_ref[...] = x + _dot(merged.astype(BF16), wout_ref[...])

    @pl.when(c == pl.num_programs(1) - 1)
    def _():
        ssm_ref[...] = st_ref[...].T


def _prompt_mixer(x2d, nb, w):
    t = x2d.shape[0]
    nc = t // nb // MIX_TILE
    tok = pl.BlockSpec((MIX_TILE, D_MODEL), lambda b, c: (b * nc + c, 0))
    return pl.pallas_call(
        _mix_body,
        grid=(nb, nc),
        in_specs=[tok, _const_spec((1, D_MODEL)),
                  _const_spec((D_MODEL, SSM_INNER + SSM_CONV_DIM)),
                  _const_spec((D_MODEL, SSM_HEADS)), _const_spec((SSM_HEADS, D_MODEL)),
                  _const_spec((D_MODEL, 5 * D_MODEL)),
                  _const_spec((SSM_CONV, SSM_CONV_DIM)), _const_spec((1, SSM_CONV_DIM)),
                  _const_spec((1, SSM_HEADS)), _const_spec((SSM_HEADS, 1)),
                  _const_spec((1, SSM_HEADS)), _const_spec((SSM_HEADS, 1)),
                  _const_spec((1, SSM_INNER)), _const_spec((1, SSM_INNER)),
                  _const_spec((SC_CONV, D_MODEL)), _const_spec((D_MODEL, D_MODEL))],
        out_specs=[tok,
                   pl.BlockSpec((None, SSM_INNER, SSM_STATE), lambda b, c: (b, 0, 0)),
                   pl.BlockSpec((None, SSM_CONV - 1, SSM_CONV_DIM), lambda b, c: (b, 0, 0)),
                   pl.BlockSpec((None, SC_CONV - 1, D_MODEL), lambda b, c: (b, 0, 0))],
        out_shape=[jax.ShapeDtypeStruct((t, D_MODEL), F32),
                   jax.ShapeDtypeStruct((nb, SSM_INNER, SSM_STATE), F32),
                   jax.ShapeDtypeStruct((nb, SSM_CONV - 1, SSM_CONV_DIM), F32),
                   jax.ShapeDtypeStruct((nb, SC_CONV - 1, D_MODEL), F32)],
        scratch_shapes=[pltpu.VMEM((SSM_STATE, SSM_INNER), F32),
                        pltpu.VMEM((MIX_TILE + SUBLANES, SSM_CONV_DIM), F32),
                        pltpu.VMEM((MIX_TILE + SUBLANES, D_MODEL), F32)],
        compiler_params=_cparams("arbitrary", "arbitrary"),
        name="prompt_mixer",
    )(x2d, w["norm_mix"], w["w_a"], w["w_dt"], w["w_dt_t"], w["w_b"], w["w_mconv"], w["b_mconv"],
      w["dt_bias"], w["dt_bias_t"], w["a_log"], w["a_log_t"], w["d_skip"], w["norm_ssm"],
      w["w_sconv"], w["w_out"])


def _router_tail(h2, gmoe_ref, wr_ref, br_ref, h2_ref, hn_ref, lg_ref):
    h2_ref[...] = h2
    hn = _rms(h2, gmoe_ref[...])
    hn_ref[...] = hn
    lg_ref[...] = _dot_nt(wr_ref[...], hn.astype(BF16)) + br_ref[...]


def _attn_body(h_ref, gx_ref, wq_ref, k_ref, v_ref, wo_ref, gmoe_ref, wr_ref, br_ref,
               h2_ref, hn_ref, lg_ref):
    h = h_ref[...]
    hn = _rms(h, gx_ref[...]).astype(BF16)
    q = _dot(hn, wq_ref[...]).astype(BF16)
    outs = []
    for hd in range(XA_HEADS):
        sl = slice(hd * XA_HEAD_DIM, (hd + 1) * XA_HEAD_DIM)
        s = _dot_nt(q[:, sl], k_ref[:, sl]) * (XA_HEAD_DIM ** -0.5)
        e = jnp.exp(s - jnp.max(s, axis=-1, keepdims=True))
        p = e / jnp.sum(e, axis=-1, keepdims=True)
        outs.append(_dot(p.astype(BF16), v_ref[:, sl]))
    o = jnp.concatenate(outs, axis=1).astype(BF16)
    h2 = h + _dot(o, wo_ref[...])
    _router_tail(h2, gmoe_ref, wr_ref, br_ref, h2_ref, hn_ref, lg_ref)


def _prompt_attn(h1, kb, vb, nb, w):
    t = h1.shape[0]
    nc = t // nb // MIX_TILE
    tok = pl.BlockSpec((MIX_TILE, D_MODEL), lambda b, c: (b * nc + c, 0))
    kv = pl.BlockSpec((N_MEM, D_MODEL), lambda b, c: (b, 0))
    return pl.pallas_call(
        _attn_body,
        grid=(nb, nc),
        in_specs=[tok, _const_spec((1, D_MODEL)), _const_spec((D_MODEL, D_MODEL)), kv, kv,
                  _const_spec((D_MODEL, D_MODEL)), _const_spec((1, D_MODEL)),
                  _const_spec((N_EXPERTS, D_MODEL)), _const_spec((N_EXPERTS, 1))],
        out_specs=[tok, tok, pl.BlockSpec((N_EXPERTS, MIX_TILE), lambda b, c: (0, b * nc + c))],
        out_shape=[jax.ShapeDtypeStruct((t, D_MODEL), F32),
                   jax.ShapeDtypeStruct((t, D_MODEL), F32),
                   jax.ShapeDtypeStruct((N_EXPERTS, t), F32)],
        compiler_params=_cparams("arbitrary", "arbitrary"),
        name="prompt_attn",
    )(h1, w["norm_xattn"], w["w_xq"], kb, vb, w["w_xo"], w["norm_moe"], w["w_router"], w["b_router"])


def _sproj_body(x_ref, gmix_ref, wa_ref, wdtc_ref, wb_ref, wmc_ref, bmc_ref, dtb_ref, alog_ref,
                wsc_ref, mst_ref, sst_ref,
                z_ref, xs_ref, dtx_ref, dec_ref, bm_ref, cm_ref, yb_ref, sga_ref, mnew_ref, snew_ref):
    x = x_ref[...]
    xn = _rms(x, gmix_ref[...]).astype(BF16)
    pa = _dot(xn, wa_ref[...])
    z_ref[...] = pa[:, :SSM_INNER]
    u = pa[:, SSM_INNER:]
    wm = wmc_ref[...]
    conv = u * wm[SSM_CONV - 1:SSM_CONV, :] + bmc_ref[...]
    for k in range(SSM_CONV - 1):
        conv = conv + mst_ref[k] * wm[k:k + 1, :]
    for k in range(SSM_CONV - 2):
        mnew_ref[k] = mst_ref[k + 1]
    mnew_ref[SSM_CONV - 2] = u
    xbc = _silu(conv)
    xs = xbc[:, :SSM_INNER]
    xs_ref[...] = xs
    bm_ref[...] = xbc[:, SSM_INNER:SSM_INNER + SSM_GROUPS * SSM_STATE]
    cm_ref[...] = xbc[:, SSM_INNER + SSM_GROUPS * SSM_STATE:]
    dt = _softplus(_dot(xn, wdtc_ref[...]) + dtb_ref[...])
    dec_ref[...] = jnp.exp(dt * (-jnp.exp(alog_ref[...])))
    dtx_ref[...] = xs * _expand_heads(dt)
    pb = _dot(xn, wb_ref[...])
    cv = pb[:, D_MODEL:2 * D_MODEL] * pb[:, 2 * D_MODEL:3 * D_MODEL]
    ws = wsc_ref[...]
    uc = cv * ws[SC_CONV - 1:SC_CONV, :]
    for k in range(SC_CONV - 1):
        uc = uc + sst_ref[k] * ws[k:k + 1, :]
    for k in range(SC_CONV - 2):
        snew_ref[k] = sst_ref[k + 1]
    snew_ref[SC_CONV - 2] = cv
    yb_ref[...] = _sigmoid(pb[:, 4 * D_MODEL:5 * D_MODEL]) * (pb[:, 0:D_MODEL] * uc)
    sga_ref[...] = _sigmoid(pb[:, 3 * D_MODEL:4 * D_MODEL])


def _sample_proj(x, mstate_t, sstate_t, w):
    nb = x.shape[0]
    f = lambda *s: jax.ShapeDtypeStruct(s, F32)
    return pl.pallas_call(
        _sproj_body,
        out_shape=[f(nb, SSM_INNER), f(nb, SSM_INNER), f(nb, SSM_INNER), f(nb, SSM_HEADS),
                   f(nb, SSM_GROUPS * SSM_STATE), f(nb, SSM_GROUPS * SSM_STATE),
                   f(nb, D_MODEL), f(nb, D_MODEL),
                   f(SSM_CONV - 1, nb, SSM_CONV_DIM), f(SC_CONV - 1, nb, D_MODEL)],
        compiler_params=pltpu.CompilerParams(vmem_limit_bytes=VMEM_LIMIT),
        name="sample_proj",
    )(x, w["norm_mix"], w["w_a"], w["w_dt"], w["w_b"], w["w_mconv"], w["b_mconv"], w["dt_bias"],
      w["a_log"], w["w_sconv"], mstate_t, sstate_t)


def _sstate_body(dec_ref, s_ref, dtx_ref, bm_ref, cm_ref, snew_ref, y_ref):
    i = pl.program_id(0)
    rows_per_blk = LANES
    for j in range(STATE_BB):
        b = i * STATE_BB + j
        dtx_row = dtx_ref[j:j + 1, :]
        y_parts = []
        for g in range(SSM_GROUPS):
            b_row = bm_ref[j:j + 1, g * SSM_STATE:(g + 1) * SSM_STATE]
            c_row = cm_ref[j:j + 1, g * SSM_STATE:(g + 1) * SSM_STATE].astype(BF16)
            new_blocks = []
            for q in range(GROUP_WIDTH // rows_per_blk):
                r0 = g * GROUP_WIDTH + q * rows_per_blk
                dcol = jnp.broadcast_to(dtx_row[:, r0:r0 + rows_per_blk], (rows_per_blk, LANES)).T
                sub = []
                for hh in range(rows_per_blk // SSM_HEAD_DIM):
                    h = r0 // SSM_HEAD_DIM + hh
                    lo = hh * SSM_HEAD_DIM
                    s_old = s_ref[j, r0 + lo:r0 + lo + SSM_HEAD_DIM, :]
                    sub.append(s_old * dec_ref[b, h] + dcol[lo:lo + SSM_HEAD_DIM, :] * b_row)
                blk = jnp.concatenate(sub, axis=0)
                snew_ref[j, r0:r0 + rows_per_blk, :] = blk
                new_blocks.append(blk.astype(BF16))
            s_g = jnp.concatenate(new_blocks, axis=0)
            y_parts.append(_dot_nt(c_row, s_g))
        y_ref[j:j + 1, :] = jnp.concatenate(y_parts, axis=1)


def _sample_state(dec, state, dtx, bm, cm):
    nb = state.shape[0]
    row = lambda wdt: pl.BlockSpec((STATE_BB, wdt), lambda i, dec: (i, 0))
    st = pl.BlockSpec((STATE_BB, SSM_INNER, SSM_STATE), lambda i, dec: (i, 0, 0))
    return pl.pallas_call(
        _sstate_body,
        grid_spec=pltpu.PrefetchScalarGridSpec(
            num_scalar_prefetch=1, grid=(nb // STATE_BB,),
            in_specs=[st, row(SSM_INNER), row(SSM_GROUPS * SSM_STATE), row(SSM_GROUPS * SSM_STATE)],
            out_specs=[st, row(SSM_INNER)]),
        out_shape=[jax.ShapeDtypeStruct(state.shape, F32), jax.ShapeDtypeStruct((nb, SSM_INNER), F32)],
        compiler_params=_cparams("arbitrary"),
        name="sample_state",
    )(dec, state, dtx, bm, cm)


def _sfin1_body(x_ref, y_ref, xs_ref, z_ref, yb_ref, sga_ref, dskip_ref, gssm_ref, wout_ref,
                gx_ref, wq_ref, h_ref, q_ref):
    y = y_ref[...] + dskip_ref[...] * xs_ref[...]
    y_a = _group_rmsnorm(y * _silu(z_ref[...]), gssm_ref[...])
    merged = sga_ref[...] * y_a + yb_ref[...]
    h = x_ref[...] + _dot(merged.astype(BF16), wout_ref[...])
    h_ref[...] = h
    q_ref[...] = _dot(_rms(h, gx_ref[...]).astype(BF16), wq_ref[...])


def _sample_fin1(x, y, xs, z, yb, sga, w):
    nb = x.shape[0]
    return pl.pallas_call(
        _sfin1_body,
        out_shape=[jax.ShapeDtypeStruct((nb, D_MODEL), F32)] * 2,
        compiler_params=pltpu.CompilerParams(vmem_limit_bytes=VMEM_LIMIT),
        name="sample_fin1",
    )(x, y, xs, z, yb, sga, w["d_skip"], w["norm_ssm"], w["w_out"], w["norm_xattn"], w["w_xq"])


def _sattn_body(q_ref, k_ref, v_ref, o_ref):
    for j in range(ATTN_BB):
        q_row = q_ref[j]
        q4 = jnp.concatenate([q_row[:, h * XA_HEAD_DIM:(h + 1) * XA_HEAD_DIM]
                              for h in range(XA_HEADS)], axis=0)
        s = jnp.sum(k_ref[j] * q4[None], axis=-1, keepdims=True) * (XA_HEAD_DIM ** -0.5)
        e = jnp.exp(s - jnp.max(s, axis=0, keepdims=True))
        p = e / jnp.sum(e, axis=0, keepdims=True)
        o4 = jnp.sum(p * v_ref[j], axis=0)
        o_ref[j] = jnp.concatenate([o4[h:h + 1, :] for h in range(XA_HEADS)], axis=1)


def _sample_attn(q3, k3, v3):
    nb = q3.shape[0]
    qs = pl.BlockSpec((ATTN_BB, 1, D_MODEL), lambda i: (i, 0, 0))
    kv = pl.BlockSpec((ATTN_BB, N_MEM, XA_HEADS, XA_HEAD_DIM), lambda i: (i, 0, 0, 0))
    return pl.pallas_call(
        _sattn_body,
        grid=(nb // ATTN_BB,),
        in_specs=[qs, kv, kv],
        out_specs=qs,
        out_shape=jax.ShapeDtypeStruct((nb, 1, D_MODEL), F32),
        compiler_params=_cparams("arbitrary"),
        name="sample_attn",
    )(q3, k3, v3)


def _sfin2_body(h_ref, o_ref, wo_ref, gmoe_ref, wr_ref, br_ref, h2_ref, hn_ref, lg_ref):
    h2 = h_ref[...] + _dot(o_ref[...].astype(BF16), wo_ref[...])
    _router_tail(h2, gmoe_ref, wr_ref, br_ref, h2_ref, hn_ref, lg_ref)


def _sample_fin2(h1, o, w):
    nb = h1.shape[0]
    return pl.pallas_call(
        _sfin2_body,
        out_shape=[jax.ShapeDtypeStruct((nb, D_MODEL), F32)] * 2
        + [jax.ShapeDtypeStruct((N_EXPERTS, nb), F32)],
        compiler_params=pltpu.CompilerParams(vmem_limit_bytes=VMEM_LIMIT),
        name="sample_fin2",
    )(h1, o, w["w_xo"], w["norm_moe"], w["w_router"], w["b_router"])


def _pad_cols(x, cols):
    return jnp.concatenate([x, jnp.zeros((x.shape[0], cols - x.shape[1]), x.dtype)], axis=1)


def _route_body(lgp_ref, lgs_ref, g_ref, loc_ref, cnt_ref, off_ref):
    tt = lgp_ref.shape[1]
    is_sample = pl.program_id(0) == pl.num_programs(0) - 1
    col = lax.broadcasted_iota(I32, (1, tt), 1)
    valid = jnp.logical_or(jnp.logical_not(is_sample), col < lgs_ref.shape[1])
    work = jnp.where(is_sample, _pad_cols(lgs_ref[...], tt), lgp_ref[...])
    sub = lax.broadcasted_iota(I32, (N_EXPERTS, tt), 0).astype(F32)
    vals, hots = [], []
    for _ in range(TOP_K):
        m = jnp.max(work, axis=0, keepdims=True)
        idx = jnp.min(jnp.where(work == m, sub, float(N_EXPERTS)), axis=0, keepdims=True)
        hot = (sub == idx) & valid
        vals.append(m)
        hots.append(hot)
        work = jnp.where(hot, -jnp.inf, work)
    exps = [jnp.exp(v - vals[0]) for v in vals]
    tot = exps[0]
    for e in exps[1:]:
        tot = tot + e
    assigned = hots[0]
    for hot in hots[1:]:
        assigned = assigned | hot
    a = assigned.astype(BF16)
    r_i = lax.broadcasted_iota(I32, (tt, tt), 0)
    c_i = lax.broadcasted_iota(I32, (tt, tt), 1)
    rank = _dot(a, (r_i < c_i).astype(BF16))
    cnt = jnp.sum(a.astype(F32), axis=1, keepdims=True)
    cnt = jnp.floor((cnt + (SUBLANES - 1)) * (1.0 / SUBLANES)) * SUBLANES
    e_r = lax.broadcasted_iota(I32, (N_EXPERTS, N_EXPERTS), 0)
    e_c = lax.broadcasted_iota(I32, (N_EXPERTS, N_EXPERTS), 1)
    cnt_cols = jnp.broadcast_to(cnt, (N_EXPERTS, LANES)).astype(BF16)
    off = _dot((e_r > e_c).astype(BF16), cnt_cols)[:, 0:1]
    slot = rank + off
    k_sub = lax.broadcasted_iota(I32, (SUBLANES, tt), 0)
    g_out = jnp.zeros((SUBLANES, tt), F32)
    l_out = jnp.full((SUBLANES, tt), -1.0, F32)
    for k in range(TOP_K):
        lk = jnp.sum(jnp.where(hots[k], slot, 0.0), axis=0, keepdims=True)
        g_out = jnp.where(k_sub == k, jnp.where(valid, exps[k] / tot, 0.0), g_out)
        l_out = jnp.where(k_sub == k, jnp.where(valid, lk, -1.0), l_out)
    g_ref[...] = g_out
    loc_ref[...] = l_out.astype(I32)
    cnt_ref[...] = cnt.astype(I32)
    off_ref[...] = off.astype(I32)


def _route(logits_p, logits_s, tt):
    ntp = logits_p.shape[1] // tt
    nt = ntp + 1
    t = nt * tt
    tk = pl.BlockSpec((SUBLANES, tt), lambda i: (0, i))
    per_tile = pl.BlockSpec((None, N_EXPERTS, 1), lambda i: (i, 0, 0))
    return pl.pallas_call(
        _route_body,
        grid=(nt,),
        in_specs=[pl.BlockSpec((N_EXPERTS, tt), lambda i: (0, jnp.minimum(i, ntp - 1))),
                  pl.BlockSpec(logits_s.shape, lambda i: (0, 0))],
        out_specs=[tk, tk, per_tile, per_tile],
        out_shape=[jax.ShapeDtypeStruct((SUBLANES, t), F32), jax.ShapeDtypeStruct((SUBLANES, t), I32),
                   jax.ShapeDtypeStruct((nt, N_EXPERTS, 1), I32), jax.ShapeDtypeStruct((nt, N_EXPERTS, 1), I32)],
        compiler_params=_cparams("arbitrary"),
        name="moe_route",
    )(logits_p, logits_s)


def _sorted_rows(tt):
    return tt * TOP_K + N_EXPERTS * SUBLANES


def _run_copies(tt, tile, cnt_ref, off_ref, base_ref, make_copy, wait):
    if wait:
        total = off_ref[tile, N_EXPERTS - 1] + cnt_ref[tile, N_EXPERTS - 1]

        @pl.when(total > 0)
        def _():
            make_copy(0, 0, pl.multiple_of(total, SUBLANES)).wait()
        return

    def per_expert(e, carry):
        n = cnt_ref[tile, e]

        @pl.when(n > 0)
        def _():
            make_copy(pl.multiple_of(off_ref[tile, e], SUBLANES),
                      pl.multiple_of(base_ref[tile, e], SUBLANES), pl.multiple_of(n, SUBLANES)).start()
        return carry

    lax.fori_loop(0, N_EXPERTS, per_expert, 0)


def _dispatch_body(tm, tt, cnt_ref, off_ref, base_ref, zstart_ref, loc_ref, x_ref, xs_ref, o_hbm,
                   zero_ref, srt_ref, zsem, sems):
    i = pl.program_id(0)
    last = pl.num_programs(0) - 1
    r = _sorted_rows(tt)

    def zero_copy(j):
        return pltpu.make_async_copy(zero_ref, o_hbm.at[pl.ds(pl.multiple_of(zstart_ref[j], tm), tm), :], zsem)

    @pl.when(i == 0)
    def _():
        zero_ref[...] = jnp.zeros_like(zero_ref)

        def start(j, carry):
            @pl.when(zstart_ref[j] >= 0)
            def _():
                zero_copy(j).start()
            return carry

        def wait(j, carry):
            @pl.when(zstart_ref[j] >= 0)
            def _():
                zero_copy(j).wait()
            return carry

        lax.fori_loop(0, zstart_ref.shape[0], start, 0)
        lax.fori_loop(0, zstart_ref.shape[0], wait, 0)

    loc = loc_ref[...]
    slot_i = lax.broadcasted_iota(I32, (r, tt), 0)
    hit = slot_i == loc[0:1, :]
    for k in range(1, TOP_K):
        hit = hit | (slot_i == loc[k:k + 1, :])
    buf = i % 2
    x = jnp.where(i == last, _pad_rows(xs_ref[...], tt), x_ref[...])
    srt_ref[buf] = _dot(hit.astype(BF16), x.astype(BF16))

    def copies(tile, wait):
        b = tile % 2

        def make_copy(lo, go, size):
            return pltpu.make_async_copy(srt_ref.at[b, pl.ds(lo, size), :], o_hbm.at[pl.ds(go, size), :],
                                         sems.at[b])

        _run_copies(tt, tile, cnt_ref, off_ref, base_ref, make_copy, wait)

    copies(i, False)

    @pl.when(i > 0)
    def _():
        copies(i - 1, True)

    @pl.when(i == last)
    def _():
        copies(i, True)


def _dispatch(cnt, off, base, zero_starts, loc, hn_p, hn_s, n_rows, tm, tt):
    ntp = hn_p.shape[0] // tt
    smem = pl.BlockSpec(memory_space=pltpu.SMEM)
    return pl.pallas_call(
        functools.partial(_dispatch_body, tm, tt),
        grid_spec=pltpu.PrefetchScalarGridSpec(
            num_scalar_prefetch=0, grid=(ntp + 1,),
            in_specs=[smem, smem, smem, smem,
                      pl.BlockSpec((SUBLANES, tt), lambda i: (0, i)),
                      pl.BlockSpec((tt, D_MODEL), lambda i: (jnp.minimum(i, ntp - 1), 0)),
                      pl.BlockSpec(hn_s.shape, lambda i: (0, 0))],
            out_specs=pl.BlockSpec(memory_space=pl.ANY),
            scratch_shapes=[pltpu.VMEM((tm, D_MODEL), F32), pltpu.VMEM((2, _sorted_rows(tt), D_MODEL), F32),
                            pltpu.SemaphoreType.DMA, pltpu.SemaphoreType.DMA((2,))]),
        out_shape=jax.ShapeDtypeStruct((n_rows, D_MODEL), F32),
        compiler_params=_cparams("arbitrary"),
        name="moe_dispatch",
    )(cnt, off, base, zero_starts, loc, hn_p, hn_s)


def _expert_body(be_ref, nu_ref, bv_ref, slot_ref, nxt_ref, x_ref, wgu_hbm, bgu_ref, wdn_hbm, bdn_ref,
                 y_ref, wgu_f, wdn_f, wgu_b, wdn_b, sems):
    i = pl.program_id(0)
    tm = x_ref.shape[0]
    valid = bv_ref[i]
    expert = be_ref[i]
    slot = slot_ref[i]

    def weight_copies(e, s):
        return (pltpu.make_async_copy(wgu_hbm.at[e], wgu_f.at[s], sems.at[0, s]),
                pltpu.make_async_copy(wdn_hbm.at[e], wdn_f.at[s], sems.at[1, s]))

    @pl.when(i == 0)
    def _():
        for cp in weight_copies(expert, slot):
            cp.start()

    @pl.when(jnp.logical_and(i < nu_ref[0], jnp.logical_or(i == 0, expert != be_ref[jnp.maximum(i - 1, 0)])))
    def _():
        for cp in weight_copies(expert, slot):
            cp.wait()
        wgu_b[...] = wgu_f[slot].astype(BF16)
        wdn_b[...] = wdn_f[slot].astype(BF16)

        @pl.when(nxt_ref[i] >= 0)
        def _():
            for cp in weight_copies(nxt_ref[i], 1 - slot):
                cp.start()

    def ffn(rows):
        gu = _dot(x_ref[0:rows, :].astype(BF16), wgu_b[...]) + bgu_ref[...]
        gate = jnp.minimum(gu[:, :D_FF], SWIGLU_LIMIT)
        up = jnp.clip(gu[:, D_FF:], -SWIGLU_LIMIT, SWIGLU_LIMIT)
        act = (up + 1.0) * (gate * _sigmoid(SWIGLU_ALPHA * gate))
        y_ref[0:rows, :] = _dot(act.astype(BF16), wdn_b[...]) + bdn_ref[...]
        if rows < tm:
            y_ref[rows:tm, :] = jnp.zeros((tm - rows, D_MODEL), F32)

    quarter = tm // EXPERT_ROW_SPLITS
    for q in range(1, EXPERT_ROW_SPLITS + 1):
        @pl.when(jnp.logical_and(valid > (q - 1) * quarter, valid <= q * quarter))
        def _(q=q):
            ffn(q * quarter)

    @pl.when(valid == 0)
    def _():
        y_ref[...] = jnp.zeros_like(y_ref)


def _experts(block_e, n_used, block_valid, block_slot, block_next, xs, wgu, bgu, wdn, bdn, tm):
    n_rows = xs.shape[0]
    return pl.pallas_call(
        _expert_body,
        grid_spec=pltpu.PrefetchScalarGridSpec(
            num_scalar_prefetch=5, grid=(n_rows // tm,),
            in_specs=[pl.BlockSpec((tm, D_MODEL), lambda i, be, nu, *_: (jnp.minimum(i, nu[0] - 1), 0)),
                      pl.BlockSpec(memory_space=pl.ANY),
                      pl.BlockSpec((None, 1, 2 * D_FF), lambda i, be, *_: (be[i], 0, 0)),
                      pl.BlockSpec(memory_space=pl.ANY),
                      pl.BlockSpec((None, 1, D_MODEL), lambda i, be, *_: (be[i], 0, 0))],
            out_specs=pl.BlockSpec((tm, D_MODEL), lambda i, *_: (i, 0)),
            scratch_shapes=[pltpu.VMEM((2, D_MODEL, 2 * D_FF), F32), pltpu.VMEM((2, D_FF, D_MODEL), F32),
                            pltpu.VMEM((D_MODEL, 2 * D_FF), BF16), pltpu.VMEM((D_FF, D_MODEL), BF16),
                            pltpu.SemaphoreType.DMA((2, 2))]),
        out_shape=jax.ShapeDtypeStruct((n_rows, D_MODEL), F32),
        compiler_params=_cparams("arbitrary"),
        name="moe_experts",
    )(block_e, n_used, block_valid, block_slot, block_next, xs, wgu, bgu, wdn, bdn)


def _combine_body(tt, cnt_ref, off_ref, base_ref, loc_ref, g_ref, h_ref, hs_ref, gfin_ref, ys_hbm,
                  y_ref, ysmp_ref, buf_ref, sems):
    i = pl.program_id(0)
    last = pl.num_programs(0) - 1
    r = _sorted_rows(tt)

    def copies(tile, wait):
        b = tile % 2

        def make_copy(lo, go, size):
            return pltpu.make_async_copy(ys_hbm.at[pl.ds(go, size), :], buf_ref.at[b, pl.ds(lo, size), :],
                                         sems.at[b])

        _run_copies(tt, tile, cnt_ref, off_ref, base_ref, make_copy, wait)

    @pl.when(i == 0)
    def _():
        buf_ref[...] = jnp.zeros_like(buf_ref)
        copies(0, False)

    @pl.when(i < last)
    def _():
        copies(i + 1, False)

    copies(i, True)
    loc = loc_ref[...]
    gates = g_ref[...]
    slot_i = lax.broadcasted_iota(I32, (r, tt), 0)
    gmat = jnp.where(slot_i == loc[0:1, :], gates[0:1, :], 0.0)
    for k in range(1, TOP_K):
        gmat = gmat + jnp.where(slot_i == loc[k:k + 1, :], gates[k:k + 1, :], 0.0)
    h = jnp.where(i == last, _pad_rows(hs_ref[...], tt), h_ref[...])
    moe = lax.dot_general(gmat.astype(BF16), buf_ref[i % 2].astype(BF16), TN_DIMS, preferred_element_type=F32)
    y = _rms(h + moe, gfin_ref[...])

    @pl.when(i < last)
    def _():
        y_ref[...] = y

    @pl.when(i == last)
    def _():
        ysmp_ref[...] = y[0:ysmp_ref.shape[0], :]


def _combine(cnt, off, base, loc, gates, h2_p, h2_s, norm_final, ys, tt):
    ntp = h2_p.shape[0] // tt
    tok = pl.BlockSpec((tt, D_MODEL), lambda i: (jnp.minimum(i, ntp - 1), 0))
    smp = pl.BlockSpec(h2_s.shape, lambda i: (0, 0))
    tk = pl.BlockSpec((SUBLANES, tt), lambda i: (0, i))
    smem = pl.BlockSpec(memory_space=pltpu.SMEM)
    return pl.pallas_call(
        functools.partial(_combine_body, tt),
        grid_spec=pltpu.PrefetchScalarGridSpec(
            num_scalar_prefetch=0, grid=(ntp + 1,),
            in_specs=[smem, smem, smem, tk, tk, tok, smp,
                      pl.BlockSpec((1, D_MODEL), lambda i: (0, 0)),
                      pl.BlockSpec(memory_space=pl.ANY)],
            out_specs=[tok, smp],
            scratch_shapes=[pltpu.VMEM((2, _sorted_rows(tt), D_MODEL), F32), pltpu.SemaphoreType.DMA((2,))]),
        out_shape=[jax.ShapeDtypeStruct(h2_p.shape, F32), jax.ShapeDtypeStruct(h2_s.shape, F32)],
        compiler_params=_cparams("arbitrary"),
        name="moe_combine",
    )(cnt, off, base, loc, gates, h2_p, h2_s, norm_final, ys)


def _moe_and_final_norm(hn_p, logits_p, h2_p, hn_s, logits_s, h2_s, w, tt, tm):
    nt = hn_p.shape[0] // tt + 1
    t = hn_p.shape[0] + hn_s.shape[0]
    gates, loc, cnt3, off3 = _route(logits_p, logits_s, tt)
    cnt = cnt3[:, :, 0]
    counts = jnp.sum(cnt, axis=0)
    padded = (counts + tm - 1) // tm * tm
    pad_end = jnp.cumsum(padded)
    start = pad_end - padded
    off = off3[:, :, 0]
    base = (start[None, :] + jnp.cumsum(cnt, axis=0) - cnt).astype(I32)
    n_blocks = (t * TOP_K + nt * N_EXPERTS * (SUBLANES - 1) + N_EXPERTS * (tm - 1)) // tm
    n_rows = n_blocks * tm
    block_start = jnp.arange(n_blocks, dtype=I32) * tm
    block_e = jnp.minimum(jnp.sum(block_start[:, None] >= pad_end[None, :], axis=-1), N_EXPERTS - 1).astype(I32)
    n_used = (pad_end[-1:] // tm).astype(I32)
    zero_starts = jnp.concatenate([jnp.where(padded > 0, pad_end - tm, -1),
                                   jnp.where(block_start >= pad_end[-1], block_start, -1)]).astype(I32)
    xs = _dispatch(cnt, off, base, zero_starts, loc, hn_p, hn_s, n_rows, tm, tt)
    e_ids = jnp.arange(N_EXPERTS, dtype=I32)
    block_hot = block_e[:, None] == e_ids[None, :]

    def per_block(table):
        return jnp.sum(jnp.where(block_hot, table[None, :], 0), axis=1).astype(I32)

    block_valid = jnp.clip(per_block(start + counts) - block_start, 0, tm)
    block_valid = jnp.where(block_start < pad_end[-1], block_valid, 0).astype(I32)
    present = padded > 0
    later = present[None, :] & (e_ids[None, :] > e_ids[:, None])
    next_e = jnp.min(jnp.where(later, e_ids[None, :], N_EXPERTS), axis=1)
    next_e = jnp.where(next_e < N_EXPERTS, next_e, -1).astype(I32)
    before = present[None, :] & (e_ids[None, :] < e_ids[:, None])
    run_slot = (jnp.sum(before.astype(I32), axis=1) % 2).astype(I32)
    ys = _experts(block_e, n_used, block_valid, per_block(run_slot), per_block(next_e), xs, w["w_gate_up"], w["b_gate_up"], w["w_down"], w["b_down"], tm)
    return _combine(cnt, off, base, loc, gates, h2_p, h2_s, w["norm_final"], ys, tt)


def kernel(x_prompt, x_sample, mem_prompt, state_ssm, state_mamba_conv, state_short_conv, cache_mem_k, cache_mem_v, norm_mix, w_in, w_mconv, b_mconv, dt_bias, a_log, d_skip, norm_ssm, w_sconv, w_out, norm_xattn, norm_mem, w_xq, w_xk, w_xv, w_xo, norm_moe, w_router, b_router, w_gate_up, b_gate_up, w_down, b_down, norm_final):
    nbp, seq, _ = x_prompt.shape
    nbs = x_sample.shape[0]
    dt_lo = SSM_INNER + SSM_CONV_DIM
    w_in0 = w_in[0]
    w_dt = w_in0[:, dt_lo:dt_lo + SSM_HEADS]
    w = {
        "norm_mix": norm_mix, "norm_ssm": norm_ssm, "norm_xattn": norm_xattn, "norm_moe": norm_moe,
        "norm_final": norm_final.reshape(1, D_MODEL),
        "w_a": w_in0[:, :dt_lo].astype(BF16),
        "w_dt": w_dt.astype(BF16), "w_dt_t": w_dt.T.astype(BF16),
        "w_b": w_in0[:, dt_lo + SSM_HEADS:].astype(BF16),
        "w_mconv": w_mconv[0], "b_mconv": b_mconv,
        "dt_bias": dt_bias, "dt_bias_t": dt_bias.reshape(SSM_HEADS, 1),
        "a_log": a_log, "a_log_t": a_log.reshape(SSM_HEADS, 1),
        "d_skip": jnp.repeat(d_skip, SSM_HEAD_DIM, axis=1),
        "w_sconv": w_sconv[0], "w_out": w_out[0].astype(BF16),
        "w_xq": w_xq[0].astype(BF16), "w_xo": w_xo[0].astype(BF16),
        "w_router": w_router[0].T.astype(BF16), "b_router": b_router.reshape(N_EXPERTS, 1),
        "w_gate_up": w_gate_up[0], "b_gate_up": b_gate_up[0].reshape(N_EXPERTS, 1, 2 * D_FF),
        "w_down": w_down[0], "b_down": b_down[0].reshape(N_EXPERTS, 1, D_MODEL),
    }

    k_p, v_p, kb, vb = _mem_kv(mem_prompt.reshape(nbp * N_MEM, D_MODEL), norm_mem,
                               w_xk[0].astype(BF16), w_xv[0].astype(BF16))
    h1, ssm_p, mconv_p, sconv_p = _prompt_mixer(x_prompt.reshape(nbp * seq, D_MODEL), nbp, w)
    h2, hn, logits = _prompt_attn(h1, kb, vb, nbp, w)

    xs2 = x_sample.reshape(nbs, D_MODEL)
    mstate_t = jnp.transpose(state_mamba_conv[0], (1, 0, 2))
    sstate_t = jnp.transpose(state_short_conv[0], (1, 0, 2))
    z, xs_, dtx, dec, bm, cm, yb, sga, mnew_t, snew_t = _sample_proj(xs2, mstate_t, sstate_t, w)
    ssm_s, y_s = _sample_state(dec, state_ssm[0].reshape(nbs, SSM_INNER, SSM_STATE), dtx, bm, cm)
    h1s, q_s = _sample_fin1(xs2, y_s, xs_, z, yb, sga, w)
    o_s = _sample_attn(q_s.reshape(nbs, 1, D_MODEL),
                       cache_mem_k[0], cache_mem_v[0])
    h2s, hns, logits_s = _sample_fin2(h1s, o_s.reshape(nbs, D_MODEL), w)
    y_prompt, y_sample = _moe_and_final_norm(hn, logits, h2, hns, logits_s, h2s, w, MIX_TILE, MOE_ROW_TILE)

    return (y_prompt.reshape(nbp, seq, D_MODEL),
            y_sample.reshape(nbs, 1, D_MODEL),
            ssm_p.reshape(1, nbp, SSM_HEADS, SSM_HEAD_DIM, SSM_STATE),
            mconv_p[None], sconv_p[None],
            k_p.reshape(1, nbp, N_MEM, XA_HEADS, XA_HEAD_DIM),
            v_p.reshape(1, nbp, N_MEM, XA_HEADS, XA_HEAD_DIM),
            ssm_s.reshape(1, nbs, SSM_HEADS, SSM_HEAD_DIM, SSM_STATE),
            jnp.transpose(mnew_t, (1, 0, 2))[None],
            jnp.transpose(snew_t, (1, 0, 2))[None])
```

```python
import functools

import jax
import jax.numpy as jnp
from jax import lax
from jax.experimental import pallas as pl
from jax.experimental.pallas import tpu as pltpu

F32 = jnp.float32
BF16 = jnp.bfloat16
I32 = jnp.int32

D_MODEL = 1024
N_MEM = 256
SSM_HEADS = 16
SSM_HEAD_DIM = 64
SSM_INNER = SSM_HEADS * SSM_HEAD_DIM
SSM_STATE = 128
SSM_GROUPS = 4
HEADS_PER_GROUP = SSM_HEADS // SSM_GROUPS
GROUP_WIDTH = SSM_INNER // SSM_GROUPS
SSM_CONV = 4
SSM_CONV_DIM = SSM_INNER + 2 * SSM_GROUPS * SSM_STATE
SC_CONV = 3
XA_HEADS = 4
XA_HEAD_DIM = D_MODEL // XA_HEADS
N_EXPERTS = 32
TOP_K = 4
D_FF = D_MODEL
SWIGLU_LIMIT = 7.0
SWIGLU_ALPHA = 1.702
EPS = 1e-6

LANES = 128
SUBLANES = 8
VMEM_LIMIT = 56 * 1024 * 1024

MIX_TILE = 256
MOE_ROW_TILE = 512
EXPERT_ROW_SPLITS = 4
EXPERT_COL_CHUNK = 512
STATE_BB = 8
ATTN_BB = 4

NT_DIMS = (((1,), (1,)), ((), ()))
TN_DIMS = (((0,), (0,)), ((), ()))


def _cparams(*sem):
    return pltpu.CompilerParams(dimension_semantics=sem, vmem_limit_bytes=VMEM_LIMIT)


def _const_spec(shape):
    nd = len(shape)
    return pl.BlockSpec(shape, lambda *_: (0,) * nd, pipeline_mode=pl.Buffered(1))


def _sigmoid(x):
    return 1.0 / (1.0 + jnp.exp(-x))


def _silu(x):
    return x * _sigmoid(x)


def _softplus(x):
    return jnp.maximum(x, 0.0) + jnp.log(1.0 + jnp.exp(-jnp.abs(x)))


def _rms(x, g):
    ms = jnp.mean(x * x, axis=-1, keepdims=True)
    return x * lax.rsqrt(ms + EPS) * g


def _dot(a, b):
    return jnp.dot(a, b, preferred_element_type=F32)


def _dot_nt(a, b):
    return lax.dot_general(a, b, NT_DIMS, preferred_element_type=F32)


def _expand_heads(v):
    rows = v.shape[0]
    lane = lax.broadcasted_iota(I32, (rows, LANES), 1)
    pieces = []
    for j in range(SSM_HEADS // 2):
        a = jnp.broadcast_to(v[:, 2 * j:2 * j + 1], (rows, LANES))
        b = jnp.broadcast_to(v[:, 2 * j + 1:2 * j + 2], (rows, LANES))
        pieces.append(jnp.where(lane < SSM_HEAD_DIM, a, b))
    return jnp.concatenate(pieces, axis=1)


def _cumsum(x, axis):
    idx = lax.broadcasted_iota(I32, x.shape, axis)
    shift = 1
    while shift < x.shape[axis]:
        x = x + jnp.where(idx >= shift, pltpu.roll(x, shift, axis), 0.0)
        shift *= 2
    return x


def _pad_rows(x, rows):
    return jnp.concatenate([x, jnp.zeros((rows - x.shape[0], x.shape[1]), x.dtype)], axis=0)


def _group_rmsnorm(u, g):
    outs = []
    for k in range(SSM_GROUPS):
        ug = u[:, k * GROUP_WIDTH:(k + 1) * GROUP_WIDTH]
        ms = jnp.mean(ug * ug, axis=-1, keepdims=True)
        outs.append(ug * lax.rsqrt(ms + EPS))
    return jnp.concatenate(outs, axis=1) * g


def _memkv_body(mem_ref, g_ref, wk_ref, wv_ref, k_ref, v_ref, kb_ref, vb_ref):
    mn = _rms(mem_ref[...], g_ref[...]).astype(BF16)
    k = _dot(mn, wk_ref[...])
    v = _dot(mn, wv_ref[...])
    k_ref[...] = k
    v_ref[...] = v
    kb_ref[...] = k.astype(BF16)
    vb_ref[...] = v.astype(BF16)


def _mem_kv(mem2d, norm_mem, wk, wv):
    rows = mem2d.shape[0]
    nb = rows // N_MEM
    blk = pl.BlockSpec((N_MEM, D_MODEL), lambda b: (b, 0))
    return pl.pallas_call(
        _memkv_body,
        grid=(nb,),
        in_specs=[blk, _const_spec((1, D_MODEL)), _const_spec((D_MODEL, D_MODEL)),
                  _const_spec((D_MODEL, D_MODEL))],
        out_specs=[blk, blk, blk, blk],
        out_shape=[jax.ShapeDtypeStruct((rows, D_MODEL), F32)] * 2
        + [jax.ShapeDtypeStruct((rows, D_MODEL), BF16)] * 2,
        compiler_params=_cparams("arbitrary"),
        name="mem_kv",
    )(mem2d, norm_mem, wk, wv)


def _mix_body(x_ref, gmix_ref, wa_ref, wdtc_ref, wdtr_ref, wb_ref, wmc_ref, bmc_ref,
              dtb_ref, dtbt_ref, alog_ref, alogt_ref, dskip_ref, gssm_ref, wsc_ref, wout_ref,
              h_ref, ssm_ref, mbuf_ref, sbuf_ref,
              st_ref, cbuf_ref, scbuf_ref):
    tq = MIX_TILE
    c = pl.program_id(1)

    @pl.when(c == 0)
    def _():
        st_ref[...] = jnp.zeros_like(st_ref)
        cbuf_ref[0:SUBLANES, :] = jnp.zeros((SUBLANES, SSM_CONV_DIM), F32)
        scbuf_ref[0:SUBLANES, :] = jnp.zeros((SUBLANES, D_MODEL), F32)

    x = x_ref[...]
    xn = _rms(x, gmix_ref[...]).astype(BF16)

    u = _dot(xn, wa_ref[:, SSM_INNER:])
    cbuf_ref[SUBLANES:SUBLANES + tq, :] = u
    wm = wmc_ref[...]
    conv = u * wm[SSM_CONV - 1:SSM_CONV, :] + bmc_ref[...]
    for k in range(SSM_CONV - 1):
        off = SUBLANES - (SSM_CONV - 1) + k
        conv = conv + cbuf_ref[off:off + tq, :] * wm[k:k + 1, :]
    tail = cbuf_ref[tq + SUBLANES - (SSM_CONV - 1):tq + SUBLANES, :]
    mbuf_ref[...] = tail
    cbuf_ref[SUBLANES - (SSM_CONV - 1):SUBLANES, :] = tail
    xbc = _silu(conv)
    xs = xbc[:, :SSM_INNER]
    bm = xbc[:, SSM_INNER:SSM_INNER + SSM_GROUPS * SSM_STATE]
    cm = xbc[:, SSM_INNER + SSM_GROUPS * SSM_STATE:]

    dt = _softplus(_dot(xn, wdtc_ref[...]) + dtb_ref[...])
    dtt = _softplus(_dot_nt(wdtr_ref[...], xn) + dtbt_ref[...])
    a_row = -jnp.exp(alog_ref[...])
    a_col = -jnp.exp(alogt_ref[...])
    row_i = lax.broadcasted_iota(I32, (tq, tq), 0)
    col_i = lax.broadcasted_iota(I32, (tq, tq), 1)
    causal = row_i >= col_i
    a_cum = _cumsum(dt * a_row, 0)
    a_cumt = _cumsum(dtt * a_col, 1)
    a_last = a_cum[tq - 1:tq, :]

    xdt = xs * _expand_heads(dt)
    in_decay = _expand_heads(jnp.exp(a_cum))
    to_end = _expand_heads(jnp.exp(a_last - a_cum))
    chunk_decay = _expand_heads(jnp.exp(a_last))
    xdt_b = xdt.astype(BF16)
    xend_b = (xdt * to_end).astype(BF16)
    lane = lax.broadcasted_iota(I32, (tq, LANES), 1)

    def proj_b(k):
        return _dot(xn, wb_ref[:, k * D_MODEL:(k + 1) * D_MODEL])

    pb = []
    y_groups = []
    for g in range(SSM_GROUPS):
        pb.append(proj_b(g))
        if g == 0:
            z = _dot(xn, wa_ref[:, :SSM_INNER])
        if g == 2:
            g_b = proj_b(SSM_GROUPS)
        cg = cm[:, g * SSM_STATE:(g + 1) * SSM_STATE].astype(BF16)
        bg_f = bm[:, g * SSM_STATE:(g + 1) * SSM_STATE]
        bg = bg_f.astype(BF16)
        scores = _dot_nt(cg, bg)
        gs = slice(g * GROUP_WIDTH, (g + 1) * GROUP_WIDTH)
        st_g = st_ref[:, gs]
        y_off = _dot(cg, st_g.astype(BF16)) * in_decay[:, gs]
        pair_out = []
        for pr in range(HEADS_PER_GROUP // 2):
            h0 = g * HEADS_PER_GROUP + 2 * pr
            xp = xdt_b[:, h0 * SSM_HEAD_DIM:(h0 + 2) * SSM_HEAD_DIM]
            ys = []
            for h in (h0, h0 + 1):
                seg = a_cum[:, h:h + 1] - a_cumt[h:h + 1, :]
                decay = jnp.where(causal, jnp.exp(jnp.minimum(seg, 0.0)), 0.0)
                ys.append(_dot((scores * decay).astype(BF16), xp))
            pair_out.append(jnp.where(lane < SSM_HEAD_DIM, ys[0], ys[1]))
        y_groups.append(jnp.concatenate(pair_out, axis=1) + y_off)
        st_ref[:, gs] = st_g * chunk_decay[:, gs] + _dot(bg_f.T.astype(BF16), xend_b[:, gs])
    y = jnp.concatenate(y_groups, axis=1) + dskip_ref[...] * xs
    y_a = _group_rmsnorm(y * _silu(z), gssm_ref[...])

    sc_b, sc_c, sc_v, g_a = pb
    cv = sc_c * sc_v
    scbuf_ref[SUBLANES:SUBLANES + tq, :] = cv
    ws = wsc_ref[...]
    uc = cv * ws[SC_CONV - 1:SC_CONV, :]
    for k in range(SC_CONV - 1):
        off = SUBLANES - (SC_CONV - 1) + k
        uc = uc + scbuf_ref[off:off + tq, :] * ws[k:k + 1, :]
    stail = scbuf_ref[tq + SUBLANES - (SC_CONV - 1):tq + SUBLANES, :]
    sbuf_ref[...] = stail
    scbuf_ref[SUBLANES - (SC_CONV - 1):SUBLANES, :] = stail
    merged = _sigmoid(g_a) * y_a + _sigmoid(g_b) * (sc_b * uc)
    h_ref[...] = x + _dot(merged.astype(BF16), wout_ref[...])

    @pl.when(c == pl.num_programs(1) - 1)
    def _():
        ssm_ref[...] = st_ref[...].T


def _prompt_mixer(x2d, nb, w):
    t = x2d.shape[0]
    nc = t // nb // MIX_TILE
    tok = pl.BlockSpec((MIX_TILE, D_MODEL), lambda b, c: (b * nc + c, 0))
    return pl.pallas_call(
        _mix_body,
        grid=(nb, nc),
        in_specs=[tok, _const_spec((1, D_MODEL)),
                  _const_spec((D_MODEL, SSM_INNER + SSM_CONV_DIM)),
                  _const_spec((D_MODEL, SSM_HEADS)), _const_spec((SSM_HEADS, D_MODEL)),
                  _const_spec((D_MODEL, 5 * D_MODEL)),
                  _const_spec((SSM_CONV, SSM_CONV_DIM)), _const_spec((1, SSM_CONV_DIM)),
                  _const_spec((1, SSM_HEADS)), _const_spec((SSM_HEADS, 1)),
                  _const_spec((1, SSM_HEADS)), _const_spec((SSM_HEADS, 1)),
                  _const_spec((1, SSM_INNER)), _const_spec((1, SSM_INNER)),
                  _const_spec((SC_CONV, D_MODEL)), _const_spec((D_MODEL, D_MODEL))],
        out_specs=[tok,
                   pl.BlockSpec((None, SSM_INNER, SSM_STATE), lambda b, c: (b, 0, 0)),
                   pl.BlockSpec((None, SSM_CONV - 1, SSM_CONV_DIM), lambda b, c: (b, 0, 0)),
                   pl.BlockSpec((None, SC_CONV - 1, D_MODEL), lambda b, c: (b, 0, 0))],
        out_shape=[jax.ShapeDtypeStruct((t, D_MODEL), F32),
                   jax.ShapeDtypeStruct((nb, SSM_INNER, SSM_STATE), F32),
                   jax.ShapeDtypeStruct((nb, SSM_CONV - 1, SSM_CONV_DIM), F32),
                   jax.ShapeDtypeStruct((nb, SC_CONV - 1, D_MODEL), F32)],
        scratch_shapes=[pltpu.VMEM((SSM_STATE, SSM_INNER), F32),
                        pltpu.VMEM((MIX_TILE + SUBLANES, SSM_CONV_DIM), F32),
                        pltpu.VMEM((MIX_TILE + SUBLANES, D_MODEL), F32)],
        compiler_params=_cparams("arbitrary", "arbitrary"),
        name="prompt_mixer",
    )(x2d, w["norm_mix"], w["w_a"], w["w_dt"], w["w_dt_t"], w["w_b"], w["w_mconv"], w["b_mconv"],
      w["dt_bias"], w["dt_bias_t"], w["a_log"], w["a_log_t"], w["d_skip"], w["norm_ssm"],
      w["w_sconv"], w["w_out"])


def _router_tail(h2, gmoe_ref, wr_ref, br_ref, h2_ref, hn_ref, lg_ref):
    h2_ref[...] = h2
    hn = _rms(h2, gmoe_ref[...])
    hn_ref[...] = hn
    lg_ref[...] = _dot_nt(wr_ref[...], hn.astype(BF16)) + br_ref[...]


def _attn_body(h_ref, gx_ref, wq_ref, k_ref, v_ref, wo_ref, gmoe_ref, wr_ref, br_ref,
               h2_ref, hn_ref, lg_ref):
    h = h_ref[...]
    hn = _rms(h, gx_ref[...]).astype(BF16)
    q = _dot(hn, wq_ref[...]).astype(BF16)
    outs = []
    for hd in range(XA_HEADS):
        sl = slice(hd * XA_HEAD_DIM, (hd + 1) * XA_HEAD_DIM)
        s = _dot_nt(q[:, sl], k_ref[:, sl]) * (XA_HEAD_DIM ** -0.5)
        e = jnp.exp(s - jnp.max(s, axis=-1, keepdims=True))
        p = e / jnp.sum(e, axis=-1, keepdims=True)
        outs.append(_dot(p.astype(BF16), v_ref[:, sl]))
    o = jnp.concatenate(outs, axis=1).astype(BF16)
    h2 = h + _dot(o, wo_ref[...])
    _router_tail(h2, gmoe_ref, wr_ref, br_ref, h2_ref, hn_ref, lg_ref)


def _prompt_attn(h1, kb, vb, nb, w):
    t = h1.shape[0]
    nc = t // nb // MIX_TILE
    tok = pl.BlockSpec((MIX_TILE, D_MODEL), lambda b, c: (b * nc + c, 0))
    kv = pl.BlockSpec((N_MEM, D_MODEL), lambda b, c: (b, 0))
    return pl.pallas_call(
        _attn_body,
        grid=(nb, nc),
        in_specs=[tok, _const_spec((1, D_MODEL)), _const_spec((D_MODEL, D_MODEL)), kv, kv,
                  _const_spec((D_MODEL, D_MODEL)), _const_spec((1, D_MODEL)),
                  _const_spec((N_EXPERTS, D_MODEL)), _const_spec((N_EXPERTS, 1))],
        out_specs=[tok, tok, pl.BlockSpec((N_EXPERTS, MIX_TILE), lambda b, c: (0, b * nc + c))],
        out_shape=[jax.ShapeDtypeStruct((t, D_MODEL), F32),
                   jax.ShapeDtypeStruct((t, D_MODEL), F32),
                   jax.ShapeDtypeStruct((N_EXPERTS, t), F32)],
        compiler_params=_cparams("arbitrary", "arbitrary"),
        name="prompt_attn",
    )(h1, w["norm_xattn"], w["w_xq"], kb, vb, w["w_xo"], w["norm_moe"], w["w_router"], w["b_router"])


def _sproj_body(x_ref, gmix_ref, wa_ref, wdtc_ref, wb_ref, wmc_ref, bmc_ref, dtb_ref, alog_ref,
                wsc_ref, mst_ref, sst_ref,
                z_ref, xs_ref, dtx_ref, dec_ref, bm_ref, cm_ref, yb_ref, sga_ref, mnew_ref, snew_ref):
    x = x_ref[...]
    xn = _rms(x, gmix_ref[...]).astype(BF16)
    pa = _dot(xn, wa_ref[...])
    z_ref[...] = pa[:, :SSM_INNER]
    u = pa[:, SSM_INNER:]
    wm = wmc_ref[...]
    conv = u * wm[SSM_CONV - 1:SSM_CONV, :] + bmc_ref[...]
    for k in range(SSM_CONV - 1):
        conv = conv + mst_ref[k] * wm[k:k + 1, :]
    for k in range(SSM_CONV - 2):
        mnew_ref[k] = mst_ref[k + 1]
    mnew_ref[SSM_CONV - 2] = u
    xbc = _silu(conv)
    xs = xbc[:, :SSM_INNER]
    xs_ref[...] = xs
    bm_ref[...] = xbc[:, SSM_INNER:SSM_INNER + SSM_GROUPS * SSM_STATE]
    cm_ref[...] = xbc[:, SSM_INNER + SSM_GROUPS * SSM_STATE:]
    dt = _softplus(_dot(xn, wdtc_ref[...]) + dtb_ref[...])
    dec_ref[...] = jnp.exp(dt * (-jnp.exp(alog_ref[...])))
    dtx_ref[...] = xs * _expand_heads(dt)
    pb = _dot(xn, wb_ref[...])
    cv = pb[:, D_MODEL:2 * D_MODEL] * pb[:, 2 * D_MODEL:3 * D_MODEL]
    ws = wsc_ref[...]
    uc = cv * ws[SC_CONV - 1:SC_CONV, :]
    for k in range(SC_CONV - 1):
        uc = uc + sst_ref[k] * ws[k:k + 1, :]
    for k in range(SC_CONV - 2):
        snew_ref[k] = sst_ref[k + 1]
    snew_ref[SC_CONV - 2] = cv
    yb_ref[...] = _sigmoid(pb[:, 4 * D_MODEL:5 * D_MODEL]) * (pb[:, 0:D_MODEL] * uc)
    sga_ref[...] = _sigmoid(pb[:, 3 * D_MODEL:4 * D_MODEL])


def _sample_proj(x, mstate_t, sstate_t, w):
    nb = x.shape[0]
    f = lambda *s: jax.ShapeDtypeStruct(s, F32)
    return pl.pallas_call(
        _sproj_body,
        out_shape=[f(nb, SSM_INNER), f(nb, SSM_INNER), f(nb, SSM_INNER), f(nb, SSM_HEADS),
                   f(nb, SSM_GROUPS * SSM_STATE), f(nb, SSM_GROUPS * SSM_STATE),
                   f(nb, D_MODEL), f(nb, D_MODEL),
                   f(SSM_CONV - 1, nb, SSM_CONV_DIM), f(SC_CONV - 1, nb, D_MODEL)],
        compiler_params=pltpu.CompilerParams(vmem_limit_bytes=VMEM_LIMIT),
        name="sample_proj",
    )(x, w["norm_mix"], w["w_a"], w["w_dt"], w["w_b"], w["w_mconv"], w["b_mconv"], w["dt_bias"],
      w["a_log"], w["w_sconv"], mstate_t, sstate_t)


def _sstate_body(dec_ref, s_ref, dtx_ref, bm_ref, cm_ref, snew_ref, y_ref):
    i = pl.program_id(0)
    rows_per_blk = LANES
    for j in range(STATE_BB):
        b = i * STATE_BB + j
        dtx_row = dtx_ref[j:j + 1, :]
        y_parts = []
        for g in range(SSM_GROUPS):
            b_row = bm_ref[j:j + 1, g * SSM_STATE:(g + 1) * SSM_STATE]
            c_row = cm_ref[j:j + 1, g * SSM_STATE:(g + 1) * SSM_STATE].astype(BF16)
            new_blocks = []
            for q in range(GROUP_WIDTH // rows_per_blk):
                r0 = g * GROUP_WIDTH + q * rows_per_blk
                dcol = jnp.broadcast_to(dtx_row[:, r0:r0 + rows_per_blk], (rows_per_blk, LANES)).T
                sub = []
                for hh in range(rows_per_blk // SSM_HEAD_DIM):
                    h = r0 // SSM_HEAD_DIM + hh
                    lo = hh * SSM_HEAD_DIM
                    s_old = s_ref[j, r0 + lo:r0 + lo + SSM_HEAD_DIM, :]
                    sub.append(s_old * dec_ref[b, h] + dcol[lo:lo + SSM_HEAD_DIM, :] * b_row)
                blk = jnp.concatenate(sub, axis=0)
                snew_ref[j, r0:r0 + rows_per_blk, :] = blk
                new_blocks.append(blk.astype(BF16))
            s_g = jnp.concatenate(new_blocks, axis=0)
            y_parts.append(_dot_nt(c_row, s_g))
        y_ref[j:j + 1, :] = jnp.concatenate(y_parts, axis=1)


def _sample_state(dec, state, dtx, bm, cm):
    nb = state.shape[0]
    row = lambda wdt: pl.BlockSpec((STATE_BB, wdt), lambda i, dec: (i, 0))
    st = pl.BlockSpec((STATE_BB, SSM_INNER, SSM_STATE), lambda i, dec: (i, 0, 0))
    return pl.pallas_call(
        _sstate_body,
        grid_spec=pltpu.PrefetchScalarGridSpec(
            num_scalar_prefetch=1, grid=(nb // STATE_BB,),
            in_specs=[st, row(SSM_INNER), row(SSM_GROUPS * SSM_STATE), row(SSM_GROUPS * SSM_STATE)],
            out_specs=[st, row(SSM_INNER)]),
        out_shape=[jax.ShapeDtypeStruct(state.shape, F32), jax.ShapeDtypeStruct((nb, SSM_INNER), F32)],
        compiler_params=_cparams("arbitrary"),
        name="sample_state",
    )(dec, state, dtx, bm, cm)


def _sfin1_body(x_ref, y_ref, xs_ref, z_ref, yb_ref, sga_ref, dskip_ref, gssm_ref, wout_ref,
                gx_ref, wq_ref, h_ref, q_ref):
    y = y_ref[...] + dskip_ref[...] * xs_ref[...]
    y_a = _group_rmsnorm(y * _silu(z_ref[...]), gssm_ref[...])
    merged = sga_ref[...] * y_a + yb_ref[...]
    h = x_ref[...] + _dot(merged.astype(BF16), wout_ref[...])
    h_ref[...] = h
    q_ref[...] = _dot(_rms(h, gx_ref[...]).astype(BF16), wq_ref[...])


def _sample_fin1(x, y, xs, z, yb, sga, w):
    nb = x.shape[0]
    return pl.pallas_call(
        _sfin1_body,
        out_shape=[jax.ShapeDtypeStruct((nb, D_MODEL), F32)] * 2,
        compiler_params=pltpu.CompilerParams(vmem_limit_bytes=VMEM_LIMIT),
        name="sample_fin1",
    )(x, y, xs, z, yb, sga, w["d_skip"], w["norm_ssm"], w["w_out"], w["norm_xattn"], w["w_xq"])


def _sattn_body(q_ref, k_ref, v_ref, o_ref):
    for j in range(ATTN_BB):
        q_row = q_ref[j]
        q4 = jnp.concatenate([q_row[:, h * XA_HEAD_DIM:(h + 1) * XA_HEAD_DIM]
                              for h in range(XA_HEADS)], axis=0)
        s = jnp.sum(k_ref[j] * q4[None], axis=-1, keepdims=True) * (XA_HEAD_DIM ** -0.5)
        e = jnp.exp(s - jnp.max(s, axis=0, keepdims=True))
        p = e / jnp.sum(e, axis=0, keepdims=True)
        o4 = jnp.sum(p * v_ref[j], axis=0)
        o_ref[j] = jnp.concatenate([o4[h:h + 1, :] for h in range(XA_HEADS)], axis=1)


def _sample_attn(q3, k3, v3):
    nb = q3.shape[0]
    qs = pl.BlockSpec((ATTN_BB, 1, D_MODEL), lambda i: (i, 0, 0))
    kv = pl.BlockSpec((ATTN_BB, N_MEM, XA_HEADS, XA_HEAD_DIM), lambda i: (i, 0, 0, 0))
    return pl.pallas_call(
        _sattn_body,
        grid=(nb // ATTN_BB,),
        in_specs=[qs, kv, kv],
        out_specs=qs,
        out_shape=jax.ShapeDtypeStruct((nb, 1, D_MODEL), F32),
        compiler_params=_cparams("arbitrary"),
        name="sample_attn",
    )(q3, k3, v3)


def _sfin2_body(h_ref, o_ref, wo_ref, gmoe_ref, wr_ref, br_ref, h2_ref, hn_ref, lg_ref):
    h2 = h_ref[...] + _dot(o_ref[...].astype(BF16), wo_ref[...])
    _router_tail(h2, gmoe_ref, wr_ref, br_ref, h2_ref, hn_ref, lg_ref)


def _sample_fin2(h1, o, w):
    nb = h1.shape[0]
    return pl.pallas_call(
        _sfin2_body,
        out_shape=[jax.ShapeDtypeStruct((nb, D_MODEL), F32)] * 2
        + [jax.ShapeDtypeStruct((N_EXPERTS, nb), F32)],
        compiler_params=pltpu.CompilerParams(vmem_limit_bytes=VMEM_LIMIT),
        name="sample_fin2",
    )(h1, o, w["w_xo"], w["norm_moe"], w["w_router"], w["b_router"])


def _pad_cols(x, cols):
    return jnp.concatenate([x, jnp.zeros((x.shape[0], cols - x.shape[1]), x.dtype)], axis=1)


def _route_body(lgp_ref, lgs_ref, g_ref, loc_ref, cnt_ref, off_ref):
    tt = lgp_ref.shape[1]
    is_sample = pl.program_id(0) == pl.num_programs(0) - 1
    col = lax.broadcasted_iota(I32, (1, tt), 1)
    valid = jnp.logical_or(jnp.logical_not(is_sample), col < lgs_ref.shape[1])
    work = jnp.where(is_sample, _pad_cols(lgs_ref[...], tt), lgp_ref[...])
    sub = lax.broadcasted_iota(I32, (N_EXPERTS, tt), 0).astype(F32)
    vals, hots = [], []
    for _ in range(TOP_K):
        m = jnp.max(work, axis=0, keepdims=True)
        idx = jnp.min(jnp.where(work == m, sub, float(N_EXPERTS)), axis=0, keepdims=True)
        hot = (sub == idx) & valid
        vals.append(m)
        hots.append(hot)
        work = jnp.where(hot, -jnp.inf, work)
    exps = [jnp.exp(v - vals[0]) for v in vals]
    tot = exps[0]
    for e in exps[1:]:
        tot = tot + e
    assigned = hots[0]
    for hot in hots[1:]:
        assigned = assigned | hot
    a = assigned.astype(BF16)
    r_i = lax.broadcasted_iota(I32, (tt, tt), 0)
    c_i = lax.broadcasted_iota(I32, (tt, tt), 1)
    rank = _dot(a, (r_i < c_i).astype(BF16))
    cnt = jnp.sum(a.astype(F32), axis=1, keepdims=True)
    cnt = jnp.floor((cnt + (SUBLANES - 1)) * (1.0 / SUBLANES)) * SUBLANES
    e_r = lax.broadcasted_iota(I32, (N_EXPERTS, N_EXPERTS), 0)
    e_c = lax.broadcasted_iota(I32, (N_EXPERTS, N_EXPERTS), 1)
    cnt_cols = jnp.broadcast_to(cnt, (N_EXPERTS, LANES)).astype(BF16)
    off = _dot((e_r > e_c).astype(BF16), cnt_cols)[:, 0:1]
    slot = rank + off
    k_sub = lax.broadcasted_iota(I32, (SUBLANES, tt), 0)
    g_out = jnp.zeros((SUBLANES, tt), F32)
    l_out = jnp.full((SUBLANES, tt), -1.0, F32)
    for k in range(TOP_K):
        lk = jnp.sum(jnp.where(hots[k], slot, 0.0), axis=0, keepdims=True)
        g_out = jnp.where(k_sub == k, jnp.where(valid, exps[k] / tot, 0.0), g_out)
        l_out = jnp.where(k_sub == k, jnp.where(valid, lk, -1.0), l_out)
    g_ref[...] = g_out
    loc_ref[...] = l_out.astype(I32)
    cnt_ref[...] = cnt.astype(I32)
    off_ref[...] = off.astype(I32)


def _route(logits_p, logits_s, tt):
    ntp = logits_p.shape[1] // tt
    nt = ntp + 1
    t = nt * tt
    tk = pl.BlockSpec((SUBLANES, tt), lambda i: (0, i))
    per_tile = pl.BlockSpec((None, N_EXPERTS, 1), lambda i: (i, 0, 0))
    return pl.pallas_call(
        _route_body,
        grid=(nt,),
        in_specs=[pl.BlockSpec((N_EXPERTS, tt), lambda i: (0, jnp.minimum(i, ntp - 1))),
                  pl.BlockSpec(logits_s.shape, lambda i: (0, 0))],
        out_specs=[tk, tk, per_tile, per_tile],
        out_shape=[jax.ShapeDtypeStruct((SUBLANES, t), F32), jax.ShapeDtypeStruct((SUBLANES, t), I32),
                   jax.ShapeDtypeStruct((nt, N_EXPERTS, 1), I32), jax.ShapeDtypeStruct((nt, N_EXPERTS, 1), I32)],
        compiler_params=_cparams("arbitrary"),
        name="moe_route",
    )(logits_p, logits_s)


def _sorted_rows(tt):
    return tt * TOP_K + N_EXPERTS * SUBLANES


def _run_copies(tt, tile, cnt_ref, off_ref, base_ref, make_copy, wait):
    if wait:
        total = off_ref[tile, N_EXPERTS - 1] + cnt_ref[tile, N_EXPERTS - 1]

        @pl.when(total > 0)
        def _():
            make_copy(0, 0, pl.multiple_of(total, SUBLANES)).wait()
        return

    def per_expert(e, carry):
        n = cnt_ref[tile, e]

        @pl.when(n > 0)
        def _():
            make_copy(pl.multiple_of(off_ref[tile, e], SUBLANES),
                      pl.multiple_of(base_ref[tile, e], SUBLANES), pl.multiple_of(n, SUBLANES)).start()
        return carry

    lax.fori_loop(0, N_EXPERTS, per_expert, 0)


def _dispatch_body(tm, tt, cnt_ref, off_ref, base_ref, zstart_ref, loc_ref, x_ref, xs_ref, o_hbm,
                   zero_ref, srt_ref, zsem, sems):
    i = pl.program_id(0)
    last = pl.num_programs(0) - 1
    r = _sorted_rows(tt)

    def zero_copy(j):
        return pltpu.make_async_copy(zero_ref, o_hbm.at[pl.ds(pl.multiple_of(zstart_ref[j], tm), tm), :], zsem)

    @pl.when(i == 0)
    def _():
        zero_ref[...] = jnp.zeros_like(zero_ref)

        def start(j, carry):
            @pl.when(zstart_ref[j] >= 0)
            def _():
                zero_copy(j).start()
            return carry

        def wait(j, carry):
            @pl.when(zstart_ref[j] >= 0)
            def _():
                zero_copy(j).wait()
            return carry

        lax.fori_loop(0, zstart_ref.shape[0], start, 0)
        lax.fori_loop(0, zstart_ref.shape[0], wait, 0)

    loc = loc_ref[...]
    slot_i = lax.broadcasted_iota(I32, (r, tt), 0)
    hit = slot_i == loc[0:1, :]
    for k in range(1, TOP_K):
        hit = hit | (slot_i == loc[k:k + 1, :])
    buf = i % 2
    x = jnp.where(i == last, _pad_rows(xs_ref[...], tt), x_ref[...])
    srt_ref[buf] = _dot(hit.astype(BF16), x.astype(BF16))

    def copies(tile, wait):
        b = tile % 2

        def make_copy(lo, go, size):
            return pltpu.make_async_copy(srt_ref.at[b, pl.ds(lo, size), :], o_hbm.at[pl.ds(go, size), :],
                                         sems.at[b])

        _run_copies(tt, tile, cnt_ref, off_ref, base_ref, make_copy, wait)

    copies(i, False)

    @pl.when(i > 0)
    def _():
        copies(i - 1, True)

    @pl.when(i == last)
    def _():
        copies(i, True)


def _dispatch(cnt, off, base, zero_starts, loc, hn_p, hn_s, n_rows, tm, tt):
    ntp = hn_p.shape[0] // tt
    smem = pl.BlockSpec(memory_space=pltpu.SMEM)
    return pl.pallas_call(
        functools.partial(_dispatch_body, tm, tt),
        grid_spec=pltpu.PrefetchScalarGridSpec(
            num_scalar_prefetch=0, grid=(ntp + 1,),
            in_specs=[smem, smem, smem, smem,
                      pl.BlockSpec((SUBLANES, tt), lambda i: (0, i)),
                      pl.BlockSpec((tt, D_MODEL), lambda i: (jnp.minimum(i, ntp - 1), 0)),
                      pl.BlockSpec(hn_s.shape, lambda i: (0, 0))],
            out_specs=pl.BlockSpec(memory_space=pl.ANY),
            scratch_shapes=[pltpu.VMEM((tm, D_MODEL), F32), pltpu.VMEM((2, _sorted_rows(tt), D_MODEL), F32),
                            pltpu.SemaphoreType.DMA, pltpu.SemaphoreType.DMA((2,))]),
        out_shape=jax.ShapeDtypeStruct((n_rows, D_MODEL), F32),
        compiler_params=_cparams("arbitrary"),
        name="moe_dispatch",
    )(cnt, off, base, zero_starts, loc, hn_p, hn_s)


def _expert_body(be_ref, nu_ref, bv_ref, slot_ref, nxt_ref, x_ref, wgu_hbm, bgu_ref, wdn_hbm, bdn_ref,
                 y_ref, wgu_f, wdn_f, wgu_b, wdn_b, sems):
    i = pl.program_id(0)
    tm = x_ref.shape[0]
    valid = bv_ref[i]
    expert = be_ref[i]
    slot = slot_ref[i]

    def weight_copies(e, s):
        return (pltpu.make_async_copy(wgu_hbm.at[e], wgu_f.at[s], sems.at[0, s]),
                pltpu.make_async_copy(wdn_hbm.at[e], wdn_f.at[s], sems.at[1, s]))

    @pl.when(i == 0)
    def _():
        for cp in weight_copies(expert, slot):
            cp.start()

    first = jnp.logical_and(i < nu_ref[0], jnp.logical_or(i == 0, expert != be_ref[jnp.maximum(i - 1, 0)]))

    @pl.when(first)
    def _():
        for cp in weight_copies(expert, slot):
            cp.wait()

        @pl.when(nxt_ref[i] >= 0)
        def _():
            for cp in weight_copies(nxt_ref[i], 1 - slot):
                cp.start()

    def ffn(rows, cast):
        xb = x_ref[0:rows, :].astype(BF16)
        gu_parts = []
        for j in range(2 * D_FF // EXPERT_COL_CHUNK):
            cs = slice(j * EXPERT_COL_CHUNK, (j + 1) * EXPERT_COL_CHUNK)
            if cast:
                wgu_b[:, cs] = wgu_f[slot, :, cs].astype(BF16)
            gu_parts.append(_dot(xb, wgu_b[:, cs]) + bgu_ref[:, cs])
        gate = jnp.minimum(jnp.concatenate(gu_parts[:len(gu_parts) // 2], axis=1), SWIGLU_LIMIT)
        up = jnp.clip(jnp.concatenate(gu_parts[len(gu_parts) // 2:], axis=1), -SWIGLU_LIMIT, SWIGLU_LIMIT)
        act = ((up + 1.0) * (gate * _sigmoid(SWIGLU_ALPHA * gate))).astype(BF16)
        for j in range(D_MODEL // EXPERT_COL_CHUNK):
            cs = slice(j * EXPERT_COL_CHUNK, (j + 1) * EXPERT_COL_CHUNK)
            if cast:
                wdn_b[:, cs] = wdn_f[slot, :, cs].astype(BF16)
            y_ref[0:rows, cs] = _dot(act, wdn_b[:, cs]) + bdn_ref[:, cs]
        if rows < tm:
            y_ref[rows:tm, :] = jnp.zeros((tm - rows, D_MODEL), F32)

    quarter = tm // EXPERT_ROW_SPLITS
    for q in range(1, EXPERT_ROW_SPLITS + 1):
        in_q = jnp.logical_and(valid > (q - 1) * quarter, valid <= q * quarter)
        for cast in (True, False):
            @pl.when(jnp.logical_and(in_q, first == cast))
            def _(q=q, cast=cast):
                ffn(q * quarter, cast)

    @pl.when(valid == 0)
    def _():
        y_ref[...] = jnp.zeros_like(y_ref)


def _experts(block_e, n_used, block_valid, block_slot, block_next, xs, wgu, bgu, wdn, bdn, tm):
    n_rows = xs.shape[0]
    return pl.pallas_call(
        _expert_body,
        grid_spec=pltpu.PrefetchScalarGridSpec(
            num_scalar_prefetch=5, grid=(n_rows // tm,),
            in_specs=[pl.BlockSpec((tm, D_MODEL), lambda i, be, nu, *_: (jnp.minimum(i, nu[0] - 1), 0)),
                      pl.BlockSpec(memory_space=pl.ANY),
                      pl.BlockSpec((None, 1, 2 * D_FF), lambda i, be, *_: (be[i], 0, 0)),
                      pl.BlockSpec(memory_space=pl.ANY),
                      pl.BlockSpec((None, 1, D_MODEL), lambda i, be, *_: (be[i], 0, 0))],
            out_specs=pl.BlockSpec((tm, D_MODEL), lambda i, *_: (i, 0)),
            scratch_shapes=[pltpu.VMEM((2, D_MODEL, 2 * D_FF), F32), pltpu.VMEM((2, D_FF, D_MODEL), F32),
                            pltpu.VMEM((D_MODEL, 2 * D_FF), BF16), pltpu.VMEM((D_FF, D_MODEL), BF16),
                            pltpu.SemaphoreType.DMA((2, 2))]),
        out_shape=jax.ShapeDtypeStruct((n_rows, D_MODEL), F32),
        compiler_params=_cparams("arbitrary"),
        name="moe_experts",
    )(block_e, n_used, block_valid, block_slot, block_next, xs, wgu, bgu, wdn, bdn)


def _combine_body(tt, cnt_ref, off_ref, base_ref, loc_ref, g_ref, h_ref, hs_ref, gfin_ref, ys_hbm,
                  y_ref, ysmp_ref, buf_ref, sems):
    i = pl.program_id(0)
    last = pl.num_programs(0) - 1
    r = _sorted_rows(tt)

    def copies(tile, wait):
        b = tile % 2

        def make_copy(lo, go, size):
            return pltpu.make_async_copy(ys_hbm.at[pl.ds(go, size), :], buf_ref.at[b, pl.ds(lo, size), :],
                                         sems.at[b])

        _run_copies(tt, tile, cnt_ref, off_ref, base_ref, make_copy, wait)

    @pl.when(i == 0)
    def _():
        buf_ref[...] = jnp.zeros_like(buf_ref)
        copies(0, False)

    @pl.when(i < last)
    def _():
        copies(i + 1, False)

    copies(i, True)
    loc = loc_ref[...]
    gates = g_ref[...]
    slot_i = lax.broadcasted_iota(I32, (r, tt), 0)
    gmat = jnp.where(slot_i == loc[0:1, :], gates[0:1, :], 0.0)
    for k in range(1, TOP_K):
        gmat = gmat + jnp.where(slot_i == loc[k:k + 1, :], gates[k:k + 1, :], 0.0)
    h = jnp.where(i == last, _pad_rows(hs_ref[...], tt), h_ref[...])
    moe = lax.dot_general(gmat.astype(BF16), buf_ref[i % 2].astype(BF16), TN_DIMS, preferred_element_type=F32)
    y = _rms(h + moe, gfin_ref[...])

    @pl.when(i < last)
    def _():
        y_ref[...] = y

    @pl.when(i == last)
    def _():
        ysmp_ref[...] = y[0:ysmp_ref.shape[0], :]


def _combine(cnt, off, base, loc, gates, h2_p, h2_s, norm_final, ys, tt):
    ntp = h2_p.shape[0] // tt
    tok = pl.BlockSpec((tt, D_MODEL), lambda i: (jnp.minimum(i, ntp - 1), 0))
    smp = pl.BlockSpec(h2_s.shape, lambda i: (0, 0))
    tk = pl.BlockSpec((SUBLANES, tt), lambda i: (0, i))
    smem = pl.BlockSpec(memory_space=pltpu.SMEM)
    return pl.pallas_call(
        functools.partial(_combine_body, tt),
        grid_spec=pltpu.PrefetchScalarGridSpec(
            num_scalar_prefetch=0, grid=(ntp + 1,),
            in_specs=[smem, smem, smem, tk, tk, tok, smp,
                      pl.BlockSpec((1, D_MODEL), lambda i: (0, 0)),
                      pl.BlockSpec(memory_space=pl.ANY)],
            out_specs=[tok, smp],
            scratch_shapes=[pltpu.VMEM((2, _sorted_rows(tt), D_MODEL), F32), pltpu.SemaphoreType.DMA((2,))]),
        out_shape=[jax.ShapeDtypeStruct(h2_p.shape, F32), jax.ShapeDtypeStruct(h2_s.shape, F32)],
        compiler_params=_cparams("arbitrary"),
        name="moe_combine",
    )(cnt, off, base, loc, gates, h2_p, h2_s, norm_final, ys)


def _moe_and_final_norm(hn_p, logits_p, h2_p, hn_s, logits_s, h2_s, w, tt, tm):
    nt = hn_p.shape[0] // tt + 1
    t = hn_p.shape[0] + hn_s.shape[0]
    gates, loc, cnt3, off3 = _route(logits_p, logits_s, tt)
    cnt = cnt3[:, :, 0]
    counts = jnp.sum(cnt, axis=0)
    padded = (counts + tm - 1) // tm * tm
    pad_end = jnp.cumsum(padded)
    start = pad_end - padded
    off = off3[:, :, 0]
    base = (start[None, :] + jnp.cumsum(cnt, axis=0) - cnt).astype(I32)
    n_blocks = (t * TOP_K + nt * N_EXPERTS * (SUBLANES - 1) + N_EXPERTS * (tm - 1)) // tm
    n_rows = n_blocks * tm
    block_start = jnp.arange(n_blocks, dtype=I32) * tm
    block_e = jnp.minimum(jnp.sum(block_start[:, None] >= pad_end[None, :], axis=-1), N_EXPERTS - 1).astype(I32)
    n_used = (pad_end[-1:] // tm).astype(I32)
    zero_starts = jnp.concatenate([jnp.where(padded > 0, pad_end - tm, -1),
                                   jnp.where(block_start >= pad_end[-1], block_start, -1)]).astype(I32)
    xs = _dispatch(cnt, off, base, zero_starts, loc, hn_p, hn_s, n_rows, tm, tt)
    e_ids = jnp.arange(N_EXPERTS, dtype=I32)
    block_hot = block_e[:, None] == e_ids[None, :]

    def per_block(table):
        return jnp.sum(jnp.where(block_hot, table[None, :], 0), axis=1).astype(I32)

    block_valid = jnp.clip(per_block(start + counts) - block_start, 0, tm)
    block_valid = jnp.where(block_start < pad_end[-1], block_valid, 0).astype(I32)
    present = padded > 0
    later = present[None, :] & (e_ids[None, :] > e_ids[:, None])
    next_e = jnp.min(jnp.where(later, e_ids[None, :], N_EXPERTS), axis=1)
    next_e = jnp.where(next_e < N_EXPERTS, next_e, -1).astype(I32)
    before = present[None, :] & (e_ids[None, :] < e_ids[:, None])
    run_slot = (jnp.sum(before.astype(I32), axis=1) % 2).astype(I32)
    ys = _experts(block_e, n_used, block_valid, per_block(run_slot), per_block(next_e), xs, w["w_gate_up"], w["b_gate_up"], w["w_down"], w["b_down"], tm)
    return _combine(cnt, off, base, loc, gates, h2_p, h2_s, w["norm_final"], ys, tt)


def kernel(x_prompt, x_sample, mem_prompt, state_ssm, state_mamba_conv, state_short_conv, cache_mem_k, cache_mem_v, norm_mix, w_in, w_mconv, b_mconv, dt_bias, a_log, d_skip, norm_ssm, w_sconv, w_out, norm_xattn, norm_mem, w_xq, w_xk, w_xv, w_xo, norm_moe, w_router, b_router, w_gate_up, b_gate_up, w_down, b_down, norm_final):
    nbp, seq, _ = x_prompt.shape
    nbs = x_sample.shape[0]
    dt_lo = SSM_INNER + SSM_CONV_DIM
    w_in0 = w_in[0]
    w_dt = w_in0[:, dt_lo:dt_lo + SSM_HEADS]
    w = {
        "norm_mix": norm_mix, "norm_ssm": norm_ssm, "norm_xattn": norm_xattn, "norm_moe": norm_moe,
        "norm_final": norm_final.reshape(1, D_MODEL),
        "w_a": w_in0[:, :dt_lo].astype(BF16),
        "w_dt": w_dt.astype(BF16), "w_dt_t": w_dt.T.astype(BF16),
        "w_b": w_in0[:, dt_lo + SSM_HEADS:].astype(BF16),
        "w_mconv": w_mconv[0], "b_mconv": b_mconv,
        "dt_bias": dt_bias, "dt_bias_t": dt_bias.reshape(SSM_HEADS, 1),
        "a_log": a_log, "a_log_t": a_log.reshape(SSM_HEADS, 1),
        "d_skip": jnp.repeat(d_skip, SSM_HEAD_DIM, axis=1),
        "w_sconv": w_sconv[0], "w_out": w_out[0].astype(BF16),
        "w_xq": w_xq[0].astype(BF16), "w_xo": w_xo[0].astype(BF16),
        "w_router": w_router[0].T.astype(BF16), "b_router": b_router.reshape(N_EXPERTS, 1),
        "w_gate_up": w_gate_up[0], "b_gate_up": b_gate_up[0].reshape(N_EXPERTS, 1, 2 * D_FF),
        "w_down": w_down[0], "b_down": b_down[0].reshape(N_EXPERTS, 1, D_MODEL),
    }

    k_p, v_p, kb, vb = _mem_kv(mem_prompt.reshape(nbp * N_MEM, D_MODEL), norm_mem,
                               w_xk[0].astype(BF16), w_xv[0].astype(BF16))
    h1, ssm_p, mconv_p, sconv_p = _prompt_mixer(x_prompt.reshape(nbp * seq, D_MODEL), nbp, w)
    h2, hn, logits = _prompt_attn(h1, kb, vb, nbp, w)

    xs2 = x_sample.reshape(nbs, D_MODEL)
    mstate_t = jnp.transpose(state_mamba_conv[0], (1, 0, 2))
    sstate_t = jnp.transpose(state_short_conv[0], (1, 0, 2))
    z, xs_, dtx, dec, bm, cm, yb, sga, mnew_t, snew_t = _sample_proj(xs2, mstate_t, sstate_t, w)
    ssm_s, y_s = _sample_state(dec, state_ssm[0].reshape(nbs, SSM_INNER, SSM_STATE), dtx, bm, cm)
    h1s, q_s = _sample_fin1(xs2, y_s, xs_, z, yb, sga, w)
    o_s = _sample_attn(q_s.reshape(nbs, 1, D_MODEL),
                       cache_mem_k[0], cache_mem_v[0])
    h2s, hns, logits_s = _sample_fin2(h1s, o_s.reshape(nbs, D_MODEL), w)
    y_prompt, y_sample = _moe_and_final_norm(hn, logits, h2, hns, logits_s, h2s, w, MIX_TILE, MOE_ROW_TILE)

    return (y_prompt.reshape(nbp, seq, D_MODEL),
            y_sample.reshape(nbs, 1, D_MODEL),
            ssm_p.reshape(1, nbp, SSM_HEADS, SSM_HEAD_DIM, SSM_STATE),
            mconv_p[None], sconv_p[None],
            k_p.reshape(1, nbp, N_MEM, XA_HEADS, XA_HEAD_DIM),
            v_p.reshape(1, nbp, N_MEM, XA_HEADS, XA_HEAD_DIM),
            ssm_s.reshape(1, nbs, SSM_HEADS, SSM_HEAD_DIM, SSM_STATE),
            jnp.transpose(mnew_t, (1, 0, 2))[None],
            jnp.transpose(snew_t, (1, 0, 2))[None])
```

```python
import functools

import jax
import jax.numpy as jnp
from jax import lax
from jax.experimental import pallas as pl
from jax.experimental.pallas import tpu as pltpu

F32 = jnp.float32
BF16 = jnp.bfloat16
I32 = jnp.int32
U32 = jnp.uint32

D_MODEL = 1024
N_MEM = 256
SSM_HEADS = 16
SSM_HEAD_DIM = 64
SSM_INNER = SSM_HEADS * SSM_HEAD_DIM
SSM_STATE = 128
SSM_GROUPS = 4
HEADS_PER_GROUP = SSM_HEADS // SSM_GROUPS
GROUP_WIDTH = SSM_INNER // SSM_GROUPS
SSM_CONV = 4
SSM_CONV_DIM = SSM_INNER + 2 * SSM_GROUPS * SSM_STATE
SC_CONV = 3
XA_HEADS = 4
XA_HEAD_DIM = D_MODEL // XA_HEADS
N_EXPERTS = 32
TOP_K = 4
D_FF = D_MODEL
SWIGLU_LIMIT = 7.0
SWIGLU_ALPHA = 1.702
EPS = 1e-6

LANES = 128
SUBLANES = 8
VMEM_LIMIT = 56 * 1024 * 1024

MIX_TILE = 256
MOE_ROW_TILE = 512
EXPERT_ROW_SPLITS = 4
EXPERT_COL_CHUNK = 512
STATE_BB = 8
ATTN_BB = 4

NT_DIMS = (((1,), (1,)), ((), ()))
TN_DIMS = (((0,), (0,)), ((), ()))


def _cparams(*sem):
    return pltpu.CompilerParams(dimension_semantics=sem, vmem_limit_bytes=VMEM_LIMIT)


def _const_spec(shape):
    nd = len(shape)
    return pl.BlockSpec(shape, lambda *_: (0,) * nd, pipeline_mode=pl.Buffered(1))


def _sigmoid(x):
    return 1.0 / (1.0 + jnp.exp(-x))


def _silu(x):
    return x * _sigmoid(x)


def _softplus(x):
    return jnp.maximum(x, 0.0) + jnp.log(1.0 + jnp.exp(-jnp.abs(x)))


def _rms(x, g):
    ms = jnp.mean(x * x, axis=-1, keepdims=True)
    return x * lax.rsqrt(ms + EPS) * g


def _dot(a, b):
    return jnp.dot(a, b, preferred_element_type=F32)


def _dot_nt(a, b):
    return lax.dot_general(a, b, NT_DIMS, preferred_element_type=F32)


def _expand_heads(v):
    rows = v.shape[0]
    lane = lax.broadcasted_iota(I32, (rows, LANES), 1)
    pieces = []
    for j in range(SSM_HEADS // 2):
        a = jnp.broadcast_to(v[:, 2 * j:2 * j + 1], (rows, LANES))
        b = jnp.broadcast_to(v[:, 2 * j + 1:2 * j + 2], (rows, LANES))
        pieces.append(jnp.where(lane < SSM_HEAD_DIM, a, b))
    return jnp.concatenate(pieces, axis=1)


def _cumsum(x, axis):
    idx = lax.broadcasted_iota(I32, x.shape, axis)
    shift = 1
    while shift < x.shape[axis]:
        x = x + jnp.where(idx >= shift, pltpu.roll(x, shift, axis), 0.0)
        shift *= 2
    return x


def _pack_bf16_pairs(x, is_bf16_valued=False):
    w = x.shape[1] // 2
    if not is_bf16_valued:
        x = x.astype(BF16).astype(F32)
    bits = lax.bitcast_convert_type(x, U32)
    return (bits[:, w:] & jnp.uint32(0xFFFF0000)) | (bits[:, :w] >> 16)


def _unpack_bf16_pairs(p):
    lo = lax.bitcast_convert_type(p << 16, F32)
    hi = lax.bitcast_convert_type(p & jnp.uint32(0xFFFF0000), F32)
    return jnp.concatenate([lo, hi], axis=1).astype(BF16)


def _pad_rows(x, rows):
    return jnp.concatenate([x, jnp.zeros((rows - x.shape[0], x.shape[1]), x.dtype)], axis=0)


def _group_rmsnorm(u, g):
    outs = []
    for k in range(SSM_GROUPS):
        ug = u[:, k * GROUP_WIDTH:(k + 1) * GROUP_WIDTH]
        ms = jnp.mean(ug * ug, axis=-1, keepdims=True)
        outs.append(ug * lax.rsqrt(ms + EPS))
    return jnp.concatenate(outs, axis=1) * g


def _memkv_body(mem_ref, g_ref, wk_ref, wv_ref, k_ref, v_ref, kb_ref, vb_ref):
    mn = _rms(mem_ref[...], g_ref[...]).astype(BF16)
    k = _dot(mn, wk_ref[...])
    v = _dot(mn, wv_ref[...])
    k_ref[...] = k
    v_ref[...] = v
    kb_ref[...] = k.astype(BF16)
    vb_ref[...] = v.astype(BF16)


def _mem_kv(mem2d, norm_mem, wk, wv):
    rows = mem2d.shape[0]
    nb = rows // N_MEM
    blk = pl.BlockSpec((N_MEM, D_MODEL), lambda b: (b, 0))
    return pl.pallas_call(
        _memkv_body,
        grid=(nb,),
        in_specs=[blk, _const_spec((1, D_MODEL)), _const_spec((D_MODEL, D_MODEL)),
                  _const_spec((D_MODEL, D_MODEL))],
        out_specs=[blk, blk, blk, blk],
        out_shape=[jax.ShapeDtypeStruct((rows, D_MODEL), F32)] * 2
        + [jax.ShapeDtypeStruct((rows, D_MODEL), BF16)] * 2,
        compiler_params=_cparams("arbitrary"),
        name="mem_kv",
    )(mem2d, norm_mem, wk, wv)


def _mix_body(x_ref, gmix_ref, wa_ref, wdtc_ref, wdtr_ref, wb_ref, wmc_ref, bmc_ref,
              dtb_ref, dtbt_ref, alog_ref, alogt_ref, dskip_ref, gssm_ref, wsc_ref, wout_ref,
              h_ref, ssm_ref, mbuf_ref, sbuf_ref,
              st_ref, cbuf_ref, scbuf_ref):
    tq = MIX_TILE
    c = pl.program_id(1)

    @pl.when(c == 0)
    def _():
        st_ref[...] = jnp.zeros_like(st_ref)
        cbuf_ref[0:SUBLANES, :] = jnp.zeros((SUBLANES, SSM_CONV_DIM), F32)
        scbuf_ref[0:SUBLANES, :] = jnp.zeros((SUBLANES, D_MODEL), F32)

    x = x_ref[...]
    xn = _rms(x, gmix_ref[...]).astype(BF16)

    u = _dot(xn, wa_ref[:, SSM_INNER:])
    cbuf_ref[SUBLANES:SUBLANES + tq, :] = u
    wm = wmc_ref[...]
    conv = u * wm[SSM_CONV - 1:SSM_CONV, :] + bmc_ref[...]
    for k in range(SSM_CONV - 1):
        off = SUBLANES - (SSM_CONV - 1) + k
        conv = conv + cbuf_ref[off:off + tq, :] * wm[k:k + 1, :]
    tail = cbuf_ref[tq + SUBLANES - (SSM_CONV - 1):tq + SUBLANES, :]
    mbuf_ref[...] = tail
    cbuf_ref[SUBLANES - (SSM_CONV - 1):SUBLANES, :] = tail
    xbc = _silu(conv)
    xs = xbc[:, :SSM_INNER]
    bm = xbc[:, SSM_INNER:SSM_INNER + SSM_GROUPS * SSM_STATE]
    cm = xbc[:, SSM_INNER + SSM_GROUPS * SSM_STATE:]

    dt = _softplus(_dot(xn, wdtc_ref[...]) + dtb_ref[...])
    dtt = _softplus(_dot_nt(wdtr_ref[...], xn) + dtbt_ref[...])
    a_row = -jnp.exp(alog_ref[...])
    a_col = -jnp.exp(alogt_ref[...])
    row_i = lax.broadcasted_iota(I32, (tq, tq), 0)
    col_i = lax.broadcasted_iota(I32, (tq, tq), 1)
    causal = row_i >= col_i
    a_cum = _cumsum(dt * a_row, 0)
    a_cumt = _cumsum(dtt * a_col, 1)
    a_last = a_cum[tq - 1:tq, :]

    xdt = xs * _expand_heads(dt)
    in_decay = _expand_heads(jnp.exp(a_cum))
    to_end = _expand_heads(jnp.exp(a_last - a_cum))
    chunk_decay = _expand_heads(jnp.exp(a_last))
    xdt_b = xdt.astype(BF16)
    xend_b = (xdt * to_end).astype(BF16)
    lane = lax.broadcasted_iota(I32, (tq, LANES), 1)

    def proj_b(k):
        return _dot(xn, wb_ref[:, k * D_MODEL:(k + 1) * D_MODEL])

    pb = []
    y_groups = []
    for g in range(SSM_GROUPS):
        pb.append(proj_b(g))
        if g == 0:
            z = _dot(xn, wa_ref[:, :SSM_INNER])
        if g == 2:
            g_b = proj_b(SSM_GROUPS)
        cg = cm[:, g * SSM_STATE:(g + 1) * SSM_STATE].astype(BF16)
        bg_f = bm[:, g * SSM_STATE:(g + 1) * SSM_STATE]
        bg = bg_f.astype(BF16)
        scores = _dot_nt(cg, bg)
        gs = slice(g * GROUP_WIDTH, (g + 1) * GROUP_WIDTH)
        st_g = st_ref[:, gs]
        y_off = _dot(cg, st_g.astype(BF16)) * in_decay[:, gs]
        pair_out = []
        for pr in range(HEADS_PER_GROUP // 2):
            h0 = g * HEADS_PER_GROUP + 2 * pr
            xp = xdt_b[:, h0 * SSM_HEAD_DIM:(h0 + 2) * SSM_HEAD_DIM]
            ys = []
            for h in (h0, h0 + 1):
                seg = a_cum[:, h:h + 1] - a_cumt[h:h + 1, :]
                decay = jnp.where(causal, jnp.exp(jnp.minimum(seg, 0.0)), 0.0)
                ys.append(_dot((scores * decay).astype(BF16), xp))
            pair_out.append(jnp.where(lane < SSM_HEAD_DIM, ys[0], ys[1]))
        y_groups.append(jnp.concatenate(pair_out, axis=1) + y_off)
        st_ref[:, gs] = st_g * chunk_decay[:, gs] + _dot(bg_f.T.astype(BF16), xend_b[:, gs])
    y = jnp.concatenate(y_groups, axis=1) + dskip_ref[...] * xs
    y_a = _group_rmsnorm(y * _silu(z), gssm_ref[...])

    sc_b, sc_c, sc_v, g_a = pb
    cv = sc_c * sc_v
    scbuf_ref[SUBLANES:SUBLANES + tq, :] = cv
    ws = wsc_ref[...]
    uc = cv * ws[SC_CONV - 1:SC_CONV, :]
    for k in range(SC_CONV - 1):
        off = SUBLANES - (SC_CONV - 1) + k
        uc = uc + scbuf_ref[off:off + tq, :] * ws[k:k + 1, :]
    stail = scbuf_ref[tq + SUBLANES - (SC_CONV - 1):tq + SUBLANES, :]
    sbuf_ref[...] = stail
    scbuf_ref[SUBLANES - (SC_CONV - 1):SUBLANES, :] = stail
    merged = _sigmoid(g_a) * y_a + _sigmoid(g_b) * (sc_b * uc)
    h_ref[...] = x + _dot(merged.astype(BF16), wout_ref[...])

    @pl.when(c == pl.num_programs(1) - 1)
    def _():
        ssm_ref[...] = st_ref[...].T


def _prompt_mixer(x2d, nb, w):
    t = x2d.shape[0]
    nc = t // nb // MIX_TILE
    tok = pl.BlockSpec((MIX_TILE, D_MODEL), lambda b, c: (b * nc + c, 0))
    return pl.pallas_call(
        _mix_body,
        grid=(nb, nc),
        in_specs=[tok, _const_spec((1, D_MODEL)),
                  _const_spec((D_MODEL, SSM_INNER + SSM_CONV_DIM)),
                  _const_spec((D_MODEL, SSM_HEADS)), _const_spec((SSM_HEADS, D_MODEL)),
                  _const_spec((D_MODEL, 5 * D_MODEL)),
                  _const_spec((SSM_CONV, SSM_CONV_DIM)), _const_spec((1, SSM_CONV_DIM)),
                  _const_spec((1, SSM_HEADS)), _const_spec((SSM_HEADS, 1)),
                  _const_spec((1, SSM_HEADS)), _const_spec((SSM_HEADS, 1)),
                  _const_spec((1, SSM_INNER)), _const_spec((1, SSM_INNER)),
                  _const_spec((SC_CONV, D_MODEL)), _const_spec((D_MODEL, D_MODEL))],
        out_specs=[tok,
                   pl.BlockSpec((None, SSM_INNER, SSM_STATE), lambda b, c: (b, 0, 0)),
                   pl.BlockSpec((None, SSM_CONV - 1, SSM_CONV_DIM), lambda b, c: (b, 0, 0)),
                   pl.BlockSpec((None, SC_CONV - 1, D_MODEL), lambda b, c: (b, 0, 0))],
        out_shape=[jax.ShapeDtypeStruct((t, D_MODEL), F32),
                   jax.ShapeDtypeStruct((nb, SSM_INNER, SSM_STATE), F32),
                   jax.ShapeDtypeStruct((nb, SSM_CONV - 1, SSM_CONV_DIM), F32),
                   jax.ShapeDtypeStruct((nb, SC_CONV - 1, D_MODEL), F32)],
        scratch_shapes=[pltpu.VMEM((SSM_STATE, SSM_INNER), F32),
                        pltpu.VMEM((MIX_TILE + SUBLANES, SSM_CONV_DIM), F32),
                        pltpu.VMEM((MIX_TILE + SUBLANES, D_MODEL), F32)],
        compiler_params=_cparams("arbitrary", "arbitrary"),
        name="prompt_mixer",
    )(x2d, w["norm_mix"], w["w_a"], w["w_dt"], w["w_dt_t"], w["w_b"], w["w_mconv"], w["b_mconv"],
      w["dt_bias"], w["dt_bias_t"], w["a_log"], w["a_log_t"], w["d_skip"], w["norm_ssm"],
      w["w_sconv"], w["w_out"])


def _router_tail(h2, gmoe_ref, wr_ref, br_ref, h2_ref, hn_ref, lg_ref):
    h2_ref[...] = h2
    hn = _rms(h2, gmoe_ref[...])
    hn_ref[...] = hn
    lg_ref[...] = _dot_nt(wr_ref[...], hn.astype(BF16)) + br_ref[...]


def _attn_body(h_ref, gx_ref, wq_ref, k_ref, v_ref, wo_ref, gmoe_ref, wr_ref, br_ref,
               h2_ref, hn_ref, lg_ref):
    h = h_ref[...]
    hn = _rms(h, gx_ref[...]).astype(BF16)
    q = _dot(hn, wq_ref[...]).astype(BF16)
    outs = []
    for hd in range(XA_HEADS):
        sl = slice(hd * XA_HEAD_DIM, (hd + 1) * XA_HEAD_DIM)
        s = _dot_nt(q[:, sl], k_ref[:, sl]) * (XA_HEAD_DIM ** -0.5)
        e = jnp.exp(s - jnp.max(s, axis=-1, keepdims=True))
        p = e / jnp.sum(e, axis=-1, keepdims=True)
        outs.append(_dot(p.astype(BF16), v_ref[:, sl]))
    o = jnp.concatenate(outs, axis=1).astype(BF16)
    h2 = h + _dot(o, wo_ref[...])
    _router_tail(h2, gmoe_ref, wr_ref, br_ref, h2_ref, hn_ref, lg_ref)


def _prompt_attn(h1, kb, vb, nb, w):
    t = h1.shape[0]
    nc = t // nb // MIX_TILE
    tok = pl.BlockSpec((MIX_TILE, D_MODEL), lambda b, c: (b * nc + c, 0))
    kv = pl.BlockSpec((N_MEM, D_MODEL), lambda b, c: (b, 0))
    return pl.pallas_call(
        _attn_body,
        grid=(nb, nc),
        in_specs=[tok, _const_spec((1, D_MODEL)), _const_spec((D_MODEL, D_MODEL)), kv, kv,
                  _const_spec((D_MODEL, D_MODEL)), _const_spec((1, D_MODEL)),
                  _const_spec((N_EXPERTS, D_MODEL)), _const_spec((N_EXPERTS, 1))],
        out_specs=[tok, tok, pl.BlockSpec((N_EXPERTS, MIX_TILE), lambda b, c: (0, b * nc + c))],
        out_shape=[jax.ShapeDtypeStruct((t, D_MODEL), F32),
                   jax.ShapeDtypeStruct((t, D_MODEL), F32),
                   jax.ShapeDtypeStruct((N_EXPERTS, t), F32)],
        compiler_params=_cparams("arbitrary", "arbitrary"),
        name="prompt_attn",
    )(h1, w["norm_xattn"], w["w_xq"], kb, vb, w["w_xo"], w["norm_moe"], w["w_router"], w["b_router"])


def _sproj_body(x_ref, gmix_ref, wa_ref, wdtc_ref, wb_ref, wmc_ref, bmc_ref, dtb_ref, alog_ref,
                wsc_ref, mst_ref, sst_ref,
                z_ref, xs_ref, dtx_ref, dec_ref, bm_ref, cm_ref, yb_ref, sga_ref, mnew_ref, snew_ref):
    x = x_ref[...]
    xn = _rms(x, gmix_ref[...]).astype(BF16)
    pa = _dot(xn, wa_ref[...])
    z_ref[...] = pa[:, :SSM_INNER]
    u = pa[:, SSM_INNER:]
    wm = wmc_ref[...]
    conv = u * wm[SSM_CONV - 1:SSM_CONV, :] + bmc_ref[...]
    for k in range(SSM_CONV - 1):
        conv = conv + mst_ref[k] * wm[k:k + 1, :]
    for k in range(SSM_CONV - 2):
        mnew_ref[k] = mst_ref[k + 1]
    mnew_ref[SSM_CONV - 2] = u
    xbc = _silu(conv)
    xs = xbc[:, :SSM_INNER]
    xs_ref[...] = xs
    bm_ref[...] = xbc[:, SSM_INNER:SSM_INNER + SSM_GROUPS * SSM_STATE]
    cm_ref[...] = xbc[:, SSM_INNER + SSM_GROUPS * SSM_STATE:]
    dt = _softplus(_dot(xn, wdtc_ref[...]) + dtb_ref[...])
    dec_ref[...] = jnp.exp(dt * (-jnp.exp(alog_ref[...])))
    dtx_ref[...] = xs * _expand_heads(dt)
    pb = _dot(xn, wb_ref[...])
    cv = pb[:, D_MODEL:2 * D_MODEL] * pb[:, 2 * D_MODEL:3 * D_MODEL]
    ws = wsc_ref[...]
    uc = cv * ws[SC_CONV - 1:SC_CONV, :]
    for k in range(SC_CONV - 1):
        uc = uc + sst_ref[k] * ws[k:k + 1, :]
    for k in range(SC_CONV - 2):
        snew_ref[k] = sst_ref[k + 1]
    snew_ref[SC_CONV - 2] = cv
    yb_ref[...] = _sigmoid(pb[:, 4 * D_MODEL:5 * D_MODEL]) * (pb[:, 0:D_MODEL] * uc)
    sga_ref[...] = _sigmoid(pb[:, 3 * D_MODEL:4 * D_MODEL])


def _sample_proj(x, mstate_t, sstate_t, w):
    nb = x.shape[0]
    f = lambda *s: jax.ShapeDtypeStruct(s, F32)
    return pl.pallas_call(
        _sproj_body,
        out_shape=[f(nb, SSM_INNER), f(nb, SSM_INNER), f(nb, SSM_INNER), f(nb, SSM_HEADS),
                   f(nb, SSM_GROUPS * SSM_STATE), f(nb, SSM_GROUPS * SSM_STATE),
                   f(nb, D_MODEL), f(nb, D_MODEL),
                   f(SSM_CONV - 1, nb, SSM_CONV_DIM), f(SC_CONV - 1, nb, D_MODEL)],
        compiler_params=pltpu.CompilerParams(vmem_limit_bytes=VMEM_LIMIT),
        name="sample_proj",
    )(x, w["norm_mix"], w["w_a"], w["w_dt"], w["w_b"], w["w_mconv"], w["b_mconv"], w["dt_bias"],
      w["a_log"], w["w_sconv"], mstate_t, sstate_t)


def _sstate_body(dec_ref, s_ref, dtx_ref, bm_ref, cm_ref, snew_ref, y_ref):
    i = pl.program_id(0)
    rows_per_blk = LANES
    for j in range(STATE_BB):
        b = i * STATE_BB + j
        dtx_row = dtx_ref[j:j + 1, :]
        y_parts = []
        for g in range(SSM_GROUPS):
            b_row = bm_ref[j:j + 1, g * SSM_STATE:(g + 1) * SSM_STATE]
            c_row = cm_ref[j:j + 1, g * SSM_STATE:(g + 1) * SSM_STATE].astype(BF16)
            new_blocks = []
            for q in range(GROUP_WIDTH // rows_per_blk):
                r0 = g * GROUP_WIDTH + q * rows_per_blk
                dcol = jnp.broadcast_to(dtx_row[:, r0:r0 + rows_per_blk], (rows_per_blk, LANES)).T
                sub = []
                for hh in range(rows_per_blk // SSM_HEAD_DIM):
                    h = r0 // SSM_HEAD_DIM + hh
                    lo = hh * SSM_HEAD_DIM
                    s_old = s_ref[j, r0 + lo:r0 + lo + SSM_HEAD_DIM, :]
                    sub.append(s_old * dec_ref[b, h] + dcol[lo:lo + SSM_HEAD_DIM, :] * b_row)
                blk = jnp.concatenate(sub, axis=0)
                snew_ref[j, r0:r0 + rows_per_blk, :] = blk
                new_blocks.append(blk.astype(BF16))
            s_g = jnp.concatenate(new_blocks, axis=0)
            y_parts.append(_dot_nt(c_row, s_g))
        y_ref[j:j + 1, :] = jnp.concatenate(y_parts, axis=1)


def _sample_state(dec, state, dtx, bm, cm):
    nb = state.shape[0]
    row = lambda wdt: pl.BlockSpec((STATE_BB, wdt), lambda i, dec: (i, 0))
    st = pl.BlockSpec((STATE_BB, SSM_INNER, SSM_STATE), lambda i, dec: (i, 0, 0))
    return pl.pallas_call(
        _sstate_body,
        grid_spec=pltpu.PrefetchScalarGridSpec(
            num_scalar_prefetch=1, grid=(nb // STATE_BB,),
            in_specs=[st, row(SSM_INNER), row(SSM_GROUPS * SSM_STATE), row(SSM_GROUPS * SSM_STATE)],
            out_specs=[st, row(SSM_INNER)]),
        out_shape=[jax.ShapeDtypeStruct(state.shape, F32), jax.ShapeDtypeStruct((nb, SSM_INNER), F32)],
        compiler_params=_cparams("arbitrary"),
        name="sample_state",
    )(dec, state, dtx, bm, cm)


def _sfin1_body(x_ref, y_ref, xs_ref, z_ref, yb_ref, sga_ref, dskip_ref, gssm_ref, wout_ref,
                gx_ref, wq_ref, h_ref, q_ref):
    y = y_ref[...] + dskip_ref[...] * xs_ref[...]
    y_a = _group_rmsnorm(y * _silu(z_ref[...]), gssm_ref[...])
    merged = sga_ref[...] * y_a + yb_ref[...]
    h = x_ref[...] + _dot(merged.astype(BF16), wout_ref[...])
    h_ref[...] = h
    q_ref[...] = _dot(_rms(h, gx_ref[...]).astype(BF16), wq_ref[...])


def _sample_fin1(x, y, xs, z, yb, sga, w):
    nb = x.shape[0]
    return pl.pallas_call(
        _sfin1_body,
        out_shape=[jax.ShapeDtypeStruct((nb, D_MODEL), F32)] * 2,
        compiler_params=pltpu.CompilerParams(vmem_limit_bytes=VMEM_LIMIT),
        name="sample_fin1",
    )(x, y, xs, z, yb, sga, w["d_skip"], w["norm_ssm"], w["w_out"], w["norm_xattn"], w["w_xq"])


def _sattn_body(q_ref, k_ref, v_ref, o_ref):
    for j in range(ATTN_BB):
        q_row = q_ref[j]
        q4 = jnp.concatenate([q_row[:, h * XA_HEAD_DIM:(h + 1) * XA_HEAD_DIM]
                              for h in range(XA_HEADS)], axis=0)
        s = jnp.sum(k_ref[j] * q4[None], axis=-1, keepdims=True) * (XA_HEAD_DIM ** -0.5)
        e = jnp.exp(s - jnp.max(s, axis=0, keepdims=True))
        p = e / jnp.sum(e, axis=0, keepdims=True)
        o4 = jnp.sum(p * v_ref[j], axis=0)
        o_ref[j] = jnp.concatenate([o4[h:h + 1, :] for h in range(XA_HEADS)], axis=1)


def _sample_attn(q3, k3, v3):
    nb = q3.shape[0]
    qs = pl.BlockSpec((ATTN_BB, 1, D_MODEL), lambda i: (i, 0, 0))
    kv = pl.BlockSpec((ATTN_BB, N_MEM, XA_HEADS, XA_HEAD_DIM), lambda i: (i, 0, 0, 0))
    return pl.pallas_call(
        _sattn_body,
        grid=(nb // ATTN_BB,),
        in_specs=[qs, kv, kv],
        out_specs=qs,
        out_shape=jax.ShapeDtypeStruct((nb, 1, D_MODEL), F32),
        compiler_params=_cparams("arbitrary"),
        name="sample_attn",
    )(q3, k3, v3)


def _sfin2_body(h_ref, o_ref, wo_ref, gmoe_ref, wr_ref, br_ref, h2_ref, hn_ref, lg_ref):
    h2 = h_ref[...] + _dot(o_ref[...].astype(BF16), wo_ref[...])
    _router_tail(h2, gmoe_ref, wr_ref, br_ref, h2_ref, hn_ref, lg_ref)


def _sample_fin2(h1, o, w):
    nb = h1.shape[0]
    return pl.pallas_call(
        _sfin2_body,
        out_shape=[jax.ShapeDtypeStruct((nb, D_MODEL), F32)] * 2
        + [jax.ShapeDtypeStruct((N_EXPERTS, nb), F32)],
        compiler_params=pltpu.CompilerParams(vmem_limit_bytes=VMEM_LIMIT),
        name="sample_fin2",
    )(h1, o, w["w_xo"], w["norm_moe"], w["w_router"], w["b_router"])


def _pad_cols(x, cols):
    return jnp.concatenate([x, jnp.zeros((x.shape[0], cols - x.shape[1]), x.dtype)], axis=1)


def _route_body(lgp_ref, lgs_ref, g_ref, loc_ref, cnt_ref, off_ref):
    tt = lgp_ref.shape[1]
    is_sample = pl.program_id(0) == pl.num_programs(0) - 1
    col = lax.broadcasted_iota(I32, (1, tt), 1)
    valid = jnp.logical_or(jnp.logical_not(is_sample), col < lgs_ref.shape[1])
    work = jnp.where(is_sample, _pad_cols(lgs_ref[...], tt), lgp_ref[...])
    sub = lax.broadcasted_iota(I32, (N_EXPERTS, tt), 0).astype(F32)
    vals, hots = [], []
    for _ in range(TOP_K):
        m = jnp.max(work, axis=0, keepdims=True)
        idx = jnp.min(jnp.where(work == m, sub, float(N_EXPERTS)), axis=0, keepdims=True)
        hot = (sub == idx) & valid
        vals.append(m)
        hots.append(hot)
        work = jnp.where(hot, -jnp.inf, work)
    exps = [jnp.exp(v - vals[0]) for v in vals]
    tot = exps[0]
    for e in exps[1:]:
        tot = tot + e
    assigned = hots[0]
    for hot in hots[1:]:
        assigned = assigned | hot
    a = assigned.astype(BF16)
    r_i = lax.broadcasted_iota(I32, (tt, tt), 0)
    c_i = lax.broadcasted_iota(I32, (tt, tt), 1)
    rank = _dot(a, (r_i < c_i).astype(BF16))
    cnt = jnp.sum(a.astype(F32), axis=1, keepdims=True)
    cnt = jnp.floor((cnt + (SUBLANES - 1)) * (1.0 / SUBLANES)) * SUBLANES
    e_r = lax.broadcasted_iota(I32, (N_EXPERTS, N_EXPERTS), 0)
    e_c = lax.broadcasted_iota(I32, (N_EXPERTS, N_EXPERTS), 1)
    cnt_cols = jnp.broadcast_to(cnt, (N_EXPERTS, LANES)).astype(BF16)
    off = _dot((e_r > e_c).astype(BF16), cnt_cols)[:, 0:1]
    slot = rank + off
    k_sub = lax.broadcasted_iota(I32, (SUBLANES, tt), 0)
    g_out = jnp.zeros((SUBLANES, tt), F32)
    l_out = jnp.full((SUBLANES, tt), -1.0, F32)
    for k in range(TOP_K):
        lk = jnp.sum(jnp.where(hots[k], slot, 0.0), axis=0, keepdims=True)
        g_out = jnp.where(k_sub == k, jnp.where(valid, exps[k] / tot, 0.0), g_out)
        l_out = jnp.where(k_sub == k, jnp.where(valid, lk, -1.0), l_out)
    g_ref[...] = g_out
    loc_ref[...] = l_out.astype(I32)
    cnt_ref[...] = cnt.astype(I32)
    off_ref[...] = off.astype(I32)


def _route(logits_p, logits_s, tt):
    ntp = logits_p.shape[1] // tt
    nt = ntp + 1
    t = nt * tt
    tk = pl.BlockSpec((SUBLANES, tt), lambda i: (0, i))
    per_tile = pl.BlockSpec((None, N_EXPERTS, 1), lambda i: (i, 0, 0))
    return pl.pallas_call(
        _route_body,
        grid=(nt,),
        in_specs=[pl.BlockSpec((N_EXPERTS, tt), lambda i: (0, jnp.minimum(i, ntp - 1))),
                  pl.BlockSpec(logits_s.shape, lambda i: (0, 0))],
        out_specs=[tk, tk, per_tile, per_tile],
        out_shape=[jax.ShapeDtypeStruct((SUBLANES, t), F32), jax.ShapeDtypeStruct((SUBLANES, t), I32),
                   jax.ShapeDtypeStruct((nt, N_EXPERTS, 1), I32), jax.ShapeDtypeStruct((nt, N_EXPERTS, 1), I32)],
        compiler_params=_cparams("arbitrary"),
        name="moe_route",
    )(logits_p, logits_s)


def _sorted_rows(tt):
    return tt * TOP_K + N_EXPERTS * SUBLANES


def _run_copies(tt, tile, cnt_ref, off_ref, base_ref, make_copy, wait):
    if wait:
        total = off_ref[tile, N_EXPERTS - 1] + cnt_ref[tile, N_EXPERTS - 1]

        @pl.when(total > 0)
        def _():
            make_copy(0, 0, pl.multiple_of(total, SUBLANES)).wait()
        return

    def per_expert(e, carry):
        n = cnt_ref[tile, e]

        @pl.when(n > 0)
        def _():
            make_copy(pl.multiple_of(off_ref[tile, e], SUBLANES),
                      pl.multiple_of(base_ref[tile, e], SUBLANES), pl.multiple_of(n, SUBLANES)).start()
        return carry

    lax.fori_loop(0, N_EXPERTS, per_expert, 0)


def _dispatch_body(tm, tt, cnt_ref, off_ref, base_ref, zstart_ref, loc_ref, x_ref, xs_ref, o_hbm,
                   zero_ref, srt_ref, zsem, sems):
    i = pl.program_id(0)
    last = pl.num_programs(0) - 1
    r = _sorted_rows(tt)

    def zero_copy(j):
        return pltpu.make_async_copy(zero_ref, o_hbm.at[pl.ds(pl.multiple_of(zstart_ref[j], tm), tm), :], zsem)

    @pl.when(i == 0)
    def _():
        zero_ref[...] = jnp.zeros_like(zero_ref)

        def start(j, carry):
            @pl.when(zstart_ref[j] >= 0)
            def _():
                zero_copy(j).start()
            return carry

        def wait(j, carry):
            @pl.when(zstart_ref[j] >= 0)
            def _():
                zero_copy(j).wait()
            return carry

        lax.fori_loop(0, zstart_ref.shape[0], start, 0)
        lax.fori_loop(0, zstart_ref.shape[0], wait, 0)

    loc = loc_ref[...]
    slot_i = lax.broadcasted_iota(I32, (r, tt), 0)
    hit = slot_i == loc[0:1, :]
    for k in range(1, TOP_K):
        hit = hit | (slot_i == loc[k:k + 1, :])
    buf = i % 2
    x = jnp.where(i == last, _pad_rows(xs_ref[...], tt), x_ref[...])
    srt_ref[buf] = _pack_bf16_pairs(_dot(hit.astype(BF16), x.astype(BF16)), is_bf16_valued=True)

    def copies(tile, wait):
        b = tile % 2

        def make_copy(lo, go, size):
            return pltpu.make_async_copy(srt_ref.at[b, pl.ds(lo, size), :], o_hbm.at[pl.ds(go, size), :],
                                         sems.at[b])

        _run_copies(tt, tile, cnt_ref, off_ref, base_ref, make_copy, wait)

    copies(i, False)

    @pl.when(i > 0)
    def _():
        copies(i - 1, True)

    @pl.when(i == last)
    def _():
        copies(i, True)


def _dispatch(cnt, off, base, zero_starts, loc, hn_p, hn_s, n_rows, tm, tt):
    ntp = hn_p.shape[0] // tt
    smem = pl.BlockSpec(memory_space=pltpu.SMEM)
    return pl.pallas_call(
        functools.partial(_dispatch_body, tm, tt),
        grid_spec=pltpu.PrefetchScalarGridSpec(
            num_scalar_prefetch=0, grid=(ntp + 1,),
            in_specs=[smem, smem, smem, smem,
                      pl.BlockSpec((SUBLANES, tt), lambda i: (0, i)),
                      pl.BlockSpec((tt, D_MODEL), lambda i: (jnp.minimum(i, ntp - 1), 0)),
                      pl.BlockSpec(hn_s.shape, lambda i: (0, 0))],
            out_specs=pl.BlockSpec(memory_space=pl.ANY),
            scratch_shapes=[pltpu.VMEM((tm, D_MODEL // 2), U32),
                            pltpu.VMEM((2, _sorted_rows(tt), D_MODEL // 2), U32),
                            pltpu.SemaphoreType.DMA, pltpu.SemaphoreType.DMA((2,))]),
        out_shape=jax.ShapeDtypeStruct((n_rows, D_MODEL // 2), U32),
        compiler_params=_cparams("arbitrary"),
        name="moe_dispatch",
    )(cnt, off, base, zero_starts, loc, hn_p, hn_s)


def _expert_body(be_ref, nu_ref, bv_ref, slot_ref, nxt_ref, x_ref, wgu_hbm, bgu_ref, wdn_hbm, bdn_ref,
                 y_ref, wgu_f, wdn_f, wgu_b, wdn_b, sems):
    i = pl.program_id(0)
    tm = x_ref.shape[0]
    valid = bv_ref[i]
    expert = be_ref[i]
    slot = slot_ref[i]

    def weight_copies(e, s):
        return (pltpu.make_async_copy(wgu_hbm.at[e], wgu_f.at[s], sems.at[0, s]),
                pltpu.make_async_copy(wdn_hbm.at[e], wdn_f.at[s], sems.at[1, s]))

    @pl.when(i == 0)
    def _():
        for cp in weight_copies(expert, slot):
            cp.start()

    first = jnp.logical_and(i < nu_ref[0], jnp.logical_or(i == 0, expert != be_ref[jnp.maximum(i - 1, 0)]))

    @pl.when(first)
    def _():
        for cp in weight_copies(expert, slot):
            cp.wait()

        @pl.when(nxt_ref[i] >= 0)
        def _():
            for cp in weight_copies(nxt_ref[i], 1 - slot):
                cp.start()

    def ffn(rows, cast):
        xb = _unpack_bf16_pairs(x_ref[0:rows, :])
        gu_parts = []
        for j in range(2 * D_FF // EXPERT_COL_CHUNK):
            cs = slice(j * EXPERT_COL_CHUNK, (j + 1) * EXPERT_COL_CHUNK)
            if cast:
                wgu_b[:, cs] = wgu_f[slot, :, cs].astype(BF16)
            gu_parts.append(_dot(xb, wgu_b[:, cs]) + bgu_ref[:, cs])
        gate = jnp.minimum(jnp.concatenate(gu_parts[:len(gu_parts) // 2], axis=1), SWIGLU_LIMIT)
        up = jnp.clip(jnp.concatenate(gu_parts[len(gu_parts) // 2:], axis=1), -SWIGLU_LIMIT, SWIGLU_LIMIT)
        act = ((up + 1.0) * (gate * _sigmoid(SWIGLU_ALPHA * gate))).astype(BF16)
        y_parts = []
        for j in range(D_MODEL // EXPERT_COL_CHUNK):
            cs = slice(j * EXPERT_COL_CHUNK, (j + 1) * EXPERT_COL_CHUNK)
            if cast:
                wdn_b[:, cs] = wdn_f[slot, :, cs].astype(BF16)
            y_parts.append(_dot(act, wdn_b[:, cs]) + bdn_ref[:, cs])
        y_ref[0:rows, :] = _pack_bf16_pairs(jnp.concatenate(y_parts, axis=1))
        if rows < tm:
            y_ref[rows:tm, :] = jnp.zeros((tm - rows, D_MODEL // 2), U32)

    quarter = tm // EXPERT_ROW_SPLITS
    for q in range(1, EXPERT_ROW_SPLITS + 1):
        in_q = jnp.logical_and(valid > (q - 1) * quarter, valid <= q * quarter)
        for cast in (True, False):
            @pl.when(jnp.logical_and(in_q, first == cast))
            def _(q=q, cast=cast):
                ffn(q * quarter, cast)

    @pl.when(valid == 0)
    def _():
        y_ref[...] = jnp.zeros_like(y_ref)


def _experts(block_e, n_used, block_valid, block_slot, block_next, xs, wgu, bgu, wdn, bdn, tm):
    n_rows = xs.shape[0]
    return pl.pallas_call(
        _expert_body,
        grid_spec=pltpu.PrefetchScalarGridSpec(
            num_scalar_prefetch=5, grid=(n_rows // tm,),
            in_specs=[pl.BlockSpec((tm, D_MODEL // 2), lambda i, be, nu, *_: (jnp.minimum(i, nu[0] - 1), 0)),
                      pl.BlockSpec(memory_space=pl.ANY),
                      pl.BlockSpec((None, 1, 2 * D_FF), lambda i, be, *_: (be[i], 0, 0)),
                      pl.BlockSpec(memory_space=pl.ANY),
                      pl.BlockSpec((None, 1, D_MODEL), lambda i, be, *_: (be[i], 0, 0))],
            out_specs=pl.BlockSpec((tm, D_MODEL // 2), lambda i, *_: (i, 0)),
            scratch_shapes=[pltpu.VMEM((2, D_MODEL, 2 * D_FF), F32), pltpu.VMEM((2, D_FF, D_MODEL), F32),
                            pltpu.VMEM((D_MODEL, 2 * D_FF), BF16), pltpu.VMEM((D_FF, D_MODEL), BF16),
                            pltpu.SemaphoreType.DMA((2, 2))]),
        out_shape=jax.ShapeDtypeStruct((n_rows, D_MODEL // 2), U32),
        compiler_params=_cparams("arbitrary"),
        name="moe_experts",
    )(block_e, n_used, block_valid, block_slot, block_next, xs, wgu, bgu, wdn, bdn)


def _combine_body(tt, cnt_ref, off_ref, base_ref, loc_ref, g_ref, h_ref, hs_ref, gfin_ref, ys_hbm,
                  y_ref, ysmp_ref, buf_ref, sems):
    i = pl.program_id(0)
    last = pl.num_programs(0) - 1
    r = _sorted_rows(tt)

    def copies(tile, wait):
        b = tile % 2

        def make_copy(lo, go, size):
            return pltpu.make_async_copy(ys_hbm.at[pl.ds(go, size), :], buf_ref.at[b, pl.ds(lo, size), :],
                                         sems.at[b])

        _run_copies(tt, tile, cnt_ref, off_ref, base_ref, make_copy, wait)

    @pl.when(i == 0)
    def _():
        buf_ref[...] = jnp.zeros_like(buf_ref)
        copies(0, False)

    @pl.when(i < last)
    def _():
        copies(i + 1, False)

    copies(i, True)
    loc = loc_ref[...]
    gates = g_ref[...]
    slot_i = lax.broadcasted_iota(I32, (r, tt), 0)
    gmat = jnp.where(slot_i == loc[0:1, :], gates[0:1, :], 0.0)
    for k in range(1, TOP_K):
        gmat = gmat + jnp.where(slot_i == loc[k:k + 1, :], gates[k:k + 1, :], 0.0)
    h = jnp.where(i == last, _pad_rows(hs_ref[...], tt), h_ref[...])
    moe = lax.dot_general(gmat.astype(BF16), _unpack_bf16_pairs(buf_ref[i % 2]), TN_DIMS,
                          preferred_element_type=F32)
    y = _rms(h + moe, gfin_ref[...])

    @pl.when(i < last)
    def _():
        y_ref[...] = y

    @pl.when(i == last)
    def _():
        ysmp_ref[...] = y[0:ysmp_ref.shape[0], :]


def _combine(cnt, off, base, loc, gates, h2_p, h2_s, norm_final, ys, tt):
    ntp = h2_p.shape[0] // tt
    tok = pl.BlockSpec((tt, D_MODEL), lambda i: (jnp.minimum(i, ntp - 1), 0))
    smp = pl.BlockSpec(h2_s.shape, lambda i: (0, 0))
    tk = pl.BlockSpec((SUBLANES, tt), lambda i: (0, i))
    smem = pl.BlockSpec(memory_space=pltpu.SMEM)
    return pl.pallas_call(
        functools.partial(_combine_body, tt),
        grid_spec=pltpu.PrefetchScalarGridSpec(
            num_scalar_prefetch=0, grid=(ntp + 1,),
            in_specs=[smem, smem, smem, tk, tk, tok, smp,
                      pl.BlockSpec((1, D_MODEL), lambda i: (0, 0)),
                      pl.BlockSpec(memory_space=pl.ANY)],
            out_specs=[tok, smp],
            scratch_shapes=[pltpu.VMEM((2, _sorted_rows(tt), D_MODEL // 2), U32), pltpu.SemaphoreType.DMA((2,))]),
        out_shape=[jax.ShapeDtypeStruct(h2_p.shape, F32), jax.ShapeDtypeStruct(h2_s.shape, F32)],
        compiler_params=_cparams("arbitrary"),
        name="moe_combine",
    )(cnt, off, base, loc, gates, h2_p, h2_s, norm_final, ys)


def _moe_and_final_norm(hn_p, logits_p, h2_p, hn_s, logits_s, h2_s, w, tt, tm):
    nt = hn_p.shape[0] // tt + 1
    t = hn_p.shape[0] + hn_s.shape[0]
    gates, loc, cnt3, off3 = _route(logits_p, logits_s, tt)
    cnt = cnt3[:, :, 0]
    counts = jnp.sum(cnt, axis=0)
    padded = (counts + tm - 1) // tm * tm
    pad_end = jnp.cumsum(padded)
    start = pad_end - padded
    off = off3[:, :, 0]
    base = (start[None, :] + jnp.cumsum(cnt, axis=0) - cnt).astype(I32)
    n_blocks = (t * TOP_K + nt * N_EXPERTS * (SUBLANES - 1) + N_EXPERTS * (tm - 1)) // tm
    n_rows = n_blocks * tm
    block_start = jnp.arange(n_blocks, dtype=I32) * tm
    block_e = jnp.minimum(jnp.sum(block_start[:, None] >= pad_end[None, :], axis=-1), N_EXPERTS - 1).astype(I32)
    n_used = (pad_end[-1:] // tm).astype(I32)
    zero_starts = jnp.concatenate([jnp.where(padded > 0, pad_end - tm, -1),
                                   jnp.where(block_start >= pad_end[-1], block_start, -1)]).astype(I32)
    xs = _dispatch(cnt, off, base, zero_starts, loc, hn_p, hn_s, n_rows, tm, tt)
    e_ids = jnp.arange(N_EXPERTS, dtype=I32)
    block_hot = block_e[:, None] == e_ids[None, :]

    def per_block(table):
        return jnp.sum(jnp.where(block_hot, table[None, :], 0), axis=1).astype(I32)

    block_valid = jnp.clip(per_block(start + counts) - block_start, 0, tm)
    block_valid = jnp.where(block_start < pad_end[-1], block_valid, 0).astype(I32)
    present = padded > 0
    later = present[None, :] & (e_ids[None, :] > e_ids[:, None])
    next_e = jnp.min(jnp.where(later, e_ids[None, :], N_EXPERTS), axis=1)
    next_e = jnp.where(next_e < N_EXPERTS, next_e, -1).astype(I32)
    before = present[None, :] & (e_ids[None, :] < e_ids[:, None])
    run_slot = (jnp.sum(before.astype(I32), axis=1) % 2).astype(I32)
    ys = _experts(block_e, n_used, block_valid, per_block(run_slot), per_block(next_e), xs, w["w_gate_up"], w["b_gate_up"], w["w_down"], w["b_down"], tm)
    return _combine(cnt, off, base, loc, gates, h2_p, h2_s, w["norm_final"], ys, tt)


def kernel(x_prompt, x_sample, mem_prompt, state_ssm, state_mamba_conv, state_short_conv, cache_mem_k, cache_mem_v, norm_mix, w_in, w_mconv, b_mconv, dt_bias, a_log, d_skip, norm_ssm, w_sconv, w_out, norm_xattn, norm_mem, w_xq, w_xk, w_xv, w_xo, norm_moe, w_router, b_router, w_gate_up, b_gate_up, w_down, b_down, norm_final):
    nbp, seq, _ = x_prompt.shape
    nbs = x_sample.shape[0]
    dt_lo = SSM_INNER + SSM_CONV_DIM
    w_in0 = w_in[0]
    w_dt = w_in0[:, dt_lo:dt_lo + SSM_HEADS]
    w = {
        "norm_mix": norm_mix, "norm_ssm": norm_ssm, "norm_xattn": norm_xattn, "norm_moe": norm_moe,
        "norm_final": norm_final.reshape(1, D_MODEL),
        "w_a": w_in0[:, :dt_lo].astype(BF16),
        "w_dt": w_dt.astype(BF16), "w_dt_t": w_dt.T.astype(BF16),
        "w_b": w_in0[:, dt_lo + SSM_HEADS:].astype(BF16),
        "w_mconv": w_mconv[0], "b_mconv": b_mconv,
        "dt_bias": dt_bias, "dt_bias_t": dt_bias.reshape(SSM_HEADS, 1),
        "a_log": a_log, "a_log_t": a_log.reshape(SSM_HEADS, 1),
        "d_skip": jnp.repeat(d_skip, SSM_HEAD_DIM, axis=1),
        "w_sconv": w_sconv[0], "w_out": w_out[0].astype(BF16),
        "w_xq": w_xq[0].astype(BF16), "w_xo": w_xo[0].astype(BF16),
        "w_router": w_router[0].T.astype(BF16), "b_router": b_router.reshape(N_EXPERTS, 1),
        "w_gate_up": w_gate_up[0], "b_gate_up": b_gate_up[0].reshape(N_EXPERTS, 1, 2 * D_FF),
        "w_down": w_down[0], "b_down": b_down[0].reshape(N_EXPERTS, 1, D_MODEL),
    }

    k_p, v_p, kb, vb = _mem_kv(mem_prompt.reshape(nbp * N_MEM, D_MODEL), norm_mem,
                               w_xk[0].astype(BF16), w_xv[0].astype(BF16))
    h1, ssm_p, mconv_p, sconv_p = _prompt_mixer(x_prompt.reshape(nbp * seq, D_MODEL), nbp, w)
    h2, hn, logits = _prompt_attn(h1, kb, vb, nbp, w)

    xs2 = x_sample.reshape(nbs, D_MODEL)
    mstate_t = jnp.transpose(state_mamba_conv[0], (1, 0, 2))
    sstate_t = jnp.transpose(state_short_conv[0], (1, 0, 2))
    z, xs_, dtx, dec, bm, cm, yb, sga, mnew_t, snew_t = _sample_proj(xs2, mstate_t, sstate_t, w)
    ssm_s, y_s = _sample_state(dec, state_ssm[0].reshape(nbs, SSM_INNER, SSM_STATE), dtx, bm, cm)
    h1s, q_s = _sample_fin1(xs2, y_s, xs_, z, yb, sga, w)
    o_s = _sample_attn(q_s.reshape(nbs, 1, D_MODEL),
                       cache_mem_k[0], cache_mem_v[0])
    h2s, hns, logits_s = _sample_fin2(h1s, o_s.reshape(nbs, D_MODEL), w)
    y_prompt, y_sample = _moe_and_final_norm(hn, logits, h2, hns, logits_s, h2s, w, MIX_TILE, MOE_ROW_TILE)

    return (y_prompt.reshape(nbp, seq, D_MODEL),
            y_sample.reshape(nbs, 1, D_MODEL),
            ssm_p.reshape(1, nbp, SSM_HEADS, SSM_HEAD_DIM, SSM_STATE),
            mconv_p[None], sconv_p[None],
            k_p.reshape(1, nbp, N_MEM, XA_HEADS, XA_HEAD_DIM),
            v_p.reshape(1, nbp, N_MEM, XA_HEADS, XA_HEAD_DIM),
            ssm_s.reshape(1, nbs, SSM_HEADS, SSM_HEAD_DIM, SSM_STATE),
            jnp.transpose(mnew_t, (1, 0, 2))[None],
            jnp.transpose(snew_t, (1, 0, 2))[None])
```

```python
import functools

import jax
import jax.numpy as jnp
from jax import lax
from jax.experimental import pallas as pl
from jax.experimental.pallas import tpu as pltpu

F32 = jnp.float32
BF16 = jnp.bfloat16
I32 = jnp.int32
U32 = jnp.uint32

D_MODEL = 1024
N_MEM = 256
SSM_HEADS = 16
SSM_HEAD_DIM = 64
SSM_INNER = SSM_HEADS * SSM_HEAD_DIM
SSM_STATE = 128
SSM_GROUPS = 4
HEADS_PER_GROUP = SSM_HEADS // SSM_GROUPS
GROUP_WIDTH = SSM_INNER // SSM_GROUPS
SSM_CONV = 4
SSM_CONV_DIM = SSM_INNER + 2 * SSM_GROUPS * SSM_STATE
SC_CONV = 3
XA_HEADS = 4
XA_HEAD_DIM = D_MODEL // XA_HEADS
N_EXPERTS = 32
TOP_K = 4
D_FF = D_MODEL
SWIGLU_LIMIT = 7.0
SWIGLU_ALPHA = 1.702
EPS = 1e-6

LANES = 128
SUBLANES = 8
VMEM_LIMIT = 56 * 1024 * 1024

MIX_TILE = 256
MOE_ROW_TILE = 1024
EXPERT_ROW_SPLITS = 4
EXPERT_COL_CHUNK = 512
STATE_BB = 8
ATTN_BB = 4

NT_DIMS = (((1,), (1,)), ((), ()))
TN_DIMS = (((0,), (0,)), ((), ()))


def _cparams(*sem):
    return pltpu.CompilerParams(dimension_semantics=sem, vmem_limit_bytes=VMEM_LIMIT)


def _const_spec(shape):
    nd = len(shape)
    return pl.BlockSpec(shape, lambda *_: (0,) * nd, pipeline_mode=pl.Buffered(1))


def _sigmoid(x):
    return 1.0 / (1.0 + jnp.exp(-x))


def _silu(x):
    return x * _sigmoid(x)


def _softplus(x):
    return jnp.maximum(x, 0.0) + jnp.log(1.0 + jnp.exp(-jnp.abs(x)))


def _rms(x, g):
    ms = jnp.mean(x * x, axis=-1, keepdims=True)
    return x * lax.rsqrt(ms + EPS) * g


def _dot(a, b):
    return jnp.dot(a, b, preferred_element_type=F32)


def _dot_nt(a, b):
    return lax.dot_general(a, b, NT_DIMS, preferred_element_type=F32)


def _expand_heads(v):
    rows = v.shape[0]
    lane = lax.broadcasted_iota(I32, (rows, LANES), 1)
    pieces = []
    for j in range(SSM_HEADS // 2):
        a = jnp.broadcast_to(v[:, 2 * j:2 * j + 1], (rows, LANES))
        b = jnp.broadcast_to(v[:, 2 * j + 1:2 * j + 2], (rows, LANES))
        pieces.append(jnp.where(lane < SSM_HEAD_DIM, a, b))
    return jnp.concatenate(pieces, axis=1)


def _cumsum(x, axis):
    idx = lax.broadcasted_iota(I32, x.shape, axis)
    shift = 1
    while shift < x.shape[axis]:
        x = x + jnp.where(idx >= shift, pltpu.roll(x, shift, axis), 0.0)
        shift *= 2
    return x


def _pack_bf16_pairs(x, is_bf16_valued=False):
    w = x.shape[1] // 2
    if not is_bf16_valued:
        x = x.astype(BF16).astype(F32)
    bits = lax.bitcast_convert_type(x, U32)
    return (bits[:, w:] & jnp.uint32(0xFFFF0000)) | (bits[:, :w] >> 16)


def _unpack_bf16_pairs(p):
    lo = lax.bitcast_convert_type(p << 16, F32)
    hi = lax.bitcast_convert_type(p & jnp.uint32(0xFFFF0000), F32)
    return jnp.concatenate([lo, hi], axis=1).astype(BF16)


def _pad_rows(x, rows):
    return jnp.concatenate([x, jnp.zeros((rows - x.shape[0], x.shape[1]), x.dtype)], axis=0)


def _group_rmsnorm(u, g):
    outs = []
    for k in range(SSM_GROUPS):
        ug = u[:, k * GROUP_WIDTH:(k + 1) * GROUP_WIDTH]
        ms = jnp.mean(ug * ug, axis=-1, keepdims=True)
        outs.append(ug * lax.rsqrt(ms + EPS))
    return jnp.concatenate(outs, axis=1) * g


def _memkv_body(mem_ref, g_ref, wk_ref, wv_ref, k_ref, v_ref, kb_ref, vb_ref):
    mn = _rms(mem_ref[...], g_ref[...]).astype(BF16)
    k = _dot(mn, wk_ref[...])
    v = _dot(mn, wv_ref[...])
    k_ref[...] = k
    v_ref[...] = v
    kb_ref[...] = k.astype(BF16)
    vb_ref[...] = v.astype(BF16)


def _mem_kv(mem2d, norm_mem, wk, wv):
    rows = mem2d.shape[0]
    nb = rows // N_MEM
    blk = pl.BlockSpec((N_MEM, D_MODEL), lambda b: (b, 0))
    return pl.pallas_call(
        _memkv_body,
        grid=(nb,),
        in_specs=[blk, _const_spec((1, D_MODEL)), _const_spec((D_MODEL, D_MODEL)),
                  _const_spec((D_MODEL, D_MODEL))],
        out_specs=[blk, blk, blk, blk],
        out_shape=[jax.ShapeDtypeStruct((rows, D_MODEL), F32)] * 2
        + [jax.ShapeDtypeStruct((rows, D_MODEL), BF16)] * 2,
        compiler_params=_cparams("arbitrary"),
        name="mem_kv",
    )(mem2d, norm_mem, wk, wv)


def _mix_body(x_ref, gmix_ref, wa_ref, wdtc_ref, wdtr_ref, wb_ref, wmc_ref, bmc_ref,
              dtb_ref, dtbt_ref, alog_ref, alogt_ref, dskip_ref, gssm_ref, wsc_ref, wout_ref,
              h_ref, ssm_ref, mbuf_ref, sbuf_ref,
              st_ref, cbuf_ref, scbuf_ref):
    tq = MIX_TILE
    c = pl.program_id(1)

    @pl.when(c == 0)
    def _():
        st_ref[...] = jnp.zeros_like(st_ref)
        cbuf_ref[0:SUBLANES, :] = jnp.zeros((SUBLANES, SSM_CONV_DIM), F32)
        scbuf_ref[0:SUBLANES, :] = jnp.zeros((SUBLANES, D_MODEL), F32)

    x = x_ref[...]
    xn = _rms(x, gmix_ref[...]).astype(BF16)

    u = _dot(xn, wa_ref[:, SSM_INNER:])
    cbuf_ref[SUBLANES:SUBLANES + tq, :] = u
    wm = wmc_ref[...]
    conv = u * wm[SSM_CONV - 1:SSM_CONV, :] + bmc_ref[...]
    for k in range(SSM_CONV - 1):
        off = SUBLANES - (SSM_CONV - 1) + k
        conv = conv + cbuf_ref[off:off + tq, :] * wm[k:k + 1, :]
    tail = cbuf_ref[tq + SUBLANES - (SSM_CONV - 1):tq + SUBLANES, :]
    mbuf_ref[...] = tail
    cbuf_ref[SUBLANES - (SSM_CONV - 1):SUBLANES, :] = tail
    xbc = _silu(conv)
    xs = xbc[:, :SSM_INNER]
    bm = xbc[:, SSM_INNER:SSM_INNER + SSM_GROUPS * SSM_STATE]
    cm = xbc[:, SSM_INNER + SSM_GROUPS * SSM_STATE:]

    dt = _softplus(_dot(xn, wdtc_ref[...]) + dtb_ref[...])
    dtt = _softplus(_dot_nt(wdtr_ref[...], xn) + dtbt_ref[...])
    a_row = -jnp.exp(alog_ref[...])
    a_col = -jnp.exp(alogt_ref[...])
    row_i = lax.broadcasted_iota(I32, (tq, tq), 0)
    col_i = lax.broadcasted_iota(I32, (tq, tq), 1)
    causal = row_i >= col_i
    a_cum = _cumsum(dt * a_row, 0)
    a_cumt = _cumsum(dtt * a_col, 1)
    a_last = a_cum[tq - 1:tq, :]

    xdt = xs * _expand_heads(dt)
    in_decay = _expand_heads(jnp.exp(a_cum))
    to_end = _expand_heads(jnp.exp(a_last - a_cum))
    chunk_decay = _expand_heads(jnp.exp(a_last))
    xdt_b = xdt.astype(BF16)
    xend_b = (xdt * to_end).astype(BF16)
    lane = lax.broadcasted_iota(I32, (tq, LANES), 1)

    def proj_b(k):
        return _dot(xn, wb_ref[:, k * D_MODEL:(k + 1) * D_MODEL])

    pb = []
    y_groups = []
    for g in range(SSM_GROUPS):
        pb.append(proj_b(g))
        if g == 0:
            z = _dot(xn, wa_ref[:, :SSM_INNER])
        if g == 2:
            g_b = proj_b(SSM_GROUPS)
        cg = cm[:, g * SSM_STATE:(g + 1) * SSM_STATE].astype(BF16)
        bg_f = bm[:, g * SSM_STATE:(g + 1) * SSM_STATE]
        bg = bg_f.astype(BF16)
        scores = _dot_nt(cg, bg)
        gs = slice(g * GROUP_WIDTH, (g + 1) * GROUP_WIDTH)
        st_g = st_ref[:, gs]
        y_off = _dot(cg, st_g.astype(BF16)) * in_decay[:, gs]
        pair_out = []
        for pr in range(HEADS_PER_GROUP // 2):
            h0 = g * HEADS_PER_GROUP + 2 * pr
            xp = xdt_b[:, h0 * SSM_HEAD_DIM:(h0 + 2) * SSM_HEAD_DIM]
            ys = []
            for h in (h0, h0 + 1):
                seg = a_cum[:, h:h + 1] - a_cumt[h:h + 1, :]
                decay = jnp.where(causal, jnp.exp(jnp.minimum(seg, 0.0)), 0.0)
                ys.append(_dot((scores * decay).astype(BF16), xp))
            pair_out.append(jnp.where(lane < SSM_HEAD_DIM, ys[0], ys[1]))
        y_groups.append(jnp.concatenate(pair_out, axis=1) + y_off)
        st_ref[:, gs] = st_g * chunk_decay[:, gs] + _dot(bg_f.T.astype(BF16), xend_b[:, gs])
    y = jnp.concatenate(y_groups, axis=1) + dskip_ref[...] * xs
    y_a = _group_rmsnorm(y * _silu(z), gssm_ref[...])

    sc_b, sc_c, sc_v, g_a = pb
    cv = sc_c * sc_v
    scbuf_ref[SUBLANES:SUBLANES + tq, :] = cv
    ws = wsc_ref[...]
    uc = cv * ws[SC_CONV - 1:SC_CONV, :]
    for k in range(SC_CONV - 1):
        off = SUBLANES - (SC_CONV - 1) + k
        uc = uc + scbuf_ref[off:off + tq, :] * ws[k:k + 1, :]
    stail = scbuf_ref[tq + SUBLANES - (SC_CONV - 1):tq + SUBLANES, :]
    sbuf_ref[...] = stail
    scbuf_ref[SUBLANES - (SC_CONV - 1):SUBLANES, :] = stail
    merged = _sigmoid(g_a) * y_a + _sigmoid(g_b) * (sc_b * uc)
    h_ref[...] = x + _dot(merged.astype(BF16), wout_ref[...])

    @pl.when(c == pl.num_programs(1) - 1)
    def _():
        ssm_ref[...] = st_ref[...].T


def _prompt_mixer(x2d, nb, w):
    t = x2d.shape[0]
    nc = t // nb // MIX_TILE
    tok = pl.BlockSpec((MIX_TILE, D_MODEL), lambda b, c: (b * nc + c, 0))
    return pl.pallas_call(
        _mix_body,
        grid=(nb, nc),
        in_specs=[tok, _const_spec((1, D_MODEL)),
                  _const_spec((D_MODEL, SSM_INNER + SSM_CONV_DIM)),
                  _const_spec((D_MODEL, SSM_HEADS)), _const_spec((SSM_HEADS, D_MODEL)),
                  _const_spec((D_MODEL, 5 * D_MODEL)),
                  _const_spec((SSM_CONV, SSM_CONV_DIM)), _const_spec((1, SSM_CONV_DIM)),
                  _const_spec((1, SSM_HEADS)), _const_spec((SSM_HEADS, 1)),
                  _const_spec((1, SSM_HEADS)), _const_spec((SSM_HEADS, 1)),
                  _const_spec((1, SSM_INNER)), _const_spec((1, SSM_INNER)),
                  _const_spec((SC_CONV, D_MODEL)), _const_spec((D_MODEL, D_MODEL))],
        out_specs=[tok,
                   pl.BlockSpec((None, SSM_INNER, SSM_STATE), lambda b, c: (b, 0, 0)),
                   pl.BlockSpec((None, SSM_CONV - 1, SSM_CONV_DIM), lambda b, c: (b, 0, 0)),
                   pl.BlockSpec((None, SC_CONV - 1, D_MODEL), lambda b, c: (b, 0, 0))],
        out_shape=[jax.ShapeDtypeStruct((t, D_MODEL), F32),
                   jax.ShapeDtypeStruct((nb, SSM_INNER, SSM_STATE), F32),
                   jax.ShapeDtypeStruct((nb, SSM_CONV - 1, SSM_CONV_DIM), F32),
                   jax.ShapeDtypeStruct((nb, SC_CONV - 1, D_MODEL), F32)],
        scratch_shapes=[pltpu.VMEM((SSM_STATE, SSM_INNER), F32),
                        pltpu.VMEM((MIX_TILE + SUBLANES, SSM_CONV_DIM), F32),
                        pltpu.VMEM((MIX_TILE + SUBLANES, D_MODEL), F32)],
        compiler_params=_cparams("arbitrary", "arbitrary"),
        name="prompt_mixer",
    )(x2d, w["norm_mix"], w["w_a"], w["w_dt"], w["w_dt_t"], w["w_b"], w["w_mconv"], w["b_mconv"],
      w["dt_bias"], w["dt_bias_t"], w["a_log"], w["a_log_t"], w["d_skip"], w["norm_ssm"],
      w["w_sconv"], w["w_out"])


def _router_tail(h2, gmoe_ref, wr_ref, br_ref, h2_ref, hn_ref, lg_ref):
    h2_ref[...] = h2
    hn = _rms(h2, gmoe_ref[...])
    hn_ref[...] = hn
    lg_ref[...] = _dot_nt(wr_ref[...], hn.astype(BF16)) + br_ref[...]


def _attn_body(h_ref, gx_ref, wq_ref, k_ref, v_ref, wo_ref, gmoe_ref, wr_ref, br_ref,
               h2_ref, hn_ref, lg_ref):
    h = h_ref[...]
    hn = _rms(h, gx_ref[...]).astype(BF16)
    q = _dot(hn, wq_ref[...]).astype(BF16)
    outs = []
    for hd in range(XA_HEADS):
        sl = slice(hd * XA_HEAD_DIM, (hd + 1) * XA_HEAD_DIM)
        s = _dot_nt(q[:, sl], k_ref[:, sl]) * (XA_HEAD_DIM ** -0.5)
        e = jnp.exp(s - jnp.max(s, axis=-1, keepdims=True))
        p = e / jnp.sum(e, axis=-1, keepdims=True)
        outs.append(_dot(p.astype(BF16), v_ref[:, sl]))
    o = jnp.concatenate(outs, axis=1).astype(BF16)
    h2 = h + _dot(o, wo_ref[...])
    _router_tail(h2, gmoe_ref, wr_ref, br_ref, h2_ref, hn_ref, lg_ref)


def _prompt_attn(h1, kb, vb, nb, w):
    t = h1.shape[0]
    nc = t // nb // MIX_TILE
    tok = pl.BlockSpec((MIX_TILE, D_MODEL), lambda b, c: (b * nc + c, 0))
    kv = pl.BlockSpec((N_MEM, D_MODEL), lambda b, c: (b, 0))
    return pl.pallas_call(
        _attn_body,
        grid=(nb, nc),
        in_specs=[tok, _const_spec((1, D_MODEL)), _const_spec((D_MODEL, D_MODEL)), kv, kv,
                  _const_spec((D_MODEL, D_MODEL)), _const_spec((1, D_MODEL)),
                  _const_spec((N_EXPERTS, D_MODEL)), _const_spec((N_EXPERTS, 1))],
        out_specs=[tok, tok, pl.BlockSpec((N_EXPERTS, MIX_TILE), lambda b, c: (0, b * nc + c))],
        out_shape=[jax.ShapeDtypeStruct((t, D_MODEL), F32),
                   jax.ShapeDtypeStruct((t, D_MODEL), F32),
                   jax.ShapeDtypeStruct((N_EXPERTS, t), F32)],
        compiler_params=_cparams("arbitrary", "arbitrary"),
        name="prompt_attn",
    )(h1, w["norm_xattn"], w["w_xq"], kb, vb, w["w_xo"], w["norm_moe"], w["w_router"], w["b_router"])


def _sproj_body(x_ref, gmix_ref, wa_ref, wdtc_ref, wb_ref, wmc_ref, bmc_ref, dtb_ref, alog_ref,
                wsc_ref, mst_ref, sst_ref,
                z_ref, xs_ref, dtx_ref, dec_ref, bm_ref, cm_ref, yb_ref, sga_ref, mnew_ref, snew_ref):
    x = x_ref[...]
    xn = _rms(x, gmix_ref[...]).astype(BF16)
    pa = _dot(xn, wa_ref[...])
    z_ref[...] = pa[:, :SSM_INNER]
    u = pa[:, SSM_INNER:]
    wm = wmc_ref[...]
    conv = u * wm[SSM_CONV - 1:SSM_CONV, :] + bmc_ref[...]
    for k in range(SSM_CONV - 1):
        conv = conv + mst_ref[k] * wm[k:k + 1, :]
    for k in range(SSM_CONV - 2):
        mnew_ref[k] = mst_ref[k + 1]
    mnew_ref[SSM_CONV - 2] = u
    xbc = _silu(conv)
    xs = xbc[:, :SSM_INNER]
    xs_ref[...] = xs
    bm_ref[...] = xbc[:, SSM_INNER:SSM_INNER + SSM_GROUPS * SSM_STATE]
    cm_ref[...] = xbc[:, SSM_INNER + SSM_GROUPS * SSM_STATE:]
    dt = _softplus(_dot(xn, wdtc_ref[...]) + dtb_ref[...])
    dec_ref[...] = jnp.exp(dt * (-jnp.exp(alog_ref[...])))
    dtx_ref[...] = xs * _expand_heads(dt)
    pb = _dot(xn, wb_ref[...])
    cv = pb[:, D_MODEL:2 * D_MODEL] * pb[:, 2 * D_MODEL:3 * D_MODEL]
    ws = wsc_ref[...]
    uc = cv * ws[SC_CONV - 1:SC_CONV, :]
    for k in range(SC_CONV - 1):
        uc = uc + sst_ref[k] * ws[k:k + 1, :]
    for k in range(SC_CONV - 2):
        snew_ref[k] = sst_ref[k + 1]
    snew_ref[SC_CONV - 2] = cv
    yb_ref[...] = _sigmoid(pb[:, 4 * D_MODEL:5 * D_MODEL]) * (pb[:, 0:D_MODEL] * uc)
    sga_ref[...] = _sigmoid(pb[:, 3 * D_MODEL:4 * D_MODEL])


def _sample_proj(x, mstate_t, sstate_t, w):
    nb = x.shape[0]
    f = lambda *s: jax.ShapeDtypeStruct(s, F32)
    return pl.pallas_call(
        _sproj_body,
        out_shape=[f(nb, SSM_INNER), f(nb, SSM_INNER), f(nb, SSM_INNER), f(nb, SSM_HEADS),
                   f(nb, SSM_GROUPS * SSM_STATE), f(nb, SSM_GROUPS * SSM_STATE),
                   f(nb, D_MODEL), f(nb, D_MODEL),
                   f(SSM_CONV - 1, nb, SSM_CONV_DIM), f(SC_CONV - 1, nb, D_MODEL)],
        compiler_params=pltpu.CompilerParams(vmem_limit_bytes=VMEM_LIMIT),
        name="sample_proj",
    )(x, w["norm_mix"], w["w_a"], w["w_dt"], w["w_b"], w["w_mconv"], w["b_mconv"], w["dt_bias"],
      w["a_log"], w["w_sconv"], mstate_t, sstate_t)


def _sstate_body(dec_ref, s_ref, dtx_ref, bm_ref, cm_ref, snew_ref, y_ref):
    i = pl.program_id(0)
    rows_per_blk = LANES
    for j in range(STATE_BB):
        b = i * STATE_BB + j
        dtx_row = dtx_ref[j:j + 1, :]
        y_parts = []
        for g in range(SSM_GROUPS):
            b_row = bm_ref[j:j + 1, g * SSM_STATE:(g + 1) * SSM_STATE]
            c_row = cm_ref[j:j + 1, g * SSM_STATE:(g + 1) * SSM_STATE].astype(BF16)
            new_blocks = []
            for q in range(GROUP_WIDTH // rows_per_blk):
                r0 = g * GROUP_WIDTH + q * rows_per_blk
                dcol = jnp.broadcast_to(dtx_row[:, r0:r0 + rows_per_blk], (rows_per_blk, LANES)).T
                sub = []
                for hh in range(rows_per_blk // SSM_HEAD_DIM):
                    h = r0 // SSM_HEAD_DIM + hh
                    lo = hh * SSM_HEAD_DIM
                    s_old = s_ref[j, r0 + lo:r0 + lo + SSM_HEAD_DIM, :]
                    sub.append(s_old * dec_ref[b, h] + dcol[lo:lo + SSM_HEAD_DIM, :] * b_row)
                blk = jnp.concatenate(sub, axis=0)
                snew_ref[j, r0:r0 + rows_per_blk, :] = blk
                new_blocks.append(blk.astype(BF16))
            s_g = jnp.concatenate(new_blocks, axis=0)
            y_parts.append(_dot_nt(c_row, s_g))
        y_ref[j:j + 1, :] = jnp.concatenate(y_parts, axis=1)


def _sample_state(dec, state, dtx, bm, cm):
    nb = state.shape[0]
    row = lambda wdt: pl.BlockSpec((STATE_BB, wdt), lambda i, dec: (i, 0))
    st = pl.BlockSpec((STATE_BB, SSM_INNER, SSM_STATE), lambda i, dec: (i, 0, 0))
    return pl.pallas_call(
        _sstate_body,
        grid_spec=pltpu.PrefetchScalarGridSpec(
            num_scalar_prefetch=1, grid=(nb // STATE_BB,),
            in_specs=[st, row(SSM_INNER), row(SSM_GROUPS * SSM_STATE), row(SSM_GROUPS * SSM_STATE)],
            out_specs=[st, row(SSM_INNER)]),
        out_shape=[jax.ShapeDtypeStruct(state.shape, F32), jax.ShapeDtypeStruct((nb, SSM_INNER), F32)],
        compiler_params=_cparams("arbitrary"),
        name="sample_state",
    )(dec, state, dtx, bm, cm)


def _sfin1_body(x_ref, y_ref, xs_ref, z_ref, yb_ref, sga_ref, dskip_ref, gssm_ref, wout_ref,
                gx_ref, wq_ref, h_ref, q_ref):
    y = y_ref[...] + dskip_ref[...] * xs_ref[...]
    y_a = _group_rmsnorm(y * _silu(z_ref[...]), gssm_ref[...])
    merged = sga_ref[...] * y_a + yb_ref[...]
    h = x_ref[...] + _dot(merged.astype(BF16), wout_ref[...])
    h_ref[...] = h
    q_ref[...] = _dot(_rms(h, gx_ref[...]).astype(BF16), wq_ref[...])


def _sample_fin1(x, y, xs, z, yb, sga, w):
    nb = x.shape[0]
    return pl.pallas_call(
        _sfin1_body,
        out_shape=[jax.ShapeDtypeStruct((nb, D_MODEL), F32)] * 2,
        compiler_params=pltpu.CompilerParams(vmem_limit_bytes=VMEM_LIMIT),
        name="sample_fin1",
    )(x, y, xs, z, yb, sga, w["d_skip"], w["norm_ssm"], w["w_out"], w["norm_xattn"], w["w_xq"])


def _sattn_body(q_ref, k_ref, v_ref, o_ref):
    for j in range(ATTN_BB):
        q_row = q_ref[j]
        q4 = jnp.concatenate([q_row[:, h * XA_HEAD_DIM:(h + 1) * XA_HEAD_DIM]
                              for h in range(XA_HEADS)], axis=0)
        s = jnp.sum(k_ref[j] * q4[None], axis=-1, keepdims=True) * (XA_HEAD_DIM ** -0.5)
        e = jnp.exp(s - jnp.max(s, axis=0, keepdims=True))
        p = e / jnp.sum(e, axis=0, keepdims=True)
        o4 = jnp.sum(p * v_ref[j], axis=0)
        o_ref[j] = jnp.concatenate([o4[h:h + 1, :] for h in range(XA_HEADS)], axis=1)


def _sample_attn(q3, k3, v3):
    nb = q3.shape[0]
    qs = pl.BlockSpec((ATTN_BB, 1, D_MODEL), lambda i: (i, 0, 0))
    kv = pl.BlockSpec((ATTN_BB, N_MEM, XA_HEADS, XA_HEAD_DIM), lambda i: (i, 0, 0, 0))
    return pl.pallas_call(
        _sattn_body,
        grid=(nb // ATTN_BB,),
        in_specs=[qs, kv, kv],
        out_specs=qs,
        out_shape=jax.ShapeDtypeStruct((nb, 1, D_MODEL), F32),
        compiler_params=_cparams("arbitrary"),
        name="sample_attn",
    )(q3, k3, v3)


def _sfin2_body(h_ref, o_ref, wo_ref, gmoe_ref, wr_ref, br_ref, h2_ref, hn_ref, lg_ref):
    h2 = h_ref[...] + _dot(o_ref[...].astype(BF16), wo_ref[...])
    _router_tail(h2, gmoe_ref, wr_ref, br_ref, h2_ref, hn_ref, lg_ref)


def _sample_fin2(h1, o, w):
    nb = h1.shape[0]
    return pl.pallas_call(
        _sfin2_body,
        out_shape=[jax.ShapeDtypeStruct((nb, D_MODEL), F32)] * 2
        + [jax.ShapeDtypeStruct((N_EXPERTS, nb), F32)],
        compiler_params=pltpu.CompilerParams(vmem_limit_bytes=VMEM_LIMIT),
        name="sample_fin2",
    )(h1, o, w["w_xo"], w["norm_moe"], w["w_router"], w["b_router"])


def _pad_cols(x, cols):
    return jnp.concatenate([x, jnp.zeros((x.shape[0], cols - x.shape[1]), x.dtype)], axis=1)


def _route_body(lgp_ref, lgs_ref, g_ref, loc_ref, cnt_ref, off_ref):
    tt = lgp_ref.shape[1]
    is_sample = pl.program_id(0) == pl.num_programs(0) - 1
    col = lax.broadcasted_iota(I32, (1, tt), 1)
    valid = jnp.logical_or(jnp.logical_not(is_sample), col < lgs_ref.shape[1])
    work = jnp.where(is_sample, _pad_cols(lgs_ref[...], tt), lgp_ref[...])
    sub = lax.broadcasted_iota(I32, (N_EXPERTS, tt), 0).astype(F32)
    vals, hots = [], []
    for _ in range(TOP_K):
        m = jnp.max(work, axis=0, keepdims=True)
        idx = jnp.min(jnp.where(work == m, sub, float(N_EXPERTS)), axis=0, keepdims=True)
        hot = (sub == idx) & valid
        vals.append(m)
        hots.append(hot)
        work = jnp.where(hot, -jnp.inf, work)
    exps = [jnp.exp(v - vals[0]) for v in vals]
    tot = exps[0]
    for e in exps[1:]:
        tot = tot + e
    assigned = hots[0]
    for hot in hots[1:]:
        assigned = assigned | hot
    a = assigned.astype(BF16)
    r_i = lax.broadcasted_iota(I32, (tt, tt), 0)
    c_i = lax.broadcasted_iota(I32, (tt, tt), 1)
    rank = _dot(a, (r_i < c_i).astype(BF16))
    cnt = jnp.sum(a.astype(F32), axis=1, keepdims=True)
    cnt = jnp.floor((cnt + (SUBLANES - 1)) * (1.0 / SUBLANES)) * SUBLANES
    e_r = lax.broadcasted_iota(I32, (N_EXPERTS, N_EXPERTS), 0)
    e_c = lax.broadcasted_iota(I32, (N_EXPERTS, N_EXPERTS), 1)
    cnt_cols = jnp.broadcast_to(cnt, (N_EXPERTS, LANES)).astype(BF16)
    off = _dot((e_r > e_c).astype(BF16), cnt_cols)[:, 0:1]
    slot = rank + off
    k_sub = lax.broadcasted_iota(I32, (SUBLANES, tt), 0)
    g_out = jnp.zeros((SUBLANES, tt), F32)
    l_out = jnp.full((SUBLANES, tt), -1.0, F32)
    for k in range(TOP_K):
        lk = jnp.sum(jnp.where(hots[k], slot, 0.0), axis=0, keepdims=True)
        g_out = jnp.where(k_sub == k, jnp.where(valid, exps[k] / tot, 0.0), g_out)
        l_out = jnp.where(k_sub == k, jnp.where(valid, lk, -1.0), l_out)
    g_ref[...] = g_out
    loc_ref[...] = l_out.astype(I32)
    cnt_ref[...] = cnt.astype(I32)
    off_ref[...] = off.astype(I32)


def _route(logits_p, logits_s, tt):
    ntp = logits_p.shape[1] // tt
    nt = ntp + 1
    t = nt * tt
    tk = pl.BlockSpec((SUBLANES, tt), lambda i: (0, i))
    per_tile = pl.BlockSpec((None, N_EXPERTS, 1), lambda i: (i, 0, 0))
    return pl.pallas_call(
        _route_body,
        grid=(nt,),
        in_specs=[pl.BlockSpec((N_EXPERTS, tt), lambda i: (0, jnp.minimum(i, ntp - 1))),
                  pl.BlockSpec(logits_s.shape, lambda i: (0, 0))],
        out_specs=[tk, tk, per_tile, per_tile],
        out_shape=[jax.ShapeDtypeStruct((SUBLANES, t), F32), jax.ShapeDtypeStruct((SUBLANES, t), I32),
                   jax.ShapeDtypeStruct((nt, N_EXPERTS, 1), I32), jax.ShapeDtypeStruct((nt, N_EXPERTS, 1), I32)],
        compiler_params=_cparams("arbitrary"),
        name="moe_route",
    )(logits_p, logits_s)


def _sorted_rows(tt):
    return tt * TOP_K + N_EXPERTS * SUBLANES


def _run_copies(tt, tile, cnt_ref, off_ref, base_ref, make_copy, wait):
    if wait:
        total = off_ref[tile, N_EXPERTS - 1] + cnt_ref[tile, N_EXPERTS - 1]

        @pl.when(total > 0)
        def _():
            make_copy(0, 0, pl.multiple_of(total, SUBLANES)).wait()
        return

    def per_expert(e, carry):
        n = cnt_ref[tile, e]

        @pl.when(n > 0)
        def _():
            make_copy(pl.multiple_of(off_ref[tile, e], SUBLANES),
                      pl.multiple_of(base_ref[tile, e], SUBLANES), pl.multiple_of(n, SUBLANES)).start()
        return carry

    lax.fori_loop(0, N_EXPERTS, per_expert, 0)


def _dispatch_body(tm, tt, cnt_ref, off_ref, base_ref, zstart_ref, zsize_ref, loc_ref, x_ref, xs_ref, o_hbm,
                   zero_ref, srt_ref, zsem, sems):
    i = pl.program_id(0)
    last = pl.num_programs(0) - 1
    r = _sorted_rows(tt)

    def zero_copy(j):
        n = pl.multiple_of(zsize_ref[j], SUBLANES)
        dst = o_hbm.at[pl.ds(pl.multiple_of(zstart_ref[j], SUBLANES), n), :]
        return pltpu.make_async_copy(zero_ref.at[pl.ds(0, n), :], dst, zsem)

    @pl.when(i == 0)
    def _():
        zero_ref[...] = jnp.zeros_like(zero_ref)

        def start(j, carry):
            @pl.when(zsize_ref[j] > 0)
            def _():
                zero_copy(j).start()
            return carry

        def wait(j, carry):
            @pl.when(zsize_ref[j] > 0)
            def _():
                zero_copy(j).wait()
            return carry

        lax.fori_loop(0, zstart_ref.shape[0], start, 0)
        lax.fori_loop(0, zstart_ref.shape[0], wait, 0)

    loc = loc_ref[...]
    slot_i = lax.broadcasted_iota(I32, (r, tt), 0)
    hit = slot_i == loc[0:1, :]
    for k in range(1, TOP_K):
        hit = hit | (slot_i == loc[k:k + 1, :])
    buf = i % 2
    x = jnp.where(i == last, _pad_rows(xs_ref[...], tt), x_ref[...])
    srt_ref[buf] = _pack_bf16_pairs(_dot(hit.astype(BF16), x.astype(BF16)), is_bf16_valued=True)

    def copies(tile, wait):
        b = tile % 2

        def make_copy(lo, go, size):
            return pltpu.make_async_copy(srt_ref.at[b, pl.ds(lo, size), :], o_hbm.at[pl.ds(go, size), :],
                                         sems.at[b])

        _run_copies(tt, tile, cnt_ref, off_ref, base_ref, make_copy, wait)

    copies(i, False)

    @pl.when(i > 0)
    def _():
        copies(i - 1, True)

    @pl.when(i == last)
    def _():
        copies(i, True)


def _dispatch(cnt, off, base, zero_starts, zero_sizes, loc, hn_p, hn_s, n_rows, tm, tt):
    ntp = hn_p.shape[0] // tt
    smem = pl.BlockSpec(memory_space=pltpu.SMEM)
    return pl.pallas_call(
        functools.partial(_dispatch_body, tm, tt),
        grid_spec=pltpu.PrefetchScalarGridSpec(
            num_scalar_prefetch=0, grid=(ntp + 1,),
            in_specs=[smem, smem, smem, smem, smem,
                      pl.BlockSpec((SUBLANES, tt), lambda i: (0, i)),
                      pl.BlockSpec((tt, D_MODEL), lambda i: (jnp.minimum(i, ntp - 1), 0)),
                      pl.BlockSpec(hn_s.shape, lambda i: (0, 0))],
            out_specs=pl.BlockSpec(memory_space=pl.ANY),
            scratch_shapes=[pltpu.VMEM((tm, D_MODEL // 2), U32),
                            pltpu.VMEM((2, _sorted_rows(tt), D_MODEL // 2), U32),
                            pltpu.SemaphoreType.DMA, pltpu.SemaphoreType.DMA((2,))]),
        out_shape=jax.ShapeDtypeStruct((n_rows, D_MODEL // 2), U32),
        compiler_params=_cparams("arbitrary"),
        name="moe_dispatch",
    )(cnt, off, base, zero_starts, zero_sizes, loc, hn_p, hn_s)


def _expert_body(be_ref, nu_ref, bv_ref, slot_ref, nxt_ref, x_ref, wgu_hbm, bgu_ref, wdn_hbm, bdn_ref,
                 y_ref, wgu_f, wdn_f, wgu_b, wdn_b, sems):
    i = pl.program_id(0)
    tm = x_ref.shape[0]
    valid = bv_ref[i]
    expert = be_ref[i]
    slot = slot_ref[i]

    def weight_copies(e, s):
        return (pltpu.make_async_copy(wgu_hbm.at[e], wgu_f.at[s], sems.at[0, s]),
                pltpu.make_async_copy(wdn_hbm.at[e], wdn_f.at[s], sems.at[1, s]))

    @pl.when(i == 0)
    def _():
        for cp in weight_copies(expert, slot):
            cp.start()

    first = jnp.logical_and(i < nu_ref[0], jnp.logical_or(i == 0, expert != be_ref[jnp.maximum(i - 1, 0)]))

    @pl.when(first)
    def _():
        for cp in weight_copies(expert, slot):
            cp.wait()

        @pl.when(nxt_ref[i] >= 0)
        def _():
            for cp in weight_copies(nxt_ref[i], 1 - slot):
                cp.start()

    def ffn(rows, cast):
        xb = _unpack_bf16_pairs(x_ref[0:rows, :])
        gu_parts = []
        for j in range(2 * D_FF // EXPERT_COL_CHUNK):
            cs = slice(j * EXPERT_COL_CHUNK, (j + 1) * EXPERT_COL_CHUNK)
            if cast:
                wgu_b[:, cs] = wgu_f[slot, :, cs].astype(BF16)
            gu_parts.append(_dot(xb, wgu_b[:, cs]) + bgu_ref[:, cs])
        gate = jnp.minimum(jnp.concatenate(gu_parts[:len(gu_parts) // 2], axis=1), SWIGLU_LIMIT)
        up = jnp.clip(jnp.concatenate(gu_parts[len(gu_parts) // 2:], axis=1), -SWIGLU_LIMIT, SWIGLU_LIMIT)
        act = ((up + 1.0) * (gate * _sigmoid(SWIGLU_ALPHA * gate))).astype(BF16)
        y_parts = []
        for j in range(D_MODEL // EXPERT_COL_CHUNK):
            cs = slice(j * EXPERT_COL_CHUNK, (j + 1) * EXPERT_COL_CHUNK)
            if cast:
                wdn_b[:, cs] = wdn_f[slot, :, cs].astype(BF16)
            y_parts.append(_dot(act, wdn_b[:, cs]) + bdn_ref[:, cs])
        y_ref[0:rows, :] = _pack_bf16_pairs(jnp.concatenate(y_parts, axis=1))
        if rows < tm:
            y_ref[rows:tm, :] = jnp.zeros((tm - rows, D_MODEL // 2), U32)

    quarter = tm // EXPERT_ROW_SPLITS
    for q in range(1, EXPERT_ROW_SPLITS + 1):
        in_q = jnp.logical_and(valid > (q - 1) * quarter, valid <= q * quarter)
        for cast in (True, False):
            @pl.when(jnp.logical_and(in_q, first == cast))
            def _(q=q, cast=cast):
                ffn(q * quarter, cast)

    @pl.when(valid == 0)
    def _():
        y_ref[...] = jnp.zeros_like(y_ref)


def _experts(block_e, n_used, block_valid, block_slot, block_next, xs, wgu, bgu, wdn, bdn, tm):
    n_rows = xs.shape[0]
    return pl.pallas_call(
        _expert_body,
        grid_spec=pltpu.PrefetchScalarGridSpec(
            num_scalar_prefetch=5, grid=(n_rows // tm,),
            in_specs=[pl.BlockSpec((tm, D_MODEL // 2), lambda i, be, nu, *_: (jnp.minimum(i, nu[0] - 1), 0)),
                      pl.BlockSpec(memory_space=pl.ANY),
                      pl.BlockSpec((None, 1, 2 * D_FF), lambda i, be, *_: (be[i], 0, 0)),
                      pl.BlockSpec(memory_space=pl.ANY),
                      pl.BlockSpec((None, 1, D_MODEL), lambda i, be, *_: (be[i], 0, 0))],
            out_specs=pl.BlockSpec((tm, D_MODEL // 2), lambda i, *_: (i, 0)),
            scratch_shapes=[pltpu.VMEM((2, D_MODEL, 2 * D_FF), F32), pltpu.VMEM((2, D_FF, D_MODEL), F32),
                            pltpu.VMEM((D_MODEL, 2 * D_FF), BF16), pltpu.VMEM((D_FF, D_MODEL), BF16),
                            pltpu.SemaphoreType.DMA((2, 2))]),
        out_shape=jax.ShapeDtypeStruct((n_rows, D_MODEL // 2), U32),
        compiler_params=_cparams("arbitrary"),
        name="moe_experts",
    )(block_e, n_used, block_valid, block_slot, block_next, xs, wgu, bgu, wdn, bdn)


def _combine_body(tt, cnt_ref, off_ref, base_ref, loc_ref, g_ref, h_ref, hs_ref, gfin_ref, ys_hbm,
                  y_ref, ysmp_ref, buf_ref, sems):
    i = pl.program_id(0)
    last = pl.num_programs(0) - 1
    r = _sorted_rows(tt)

    def copies(tile, wait):
        b = tile % 2

        def make_copy(lo, go, size):
            return pltpu.make_async_copy(ys_hbm.at[pl.ds(go, size), :], buf_ref.at[b, pl.ds(lo, size), :],
                                         sems.at[b])

        _run_copies(tt, tile, cnt_ref, off_ref, base_ref, make_copy, wait)

    @pl.when(i == 0)
    def _():
        buf_ref[...] = jnp.zeros_like(buf_ref)
        copies(0, False)

    @pl.when(i < last)
    def _():
        copies(i + 1, False)

    copies(i, True)
    loc = loc_ref[...]
    gates = g_ref[...]
    slot_i = lax.broadcasted_iota(I32, (r, tt), 0)
    gmat = jnp.where(slot_i == loc[0:1, :], gates[0:1, :], 0.0)
    for k in range(1, TOP_K):
        gmat = gmat + jnp.where(slot_i == loc[k:k + 1, :], gates[k:k + 1, :], 0.0)
    h = jnp.where(i == last, _pad_rows(hs_ref[...], tt), h_ref[...])
    moe = lax.dot_general(gmat.astype(BF16), _unpack_bf16_pairs(buf_ref[i % 2]), TN_DIMS,
                          preferred_element_type=F32)
    y = _rms(h + moe, gfin_ref[...])

    @pl.when(i < last)
    def _():
        y_ref[...] = y

    @pl.when(i == last)
    def _():
        ysmp_ref[...] = y[0:ysmp_ref.shape[0], :]


def _combine(cnt, off, base, loc, gates, h2_p, h2_s, norm_final, ys, tt):
    ntp = h2_p.shape[0] // tt
    tok = pl.BlockSpec((tt, D_MODEL), lambda i: (jnp.minimum(i, ntp - 1), 0))
    smp = pl.BlockSpec(h2_s.shape, lambda i: (0, 0))
    tk = pl.BlockSpec((SUBLANES, tt), lambda i: (0, i))
    smem = pl.BlockSpec(memory_space=pltpu.SMEM)
    return pl.pallas_call(
        functools.partial(_combine_body, tt),
        grid_spec=pltpu.PrefetchScalarGridSpec(
            num_scalar_prefetch=0, grid=(ntp + 1,),
            in_specs=[smem, smem, smem, tk, tk, tok, smp,
                      pl.BlockSpec((1, D_MODEL), lambda i: (0, 0)),
                      pl.BlockSpec(memory_space=pl.ANY)],
            out_specs=[tok, smp],
            scratch_shapes=[pltpu.VMEM((2, _sorted_rows(tt), D_MODEL // 2), U32), pltpu.SemaphoreType.DMA((2,))]),
        out_shape=[jax.ShapeDtypeStruct(h2_p.shape, F32), jax.ShapeDtypeStruct(h2_s.shape, F32)],
        compiler_params=_cparams("arbitrary"),
        name="moe_combine",
    )(cnt, off, base, loc, gates, h2_p, h2_s, norm_final, ys)


def _moe_and_final_norm(hn_p, logits_p, h2_p, hn_s, logits_s, h2_s, w, tt, tm):
    nt = hn_p.shape[0] // tt + 1
    t = hn_p.shape[0] + hn_s.shape[0]
    gates, loc, cnt3, off3 = _route(logits_p, logits_s, tt)
    cnt = cnt3[:, :, 0]
    counts = jnp.sum(cnt, axis=0)
    padded = (counts + tm - 1) // tm * tm
    pad_end = jnp.cumsum(padded)
    start = pad_end - padded
    off = off3[:, :, 0]
    base = (start[None, :] + jnp.cumsum(cnt, axis=0) - cnt).astype(I32)
    n_blocks = (t * TOP_K + nt * N_EXPERTS * (SUBLANES - 1) + N_EXPERTS * (tm - 1)) // tm
    n_rows = n_blocks * tm
    block_start = jnp.arange(n_blocks, dtype=I32) * tm
    block_e = jnp.minimum(jnp.sum(block_start[:, None] >= pad_end[None, :], axis=-1), N_EXPERTS - 1).astype(I32)
    n_used = (pad_end[-1:] // tm).astype(I32)
    zero_starts = jnp.concatenate([start + counts, block_start]).astype(I32)
    zero_sizes = jnp.concatenate([padded - counts,
                                  jnp.where(block_start >= pad_end[-1], tm, 0)]).astype(I32)
    xs = _dispatch(cnt, off, base, zero_starts, zero_sizes, loc, hn_p, hn_s, n_rows, tm, tt)
    e_ids = jnp.arange(N_EXPERTS, dtype=I32)
    block_hot = block_e[:, None] == e_ids[None, :]

    def per_block(table):
        return jnp.sum(jnp.where(block_hot, table[None, :], 0), axis=1).astype(I32)

    block_valid = jnp.clip(per_block(start + counts) - block_start, 0, tm)
    block_valid = jnp.where(block_start < pad_end[-1], block_valid, 0).astype(I32)
    present = padded > 0
    later = present[None, :] & (e_ids[None, :] > e_ids[:, None])
    next_e = jnp.min(jnp.where(later, e_ids[None, :], N_EXPERTS), axis=1)
    next_e = jnp.where(next_e < N_EXPERTS, next_e, -1).astype(I32)
    before = present[None, :] & (e_ids[None, :] < e_ids[:, None])
    run_slot = (jnp.sum(before.astype(I32), axis=1) % 2).astype(I32)
    ys = _experts(block_e, n_used, block_valid, per_block(run_slot), per_block(next_e), xs, w["w_gate_up"], w["b_gate_up"], w["w_down"], w["b_down"], tm)
    return _combine(cnt, off, base, loc, gates, h2_p, h2_s, w["norm_final"], ys, tt)


def kernel(x_prompt, x_sample, mem_prompt, state_ssm, state_mamba_conv, state_short_conv, cache_mem_k, cache_mem_v, norm_mix, w_in, w_mconv, b_mconv, dt_bias, a_log, d_skip, norm_ssm, w_sconv, w_out, norm_xattn, norm_mem, w_xq, w_xk, w_xv, w_xo, norm_moe, w_router, b_router, w_gate_up, b_gate_up, w_down, b_down, norm_final):
    nbp, seq, _ = x_prompt.shape
    nbs = x_sample.shape[0]
    dt_lo = SSM_INNER + SSM_CONV_DIM
    w_in0 = w_in[0]
    w_dt = w_in0[:, dt_lo:dt_lo + SSM_HEADS]
    w = {
        "norm_mix": norm_mix, "norm_ssm": norm_ssm, "norm_xattn": norm_xattn, "norm_moe": norm_moe,
        "norm_final": norm_final.reshape(1, D_MODEL),
        "w_a": w_in0[:, :dt_lo].astype(BF16),
        "w_dt": w_dt.astype(BF16), "w_dt_t": w_dt.T.astype(BF16),
        "w_b": w_in0[:, dt_lo + SSM_HEADS:].astype(BF16),
        "w_mconv": w_mconv[0], "b_mconv": b_mconv,
        "dt_bias": dt_bias, "dt_bias_t": dt_bias.reshape(SSM_HEADS, 1),
        "a_log": a_log, "a_log_t": a_log.reshape(SSM_HEADS, 1),
        "d_skip": jnp.repeat(d_skip, SSM_HEAD_DIM, axis=1),
        "w_sconv": w_sconv[0], "w_out": w_out[0].astype(BF16),
        "w_xq": w_xq[0].astype(BF16), "w_xo": w_xo[0].astype(BF16),
        "w_router": w_router[0].T.astype(BF16), "b_router": b_router.reshape(N_EXPERTS, 1),
        "w_gate_up": w_gate_up[0], "b_gate_up": b_gate_up[0].reshape(N_EXPERTS, 1, 2 * D_FF),
        "w_down": w_down[0], "b_down": b_down[0].reshape(N_EXPERTS, 1, D_MODEL),
    }

    k_p, v_p, kb, vb = _mem_kv(mem_prompt.reshape(nbp * N_MEM, D_MODEL), norm_mem,
                               w_xk[0].astype(BF16), w_xv[0].astype(BF16))
    h1, ssm_p, mconv_p, sconv_p = _prompt_mixer(x_prompt.reshape(nbp * seq, D_MODEL), nbp, w)
    h2, hn, logits = _prompt_attn(h1, kb, vb, nbp, w)

    xs2 = x_sample.reshape(nbs, D_MODEL)
    mstate_t = jnp.transpose(state_mamba_conv[0], (1, 0, 2))
    sstate_t = jnp.transpose(state_short_conv[0], (1, 0, 2))
    z, xs_, dtx, dec, bm, cm, yb, sga, mnew_t, snew_t = _sample_proj(xs2, mstate_t, sstate_t, w)
    ssm_s, y_s = _sample_state(dec, state_ssm[0].reshape(nbs, SSM_INNER, SSM_STATE), dtx, bm, cm)
    h1s, q_s = _sample_fin1(xs2, y_s, xs_, z, yb, sga, w)
    o_s = _sample_attn(q_s.reshape(nbs, 1, D_MODEL),
                       cache_mem_k[0], cache_mem_v[0])
    h2s, hns, logits_s = _sample_fin2(h1s, o_s.reshape(nbs, D_MODEL), w)
    y_prompt, y_sample = _moe_and_final_norm(hn, logits, h2, hns, logits_s, h2s, w, MIX_TILE, MOE_ROW_TILE)

    return (y_prompt.reshape(nbp, seq, D_MODEL),
            y_sample.reshape(nbs, 1, D_MODEL),
            ssm_p.reshape(1, nbp, SSM_HEADS, SSM_HEAD_DIM, SSM_STATE),
            mconv_p[None], sconv_p[None],
            k_p.reshape(1, nbp, N_MEM, XA_HEADS, XA_HEAD_DIM),
            v_p.reshape(1, nbp, N_MEM, XA_HEADS, XA_HEAD_DIM),
            ssm_s.reshape(1, nbs, SSM_HEADS, SSM_HEAD_DIM, SSM_STATE),
            jnp.transpose(mnew_t, (1, 0, 2))[None],
            jnp.transpose(snew_t, (1, 0, 2))[None])
```

```python
import functools

import jax
import jax.numpy as jnp
from jax import lax
from jax.experimental import pallas as pl
from jax.experimental.pallas import tpu as pltpu

F32 = jnp.float32
BF16 = jnp.bfloat16
I32 = jnp.int32
U32 = jnp.uint32

D_MODEL = 1024
N_MEM = 256
SSM_HEADS = 16
SSM_HEAD_DIM = 64
SSM_INNER = SSM_HEADS * SSM_HEAD_DIM
SSM_STATE = 128
SSM_GROUPS = 4
HEADS_PER_GROUP = SSM_HEADS // SSM_GROUPS
GROUP_WIDTH = SSM_INNER // SSM_GROUPS
SSM_CONV = 4
SSM_CONV_DIM = SSM_INNER + 2 * SSM_GROUPS * SSM_STATE
SC_CONV = 3
XA_HEADS = 4
XA_HEAD_DIM = D_MODEL // XA_HEADS
N_EXPERTS = 32
TOP_K = 4
D_FF = D_MODEL
SWIGLU_LIMIT = 7.0
SWIGLU_ALPHA = 1.702
EPS = 1e-6

LANES = 128
SUBLANES = 8
VMEM_LIMIT = 56 * 1024 * 1024

MIX_TILE = 256
MOE_ROW_TILE = 512
RUN_COPY_UNROLL = 4
EXPERT_ROW_SPLITS = 4
EXPERT_COL_CHUNK = 512
STATE_BB = 8
ATTN_BB = 4

NT_DIMS = (((1,), (1,)), ((), ()))
TN_DIMS = (((0,), (0,)), ((), ()))


def _cparams(*sem):
    return pltpu.CompilerParams(dimension_semantics=sem, vmem_limit_bytes=VMEM_LIMIT)


def _const_spec(shape):
    nd = len(shape)
    return pl.BlockSpec(shape, lambda *_: (0,) * nd, pipeline_mode=pl.Buffered(1))


def _sigmoid(x):
    return 1.0 / (1.0 + jnp.exp(-x))


def _silu(x):
    return x * _sigmoid(x)


def _softplus(x):
    return jnp.maximum(x, 0.0) + jnp.log(1.0 + jnp.exp(-jnp.abs(x)))


def _rms(x, g):
    ms = jnp.mean(x * x, axis=-1, keepdims=True)
    return x * lax.rsqrt(ms + EPS) * g


def _dot(a, b):
    return jnp.dot(a, b, preferred_element_type=F32)


def _dot_nt(a, b):
    return lax.dot_general(a, b, NT_DIMS, preferred_element_type=F32)


def _expand_heads(v):
    rows = v.shape[0]
    lane = lax.broadcasted_iota(I32, (rows, LANES), 1)
    pieces = []
    for j in range(SSM_HEADS // 2):
        a = jnp.broadcast_to(v[:, 2 * j:2 * j + 1], (rows, LANES))
        b = jnp.broadcast_to(v[:, 2 * j + 1:2 * j + 2], (rows, LANES))
        pieces.append(jnp.where(lane < SSM_HEAD_DIM, a, b))
    return jnp.concatenate(pieces, axis=1)


def _cumsum(x, axis):
    idx = lax.broadcasted_iota(I32, x.shape, axis)
    shift = 1
    while shift < x.shape[axis]:
        x = x + jnp.where(idx >= shift, pltpu.roll(x, shift, axis), 0.0)
        shift *= 2
    return x


def _pack_bf16_pairs(x, is_bf16_valued=False):
    w = x.shape[1] // 2
    if not is_bf16_valued:
        x = x.astype(BF16).astype(F32)
    bits = lax.bitcast_convert_type(x, U32)
    return (bits[:, w:] & jnp.uint32(0xFFFF0000)) | (bits[:, :w] >> 16)


def _unpack_bf16_pairs(p):
    lo = lax.bitcast_convert_type(p << 16, F32)
    hi = lax.bitcast_convert_type(p & jnp.uint32(0xFFFF0000), F32)
    return jnp.concatenate([lo, hi], axis=1).astype(BF16)


def _pad_rows(x, rows):
    return jnp.concatenate([x, jnp.zeros((rows - x.shape[0], x.shape[1]), x.dtype)], axis=0)


def _group_rmsnorm(u, g):
    outs = []
    for k in range(SSM_GROUPS):
        ug = u[:, k * GROUP_WIDTH:(k + 1) * GROUP_WIDTH]
        ms = jnp.mean(ug * ug, axis=-1, keepdims=True)
        outs.append(ug * lax.rsqrt(ms + EPS))
    return jnp.concatenate(outs, axis=1) * g


def _memkv_body(mem_ref, g_ref, wk_ref, wv_ref, k_ref, v_ref, kb_ref, vb_ref):
    mn = _rms(mem_ref[...], g_ref[...]).astype(BF16)
    k = _dot(mn, wk_ref[...])
    v = _dot(mn, wv_ref[...])
    k_ref[...] = k
    v_ref[...] = v
    kb_ref[...] = k.astype(BF16)
    vb_ref[...] = v.astype(BF16)


def _mem_kv(mem2d, norm_mem, wk, wv):
    rows = mem2d.shape[0]
    nb = rows // N_MEM
    blk = pl.BlockSpec((N_MEM, D_MODEL), lambda b: (b, 0))
    return pl.pallas_call(
        _memkv_body,
        grid=(nb,),
        in_specs=[blk, _const_spec((1, D_MODEL)), _const_spec((D_MODEL, D_MODEL)),
                  _const_spec((D_MODEL, D_MODEL))],
        out_specs=[blk, blk, blk, blk],
        out_shape=[jax.ShapeDtypeStruct((rows, D_MODEL), F32)] * 2
        + [jax.ShapeDtypeStruct((rows, D_MODEL), BF16)] * 2,
        compiler_params=_cparams("arbitrary"),
        name="mem_kv",
    )(mem2d, norm_mem, wk, wv)


def _mix_body(x_ref, gmix_ref, wa_ref, wdtc_ref, wdtr_ref, wb_ref, wmc_ref, bmc_ref,
              dtb_ref, dtbt_ref, alog_ref, alogt_ref, dskip_ref, gssm_ref, wsc_ref, wout_ref,
              h_ref, ssm_ref, mbuf_ref, sbuf_ref,
              st_ref, cbuf_ref, scbuf_ref):
    tq = MIX_TILE
    c = pl.program_id(1)

    @pl.when(c == 0)
    def _():
        st_ref[...] = jnp.zeros_like(st_ref)
        cbuf_ref[0:SUBLANES, :] = jnp.zeros((SUBLANES, SSM_CONV_DIM), F32)
        scbuf_ref[0:SUBLANES, :] = jnp.zeros((SUBLANES, D_MODEL), F32)

    x = x_ref[...]
    xn = _rms(x, gmix_ref[...]).astype(BF16)

    u = _dot(xn, wa_ref[:, SSM_INNER:])
    cbuf_ref[SUBLANES:SUBLANES + tq, :] = u
    wm = wmc_ref[...]
    conv = u * wm[SSM_CONV - 1:SSM_CONV, :] + bmc_ref[...]
    for k in range(SSM_CONV - 1):
        off = SUBLANES - (SSM_CONV - 1) + k
        conv = conv + cbuf_ref[off:off + tq, :] * wm[k:k + 1, :]
    tail = cbuf_ref[tq + SUBLANES - (SSM_CONV - 1):tq + SUBLANES, :]
    mbuf_ref[...] = tail
    cbuf_ref[SUBLANES - (SSM_CONV - 1):SUBLANES, :] = tail
    xbc = _silu(conv)
    xs = xbc[:, :SSM_INNER]
    bm = xbc[:, SSM_INNER:SSM_INNER + SSM_GROUPS * SSM_STATE]
    cm = xbc[:, SSM_INNER + SSM_GROUPS * SSM_STATE:]

    dt = _softplus(_dot(xn, wdtc_ref[...]) + dtb_ref[...])
    dtt = _softplus(_dot_nt(wdtr_ref[...], xn) + dtbt_ref[...])
    a_row = -jnp.exp(alog_ref[...])
    a_col = -jnp.exp(alogt_ref[...])
    row_i = lax.broadcasted_iota(I32, (tq, tq), 0)
    col_i = lax.broadcasted_iota(I32, (tq, tq), 1)
    causal = row_i >= col_i
    a_cum = _cumsum(dt * a_row, 0)
    a_cumt = _cumsum(dtt * a_col, 1)
    a_last = a_cum[tq - 1:tq, :]

    xdt = xs * _expand_heads(dt)
    in_decay = _expand_heads(jnp.exp(a_cum))
    to_end = _expand_heads(jnp.exp(a_last - a_cum))
    chunk_decay = _expand_heads(jnp.exp(a_last))
    xdt_b = xdt.astype(BF16)
    xend_b = (xdt * to_end).astype(BF16)
    lane = lax.broadcasted_iota(I32, (tq, LANES), 1)

    def proj_b(k):
        return _dot(xn, wb_ref[:, k * D_MODEL:(k + 1) * D_MODEL])

    pb = []
    y_groups = []
    for g in range(SSM_GROUPS):
        pb.append(proj_b(g))
        if g == 0:
            z = _dot(xn, wa_ref[:, :SSM_INNER])
        if g == 2:
            g_b = proj_b(SSM_GROUPS)
        cg = cm[:, g * SSM_STATE:(g + 1) * SSM_STATE].astype(BF16)
        bg_f = bm[:, g * SSM_STATE:(g + 1) * SSM_STATE]
        bg = bg_f.astype(BF16)
        scores = _dot_nt(cg, bg)
        gs = slice(g * GROUP_WIDTH, (g + 1) * GROUP_WIDTH)
        st_g = st_ref[:, gs]
        y_off = _dot(cg, st_g.astype(BF16)) * in_decay[:, gs]
        pair_out = []
        for pr in range(HEADS_PER_GROUP // 2):
            h0 = g * HEADS_PER_GROUP + 2 * pr
            xp = xdt_b[:, h0 * SSM_HEAD_DIM:(h0 + 2) * SSM_HEAD_DIM]
            ys = []
            for h in (h0, h0 + 1):
                seg = a_cum[:, h:h + 1] - a_cumt[h:h + 1, :]
                decay = jnp.where(causal, jnp.exp(jnp.minimum(seg, 0.0)), 0.0)
                ys.append(_dot((scores * decay).astype(BF16), xp))
            pair_out.append(jnp.where(lane < SSM_HEAD_DIM, ys[0], ys[1]))
        y_groups.append(jnp.concatenate(pair_out, axis=1) + y_off)
        st_ref[:, gs] = st_g * chunk_decay[:, gs] + _dot(bg_f.T.astype(BF16), xend_b[:, gs])
    y = jnp.concatenate(y_groups, axis=1) + dskip_ref[...] * xs
    y_a = _group_rmsnorm(y * _silu(z), gssm_ref[...])

    sc_b, sc_c, sc_v, g_a = pb
    cv = sc_c * sc_v
    scbuf_ref[SUBLANES:SUBLANES + tq, :] = cv
    ws = wsc_ref[...]
    uc = cv * ws[SC_CONV - 1:SC_CONV, :]
    for k in range(SC_CONV - 1):
        off = SUBLANES - (SC_CONV - 1) + k
        uc = uc + scbuf_ref[off:off + tq, :] * ws[k:k + 1, :]
    stail = scbuf_ref[tq + SUBLANES - (SC_CONV - 1):tq + SUBLANES, :]
    sbuf_ref[...] = stail
    scbuf_ref[SUBLANES - (SC_CONV - 1):SUBLANES, :] = stail
    merged = _sigmoid(g_a) * y_a + _sigmoid(g_b) * (sc_b * uc)
    h_ref[...] = x + _dot(merged.astype(BF16), wout_ref[...])

    @pl.when(c == pl.num_programs(1) - 1)
    def _():
        ssm_ref[...] = st_ref[...].T


def _prompt_mixer(x2d, nb, w):
    t = x2d.shape[0]
    nc = t // nb // MIX_TILE
    tok = pl.BlockSpec((MIX_TILE, D_MODEL), lambda b, c: (b * nc + c, 0))
    return pl.pallas_call(
        _mix_body,
        grid=(nb, nc),
        in_specs=[tok, _const_spec((1, D_MODEL)),
                  _const_spec((D_MODEL, SSM_INNER + SSM_CONV_DIM)),
                  _const_spec((D_MODEL, SSM_HEADS)), _const_spec((SSM_HEADS, D_MODEL)),
                  _const_spec((D_MODEL, 5 * D_MODEL)),
                  _const_spec((SSM_CONV, SSM_CONV_DIM)), _const_spec((1, SSM_CONV_DIM)),
                  _const_spec((1, SSM_HEADS)), _const_spec((SSM_HEADS, 1)),
                  _const_spec((1, SSM_HEADS)), _const_spec((SSM_HEADS, 1)),
                  _const_spec((1, SSM_INNER)), _const_spec((1, SSM_INNER)),
                  _const_spec((SC_CONV, D_MODEL)), _const_spec((D_MODEL, D_MODEL))],
        out_specs=[tok,
                   pl.BlockSpec((None, SSM_INNER, SSM_STATE), lambda b, c: (b, 0, 0)),
                   pl.BlockSpec((None, SSM_CONV - 1, SSM_CONV_DIM), lambda b, c: (b, 0, 0)),
                   pl.BlockSpec((None, SC_CONV - 1, D_MODEL), lambda b, c: (b, 0, 0))],
        out_shape=[jax.ShapeDtypeStruct((t, D_MODEL), F32),
                   jax.ShapeDtypeStruct((nb, SSM_INNER, SSM_STATE), F32),
                   jax.ShapeDtypeStruct((nb, SSM_CONV - 1, SSM_CONV_DIM), F32),
                   jax.ShapeDtypeStruct((nb, SC_CONV - 1, D_MODEL), F32)],
        scratch_shapes=[pltpu.VMEM((SSM_STATE, SSM_INNER), F32),
                        pltpu.VMEM((MIX_TILE + SUBLANES, SSM_CONV_DIM), F32),
                        pltpu.VMEM((MIX_TILE + SUBLANES, D_MODEL), F32)],
        compiler_params=_cparams("arbitrary", "arbitrary"),
        name="prompt_mixer",
    )(x2d, w["norm_mix"], w["w_a"], w["w_dt"], w["w_dt_t"], w["w_b"], w["w_mconv"], w["b_mconv"],
      w["dt_bias"], w["dt_bias_t"], w["a_log"], w["a_log_t"], w["d_skip"], w["norm_ssm"],
      w["w_sconv"], w["w_out"])


def _router_tail(h2, gmoe_ref, wr_ref, br_ref, h2_ref, hn_ref, lg_ref):
    h2_ref[...] = h2
    hn = _rms(h2, gmoe_ref[...])
    hn_ref[...] = hn
    lg_ref[...] = _dot_nt(wr_ref[...], hn.astype(BF16)) + br_ref[...]


def _attn_body(h_ref, gx_ref, wq_ref, k_ref, v_ref, wo_ref, gmoe_ref, wr_ref, br_ref,
               h2_ref, hn_ref, lg_ref):
    h = h_ref[...]
    hn = _rms(h, gx_ref[...]).astype(BF16)
    q = _dot(hn, wq_ref[...]).astype(BF16)
    outs = []
    for hd in range(XA_HEADS):
        sl = slice(hd * XA_HEAD_DIM, (hd + 1) * XA_HEAD_DIM)
        s = _dot_nt(q[:, sl], k_ref[:, sl]) * (XA_HEAD_DIM ** -0.5)
        e = jnp.exp(s - jnp.max(s, axis=-1, keepdims=True))
        p = e / jnp.sum(e, axis=-1, keepdims=True)
        outs.append(_dot(p.astype(BF16), v_ref[:, sl]))
    o = jnp.concatenate(outs, axis=1).astype(BF16)
    h2 = h + _dot(o, wo_ref[...])
    _router_tail(h2, gmoe_ref, wr_ref, br_ref, h2_ref, hn_ref, lg_ref)


def _prompt_attn(h1, kb, vb, nb, w):
    t = h1.shape[0]
    nc = t // nb // MIX_TILE
    tok = pl.BlockSpec((MIX_TILE, D_MODEL), lambda b, c: (b * nc + c, 0))
    kv = pl.BlockSpec((N_MEM, D_MODEL), lambda b, c: (b, 0))
    return pl.pallas_call(
        _attn_body,
        grid=(nb, nc),
        in_specs=[tok, _const_spec((1, D_MODEL)), _const_spec((D_MODEL, D_MODEL)), kv, kv,
                  _const_spec((D_MODEL, D_MODEL)), _const_spec((1, D_MODEL)),
                  _const_spec((N_EXPERTS, D_MODEL)), _const_spec((N_EXPERTS, 1))],
        out_specs=[tok, tok, pl.BlockSpec((N_EXPERTS, MIX_TILE), lambda b, c: (0, b * nc + c))],
        out_shape=[jax.ShapeDtypeStruct((t, D_MODEL), F32),
                   jax.ShapeDtypeStruct((t, D_MODEL), F32),
                   jax.ShapeDtypeStruct((N_EXPERTS, t), F32)],
        compiler_params=_cparams("arbitrary", "arbitrary"),
        name="prompt_attn",
    )(h1, w["norm_xattn"], w["w_xq"], kb, vb, w["w_xo"], w["norm_moe"], w["w_router"], w["b_router"])


def _sproj_body(x_ref, gmix_ref, wa_ref, wdtc_ref, wb_ref, wmc_ref, bmc_ref, dtb_ref, alog_ref,
                wsc_ref, mst_ref, sst_ref,
                z_ref, xs_ref, dtx_ref, dec_ref, bm_ref, cm_ref, yb_ref, sga_ref, mnew_ref, snew_ref):
    x = x_ref[...]
    xn = _rms(x, gmix_ref[...]).astype(BF16)
    pa = _dot(xn, wa_ref[...])
    z_ref[...] = pa[:, :SSM_INNER]
    u = pa[:, SSM_INNER:]
    wm = wmc_ref[...]
    conv = u * wm[SSM_CONV - 1:SSM_CONV, :] + bmc_ref[...]
    for k in range(SSM_CONV - 1):
        conv = conv + mst_ref[k] * wm[k:k + 1, :]
    for k in range(SSM_CONV - 2):
        mnew_ref[k] = mst_ref[k + 1]
    mnew_ref[SSM_CONV - 2] = u
    xbc = _silu(conv)
    xs = xbc[:, :SSM_INNER]
    xs_ref[...] = xs
    bm_ref[...] = xbc[:, SSM_INNER:SSM_INNER + SSM_GROUPS * SSM_STATE]
    cm_ref[...] = xbc[:, SSM_INNER + SSM_GROUPS * SSM_STATE:]
    dt = _softplus(_dot(xn, wdtc_ref[...]) + dtb_ref[...])
    dec_ref[...] = jnp.exp(dt * (-jnp.exp(alog_ref[...])))
    dtx_ref[...] = xs * _expand_heads(dt)
    pb = _dot(xn, wb_ref[...])
    cv = pb[:, D_MODEL:2 * D_MODEL] * pb[:, 2 * D_MODEL:3 * D_MODEL]
    ws = wsc_ref[...]
    uc = cv * ws[SC_CONV - 1:SC_CONV, :]
    for k in range(SC_CONV - 1):
        uc = uc + sst_ref[k] * ws[k:k + 1, :]
    for k in range(SC_CONV - 2):
        snew_ref[k] = sst_ref[k + 1]
    snew_ref[SC_CONV - 2] = cv
    yb_ref[...] = _sigmoid(pb[:, 4 * D_MODEL:5 * D_MODEL]) * (pb[:, 0:D_MODEL] * uc)
    sga_ref[...] = _sigmoid(pb[:, 3 * D_MODEL:4 * D_MODEL])


def _sample_proj(x, mstate_t, sstate_t, w):
    nb = x.shape[0]
    f = lambda *s: jax.ShapeDtypeStruct(s, F32)
    return pl.pallas_call(
        _sproj_body,
        out_shape=[f(nb, SSM_INNER), f(nb, SSM_INNER), f(nb, SSM_INNER), f(nb, SSM_HEADS),
                   f(nb, SSM_GROUPS * SSM_STATE), f(nb, SSM_GROUPS * SSM_STATE),
                   f(nb, D_MODEL), f(nb, D_MODEL),
                   f(SSM_CONV - 1, nb, SSM_CONV_DIM), f(SC_CONV - 1, nb, D_MODEL)],
        compiler_params=pltpu.CompilerParams(vmem_limit_bytes=VMEM_LIMIT),
        name="sample_proj",
    )(x, w["norm_mix"], w["w_a"], w["w_dt"], w["w_b"], w["w_mconv"], w["b_mconv"], w["dt_bias"],
      w["a_log"], w["w_sconv"], mstate_t, sstate_t)


def _sstate_body(dec_ref, s_ref, dtx_ref, bm_ref, cm_ref, snew_ref, y_ref):
    i = pl.program_id(0)
    rows_per_blk = LANES
    for j in range(STATE_BB):
        b = i * STATE_BB + j
        dtx_row = dtx_ref[j:j + 1, :]
        y_parts = []
        for g in range(SSM_GROUPS):
            b_row = bm_ref[j:j + 1, g * SSM_STATE:(g + 1) * SSM_STATE]
            c_row = cm_ref[j:j + 1, g * SSM_STATE:(g + 1) * SSM_STATE].astype(BF16)
            new_blocks = []
            for q in range(GROUP_WIDTH // rows_per_blk):
                r0 = g * GROUP_WIDTH + q * rows_per_blk
                dcol = jnp.broadcast_to(dtx_row[:, r0:r0 + rows_per_blk], (rows_per_blk, LANES)).T
                sub = []
                for hh in range(rows_per_blk // SSM_HEAD_DIM):
                    h = r0 // SSM_HEAD_DIM + hh
                    lo = hh * SSM_HEAD_DIM
                    s_old = s_ref[j, r0 + lo:r0 + lo + SSM_HEAD_DIM, :]
                    sub.append(s_old * dec_ref[b, h] + dcol[lo:lo + SSM_HEAD_DIM, :] * b_row)
                blk = jnp.concatenate(sub, axis=0)
                snew_ref[j, r0:r0 + rows_per_blk, :] = blk
                new_blocks.append(blk.astype(BF16))
            s_g = jnp.concatenate(new_blocks, axis=0)
            y_parts.append(_dot_nt(c_row, s_g))
        y_ref[j:j + 1, :] = jnp.concatenate(y_parts, axis=1)


def _sample_state(dec, state, dtx, bm, cm):
    nb = state.shape[0]
    row = lambda wdt: pl.BlockSpec((STATE_BB, wdt), lambda i, dec: (i, 0))
    st = pl.BlockSpec((STATE_BB, SSM_INNER, SSM_STATE), lambda i, dec: (i, 0, 0))
    return pl.pallas_call(
        _sstate_body,
        grid_spec=pltpu.PrefetchScalarGridSpec(
            num_scalar_prefetch=1, grid=(nb // STATE_BB,),
            in_specs=[st, row(SSM_INNER), row(SSM_GROUPS * SSM_STATE), row(SSM_GROUPS * SSM_STATE)],
            out_specs=[st, row(SSM_INNER)]),
        out_shape=[jax.ShapeDtypeStruct(state.shape, F32), jax.ShapeDtypeStruct((nb, SSM_INNER), F32)],
        compiler_params=_cparams("arbitrary"),
        name="sample_state",
    )(dec, state, dtx, bm, cm)


def _sfin1_body(x_ref, y_ref, xs_ref, z_ref, yb_ref, sga_ref, dskip_ref, gssm_ref, wout_ref,
                gx_ref, wq_ref, h_ref, q_ref):
    y = y_ref[...] + dskip_ref[...] * xs_ref[...]
    y_a = _group_rmsnorm(y * _silu(z_ref[...]), gssm_ref[...])
    merged = sga_ref[...] * y_a + yb_ref[...]
    h = x_ref[...] + _dot(merged.astype(BF16), wout_ref[...])
    h_ref[...] = h
    q_ref[...] = _dot(_rms(h, gx_ref[...]).astype(BF16), wq_ref[...])


def _sample_fin1(x, y, xs, z, yb, sga, w):
    nb = x.shape[0]
    return pl.pallas_call(
        _sfin1_body,
        out_shape=[jax.ShapeDtypeStruct((nb, D_MODEL), F32)] * 2,
        compiler_params=pltpu.CompilerParams(vmem_limit_bytes=VMEM_LIMIT),
        name="sample_fin1",
    )(x, y, xs, z, yb, sga, w["d_skip"], w["norm_ssm"], w["w_out"], w["norm_xattn"], w["w_xq"])


def _sattn_body(q_ref, k_ref, v_ref, o_ref):
    for j in range(ATTN_BB):
        q_row = q_ref[j]
        q4 = jnp.concatenate([q_row[:, h * XA_HEAD_DIM:(h + 1) * XA_HEAD_DIM]
                              for h in range(XA_HEADS)], axis=0)
        s = jnp.sum(k_ref[j] * q4[None], axis=-1, keepdims=True) * (XA_HEAD_DIM ** -0.5)
        e = jnp.exp(s - jnp.max(s, axis=0, keepdims=True))
        p = e / jnp.sum(e, axis=0, keepdims=True)
        o4 = jnp.sum(p * v_ref[j], axis=0)
        o_ref[j] = jnp.concatenate([o4[h:h + 1, :] for h in range(XA_HEADS)], axis=1)


def _sample_attn(q3, k3, v3):
    nb = q3.shape[0]
    qs = pl.BlockSpec((ATTN_BB, 1, D_MODEL), lambda i: (i, 0, 0))
    kv = pl.BlockSpec((ATTN_BB, N_MEM, XA_HEADS, XA_HEAD_DIM), lambda i: (i, 0, 0, 0))
    return pl.pallas_call(
        _sattn_body,
        grid=(nb // ATTN_BB,),
        in_specs=[qs, kv, kv],
        out_specs=qs,
        out_shape=jax.ShapeDtypeStruct((nb, 1, D_MODEL), F32),
        compiler_params=_cparams("arbitrary"),
        name="sample_attn",
    )(q3, k3, v3)


def _sfin2_body(h_ref, o_ref, wo_ref, gmoe_ref, wr_ref, br_ref, h2_ref, hn_ref, lg_ref):
    h2 = h_ref[...] + _dot(o_ref[...].astype(BF16), wo_ref[...])
    _router_tail(h2, gmoe_ref, wr_ref, br_ref, h2_ref, hn_ref, lg_ref)


def _sample_fin2(h1, o, w):
    nb = h1.shape[0]
    return pl.pallas_call(
        _sfin2_body,
        out_shape=[jax.ShapeDtypeStruct((nb, D_MODEL), F32)] * 2
        + [jax.ShapeDtypeStruct((N_EXPERTS, nb), F32)],
        compiler_params=pltpu.CompilerParams(vmem_limit_bytes=VMEM_LIMIT),
        name="sample_fin2",
    )(h1, o, w["w_xo"], w["norm_moe"], w["w_router"], w["b_router"])


def _pad_cols(x, cols):
    return jnp.concatenate([x, jnp.zeros((x.shape[0], cols - x.shape[1]), x.dtype)], axis=1)


def _route_body(lgp_ref, lgs_ref, g_ref, loc_ref, cnt_ref, off_ref):
    tt = lgp_ref.shape[1]
    is_sample = pl.program_id(0) == pl.num_programs(0) - 1
    col = lax.broadcasted_iota(I32, (1, tt), 1)
    valid = jnp.logical_or(jnp.logical_not(is_sample), col < lgs_ref.shape[1])
    work = jnp.where(is_sample, _pad_cols(lgs_ref[...], tt), lgp_ref[...])
    sub = lax.broadcasted_iota(I32, (N_EXPERTS, tt), 0).astype(F32)
    vals, hots = [], []
    for _ in range(TOP_K):
        m = jnp.max(work, axis=0, keepdims=True)
        idx = jnp.min(jnp.where(work == m, sub, float(N_EXPERTS)), axis=0, keepdims=True)
        hot = (sub == idx) & valid
        vals.append(m)
        hots.append(hot)
        work = jnp.where(hot, -jnp.inf, work)
    exps = [jnp.exp(v - vals[0]) for v in vals]
    tot = exps[0]
    for e in exps[1:]:
        tot = tot + e
    assigned = hots[0]
    for hot in hots[1:]:
        assigned = assigned | hot
    a = assigned.astype(BF16)
    r_i = lax.broadcasted_iota(I32, (tt, tt), 0)
    c_i = lax.broadcasted_iota(I32, (tt, tt), 1)
    rank = _dot(a, (r_i < c_i).astype(BF16))
    cnt = jnp.sum(a.astype(F32), axis=1, keepdims=True)
    cnt = jnp.floor((cnt + (SUBLANES - 1)) * (1.0 / SUBLANES)) * SUBLANES
    e_r = lax.broadcasted_iota(I32, (N_EXPERTS, N_EXPERTS), 0)
    e_c = lax.broadcasted_iota(I32, (N_EXPERTS, N_EXPERTS), 1)
    cnt_cols = jnp.broadcast_to(cnt, (N_EXPERTS, LANES)).astype(BF16)
    off = _dot((e_r > e_c).astype(BF16), cnt_cols)[:, 0:1]
    slot = rank + off
    k_sub = lax.broadcasted_iota(I32, (SUBLANES, tt), 0)
    g_out = jnp.zeros((SUBLANES, tt), F32)
    l_out = jnp.full((SUBLANES, tt), -1.0, F32)
    for k in range(TOP_K):
        lk = jnp.sum(jnp.where(hots[k], slot, 0.0), axis=0, keepdims=True)
        g_out = jnp.where(k_sub == k, jnp.where(valid, exps[k] / tot, 0.0), g_out)
        l_out = jnp.where(k_sub == k, jnp.where(valid, lk, -1.0), l_out)
    g_ref[...] = g_out
    loc_ref[...] = l_out.astype(I32)
    cnt_ref[...] = cnt.astype(I32)
    off_ref[...] = off.astype(I32)


def _route(logits_p, logits_s, tt):
    ntp = logits_p.shape[1] // tt
    nt = ntp + 1
    t = nt * tt
    tk = pl.BlockSpec((SUBLANES, tt), lambda i: (0, i))
    per_tile = pl.BlockSpec((None, N_EXPERTS, 1), lambda i: (i, 0, 0))
    return pl.pallas_call(
        _route_body,
        grid=(nt,),
        in_specs=[pl.BlockSpec((N_EXPERTS, tt), lambda i: (0, jnp.minimum(i, ntp - 1))),
                  pl.BlockSpec(logits_s.shape, lambda i: (0, 0))],
        out_specs=[tk, tk, per_tile, per_tile],
        out_shape=[jax.ShapeDtypeStruct((SUBLANES, t), F32), jax.ShapeDtypeStruct((SUBLANES, t), I32),
                   jax.ShapeDtypeStruct((nt, N_EXPERTS, 1), I32), jax.ShapeDtypeStruct((nt, N_EXPERTS, 1), I32)],
        compiler_params=_cparams("arbitrary"),
        name="moe_route",
    )(logits_p, logits_s)


def _sorted_rows(tt):
    return tt * TOP_K + N_EXPERTS * SUBLANES


def _run_copies(tt, tile, cnt_ref, off_ref, base_ref, make_copy, wait):
    if wait:
        total = off_ref[tile, N_EXPERTS - 1] + cnt_ref[tile, N_EXPERTS - 1]

        @pl.when(total > 0)
        def _():
            make_copy(0, 0, pl.multiple_of(total, SUBLANES)).wait()
        return

    def per_expert(e):
        n = cnt_ref[tile, e]

        @pl.when(n > 0)
        def _():
            make_copy(pl.multiple_of(off_ref[tile, e], SUBLANES),
                      pl.multiple_of(base_ref[tile, e], SUBLANES), pl.multiple_of(n, SUBLANES)).start()

    def four_experts(j, carry):
        for u in range(RUN_COPY_UNROLL):
            per_expert(j * RUN_COPY_UNROLL + u)
        return carry

    lax.fori_loop(0, N_EXPERTS // RUN_COPY_UNROLL, four_experts, 0)


def _dispatch_body(tm, tt, cnt_ref, off_ref, base_ref, zstart_ref, zsize_ref, loc_ref, x_ref, xs_ref, o_hbm,
                   zero_ref, srt_ref, zsem, sems):
    i = pl.program_id(0)
    last = pl.num_programs(0) - 1
    r = _sorted_rows(tt)

    def zero_copy(j):
        n = pl.multiple_of(zsize_ref[j], SUBLANES)
        dst = o_hbm.at[pl.ds(pl.multiple_of(zstart_ref[j], SUBLANES), n), :]
        return pltpu.make_async_copy(zero_ref.at[pl.ds(0, n), :], dst, zsem)

    @pl.when(i == 0)
    def _():
        zero_ref[...] = jnp.zeros_like(zero_ref)

        def start(j, carry):
            @pl.when(zsize_ref[j] > 0)
            def _():
                zero_copy(j).start()
            return carry

        def wait(j, carry):
            @pl.when(zsize_ref[j] > 0)
            def _():
                zero_copy(j).wait()
            return carry

        lax.fori_loop(0, zstart_ref.shape[0], start, 0)
        lax.fori_loop(0, zstart_ref.shape[0], wait, 0)

    loc = loc_ref[...]
    slot_i = lax.broadcasted_iota(I32, (r, tt), 0)
    hit = slot_i == loc[0:1, :]
    for k in range(1, TOP_K):
        hit = hit | (slot_i == loc[k:k + 1, :])
    buf = i % 2
    x = jnp.where(i == last, _pad_rows(xs_ref[...], tt), x_ref[...])
    srt_ref[buf] = _pack_bf16_pairs(_dot(hit.astype(BF16), x.astype(BF16)), is_bf16_valued=True)

    def copies(tile, wait):
        b = tile % 2

        def make_copy(lo, go, size):
            return pltpu.make_async_copy(srt_ref.at[b, pl.ds(lo, size), :], o_hbm.at[pl.ds(go, size), :],
                                         sems.at[b])

        _run_copies(tt, tile, cnt_ref, off_ref, base_ref, make_copy, wait)

    copies(i, False)

    @pl.when(i > 0)
    def _():
        copies(i - 1, True)

    @pl.when(i == last)
    def _():
        copies(i, True)


def _dispatch(cnt, off, base, zero_starts, zero_sizes, loc, hn_p, hn_s, n_rows, tm, tt):
    ntp = hn_p.shape[0] // tt
    smem = pl.BlockSpec(memory_space=pltpu.SMEM)
    return pl.pallas_call(
        functools.partial(_dispatch_body, tm, tt),
        grid_spec=pltpu.PrefetchScalarGridSpec(
            num_scalar_prefetch=0, grid=(ntp + 1,),
            in_specs=[smem, smem, smem, smem, smem,
                      pl.BlockSpec((SUBLANES, tt), lambda i: (0, i)),
                      pl.BlockSpec((tt, D_MODEL), lambda i: (jnp.minimum(i, ntp - 1), 0)),
                      pl.BlockSpec(hn_s.shape, lambda i: (0, 0))],
            out_specs=pl.BlockSpec(memory_space=pl.ANY),
            scratch_shapes=[pltpu.VMEM((tm, D_MODEL // 2), U32),
                            pltpu.VMEM((2, _sorted_rows(tt), D_MODEL // 2), U32),
                            pltpu.SemaphoreType.DMA, pltpu.SemaphoreType.DMA((2,))]),
        out_shape=jax.ShapeDtypeStruct((n_rows, D_MODEL // 2), U32),
        compiler_params=_cparams("arbitrary"),
        name="moe_dispatch",
    )(cnt, off, base, zero_starts, zero_sizes, loc, hn_p, hn_s)


def _expert_body(be_ref, nu_ref, bv_ref, slot_ref, nxt_ref, x_ref, wgu_hbm, bgu_ref, wdn_hbm, bdn_ref,
                 y_ref, wgu_f, wdn_f, wgu_b, wdn_b, sems):
    i = pl.program_id(0)
    tm = x_ref.shape[0]
    valid = bv_ref[i]
    expert = be_ref[i]
    slot = slot_ref[i]

    def weight_copies(e, s):
        return (pltpu.make_async_copy(wgu_hbm.at[e], wgu_f.at[s], sems.at[0, s]),
                pltpu.make_async_copy(wdn_hbm.at[e], wdn_f.at[s], sems.at[1, s]))

    @pl.when(i == 0)
    def _():
        for cp in weight_copies(expert, slot):
            cp.start()

    first = jnp.logical_and(i < nu_ref[0], jnp.logical_or(i == 0, expert != be_ref[jnp.maximum(i - 1, 0)]))

    @pl.when(first)
    def _():
        for cp in weight_copies(expert, slot):
            cp.wait()

        @pl.when(nxt_ref[i] >= 0)
        def _():
            for cp in weight_copies(nxt_ref[i], 1 - slot):
                cp.start()

    def ffn(rows, cast):
        xb = _unpack_bf16_pairs(x_ref[0:rows, :])
        gu_parts = []
        for j in range(2 * D_FF // EXPERT_COL_CHUNK):
            cs = slice(j * EXPERT_COL_CHUNK, (j + 1) * EXPERT_COL_CHUNK)
            if cast:
                wgu_b[:, cs] = wgu_f[slot, :, cs].astype(BF16)
            gu_parts.append(_dot(xb, wgu_b[:, cs]) + bgu_ref[:, cs])
        gate = jnp.minimum(jnp.concatenate(gu_parts[:len(gu_parts) // 2], axis=1), SWIGLU_LIMIT)
        up = jnp.clip(jnp.concatenate(gu_parts[len(gu_parts) // 2:], axis=1), -SWIGLU_LIMIT, SWIGLU_LIMIT)
        act = ((up + 1.0) * (gate * _sigmoid(SWIGLU_ALPHA * gate))).astype(BF16)
        y_parts = []
        for j in range(D_MODEL // EXPERT_COL_CHUNK):
            cs = slice(j * EXPERT_COL_CHUNK, (j + 1) * EXPERT_COL_CHUNK)
            if cast:
                wdn_b[:, cs] = wdn_f[slot, :, cs].astype(BF16)
            y_parts.append(_dot(act, wdn_b[:, cs]) + bdn_ref[:, cs])
        y_ref[0:rows, :] = _pack_bf16_pairs(jnp.concatenate(y_parts, axis=1))
        if rows < tm:
            y_ref[rows:tm, :] = jnp.zeros((tm - rows, D_MODEL // 2), U32)

    quarter = tm // EXPERT_ROW_SPLITS
    for q in range(1, EXPERT_ROW_SPLITS + 1):
        in_q = jnp.logical_and(valid > (q - 1) * quarter, valid <= q * quarter)
        for cast in (True, False):
            @pl.when(jnp.logical_and(in_q, first == cast))
            def _(q=q, cast=cast):
                ffn(q * quarter, cast)

    @pl.when(valid == 0)
    def _():
        y_ref[...] = jnp.zeros_like(y_ref)


def _experts(block_e, n_used, block_valid, block_slot, block_next, xs, wgu, bgu, wdn, bdn, tm):
    n_rows = xs.shape[0]
    return pl.pallas_call(
        _expert_body,
        grid_spec=pltpu.PrefetchScalarGridSpec(
            num_scalar_prefetch=5, grid=(n_rows // tm,),
            in_specs=[pl.BlockSpec((tm, D_MODEL // 2), lambda i, be, nu, *_: (jnp.minimum(i, nu[0] - 1), 0)),
                      pl.BlockSpec(memory_space=pl.ANY),
                      pl.BlockSpec((None, 1, 2 * D_FF), lambda i, be, *_: (be[i], 0, 0)),
                      pl.BlockSpec(memory_space=pl.ANY),
                      pl.BlockSpec((None, 1, D_MODEL), lambda i, be, *_: (be[i], 0, 0))],
            out_specs=pl.BlockSpec((tm, D_MODEL // 2), lambda i, *_: (i, 0)),
            scratch_shapes=[pltpu.VMEM((2, D_MODEL, 2 * D_FF), F32), pltpu.VMEM((2, D_FF, D_MODEL), F32),
                            pltpu.VMEM((D_MODEL, 2 * D_FF), BF16), pltpu.VMEM((D_FF, D_MODEL), BF16),
                            pltpu.SemaphoreType.DMA((2, 2))]),
        out_shape=jax.ShapeDtypeStruct((n_rows, D_MODEL // 2), U32),
        compiler_params=_cparams("arbitrary"),
        name="moe_experts",
    )(block_e, n_used, block_valid, block_slot, block_next, xs, wgu, bgu, wdn, bdn)


def _combine_body(tt, cnt_ref, off_ref, base_ref, loc_ref, g_ref, h_ref, hs_ref, gfin_ref, ys_hbm,
                  y_ref, ysmp_ref, buf_ref, sems):
    i = pl.program_id(0)
    last = pl.num_programs(0) - 1
    r = _sorted_rows(tt)

    def copies(tile, wait):
        b = tile % 2

        def make_copy(lo, go, size):
            return pltpu.make_async_copy(ys_hbm.at[pl.ds(go, size), :], buf_ref.at[b, pl.ds(lo, size), :],
                                         sems.at[b])

        _run_copies(tt, tile, cnt_ref, off_ref, base_ref, make_copy, wait)

    @pl.when(i == 0)
    def _():
        buf_ref[...] = jnp.zeros_like(buf_ref)
        copies(0, False)

    @pl.when(i < last)
    def _():
        copies(i + 1, False)

    copies(i, True)
    loc = loc_ref[...]
    gates = g_ref[...]
    slot_i = lax.broadcasted_iota(I32, (r, tt), 0)
    gmat = jnp.where(slot_i == loc[0:1, :], gates[0:1, :], 0.0)
    for k in range(1, TOP_K):
        gmat = gmat + jnp.where(slot_i == loc[k:k + 1, :], gates[k:k + 1, :], 0.0)
    h = jnp.where(i == last, _pad_rows(hs_ref[...], tt), h_ref[...])
    moe = lax.dot_general(gmat.astype(BF16), _unpack_bf16_pairs(buf_ref[i % 2]), TN_DIMS,
                          preferred_element_type=F32)
    y = _rms(h + moe, gfin_ref[...])

    @pl.when(i < last)
    def _():
        y_ref[...] = y

    @pl.when(i == last)
    def _():
        ysmp_ref[...] = y[0:ysmp_ref.shape[0], :]


def _combine(cnt, off, base, loc, gates, h2_p, h2_s, norm_final, ys, tt):
    ntp = h2_p.shape[0] // tt
    tok = pl.BlockSpec((tt, D_MODEL), lambda i: (jnp.minimum(i, ntp - 1), 0))
    smp = pl.BlockSpec(h2_s.shape, lambda i: (0, 0))
    tk = pl.BlockSpec((SUBLANES, tt), lambda i: (0, i))
    smem = pl.BlockSpec(memory_space=pltpu.SMEM)
    return pl.pallas_call(
        functools.partial(_combine_body, tt),
        grid_spec=pltpu.PrefetchScalarGridSpec(
            num_scalar_prefetch=0, grid=(ntp + 1,),
            in_specs=[smem, smem, smem, tk, tk, tok, smp,
                      pl.BlockSpec((1, D_MODEL), lambda i: (0, 0)),
                      pl.BlockSpec(memory_space=pl.ANY)],
            out_specs=[tok, smp],
            scratch_shapes=[pltpu.VMEM((2, _sorted_rows(tt), D_MODEL // 2), U32), pltpu.SemaphoreType.DMA((2,))]),
        out_shape=[jax.ShapeDtypeStruct(h2_p.shape, F32), jax.ShapeDtypeStruct(h2_s.shape, F32)],
        compiler_params=_cparams("arbitrary"),
        name="moe_combine",
    )(cnt, off, base, loc, gates, h2_p, h2_s, norm_final, ys)


def _moe_and_final_norm(hn_p, logits_p, h2_p, hn_s, logits_s, h2_s, w, tt, tm):
    nt = hn_p.shape[0] // tt + 1
    t = hn_p.shape[0] + hn_s.shape[0]
    gates, loc, cnt3, off3 = _route(logits_p, logits_s, tt)
    cnt = cnt3[:, :, 0]
    counts = jnp.sum(cnt, axis=0)
    padded = (counts + tm - 1) // tm * tm
    pad_end = jnp.cumsum(padded)
    start = pad_end - padded
    off = off3[:, :, 0]
    base = (start[None, :] + jnp.cumsum(cnt, axis=0) - cnt).astype(I32)
    n_blocks = (t * TOP_K + nt * N_EXPERTS * (SUBLANES - 1) + N_EXPERTS * (tm - 1)) // tm
    n_rows = n_blocks * tm
    block_start = jnp.arange(n_blocks, dtype=I32) * tm
    block_e = jnp.minimum(jnp.sum(block_start[:, None] >= pad_end[None, :], axis=-1), N_EXPERTS - 1).astype(I32)
    n_used = (pad_end[-1:] // tm).astype(I32)
    zero_starts = jnp.concatenate([start + counts, block_start]).astype(I32)
    zero_sizes = jnp.concatenate([padded - counts,
                                  jnp.where(block_start >= pad_end[-1], tm, 0)]).astype(I32)
    xs = _dispatch(cnt, off, base, zero_starts, zero_sizes, loc, hn_p, hn_s, n_rows, tm, tt)
    e_ids = jnp.arange(N_EXPERTS, dtype=I32)
    block_hot = block_e[:, None] == e_ids[None, :]

    def per_block(table):
        return jnp.sum(jnp.where(block_hot, table[None, :], 0), axis=1).astype(I32)

    block_valid = jnp.clip(per_block(start + counts) - block_start, 0, tm)
    block_valid = jnp.where(block_start < pad_end[-1], block_valid, 0).astype(I32)
    present = padded > 0
    later = present[None, :] & (e_ids[None, :] > e_ids[:, None])
    next_e = jnp.min(jnp.where(later, e_ids[None, :], N_EXPERTS), axis=1)
    next_e = jnp.where(next_e < N_EXPERTS, next_e, -1).astype(I32)
    before = present[None, :] & (e_ids[None, :] < e_ids[:, None])
    run_slot = (jnp.sum(before.astype(I32), axis=1) % 2).astype(I32)
    ys = _experts(block_e, n_used, block_valid, per_block(run_slot), per_block(next_e), xs, w["w_gate_up"], w["b_gate_up"], w["w_down"], w["b_down"], tm)
    return _combine(cnt, off, base, loc, gates, h2_p, h2_s, w["norm_final"], ys, tt)


def kernel(x_prompt, x_sample, mem_prompt, state_ssm, state_mamba_conv, state_short_conv, cache_mem_k, cache_mem_v, norm_mix, w_in, w_mconv, b_mconv, dt_bias, a_log, d_skip, norm_ssm, w_sconv, w_out, norm_xattn, norm_mem, w_xq, w_xk, w_xv, w_xo, norm_moe, w_router, b_router, w_gate_up, b_gate_up, w_down, b_down, norm_final):
    nbp, seq, _ = x_prompt.shape
    nbs = x_sample.shape[0]
    dt_lo = SSM_INNER + SSM_CONV_DIM
    w_in0 = w_in[0]
    w_dt = w_in0[:, dt_lo:dt_lo + SSM_HEADS]
    w = {
        "norm_mix": norm_mix, "norm_ssm": norm_ssm, "norm_xattn": norm_xattn, "norm_moe": norm_moe,
        "norm_final": norm_final.reshape(1, D_MODEL),
        "w_a": w_in0[:, :dt_lo].astype(BF16),
        "w_dt": w_dt.astype(BF16), "w_dt_t": w_dt.T.astype(BF16),
        "w_b": w_in0[:, dt_lo + SSM_HEADS:].astype(BF16),
        "w_mconv": w_mconv[0], "b_mconv": b_mconv,
        "dt_bias": dt_bias, "dt_bias_t": dt_bias.reshape(SSM_HEADS, 1),
        "a_log": a_log, "a_log_t": a_log.reshape(SSM_HEADS, 1),
        "d_skip": jnp.repeat(d_skip, SSM_HEAD_DIM, axis=1),
        "w_sconv": w_sconv[0], "w_out": w_out[0].astype(BF16),
        "w_xq": w_xq[0].astype(BF16), "w_xo": w_xo[0].astype(BF16),
        "w_router": w_router[0].T.astype(BF16), "b_router": b_router.reshape(N_EXPERTS, 1),
        "w_gate_up": w_gate_up[0], "b_gate_up": b_gate_up[0].reshape(N_EXPERTS, 1, 2 * D_FF),
        "w_down": w_down[0], "b_down": b_down[0].reshape(N_EXPERTS, 1, D_MODEL),
    }

    k_p, v_p, kb, vb = _mem_kv(mem_prompt.reshape(nbp * N_MEM, D_MODEL), norm_mem,
                               w_xk[0].astype(BF16), w_xv[0].astype(BF16))
    h1, ssm_p, mconv_p, sconv_p = _prompt_mixer(x_prompt.reshape(nbp * seq, D_MODEL), nbp, w)
    h2, hn, logits = _prompt_attn(h1, kb, vb, nbp, w)

    xs2 = x_sample.reshape(nbs, D_MODEL)
    mstate_t = jnp.transpose(state_mamba_conv[0], (1, 0, 2))
    sstate_t = jnp.transpose(state_short_conv[0], (1, 0, 2))
    z, xs_, dtx, dec, bm, cm, yb, sga, mnew_t, snew_t = _sample_proj(xs2, mstate_t, sstate_t, w)
    ssm_s, y_s = _sample_state(dec, state_ssm[0].reshape(nbs, SSM_INNER, SSM_STATE), dtx, bm, cm)
    h1s, q_s = _sample_fin1(xs2, y_s, xs_, z, yb, sga, w)
    o_s = _sample_attn(q_s.reshape(nbs, 1, D_MODEL),
                       cache_mem_k[0], cache_mem_v[0])
    h2s, hns, logits_s = _sample_fin2(h1s, o_s.reshape(nbs, D_MODEL), w)
    y_prompt, y_sample = _moe_and_final_norm(hn, logits, h2, hns, logits_s, h2s, w, MIX_TILE, MOE_ROW_TILE)

    return (y_prompt.reshape(nbp, seq, D_MODEL),
            y_sample.reshape(nbs, 1, D_MODEL),
            ssm_p.reshape(1, nbp, SSM_HEADS, SSM_HEAD_DIM, SSM_STATE),
            mconv_p[None], sconv_p[None],
            k_p.reshape(1, nbp, N_MEM, XA_HEADS, XA_HEAD_DIM),
            v_p.reshape(1, nbp, N_MEM, XA_HEADS, XA_HEAD_DIM),
            ssm_s.reshape(1, nbs, SSM_HEADS, SSM_HEAD_DIM, SSM_STATE),
            jnp.transpose(mnew_t, (1, 0, 2))[None],
            jnp.transpose(snew_t, (1, 0, 2))[None])
```

```python
import functools

import jax
import jax.numpy as jnp
from jax import lax
from jax.experimental import pallas as pl
from jax.experimental.pallas import tpu as pltpu

F32 = jnp.float32
BF16 = jnp.bfloat16
I32 = jnp.int32
U32 = jnp.uint32

D_MODEL = 1024
N_MEM = 256
SSM_HEADS = 16
SSM_HEAD_DIM = 64
SSM_INNER = SSM_HEADS * SSM_HEAD_DIM
SSM_STATE = 128
SSM_GROUPS = 4
HEADS_PER_GROUP = SSM_HEADS // SSM_GROUPS
GROUP_WIDTH = SSM_INNER // SSM_GROUPS
SSM_CONV = 4
SSM_CONV_DIM = SSM_INNER + 2 * SSM_GROUPS * SSM_STATE
SC_CONV = 3
XA_HEADS = 4
XA_HEAD_DIM = D_MODEL // XA_HEADS
N_EXPERTS = 32
TOP_K = 4
D_FF = D_MODEL
SWIGLU_LIMIT = 7.0
SWIGLU_ALPHA = 1.702
EPS = 1e-6

LANES = 128
SUBLANES = 8
VMEM_LIMIT = 56 * 1024 * 1024

MIX_TILE = 256
ATTN_TILE = 512
MOE_ROW_TILE = 512
RUN_COPY_UNROLL = 4
EXPERT_ROW_SPLITS = 4
EXPERT_COL_CHUNK = 512
STATE_BB = 8
ATTN_BB = 4

NT_DIMS = (((1,), (1,)), ((), ()))
TN_DIMS = (((0,), (0,)), ((), ()))


def _cparams(*sem):
    return pltpu.CompilerParams(dimension_semantics=sem, vmem_limit_bytes=VMEM_LIMIT)


def _const_spec(shape):
    nd = len(shape)
    return pl.BlockSpec(shape, lambda *_: (0,) * nd, pipeline_mode=pl.Buffered(1))


def _sigmoid(x):
    return 1.0 / (1.0 + jnp.exp(-x))


def _silu(x):
    return x * _sigmoid(x)


def _softplus(x):
    return jnp.maximum(x, 0.0) + jnp.log(1.0 + jnp.exp(-jnp.abs(x)))


def _rms(x, g):
    ms = jnp.mean(x * x, axis=-1, keepdims=True)
    return x * lax.rsqrt(ms + EPS) * g


def _dot(a, b):
    return jnp.dot(a, b, preferred_element_type=F32)


def _dot_nt(a, b):
    return lax.dot_general(a, b, NT_DIMS, preferred_element_type=F32)


def _expand_heads(v):
    rows = v.shape[0]
    lane = lax.broadcasted_iota(I32, (rows, LANES), 1)
    pieces = []
    for j in range(SSM_HEADS // 2):
        a = jnp.broadcast_to(v[:, 2 * j:2 * j + 1], (rows, LANES))
        b = jnp.broadcast_to(v[:, 2 * j + 1:2 * j + 2], (rows, LANES))
        pieces.append(jnp.where(lane < SSM_HEAD_DIM, a, b))
    return jnp.concatenate(pieces, axis=1)


def _cumsum(x, axis):
    idx = lax.broadcasted_iota(I32, x.shape, axis)
    shift = 1
    while shift < x.shape[axis]:
        x = x + jnp.where(idx >= shift, pltpu.roll(x, shift, axis), 0.0)
        shift *= 2
    return x


def _pack_bf16_pairs(x, is_bf16_valued=False):
    w = x.shape[1] // 2
    if not is_bf16_valued:
        x = x.astype(BF16).astype(F32)
    bits = lax.bitcast_convert_type(x, U32)
    return (bits[:, w:] & jnp.uint32(0xFFFF0000)) | (bits[:, :w] >> 16)


def _unpack_bf16_pairs(p):
    lo = lax.bitcast_convert_type(p << 16, F32)
    hi = lax.bitcast_convert_type(p & jnp.uint32(0xFFFF0000), F32)
    return jnp.concatenate([lo, hi], axis=1).astype(BF16)


def _pad_rows(x, rows):
    return jnp.concatenate([x, jnp.zeros((rows - x.shape[0], x.shape[1]), x.dtype)], axis=0)


def _group_rmsnorm(u, g):
    outs = []
    for k in range(SSM_GROUPS):
        ug = u[:, k * GROUP_WIDTH:(k + 1) * GROUP_WIDTH]
        ms = jnp.mean(ug * ug, axis=-1, keepdims=True)
        outs.append(ug * lax.rsqrt(ms + EPS))
    return jnp.concatenate(outs, axis=1) * g


def _memkv_body(mem_ref, g_ref, wk_ref, wv_ref, k_ref, v_ref, kb_ref, vb_ref):
    mn = _rms(mem_ref[...], g_ref[...]).astype(BF16)
    k = _dot(mn, wk_ref[...])
    v = _dot(mn, wv_ref[...])
    k_ref[...] = k
    v_ref[...] = v
    kb_ref[...] = k.astype(BF16)
    vb_ref[...] = v.astype(BF16)


def _mem_kv(mem2d, norm_mem, wk, wv):
    rows = mem2d.shape[0]
    nb = rows // N_MEM
    blk = pl.BlockSpec((N_MEM, D_MODEL), lambda b: (b, 0))
    return pl.pallas_call(
        _memkv_body,
        grid=(nb,),
        in_specs=[blk, _const_spec((1, D_MODEL)), _const_spec((D_MODEL, D_MODEL)),
                  _const_spec((D_MODEL, D_MODEL))],
        out_specs=[blk, blk, blk, blk],
        out_shape=[jax.ShapeDtypeStruct((rows, D_MODEL), F32)] * 2
        + [jax.ShapeDtypeStruct((rows, D_MODEL), BF16)] * 2,
        compiler_params=_cparams("arbitrary"),
        name="mem_kv",
    )(mem2d, norm_mem, wk, wv)


def _mix_body(x_ref, gmix_ref, wa_ref, wdtc_ref, wdtr_ref, wb_ref, wmc_ref, bmc_ref,
              dtb_ref, dtbt_ref, alog_ref, alogt_ref, dskip_ref, gssm_ref, wsc_ref, wout_ref,
              h_ref, ssm_ref, mbuf_ref, sbuf_ref,
              st_ref, cbuf_ref, scbuf_ref):
    tq = MIX_TILE
    c = pl.program_id(1)

    @pl.when(c == 0)
    def _():
        st_ref[...] = jnp.zeros_like(st_ref)
        cbuf_ref[0:SUBLANES, :] = jnp.zeros((SUBLANES, SSM_CONV_DIM), F32)
        scbuf_ref[0:SUBLANES, :] = jnp.zeros((SUBLANES, D_MODEL), F32)

    x = x_ref[...]
    xn = _rms(x, gmix_ref[...]).astype(BF16)

    u = _dot(xn, wa_ref[:, SSM_INNER:])
    cbuf_ref[SUBLANES:SUBLANES + tq, :] = u
    wm = wmc_ref[...]
    conv = u * wm[SSM_CONV - 1:SSM_CONV, :] + bmc_ref[...]
    for k in range(SSM_CONV - 1):
        off = SUBLANES - (SSM_CONV - 1) + k
        conv = conv + cbuf_ref[off:off + tq, :] * wm[k:k + 1, :]
    tail = cbuf_ref[tq + SUBLANES - (SSM_CONV - 1):tq + SUBLANES, :]
    mbuf_ref[...] = tail
    cbuf_ref[SUBLANES - (SSM_CONV - 1):SUBLANES, :] = tail
    xbc = _silu(conv)
    xs = xbc[:, :SSM_INNER]
    bm = xbc[:, SSM_INNER:SSM_INNER + SSM_GROUPS * SSM_STATE]
    cm = xbc[:, SSM_INNER + SSM_GROUPS * SSM_STATE:]

    dt = _softplus(_dot(xn, wdtc_ref[...]) + dtb_ref[...])
    dtt = _softplus(_dot_nt(wdtr_ref[...], xn) + dtbt_ref[...])
    a_row = -jnp.exp(alog_ref[...])
    a_col = -jnp.exp(alogt_ref[...])
    row_i = lax.broadcasted_iota(I32, (tq, tq), 0)
    col_i = lax.broadcasted_iota(I32, (tq, tq), 1)
    causal = row_i >= col_i
    a_cum = _cumsum(dt * a_row, 0)
    a_cumt = _cumsum(dtt * a_col, 1)
    a_last = a_cum[tq - 1:tq, :]

    xdt = xs * _expand_heads(dt)
    in_decay = _expand_heads(jnp.exp(a_cum))
    to_end = _expand_heads(jnp.exp(a_last - a_cum))
    chunk_decay = _expand_heads(jnp.exp(a_last))
    xdt_b = xdt.astype(BF16)
    xend_b = (xdt * to_end).astype(BF16)
    lane = lax.broadcasted_iota(I32, (tq, LANES), 1)

    def proj_b(k):
        return _dot(xn, wb_ref[:, k * D_MODEL:(k + 1) * D_MODEL])

    pb = []
    y_groups = []
    for g in range(SSM_GROUPS):
        pb.append(proj_b(g))
        if g == 0:
            z = _dot(xn, wa_ref[:, :SSM_INNER])
        if g == 2:
            g_b = proj_b(SSM_GROUPS)
        cg = cm[:, g * SSM_STATE:(g + 1) * SSM_STATE].astype(BF16)
        bg_f = bm[:, g * SSM_STATE:(g + 1) * SSM_STATE]
        bg = bg_f.astype(BF16)
        scores = _dot_nt(cg, bg)
        gs = slice(g * GROUP_WIDTH, (g + 1) * GROUP_WIDTH)
        st_g = st_ref[:, gs]
        y_off = _dot(cg, st_g.astype(BF16)) * in_decay[:, gs]
        pair_out = []
        for pr in range(HEADS_PER_GROUP // 2):
            h0 = g * HEADS_PER_GROUP + 2 * pr
            xp = xdt_b[:, h0 * SSM_HEAD_DIM:(h0 + 2) * SSM_HEAD_DIM]
            ys = []
            for h in (h0, h0 + 1):
                seg = a_cum[:, h:h + 1] - a_cumt[h:h + 1, :]
                decay = jnp.where(causal, jnp.exp(jnp.minimum(seg, 0.0)), 0.0)
                ys.append(_dot((scores * decay).astype(BF16), xp))
            pair_out.append(jnp.where(lane < SSM_HEAD_DIM, ys[0], ys[1]))
        y_groups.append(jnp.concatenate(pair_out, axis=1) + y_off)
        st_ref[:, gs] = st_g * chunk_decay[:, gs] + _dot(bg_f.T.astype(BF16), xend_b[:, gs])
    y = jnp.concatenate(y_groups, axis=1) + dskip_ref[...] * xs
    y_a = _group_rmsnorm(y * _silu(z), gssm_ref[...])

    sc_b, sc_c, sc_v, g_a = pb
    cv = sc_c * sc_v
    scbuf_ref[SUBLANES:SUBLANES + tq, :] = cv
    ws = wsc_ref[...]
    uc = cv * ws[SC_CONV - 1:SC_CONV, :]
    for k in range(SC_CONV - 1):
        off = SUBLANES - (SC_CONV - 1) + k
        uc = uc + scbuf_ref[off:off + tq, :] * ws[k:k + 1, :]
    stail = scbuf_ref[tq + SUBLANES - (SC_CONV - 1):tq + SUBLANES, :]
    sbuf_ref[...] = stail
    scbuf_ref[SUBLANES - (SC_CONV - 1):SUBLANES, :] = stail
    merged = _sigmoid(g_a) * y_a + _sigmoid(g_b) * (sc_b * uc)
    h_ref[...] = x + _dot(merged.astype(BF16), wout_ref[...])

    @pl.when(c == pl.num_programs(1) - 1)
    def _():
        ssm_ref[...] = st_ref[...].T


def _prompt_mixer(x2d, nb, w):
    t = x2d.shape[0]
    nc = t // nb // MIX_TILE
    tok = pl.BlockSpec((MIX_TILE, D_MODEL), lambda b, c: (b * nc + c, 0))
    return pl.pallas_call(
        _mix_body,
        grid=(nb, nc),
        in_specs=[tok, _const_spec((1, D_MODEL)),
                  _const_spec((D_MODEL, SSM_INNER + SSM_CONV_DIM)),
                  _const_spec((D_MODEL, SSM_HEADS)), _const_spec((SSM_HEADS, D_MODEL)),
                  _const_spec((D_MODEL, 5 * D_MODEL)),
                  _const_spec((SSM_CONV, SSM_CONV_DIM)), _const_spec((1, SSM_CONV_DIM)),
                  _const_spec((1, SSM_HEADS)), _const_spec((SSM_HEADS, 1)),
                  _const_spec((1, SSM_HEADS)), _const_spec((SSM_HEADS, 1)),
                  _const_spec((1, SSM_INNER)), _const_spec((1, SSM_INNER)),
                  _const_spec((SC_CONV, D_MODEL)), _const_spec((D_MODEL, D_MODEL))],
        out_specs=[tok,
                   pl.BlockSpec((None, SSM_INNER, SSM_STATE), lambda b, c: (b, 0, 0)),
                   pl.BlockSpec((None, SSM_CONV - 1, SSM_CONV_DIM), lambda b, c: (b, 0, 0)),
                   pl.BlockSpec((None, SC_CONV - 1, D_MODEL), lambda b, c: (b, 0, 0))],
        out_shape=[jax.ShapeDtypeStruct((t, D_MODEL), F32),
                   jax.ShapeDtypeStruct((nb, SSM_INNER, SSM_STATE), F32),
                   jax.ShapeDtypeStruct((nb, SSM_CONV - 1, SSM_CONV_DIM), F32),
                   jax.ShapeDtypeStruct((nb, SC_CONV - 1, D_MODEL), F32)],
        scratch_shapes=[pltpu.VMEM((SSM_STATE, SSM_INNER), F32),
                        pltpu.VMEM((MIX_TILE + SUBLANES, SSM_CONV_DIM), F32),
                        pltpu.VMEM((MIX_TILE + SUBLANES, D_MODEL), F32)],
        compiler_params=_cparams("arbitrary", "arbitrary"),
        name="prompt_mixer",
    )(x2d, w["norm_mix"], w["w_a"], w["w_dt"], w["w_dt_t"], w["w_b"], w["w_mconv"], w["b_mconv"],
      w["dt_bias"], w["dt_bias_t"], w["a_log"], w["a_log_t"], w["d_skip"], w["norm_ssm"],
      w["w_sconv"], w["w_out"])


def _router_tail(h2, gmoe_ref, wr_ref, br_ref, h2_ref, hn_ref, lg_ref):
    h2_ref[...] = h2
    hn = _rms(h2, gmoe_ref[...])
    hn_ref[...] = hn
    lg_ref[...] = _dot_nt(wr_ref[...], hn.astype(BF16)) + br_ref[...]


def _attn_body(h_ref, gx_ref, wq_ref, k_ref, v_ref, wo_ref, gmoe_ref, wr_ref, br_ref,
               h2_ref, hn_ref, lg_ref):
    h = h_ref[...]
    hn = _rms(h, gx_ref[...]).astype(BF16)
    q = _dot(hn, wq_ref[...]).astype(BF16)
    outs = []
    for hd in range(XA_HEADS):
        sl = slice(hd * XA_HEAD_DIM, (hd + 1) * XA_HEAD_DIM)
        s = _dot_nt(q[:, sl], k_ref[:, sl]) * (XA_HEAD_DIM ** -0.5)
        e = jnp.exp(s - jnp.max(s, axis=-1, keepdims=True))
        p = e / jnp.sum(e, axis=-1, keepdims=True)
        outs.append(_dot(p.astype(BF16), v_ref[:, sl]))
    o = jnp.concatenate(outs, axis=1).astype(BF16)
    h2 = h + _dot(o, wo_ref[...])
    _router_tail(h2, gmoe_ref, wr_ref, br_ref, h2_ref, hn_ref, lg_ref)


def _prompt_attn(h1, kb, vb, nb, w):
    t = h1.shape[0]
    nc = t // nb // ATTN_TILE
    tok = pl.BlockSpec((ATTN_TILE, D_MODEL), lambda b, c: (b * nc + c, 0))
    kv = pl.BlockSpec((N_MEM, D_MODEL), lambda b, c: (b, 0))
    return pl.pallas_call(
        _attn_body,
        grid=(nb, nc),
        in_specs=[tok, _const_spec((1, D_MODEL)), _const_spec((D_MODEL, D_MODEL)), kv, kv,
                  _const_spec((D_MODEL, D_MODEL)), _const_spec((1, D_MODEL)),
                  _const_spec((N_EXPERTS, D_MODEL)), _const_spec((N_EXPERTS, 1))],
        out_specs=[tok, tok, pl.BlockSpec((N_EXPERTS, ATTN_TILE), lambda b, c: (0, b * nc + c))],
        out_shape=[jax.ShapeDtypeStruct((t, D_MODEL), F32),
                   jax.ShapeDtypeStruct((t, D_MODEL), F32),
                   jax.ShapeDtypeStruct((N_EXPERTS, t), F32)],
        compiler_params=_cparams("arbitrary", "arbitrary"),
        name="prompt_attn",
    )(h1, w["norm_xattn"], w["w_xq"], kb, vb, w["w_xo"], w["norm_moe"], w["w_router"], w["b_router"])


def _sproj_body(x_ref, gmix_ref, wa_ref, wdtc_ref, wb_ref, wmc_ref, bmc_ref, dtb_ref, alog_ref,
                wsc_ref, mst_ref, sst_ref,
                z_ref, xs_ref, dtx_ref, dec_ref, bm_ref, cm_ref, yb_ref, sga_ref, mnew_ref, snew_ref):
    x = x_ref[...]
    xn = _rms(x, gmix_ref[...]).astype(BF16)
    pa = _dot(xn, wa_ref[...])
    z_ref[...] = pa[:, :SSM_INNER]
    u = pa[:, SSM_INNER:]
    wm = wmc_ref[...]
    conv = u * wm[SSM_CONV - 1:SSM_CONV, :] + bmc_ref[...]
    for k in range(SSM_CONV - 1):
        conv = conv + mst_ref[k] * wm[k:k + 1, :]
    for k in range(SSM_CONV - 2):
        mnew_ref[k] = mst_ref[k + 1]
    mnew_ref[SSM_CONV - 2] = u
    xbc = _silu(conv)
    xs = xbc[:, :SSM_INNER]
    xs_ref[...] = xs
    bm_ref[...] = xbc[:, SSM_INNER:SSM_INNER + SSM_GROUPS * SSM_STATE]
    cm_ref[...] = xbc[:, SSM_INNER + SSM_GROUPS * SSM_STATE:]
    dt = _softplus(_dot(xn, wdtc_ref[...]) + dtb_ref[...])
    dec_ref[...] = jnp.exp(dt * (-jnp.exp(alog_ref[...])))
    dtx_ref[...] = xs * _expand_heads(dt)
    pb = _dot(xn, wb_ref[...])
    cv = pb[:, D_MODEL:2 * D_MODEL] * pb[:, 2 * D_MODEL:3 * D_MODEL]
    ws = wsc_ref[...]
    uc = cv * ws[SC_CONV - 1:SC_CONV, :]
    for k in range(SC_CONV - 1):
        uc = uc + sst_ref[k] * ws[k:k + 1, :]
    for k in range(SC_CONV - 2):
        snew_ref[k] = sst_ref[k + 1]
    snew_ref[SC_CONV - 2] = cv
    yb_ref[...] = _sigmoid(pb[:, 4 * D_MODEL:5 * D_MODEL]) * (pb[:, 0:D_MODEL] * uc)
    sga_ref[...] = _sigmoid(pb[:, 3 * D_MODEL:4 * D_MODEL])


def _sample_proj(x, mstate_t, sstate_t, w):
    nb = x.shape[0]
    f = lambda *s: jax.ShapeDtypeStruct(s, F32)
    return pl.pallas_call(
        _sproj_body,
        out_shape=[f(nb, SSM_INNER), f(nb, SSM_INNER), f(nb, SSM_INNER), f(nb, SSM_HEADS),
                   f(nb, SSM_GROUPS * SSM_STATE), f(nb, SSM_GROUPS * SSM_STATE),
                   f(nb, D_MODEL), f(nb, D_MODEL),
                   f(SSM_CONV - 1, nb, SSM_CONV_DIM), f(SC_CONV - 1, nb, D_MODEL)],
        compiler_params=pltpu.CompilerParams(vmem_limit_bytes=VMEM_LIMIT),
        name="sample_proj",
    )(x, w["norm_mix"], w["w_a"], w["w_dt"], w["w_b"], w["w_mconv"], w["b_mconv"], w["dt_bias"],
      w["a_log"], w["w_sconv"], mstate_t, sstate_t)


def _sstate_body(dec_ref, s_ref, dtx_ref, bm_ref, cm_ref, snew_ref, y_ref):
    i = pl.program_id(0)
    rows_per_blk = LANES
    for j in range(STATE_BB):
        b = i * STATE_BB + j
        dtx_row = dtx_ref[j:j + 1, :]
        y_parts = []
        for g in range(SSM_GROUPS):
            b_row = bm_ref[j:j + 1, g * SSM_STATE:(g + 1) * SSM_STATE]
            c_row = cm_ref[j:j + 1, g * SSM_STATE:(g + 1) * SSM_STATE].astype(BF16)
            new_blocks = []
            for q in range(GROUP_WIDTH // rows_per_blk):
                r0 = g * GROUP_WIDTH + q * rows_per_blk
                dcol = jnp.broadcast_to(dtx_row[:, r0:r0 + rows_per_blk], (rows_per_blk, LANES)).T
                sub = []
                for hh in range(rows_per_blk // SSM_HEAD_DIM):
                    h = r0 // SSM_HEAD_DIM + hh
                    lo = hh * SSM_HEAD_DIM
                    s_old = s_ref[j, r0 + lo:r0 + lo + SSM_HEAD_DIM, :]
                    sub.append(s_old * dec_ref[b, h] + dcol[lo:lo + SSM_HEAD_DIM, :] * b_row)
                blk = jnp.concatenate(sub, axis=0)
                snew_ref[j, r0:r0 + rows_per_blk, :] = blk
                new_blocks.append(blk.astype(BF16))
            s_g = jnp.concatenate(new_blocks, axis=0)
            y_parts.append(_dot_nt(c_row, s_g))
        y_ref[j:j + 1, :] = jnp.concatenate(y_parts, axis=1)


def _sample_state(dec, state, dtx, bm, cm):
    nb = state.shape[0]
    row = lambda wdt: pl.BlockSpec((STATE_BB, wdt), lambda i, dec: (i, 0))
    st = pl.BlockSpec((STATE_BB, SSM_INNER, SSM_STATE), lambda i, dec: (i, 0, 0))
    return pl.pallas_call(
        _sstate_body,
        grid_spec=pltpu.PrefetchScalarGridSpec(
            num_scalar_prefetch=1, grid=(nb // STATE_BB,),
            in_specs=[st, row(SSM_INNER), row(SSM_GROUPS * SSM_STATE), row(SSM_GROUPS * SSM_STATE)],
            out_specs=[st, row(SSM_INNER)]),
        out_shape=[jax.ShapeDtypeStruct(state.shape, F32), jax.ShapeDtypeStruct((nb, SSM_INNER), F32)],
        compiler_params=_cparams("arbitrary"),
        name="sample_state",
    )(dec, state, dtx, bm, cm)


def _sfin1_body(x_ref, y_ref, xs_ref, z_ref, yb_ref, sga_ref, dskip_ref, gssm_ref, wout_ref,
                gx_ref, wq_ref, h_ref, q_ref):
    y = y_ref[...] + dskip_ref[...] * xs_ref[...]
    y_a = _group_rmsnorm(y * _silu(z_ref[...]), gssm_ref[...])
    merged = sga_ref[...] * y_a + yb_ref[...]
    h = x_ref[...] + _dot(merged.astype(BF16), wout_ref[...])
    h_ref[...] = h
    q_ref[...] = _dot(_rms(h, gx_ref[...]).astype(BF16), wq_ref[...])


def _sample_fin1(x, y, xs, z, yb, sga, w):
    nb = x.shape[0]
    return pl.pallas_call(
        _sfin1_body,
        out_shape=[jax.ShapeDtypeStruct((nb, D_MODEL), F32)] * 2,
        compiler_params=pltpu.CompilerParams(vmem_limit_bytes=VMEM_LIMIT),
        name="sample_fin1",
    )(x, y, xs, z, yb, sga, w["d_skip"], w["norm_ssm"], w["w_out"], w["norm_xattn"], w["w_xq"])


def _sattn_body(q_ref, k_ref, v_ref, o_ref):
    for j in range(ATTN_BB):
        q_row = q_ref[j]
        q4 = jnp.concatenate([q_row[:, h * XA_HEAD_DIM:(h + 1) * XA_HEAD_DIM]
                              for h in range(XA_HEADS)], axis=0)
        s = jnp.sum(k_ref[j] * q4[None], axis=-1, keepdims=True) * (XA_HEAD_DIM ** -0.5)
        e = jnp.exp(s - jnp.max(s, axis=0, keepdims=True))
        p = e / jnp.sum(e, axis=0, keepdims=True)
        o4 = jnp.sum(p * v_ref[j], axis=0)
        o_ref[j] = jnp.concatenate([o4[h:h + 1, :] for h in range(XA_HEADS)], axis=1)


def _sample_attn(q3, k3, v3):
    nb = q3.shape[0]
    qs = pl.BlockSpec((ATTN_BB, 1, D_MODEL), lambda i: (i, 0, 0))
    kv = pl.BlockSpec((ATTN_BB, N_MEM, XA_HEADS, XA_HEAD_DIM), lambda i: (i, 0, 0, 0))
    return pl.pallas_call(
        _sattn_body,
        grid=(nb // ATTN_BB,),
        in_specs=[qs, kv, kv],
        out_specs=qs,
        out_shape=jax.ShapeDtypeStruct((nb, 1, D_MODEL), F32),
        compiler_params=_cparams("arbitrary"),
        name="sample_attn",
    )(q3, k3, v3)


def _sfin2_body(h_ref, o_ref, wo_ref, gmoe_ref, wr_ref, br_ref, h2_ref, hn_ref, lg_ref):
    h2 = h_ref[...] + _dot(o_ref[...].astype(BF16), wo_ref[...])
    _router_tail(h2, gmoe_ref, wr_ref, br_ref, h2_ref, hn_ref, lg_ref)


def _sample_fin2(h1, o, w):
    nb = h1.shape[0]
    return pl.pallas_call(
        _sfin2_body,
        out_shape=[jax.ShapeDtypeStruct((nb, D_MODEL), F32)] * 2
        + [jax.ShapeDtypeStruct((N_EXPERTS, nb), F32)],
        compiler_params=pltpu.CompilerParams(vmem_limit_bytes=VMEM_LIMIT),
        name="sample_fin2",
    )(h1, o, w["w_xo"], w["norm_moe"], w["w_router"], w["b_router"])


def _pad_cols(x, cols):
    return jnp.concatenate([x, jnp.zeros((x.shape[0], cols - x.shape[1]), x.dtype)], axis=1)


def _route_body(lgp_ref, lgs_ref, g_ref, loc_ref, cnt_ref, off_ref):
    tt = lgp_ref.shape[1]
    is_sample = pl.program_id(0) == pl.num_programs(0) - 1
    col = lax.broadcasted_iota(I32, (1, tt), 1)
    valid = jnp.logical_or(jnp.logical_not(is_sample), col < lgs_ref.shape[1])
    work = jnp.where(is_sample, _pad_cols(lgs_ref[...], tt), lgp_ref[...])
    sub = lax.broadcasted_iota(I32, (N_EXPERTS, tt), 0).astype(F32)
    vals, hots = [], []
    for _ in range(TOP_K):
        m = jnp.max(work, axis=0, keepdims=True)
        idx = jnp.min(jnp.where(work == m, sub, float(N_EXPERTS)), axis=0, keepdims=True)
        hot = (sub == idx) & valid
        vals.append(m)
        hots.append(hot)
        work = jnp.where(hot, -jnp.inf, work)
    exps = [jnp.exp(v - vals[0]) for v in vals]
    tot = exps[0]
    for e in exps[1:]:
        tot = tot + e
    assigned = hots[0]
    for hot in hots[1:]:
        assigned = assigned | hot
    a = assigned.astype(BF16)
    r_i = lax.broadcasted_iota(I32, (tt, tt), 0)
    c_i = lax.broadcasted_iota(I32, (tt, tt), 1)
    rank = _dot(a, (r_i < c_i).astype(BF16))
    cnt = jnp.sum(a.astype(F32), axis=1, keepdims=True)
    cnt = jnp.floor((cnt + (SUBLANES - 1)) * (1.0 / SUBLANES)) * SUBLANES
    e_r = lax.broadcasted_iota(I32, (N_EXPERTS, N_EXPERTS), 0)
    e_c = lax.broadcasted_iota(I32, (N_EXPERTS, N_EXPERTS), 1)
    cnt_cols = jnp.broadcast_to(cnt, (N_EXPERTS, LANES)).astype(BF16)
    off = _dot((e_r > e_c).astype(BF16), cnt_cols)[:, 0:1]
    slot = rank + off
    k_sub = lax.broadcasted_iota(I32, (SUBLANES, tt), 0)
    g_out = jnp.zeros((SUBLANES, tt), F32)
    l_out = jnp.full((SUBLANES, tt), -1.0, F32)
    for k in range(TOP_K):
        lk = jnp.sum(jnp.where(hots[k], slot, 0.0), axis=0, keepdims=True)
        g_out = jnp.where(k_sub == k, jnp.where(valid, exps[k] / tot, 0.0), g_out)
        l_out = jnp.where(k_sub == k, jnp.where(valid, lk, -1.0), l_out)
    g_ref[...] = g_out
    loc_ref[...] = l_out.astype(I32)
    cnt_ref[...] = cnt.astype(I32)
    off_ref[...] = off.astype(I32)


def _route(logits_p, logits_s, tt):
    ntp = logits_p.shape[1] // tt
    nt = ntp + 1
    t = nt * tt
    tk = pl.BlockSpec((SUBLANES, tt), lambda i: (0, i))
    per_tile = pl.BlockSpec((None, N_EXPERTS, 1), lambda i: (i, 0, 0))
    return pl.pallas_call(
        _route_body,
        grid=(nt,),
        in_specs=[pl.BlockSpec((N_EXPERTS, tt), lambda i: (0, jnp.minimum(i, ntp - 1))),
                  pl.BlockSpec(logits_s.shape, lambda i: (0, 0))],
        out_specs=[tk, tk, per_tile, per_tile],
        out_shape=[jax.ShapeDtypeStruct((SUBLANES, t), F32), jax.ShapeDtypeStruct((SUBLANES, t), I32),
                   jax.ShapeDtypeStruct((nt, N_EXPERTS, 1), I32), jax.ShapeDtypeStruct((nt, N_EXPERTS, 1), I32)],
        compiler_params=_cparams("arbitrary"),
        name="moe_route",
    )(logits_p, logits_s)


def _sorted_rows(tt):
    return tt * TOP_K + N_EXPERTS * SUBLANES


def _run_copies(tt, tile, cnt_ref, off_ref, base_ref, make_copy, wait):
    if wait:
        total = off_ref[tile, N_EXPERTS - 1] + cnt_ref[tile, N_EXPERTS - 1]

        @pl.when(total > 0)
        def _():
            make_copy(0, 0, pl.multiple_of(total, SUBLANES)).wait()
        return

    def per_expert(e):
        n = cnt_ref[tile, e]

        @pl.when(n > 0)
        def _():
            make_copy(pl.multiple_of(off_ref[tile, e], SUBLANES),
                      pl.multiple_of(base_ref[tile, e], SUBLANES), pl.multiple_of(n, SUBLANES)).start()

    def four_experts(j, carry):
        for u in range(RUN_COPY_UNROLL):
            per_expert(j * RUN_COPY_UNROLL + u)
        return carry

    lax.fori_loop(0, N_EXPERTS // RUN_COPY_UNROLL, four_experts, 0)


def _dispatch_body(tm, tt, cnt_ref, off_ref, base_ref, zstart_ref, zsize_ref, loc_ref, x_ref, xs_ref, o_hbm,
                   zero_ref, srt_ref, zsem, sems):
    i = pl.program_id(0)
    last = pl.num_programs(0) - 1
    r = _sorted_rows(tt)

    def zero_copy(j):
        n = pl.multiple_of(zsize_ref[j], SUBLANES)
        dst = o_hbm.at[pl.ds(pl.multiple_of(zstart_ref[j], SUBLANES), n), :]
        return pltpu.make_async_copy(zero_ref.at[pl.ds(0, n), :], dst, zsem)

    @pl.when(i == 0)
    def _():
        zero_ref[...] = jnp.zeros_like(zero_ref)

        def start(j, carry):
            @pl.when(zsize_ref[j] > 0)
            def _():
                zero_copy(j).start()
            return carry

        def wait(j, carry):
            @pl.when(zsize_ref[j] > 0)
            def _():
                zero_copy(j).wait()
            return carry

        lax.fori_loop(0, zstart_ref.shape[0], start, 0)
        lax.fori_loop(0, zstart_ref.shape[0], wait, 0)

    loc = loc_ref[...]
    slot_i = lax.broadcasted_iota(I32, (r, tt), 0)
    hit = slot_i == loc[0:1, :]
    for k in range(1, TOP_K):
        hit = hit | (slot_i == loc[k:k + 1, :])
    buf = i % 2
    x = jnp.where(i == last, _pad_rows(xs_ref[...], tt), x_ref[...])
    srt_ref[buf] = _pack_bf16_pairs(_dot(hit.astype(BF16), x.astype(BF16)), is_bf16_valued=True)

    def copies(tile, wait):
        b = tile % 2

        def make_copy(lo, go, size):
            return pltpu.make_async_copy(srt_ref.at[b, pl.ds(lo, size), :], o_hbm.at[pl.ds(go, size), :],
                                         sems.at[b])

        _run_copies(tt, tile, cnt_ref, off_ref, base_ref, make_copy, wait)

    copies(i, False)

    @pl.when(i > 0)
    def _():
        copies(i - 1, True)

    @pl.when(i == last)
    def _():
        copies(i, True)


def _dispatch(cnt, off, base, zero_starts, zero_sizes, loc, hn_p, hn_s, n_rows, tm, tt):
    ntp = hn_p.shape[0] // tt
    smem = pl.BlockSpec(memory_space=pltpu.SMEM)
    return pl.pallas_call(
        functools.partial(_dispatch_body, tm, tt),
        grid_spec=pltpu.PrefetchScalarGridSpec(
            num_scalar_prefetch=0, grid=(ntp + 1,),
            in_specs=[smem, smem, smem, smem, smem,
                      pl.BlockSpec((SUBLANES, tt), lambda i: (0, i)),
                      pl.BlockSpec((tt, D_MODEL), lambda i: (jnp.minimum(i, ntp - 1), 0)),
                      pl.BlockSpec(hn_s.shape, lambda i: (0, 0))],
            out_specs=pl.BlockSpec(memory_space=pl.ANY),
            scratch_shapes=[pltpu.VMEM((tm, D_MODEL // 2), U32),
                            pltpu.VMEM((2, _sorted_rows(tt), D_MODEL // 2), U32),
                            pltpu.SemaphoreType.DMA, pltpu.SemaphoreType.DMA((2,))]),
        out_shape=jax.ShapeDtypeStruct((n_rows, D_MODEL // 2), U32),
        compiler_params=_cparams("arbitrary"),
        name="moe_dispatch",
    )(cnt, off, base, zero_starts, zero_sizes, loc, hn_p, hn_s)


def _expert_body(be_ref, nu_ref, bv_ref, slot_ref, nxt_ref, x_ref, wgu_hbm, bgu_ref, wdn_hbm, bdn_ref,
                 y_ref, wgu_f, wdn_f, wgu_b, wdn_b, sems):
    i = pl.program_id(0)
    tm = x_ref.shape[0]
    valid = bv_ref[i]
    expert = be_ref[i]
    slot = slot_ref[i]

    def weight_copies(e, s):
        return (pltpu.make_async_copy(wgu_hbm.at[e], wgu_f.at[s], sems.at[0, s]),
                pltpu.make_async_copy(wdn_hbm.at[e], wdn_f.at[s], sems.at[1, s]))

    @pl.when(i == 0)
    def _():
        for cp in weight_copies(expert, slot):
            cp.start()

    first = jnp.logical_and(i < nu_ref[0], jnp.logical_or(i == 0, expert != be_ref[jnp.maximum(i - 1, 0)]))

    @pl.when(first)
    def _():
        for cp in weight_copies(expert, slot):
            cp.wait()

        @pl.when(nxt_ref[i] >= 0)
        def _():
            for cp in weight_copies(nxt_ref[i], 1 - slot):
                cp.start()

    def ffn(rows, cast):
        xb = _unpack_bf16_pairs(x_ref[0:rows, :])
        gu_parts = []
        for j in range(2 * D_FF // EXPERT_COL_CHUNK):
            cs = slice(j * EXPERT_COL_CHUNK, (j + 1) * EXPERT_COL_CHUNK)
            if cast:
                wgu_b[:, cs] = wgu_f[slot, :, cs].astype(BF16)
            gu_parts.append(_dot(xb, wgu_b[:, cs]) + bgu_ref[:, cs])
        gate = jnp.minimum(jnp.concatenate(gu_parts[:len(gu_parts) // 2], axis=1), SWIGLU_LIMIT)
        up = jnp.clip(jnp.concatenate(gu_parts[len(gu_parts) // 2:], axis=1), -SWIGLU_LIMIT, SWIGLU_LIMIT)
        act = ((up + 1.0) * (gate * _sigmoid(SWIGLU_ALPHA * gate))).astype(BF16)
        y_parts = []
        for j in range(D_MODEL // EXPERT_COL_CHUNK):
            cs = slice(j * EXPERT_COL_CHUNK, (j + 1) * EXPERT_COL_CHUNK)
            if cast:
                wdn_b[:, cs] = wdn_f[slot, :, cs].astype(BF16)
            y_parts.append(_dot(act, wdn_b[:, cs]) + bdn_ref[:, cs])
        y_ref[0:rows, :] = _pack_bf16_pairs(jnp.concatenate(y_parts, axis=1))
        if rows < tm:
            y_ref[rows:tm, :] = jnp.zeros((tm - rows, D_MODEL // 2), U32)

    quarter = tm // EXPERT_ROW_SPLITS
    for q in range(1, EXPERT_ROW_SPLITS + 1):
        in_q = jnp.logical_and(valid > (q - 1) * quarter, valid <= q * quarter)
        for cast in (True, False):
            @pl.when(jnp.logical_and(in_q, first == cast))
            def _(q=q, cast=cast):
                ffn(q * quarter, cast)

    @pl.when(valid == 0)
    def _():
        y_ref[...] = jnp.zeros_like(y_ref)


def _experts(block_e, n_used, block_valid, block_slot, block_next, xs, wgu, bgu, wdn, bdn, tm):
    n_rows = xs.shape[0]
    return pl.pallas_call(
        _expert_body,
        grid_spec=pltpu.PrefetchScalarGridSpec(
            num_scalar_prefetch=5, grid=(n_rows // tm,),
            in_specs=[pl.BlockSpec((tm, D_MODEL // 2), lambda i, be, nu, *_: (jnp.minimum(i, nu[0] - 1), 0)),
                      pl.BlockSpec(memory_space=pl.ANY),
                      pl.BlockSpec((None, 1, 2 * D_FF), lambda i, be, *_: (be[i], 0, 0)),
                      pl.BlockSpec(memory_space=pl.ANY),
                      pl.BlockSpec((None, 1, D_MODEL), lambda i, be, *_: (be[i], 0, 0))],
            out_specs=pl.BlockSpec((tm, D_MODEL // 2), lambda i, *_: (i, 0)),
            scratch_shapes=[pltpu.VMEM((2, D_MODEL, 2 * D_FF), F32), pltpu.VMEM((2, D_FF, D_MODEL), F32),
                            pltpu.VMEM((D_MODEL, 2 * D_FF), BF16), pltpu.VMEM((D_FF, D_MODEL), BF16),
                            pltpu.SemaphoreType.DMA((2, 2))]),
        out_shape=jax.ShapeDtypeStruct((n_rows, D_MODEL // 2), U32),
        compiler_params=_cparams("arbitrary"),
        name="moe_experts",
    )(block_e, n_used, block_valid, block_slot, block_next, xs, wgu, bgu, wdn, bdn)


def _combine_body(tt, cnt_ref, off_ref, base_ref, loc_ref, g_ref, h_ref, hs_ref, gfin_ref, ys_hbm,
                  y_ref, ysmp_ref, buf_ref, sems):
    i = pl.program_id(0)
    last = pl.num_programs(0) - 1
    r = _sorted_rows(tt)

    def copies(tile, wait):
        b = tile % 2

        def make_copy(lo, go, size):
            return pltpu.make_async_copy(ys_hbm.at[pl.ds(go, size), :], buf_ref.at[b, pl.ds(lo, size), :],
                                         sems.at[b])

        _run_copies(tt, tile, cnt_ref, off_ref, base_ref, make_copy, wait)

    @pl.when(i == 0)
    def _():
        buf_ref[...] = jnp.zeros_like(buf_ref)
        copies(0, False)

    @pl.when(i < last)
    def _():
        copies(i + 1, False)

    copies(i, True)
    loc = loc_ref[...]
    gates = g_ref[...]
    slot_i = lax.broadcasted_iota(I32, (r, tt), 0)
    gmat = jnp.where(slot_i == loc[0:1, :], gates[0:1, :], 0.0)
    for k in range(1, TOP_K):
        gmat = gmat + jnp.where(slot_i == loc[k:k + 1, :], gates[k:k + 1, :], 0.0)
    h = jnp.where(i == last, _pad_rows(hs_ref[...], tt), h_ref[...])
    moe = lax.dot_general(gmat.astype(BF16), _unpack_bf16_pairs(buf_ref[i % 2]), TN_DIMS,
                          preferred_element_type=F32)
    y = _rms(h + moe, gfin_ref[...])

    @pl.when(i < last)
    def _():
        y_ref[...] = y

    @pl.when(i == last)
    def _():
        ysmp_ref[...] = y[0:ysmp_ref.shape[0], :]


def _combine(cnt, off, base, loc, gates, h2_p, h2_s, norm_final, ys, tt):
    ntp = h2_p.shape[0] // tt
    tok = pl.BlockSpec((tt, D_MODEL), lambda i: (jnp.minimum(i, ntp - 1), 0))
    smp = pl.BlockSpec(h2_s.shape, lambda i: (0, 0))
    tk = pl.BlockSpec((SUBLANES, tt), lambda i: (0, i))
    smem = pl.BlockSpec(memory_space=pltpu.SMEM)
    return pl.pallas_call(
        functools.partial(_combine_body, tt),
        grid_spec=pltpu.PrefetchScalarGridSpec(
            num_scalar_prefetch=0, grid=(ntp + 1,),
            in_specs=[smem, smem, smem, tk, tk, tok, smp,
                      pl.BlockSpec((1, D_MODEL), lambda i: (0, 0)),
                      pl.BlockSpec(memory_space=pl.ANY)],
            out_specs=[tok, smp],
            scratch_shapes=[pltpu.VMEM((2, _sorted_rows(tt), D_MODEL // 2), U32), pltpu.SemaphoreType.DMA((2,))]),
        out_shape=[jax.ShapeDtypeStruct(h2_p.shape, F32), jax.ShapeDtypeStruct(h2_s.shape, F32)],
        compiler_params=_cparams("arbitrary"),
        name="moe_combine",
    )(cnt, off, base, loc, gates, h2_p, h2_s, norm_final, ys)


def _moe_and_final_norm(hn_p, logits_p, h2_p, hn_s, logits_s, h2_s, w, tt, tm):
    nt = hn_p.shape[0] // tt + 1
    t = hn_p.shape[0] + hn_s.shape[0]
    gates, loc, cnt3, off3 = _route(logits_p, logits_s, tt)
    cnt = cnt3[:, :, 0]
    counts = jnp.sum(cnt, axis=0)
    padded = (counts + tm - 1) // tm * tm
    pad_end = jnp.cumsum(padded)
    start = pad_end - padded
    off = off3[:, :, 0]
    base = (start[None, :] + jnp.cumsum(cnt, axis=0) - cnt).astype(I32)
    n_blocks = (t * TOP_K + nt * N_EXPERTS * (SUBLANES - 1) + N_EXPERTS * (tm - 1)) // tm
    n_rows = n_blocks * tm
    block_start = jnp.arange(n_blocks, dtype=I32) * tm
    block_e = jnp.minimum(jnp.sum(block_start[:, None] >= pad_end[None, :], axis=-1), N_EXPERTS - 1).astype(I32)
    n_used = (pad_end[-1:] // tm).astype(I32)
    zero_starts = jnp.concatenate([start + counts, block_start]).astype(I32)
    zero_sizes = jnp.concatenate([padded - counts,
                                  jnp.where(block_start >= pad_end[-1], tm, 0)]).astype(I32)
    xs = _dispatch(cnt, off, base, zero_starts, zero_sizes, loc, hn_p, hn_s, n_rows, tm, tt)
    e_ids = jnp.arange(N_EXPERTS, dtype=I32)
    block_hot = block_e[:, None] == e_ids[None, :]

    def per_block(table):
        return jnp.sum(jnp.where(block_hot, table[None, :], 0), axis=1).astype(I32)

    block_valid = jnp.clip(per_block(start + counts) - block_start, 0, tm)
    block_valid = jnp.where(block_start < pad_end[-1], block_valid, 0).astype(I32)
    present = padded > 0
    later = present[None, :] & (e_ids[None, :] > e_ids[:, None])
    next_e = jnp.min(jnp.where(later, e_ids[None, :], N_EXPERTS), axis=1)
    next_e = jnp.where(next_e < N_EXPERTS, next_e, -1).astype(I32)
    before = present[None, :] & (e_ids[None, :] < e_ids[:, None])
    run_slot = (jnp.sum(before.astype(I32), axis=1) % 2).astype(I32)
    ys = _experts(block_e, n_used, block_valid, per_block(run_slot), per_block(next_e), xs, w["w_gate_up"], w["b_gate_up"], w["w_down"], w["b_down"], tm)
    return _combine(cnt, off, base, loc, gates, h2_p, h2_s, w["norm_final"], ys, tt)


def kernel(x_prompt, x_sample, mem_prompt, state_ssm, state_mamba_conv, state_short_conv, cache_mem_k, cache_mem_v, norm_mix, w_in, w_mconv, b_mconv, dt_bias, a_log, d_skip, norm_ssm, w_sconv, w_out, norm_xattn, norm_mem, w_xq, w_xk, w_xv, w_xo, norm_moe, w_router, b_router, w_gate_up, b_gate_up, w_down, b_down, norm_final):
    nbp, seq, _ = x_prompt.shape
    nbs = x_sample.shape[0]
    dt_lo = SSM_INNER + SSM_CONV_DIM
    w_in0 = w_in[0]
    w_dt = w_in0[:, dt_lo:dt_lo + SSM_HEADS]
    w = {
        "norm_mix": norm_mix, "norm_ssm": norm_ssm, "norm_xattn": norm_xattn, "norm_moe": norm_moe,
        "norm_final": norm_final.reshape(1, D_MODEL),
        "w_a": w_in0[:, :dt_lo].astype(BF16),
        "w_dt": w_dt.astype(BF16), "w_dt_t": w_dt.T.astype(BF16),
        "w_b": w_in0[:, dt_lo + SSM_HEADS:].astype(BF16),
        "w_mconv": w_mconv[0], "b_mconv": b_mconv,
        "dt_bias": dt_bias, "dt_bias_t": dt_bias.reshape(SSM_HEADS, 1),
        "a_log": a_log, "a_log_t": a_log.reshape(SSM_HEADS, 1),
        "d_skip": jnp.repeat(d_skip, SSM_HEAD_DIM, axis=1),
        "w_sconv": w_sconv[0], "w_out": w_out[0].astype(BF16),
        "w_xq": w_xq[0].astype(BF16), "w_xo": w_xo[0].astype(BF16),
        "w_router": w_router[0].T.astype(BF16), "b_router": b_router.reshape(N_EXPERTS, 1),
        "w_gate_up": w_gate_up[0], "b_gate_up": b_gate_up[0].reshape(N_EXPERTS, 1, 2 * D_FF),
        "w_down": w_down[0], "b_down": b_down[0].reshape(N_EXPERTS, 1, D_MODEL),
    }

    k_p, v_p, kb, vb = _mem_kv(mem_prompt.reshape(nbp * N_MEM, D_MODEL), norm_mem,
                               w_xk[0].astype(BF16), w_xv[0].astype(BF16))
    h1, ssm_p, mconv_p, sconv_p = _prompt_mixer(x_prompt.reshape(nbp * seq, D_MODEL), nbp, w)
    h2, hn, logits = _prompt_attn(h1, kb, vb, nbp, w)

    xs2 = x_sample.reshape(nbs, D_MODEL)
    mstate_t = jnp.transpose(state_mamba_conv[0], (1, 0, 2))
    sstate_t = jnp.transpose(state_short_conv[0], (1, 0, 2))
    z, xs_, dtx, dec, bm, cm, yb, sga, mnew_t, snew_t = _sample_proj(xs2, mstate_t, sstate_t, w)
    ssm_s, y_s = _sample_state(dec, state_ssm[0].reshape(nbs, SSM_INNER, SSM_STATE), dtx, bm, cm)
    h1s, q_s = _sample_fin1(xs2, y_s, xs_, z, yb, sga, w)
    o_s = _sample_attn(q_s.reshape(nbs, 1, D_MODEL),
                       cache_mem_k[0], cache_mem_v[0])
    h2s, hns, logits_s = _sample_fin2(h1s, o_s.reshape(nbs, D_MODEL), w)
    y_prompt, y_sample = _moe_and_final_norm(hn, logits, h2, hns, logits_s, h2s, w, MIX_TILE, MOE_ROW_TILE)

    return (y_prompt.reshape(nbp, seq, D_MODEL),
            y_sample.reshape(nbs, 1, D_MODEL),
            ssm_p.reshape(1, nbp, SSM_HEADS, SSM_HEAD_DIM, SSM_STATE),
            mconv_p[None], sconv_p[None],
            k_p.reshape(1, nbp, N_MEM, XA_HEADS, XA_HEAD_DIM),
            v_p.reshape(1, nbp, N_MEM, XA_HEADS, XA_HEAD_DIM),
            ssm_s.reshape(1, nbs, SSM_HEADS, SSM_HEAD_DIM, SSM_STATE),
            jnp.transpose(mnew_t, (1, 0, 2))[None],
            jnp.transpose(snew_t, (1, 0, 2))[None])
```

```python
import functools

import jax
import jax.numpy as jnp
from jax import lax
from jax.experimental import pallas as pl
from jax.experimental.pallas import tpu as pltpu

F32 = jnp.float32
BF16 = jnp.bfloat16
I32 = jnp.int32
U32 = jnp.uint32

D_MODEL = 1024
N_MEM = 256
SSM_HEADS = 16
SSM_HEAD_DIM = 64
SSM_INNER = SSM_HEADS * SSM_HEAD_DIM
SSM_STATE = 128
SSM_GROUPS = 4
HEADS_PER_GROUP = SSM_HEADS // SSM_GROUPS
GROUP_WIDTH = SSM_INNER // SSM_GROUPS
SSM_CONV = 4
SSM_CONV_DIM = SSM_INNER + 2 * SSM_GROUPS * SSM_STATE
SC_CONV = 3
XA_HEADS = 4
XA_HEAD_DIM = D_MODEL // XA_HEADS
N_EXPERTS = 32
TOP_K = 4
D_FF = D_MODEL
SWIGLU_LIMIT = 7.0
SWIGLU_ALPHA = 1.702
EPS = 1e-6

LANES = 128
SUBLANES = 8
VMEM_LIMIT = 56 * 1024 * 1024

MIX_TILE = 256
ATTN_TILE = 1024
MOE_ROW_TILE = 512
RUN_COPY_UNROLL = 4
EXPERT_ROW_SPLITS = 4
EXPERT_COL_CHUNK = 512
STATE_BB = 8
ATTN_BB = 4

NT_DIMS = (((1,), (1,)), ((), ()))
TN_DIMS = (((0,), (0,)), ((), ()))


def _cparams(*sem):
    return pltpu.CompilerParams(dimension_semantics=sem, vmem_limit_bytes=VMEM_LIMIT)


def _const_spec(shape):
    nd = len(shape)
    return pl.BlockSpec(shape, lambda *_: (0,) * nd, pipeline_mode=pl.Buffered(1))


def _sigmoid(x):
    return 1.0 / (1.0 + jnp.exp(-x))


def _silu(x):
    return x * _sigmoid(x)


def _softplus(x):
    return jnp.maximum(x, 0.0) + jnp.log(1.0 + jnp.exp(-jnp.abs(x)))


def _rms(x, g):
    ms = jnp.mean(x * x, axis=-1, keepdims=True)
    return x * lax.rsqrt(ms + EPS) * g


def _dot(a, b):
    return jnp.dot(a, b, preferred_element_type=F32)


def _dot_nt(a, b):
    return lax.dot_general(a, b, NT_DIMS, preferred_element_type=F32)


def _expand_heads(v):
    rows = v.shape[0]
    lane = lax.broadcasted_iota(I32, (rows, LANES), 1)
    pieces = []
    for j in range(SSM_HEADS // 2):
        a = jnp.broadcast_to(v[:, 2 * j:2 * j + 1], (rows, LANES))
        b = jnp.broadcast_to(v[:, 2 * j + 1:2 * j + 2], (rows, LANES))
        pieces.append(jnp.where(lane < SSM_HEAD_DIM, a, b))
    return jnp.concatenate(pieces, axis=1)


def _cumsum(x, axis):
    idx = lax.broadcasted_iota(I32, x.shape, axis)
    shift = 1
    while shift < x.shape[axis]:
        x = x + jnp.where(idx >= shift, pltpu.roll(x, shift, axis), 0.0)
        shift *= 2
    return x


def _pack_bf16_pairs(x, is_bf16_valued=False):
    w = x.shape[1] // 2
    if not is_bf16_valued:
        x = x.astype(BF16).astype(F32)
    bits = lax.bitcast_convert_type(x, U32)
    return (bits[:, w:] & jnp.uint32(0xFFFF0000)) | (bits[:, :w] >> 16)


def _unpack_bf16_pairs(p):
    lo = lax.bitcast_convert_type(p << 16, F32)
    hi = lax.bitcast_convert_type(p & jnp.uint32(0xFFFF0000), F32)
    return jnp.concatenate([lo, hi], axis=1).astype(BF16)


def _pad_rows(x, rows):
    return jnp.concatenate([x, jnp.zeros((rows - x.shape[0], x.shape[1]), x.dtype)], axis=0)


def _group_rmsnorm(u, g):
    outs = []
    for k in range(SSM_GROUPS):
        ug = u[:, k * GROUP_WIDTH:(k + 1) * GROUP_WIDTH]
        ms = jnp.mean(ug * ug, axis=-1, keepdims=True)
        outs.append(ug * lax.rsqrt(ms + EPS))
    return jnp.concatenate(outs, axis=1) * g


def _memkv_body(mem_ref, g_ref, wk_ref, wv_ref, k_ref, v_ref, kb_ref, vb_ref):
    mn = _rms(mem_ref[...], g_ref[...]).astype(BF16)
    k = _dot(mn, wk_ref[...])
    v = _dot(mn, wv_ref[...])
    k_ref[...] = k
    v_ref[...] = v
    kb_ref[...] = k.astype(BF16)
    vb_ref[...] = v.astype(BF16)


def _mem_kv(mem2d, norm_mem, wk, wv):
    rows = mem2d.shape[0]
    nb = rows // N_MEM
    blk = pl.BlockSpec((N_MEM, D_MODEL), lambda b: (b, 0))
    return pl.pallas_call(
        _memkv_body,
        grid=(nb,),
        in_specs=[blk, _const_spec((1, D_MODEL)), _const_spec((D_MODEL, D_MODEL)),
                  _const_spec((D_MODEL, D_MODEL))],
        out_specs=[blk, blk, blk, blk],
        out_shape=[jax.ShapeDtypeStruct((rows, D_MODEL), F32)] * 2
        + [jax.ShapeDtypeStruct((rows, D_MODEL), BF16)] * 2,
        compiler_params=_cparams("arbitrary"),
        name="mem_kv",
    )(mem2d, norm_mem, wk, wv)


def _mix_body(x_ref, gmix_ref, wa_ref, wdtc_ref, wdtr_ref, wb_ref, wmc_ref, bmc_ref,
              dtb_ref, dtbt_ref, alog_ref, alogt_ref, dskip_ref, gssm_ref, wsc_ref, wout_ref,
              h_ref, ssm_ref, mbuf_ref, sbuf_ref,
              st_ref, cbuf_ref, scbuf_ref):
    tq = MIX_TILE
    c = pl.program_id(1)

    @pl.when(c == 0)
    def _():
        st_ref[...] = jnp.zeros_like(st_ref)
        cbuf_ref[0:SUBLANES, :] = jnp.zeros((SUBLANES, SSM_CONV_DIM), F32)
        scbuf_ref[0:SUBLANES, :] = jnp.zeros((SUBLANES, D_MODEL), F32)

    x = x_ref[...]
    xn = _rms(x, gmix_ref[...]).astype(BF16)

    u = _dot(xn, wa_ref[:, SSM_INNER:])
    cbuf_ref[SUBLANES:SUBLANES + tq, :] = u
    wm = wmc_ref[...]
    conv = u * wm[SSM_CONV - 1:SSM_CONV, :] + bmc_ref[...]
    for k in range(SSM_CONV - 1):
        off = SUBLANES - (SSM_CONV - 1) + k
        conv = conv + cbuf_ref[off:off + tq, :] * wm[k:k + 1, :]
    tail = cbuf_ref[tq + SUBLANES - (SSM_CONV - 1):tq + SUBLANES, :]
    mbuf_ref[...] = tail
    cbuf_ref[SUBLANES - (SSM_CONV - 1):SUBLANES, :] = tail
    xbc = _silu(conv)
    xs = xbc[:, :SSM_INNER]
    bm = xbc[:, SSM_INNER:SSM_INNER + SSM_GROUPS * SSM_STATE]
    cm = xbc[:, SSM_INNER + SSM_GROUPS * SSM_STATE:]

    dt = _softplus(_dot(xn, wdtc_ref[...]) + dtb_ref[...])
    dtt = _softplus(_dot_nt(wdtr_ref[...], xn) + dtbt_ref[...])
    a_row = -jnp.exp(alog_ref[...])
    a_col = -jnp.exp(alogt_ref[...])
    row_i = lax.broadcasted_iota(I32, (tq, tq), 0)
    col_i = lax.broadcasted_iota(I32, (tq, tq), 1)
    causal = row_i >= col_i
    a_cum = _cumsum(dt * a_row, 0)
    a_cumt = _cumsum(dtt * a_col, 1)
    a_last = a_cum[tq - 1:tq, :]

    xdt = xs * _expand_heads(dt)
    in_decay = _expand_heads(jnp.exp(a_cum))
    to_end = _expand_heads(jnp.exp(a_last - a_cum))
    chunk_decay = _expand_heads(jnp.exp(a_last))
    xdt_b = xdt.astype(BF16)
    xend_b = (xdt * to_end).astype(BF16)
    lane = lax.broadcasted_iota(I32, (tq, LANES), 1)

    def proj_b(k):
        return _dot(xn, wb_ref[:, k * D_MODEL:(k + 1) * D_MODEL])

    pb = []
    y_groups = []
    for g in range(SSM_GROUPS):
        pb.append(proj_b(g))
        if g == 0:
            z = _dot(xn, wa_ref[:, :SSM_INNER])
        if g == 2:
            g_b = proj_b(SSM_GROUPS)
        cg = cm[:, g * SSM_STATE:(g + 1) * SSM_STATE].astype(BF16)
        bg_f = bm[:, g * SSM_STATE:(g + 1) * SSM_STATE]
        bg = bg_f.astype(BF16)
        scores = _dot_nt(cg, bg)
        gs = slice(g * GROUP_WIDTH, (g + 1) * GROUP_WIDTH)
        st_g = st_ref[:, gs]
        y_off = _dot(cg, st_g.astype(BF16)) * in_decay[:, gs]
        pair_out = []
        for pr in range(HEADS_PER_GROUP // 2):
            h0 = g * HEADS_PER_GROUP + 2 * pr
            xp = xdt_b[:, h0 * SSM_HEAD_DIM:(h0 + 2) * SSM_HEAD_DIM]
            ys = []
            for h in (h0, h0 + 1):
                seg = a_cum[:, h:h + 1] - a_cumt[h:h + 1, :]
                decay = jnp.where(causal, jnp.exp(jnp.minimum(seg, 0.0)), 0.0)
                ys.append(_dot((scores * decay).astype(BF16), xp))
            pair_out.append(jnp.where(lane < SSM_HEAD_DIM, ys[0], ys[1]))
        y_groups.append(jnp.concatenate(pair_out, axis=1) + y_off)
        st_ref[:, gs] = st_g * chunk_decay[:, gs] + _dot(bg_f.T.astype(BF16), xend_b[:, gs])
    y = jnp.concatenate(y_groups, axis=1) + dskip_ref[...] * xs
    y_a = _group_rmsnorm(y * _silu(z), gssm_ref[...])

    sc_b, sc_c, sc_v, g_a = pb
    cv = sc_c * sc_v
    scbuf_ref[SUBLANES:SUBLANES + tq, :] = cv
    ws = wsc_ref[...]
    uc = cv * ws[SC_CONV - 1:SC_CONV, :]
    for k in range(SC_CONV - 1):
        off = SUBLANES - (SC_CONV - 1) + k
        uc = uc + scbuf_ref[off:off + tq, :] * ws[k:k + 1, :]
    stail = scbuf_ref[tq + SUBLANES - (SC_CONV - 1):tq + SUBLANES, :]
    sbuf_ref[...] = stail
    scbuf_ref[SUBLANES - (SC_CONV - 1):SUBLANES, :] = stail
    merged = _sigmoid(g_a) * y_a + _sigmoid(g_b) * (sc_b * uc)
    h_ref[...] = x + _dot(merged.astype(BF16), wout_ref[...])

    @pl.when(c == pl.num_programs(1) - 1)
    def _():
        ssm_ref[...] = st_ref[...].T


def _prompt_mixer(x2d, nb, w):
    t = x2d.shape[0]
    assert (t // nb) % MIX_TILE == 0, "prompt length must be a multiple of MIX_TILE"
    nc = t // nb // MIX_TILE
    tok = pl.BlockSpec((MIX_TILE, D_MODEL), lambda b, c: (b * nc + c, 0))
    return pl.pallas_call(
        _mix_body,
        grid=(nb, nc),
        in_specs=[tok, _const_spec((1, D_MODEL)),
                  _const_spec((D_MODEL, SSM_INNER + SSM_CONV_DIM)),
                  _const_spec((D_MODEL, SSM_HEADS)), _const_spec((SSM_HEADS, D_MODEL)),
                  _const_spec((D_MODEL, 5 * D_MODEL)),
                  _const_spec((SSM_CONV, SSM_CONV_DIM)), _const_spec((1, SSM_CONV_DIM)),
                  _const_spec((1, SSM_HEADS)), _const_spec((SSM_HEADS, 1)),
                  _const_spec((1, SSM_HEADS)), _const_spec((SSM_HEADS, 1)),
                  _const_spec((1, SSM_INNER)), _const_spec((1, SSM_INNER)),
                  _const_spec((SC_CONV, D_MODEL)), _const_spec((D_MODEL, D_MODEL))],
        out_specs=[tok,
                   pl.BlockSpec((None, SSM_INNER, SSM_STATE), lambda b, c: (b, 0, 0)),
                   pl.BlockSpec((None, SSM_CONV - 1, SSM_CONV_DIM), lambda b, c: (b, 0, 0)),
                   pl.BlockSpec((None, SC_CONV - 1, D_MODEL), lambda b, c: (b, 0, 0))],
        out_shape=[jax.ShapeDtypeStruct((t, D_MODEL), F32),
                   jax.ShapeDtypeStruct((nb, SSM_INNER, SSM_STATE), F32),
                   jax.ShapeDtypeStruct((nb, SSM_CONV - 1, SSM_CONV_DIM), F32),
                   jax.ShapeDtypeStruct((nb, SC_CONV - 1, D_MODEL), F32)],
        scratch_shapes=[pltpu.VMEM((SSM_STATE, SSM_INNER), F32),
                        pltpu.VMEM((MIX_TILE + SUBLANES, SSM_CONV_DIM), F32),
                        pltpu.VMEM((MIX_TILE + SUBLANES, D_MODEL), F32)],
        compiler_params=_cparams("arbitrary", "arbitrary"),
        name="prompt_mixer",
    )(x2d, w["norm_mix"], w["w_a"], w["w_dt"], w["w_dt_t"], w["w_b"], w["w_mconv"], w["b_mconv"],
      w["dt_bias"], w["dt_bias_t"], w["a_log"], w["a_log_t"], w["d_skip"], w["norm_ssm"],
      w["w_sconv"], w["w_out"])


def _router_tail(h2, gmoe_ref, wr_ref, br_ref, h2_ref, hn_ref, lg_ref):
    h2_ref[...] = h2
    hn = _rms(h2, gmoe_ref[...])
    hn_ref[...] = hn
    lg_ref[...] = _dot_nt(wr_ref[...], hn.astype(BF16)) + br_ref[...]


def _attn_body(h_ref, gx_ref, wq_ref, k_ref, v_ref, wo_ref, gmoe_ref, wr_ref, br_ref,
               h2_ref, hn_ref, lg_ref):
    h = h_ref[...]
    hn = _rms(h, gx_ref[...]).astype(BF16)
    q = _dot(hn, wq_ref[...]).astype(BF16)
    outs = []
    for hd in range(XA_HEADS):
        sl = slice(hd * XA_HEAD_DIM, (hd + 1) * XA_HEAD_DIM)
        s = _dot_nt(q[:, sl], k_ref[:, sl]) * (XA_HEAD_DIM ** -0.5)
        e = jnp.exp(s - jnp.max(s, axis=-1, keepdims=True))
        p = e / jnp.sum(e, axis=-1, keepdims=True)
        outs.append(_dot(p.astype(BF16), v_ref[:, sl]))
    o = jnp.concatenate(outs, axis=1).astype(BF16)
    h2 = h + _dot(o, wo_ref[...])
    _router_tail(h2, gmoe_ref, wr_ref, br_ref, h2_ref, hn_ref, lg_ref)


def _prompt_attn(h1, kb, vb, nb, w):
    t = h1.shape[0]
    assert (t // nb) % ATTN_TILE == 0, "prompt length must be a multiple of ATTN_TILE"
    nc = t // nb // ATTN_TILE
    tok = pl.BlockSpec((ATTN_TILE, D_MODEL), lambda b, c: (b * nc + c, 0))
    kv = pl.BlockSpec((N_MEM, D_MODEL), lambda b, c: (b, 0))
    return pl.pallas_call(
        _attn_body,
        grid=(nb, nc),
        in_specs=[tok, _const_spec((1, D_MODEL)), _const_spec((D_MODEL, D_MODEL)), kv, kv,
                  _const_spec((D_MODEL, D_MODEL)), _const_spec((1, D_MODEL)),
                  _const_spec((N_EXPERTS, D_MODEL)), _const_spec((N_EXPERTS, 1))],
        out_specs=[tok, tok, pl.BlockSpec((N_EXPERTS, ATTN_TILE), lambda b, c: (0, b * nc + c))],
        out_shape=[jax.ShapeDtypeStruct((t, D_MODEL), F32),
                   jax.ShapeDtypeStruct((t, D_MODEL), F32),
                   jax.ShapeDtypeStruct((N_EXPERTS, t), F32)],
        compiler_params=_cparams("arbitrary", "arbitrary"),
        name="prompt_attn",
    )(h1, w["norm_xattn"], w["w_xq"], kb, vb, w["w_xo"], w["norm_moe"], w["w_router"], w["b_router"])


def _sproj_body(x_ref, gmix_ref, wa_ref, wdtc_ref, wb_ref, wmc_ref, bmc_ref, dtb_ref, alog_ref,
                wsc_ref, mst_ref, sst_ref,
                z_ref, xs_ref, dtx_ref, dec_ref, bm_ref, cm_ref, yb_ref, sga_ref, mnew_ref, snew_ref):
    x = x_ref[...]
    xn = _rms(x, gmix_ref[...]).astype(BF16)
    pa = _dot(xn, wa_ref[...])
    z_ref[...] = pa[:, :SSM_INNER]
    u = pa[:, SSM_INNER:]
    wm = wmc_ref[...]
    conv = u * wm[SSM_CONV - 1:SSM_CONV, :] + bmc_ref[...]
    for k in range(SSM_CONV - 1):
        conv = conv + mst_ref[k] * wm[k:k + 1, :]
    for k in range(SSM_CONV - 2):
        mnew_ref[k] = mst_ref[k + 1]
    mnew_ref[SSM_CONV - 2] = u
    xbc = _silu(conv)
    xs = xbc[:, :SSM_INNER]
    xs_ref[...] = xs
    bm_ref[...] = xbc[:, SSM_INNER:SSM_INNER + SSM_GROUPS * SSM_STATE]
    cm_ref[...] = xbc[:, SSM_INNER + SSM_GROUPS * SSM_STATE:]
    dt = _softplus(_dot(xn, wdtc_ref[...]) + dtb_ref[...])
    dec_ref[...] = jnp.exp(dt * (-jnp.exp(alog_ref[...])))
    dtx_ref[...] = xs * _expand_heads(dt)
    pb = _dot(xn, wb_ref[...])
    cv = pb[:, D_MODEL:2 * D_MODEL] * pb[:, 2 * D_MODEL:3 * D_MODEL]
    ws = wsc_ref[...]
    uc = cv * ws[SC_CONV - 1:SC_CONV, :]
    for k in range(SC_CONV - 1):
        uc = uc + sst_ref[k] * ws[k:k + 1, :]
    for k in range(SC_CONV - 2):
        snew_ref[k] = sst_ref[k + 1]
    snew_ref[SC_CONV - 2] = cv
    yb_ref[...] = _sigmoid(pb[:, 4 * D_MODEL:5 * D_MODEL]) * (pb[:, 0:D_MODEL] * uc)
    sga_ref[...] = _sigmoid(pb[:, 3 * D_MODEL:4 * D_MODEL])


def _sample_proj(x, mstate_t, sstate_t, w):
    nb = x.shape[0]
    f = lambda *s: jax.ShapeDtypeStruct(s, F32)
    return pl.pallas_call(
        _sproj_body,
        out_shape=[f(nb, SSM_INNER), f(nb, SSM_INNER), f(nb, SSM_INNER), f(nb, SSM_HEADS),
                   f(nb, SSM_GROUPS * SSM_STATE), f(nb, SSM_GROUPS * SSM_STATE),
                   f(nb, D_MODEL), f(nb, D_MODEL),
                   f(SSM_CONV - 1, nb, SSM_CONV_DIM), f(SC_CONV - 1, nb, D_MODEL)],
        compiler_params=pltpu.CompilerParams(vmem_limit_bytes=VMEM_LIMIT),
        name="sample_proj",
    )(x, w["norm_mix"], w["w_a"], w["w_dt"], w["w_b"], w["w_mconv"], w["b_mconv"], w["dt_bias"],
      w["a_log"], w["w_sconv"], mstate_t, sstate_t)


def _sstate_body(dec_ref, s_ref, dtx_ref, bm_ref, cm_ref, snew_ref, y_ref):
    i = pl.program_id(0)
    rows_per_blk = LANES
    for j in range(STATE_BB):
        b = i * STATE_BB + j
        dtx_row = dtx_ref[j:j + 1, :]
        y_parts = []
        for g in range(SSM_GROUPS):
            b_row = bm_ref[j:j + 1, g * SSM_STATE:(g + 1) * SSM_STATE]
            c_row = cm_ref[j:j + 1, g * SSM_STATE:(g + 1) * SSM_STATE].astype(BF16)
            new_blocks = []
            for q in range(GROUP_WIDTH // rows_per_blk):
                r0 = g * GROUP_WIDTH + q * rows_per_blk
                dcol = jnp.broadcast_to(dtx_row[:, r0:r0 + rows_per_blk], (rows_per_blk, LANES)).T
                sub = []
                for hh in range(rows_per_blk // SSM_HEAD_DIM):
                    h = r0 // SSM_HEAD_DIM + hh
                    lo = hh * SSM_HEAD_DIM
                    s_old = s_ref[j, r0 + lo:r0 + lo + SSM_HEAD_DIM, :]
                    sub.append(s_old * dec_ref[b, h] + dcol[lo:lo + SSM_HEAD_DIM, :] * b_row)
                blk = jnp.concatenate(sub, axis=0)
                snew_ref[j, r0:r0 + rows_per_blk, :] = blk
                new_blocks.append(blk.astype(BF16))
            s_g = jnp.concatenate(new_blocks, axis=0)
            y_parts.append(_dot_nt(c_row, s_g))
        y_ref[j:j + 1, :] = jnp.concatenate(y_parts, axis=1)


def _sample_state(dec, state, dtx, bm, cm):
    nb = state.shape[0]
    row = lambda wdt: pl.BlockSpec((STATE_BB, wdt), lambda i, dec: (i, 0))
    st = pl.BlockSpec((STATE_BB, SSM_INNER, SSM_STATE), lambda i, dec: (i, 0, 0))
    return pl.pallas_call(
        _sstate_body,
        grid_spec=pltpu.PrefetchScalarGridSpec(
            num_scalar_prefetch=1, grid=(nb // STATE_BB,),
            in_specs=[st, row(SSM_INNER), row(SSM_GROUPS * SSM_STATE), row(SSM_GROUPS * SSM_STATE)],
            out_specs=[st, row(SSM_INNER)]),
        out_shape=[jax.ShapeDtypeStruct(state.shape, F32), jax.ShapeDtypeStruct((nb, SSM_INNER), F32)],
        compiler_params=_cparams("arbitrary"),
        name="sample_state",
    )(dec, state, dtx, bm, cm)


def _sfin1_body(x_ref, y_ref, xs_ref, z_ref, yb_ref, sga_ref, dskip_ref, gssm_ref, wout_ref,
                gx_ref, wq_ref, h_ref, q_ref):
    y = y_ref[...] + dskip_ref[...] * xs_ref[...]
    y_a = _group_rmsnorm(y * _silu(z_ref[...]), gssm_ref[...])
    merged = sga_ref[...] * y_a + yb_ref[...]
    h = x_ref[...] + _dot(merged.astype(BF16), wout_ref[...])
    h_ref[...] = h
    q_ref[...] = _dot(_rms(h, gx_ref[...]).astype(BF16), wq_ref[...])


def _sample_fin1(x, y, xs, z, yb, sga, w):
    nb = x.shape[0]
    return pl.pallas_call(
        _sfin1_body,
        out_shape=[jax.ShapeDtypeStruct((nb, D_MODEL), F32)] * 2,
        compiler_params=pltpu.CompilerParams(vmem_limit_bytes=VMEM_LIMIT),
        name="sample_fin1",
    )(x, y, xs, z, yb, sga, w["d_skip"], w["norm_ssm"], w["w_out"], w["norm_xattn"], w["w_xq"])


def _sattn_body(q_ref, k_ref, v_ref, o_ref):
    for j in range(ATTN_BB):
        q_row = q_ref[j]
        q4 = jnp.concatenate([q_row[:, h * XA_HEAD_DIM:(h + 1) * XA_HEAD_DIM]
                              for h in range(XA_HEADS)], axis=0)
        s = jnp.sum(k_ref[j] * q4[None], axis=-1, keepdims=True) * (XA_HEAD_DIM ** -0.5)
        e = jnp.exp(s - jnp.max(s, axis=0, keepdims=True))
        p = e / jnp.sum(e, axis=0, keepdims=True)
        o4 = jnp.sum(p * v_ref[j], axis=0)
        o_ref[j] = jnp.concatenate([o4[h:h + 1, :] for h in range(XA_HEADS)], axis=1)


def _sample_attn(q3, k3, v3):
    nb = q3.shape[0]
    qs = pl.BlockSpec((ATTN_BB, 1, D_MODEL), lambda i: (i, 0, 0))
    kv = pl.BlockSpec((ATTN_BB, N_MEM, XA_HEADS, XA_HEAD_DIM), lambda i: (i, 0, 0, 0))
    return pl.pallas_call(
        _sattn_body,
        grid=(nb // ATTN_BB,),
        in_specs=[qs, kv, kv],
        out_specs=qs,
        out_shape=jax.ShapeDtypeStruct((nb, 1, D_MODEL), F32),
        compiler_params=_cparams("arbitrary"),
        name="sample_attn",
    )(q3, k3, v3)


def _sfin2_body(h_ref, o_ref, wo_ref, gmoe_ref, wr_ref, br_ref, h2_ref, hn_ref, lg_ref):
    h2 = h_ref[...] + _dot(o_ref[...].astype(BF16), wo_ref[...])
    _router_tail(h2, gmoe_ref, wr_ref, br_ref, h2_ref, hn_ref, lg_ref)


def _sample_fin2(h1, o, w):
    nb = h1.shape[0]
    return pl.pallas_call(
        _sfin2_body,
        out_shape=[jax.ShapeDtypeStruct((nb, D_MODEL), F32)] * 2
        + [jax.ShapeDtypeStruct((N_EXPERTS, nb), F32)],
        compiler_params=pltpu.CompilerParams(vmem_limit_bytes=VMEM_LIMIT),
        name="sample_fin2",
    )(h1, o, w["w_xo"], w["norm_moe"], w["w_router"], w["b_router"])


def _pad_cols(x, cols):
    return jnp.concatenate([x, jnp.zeros((x.shape[0], cols - x.shape[1]), x.dtype)], axis=1)


def _route_body(lgp_ref, lgs_ref, g_ref, loc_ref, cnt_ref, off_ref):
    tt = lgp_ref.shape[1]
    is_sample = pl.program_id(0) == pl.num_programs(0) - 1
    col = lax.broadcasted_iota(I32, (1, tt), 1)
    valid = jnp.logical_or(jnp.logical_not(is_sample), col < lgs_ref.shape[1])
    work = jnp.where(is_sample, _pad_cols(lgs_ref[...], tt), lgp_ref[...])
    sub = lax.broadcasted_iota(I32, (N_EXPERTS, tt), 0).astype(F32)
    vals, hots = [], []
    for _ in range(TOP_K):
        m = jnp.max(work, axis=0, keepdims=True)
        idx = jnp.min(jnp.where(work == m, sub, float(N_EXPERTS)), axis=0, keepdims=True)
        hot = (sub == idx) & valid
        vals.append(m)
        hots.append(hot)
        work = jnp.where(hot, -jnp.inf, work)
    exps = [jnp.exp(v - vals[0]) for v in vals]
    tot = exps[0]
    for e in exps[1:]:
        tot = tot + e
    assigned = hots[0]
    for hot in hots[1:]:
        assigned = assigned | hot
    a = assigned.astype(BF16)
    r_i = lax.broadcasted_iota(I32, (tt, tt), 0)
    c_i = lax.broadcasted_iota(I32, (tt, tt), 1)
    rank = _dot(a, (r_i < c_i).astype(BF16))
    cnt = jnp.sum(a.astype(F32), axis=1, keepdims=True)
    cnt = jnp.floor((cnt + (SUBLANES - 1)) * (1.0 / SUBLANES)) * SUBLANES
    e_r = lax.broadcasted_iota(I32, (N_EXPERTS, N_EXPERTS), 0)
    e_c = lax.broadcasted_iota(I32, (N_EXPERTS, N_EXPERTS), 1)
    cnt_cols = jnp.broadcast_to(cnt, (N_EXPERTS, LANES)).astype(BF16)
    off = _dot((e_r > e_c).astype(BF16), cnt_cols)[:, 0:1]
    slot = rank + off
    k_sub = lax.broadcasted_iota(I32, (SUBLANES, tt), 0)
    g_out = jnp.zeros((SUBLANES, tt), F32)
    l_out = jnp.full((SUBLANES, tt), -1.0, F32)
    for k in range(TOP_K):
        lk = jnp.sum(jnp.where(hots[k], slot, 0.0), axis=0, keepdims=True)
        g_out = jnp.where(k_sub == k, jnp.where(valid, exps[k] / tot, 0.0), g_out)
        l_out = jnp.where(k_sub == k, jnp.where(valid, lk, -1.0), l_out)
    g_ref[...] = g_out
    loc_ref[...] = l_out.astype(I32)
    cnt_ref[...] = cnt.astype(I32)
    off_ref[...] = off.astype(I32)


def _route(logits_p, logits_s, tt):
    ntp = logits_p.shape[1] // tt
    nt = ntp + 1
    t = nt * tt
    tk = pl.BlockSpec((SUBLANES, tt), lambda i: (0, i))
    per_tile = pl.BlockSpec((None, N_EXPERTS, 1), lambda i: (i, 0, 0))
    return pl.pallas_call(
        _route_body,
        grid=(nt,),
        in_specs=[pl.BlockSpec((N_EXPERTS, tt), lambda i: (0, jnp.minimum(i, ntp - 1))),
                  pl.BlockSpec(logits_s.shape, lambda i: (0, 0))],
        out_specs=[tk, tk, per_tile, per_tile],
        out_shape=[jax.ShapeDtypeStruct((SUBLANES, t), F32), jax.ShapeDtypeStruct((SUBLANES, t), I32),
                   jax.ShapeDtypeStruct((nt, N_EXPERTS, 1), I32), jax.ShapeDtypeStruct((nt, N_EXPERTS, 1), I32)],
        compiler_params=_cparams("arbitrary"),
        name="moe_route",
    )(logits_p, logits_s)


def _sorted_rows(tt):
    return tt * TOP_K + N_EXPERTS * SUBLANES


def _run_copies(tt, tile, cnt_ref, off_ref, base_ref, make_copy, wait):
    if wait:
        total = off_ref[tile, N_EXPERTS - 1] + cnt_ref[tile, N_EXPERTS - 1]

        @pl.when(total > 0)
        def _():
            make_copy(0, 0, pl.multiple_of(total, SUBLANES)).wait()
        return

    def per_expert(e):
        n = cnt_ref[tile, e]

        @pl.when(n > 0)
        def _():
            make_copy(pl.multiple_of(off_ref[tile, e], SUBLANES),
                      pl.multiple_of(base_ref[tile, e], SUBLANES), pl.multiple_of(n, SUBLANES)).start()

    def four_experts(j, carry):
        for u in range(RUN_COPY_UNROLL):
            per_expert(j * RUN_COPY_UNROLL + u)
        return carry

    lax.fori_loop(0, N_EXPERTS // RUN_COPY_UNROLL, four_experts, 0)


def _dispatch_body(tm, tt, cnt_ref, off_ref, base_ref, zstart_ref, zsize_ref, loc_ref, x_ref, xs_ref, o_hbm,
                   zero_ref, srt_ref, zsem, sems):
    i = pl.program_id(0)
    last = pl.num_programs(0) - 1
    r = _sorted_rows(tt)

    def zero_copy(j):
        n = pl.multiple_of(zsize_ref[j], SUBLANES)
        dst = o_hbm.at[pl.ds(pl.multiple_of(zstart_ref[j], SUBLANES), n), :]
        return pltpu.make_async_copy(zero_ref.at[pl.ds(0, n), :], dst, zsem)

    @pl.when(i == 0)
    def _():
        zero_ref[...] = jnp.zeros_like(zero_ref)

        def start(j, carry):
            @pl.when(zsize_ref[j] > 0)
            def _():
                zero_copy(j).start()
            return carry

        def wait(j, carry):
            @pl.when(zsize_ref[j] > 0)
            def _():
                zero_copy(j).wait()
            return carry

        lax.fori_loop(0, zstart_ref.shape[0], start, 0)
        lax.fori_loop(0, zstart_ref.shape[0], wait, 0)

    loc = loc_ref[...]
    slot_i = lax.broadcasted_iota(I32, (r, tt), 0)
    hit = slot_i == loc[0:1, :]
    for k in range(1, TOP_K):
        hit = hit | (slot_i == loc[k:k + 1, :])
    buf = i % 2
    x = jnp.where(i == last, _pad_rows(xs_ref[...], tt), x_ref[...])
    srt_ref[buf] = _pack_bf16_pairs(_dot(hit.astype(BF16), x.astype(BF16)), is_bf16_valued=True)

    def copies(tile, wait):
        b = tile % 2

        def make_copy(lo, go, size):
            return pltpu.make_async_copy(srt_ref.at[b, pl.ds(lo, size), :], o_hbm.at[pl.ds(go, size), :],
                                         sems.at[b])

        _run_copies(tt, tile, cnt_ref, off_ref, base_ref, make_copy, wait)

    copies(i, False)

    @pl.when(i > 0)
    def _():
        copies(i - 1, True)

    @pl.when(i == last)
    def _():
        copies(i, True)


def _dispatch(cnt, off, base, zero_starts, zero_sizes, loc, hn_p, hn_s, n_rows, tm, tt):
    ntp = hn_p.shape[0] // tt
    smem = pl.BlockSpec(memory_space=pltpu.SMEM)
    return pl.pallas_call(
        functools.partial(_dispatch_body, tm, tt),
        grid_spec=pltpu.PrefetchScalarGridSpec(
            num_scalar_prefetch=0, grid=(ntp + 1,),
            in_specs=[smem, smem, smem, smem, smem,
                      pl.BlockSpec((SUBLANES, tt), lambda i: (0, i)),
                      pl.BlockSpec((tt, D_MODEL), lambda i: (jnp.minimum(i, ntp - 1), 0)),
                      pl.BlockSpec(hn_s.shape, lambda i: (0, 0))],
            out_specs=pl.BlockSpec(memory_space=pl.ANY),
            scratch_shapes=[pltpu.VMEM((tm, D_MODEL // 2), U32),
                            pltpu.VMEM((2, _sorted_rows(tt), D_MODEL // 2), U32),
                            pltpu.SemaphoreType.DMA, pltpu.SemaphoreType.DMA((2,))]),
        out_shape=jax.ShapeDtypeStruct((n_rows, D_MODEL // 2), U32),
        compiler_params=_cparams("arbitrary"),
        name="moe_dispatch",
    )(cnt, off, base, zero_starts, zero_sizes, loc, hn_p, hn_s)


def _expert_body(be_ref, nu_ref, bv_ref, slot_ref, nxt_ref, x_ref, wgu_hbm, bgu_ref, wdn_hbm, bdn_ref,
                 y_ref, wgu_f, wdn_f, wgu_b, wdn_b, sems):
    i = pl.program_id(0)
    tm = x_ref.shape[0]
    valid = bv_ref[i]
    expert = be_ref[i]
    slot = slot_ref[i]

    def weight_copies(e, s):
        return (pltpu.make_async_copy(wgu_hbm.at[e], wgu_f.at[s], sems.at[0, s]),
                pltpu.make_async_copy(wdn_hbm.at[e], wdn_f.at[s], sems.at[1, s]))

    @pl.when(i == 0)
    def _():
        for cp in weight_copies(expert, slot):
            cp.start()

    first = jnp.logical_and(i < nu_ref[0], jnp.logical_or(i == 0, expert != be_ref[jnp.maximum(i - 1, 0)]))

    @pl.when(first)
    def _():
        for cp in weight_copies(expert, slot):
            cp.wait()

        @pl.when(nxt_ref[i] >= 0)
        def _():
            for cp in weight_copies(nxt_ref[i], 1 - slot):
                cp.start()

    def ffn(rows, cast):
        xb = _unpack_bf16_pairs(x_ref[0:rows, :])
        gu_parts = []
        for j in range(2 * D_FF // EXPERT_COL_CHUNK):
            cs = slice(j * EXPERT_COL_CHUNK, (j + 1) * EXPERT_COL_CHUNK)
            if cast:
                wgu_b[:, cs] = wgu_f[slot, :, cs].astype(BF16)
            gu_parts.append(_dot(xb, wgu_b[:, cs]) + bgu_ref[:, cs])
        gate = jnp.minimum(jnp.concatenate(gu_parts[:len(gu_parts) // 2], axis=1), SWIGLU_LIMIT)
        up = jnp.clip(jnp.concatenate(gu_parts[len(gu_parts) // 2:], axis=1), -SWIGLU_LIMIT, SWIGLU_LIMIT)
        act = ((up + 1.0) * (gate * _sigmoid(SWIGLU_ALPHA * gate))).astype(BF16)
        y_parts = []
        for j in range(D_MODEL // EXPERT_COL_CHUNK):
            cs = slice(j * EXPERT_COL_CHUNK, (j + 1) * EXPERT_COL_CHUNK)
            if cast:
                wdn_b[:, cs] = wdn_f[slot, :, cs].astype(BF16)
            y_parts.append(_dot(act, wdn_b[:, cs]) + bdn_ref[:, cs])
        y_ref[0:rows, :] = _pack_bf16_pairs(jnp.concatenate(y_parts, axis=1))
        if rows < tm:
            y_ref[rows:tm, :] = jnp.zeros((tm - rows, D_MODEL // 2), U32)

    quarter = tm // EXPERT_ROW_SPLITS
    for q in range(1, EXPERT_ROW_SPLITS + 1):
        in_q = jnp.logical_and(valid > (q - 1) * quarter, valid <= q * quarter)
        for cast in (True, False):
            @pl.when(jnp.logical_and(in_q, first == cast))
            def _(q=q, cast=cast):
                ffn(q * quarter, cast)

    @pl.when(valid == 0)
    def _():
        y_ref[...] = jnp.zeros_like(y_ref)


def _experts(block_e, n_used, block_valid, block_slot, block_next, xs, wgu, bgu, wdn, bdn, tm):
    n_rows = xs.shape[0]
    return pl.pallas_call(
        _expert_body,
        grid_spec=pltpu.PrefetchScalarGridSpec(
            num_scalar_prefetch=5, grid=(n_rows // tm,),
            in_specs=[pl.BlockSpec((tm, D_MODEL // 2), lambda i, be, nu, *_: (jnp.minimum(i, nu[0] - 1), 0)),
                      pl.BlockSpec(memory_space=pl.ANY),
                      pl.BlockSpec((None, 1, 2 * D_FF), lambda i, be, *_: (be[i], 0, 0)),
                      pl.BlockSpec(memory_space=pl.ANY),
                      pl.BlockSpec((None, 1, D_MODEL), lambda i, be, *_: (be[i], 0, 0))],
            out_specs=pl.BlockSpec((tm, D_MODEL // 2), lambda i, *_: (i, 0)),
            scratch_shapes=[pltpu.VMEM((2, D_MODEL, 2 * D_FF), F32), pltpu.VMEM((2, D_FF, D_MODEL), F32),
                            pltpu.VMEM((D_MODEL, 2 * D_FF), BF16), pltpu.VMEM((D_FF, D_MODEL), BF16),
                            pltpu.SemaphoreType.DMA((2, 2))]),
        out_shape=jax.ShapeDtypeStruct((n_rows, D_MODEL // 2), U32),
        compiler_params=_cparams("arbitrary"),
        name="moe_experts",
    )(block_e, n_used, block_valid, block_slot, block_next, xs, wgu, bgu, wdn, bdn)


def _combine_body(tt, cnt_ref, off_ref, base_ref, loc_ref, g_ref, h_ref, hs_ref, gfin_ref, ys_hbm,
                  y_ref, ysmp_ref, buf_ref, sems):
    i = pl.program_id(0)
    last = pl.num_programs(0) - 1
    r = _sorted_rows(tt)

    def copies(tile, wait):
        b = tile % 2

        def make_copy(lo, go, size):
            return pltpu.make_async_copy(ys_hbm.at[pl.ds(go, size), :], buf_ref.at[b, pl.ds(lo, size), :],
                                         sems.at[b])

        _run_copies(tt, tile, cnt_ref, off_ref, base_ref, make_copy, wait)

    @pl.when(i == 0)
    def _():
        buf_ref[...] = jnp.zeros_like(buf_ref)
        copies(0, False)

    @pl.when(i < last)
    def _():
        copies(i + 1, False)

    copies(i, True)
    loc = loc_ref[...]
    gates = g_ref[...]
    slot_i = lax.broadcasted_iota(I32, (r, tt), 0)
    gmat = jnp.where(slot_i == loc[0:1, :], gates[0:1, :], 0.0)
    for k in range(1, TOP_K):
        gmat = gmat + jnp.where(slot_i == loc[k:k + 1, :], gates[k:k + 1, :], 0.0)
    h = jnp.where(i == last, _pad_rows(hs_ref[...], tt), h_ref[...])
    moe = lax.dot_general(gmat.astype(BF16), _unpack_bf16_pairs(buf_ref[i % 2]), TN_DIMS,
                          preferred_element_type=F32)
    y = _rms(h + moe, gfin_ref[...])

    @pl.when(i < last)
    def _():
        y_ref[...] = y

    @pl.when(i == last)
    def _():
        ysmp_ref[...] = y[0:ysmp_ref.shape[0], :]


def _combine(cnt, off, base, loc, gates, h2_p, h2_s, norm_final, ys, tt):
    ntp = h2_p.shape[0] // tt
    tok = pl.BlockSpec((tt, D_MODEL), lambda i: (jnp.minimum(i, ntp - 1), 0))
    smp = pl.BlockSpec(h2_s.shape, lambda i: (0, 0))
    tk = pl.BlockSpec((SUBLANES, tt), lambda i: (0, i))
    smem = pl.BlockSpec(memory_space=pltpu.SMEM)
    return pl.pallas_call(
        functools.partial(_combine_body, tt),
        grid_spec=pltpu.PrefetchScalarGridSpec(
            num_scalar_prefetch=0, grid=(ntp + 1,),
            in_specs=[smem, smem, smem, tk, tk, tok, smp,
                      pl.BlockSpec((1, D_MODEL), lambda i: (0, 0)),
                      pl.BlockSpec(memory_space=pl.ANY)],
            out_specs=[tok, smp],
            scratch_shapes=[pltpu.VMEM((2, _sorted_rows(tt), D_MODEL // 2), U32), pltpu.SemaphoreType.DMA((2,))]),
        out_shape=[jax.ShapeDtypeStruct(h2_p.shape, F32), jax.ShapeDtypeStruct(h2_s.shape, F32)],
        compiler_params=_cparams("arbitrary"),
        name="moe_combine",
    )(cnt, off, base, loc, gates, h2_p, h2_s, norm_final, ys)


def _moe_and_final_norm(hn_p, logits_p, h2_p, hn_s, logits_s, h2_s, w, tt, tm):
    nt = hn_p.shape[0] // tt + 1
    t = hn_p.shape[0] + hn_s.shape[0]
    gates, loc, cnt3, off3 = _route(logits_p, logits_s, tt)
    cnt = cnt3[:, :, 0]
    counts = jnp.sum(cnt, axis=0)
    padded = (counts + tm - 1) // tm * tm
    pad_end = jnp.cumsum(padded)
    start = pad_end - padded
    off = off3[:, :, 0]
    base = (start[None, :] + jnp.cumsum(cnt, axis=0) - cnt).astype(I32)
    n_blocks = (t * TOP_K + nt * N_EXPERTS * (SUBLANES - 1) + N_EXPERTS * (tm - 1)) // tm
    n_rows = n_blocks * tm
    block_start = jnp.arange(n_blocks, dtype=I32) * tm
    block_e = jnp.minimum(jnp.sum(block_start[:, None] >= pad_end[None, :], axis=-1), N_EXPERTS - 1).astype(I32)
    n_used = (pad_end[-1:] // tm).astype(I32)
    zero_starts = jnp.concatenate([start + counts, block_start]).astype(I32)
    zero_sizes = jnp.concatenate([padded - counts,
                                  jnp.where(block_start >= pad_end[-1], tm, 0)]).astype(I32)
    xs = _dispatch(cnt, off, base, zero_starts, zero_sizes, loc, hn_p, hn_s, n_rows, tm, tt)
    e_ids = jnp.arange(N_EXPERTS, dtype=I32)
    block_hot = block_e[:, None] == e_ids[None, :]

    def per_block(table):
        return jnp.sum(jnp.where(block_hot, table[None, :], 0), axis=1).astype(I32)

    block_valid = jnp.clip(per_block(start + counts) - block_start, 0, tm)
    block_valid = jnp.where(block_start < pad_end[-1], block_valid, 0).astype(I32)
    present = padded > 0
    later = present[None, :] & (e_ids[None, :] > e_ids[:, None])
    next_e = jnp.min(jnp.where(later, e_ids[None, :], N_EXPERTS), axis=1)
    next_e = jnp.where(next_e < N_EXPERTS, next_e, -1).astype(I32)
    before = present[None, :] & (e_ids[None, :] < e_ids[:, None])
    run_slot = (jnp.sum(before.astype(I32), axis=1) % 2).astype(I32)
    ys = _experts(block_e, n_used, block_valid, per_block(run_slot), per_block(next_e), xs, w["w_gate_up"], w["b_gate_up"], w["w_down"], w["b_down"], tm)
    return _combine(cnt, off, base, loc, gates, h2_p, h2_s, w["norm_final"], ys, tt)


def kernel(x_prompt, x_sample, mem_prompt, state_ssm, state_mamba_conv, state_short_conv, cache_mem_k, cache_mem_v, norm_mix, w_in, w_mconv, b_mconv, dt_bias, a_log, d_skip, norm_ssm, w_sconv, w_out, norm_xattn, norm_mem, w_xq, w_xk, w_xv, w_xo, norm_moe, w_router, b_router, w_gate_up, b_gate_up, w_down, b_down, norm_final):
    nbp, seq, _ = x_prompt.shape
    nbs = x_sample.shape[0]
    dt_lo = SSM_INNER + SSM_CONV_DIM
    w_in0 = w_in[0]
    w_dt = w_in0[:, dt_lo:dt_lo + SSM_HEADS]
    w = {
        "norm_mix": norm_mix, "norm_ssm": norm_ssm, "norm_xattn": norm_xattn, "norm_moe": norm_moe,
        "norm_final": norm_final.reshape(1, D_MODEL),
        "w_a": w_in0[:, :dt_lo].astype(BF16),
        "w_dt": w_dt.astype(BF16), "w_dt_t": w_dt.T.astype(BF16),
        "w_b": w_in0[:, dt_lo + SSM_HEADS:].astype(BF16),
        "w_mconv": w_mconv[0], "b_mconv": b_mconv,
        "dt_bias": dt_bias, "dt_bias_t": dt_bias.reshape(SSM_HEADS, 1),
        "a_log": a_log, "a_log_t": a_log.reshape(SSM_HEADS, 1),
        "d_skip": jnp.repeat(d_skip, SSM_HEAD_DIM, axis=1),
        "w_sconv": w_sconv[0], "w_out": w_out[0].astype(BF16),
        "w_xq": w_xq[0].astype(BF16), "w_xo": w_xo[0].astype(BF16),
        "w_router": w_router[0].T.astype(BF16), "b_router": b_router.reshape(N_EXPERTS, 1),
        "w_gate_up": w_gate_up[0], "b_gate_up": b_gate_up[0].reshape(N_EXPERTS, 1, 2 * D_FF),
        "w_down": w_down[0], "b_down": b_down[0].reshape(N_EXPERTS, 1, D_MODEL),
    }

    k_p, v_p, kb, vb = _mem_kv(mem_prompt.reshape(nbp * N_MEM, D_MODEL), norm_mem,
                               w_xk[0].astype(BF16), w_xv[0].astype(BF16))
    h1, ssm_p, mconv_p, sconv_p = _prompt_mixer(x_prompt.reshape(nbp * seq, D_MODEL), nbp, w)
    h2, hn, logits = _prompt_attn(h1, kb, vb, nbp, w)

    xs2 = x_sample.reshape(nbs, D_MODEL)
    mstate_t = jnp.transpose(state_mamba_conv[0], (1, 0, 2))
    sstate_t = jnp.transpose(state_short_conv[0], (1, 0, 2))
    z, xs_, dtx, dec, bm, cm, yb, sga, mnew_t, snew_t = _sample_proj(xs2, mstate_t, sstate_t, w)
    ssm_s, y_s = _sample_state(dec, state_ssm[0].reshape(nbs, SSM_INNER, SSM_STATE), dtx, bm, cm)
    h1s, q_s = _sample_fin1(xs2, y_s, xs_, z, yb, sga, w)
    o_s = _sample_attn(q_s.reshape(nbs, 1, D_MODEL),
                       cache_mem_k[0], cache_mem_v[0])
    h2s, hns, logits_s = _sample_fin2(h1s, o_s.reshape(nbs, D_MODEL), w)
    y_prompt, y_sample = _moe_and_final_norm(hn, logits, h2, hns, logits_s, h2s, w, MIX_TILE, MOE_ROW_TILE)

    return (y_prompt.reshape(nbp, seq, D_MODEL),
            y_sample.reshape(nbs, 1, D_MODEL),
            ssm_p.reshape(1, nbp, SSM_HEADS, SSM_HEAD_DIM, SSM_STATE),
            mconv_p[None], sconv_p[None],
            k_p.reshape(1, nbp, N_MEM, XA_HEADS, XA_HEAD_DIM),
            v_p.reshape(1, nbp, N_MEM, XA_HEADS, XA_HEAD_DIM),
            ssm_s.reshape(1, nbs, SSM_HEADS, SSM_HEAD_DIM, SSM_STATE),
            jnp.transpose(mnew_t, (1, 0, 2))[None],
            jnp.transpose(snew_t, (1, 0, 2))[None])
```

```python
import functools

import jax
import jax.numpy as jnp
from jax import lax
from jax.experimental import pallas as pl
from jax.experimental.pallas import tpu as pltpu

F32 = jnp.float32
BF16 = jnp.bfloat16
I32 = jnp.int32
U32 = jnp.uint32

D_MODEL = 1024
N_MEM = 256
SSM_HEADS = 16
SSM_HEAD_DIM = 64
SSM_INNER = SSM_HEADS * SSM_HEAD_DIM
SSM_STATE = 128
SSM_GROUPS = 4
HEADS_PER_GROUP = SSM_HEADS // SSM_GROUPS
GROUP_WIDTH = SSM_INNER // SSM_GROUPS
SSM_CONV = 4
SSM_CONV_DIM = SSM_INNER + 2 * SSM_GROUPS * SSM_STATE
SC_CONV = 3
XA_HEADS = 4
XA_HEAD_DIM = D_MODEL // XA_HEADS
N_EXPERTS = 32
TOP_K = 4
D_FF = D_MODEL
SWIGLU_LIMIT = 7.0
SWIGLU_ALPHA = 1.702
EPS = 1e-6

LANES = 128
SUBLANES = 8
V7X_VMEM_BYTES = 64 * 1024 * 1024
VMEM_LIMIT = V7X_VMEM_BYTES * 7 // 8
HIGH_HALF = 0xFFFF0000

MIX_TILE = 256
ATTN_TILE = 1024
MOE_ROW_TILE = 512
RUN_COPY_UNROLL = 4
EXPERT_ROW_SPLITS = 4
EXPERT_COL_CHUNK = 512
STATE_BB = 8
ATTN_BB = 4

NT_DIMS = (((1,), (1,)), ((), ()))
TN_DIMS = (((0,), (0,)), ((), ()))


def _cparams(*sem):
    return pltpu.CompilerParams(dimension_semantics=sem, vmem_limit_bytes=VMEM_LIMIT)


def _const_spec(shape):
    nd = len(shape)
    return pl.BlockSpec(shape, lambda *_: (0,) * nd, pipeline_mode=pl.Buffered(1))


def _sigmoid(x):
    return 0.5 * jnp.tanh(0.5 * x) + 0.5


def _silu(x):
    return x * _sigmoid(x)


def _softplus(x):
    return jnp.maximum(x, 0.0) + jnp.log(1.0 + jnp.exp(-jnp.abs(x)))


def _rms(x, g):
    ms = jnp.mean(x * x, axis=-1, keepdims=True)
    return x * lax.rsqrt(ms + EPS) * g


def _dot(a, b):
    return jnp.dot(a, b, preferred_element_type=F32)


def _dot_nt(a, b):
    return lax.dot_general(a, b, NT_DIMS, preferred_element_type=F32)


def _expand_heads(v):
    assert LANES == 2 * SSM_HEAD_DIM
    rows = v.shape[0]
    lane = lax.broadcasted_iota(I32, (rows, LANES), 1)
    pieces = []
    for j in range(SSM_HEADS // 2):
        a = jnp.broadcast_to(v[:, 2 * j:2 * j + 1], (rows, LANES))
        b = jnp.broadcast_to(v[:, 2 * j + 1:2 * j + 2], (rows, LANES))
        pieces.append(jnp.where(lane < SSM_HEAD_DIM, a, b))
    return jnp.concatenate(pieces, axis=1)


def _cumsum(x, axis):
    idx = lax.broadcasted_iota(I32, x.shape, axis)
    shift = 1
    while shift < x.shape[axis]:
        x = x + jnp.where(idx >= shift, pltpu.roll(x, shift, axis), 0.0)
        shift *= 2
    return x


def _pack_bf16_pairs(x, is_bf16_valued=False):
    w = x.shape[1] // 2
    if not is_bf16_valued:
        x = x.astype(BF16).astype(F32)
    bits = lax.bitcast_convert_type(x, U32)
    return (bits[:, w:] & jnp.uint32(HIGH_HALF)) | (bits[:, :w] >> 16)


def _unpack_bf16_pairs(p):
    lo = lax.bitcast_convert_type(p << 16, F32)
    hi = lax.bitcast_convert_type(p & jnp.uint32(HIGH_HALF), F32)
    return jnp.concatenate([lo, hi], axis=1).astype(BF16)


def _pad_rows(x, rows):
    return jnp.concatenate([x, jnp.zeros((rows - x.shape[0], x.shape[1]), x.dtype)], axis=0)


def _group_rmsnorm(u, g):
    outs = []
    for k in range(SSM_GROUPS):
        ug = u[:, k * GROUP_WIDTH:(k + 1) * GROUP_WIDTH]
        ms = jnp.mean(ug * ug, axis=-1, keepdims=True)
        outs.append(ug * lax.rsqrt(ms + EPS))
    return jnp.concatenate(outs, axis=1) * g


def _memkv_body(mem_ref, g_ref, wk_ref, wv_ref, k_ref, v_ref, kb_ref, vb_ref):
    mn = _rms(mem_ref[...], g_ref[...]).astype(BF16)
    k = _dot(mn, wk_ref[...])
    v = _dot(mn, wv_ref[...])
    k_ref[...] = k
    v_ref[...] = v
    kb_ref[...] = k.astype(BF16)
    vb_ref[...] = v.astype(BF16)


def _mem_kv(mem2d, norm_mem, wk, wv):
    rows = mem2d.shape[0]
    nb = rows // N_MEM
    blk = pl.BlockSpec((N_MEM, D_MODEL), lambda b: (b, 0))
    return pl.pallas_call(
        _memkv_body,
        grid=(nb,),
        in_specs=[blk, _const_spec((1, D_MODEL)), _const_spec((D_MODEL, D_MODEL)),
                  _const_spec((D_MODEL, D_MODEL))],
        out_specs=[blk, blk, blk, blk],
        out_shape=[jax.ShapeDtypeStruct((rows, D_MODEL), F32)] * 2
        + [jax.ShapeDtypeStruct((rows, D_MODEL), BF16)] * 2,
        compiler_params=_cparams("arbitrary"),
        name="mem_kv",
    )(mem2d, norm_mem, wk, wv)


def _mix_body(x_ref, gmix_ref, wa_ref, wdtc_ref, wdtr_ref, wb_ref, wmc_ref, bmc_ref,
              dtb_ref, dtbt_ref, alog_ref, alogt_ref, dskip_ref, gssm_ref, wsc_ref, wout_ref,
              h_ref, ssm_ref, mbuf_ref, sbuf_ref,
              st_ref, cbuf_ref, scbuf_ref):
    tq = MIX_TILE
    c = pl.program_id(1)

    @pl.when(c == 0)
    def _():
        st_ref[...] = jnp.zeros_like(st_ref)
        cbuf_ref[0:SUBLANES, :] = jnp.zeros((SUBLANES, SSM_CONV_DIM), F32)
        scbuf_ref[0:SUBLANES, :] = jnp.zeros((SUBLANES, D_MODEL), F32)

    x = x_ref[...]
    xn = _rms(x, gmix_ref[...]).astype(BF16)

    u = _dot(xn, wa_ref[:, SSM_INNER:])
    cbuf_ref[SUBLANES:SUBLANES + tq, :] = u
    wm = wmc_ref[...]
    conv = u * wm[SSM_CONV - 1:SSM_CONV, :] + bmc_ref[...]
    for k in range(SSM_CONV - 1):
        off = SUBLANES - (SSM_CONV - 1) + k
        conv = conv + cbuf_ref[off:off + tq, :] * wm[k:k + 1, :]
    tail = cbuf_ref[tq + SUBLANES - (SSM_CONV - 1):tq + SUBLANES, :]
    mbuf_ref[...] = tail
    cbuf_ref[SUBLANES - (SSM_CONV - 1):SUBLANES, :] = tail
    xbc = _silu(conv)
    xs = xbc[:, :SSM_INNER]
    bm = xbc[:, SSM_INNER:SSM_INNER + SSM_GROUPS * SSM_STATE]
    cm = xbc[:, SSM_INNER + SSM_GROUPS * SSM_STATE:]

    dt = _softplus(_dot(xn, wdtc_ref[...]) + dtb_ref[...])
    dtt = _softplus(_dot_nt(wdtr_ref[...], xn) + dtbt_ref[...])
    a_row = -jnp.exp(alog_ref[...])
    a_col = -jnp.exp(alogt_ref[...])
    row_i = lax.broadcasted_iota(I32, (tq, tq), 0)
    col_i = lax.broadcasted_iota(I32, (tq, tq), 1)
    causal = row_i >= col_i
    a_cum = _cumsum(dt * a_row, 0)
    a_cumt = _cumsum(dtt * a_col, 1)
    a_last = a_cum[tq - 1:tq, :]

    xdt = xs * _expand_heads(dt)
    in_decay = _expand_heads(jnp.exp(a_cum))
    to_end = _expand_heads(jnp.exp(a_last - a_cum))
    chunk_decay = _expand_heads(jnp.exp(a_last))
    xdt_b = xdt.astype(BF16)
    xend_b = (xdt * to_end).astype(BF16)
    lane = lax.broadcasted_iota(I32, (tq, LANES), 1)

    def proj_b(k):
        return _dot(xn, wb_ref[:, k * D_MODEL:(k + 1) * D_MODEL])

    pb = []
    y_groups = []
    for g in range(SSM_GROUPS):
        pb.append(proj_b(g))
        if g == 0:
            z = _dot(xn, wa_ref[:, :SSM_INNER])
        if g == 2:
            g_b = proj_b(SSM_GROUPS)
        cg = cm[:, g * SSM_STATE:(g + 1) * SSM_STATE].astype(BF16)
        bg_f = bm[:, g * SSM_STATE:(g + 1) * SSM_STATE]
        bg = bg_f.astype(BF16)
        scores = _dot_nt(cg, bg)
        gs = slice(g * GROUP_WIDTH, (g + 1) * GROUP_WIDTH)
        st_g = st_ref[:, gs]
        y_off = _dot(cg, st_g.astype(BF16)) * in_decay[:, gs]
        pair_out = []
        for pr in range(HEADS_PER_GROUP // 2):
            h0 = g * HEADS_PER_GROUP + 2 * pr
            xp = xdt_b[:, h0 * SSM_HEAD_DIM:(h0 + 2) * SSM_HEAD_DIM]
            ys = []
            for h in (h0, h0 + 1):
                seg = a_cum[:, h:h + 1] - a_cumt[h:h + 1, :]
                decay = jnp.where(causal, jnp.exp(jnp.minimum(seg, 0.0)), 0.0)
                ys.append(_dot((scores * decay).astype(BF16), xp))
            pair_out.append(jnp.where(lane < SSM_HEAD_DIM, ys[0], ys[1]))
        y_groups.append(jnp.concatenate(pair_out, axis=1) + y_off)
        st_ref[:, gs] = st_g * chunk_decay[:, gs] + _dot(bg_f.T.astype(BF16), xend_b[:, gs])
    y = jnp.concatenate(y_groups, axis=1) + dskip_ref[...] * xs
    y_a = _group_rmsnorm(y * _silu(z), gssm_ref[...])

    sc_b, sc_c, sc_v, g_a = pb
    cv = sc_c * sc_v
    scbuf_ref[SUBLANES:SUBLANES + tq, :] = cv
    ws = wsc_ref[...]
    uc = cv * ws[SC_CONV - 1:SC_CONV, :]
    for k in range(SC_CONV - 1):
        off = SUBLANES - (SC_CONV - 1) + k
        uc = uc + scbuf_ref[off:off + tq, :] * ws[k:k + 1, :]
    stail = scbuf_ref[tq + SUBLANES - (SC_CONV - 1):tq + SUBLANES, :]
    sbuf_ref[...] = stail
    scbuf_ref[SUBLANES - (SC_CONV - 1):SUBLANES, :] = stail
    merged = _sigmoid(g_a) * y_a + _sigmoid(g_b) * (sc_b * uc)
    h_ref[...] = x + _dot(merged.astype(BF16), wout_ref[...])

    @pl.when(c == pl.num_programs(1) - 1)
    def _():
        ssm_ref[...] = st_ref[...].T


def _prompt_mixer(x2d, nb, w):
    t = x2d.shape[0]
    assert (t // nb) % MIX_TILE == 0, "prompt length must be a multiple of MIX_TILE"
    nc = t // nb // MIX_TILE
    tok = pl.BlockSpec((MIX_TILE, D_MODEL), lambda b, c: (b * nc + c, 0))
    return pl.pallas_call(
        _mix_body,
        grid=(nb, nc),
        in_specs=[tok, _const_spec((1, D_MODEL)),
                  _const_spec((D_MODEL, SSM_INNER + SSM_CONV_DIM)),
                  _const_spec((D_MODEL, SSM_HEADS)), _const_spec((SSM_HEADS, D_MODEL)),
                  _const_spec((D_MODEL, 5 * D_MODEL)),
                  _const_spec((SSM_CONV, SSM_CONV_DIM)), _const_spec((1, SSM_CONV_DIM)),
                  _const_spec((1, SSM_HEADS)), _const_spec((SSM_HEADS, 1)),
                  _const_spec((1, SSM_HEADS)), _const_spec((SSM_HEADS, 1)),
                  _const_spec((1, SSM_INNER)), _const_spec((1, SSM_INNER)),
                  _const_spec((SC_CONV, D_MODEL)), _const_spec((D_MODEL, D_MODEL))],
        out_specs=[tok,
                   pl.BlockSpec((None, SSM_INNER, SSM_STATE), lambda b, c: (b, 0, 0)),
                   pl.BlockSpec((None, SSM_CONV - 1, SSM_CONV_DIM), lambda b, c: (b, 0, 0)),
                   pl.BlockSpec((None, SC_CONV - 1, D_MODEL), lambda b, c: (b, 0, 0))],
        out_shape=[jax.ShapeDtypeStruct((t, D_MODEL), F32),
                   jax.ShapeDtypeStruct((nb, SSM_INNER, SSM_STATE), F32),
                   jax.ShapeDtypeStruct((nb, SSM_CONV - 1, SSM_CONV_DIM), F32),
                   jax.ShapeDtypeStruct((nb, SC_CONV - 1, D_MODEL), F32)],
        scratch_shapes=[pltpu.VMEM((SSM_STATE, SSM_INNER), F32),
                        pltpu.VMEM((MIX_TILE + SUBLANES, SSM_CONV_DIM), F32),
                        pltpu.VMEM((MIX_TILE + SUBLANES, D_MODEL), F32)],
        compiler_params=_cparams("arbitrary", "arbitrary"),
        name="prompt_mixer",
    )(x2d, w["norm_mix"], w["w_a"], w["w_dt"], w["w_dt_t"], w["w_b"], w["w_mconv"], w["b_mconv"],
      w["dt_bias"], w["dt_bias_t"], w["a_log"], w["a_log_t"], w["d_skip"], w["norm_ssm"],
      w["w_sconv"], w["w_out"])


def _router_tail(h2, gmoe_ref, wr_ref, br_ref, h2_ref, hn_ref, lg_ref):
    h2_ref[...] = h2
    hn = _rms(h2, gmoe_ref[...])
    hn_ref[...] = hn
    lg_ref[...] = _dot_nt(wr_ref[...], hn.astype(BF16)) + br_ref[...]


def _attn_body(h_ref, gx_ref, wq_ref, k_ref, v_ref, wo_ref, gmoe_ref, wr_ref, br_ref,
               h2_ref, hn_ref, lg_ref):
    h = h_ref[...]
    hn = _rms(h, gx_ref[...]).astype(BF16)
    q = _dot(hn, wq_ref[...]).astype(BF16)
    outs = []
    for hd in range(XA_HEADS):
        sl = slice(hd * XA_HEAD_DIM, (hd + 1) * XA_HEAD_DIM)
        s = _dot_nt(q[:, sl], k_ref[:, sl]) * (XA_HEAD_DIM ** -0.5)
        e = jnp.exp(s - jnp.max(s, axis=-1, keepdims=True))
        p = e / jnp.sum(e, axis=-1, keepdims=True)
        outs.append(_dot(p.astype(BF16), v_ref[:, sl]))
    o = jnp.concatenate(outs, axis=1).astype(BF16)
    h2 = h + _dot(o, wo_ref[...])
    _router_tail(h2, gmoe_ref, wr_ref, br_ref, h2_ref, hn_ref, lg_ref)


def _prompt_attn(h1, kb, vb, nb, w):
    t = h1.shape[0]
    assert (t // nb) % ATTN_TILE == 0, "prompt length must be a multiple of ATTN_TILE"
    nc = t // nb // ATTN_TILE
    tok = pl.BlockSpec((ATTN_TILE, D_MODEL), lambda b, c: (b * nc + c, 0))
    kv = pl.BlockSpec((N_MEM, D_MODEL), lambda b, c: (b, 0))
    return pl.pallas_call(
        _attn_body,
        grid=(nb, nc),
        in_specs=[tok, _const_spec((1, D_MODEL)), _const_spec((D_MODEL, D_MODEL)), kv, kv,
                  _const_spec((D_MODEL, D_MODEL)), _const_spec((1, D_MODEL)),
                  _const_spec((N_EXPERTS, D_MODEL)), _const_spec((N_EXPERTS, 1))],
        out_specs=[tok, tok, pl.BlockSpec((N_EXPERTS, ATTN_TILE), lambda b, c: (0, b * nc + c))],
        out_shape=[jax.ShapeDtypeStruct((t, D_MODEL), F32),
                   jax.ShapeDtypeStruct((t, D_MODEL), F32),
                   jax.ShapeDtypeStruct((N_EXPERTS, t), F32)],
        compiler_params=_cparams("arbitrary", "arbitrary"),
        name="prompt_attn",
    )(h1, w["norm_xattn"], w["w_xq"], kb, vb, w["w_xo"], w["norm_moe"], w["w_router"], w["b_router"])


def _sproj_body(x_ref, gmix_ref, wa_ref, wdtc_ref, wb_ref, wmc_ref, bmc_ref, dtb_ref, alog_ref,
                wsc_ref, mst_ref, sst_ref,
                z_ref, xs_ref, dtx_ref, dec_ref, bm_ref, cm_ref, yb_ref, sga_ref, mnew_ref, snew_ref):
    x = x_ref[...]
    xn = _rms(x, gmix_ref[...]).astype(BF16)
    pa = _dot(xn, wa_ref[...])
    z_ref[...] = pa[:, :SSM_INNER]
    u = pa[:, SSM_INNER:]
    wm = wmc_ref[...]
    conv = u * wm[SSM_CONV - 1:SSM_CONV, :] + bmc_ref[...]
    for k in range(SSM_CONV - 1):
        conv = conv + mst_ref[k] * wm[k:k + 1, :]
    for k in range(SSM_CONV - 2):
        mnew_ref[k] = mst_ref[k + 1]
    mnew_ref[SSM_CONV - 2] = u
    xbc = _silu(conv)
    xs = xbc[:, :SSM_INNER]
    xs_ref[...] = xs
    bm_ref[...] = xbc[:, SSM_INNER:SSM_INNER + SSM_GROUPS * SSM_STATE]
    cm_ref[...] = xbc[:, SSM_INNER + SSM_GROUPS * SSM_STATE:]
    dt = _softplus(_dot(xn, wdtc_ref[...]) + dtb_ref[...])
    dec_ref[...] = jnp.exp(dt * (-jnp.exp(alog_ref[...])))
    dtx_ref[...] = xs * _expand_heads(dt)
    pb = _dot(xn, wb_ref[...])
    cv = pb[:, D_MODEL:2 * D_MODEL] * pb[:, 2 * D_MODEL:3 * D_MODEL]
    ws = wsc_ref[...]
    uc = cv * ws[SC_CONV - 1:SC_CONV, :]
    for k in range(SC_CONV - 1):
        uc = uc + sst_ref[k] * ws[k:k + 1, :]
    for k in range(SC_CONV - 2):
        snew_ref[k] = sst_ref[k + 1]
    snew_ref[SC_CONV - 2] = cv
    yb_ref[...] = _sigmoid(pb[:, 4 * D_MODEL:5 * D_MODEL]) * (pb[:, 0:D_MODEL] * uc)
    sga_ref[...] = _sigmoid(pb[:, 3 * D_MODEL:4 * D_MODEL])


def _sample_proj(x, mstate_t, sstate_t, w):
    nb = x.shape[0]
    f = lambda *s: jax.ShapeDtypeStruct(s, F32)
    return pl.pallas_call(
        _sproj_body,
        out_shape=[f(nb, SSM_INNER), f(nb, SSM_INNER), f(nb, SSM_INNER), f(nb, SSM_HEADS),
                   f(nb, SSM_GROUPS * SSM_STATE), f(nb, SSM_GROUPS * SSM_STATE),
                   f(nb, D_MODEL), f(nb, D_MODEL),
                   f(SSM_CONV - 1, nb, SSM_CONV_DIM), f(SC_CONV - 1, nb, D_MODEL)],
        compiler_params=pltpu.CompilerParams(vmem_limit_bytes=VMEM_LIMIT),
        name="sample_proj",
    )(x, w["norm_mix"], w["w_a"], w["w_dt"], w["w_b"], w["w_mconv"], w["b_mconv"], w["dt_bias"],
      w["a_log"], w["w_sconv"], mstate_t, sstate_t)


def _sstate_body(dec_ref, s_ref, dtx_ref, bm_ref, cm_ref, snew_ref, y_ref):
    i = pl.program_id(0)
    rows_per_blk = LANES
    for j in range(STATE_BB):
        b = i * STATE_BB + j
        dtx_row = dtx_ref[j:j + 1, :]
        y_parts = []
        for g in range(SSM_GROUPS):
            b_row = bm_ref[j:j + 1, g * SSM_STATE:(g + 1) * SSM_STATE]
            c_row = cm_ref[j:j + 1, g * SSM_STATE:(g + 1) * SSM_STATE].astype(BF16)
            new_blocks = []
            for q in range(GROUP_WIDTH // rows_per_blk):
                r0 = g * GROUP_WIDTH + q * rows_per_blk
                dcol = jnp.broadcast_to(dtx_row[:, r0:r0 + rows_per_blk], (rows_per_blk, LANES)).T
                sub = []
                for hh in range(rows_per_blk // SSM_HEAD_DIM):
                    h = r0 // SSM_HEAD_DIM + hh
                    lo = hh * SSM_HEAD_DIM
                    s_old = s_ref[j, r0 + lo:r0 + lo + SSM_HEAD_DIM, :]
                    sub.append(s_old * dec_ref[b, h] + dcol[lo:lo + SSM_HEAD_DIM, :] * b_row)
                blk = jnp.concatenate(sub, axis=0)
                snew_ref[j, r0:r0 + rows_per_blk, :] = blk
                new_blocks.append(blk.astype(BF16))
            s_g = jnp.concatenate(new_blocks, axis=0)
            y_parts.append(_dot_nt(c_row, s_g))
        y_ref[j:j + 1, :] = jnp.concatenate(y_parts, axis=1)


def _sample_state(dec, state, dtx, bm, cm):
    nb = state.shape[0]
    row = lambda wdt: pl.BlockSpec((STATE_BB, wdt), lambda i, dec: (i, 0))
    st = pl.BlockSpec((STATE_BB, SSM_INNER, SSM_STATE), lambda i, dec: (i, 0, 0))
    return pl.pallas_call(
        _sstate_body,
        grid_spec=pltpu.PrefetchScalarGridSpec(
            num_scalar_prefetch=1, grid=(nb // STATE_BB,),
            in_specs=[st, row(SSM_INNER), row(SSM_GROUPS * SSM_STATE), row(SSM_GROUPS * SSM_STATE)],
            out_specs=[st, row(SSM_INNER)]),
        out_shape=[jax.ShapeDtypeStruct(state.shape, F32), jax.ShapeDtypeStruct((nb, SSM_INNER), F32)],
        compiler_params=_cparams("arbitrary"),
        name="sample_state",
    )(dec, state, dtx, bm, cm)


def _sfin1_body(x_ref, y_ref, xs_ref, z_ref, yb_ref, sga_ref, dskip_ref, gssm_ref, wout_ref,
                gx_ref, wq_ref, h_ref, q_ref):
    y = y_ref[...] + dskip_ref[...] * xs_ref[...]
    y_a = _group_rmsnorm(y * _silu(z_ref[...]), gssm_ref[...])
    merged = sga_ref[...] * y_a + yb_ref[...]
    h = x_ref[...] + _dot(merged.astype(BF16), wout_ref[...])
    h_ref[...] = h
    q_ref[...] = _dot(_rms(h, gx_ref[...]).astype(BF16), wq_ref[...])


def _sample_fin1(x, y, xs, z, yb, sga, w):
    nb = x.shape[0]
    return pl.pallas_call(
        _sfin1_body,
        out_shape=[jax.ShapeDtypeStruct((nb, D_MODEL), F32)] * 2,
        compiler_params=pltpu.CompilerParams(vmem_limit_bytes=VMEM_LIMIT),
        name="sample_fin1",
    )(x, y, xs, z, yb, sga, w["d_skip"], w["norm_ssm"], w["w_out"], w["norm_xattn"], w["w_xq"])


def _sattn_body(q_ref, k_ref, v_ref, o_ref):
    for j in range(ATTN_BB):
        q_row = q_ref[j]
        q4 = jnp.concatenate([q_row[:, h * XA_HEAD_DIM:(h + 1) * XA_HEAD_DIM]
                              for h in range(XA_HEADS)], axis=0)
        s = jnp.sum(k_ref[j] * q4[None], axis=-1, keepdims=True) * (XA_HEAD_DIM ** -0.5)
        e = jnp.exp(s - jnp.max(s, axis=0, keepdims=True))
        p = e / jnp.sum(e, axis=0, keepdims=True)
        o4 = jnp.sum(p * v_ref[j], axis=0)
        o_ref[j] = jnp.concatenate([o4[h:h + 1, :] for h in range(XA_HEADS)], axis=1)


def _sample_attn(q3, k3, v3):
    nb = q3.shape[0]
    qs = pl.BlockSpec((ATTN_BB, 1, D_MODEL), lambda i: (i, 0, 0))
    kv = pl.BlockSpec((ATTN_BB, N_MEM, XA_HEADS, XA_HEAD_DIM), lambda i: (i, 0, 0, 0))
    return pl.pallas_call(
        _sattn_body,
        grid=(nb // ATTN_BB,),
        in_specs=[qs, kv, kv],
        out_specs=qs,
        out_shape=jax.ShapeDtypeStruct((nb, 1, D_MODEL), F32),
        compiler_params=_cparams("arbitrary"),
        name="sample_attn",
    )(q3, k3, v3)


def _sfin2_body(h_ref, o_ref, wo_ref, gmoe_ref, wr_ref, br_ref, h2_ref, hn_ref, lg_ref):
    h2 = h_ref[...] + _dot(o_ref[...].astype(BF16), wo_ref[...])
    _router_tail(h2, gmoe_ref, wr_ref, br_ref, h2_ref, hn_ref, lg_ref)


def _sample_fin2(h1, o, w):
    nb = h1.shape[0]
    return pl.pallas_call(
        _sfin2_body,
        out_shape=[jax.ShapeDtypeStruct((nb, D_MODEL), F32)] * 2
        + [jax.ShapeDtypeStruct((N_EXPERTS, nb), F32)],
        compiler_params=pltpu.CompilerParams(vmem_limit_bytes=VMEM_LIMIT),
        name="sample_fin2",
    )(h1, o, w["w_xo"], w["norm_moe"], w["w_router"], w["b_router"])


def _pad_cols(x, cols):
    return jnp.concatenate([x, jnp.zeros((x.shape[0], cols - x.shape[1]), x.dtype)], axis=1)


def _route_body(lgp_ref, lgs_ref, g_ref, loc_ref, cnt_ref, off_ref):
    tt = lgp_ref.shape[1]
    is_sample = pl.program_id(0) == pl.num_programs(0) - 1
    col = lax.broadcasted_iota(I32, (1, tt), 1)
    valid = jnp.logical_or(jnp.logical_not(is_sample), col < lgs_ref.shape[1])
    work = jnp.where(is_sample, _pad_cols(lgs_ref[...], tt), lgp_ref[...])
    sub = lax.broadcasted_iota(I32, (N_EXPERTS, tt), 0).astype(F32)
    vals, hots = [], []
    for _ in range(TOP_K):
        m = jnp.max(work, axis=0, keepdims=True)
        idx = jnp.min(jnp.where(work == m, sub, float(N_EXPERTS)), axis=0, keepdims=True)
        hot = (sub == idx) & valid
        vals.append(m)
        hots.append(hot)
        work = jnp.where(hot, -jnp.inf, work)
    exps = [jnp.exp(v - vals[0]) for v in vals]
    tot = exps[0]
    for e in exps[1:]:
        tot = tot + e
    assigned = hots[0]
    for hot in hots[1:]:
        assigned = assigned | hot
    a = assigned.astype(BF16)
    r_i = lax.broadcasted_iota(I32, (tt, tt), 0)
    c_i = lax.broadcasted_iota(I32, (tt, tt), 1)
    rank = _dot(a, (r_i < c_i).astype(BF16))
    cnt = jnp.sum(a.astype(F32), axis=1, keepdims=True)
    cnt = jnp.floor((cnt + (SUBLANES - 1)) * (1.0 / SUBLANES)) * SUBLANES
    e_r = lax.broadcasted_iota(I32, (N_EXPERTS, N_EXPERTS), 0)
    e_c = lax.broadcasted_iota(I32, (N_EXPERTS, N_EXPERTS), 1)
    cnt_cols = jnp.broadcast_to(cnt, (N_EXPERTS, LANES)).astype(BF16)
    off = _dot((e_r > e_c).astype(BF16), cnt_cols)[:, 0:1]
    slot = rank + off
    k_sub = lax.broadcasted_iota(I32, (SUBLANES, tt), 0)
    g_out = jnp.zeros((SUBLANES, tt), F32)
    l_out = jnp.full((SUBLANES, tt), -1.0, F32)
    for k in range(TOP_K):
        lk = jnp.sum(jnp.where(hots[k], slot, 0.0), axis=0, keepdims=True)
        g_out = jnp.where(k_sub == k, jnp.where(valid, exps[k] / tot, 0.0), g_out)
        l_out = jnp.where(k_sub == k, jnp.where(valid, lk, -1.0), l_out)
    g_ref[...] = g_out
    loc_ref[...] = l_out.astype(I32)
    cnt_ref[...] = cnt.astype(I32)
    off_ref[...] = off.astype(I32)


def _route(logits_p, logits_s, tt):
    ntp = logits_p.shape[1] // tt
    nt = ntp + 1
    t = nt * tt
    tk = pl.BlockSpec((SUBLANES, tt), lambda i: (0, i))
    per_tile = pl.BlockSpec((None, N_EXPERTS, 1), lambda i: (i, 0, 0))
    return pl.pallas_call(
        _route_body,
        grid=(nt,),
        in_specs=[pl.BlockSpec((N_EXPERTS, tt), lambda i: (0, jnp.minimum(i, ntp - 1))),
                  pl.BlockSpec(logits_s.shape, lambda i: (0, 0))],
        out_specs=[tk, tk, per_tile, per_tile],
        out_shape=[jax.ShapeDtypeStruct((SUBLANES, t), F32), jax.ShapeDtypeStruct((SUBLANES, t), I32),
                   jax.ShapeDtypeStruct((nt, N_EXPERTS, 1), I32), jax.ShapeDtypeStruct((nt, N_EXPERTS, 1), I32)],
        compiler_params=_cparams("arbitrary"),
        name="moe_route",
    )(logits_p, logits_s)


def _sorted_rows(tt):
    return tt * TOP_K + N_EXPERTS * SUBLANES


def _run_copies(tt, tile, cnt_ref, off_ref, base_ref, make_copy, wait):
    if wait:
        total = off_ref[tile, N_EXPERTS - 1] + cnt_ref[tile, N_EXPERTS - 1]

        @pl.when(total > 0)
        def _():
            make_copy(0, 0, pl.multiple_of(total, SUBLANES)).wait()
        return

    def per_expert(e):
        n = cnt_ref[tile, e]

        @pl.when(n > 0)
        def _():
            make_copy(pl.multiple_of(off_ref[tile, e], SUBLANES),
                      pl.multiple_of(base_ref[tile, e], SUBLANES), pl.multiple_of(n, SUBLANES)).start()

    def four_experts(j, carry):
        for u in range(RUN_COPY_UNROLL):
            per_expert(j * RUN_COPY_UNROLL + u)
        return carry

    lax.fori_loop(0, N_EXPERTS // RUN_COPY_UNROLL, four_experts, 0)


def _dispatch_body(tm, tt, cnt_ref, off_ref, base_ref, zstart_ref, zsize_ref, loc_ref, x_ref, xs_ref, o_hbm,
                   zero_ref, srt_ref, zsem, sems):
    i = pl.program_id(0)
    last = pl.num_programs(0) - 1
    r = _sorted_rows(tt)

    def zero_copy(j):
        n = pl.multiple_of(zsize_ref[j], SUBLANES)
        dst = o_hbm.at[pl.ds(pl.multiple_of(zstart_ref[j], SUBLANES), n), :]
        return pltpu.make_async_copy(zero_ref.at[pl.ds(0, n), :], dst, zsem)

    @pl.when(i == 0)
    def _():
        zero_ref[...] = jnp.zeros_like(zero_ref)

        def start(j, carry):
            @pl.when(zsize_ref[j] > 0)
            def _():
                zero_copy(j).start()
            return carry

        def wait(j, carry):
            @pl.when(zsize_ref[j] > 0)
            def _():
                zero_copy(j).wait()
            return carry

        lax.fori_loop(0, zstart_ref.shape[0], start, 0)
        lax.fori_loop(0, zstart_ref.shape[0], wait, 0)

    loc = loc_ref[...]
    slot_i = lax.broadcasted_iota(I32, (r, tt), 0)
    hit = slot_i == loc[0:1, :]
    for k in range(1, TOP_K):
        hit = hit | (slot_i == loc[k:k + 1, :])
    buf = i % 2
    x = jnp.where(i == last, _pad_rows(xs_ref[...], tt), x_ref[...])
    srt_ref[buf] = _pack_bf16_pairs(_dot(hit.astype(BF16), x.astype(BF16)), is_bf16_valued=True)

    def copies(tile, wait):
        b = tile % 2

        def make_copy(lo, go, size):
            return pltpu.make_async_copy(srt_ref.at[b, pl.ds(lo, size), :], o_hbm.at[pl.ds(go, size), :],
                                         sems.at[b])

        _run_copies(tt, tile, cnt_ref, off_ref, base_ref, make_copy, wait)

    copies(i, False)

    @pl.when(i > 0)
    def _():
        copies(i - 1, True)

    @pl.when(i == last)
    def _():
        copies(i, True)


def _dispatch(cnt, off, base, zero_starts, zero_sizes, loc, hn_p, hn_s, n_rows, tm, tt):
    ntp = hn_p.shape[0] // tt
    smem = pl.BlockSpec(memory_space=pltpu.SMEM)
    return pl.pallas_call(
        functools.partial(_dispatch_body, tm, tt),
        grid_spec=pltpu.PrefetchScalarGridSpec(
            num_scalar_prefetch=0, grid=(ntp + 1,),
            in_specs=[smem, smem, smem, smem, smem,
                      pl.BlockSpec((SUBLANES, tt), lambda i: (0, i)),
                      pl.BlockSpec((tt, D_MODEL), lambda i: (jnp.minimum(i, ntp - 1), 0)),
                      pl.BlockSpec(hn_s.shape, lambda i: (0, 0))],
            out_specs=pl.BlockSpec(memory_space=pl.ANY),
            scratch_shapes=[pltpu.VMEM((tm, D_MODEL // 2), U32),
                            pltpu.VMEM((2, _sorted_rows(tt), D_MODEL // 2), U32),
                            pltpu.SemaphoreType.DMA, pltpu.SemaphoreType.DMA((2,))]),
        out_shape=jax.ShapeDtypeStruct((n_rows, D_MODEL // 2), U32),
        compiler_params=_cparams("arbitrary"),
        name="moe_dispatch",
    )(cnt, off, base, zero_starts, zero_sizes, loc, hn_p, hn_s)


def _expert_body(be_ref, nu_ref, bv_ref, slot_ref, nxt_ref, x_ref, wgu_hbm, bgu_ref, wdn_hbm, bdn_ref,
                 y_ref, wgu_f, wdn_f, wgu_b, wdn_b, sems):
    i = pl.program_id(0)
    tm = x_ref.shape[0]
    valid = bv_ref[i]
    expert = be_ref[i]
    slot = slot_ref[i]

    def weight_copies(e, s):
        return (pltpu.make_async_copy(wgu_hbm.at[e], wgu_f.at[s], sems.at[0, s]),
                pltpu.make_async_copy(wdn_hbm.at[e], wdn_f.at[s], sems.at[1, s]))

    @pl.when(i == 0)
    def _():
        for cp in weight_copies(expert, slot):
            cp.start()

    first = jnp.logical_and(i < nu_ref[0], jnp.logical_or(i == 0, expert != be_ref[jnp.maximum(i - 1, 0)]))

    @pl.when(first)
    def _():
        for cp in weight_copies(expert, slot):
            cp.wait()

        @pl.when(nxt_ref[i] >= 0)
        def _():
            for cp in weight_copies(nxt_ref[i], 1 - slot):
                cp.start()

    def ffn(rows, cast):
        xb = _unpack_bf16_pairs(x_ref[0:rows, :])
        gu_parts = []
        for j in range(2 * D_FF // EXPERT_COL_CHUNK):
            cs = slice(j * EXPERT_COL_CHUNK, (j + 1) * EXPERT_COL_CHUNK)
            if cast:
                wgu_b[:, cs] = wgu_f[slot, :, cs].astype(BF16)
            gu_parts.append(_dot(xb, wgu_b[:, cs]) + bgu_ref[:, cs])
        gate = jnp.minimum(jnp.concatenate(gu_parts[:len(gu_parts) // 2], axis=1), SWIGLU_LIMIT)
        up = jnp.clip(jnp.concatenate(gu_parts[len(gu_parts) // 2:], axis=1), -SWIGLU_LIMIT, SWIGLU_LIMIT)
        act = ((up + 1.0) * (gate * _sigmoid(SWIGLU_ALPHA * gate))).astype(BF16)
        y_parts = []
        for j in range(D_MODEL // EXPERT_COL_CHUNK):
            cs = slice(j * EXPERT_COL_CHUNK, (j + 1) * EXPERT_COL_CHUNK)
            if cast:
                wdn_b[:, cs] = wdn_f[slot, :, cs].astype(BF16)
            y_parts.append(_dot(act, wdn_b[:, cs]) + bdn_ref[:, cs])
        y_ref[0:rows, :] = _pack_bf16_pairs(jnp.concatenate(y_parts, axis=1))
        if rows < tm:
            y_ref[rows:tm, :] = jnp.zeros((tm - rows, D_MODEL // 2), U32)

    quarter = tm // EXPERT_ROW_SPLITS
    for q in range(1, EXPERT_ROW_SPLITS + 1):
        in_q = jnp.logical_and(valid > (q - 1) * quarter, valid <= q * quarter)
        for cast in (True, False):
            @pl.when(jnp.logical_and(in_q, first == cast))
            def _(q=q, cast=cast):
                ffn(q * quarter, cast)

    @pl.when(valid == 0)
    def _():
        y_ref[...] = jnp.zeros_like(y_ref)


def _experts(block_e, n_used, block_valid, block_slot, block_next, xs, wgu, bgu, wdn, bdn, tm):
    n_rows = xs.shape[0]
    return pl.pallas_call(
        _expert_body,
        grid_spec=pltpu.PrefetchScalarGridSpec(
            num_scalar_prefetch=5, grid=(n_rows // tm,),
            in_specs=[pl.BlockSpec((tm, D_MODEL // 2), lambda i, be, nu, *_: (jnp.minimum(i, nu[0] - 1), 0)),
                      pl.BlockSpec(memory_space=pl.ANY),
                      pl.BlockSpec((None, 1, 2 * D_FF), lambda i, be, *_: (be[i], 0, 0)),
                      pl.BlockSpec(memory_space=pl.ANY),
                      pl.BlockSpec((None, 1, D_MODEL), lambda i, be, *_: (be[i], 0, 0))],
            out_specs=pl.BlockSpec((tm, D_MODEL // 2), lambda i, *_: (i, 0)),
            scratch_shapes=[pltpu.VMEM((2, D_MODEL, 2 * D_FF), F32), pltpu.VMEM((2, D_FF, D_MODEL), F32),
                            pltpu.VMEM((D_MODEL, 2 * D_FF), BF16), pltpu.VMEM((D_FF, D_MODEL), BF16),
                            pltpu.SemaphoreType.DMA((2, 2))]),
        out_shape=jax.ShapeDtypeStruct((n_rows, D_MODEL // 2), U32),
        compiler_params=_cparams("arbitrary"),
        name="moe_experts",
    )(block_e, n_used, block_valid, block_slot, block_next, xs, wgu, bgu, wdn, bdn)


def _combine_body(tt, cnt_ref, off_ref, base_ref, loc_ref, g_ref, h_ref, hs_ref, gfin_ref, ys_hbm,
                  y_ref, ysmp_ref, buf_ref, sems):
    i = pl.program_id(0)
    last = pl.num_programs(0) - 1
    r = _sorted_rows(tt)

    def copies(tile, wait):
        b = tile % 2

        def make_copy(lo, go, size):
            return pltpu.make_async_copy(ys_hbm.at[pl.ds(go, size), :], buf_ref.at[b, pl.ds(lo, size), :],
                                         sems.at[b])

        _run_copies(tt, tile, cnt_ref, off_ref, base_ref, make_copy, wait)

    @pl.when(i == 0)
    def _():
        buf_ref[...] = jnp.zeros_like(buf_ref)
        copies(0, False)

    @pl.when(i < last)
    def _():
        copies(i + 1, False)

    copies(i, True)
    loc = loc_ref[...]
    gates = g_ref[...]
    slot_i = lax.broadcasted_iota(I32, (r, tt), 0)
    gmat = jnp.where(slot_i == loc[0:1, :], gates[0:1, :], 0.0)
    for k in range(1, TOP_K):
        gmat = gmat + jnp.where(slot_i == loc[k:k + 1, :], gates[k:k + 1, :], 0.0)
    h = jnp.where(i == last, _pad_rows(hs_ref[...], tt), h_ref[...])
    moe = lax.dot_general(gmat.astype(BF16), _unpack_bf16_pairs(buf_ref[i % 2]), TN_DIMS,
                          preferred_element_type=F32)
    y = _rms(h + moe, gfin_ref[...])

    @pl.when(i < last)
    def _():
        y_ref[...] = y

    @pl.when(i == last)
    def _():
        ysmp_ref[...] = y[0:ysmp_ref.shape[0], :]


def _combine(cnt, off, base, loc, gates, h2_p, h2_s, norm_final, ys, tt):
    ntp = h2_p.shape[0] // tt
    tok = pl.BlockSpec((tt, D_MODEL), lambda i: (jnp.minimum(i, ntp - 1), 0))
    smp = pl.BlockSpec(h2_s.shape, lambda i: (0, 0))
    tk = pl.BlockSpec((SUBLANES, tt), lambda i: (0, i))
    smem = pl.BlockSpec(memory_space=pltpu.SMEM)
    return pl.pallas_call(
        functools.partial(_combine_body, tt),
        grid_spec=pltpu.PrefetchScalarGridSpec(
            num_scalar_prefetch=0, grid=(ntp + 1,),
            in_specs=[smem, smem, smem, tk, tk, tok, smp,
                      pl.BlockSpec((1, D_MODEL), lambda i: (0, 0)),
                      pl.BlockSpec(memory_space=pl.ANY)],
            out_specs=[tok, smp],
            scratch_shapes=[pltpu.VMEM((2, _sorted_rows(tt), D_MODEL // 2), U32), pltpu.SemaphoreType.DMA((2,))]),
        out_shape=[jax.ShapeDtypeStruct(h2_p.shape, F32), jax.ShapeDtypeStruct(h2_s.shape, F32)],
        compiler_params=_cparams("arbitrary"),
        name="moe_combine",
    )(cnt, off, base, loc, gates, h2_p, h2_s, norm_final, ys)


def _moe_and_final_norm(hn_p, logits_p, h2_p, hn_s, logits_s, h2_s, w, tt, tm):
    nt = hn_p.shape[0] // tt + 1
    t = hn_p.shape[0] + hn_s.shape[0]
    gates, loc, cnt3, off3 = _route(logits_p, logits_s, tt)
    cnt = cnt3[:, :, 0]
    counts = jnp.sum(cnt, axis=0)
    padded = (counts + tm - 1) // tm * tm
    pad_end = jnp.cumsum(padded)
    start = pad_end - padded
    off = off3[:, :, 0]
    base = (start[None, :] + jnp.cumsum(cnt, axis=0) - cnt).astype(I32)
    n_blocks = (t * TOP_K + nt * N_EXPERTS * (SUBLANES - 1) + N_EXPERTS * (tm - 1)) // tm
    n_rows = n_blocks * tm
    block_start = jnp.arange(n_blocks, dtype=I32) * tm
    block_e = jnp.minimum(jnp.sum(block_start[:, None] >= pad_end[None, :], axis=-1), N_EXPERTS - 1).astype(I32)
    n_used = (pad_end[-1:] // tm).astype(I32)
    zero_starts = jnp.concatenate([start + counts, block_start]).astype(I32)
    zero_sizes = jnp.concatenate([padded - counts,
                                  jnp.where(block_start >= pad_end[-1], tm, 0)]).astype(I32)
    xs = _dispatch(cnt, off, base, zero_starts, zero_sizes, loc, hn_p, hn_s, n_rows, tm, tt)
    e_ids = jnp.arange(N_EXPERTS, dtype=I32)
    block_hot = block_e[:, None] == e_ids[None, :]

    def per_block(table):
        return jnp.sum(jnp.where(block_hot, table[None, :], 0), axis=1).astype(I32)

    block_valid = jnp.clip(per_block(start + counts) - block_start, 0, tm)
    block_valid = jnp.where(block_start < pad_end[-1], block_valid, 0).astype(I32)
    present = padded > 0
    later = present[None, :] & (e_ids[None, :] > e_ids[:, None])
    next_e = jnp.min(jnp.where(later, e_ids[None, :], N_EXPERTS), axis=1)
    next_e = jnp.where(next_e < N_EXPERTS, next_e, -1).astype(I32)
    before = present[None, :] & (e_ids[None, :] < e_ids[:, None])
    run_slot = (jnp.sum(before.astype(I32), axis=1) % 2).astype(I32)
    ys = _experts(block_e, n_used, block_valid, per_block(run_slot), per_block(next_e), xs, w["w_gate_up"], w["b_gate_up"], w["w_down"], w["b_down"], tm)
    return _combine(cnt, off, base, loc, gates, h2_p, h2_s, w["norm_final"], ys, tt)


def kernel(x_prompt, x_sample, mem_prompt, state_ssm, state_mamba_conv, state_short_conv, cache_mem_k, cache_mem_v, norm_mix, w_in, w_mconv, b_mconv, dt_bias, a_log, d_skip, norm_ssm, w_sconv, w_out, norm_xattn, norm_mem, w_xq, w_xk, w_xv, w_xo, norm_moe, w_router, b_router, w_gate_up, b_gate_up, w_down, b_down, norm_final):
    nbp, seq, _ = x_prompt.shape
    nbs = x_sample.shape[0]
    dt_lo = SSM_INNER + SSM_CONV_DIM
    w_in0 = w_in[0]
    w_dt = w_in0[:, dt_lo:dt_lo + SSM_HEADS]
    w = {
        "norm_mix": norm_mix, "norm_ssm": norm_ssm, "norm_xattn": norm_xattn, "norm_moe": norm_moe,
        "norm_final": norm_final.reshape(1, D_MODEL),
        "w_a": w_in0[:, :dt_lo].astype(BF16),
        "w_dt": w_dt.astype(BF16), "w_dt_t": w_dt.T.astype(BF16),
        "w_b": w_in0[:, dt_lo + SSM_HEADS:].astype(BF16),
        "w_mconv": w_mconv[0], "b_mconv": b_mconv,
        "dt_bias": dt_bias, "dt_bias_t": dt_bias.reshape(SSM_HEADS, 1),
        "a_log": a_log, "a_log_t": a_log.reshape(SSM_HEADS, 1),
        "d_skip": jnp.repeat(d_skip, SSM_HEAD_DIM, axis=1),
        "w_sconv": w_sconv[0], "w_out": w_out[0].astype(BF16),
        "w_xq": w_xq[0].astype(BF16), "w_xo": w_xo[0].astype(BF16),
        "w_router": w_router[0].T.astype(BF16), "b_router": b_router.reshape(N_EXPERTS, 1),
        "w_gate_up": w_gate_up[0], "b_gate_up": b_gate_up[0].reshape(N_EXPERTS, 1, 2 * D_FF),
        "w_down": w_down[0], "b_down": b_down[0].reshape(N_EXPERTS, 1, D_MODEL),
    }

    k_p, v_p, kb, vb = _mem_kv(mem_prompt.reshape(nbp * N_MEM, D_MODEL), norm_mem,
                               w_xk[0].astype(BF16), w_xv[0].astype(BF16))
    h1, ssm_p, mconv_p, sconv_p = _prompt_mixer(x_prompt.reshape(nbp * seq, D_MODEL), nbp, w)
    h2, hn, logits = _prompt_attn(h1, kb, vb, nbp, w)

    xs2 = x_sample.reshape(nbs, D_MODEL)
    mstate_t = jnp.transpose(state_mamba_conv[0], (1, 0, 2))
    sstate_t = jnp.transpose(state_short_conv[0], (1, 0, 2))
    z, xs_, dtx, dec, bm, cm, yb, sga, mnew_t, snew_t = _sample_proj(xs2, mstate_t, sstate_t, w)
    ssm_s, y_s = _sample_state(dec, state_ssm[0].reshape(nbs, SSM_INNER, SSM_STATE), dtx, bm, cm)
    h1s, q_s = _sample_fin1(xs2, y_s, xs_, z, yb, sga, w)
    o_s = _sample_attn(q_s.reshape(nbs, 1, D_MODEL),
                       cache_mem_k[0], cache_mem_v[0])
    h2s, hns, logits_s = _sample_fin2(h1s, o_s.reshape(nbs, D_MODEL), w)
    y_prompt, y_sample = _moe_and_final_norm(hn, logits, h2, hns, logits_s, h2s, w, MIX_TILE, MOE_ROW_TILE)

    return (y_prompt.reshape(nbp, seq, D_MODEL),
            y_sample.reshape(nbs, 1, D_MODEL),
            ssm_p.reshape(1, nbp, SSM_HEADS, SSM_HEAD_DIM, SSM_STATE),
            mconv_p[None], sconv_p[None],
            k_p.reshape(1, nbp, N_MEM, XA_HEADS, XA_HEAD_DIM),
            v_p.reshape(1, nbp, N_MEM, XA_HEADS, XA_HEAD_DIM),
            ssm_s.reshape(1, nbs, SSM_HEADS, SSM_HEAD_DIM, SSM_STATE),
            jnp.transpose(mnew_t, (1, 0, 2))[None],
            jnp.transpose(snew_t, (1, 0, 2))[None])
```

```python
import functools

import jax
import jax.numpy as jnp
from jax import lax
from jax.experimental import pallas as pl
from jax.experimental.pallas import tpu as pltpu

F32 = jnp.float32
BF16 = jnp.bfloat16
I32 = jnp.int32
U32 = jnp.uint32

D_MODEL = 1024
N_MEM = 256
SSM_HEADS = 16
SSM_HEAD_DIM = 64
SSM_INNER = SSM_HEADS * SSM_HEAD_DIM
SSM_STATE = 128
SSM_GROUPS = 4
HEADS_PER_GROUP = SSM_HEADS // SSM_GROUPS
GROUP_WIDTH = SSM_INNER // SSM_GROUPS
SSM_CONV = 4
SSM_CONV_DIM = SSM_INNER + 2 * SSM_GROUPS * SSM_STATE
SC_CONV = 3
XA_HEADS = 4
XA_HEAD_DIM = D_MODEL // XA_HEADS
N_EXPERTS = 32
TOP_K = 4
D_FF = D_MODEL
SWIGLU_LIMIT = 7.0
SWIGLU_ALPHA = 1.702
EPS = 1e-6

LANES = 128
SUBLANES = 8
V7X_VMEM_BYTES = 64 * 1024 * 1024
VMEM_LIMIT = V7X_VMEM_BYTES * 7 // 8
HIGH_HALF = 0xFFFF0000

MIX_TILE = 256
ATTN_TILE = 1024
MOE_ROW_TILE = 512
RUN_COPY_UNROLL = 4
DISPATCH_BUFFERS = 3
EXPERT_ROW_SPLITS = 4
EXPERT_COL_CHUNK = 512
STATE_BB = 8
ATTN_BB = 4

NT_DIMS = (((1,), (1,)), ((), ()))
TN_DIMS = (((0,), (0,)), ((), ()))


def _cparams(*sem):
    return pltpu.CompilerParams(dimension_semantics=sem, vmem_limit_bytes=VMEM_LIMIT)


def _const_spec(shape):
    nd = len(shape)
    return pl.BlockSpec(shape, lambda *_: (0,) * nd, pipeline_mode=pl.Buffered(1))


def _sigmoid(x):
    return 0.5 * jnp.tanh(0.5 * x) + 0.5


def _x_sigmoid(x, scale=1.0):
    h = 0.5 * x
    return h * jnp.tanh(h if scale == 1.0 else scale * h) + h


def _silu(x):
    return _x_sigmoid(x)


def _softplus(x):
    return jnp.maximum(x, 0.0) + jnp.log(1.0 + jnp.exp(-jnp.abs(x)))


def _rms(x, g):
    ms = jnp.mean(x * x, axis=-1, keepdims=True)
    return x * lax.rsqrt(ms + EPS) * g


def _dot(a, b):
    return jnp.dot(a, b, preferred_element_type=F32)


def _dot_nt(a, b):
    return lax.dot_general(a, b, NT_DIMS, preferred_element_type=F32)


def _expand_heads(v):
    assert LANES == 2 * SSM_HEAD_DIM
    rows = v.shape[0]
    lane = lax.broadcasted_iota(I32, (rows, LANES), 1)
    pieces = []
    for j in range(SSM_HEADS // 2):
        a = jnp.broadcast_to(v[:, 2 * j:2 * j + 1], (rows, LANES))
        b = jnp.broadcast_to(v[:, 2 * j + 1:2 * j + 2], (rows, LANES))
        pieces.append(jnp.where(lane < SSM_HEAD_DIM, a, b))
    return jnp.concatenate(pieces, axis=1)


def _cumsum(x, axis):
    idx = lax.broadcasted_iota(I32, x.shape, axis)
    shift = 1
    while shift < x.shape[axis]:
        x = x + jnp.where(idx >= shift, pltpu.roll(x, shift, axis), 0.0)
        shift *= 2
    return x


def _pack_bf16_pairs(x, is_bf16_valued=False):
    w = x.shape[1] // 2
    if not is_bf16_valued:
        x = x.astype(BF16).astype(F32)
    bits = lax.bitcast_convert_type(x, U32)
    return (bits[:, w:] & jnp.uint32(HIGH_HALF)) | (bits[:, :w] >> 16)


def _unpack_bf16_pairs(p):
    lo = lax.bitcast_convert_type(p << 16, F32)
    hi = lax.bitcast_convert_type(p & jnp.uint32(HIGH_HALF), F32)
    return jnp.concatenate([lo, hi], axis=1).astype(BF16)


def _pad_rows(x, rows):
    return jnp.concatenate([x, jnp.zeros((rows - x.shape[0], x.shape[1]), x.dtype)], axis=0)


def _group_rmsnorm(u, g):
    outs = []
    for k in range(SSM_GROUPS):
        ug = u[:, k * GROUP_WIDTH:(k + 1) * GROUP_WIDTH]
        ms = jnp.mean(ug * ug, axis=-1, keepdims=True)
        outs.append(ug * lax.rsqrt(ms + EPS))
    return jnp.concatenate(outs, axis=1) * g


def _memkv_body(mem_ref, g_ref, wk_ref, wv_ref, k_ref, v_ref, kb_ref, vb_ref):
    mn = _rms(mem_ref[...], g_ref[...]).astype(BF16)
    k = _dot(mn, wk_ref[...])
    v = _dot(mn, wv_ref[...])
    k_ref[...] = k
    v_ref[...] = v
    kb_ref[...] = k.astype(BF16)
    vb_ref[...] = v.astype(BF16)


def _mem_kv(mem2d, norm_mem, wk, wv):
    rows = mem2d.shape[0]
    nb = rows // N_MEM
    blk = pl.BlockSpec((N_MEM, D_MODEL), lambda b: (b, 0))
    return pl.pallas_call(
        _memkv_body,
        grid=(nb,),
        in_specs=[blk, _const_spec((1, D_MODEL)), _const_spec((D_MODEL, D_MODEL)),
                  _const_spec((D_MODEL, D_MODEL))],
        out_specs=[blk, blk, blk, blk],
        out_shape=[jax.ShapeDtypeStruct((rows, D_MODEL), F32)] * 2
        + [jax.ShapeDtypeStruct((rows, D_MODEL), BF16)] * 2,
        compiler_params=_cparams("arbitrary"),
        name="mem_kv",
    )(mem2d, norm_mem, wk, wv)


def _mix_body(x_ref, gmix_ref, wa_ref, wdtc_ref, wdtr_ref, wb_ref, wmc_ref, bmc_ref,
              dtb_ref, dtbt_ref, alog_ref, alogt_ref, dskip_ref, gssm_ref, wsc_ref, wout_ref,
              h_ref, ssm_ref, mbuf_ref, sbuf_ref,
              st_ref, cbuf_ref, scbuf_ref):
    tq = MIX_TILE
    c = pl.program_id(1)

    @pl.when(c == 0)
    def _():
        st_ref[...] = jnp.zeros_like(st_ref)
        cbuf_ref[0:SUBLANES, :] = jnp.zeros((SUBLANES, SSM_CONV_DIM), F32)
        scbuf_ref[0:SUBLANES, :] = jnp.zeros((SUBLANES, D_MODEL), F32)

    x = x_ref[...]
    xn = _rms(x, gmix_ref[...]).astype(BF16)

    u = _dot(xn, wa_ref[:, SSM_INNER:])
    cbuf_ref[SUBLANES:SUBLANES + tq, :] = u
    wm = wmc_ref[...]
    conv = u * wm[SSM_CONV - 1:SSM_CONV, :] + bmc_ref[...]
    for k in range(SSM_CONV - 1):
        off = SUBLANES - (SSM_CONV - 1) + k
        conv = conv + cbuf_ref[off:off + tq, :] * wm[k:k + 1, :]
    tail = cbuf_ref[tq + SUBLANES - (SSM_CONV - 1):tq + SUBLANES, :]
    mbuf_ref[...] = tail
    cbuf_ref[SUBLANES - (SSM_CONV - 1):SUBLANES, :] = tail
    xbc = _silu(conv)
    xs = xbc[:, :SSM_INNER]
    bm = xbc[:, SSM_INNER:SSM_INNER + SSM_GROUPS * SSM_STATE]
    cm = xbc[:, SSM_INNER + SSM_GROUPS * SSM_STATE:]

    dt = _softplus(_dot(xn, wdtc_ref[...]) + dtb_ref[...])
    dtt = _softplus(_dot_nt(wdtr_ref[...], xn) + dtbt_ref[...])
    a_row = -jnp.exp(alog_ref[...])
    a_col = -jnp.exp(alogt_ref[...])
    row_i = lax.broadcasted_iota(I32, (tq, tq), 0)
    col_i = lax.broadcasted_iota(I32, (tq, tq), 1)
    causal = row_i >= col_i
    a_cum = _cumsum(dt * a_row, 0)
    a_cumt = _cumsum(dtt * a_col, 1)
    a_last = a_cum[tq - 1:tq, :]

    xdt = xs * _expand_heads(dt)
    in_decay = _expand_heads(jnp.exp(a_cum))
    to_end = _expand_heads(jnp.exp(a_last - a_cum))
    chunk_decay = _expand_heads(jnp.exp(a_last))
    xdt_b = xdt.astype(BF16)
    xend_b = (xdt * to_end).astype(BF16)
    lane = lax.broadcasted_iota(I32, (tq, LANES), 1)

    def proj_b(k):
        return _dot(xn, wb_ref[:, k * D_MODEL:(k + 1) * D_MODEL])

    pb = []
    y_groups = []
    for g in range(SSM_GROUPS):
        pb.append(proj_b(g))
        if g == 0:
            z = _dot(xn, wa_ref[:, :SSM_INNER])
        if g == 2:
            g_b = proj_b(SSM_GROUPS)
        cg = cm[:, g * SSM_STATE:(g + 1) * SSM_STATE].astype(BF16)
        bg_f = bm[:, g * SSM_STATE:(g + 1) * SSM_STATE]
        bg = bg_f.astype(BF16)
        scores = _dot_nt(cg, bg)
        gs = slice(g * GROUP_WIDTH, (g + 1) * GROUP_WIDTH)
        st_g = st_ref[:, gs]
        y_off = _dot(cg, st_g.astype(BF16)) * in_decay[:, gs]
        pair_out = []
        for pr in range(HEADS_PER_GROUP // 2):
            h0 = g * HEADS_PER_GROUP + 2 * pr
            xp = xdt_b[:, h0 * SSM_HEAD_DIM:(h0 + 2) * SSM_HEAD_DIM]
            ys = []
            for h in (h0, h0 + 1):
                seg = a_cum[:, h:h + 1] - a_cumt[h:h + 1, :]
                decay = jnp.where(causal, jnp.exp(jnp.minimum(seg, 0.0)), 0.0)
                ys.append(_dot((scores * decay).astype(BF16), xp))
            pair_out.append(jnp.where(lane < SSM_HEAD_DIM, ys[0], ys[1]))
        y_groups.append(jnp.concatenate(pair_out, axis=1) + y_off)
        st_ref[:, gs] = st_g * chunk_decay[:, gs] + _dot(bg_f.T.astype(BF16), xend_b[:, gs])
    y = jnp.concatenate(y_groups, axis=1) + dskip_ref[...] * xs
    y_a = _group_rmsnorm(y * _silu(z), gssm_ref[...])

    sc_b, sc_c, sc_v, g_a = pb
    cv = sc_c * sc_v
    scbuf_ref[SUBLANES:SUBLANES + tq, :] = cv
    ws = wsc_ref[...]
    uc = cv * ws[SC_CONV - 1:SC_CONV, :]
    for k in range(SC_CONV - 1):
        off = SUBLANES - (SC_CONV - 1) + k
        uc = uc + scbuf_ref[off:off + tq, :] * ws[k:k + 1, :]
    stail = scbuf_ref[tq + SUBLANES - (SC_CONV - 1):tq + SUBLANES, :]
    sbuf_ref[...] = stail
    scbuf_ref[SUBLANES - (SC_CONV - 1):SUBLANES, :] = stail
    merged = _sigmoid(g_a) * y_a + _sigmoid(g_b) * (sc_b * uc)
    h_ref[...] = x + _dot(merged.astype(BF16), wout_ref[...])

    @pl.when(c == pl.num_programs(1) - 1)
    def _():
        ssm_ref[...] = st_ref[...].T


def _prompt_mixer(x2d, nb, w):
    t = x2d.shape[0]
    assert (t // nb) % MIX_TILE == 0, "prompt length must be a multiple of MIX_TILE"
    nc = t // nb // MIX_TILE
    tok = pl.BlockSpec((MIX_TILE, D_MODEL), lambda b, c: (b * nc + c, 0))
    return pl.pallas_call(
        _mix_body,
        grid=(nb, nc),
        in_specs=[tok, _const_spec((1, D_MODEL)),
                  _const_spec((D_MODEL, SSM_INNER + SSM_CONV_DIM)),
                  _const_spec((D_MODEL, SSM_HEADS)), _const_spec((SSM_HEADS, D_MODEL)),
                  _const_spec((D_MODEL, 5 * D_MODEL)),
                  _const_spec((SSM_CONV, SSM_CONV_DIM)), _const_spec((1, SSM_CONV_DIM)),
                  _const_spec((1, SSM_HEADS)), _const_spec((SSM_HEADS, 1)),
                  _const_spec((1, SSM_HEADS)), _const_spec((SSM_HEADS, 1)),
                  _const_spec((1, SSM_INNER)), _const_spec((1, SSM_INNER)),
                  _const_spec((SC_CONV, D_MODEL)), _const_spec((D_MODEL, D_MODEL))],
        out_specs=[tok,
                   pl.BlockSpec((None, SSM_INNER, SSM_STATE), lambda b, c: (b, 0, 0)),
                   pl.BlockSpec((None, SSM_CONV - 1, SSM_CONV_DIM), lambda b, c: (b, 0, 0)),
                   pl.BlockSpec((None, SC_CONV - 1, D_MODEL), lambda b, c: (b, 0, 0))],
        out_shape=[jax.ShapeDtypeStruct((t, D_MODEL), F32),
                   jax.ShapeDtypeStruct((nb, SSM_INNER, SSM_STATE), F32),
                   jax.ShapeDtypeStruct((nb, SSM_CONV - 1, SSM_CONV_DIM), F32),
                   jax.ShapeDtypeStruct((nb, SC_CONV - 1, D_MODEL), F32)],
        scratch_shapes=[pltpu.VMEM((SSM_STATE, SSM_INNER), F32),
                        pltpu.VMEM((MIX_TILE + SUBLANES, SSM_CONV_DIM), F32),
                        pltpu.VMEM((MIX_TILE + SUBLANES, D_MODEL), F32)],
        compiler_params=_cparams("arbitrary", "arbitrary"),
        name="prompt_mixer",
    )(x2d, w["norm_mix"], w["w_a"], w["w_dt"], w["w_dt_t"], w["w_b"], w["w_mconv"], w["b_mconv"],
      w["dt_bias"], w["dt_bias_t"], w["a_log"], w["a_log_t"], w["d_skip"], w["norm_ssm"],
      w["w_sconv"], w["w_out"])


def _router_tail(h2, gmoe_ref, wr_ref, br_ref, h2_ref, hn_ref, lg_ref):
    h2_ref[...] = h2
    hn = _rms(h2, gmoe_ref[...])
    hn_ref[...] = hn
    lg_ref[...] = _dot_nt(wr_ref[...], hn.astype(BF16)) + br_ref[...]


def _attn_body(h_ref, gx_ref, wq_ref, k_ref, v_ref, wo_ref, gmoe_ref, wr_ref, br_ref,
               h2_ref, hn_ref, lg_ref):
    h = h_ref[...]
    hn = _rms(h, gx_ref[...]).astype(BF16)
    q = _dot(hn, wq_ref[...]).astype(BF16)
    outs = []
    for hd in range(XA_HEADS):
        sl = slice(hd * XA_HEAD_DIM, (hd + 1) * XA_HEAD_DIM)
        s = _dot_nt(q[:, sl], k_ref[:, sl]) * (XA_HEAD_DIM ** -0.5)
        e = jnp.exp(s - jnp.max(s, axis=-1, keepdims=True))
        p = e / jnp.sum(e, axis=-1, keepdims=True)
        outs.append(_dot(p.astype(BF16), v_ref[:, sl]))
    o = jnp.concatenate(outs, axis=1).astype(BF16)
    h2 = h + _dot(o, wo_ref[...])
    _router_tail(h2, gmoe_ref, wr_ref, br_ref, h2_ref, hn_ref, lg_ref)


def _prompt_attn(h1, kb, vb, nb, w):
    t = h1.shape[0]
    assert (t // nb) % ATTN_TILE == 0, "prompt length must be a multiple of ATTN_TILE"
    nc = t // nb // ATTN_TILE
    tok = pl.BlockSpec((ATTN_TILE, D_MODEL), lambda b, c: (b * nc + c, 0))
    kv = pl.BlockSpec((N_MEM, D_MODEL), lambda b, c: (b, 0))
    return pl.pallas_call(
        _attn_body,
        grid=(nb, nc),
        in_specs=[tok, _const_spec((1, D_MODEL)), _const_spec((D_MODEL, D_MODEL)), kv, kv,
                  _const_spec((D_MODEL, D_MODEL)), _const_spec((1, D_MODEL)),
                  _const_spec((N_EXPERTS, D_MODEL)), _const_spec((N_EXPERTS, 1))],
        out_specs=[tok, tok, pl.BlockSpec((N_EXPERTS, ATTN_TILE), lambda b, c: (0, b * nc + c))],
        out_shape=[jax.ShapeDtypeStruct((t, D_MODEL), F32),
                   jax.ShapeDtypeStruct((t, D_MODEL), F32),
                   jax.ShapeDtypeStruct((N_EXPERTS, t), F32)],
        compiler_params=_cparams("arbitrary", "arbitrary"),
        name="prompt_attn",
    )(h1, w["norm_xattn"], w["w_xq"], kb, vb, w["w_xo"], w["norm_moe"], w["w_router"], w["b_router"])


def _sproj_body(x_ref, gmix_ref, wa_ref, wdtc_ref, wb_ref, wmc_ref, bmc_ref, dtb_ref, alog_ref,
                wsc_ref, mst_ref, sst_ref,
                z_ref, xs_ref, dtx_ref, dec_ref, bm_ref, cm_ref, yb_ref, sga_ref, mnew_ref, snew_ref):
    x = x_ref[...]
    xn = _rms(x, gmix_ref[...]).astype(BF16)
    pa = _dot(xn, wa_ref[...])
    z_ref[...] = pa[:, :SSM_INNER]
    u = pa[:, SSM_INNER:]
    wm = wmc_ref[...]
    conv = u * wm[SSM_CONV - 1:SSM_CONV, :] + bmc_ref[...]
    for k in range(SSM_CONV - 1):
        conv = conv + mst_ref[k] * wm[k:k + 1, :]
    for k in range(SSM_CONV - 2):
        mnew_ref[k] = mst_ref[k + 1]
    mnew_ref[SSM_CONV - 2] = u
    xbc = _silu(conv)
    xs = xbc[:, :SSM_INNER]
    xs_ref[...] = xs
    bm_ref[...] = xbc[:, SSM_INNER:SSM_INNER + SSM_GROUPS * SSM_STATE]
    cm_ref[...] = xbc[:, SSM_INNER + SSM_GROUPS * SSM_STATE:]
    dt = _softplus(_dot(xn, wdtc_ref[...]) + dtb_ref[...])
    dec_ref[...] = jnp.exp(dt * (-jnp.exp(alog_ref[...])))
    dtx_ref[...] = xs * _expand_heads(dt)
    pb = _dot(xn, wb_ref[...])
    cv = pb[:, D_MODEL:2 * D_MODEL] * pb[:, 2 * D_MODEL:3 * D_MODEL]
    ws = wsc_ref[...]
    uc = cv * ws[SC_CONV - 1:SC_CONV, :]
    for k in range(SC_CONV - 1):
        uc = uc + sst_ref[k] * ws[k:k + 1, :]
    for k in range(SC_CONV - 2):
        snew_ref[k] = sst_ref[k + 1]
    snew_ref[SC_CONV - 2] = cv
    yb_ref[...] = _sigmoid(pb[:, 4 * D_MODEL:5 * D_MODEL]) * (pb[:, 0:D_MODEL] * uc)
    sga_ref[...] = _sigmoid(pb[:, 3 * D_MODEL:4 * D_MODEL])


def _sample_proj(x, mstate_t, sstate_t, w):
    nb = x.shape[0]
    f = lambda *s: jax.ShapeDtypeStruct(s, F32)
    return pl.pallas_call(
        _sproj_body,
        out_shape=[f(nb, SSM_INNER), f(nb, SSM_INNER), f(nb, SSM_INNER), f(nb, SSM_HEADS),
                   f(nb, SSM_GROUPS * SSM_STATE), f(nb, SSM_GROUPS * SSM_STATE),
                   f(nb, D_MODEL), f(nb, D_MODEL),
                   f(SSM_CONV - 1, nb, SSM_CONV_DIM), f(SC_CONV - 1, nb, D_MODEL)],
        compiler_params=pltpu.CompilerParams(vmem_limit_bytes=VMEM_LIMIT),
        name="sample_proj",
    )(x, w["norm_mix"], w["w_a"], w["w_dt"], w["w_b"], w["w_mconv"], w["b_mconv"], w["dt_bias"],
      w["a_log"], w["w_sconv"], mstate_t, sstate_t)


def _sstate_body(dec_ref, s_ref, dtx_ref, bm_ref, cm_ref, snew_ref, y_ref):
    i = pl.program_id(0)
    rows_per_blk = LANES
    for j in range(STATE_BB):
        b = i * STATE_BB + j
        dtx_row = dtx_ref[j:j + 1, :]
        y_parts = []
        for g in range(SSM_GROUPS):
            b_row = bm_ref[j:j + 1, g * SSM_STATE:(g + 1) * SSM_STATE]
            c_row = cm_ref[j:j + 1, g * SSM_STATE:(g + 1) * SSM_STATE].astype(BF16)
            new_blocks = []
            for q in range(GROUP_WIDTH // rows_per_blk):
                r0 = g * GROUP_WIDTH + q * rows_per_blk
                dcol = jnp.broadcast_to(dtx_row[:, r0:r0 + rows_per_blk], (rows_per_blk, LANES)).T
                sub = []
                for hh in range(rows_per_blk // SSM_HEAD_DIM):
                    h = r0 // SSM_HEAD_DIM + hh
                    lo = hh * SSM_HEAD_DIM
                    s_old = s_ref[j, r0 + lo:r0 + lo + SSM_HEAD_DIM, :]
                    sub.append(s_old * dec_ref[b, h] + dcol[lo:lo + SSM_HEAD_DIM, :] * b_row)
                blk = jnp.concatenate(sub, axis=0)
                snew_ref[j, r0:r0 + rows_per_blk, :] = blk
                new_blocks.append(blk.astype(BF16))
            s_g = jnp.concatenate(new_blocks, axis=0)
            y_parts.append(_dot_nt(c_row, s_g))
        y_ref[j:j + 1, :] = jnp.concatenate(y_parts, axis=1)


def _sample_state(dec, state, dtx, bm, cm):
    nb = state.shape[0]
    row = lambda wdt: pl.BlockSpec((STATE_BB, wdt), lambda i, dec: (i, 0))
    st = pl.BlockSpec((STATE_BB, SSM_INNER, SSM_STATE), lambda i, dec: (i, 0, 0))
    return pl.pallas_call(
        _sstate_body,
        grid_spec=pltpu.PrefetchScalarGridSpec(
            num_scalar_prefetch=1, grid=(nb // STATE_BB,),
            in_specs=[st, row(SSM_INNER), row(SSM_GROUPS * SSM_STATE), row(SSM_GROUPS * SSM_STATE)],
            out_specs=[st, row(SSM_INNER)]),
        out_shape=[jax.ShapeDtypeStruct(state.shape, F32), jax.ShapeDtypeStruct((nb, SSM_INNER), F32)],
        compiler_params=_cparams("arbitrary"),
        name="sample_state",
    )(dec, state, dtx, bm, cm)


def _sfin1_body(x_ref, y_ref, xs_ref, z_ref, yb_ref, sga_ref, dskip_ref, gssm_ref, wout_ref,
                gx_ref, wq_ref, h_ref, q_ref):
    y = y_ref[...] + dskip_ref[...] * xs_ref[...]
    y_a = _group_rmsnorm(y * _silu(z_ref[...]), gssm_ref[...])
    merged = sga_ref[...] * y_a + yb_ref[...]
    h = x_ref[...] + _dot(merged.astype(BF16), wout_ref[...])
    h_ref[...] = h
    q_ref[...] = _dot(_rms(h, gx_ref[...]).astype(BF16), wq_ref[...])


def _sample_fin1(x, y, xs, z, yb, sga, w):
    nb = x.shape[0]
    return pl.pallas_call(
        _sfin1_body,
        out_shape=[jax.ShapeDtypeStruct((nb, D_MODEL), F32)] * 2,
        compiler_params=pltpu.CompilerParams(vmem_limit_bytes=VMEM_LIMIT),
        name="sample_fin1",
    )(x, y, xs, z, yb, sga, w["d_skip"], w["norm_ssm"], w["w_out"], w["norm_xattn"], w["w_xq"])


def _sattn_body(q_ref, k_ref, v_ref, o_ref):
    for j in range(ATTN_BB):
        q_row = q_ref[j]
        q4 = jnp.concatenate([q_row[:, h * XA_HEAD_DIM:(h + 1) * XA_HEAD_DIM]
                              for h in range(XA_HEADS)], axis=0)
        s = jnp.sum(k_ref[j] * q4[None], axis=-1, keepdims=True) * (XA_HEAD_DIM ** -0.5)
        e = jnp.exp(s - jnp.max(s, axis=0, keepdims=True))
        p = e / jnp.sum(e, axis=0, keepdims=True)
        o4 = jnp.sum(p * v_ref[j], axis=0)
        o_ref[j] = jnp.concatenate([o4[h:h + 1, :] for h in range(XA_HEADS)], axis=1)


def _sample_attn(q3, k3, v3):
    nb = q3.shape[0]
    qs = pl.BlockSpec((ATTN_BB, 1, D_MODEL), lambda i: (i, 0, 0))
    kv = pl.BlockSpec((ATTN_BB, N_MEM, XA_HEADS, XA_HEAD_DIM), lambda i: (i, 0, 0, 0))
    return pl.pallas_call(
        _sattn_body,
        grid=(nb // ATTN_BB,),
        in_specs=[qs, kv, kv],
        out_specs=qs,
        out_shape=jax.ShapeDtypeStruct((nb, 1, D_MODEL), F32),
        compiler_params=_cparams("arbitrary"),
        name="sample_attn",
    )(q3, k3, v3)


def _sfin2_body(h_ref, o_ref, wo_ref, gmoe_ref, wr_ref, br_ref, h2_ref, hn_ref, lg_ref):
    h2 = h_ref[...] + _dot(o_ref[...].astype(BF16), wo_ref[...])
    _router_tail(h2, gmoe_ref, wr_ref, br_ref, h2_ref, hn_ref, lg_ref)


def _sample_fin2(h1, o, w):
    nb = h1.shape[0]
    return pl.pallas_call(
        _sfin2_body,
        out_shape=[jax.ShapeDtypeStruct((nb, D_MODEL), F32)] * 2
        + [jax.ShapeDtypeStruct((N_EXPERTS, nb), F32)],
        compiler_params=pltpu.CompilerParams(vmem_limit_bytes=VMEM_LIMIT),
        name="sample_fin2",
    )(h1, o, w["w_xo"], w["norm_moe"], w["w_router"], w["b_router"])


def _pad_cols(x, cols):
    return jnp.concatenate([x, jnp.zeros((x.shape[0], cols - x.shape[1]), x.dtype)], axis=1)


def _route_body(lgp_ref, lgs_ref, g_ref, loc_ref, cnt_ref, off_ref):
    tt = lgp_ref.shape[1]
    is_sample = pl.program_id(0) == pl.num_programs(0) - 1
    col = lax.broadcasted_iota(I32, (1, tt), 1)
    valid = jnp.logical_or(jnp.logical_not(is_sample), col < lgs_ref.shape[1])
    work = jnp.where(is_sample, _pad_cols(lgs_ref[...], tt), lgp_ref[...])
    sub = lax.broadcasted_iota(I32, (N_EXPERTS, tt), 0).astype(F32)
    vals, hots = [], []
    for _ in range(TOP_K):
        m = jnp.max(work, axis=0, keepdims=True)
        idx = jnp.min(jnp.where(work == m, sub, float(N_EXPERTS)), axis=0, keepdims=True)
        hot = (sub == idx) & valid
        vals.append(m)
        hots.append(hot)
        work = jnp.where(hot, -jnp.inf, work)
    exps = [jnp.exp(v - vals[0]) for v in vals]
    tot = exps[0]
    for e in exps[1:]:
        tot = tot + e
    assigned = hots[0]
    for hot in hots[1:]:
        assigned = assigned | hot
    a = assigned.astype(BF16)
    r_i = lax.broadcasted_iota(I32, (tt, tt), 0)
    c_i = lax.broadcasted_iota(I32, (tt, tt), 1)
    rank = _dot(a, (r_i < c_i).astype(BF16))
    cnt = jnp.sum(a.astype(F32), axis=1, keepdims=True)
    cnt = jnp.floor((cnt + (SUBLANES - 1)) * (1.0 / SUBLANES)) * SUBLANES
    e_r = lax.broadcasted_iota(I32, (N_EXPERTS, N_EXPERTS), 0)
    e_c = lax.broadcasted_iota(I32, (N_EXPERTS, N_EXPERTS), 1)
    cnt_cols = jnp.broadcast_to(cnt, (N_EXPERTS, LANES)).astype(BF16)
    off = _dot((e_r > e_c).astype(BF16), cnt_cols)[:, 0:1]
    slot = rank + off
    k_sub = lax.broadcasted_iota(I32, (SUBLANES, tt), 0)
    g_out = jnp.zeros((SUBLANES, tt), F32)
    l_out = jnp.full((SUBLANES, tt), -1.0, F32)
    for k in range(TOP_K):
        lk = jnp.sum(jnp.where(hots[k], slot, 0.0), axis=0, keepdims=True)
        g_out = jnp.where(k_sub == k, jnp.where(valid, exps[k] / tot, 0.0), g_out)
        l_out = jnp.where(k_sub == k, jnp.where(valid, lk, -1.0), l_out)
    g_ref[...] = g_out
    loc_ref[...] = l_out.astype(I32)
    cnt_ref[...] = cnt.astype(I32)
    off_ref[...] = off.astype(I32)


def _route(logits_p, logits_s, tt):
    ntp = logits_p.shape[1] // tt
    nt = ntp + 1
    t = nt * tt
    tk = pl.BlockSpec((SUBLANES, tt), lambda i: (0, i))
    per_tile = pl.BlockSpec((None, N_EXPERTS, 1), lambda i: (i, 0, 0))
    return pl.pallas_call(
        _route_body,
        grid=(nt,),
        in_specs=[pl.BlockSpec((N_EXPERTS, tt), lambda i: (0, jnp.minimum(i, ntp - 1))),
                  pl.BlockSpec(logits_s.shape, lambda i: (0, 0))],
        out_specs=[tk, tk, per_tile, per_tile],
        out_shape=[jax.ShapeDtypeStruct((SUBLANES, t), F32), jax.ShapeDtypeStruct((SUBLANES, t), I32),
                   jax.ShapeDtypeStruct((nt, N_EXPERTS, 1), I32), jax.ShapeDtypeStruct((nt, N_EXPERTS, 1), I32)],
        compiler_params=_cparams("arbitrary"),
        name="moe_route",
    )(logits_p, logits_s)


def _sorted_rows(tt):
    return tt * TOP_K + N_EXPERTS * SUBLANES


def _run_copies(tt, tile, cnt_ref, off_ref, base_ref, make_copy, wait):
    if wait:
        total = off_ref[tile, N_EXPERTS - 1] + cnt_ref[tile, N_EXPERTS - 1]

        @pl.when(total > 0)
        def _():
            make_copy(0, 0, pl.multiple_of(total, SUBLANES)).wait()
        return

    def per_expert(e):
        n = cnt_ref[tile, e]

        @pl.when(n > 0)
        def _():
            make_copy(pl.multiple_of(off_ref[tile, e], SUBLANES),
                      pl.multiple_of(base_ref[tile, e], SUBLANES), pl.multiple_of(n, SUBLANES)).start()

    def four_experts(j, carry):
        for u in range(RUN_COPY_UNROLL):
            per_expert(j * RUN_COPY_UNROLL + u)
        return carry

    lax.fori_loop(0, N_EXPERTS // RUN_COPY_UNROLL, four_experts, 0)


def _dispatch_body(tm, tt, cnt_ref, off_ref, base_ref, zstart_ref, zsize_ref, loc_ref, x_ref, xs_ref, o_hbm,
                   zero_ref, srt_ref, zsem, sems):
    i = pl.program_id(0)
    last = pl.num_programs(0) - 1
    r = _sorted_rows(tt)

    def zero_copy(j):
        n = pl.multiple_of(zsize_ref[j], SUBLANES)
        dst = o_hbm.at[pl.ds(pl.multiple_of(zstart_ref[j], SUBLANES), n), :]
        return pltpu.make_async_copy(zero_ref.at[pl.ds(0, n), :], dst, zsem)

    @pl.when(i == 0)
    def _():
        zero_ref[...] = jnp.zeros_like(zero_ref)

        def start(j, carry):
            @pl.when(zsize_ref[j] > 0)
            def _():
                zero_copy(j).start()
            return carry

        def wait(j, carry):
            @pl.when(zsize_ref[j] > 0)
            def _():
                zero_copy(j).wait()
            return carry

        lax.fori_loop(0, zstart_ref.shape[0], start, 0)
        lax.fori_loop(0, zstart_ref.shape[0], wait, 0)

    loc = loc_ref[...]
    slot_i = lax.broadcasted_iota(I32, (r, tt), 0)
    hit = slot_i == loc[0:1, :]
    for k in range(1, TOP_K):
        hit = hit | (slot_i == loc[k:k + 1, :])
    buf = i % DISPATCH_BUFFERS
    x = jnp.where(i == last, _pad_rows(xs_ref[...], tt), x_ref[...])
    srt_ref[buf] = _pack_bf16_pairs(_dot(hit.astype(BF16), x.astype(BF16)), is_bf16_valued=True)

    def copies(tile, wait):
        b = tile % DISPATCH_BUFFERS

        def make_copy(lo, go, size):
            return pltpu.make_async_copy(srt_ref.at[b, pl.ds(lo, size), :], o_hbm.at[pl.ds(go, size), :],
                                         sems.at[b])

        _run_copies(tt, tile, cnt_ref, off_ref, base_ref, make_copy, wait)

    copies(i, False)
    oldest = i - (DISPATCH_BUFFERS - 1)

    @pl.when(oldest >= 0)
    def _():
        copies(oldest, True)

    @pl.when(i == last)
    def _():
        for back in range(DISPATCH_BUFFERS - 2, -1, -1):
            @pl.when(i - back >= 0)
            def _(back=back):
                copies(i - back, True)


def _dispatch(cnt, off, base, zero_starts, zero_sizes, loc, hn_p, hn_s, n_rows, tm, tt):
    ntp = hn_p.shape[0] // tt
    smem = pl.BlockSpec(memory_space=pltpu.SMEM)
    return pl.pallas_call(
        functools.partial(_dispatch_body, tm, tt),
        grid_spec=pltpu.PrefetchScalarGridSpec(
            num_scalar_prefetch=0, grid=(ntp + 1,),
            in_specs=[smem, smem, smem, smem, smem,
                      pl.BlockSpec((SUBLANES, tt), lambda i: (0, i)),
                      pl.BlockSpec((tt, D_MODEL), lambda i: (jnp.minimum(i, ntp - 1), 0)),
                      pl.BlockSpec(hn_s.shape, lambda i: (0, 0))],
            out_specs=pl.BlockSpec(memory_space=pl.ANY),
            scratch_shapes=[pltpu.VMEM((tm, D_MODEL // 2), U32),
                            pltpu.VMEM((DISPATCH_BUFFERS, _sorted_rows(tt), D_MODEL // 2), U32),
                            pltpu.SemaphoreType.DMA, pltpu.SemaphoreType.DMA((DISPATCH_BUFFERS,))]),
        out_shape=jax.ShapeDtypeStruct((n_rows, D_MODEL // 2), U32),
        compiler_params=_cparams("arbitrary"),
        name="moe_dispatch",
    )(cnt, off, base, zero_starts, zero_sizes, loc, hn_p, hn_s)


def _expert_body(be_ref, nu_ref, bv_ref, slot_ref, nxt_ref, x_ref, wgu_hbm, bgu_ref, wdn_hbm, bdn_ref,
                 y_ref, wgu_f, wdn_f, wgu_b, wdn_b, sems):
    i = pl.program_id(0)
    tm = x_ref.shape[0]
    valid = bv_ref[i]
    expert = be_ref[i]
    slot = slot_ref[i]

    def weight_copies(e, s):
        return (pltpu.make_async_copy(wgu_hbm.at[e], wgu_f.at[s], sems.at[0, s]),
                pltpu.make_async_copy(wdn_hbm.at[e], wdn_f.at[s], sems.at[1, s]))

    @pl.when(i == 0)
    def _():
        for cp in weight_copies(expert, slot):
            cp.start()

    first = jnp.logical_and(i < nu_ref[0], jnp.logical_or(i == 0, expert != be_ref[jnp.maximum(i - 1, 0)]))

    @pl.when(first)
    def _():
        for cp in weight_copies(expert, slot):
            cp.wait()

        @pl.when(nxt_ref[i] >= 0)
        def _():
            for cp in weight_copies(nxt_ref[i], 1 - slot):
                cp.start()

    def ffn(rows, cast):
        xb = _unpack_bf16_pairs(x_ref[0:rows, :])
        gu_parts = []
        for j in range(2 * D_FF // EXPERT_COL_CHUNK):
            cs = slice(j * EXPERT_COL_CHUNK, (j + 1) * EXPERT_COL_CHUNK)
            if cast:
                wgu_b[:, cs] = wgu_f[slot, :, cs].astype(BF16)
            gu_parts.append(_dot(xb, wgu_b[:, cs]) + bgu_ref[:, cs])
        gate = jnp.minimum(jnp.concatenate(gu_parts[:len(gu_parts) // 2], axis=1), SWIGLU_LIMIT)
        up = jnp.clip(jnp.concatenate(gu_parts[len(gu_parts) // 2:], axis=1), -SWIGLU_LIMIT, SWIGLU_LIMIT)
        act = ((up + 1.0) * _x_sigmoid(gate, SWIGLU_ALPHA)).astype(BF16)
        y_parts = []
        for j in range(D_MODEL // EXPERT_COL_CHUNK):
            cs = slice(j * EXPERT_COL_CHUNK, (j + 1) * EXPERT_COL_CHUNK)
            if cast:
                wdn_b[:, cs] = wdn_f[slot, :, cs].astype(BF16)
            y_parts.append(_dot(act, wdn_b[:, cs]) + bdn_ref[:, cs])
        y_ref[0:rows, :] = _pack_bf16_pairs(jnp.concatenate(y_parts, axis=1))
        if rows < tm:
            y_ref[rows:tm, :] = jnp.zeros((tm - rows, D_MODEL // 2), U32)

    quarter = tm // EXPERT_ROW_SPLITS
    for q in range(1, EXPERT_ROW_SPLITS + 1):
        in_q = jnp.logical_and(valid > (q - 1) * quarter, valid <= q * quarter)
        for cast in (True, False):
            @pl.when(jnp.logical_and(in_q, first == cast))
            def _(q=q, cast=cast):
                ffn(q * quarter, cast)

    @pl.when(valid == 0)
    def _():
        y_ref[...] = jnp.zeros_like(y_ref)


def _experts(block_e, n_used, block_valid, block_slot, block_next, xs, wgu, bgu, wdn, bdn, tm):
    n_rows = xs.shape[0]
    return pl.pallas_call(
        _expert_body,
        grid_spec=pltpu.PrefetchScalarGridSpec(
            num_scalar_prefetch=5, grid=(n_rows // tm,),
            in_specs=[pl.BlockSpec((tm, D_MODEL // 2), lambda i, be, nu, *_: (jnp.minimum(i, nu[0] - 1), 0)),
                      pl.BlockSpec(memory_space=pl.ANY),
                      pl.BlockSpec((None, 1, 2 * D_FF), lambda i, be, *_: (be[i], 0, 0)),
                      pl.BlockSpec(memory_space=pl.ANY),
                      pl.BlockSpec((None, 1, D_MODEL), lambda i, be, *_: (be[i], 0, 0))],
            out_specs=pl.BlockSpec((tm, D_MODEL // 2), lambda i, *_: (i, 0)),
            scratch_shapes=[pltpu.VMEM((2, D_MODEL, 2 * D_FF), F32), pltpu.VMEM((2, D_FF, D_MODEL), F32),
                            pltpu.VMEM((D_MODEL, 2 * D_FF), BF16), pltpu.VMEM((D_FF, D_MODEL), BF16),
                            pltpu.SemaphoreType.DMA((2, 2))]),
        out_shape=jax.ShapeDtypeStruct((n_rows, D_MODEL // 2), U32),
        compiler_params=_cparams("arbitrary"),
        name="moe_experts",
    )(block_e, n_used, block_valid, block_slot, block_next, xs, wgu, bgu, wdn, bdn)


def _combine_body(tt, cnt_ref, off_ref, base_ref, loc_ref, g_ref, h_ref, hs_ref, gfin_ref, ys_hbm,
                  y_ref, ysmp_ref, buf_ref, sems):
    i = pl.program_id(0)
    last = pl.num_programs(0) - 1
    r = _sorted_rows(tt)

    def copies(tile, wait):
        b = tile % 2

        def make_copy(lo, go, size):
            return pltpu.make_async_copy(ys_hbm.at[pl.ds(go, size), :], buf_ref.at[b, pl.ds(lo, size), :],
                                         sems.at[b])

        _run_copies(tt, tile, cnt_ref, off_ref, base_ref, make_copy, wait)

    @pl.when(i == 0)
    def _():
        buf_ref[...] = jnp.zeros_like(buf_ref)
        copies(0, False)

    @pl.when(i < last)
    def _():
        copies(i + 1, False)

    copies(i, True)
    loc = loc_ref[...]
    gates = g_ref[...]
    slot_i = lax.broadcasted_iota(I32, (r, tt), 0)
    gmat = jnp.where(slot_i == loc[0:1, :], gates[0:1, :], 0.0)
    for k in range(1, TOP_K):
        gmat = gmat + jnp.where(slot_i == loc[k:k + 1, :], gates[k:k + 1, :], 0.0)
    h = jnp.where(i == last, _pad_rows(hs_ref[...], tt), h_ref[...])
    moe = lax.dot_general(gmat.astype(BF16), _unpack_bf16_pairs(buf_ref[i % 2]), TN_DIMS,
                          preferred_element_type=F32)
    y = _rms(h + moe, gfin_ref[...])

    @pl.when(i < last)
    def _():
        y_ref[...] = y

    @pl.when(i == last)
    def _():
        ysmp_ref[...] = y[0:ysmp_ref.shape[0], :]


def _combine(cnt, off, base, loc, gates, h2_p, h2_s, norm_final, ys, tt):
    ntp = h2_p.shape[0] // tt
    tok = pl.BlockSpec((tt, D_MODEL), lambda i: (jnp.minimum(i, ntp - 1), 0))
    smp = pl.BlockSpec(h2_s.shape, lambda i: (0, 0))
    tk = pl.BlockSpec((SUBLANES, tt), lambda i: (0, i))
    smem = pl.BlockSpec(memory_space=pltpu.SMEM)
    return pl.pallas_call(
        functools.partial(_combine_body, tt),
        grid_spec=pltpu.PrefetchScalarGridSpec(
            num_scalar_prefetch=0, grid=(ntp + 1,),
            in_specs=[smem, smem, smem, tk, tk, tok, smp,
                      pl.BlockSpec((1, D_MODEL), lambda i: (0, 0)),
                      pl.BlockSpec(memory_space=pl.ANY)],
            out_specs=[tok, smp],
            scratch_shapes=[pltpu.VMEM((2, _sorted_rows(tt), D_MODEL // 2), U32), pltpu.SemaphoreType.DMA((2,))]),
        out_shape=[jax.ShapeDtypeStruct(h2_p.shape, F32), jax.ShapeDtypeStruct(h2_s.shape, F32)],
        compiler_params=_cparams("arbitrary"),
        name="moe_combine",
    )(cnt, off, base, loc, gates, h2_p, h2_s, norm_final, ys)


def _moe_and_final_norm(hn_p, logits_p, h2_p, hn_s, logits_s, h2_s, w, tt, tm):
    nt = hn_p.shape[0] // tt + 1
    t = hn_p.shape[0] + hn_s.shape[0]
    gates, loc, cnt3, off3 = _route(logits_p, logits_s, tt)
    cnt = cnt3[:, :, 0]
    counts = jnp.sum(cnt, axis=0)
    padded = (counts + tm - 1) // tm * tm
    pad_end = jnp.cumsum(padded)
    start = pad_end - padded
    off = off3[:, :, 0]
    base = (start[None, :] + jnp.cumsum(cnt, axis=0) - cnt).astype(I32)
    n_blocks = (t * TOP_K + nt * N_EXPERTS * (SUBLANES - 1) + N_EXPERTS * (tm - 1)) // tm
    n_rows = n_blocks * tm
    block_start = jnp.arange(n_blocks, dtype=I32) * tm
    block_e = jnp.minimum(jnp.sum(block_start[:, None] >= pad_end[None, :], axis=-1), N_EXPERTS - 1).astype(I32)
    n_used = (pad_end[-1:] // tm).astype(I32)
    zero_starts = jnp.concatenate([start + counts, block_start]).astype(I32)
    zero_sizes = jnp.concatenate([padded - counts,
                                  jnp.where(block_start >= pad_end[-1], tm, 0)]).astype(I32)
    xs = _dispatch(cnt, off, base, zero_starts, zero_sizes, loc, hn_p, hn_s, n_rows, tm, tt)
    e_ids = jnp.arange(N_EXPERTS, dtype=I32)
    block_hot = block_e[:, None] == e_ids[None, :]

    def per_block(table):
        return jnp.sum(jnp.where(block_hot, table[None, :], 0), axis=1).astype(I32)

    block_valid = jnp.clip(per_block(start + counts) - block_start, 0, tm)
    block_valid = jnp.where(block_start < pad_end[-1], block_valid, 0).astype(I32)
    present = padded > 0
    later = present[None, :] & (e_ids[None, :] > e_ids[:, None])
    next_e = jnp.min(jnp.where(later, e_ids[None, :], N_EXPERTS), axis=1)
    next_e = jnp.where(next_e < N_EXPERTS, next_e, -1).astype(I32)
    before = present[None, :] & (e_ids[None, :] < e_ids[:, None])
    run_slot = (jnp.sum(before.astype(I32), axis=1) % 2).astype(I32)
    ys = _experts(block_e, n_used, block_valid, per_block(run_slot), per_block(next_e), xs, w["w_gate_up"], w["b_gate_up"], w["w_down"], w["b_down"], tm)
    return _combine(cnt, off, base, loc, gates, h2_p, h2_s, w["norm_final"], ys, tt)


def kernel(x_prompt, x_sample, mem_prompt, state_ssm, state_mamba_conv, state_short_conv, cache_mem_k, cache_mem_v, norm_mix, w_in, w_mconv, b_mconv, dt_bias, a_log, d_skip, norm_ssm, w_sconv, w_out, norm_xattn, norm_mem, w_xq, w_xk, w_xv, w_xo, norm_moe, w_router, b_router, w_gate_up, b_gate_up, w_down, b_down, norm_final):
    nbp, seq, _ = x_prompt.shape
    nbs = x_sample.shape[0]
    dt_lo = SSM_INNER + SSM_CONV_DIM
    w_in0 = w_in[0]
    w_dt = w_in0[:, dt_lo:dt_lo + SSM_HEADS]
    w = {
        "norm_mix": norm_mix, "norm_ssm": norm_ssm, "norm_xattn": norm_xattn, "norm_moe": norm_moe,
        "norm_final": norm_final.reshape(1, D_MODEL),
        "w_a": w_in0[:, :dt_lo].astype(BF16),
        "w_dt": w_dt.astype(BF16), "w_dt_t": w_dt.T.astype(BF16),
        "w_b": w_in0[:, dt_lo + SSM_HEADS:].astype(BF16),
        "w_mconv": w_mconv[0], "b_mconv": b_mconv,
        "dt_bias": dt_bias, "dt_bias_t": dt_bias.reshape(SSM_HEADS, 1),
        "a_log": a_log, "a_log_t": a_log.reshape(SSM_HEADS, 1),
        "d_skip": jnp.repeat(d_skip, SSM_HEAD_DIM, axis=1),
        "w_sconv": w_sconv[0], "w_out": w_out[0].astype(BF16),
        "w_xq": w_xq[0].astype(BF16), "w_xo": w_xo[0].astype(BF16),
        "w_router": w_router[0].T.astype(BF16), "b_router": b_router.reshape(N_EXPERTS, 1),
        "w_gate_up": w_gate_up[0], "b_gate_up": b_gate_up[0].reshape(N_EXPERTS, 1, 2 * D_FF),
        "w_down": w_down[0], "b_down": b_down[0].reshape(N_EXPERTS, 1, D_MODEL),
    }

    k_p, v_p, kb, vb = _mem_kv(mem_prompt.reshape(nbp * N_MEM, D_MODEL), norm_mem,
                               w_xk[0].astype(BF16), w_xv[0].astype(BF16))
    h1, ssm_p, mconv_p, sconv_p = _prompt_mixer(x_prompt.reshape(nbp * seq, D_MODEL), nbp, w)
    h2, hn, logits = _prompt_attn(h1, kb, vb, nbp, w)

    xs2 = x_sample.reshape(nbs, D_MODEL)
    mstate_t = jnp.transpose(state_mamba_conv[0], (1, 0, 2))
    sstate_t = jnp.transpose(state_short_conv[0], (1, 0, 2))
    z, xs_, dtx, dec, bm, cm, yb, sga, mnew_t, snew_t = _sample_proj(xs2, mstate_t, sstate_t, w)
    ssm_s, y_s = _sample_state(dec, state_ssm[0].reshape(nbs, SSM_INNER, SSM_STATE), dtx, bm, cm)
    h1s, q_s = _sample_fin1(xs2, y_s, xs_, z, yb, sga, w)
    o_s = _sample_attn(q_s.reshape(nbs, 1, D_MODEL),
                       cache_mem_k[0], cache_mem_v[0])
    h2s, hns, logits_s = _sample_fin2(h1s, o_s.reshape(nbs, D_MODEL), w)
    y_prompt, y_sample = _moe_and_final_norm(hn, logits, h2, hns, logits_s, h2s, w, MIX_TILE, MOE_ROW_TILE)

    return (y_prompt.reshape(nbp, seq, D_MODEL),
            y_sample.reshape(nbs, 1, D_MODEL),
            ssm_p.reshape(1, nbp, SSM_HEADS, SSM_HEAD_DIM, SSM_STATE),
            mconv_p[None], sconv_p[None],
            k_p.reshape(1, nbp, N_MEM, XA_HEADS, XA_HEAD_DIM),
            v_p.reshape(1, nbp, N_MEM, XA_HEADS, XA_HEAD_DIM),
            ssm_s.reshape(1, nbs, SSM_HEADS, SSM_HEAD_DIM, SSM_STATE),
            jnp.transpose(mnew_t, (1, 0, 2))[None],
            jnp.transpose(snew_t, (1, 0, 2))[None])
```

```python
import functools

import jax
import jax.numpy as jnp
from jax import lax
from jax.experimental import pallas as pl
from jax.experimental.pallas import tpu as pltpu

F32 = jnp.float32
BF16 = jnp.bfloat16
I32 = jnp.int32
U32 = jnp.uint32

D_MODEL = 1024
N_MEM = 256
SSM_HEADS = 16
SSM_HEAD_DIM = 64
SSM_INNER = SSM_HEADS * SSM_HEAD_DIM
SSM_STATE = 128
SSM_GROUPS = 4
HEADS_PER_GROUP = SSM_HEADS // SSM_GROUPS
GROUP_WIDTH = SSM_INNER // SSM_GROUPS
SSM_CONV = 4
SSM_CONV_DIM = SSM_INNER + 2 * SSM_GROUPS * SSM_STATE
SC_CONV = 3
XA_HEADS = 4
XA_HEAD_DIM = D_MODEL // XA_HEADS
N_EXPERTS = 32
TOP_K = 4
D_FF = D_MODEL
SWIGLU_LIMIT = 7.0
SWIGLU_ALPHA = 1.702
EPS = 1e-6

LANES = 128
SUBLANES = 8
V7X_VMEM_BYTES = 64 * 1024 * 1024
VMEM_LIMIT = V7X_VMEM_BYTES * 7 // 8
HIGH_HALF = 0xFFFF0000

MIX_TILE = 256
ATTN_TILE = 1024
MOE_ROW_TILE = 512
RUN_COPY_UNROLL = 4
EXPERT_ROW_SPLITS = 4
EXPERT_COL_CHUNK = 512
STATE_BB = 8
ATTN_BB = 4

NT_DIMS = (((1,), (1,)), ((), ()))
TN_DIMS = (((0,), (0,)), ((), ()))


def _cparams(*sem):
    return pltpu.CompilerParams(dimension_semantics=sem, vmem_limit_bytes=VMEM_LIMIT)


def _const_spec(shape):
    nd = len(shape)
    return pl.BlockSpec(shape, lambda *_: (0,) * nd, pipeline_mode=pl.Buffered(1))


def _sigmoid(x):
    return 0.5 * jnp.tanh(0.5 * x) + 0.5


def _silu(x):
    return x * _sigmoid(x)


def _softplus(x):
    return jnp.maximum(x, 0.0) + jnp.log(1.0 + jnp.exp(-jnp.abs(x)))


def _rms(x, g):
    ms = jnp.mean(x * x, axis=-1, keepdims=True)
    return x * lax.rsqrt(ms + EPS) * g


def _dot(a, b):
    return jnp.dot(a, b, preferred_element_type=F32)


def _dot_nt(a, b):
    return lax.dot_general(a, b, NT_DIMS, preferred_element_type=F32)


def _expand_heads(v):
    assert LANES == 2 * SSM_HEAD_DIM
    rows = v.shape[0]
    lane = lax.broadcasted_iota(I32, (rows, LANES), 1)
    pieces = []
    for j in range(SSM_HEADS // 2):
        a = jnp.broadcast_to(v[:, 2 * j:2 * j + 1], (rows, LANES))
        b = jnp.broadcast_to(v[:, 2 * j + 1:2 * j + 2], (rows, LANES))
        pieces.append(jnp.where(lane < SSM_HEAD_DIM, a, b))
    return jnp.concatenate(pieces, axis=1)


def _cumsum(x, axis):
    idx = lax.broadcasted_iota(I32, x.shape, axis)
    shift = 1
    while shift < x.shape[axis]:
        x = x + jnp.where(idx >= shift, pltpu.roll(x, shift, axis), 0.0)
        shift *= 2
    return x


def _pack_bf16_pairs(x, is_bf16_valued=False):
    w = x.shape[1] // 2
    if not is_bf16_valued:
        x = x.astype(BF16).astype(F32)
    bits = lax.bitcast_convert_type(x, U32)
    return (bits[:, w:] & jnp.uint32(HIGH_HALF)) | (bits[:, :w] >> 16)


def _unpack_bf16_pairs(p):
    lo = lax.bitcast_convert_type(p << 16, F32)
    hi = lax.bitcast_convert_type(p & jnp.uint32(HIGH_HALF), F32)
    return jnp.concatenate([lo, hi], axis=1).astype(BF16)


def _pad_rows(x, rows):
    return jnp.concatenate([x, jnp.zeros((rows - x.shape[0], x.shape[1]), x.dtype)], axis=0)


def _group_rmsnorm(u, g):
    outs = []
    for k in range(SSM_GROUPS):
        ug = u[:, k * GROUP_WIDTH:(k + 1) * GROUP_WIDTH]
        ms = jnp.mean(ug * ug, axis=-1, keepdims=True)
        outs.append(ug * lax.rsqrt(ms + EPS))
    return jnp.concatenate(outs, axis=1) * g


def _memkv_body(mem_ref, g_ref, wk_ref, wv_ref, k_ref, v_ref, kb_ref, vb_ref):
    mn = _rms(mem_ref[...], g_ref[...]).astype(BF16)
    k = _dot(mn, wk_ref[...])
    v = _dot(mn, wv_ref[...])
    for hd in range(XA_HEADS):
        sl = slice(hd * XA_HEAD_DIM, (hd + 1) * XA_HEAD_DIM)
        k_ref[:, hd, :] = k[:, sl]
        v_ref[:, hd, :] = v[:, sl]
    kb_ref[...] = k.astype(BF16)
    vb_ref[...] = v.astype(BF16)


def _mem_kv(mem2d, norm_mem, wk, wv):
    rows = mem2d.shape[0]
    nb = rows // N_MEM
    blk = pl.BlockSpec((N_MEM, D_MODEL), lambda b: (b, 0))
    head_blk = pl.BlockSpec((None, N_MEM, XA_HEADS, XA_HEAD_DIM), lambda b: (b, 0, 0, 0))
    return pl.pallas_call(
        _memkv_body,
        grid=(nb,),
        in_specs=[blk, _const_spec((1, D_MODEL)), _const_spec((D_MODEL, D_MODEL)),
                  _const_spec((D_MODEL, D_MODEL))],
        out_specs=[head_blk, head_blk, blk, blk],
        out_shape=[jax.ShapeDtypeStruct((nb, N_MEM, XA_HEADS, XA_HEAD_DIM), F32)] * 2
        + [jax.ShapeDtypeStruct((rows, D_MODEL), BF16)] * 2,
        compiler_params=_cparams("arbitrary"),
        name="mem_kv",
    )(mem2d, norm_mem, wk, wv)


def _mix_body(x_ref, gmix_ref, wa_ref, wdtc_ref, wdtr_ref, wb_ref, wmc_ref, bmc_ref,
              dtb_ref, dtbt_ref, alog_ref, alogt_ref, dskip_ref, gssm_ref, wsc_ref, wout_ref,
              h_ref, ssm_ref, mbuf_ref, sbuf_ref,
              st_ref, cbuf_ref, scbuf_ref):
    tq = MIX_TILE
    c = pl.program_id(1)

    @pl.when(c == 0)
    def _():
        st_ref[...] = jnp.zeros_like(st_ref)
        cbuf_ref[0:SUBLANES, :] = jnp.zeros((SUBLANES, SSM_CONV_DIM), F32)
        scbuf_ref[0:SUBLANES, :] = jnp.zeros((SUBLANES, D_MODEL), F32)

    x = x_ref[...]
    xn = _rms(x, gmix_ref[...]).astype(BF16)

    u = _dot(xn, wa_ref[:, SSM_INNER:])
    cbuf_ref[SUBLANES:SUBLANES + tq, :] = u
    wm = wmc_ref[...]
    conv = u * wm[SSM_CONV - 1:SSM_CONV, :] + bmc_ref[...]
    for k in range(SSM_CONV - 1):
        off = SUBLANES - (SSM_CONV - 1) + k
        conv = conv + cbuf_ref[off:off + tq, :] * wm[k:k + 1, :]
    tail = cbuf_ref[tq + SUBLANES - (SSM_CONV - 1):tq + SUBLANES, :]
    mbuf_ref[...] = tail
    cbuf_ref[SUBLANES - (SSM_CONV - 1):SUBLANES, :] = tail
    xbc = _silu(conv)
    xs = xbc[:, :SSM_INNER]
    bm = xbc[:, SSM_INNER:SSM_INNER + SSM_GROUPS * SSM_STATE]
    cm = xbc[:, SSM_INNER + SSM_GROUPS * SSM_STATE:]

    dt = _softplus(_dot(xn, wdtc_ref[...]) + dtb_ref[...])
    dtt = _softplus(_dot_nt(wdtr_ref[...], xn) + dtbt_ref[...])
    a_row = -jnp.exp(alog_ref[...])
    a_col = -jnp.exp(alogt_ref[...])
    row_i = lax.broadcasted_iota(I32, (tq, tq), 0)
    col_i = lax.broadcasted_iota(I32, (tq, tq), 1)
    causal = row_i >= col_i
    a_cum = _cumsum(dt * a_row, 0)
    a_cumt = _cumsum(dtt * a_col, 1)
    a_last = a_cum[tq - 1:tq, :]

    xdt = xs * _expand_heads(dt)
    in_decay = _expand_heads(jnp.exp(a_cum))
    to_end = _expand_heads(jnp.exp(a_last - a_cum))
    chunk_decay = _expand_heads(jnp.exp(a_last))
    xdt_b = xdt.astype(BF16)
    xend_b = (xdt * to_end).astype(BF16)
    lane = lax.broadcasted_iota(I32, (tq, LANES), 1)

    def proj_b(k):
        return _dot(xn, wb_ref[:, k * D_MODEL:(k + 1) * D_MODEL])

    pb = []
    y_groups = []
    for g in range(SSM_GROUPS):
        pb.append(proj_b(g))
        if g == 0:
            z = _dot(xn, wa_ref[:, :SSM_INNER])
        if g == 2:
            g_b = proj_b(SSM_GROUPS)
        cg = cm[:, g * SSM_STATE:(g + 1) * SSM_STATE].astype(BF16)
        bg_f = bm[:, g * SSM_STATE:(g + 1) * SSM_STATE]
        bg = bg_f.astype(BF16)
        scores = _dot_nt(cg, bg)
        gs = slice(g * GROUP_WIDTH, (g + 1) * GROUP_WIDTH)
        st_g = st_ref[:, gs]
        y_off = _dot(cg, st_g.astype(BF16)) * in_decay[:, gs]
        pair_out = []
        for pr in range(HEADS_PER_GROUP // 2):
            h0 = g * HEADS_PER_GROUP + 2 * pr
            xp = xdt_b[:, h0 * SSM_HEAD_DIM:(h0 + 2) * SSM_HEAD_DIM]
            ys = []
            for h in (h0, h0 + 1):
                seg = a_cum[:, h:h + 1] - a_cumt[h:h + 1, :]
                decay = jnp.where(causal, jnp.exp(jnp.minimum(seg, 0.0)), 0.0)
                ys.append(_dot((scores * decay).astype(BF16), xp))
            pair_out.append(jnp.where(lane < SSM_HEAD_DIM, ys[0], ys[1]))
        y_groups.append(jnp.concatenate(pair_out, axis=1) + y_off)
        st_ref[:, gs] = st_g * chunk_decay[:, gs] + _dot(bg_f.T.astype(BF16), xend_b[:, gs])
    y = jnp.concatenate(y_groups, axis=1) + dskip_ref[...] * xs
    y_a = _group_rmsnorm(y * _silu(z), gssm_ref[...])

    sc_b, sc_c, sc_v, g_a = pb
    cv = sc_c * sc_v
    scbuf_ref[SUBLANES:SUBLANES + tq, :] = cv
    ws = wsc_ref[...]
    uc = cv * ws[SC_CONV - 1:SC_CONV, :]
    for k in range(SC_CONV - 1):
        off = SUBLANES - (SC_CONV - 1) + k
        uc = uc + scbuf_ref[off:off + tq, :] * ws[k:k + 1, :]
    stail = scbuf_ref[tq + SUBLANES - (SC_CONV - 1):tq + SUBLANES, :]
    sbuf_ref[...] = stail
    scbuf_ref[SUBLANES - (SC_CONV - 1):SUBLANES, :] = stail
    merged = _sigmoid(g_a) * y_a + _sigmoid(g_b) * (sc_b * uc)
    h_ref[...] = x + _dot(merged.astype(BF16), wout_ref[...])

    @pl.when(c == pl.num_programs(1) - 1)
    def _():
        ssm_ref[...] = st_ref[...].T


def _prompt_mixer(x2d, nb, w):
    t = x2d.shape[0]
    assert (t // nb) % MIX_TILE == 0, "prompt length must be a multiple of MIX_TILE"
    nc = t // nb // MIX_TILE
    tok = pl.BlockSpec((MIX_TILE, D_MODEL), lambda b, c: (b * nc + c, 0))
    return pl.pallas_call(
        _mix_body,
        grid=(nb, nc),
        in_specs=[tok, _const_spec((1, D_MODEL)),
                  _const_spec((D_MODEL, SSM_INNER + SSM_CONV_DIM)),
                  _const_spec((D_MODEL, SSM_HEADS)), _const_spec((SSM_HEADS, D_MODEL)),
                  _const_spec((D_MODEL, 5 * D_MODEL)),
                  _const_spec((SSM_CONV, SSM_CONV_DIM)), _const_spec((1, SSM_CONV_DIM)),
                  _const_spec((1, SSM_HEADS)), _const_spec((SSM_HEADS, 1)),
                  _const_spec((1, SSM_HEADS)), _const_spec((SSM_HEADS, 1)),
                  _const_spec((1, SSM_INNER)), _const_spec((1, SSM_INNER)),
                  _const_spec((SC_CONV, D_MODEL)), _const_spec((D_MODEL, D_MODEL))],
        out_specs=[tok,
                   pl.BlockSpec((None, SSM_INNER, SSM_STATE), lambda b, c: (b, 0, 0)),
                   pl.BlockSpec((None, SSM_CONV - 1, SSM_CONV_DIM), lambda b, c: (b, 0, 0)),
                   pl.BlockSpec((None, SC_CONV - 1, D_MODEL), lambda b, c: (b, 0, 0))],
        out_shape=[jax.ShapeDtypeStruct((t, D_MODEL), F32),
                   jax.ShapeDtypeStruct((nb, SSM_INNER, SSM_STATE), F32),
                   jax.ShapeDtypeStruct((nb, SSM_CONV - 1, SSM_CONV_DIM), F32),
                   jax.ShapeDtypeStruct((nb, SC_CONV - 1, D_MODEL), F32)],
        scratch_shapes=[pltpu.VMEM((SSM_STATE, SSM_INNER), F32),
                        pltpu.VMEM((MIX_TILE + SUBLANES, SSM_CONV_DIM), F32),
                        pltpu.VMEM((MIX_TILE + SUBLANES, D_MODEL), F32)],
        compiler_params=_cparams("arbitrary", "arbitrary"),
        name="prompt_mixer",
    )(x2d, w["norm_mix"], w["w_a"], w["w_dt"], w["w_dt_t"], w["w_b"], w["w_mconv"], w["b_mconv"],
      w["dt_bias"], w["dt_bias_t"], w["a_log"], w["a_log_t"], w["d_skip"], w["norm_ssm"],
      w["w_sconv"], w["w_out"])


def _router_tail(h2, gmoe_ref, wr_ref, br_ref, h2_ref, hn_ref, lg_ref):
    h2_ref[...] = h2
    hn = _rms(h2, gmoe_ref[...])
    hn_ref[...] = hn
    lg_ref[...] = _dot_nt(wr_ref[...], hn.astype(BF16)) + br_ref[...]


def _attn_body(h_ref, gx_ref, wq_ref, k_ref, v_ref, wo_ref, gmoe_ref, wr_ref, br_ref,
               h2_ref, hn_ref, lg_ref):
    h = h_ref[...]
    hn = _rms(h, gx_ref[...]).astype(BF16)
    q = _dot(hn, wq_ref[...]).astype(BF16)
    outs = []
    for hd in range(XA_HEADS):
        sl = slice(hd * XA_HEAD_DIM, (hd + 1) * XA_HEAD_DIM)
        s = _dot_nt(q[:, sl], k_ref[:, sl]) * (XA_HEAD_DIM ** -0.5)
        e = jnp.exp(s - jnp.max(s, axis=-1, keepdims=True))
        p = e / jnp.sum(e, axis=-1, keepdims=True)
        outs.append(_dot(p.astype(BF16), v_ref[:, sl]))
    o = jnp.concatenate(outs, axis=1).astype(BF16)
    h2 = h + _dot(o, wo_ref[...])
    _router_tail(h2, gmoe_ref, wr_ref, br_ref, h2_ref, hn_ref, lg_ref)


def _prompt_attn(h1, kb, vb, nb, w):
    t = h1.shape[0]
    assert (t // nb) % ATTN_TILE == 0, "prompt length must be a multiple of ATTN_TILE"
    nc = t // nb // ATTN_TILE
    tok = pl.BlockSpec((ATTN_TILE, D_MODEL), lambda b, c: (b * nc + c, 0))
    kv = pl.BlockSpec((N_MEM, D_MODEL), lambda b, c: (b, 0))
    return pl.pallas_call(
        _attn_body,
        grid=(nb, nc),
        in_specs=[tok, _const_spec((1, D_MODEL)), _const_spec((D_MODEL, D_MODEL)), kv, kv,
                  _const_spec((D_MODEL, D_MODEL)), _const_spec((1, D_MODEL)),
                  _const_spec((N_EXPERTS, D_MODEL)), _const_spec((N_EXPERTS, 1))],
        out_specs=[tok, tok, pl.BlockSpec((N_EXPERTS, ATTN_TILE), lambda b, c: (0, b * nc + c))],
        out_shape=[jax.ShapeDtypeStruct((t, D_MODEL), F32),
                   jax.ShapeDtypeStruct((t, D_MODEL), F32),
                   jax.ShapeDtypeStruct((N_EXPERTS, t), F32)],
        compiler_params=_cparams("arbitrary", "arbitrary"),
        name="prompt_attn",
    )(h1, w["norm_xattn"], w["w_xq"], kb, vb, w["w_xo"], w["norm_moe"], w["w_router"], w["b_router"])


def _sproj_body(x_ref, gmix_ref, wa_ref, wdtc_ref, wb_ref, wmc_ref, bmc_ref, dtb_ref, alog_ref,
                wsc_ref, mst_ref, sst_ref,
                z_ref, xs_ref, dtx_ref, dec_ref, bm_ref, cm_ref, yb_ref, sga_ref, mnew_ref, snew_ref):
    x = x_ref[...]
    xn = _rms(x, gmix_ref[...]).astype(BF16)
    pa = _dot(xn, wa_ref[...])
    z_ref[...] = pa[:, :SSM_INNER]
    u = pa[:, SSM_INNER:]
    wm = wmc_ref[...]
    conv = u * wm[SSM_CONV - 1:SSM_CONV, :] + bmc_ref[...]
    for k in range(SSM_CONV - 1):
        conv = conv + mst_ref[k] * wm[k:k + 1, :]
    for k in range(SSM_CONV - 2):
        mnew_ref[k] = mst_ref[k + 1]
    mnew_ref[SSM_CONV - 2] = u
    xbc = _silu(conv)
    xs = xbc[:, :SSM_INNER]
    xs_ref[...] = xs
    bm_ref[...] = xbc[:, SSM_INNER:SSM_INNER + SSM_GROUPS * SSM_STATE]
    cm_ref[...] = xbc[:, SSM_INNER + SSM_GROUPS * SSM_STATE:]
    dt = _softplus(_dot(xn, wdtc_ref[...]) + dtb_ref[...])
    dec_ref[...] = jnp.exp(dt * (-jnp.exp(alog_ref[...])))
    dtx_ref[...] = xs * _expand_heads(dt)
    pb = _dot(xn, wb_ref[...])
    cv = pb[:, D_MODEL:2 * D_MODEL] * pb[:, 2 * D_MODEL:3 * D_MODEL]
    ws = wsc_ref[...]
    uc = cv * ws[SC_CONV - 1:SC_CONV, :]
    for k in range(SC_CONV - 1):
        uc = uc + sst_ref[k] * ws[k:k + 1, :]
    for k in range(SC_CONV - 2):
        snew_ref[k] = sst_ref[k + 1]
    snew_ref[SC_CONV - 2] = cv
    yb_ref[...] = _sigmoid(pb[:, 4 * D_MODEL:5 * D_MODEL]) * (pb[:, 0:D_MODEL] * uc)
    sga_ref[...] = _sigmoid(pb[:, 3 * D_MODEL:4 * D_MODEL])


def _sample_proj(x, mstate_t, sstate_t, w):
    nb = x.shape[0]
    f = lambda *s: jax.ShapeDtypeStruct(s, F32)
    return pl.pallas_call(
        _sproj_body,
        out_shape=[f(nb, SSM_INNER), f(nb, SSM_INNER), f(nb, SSM_INNER), f(nb, SSM_HEADS),
                   f(nb, SSM_GROUPS * SSM_STATE), f(nb, SSM_GROUPS * SSM_STATE),
                   f(nb, D_MODEL), f(nb, D_MODEL),
                   f(SSM_CONV - 1, nb, SSM_CONV_DIM), f(SC_CONV - 1, nb, D_MODEL)],
        compiler_params=pltpu.CompilerParams(vmem_limit_bytes=VMEM_LIMIT),
        name="sample_proj",
    )(x, w["norm_mix"], w["w_a"], w["w_dt"], w["w_b"], w["w_mconv"], w["b_mconv"], w["dt_bias"],
      w["a_log"], w["w_sconv"], mstate_t, sstate_t)


def _sstate_body(dec_ref, s_ref, dtx_ref, bm_ref, cm_ref, snew_ref, y_ref):
    i = pl.program_id(0)
    rows_per_blk = LANES
    for j in range(STATE_BB):
        b = i * STATE_BB + j
        dtx_row = dtx_ref[j:j + 1, :]
        y_parts = []
        for g in range(SSM_GROUPS):
            b_row = bm_ref[j:j + 1, g * SSM_STATE:(g + 1) * SSM_STATE]
            c_row = cm_ref[j:j + 1, g * SSM_STATE:(g + 1) * SSM_STATE].astype(BF16)
            new_blocks = []
            for q in range(GROUP_WIDTH // rows_per_blk):
                r0 = g * GROUP_WIDTH + q * rows_per_blk
                dcol = jnp.broadcast_to(dtx_row[:, r0:r0 + rows_per_blk], (rows_per_blk, LANES)).T
                sub = []
                for hh in range(rows_per_blk // SSM_HEAD_DIM):
                    h = r0 // SSM_HEAD_DIM + hh
                    lo = hh * SSM_HEAD_DIM
                    s_old = s_ref[j, r0 + lo:r0 + lo + SSM_HEAD_DIM, :]
                    sub.append(s_old * dec_ref[b, h] + dcol[lo:lo + SSM_HEAD_DIM, :] * b_row)
                blk = jnp.concatenate(sub, axis=0)
                snew_ref[j, r0:r0 + rows_per_blk, :] = blk
                new_blocks.append(blk.astype(BF16))
            s_g = jnp.concatenate(new_blocks, axis=0)
            y_parts.append(_dot_nt(c_row, s_g))
        y_ref[j:j + 1, :] = jnp.concatenate(y_parts, axis=1)


def _sample_state(dec, state, dtx, bm, cm):
    nb = state.shape[0]
    row = lambda wdt: pl.BlockSpec((STATE_BB, wdt), lambda i, dec: (i, 0))
    st = pl.BlockSpec((STATE_BB, SSM_INNER, SSM_STATE), lambda i, dec: (i, 0, 0))
    return pl.pallas_call(
        _sstate_body,
        grid_spec=pltpu.PrefetchScalarGridSpec(
            num_scalar_prefetch=1, grid=(nb // STATE_BB,),
            in_specs=[st, row(SSM_INNER), row(SSM_GROUPS * SSM_STATE), row(SSM_GROUPS * SSM_STATE)],
            out_specs=[st, row(SSM_INNER)]),
        out_shape=[jax.ShapeDtypeStruct(state.shape, F32), jax.ShapeDtypeStruct((nb, SSM_INNER), F32)],
        compiler_params=_cparams("arbitrary"),
        name="sample_state",
    )(dec, state, dtx, bm, cm)


def _sfin1_body(x_ref, y_ref, xs_ref, z_ref, yb_ref, sga_ref, dskip_ref, gssm_ref, wout_ref,
                gx_ref, wq_ref, h_ref, q_ref):
    y = y_ref[...] + dskip_ref[...] * xs_ref[...]
    y_a = _group_rmsnorm(y * _silu(z_ref[...]), gssm_ref[...])
    merged = sga_ref[...] * y_a + yb_ref[...]
    h = x_ref[...] + _dot(merged.astype(BF16), wout_ref[...])
    h_ref[...] = h
    q_ref[...] = _dot(_rms(h, gx_ref[...]).astype(BF16), wq_ref[...])


def _sample_fin1(x, y, xs, z, yb, sga, w):
    nb = x.shape[0]
    return pl.pallas_call(
        _sfin1_body,
        out_shape=[jax.ShapeDtypeStruct((nb, D_MODEL), F32)] * 2,
        compiler_params=pltpu.CompilerParams(vmem_limit_bytes=VMEM_LIMIT),
        name="sample_fin1",
    )(x, y, xs, z, yb, sga, w["d_skip"], w["norm_ssm"], w["w_out"], w["norm_xattn"], w["w_xq"])


def _sattn_body(q_ref, k_ref, v_ref, o_ref):
    for j in range(ATTN_BB):
        q_row = q_ref[j]
        q4 = jnp.concatenate([q_row[:, h * XA_HEAD_DIM:(h + 1) * XA_HEAD_DIM]
                              for h in range(XA_HEADS)], axis=0)
        s = jnp.sum(k_ref[j] * q4[None], axis=-1, keepdims=True) * (XA_HEAD_DIM ** -0.5)
        e = jnp.exp(s - jnp.max(s, axis=0, keepdims=True))
        p = e / jnp.sum(e, axis=0, keepdims=True)
        o4 = jnp.sum(p * v_ref[j], axis=0)
        o_ref[j] = jnp.concatenate([o4[h:h + 1, :] for h in range(XA_HEADS)], axis=1)


def _sample_attn(q3, k3, v3):
    nb = q3.shape[0]
    qs = pl.BlockSpec((ATTN_BB, 1, D_MODEL), lambda i: (i, 0, 0))
    kv = pl.BlockSpec((ATTN_BB, N_MEM, XA_HEADS, XA_HEAD_DIM), lambda i: (i, 0, 0, 0))
    return pl.pallas_call(
        _sattn_body,
        grid=(nb // ATTN_BB,),
        in_specs=[qs, kv, kv],
        out_specs=qs,
        out_shape=jax.ShapeDtypeStruct((nb, 1, D_MODEL), F32),
        compiler_params=_cparams("arbitrary"),
        name="sample_attn",
    )(q3, k3, v3)


def _sfin2_body(h_ref, o_ref, wo_ref, gmoe_ref, wr_ref, br_ref, h2_ref, hn_ref, lg_ref):
    h2 = h_ref[...] + _dot(o_ref[...].astype(BF16), wo_ref[...])
    _router_tail(h2, gmoe_ref, wr_ref, br_ref, h2_ref, hn_ref, lg_ref)


def _sample_fin2(h1, o, w):
    nb = h1.shape[0]
    return pl.pallas_call(
        _sfin2_body,
        out_shape=[jax.ShapeDtypeStruct((nb, D_MODEL), F32)] * 2
        + [jax.ShapeDtypeStruct((N_EXPERTS, nb), F32)],
        compiler_params=pltpu.CompilerParams(vmem_limit_bytes=VMEM_LIMIT),
        name="sample_fin2",
    )(h1, o, w["w_xo"], w["norm_moe"], w["w_router"], w["b_router"])


def _pad_cols(x, cols):
    return jnp.concatenate([x, jnp.zeros((x.shape[0], cols - x.shape[1]), x.dtype)], axis=1)


def _route_body(lgp_ref, lgs_ref, g_ref, loc_ref, cnt_ref, off_ref):
    tt = lgp_ref.shape[1]
    is_sample = pl.program_id(0) == pl.num_programs(0) - 1
    col = lax.broadcasted_iota(I32, (1, tt), 1)
    valid = jnp.logical_or(jnp.logical_not(is_sample), col < lgs_ref.shape[1])
    work = jnp.where(is_sample, _pad_cols(lgs_ref[...], tt), lgp_ref[...])
    sub = lax.broadcasted_iota(I32, (N_EXPERTS, tt), 0).astype(F32)
    vals, hots = [], []
    for _ in range(TOP_K):
        m = jnp.max(work, axis=0, keepdims=True)
        idx = jnp.min(jnp.where(work == m, sub, float(N_EXPERTS)), axis=0, keepdims=True)
        hot = (sub == idx) & valid
        vals.append(m)
        hots.append(hot)
        work = jnp.where(hot, -jnp.inf, work)
    exps = [jnp.exp(v - vals[0]) for v in vals]
    tot = exps[0]
    for e in exps[1:]:
        tot = tot + e
    assigned = hots[0]
    for hot in hots[1:]:
        assigned = assigned | hot
    a = assigned.astype(BF16)
    r_i = lax.broadcasted_iota(I32, (tt, tt), 0)
    c_i = lax.broadcasted_iota(I32, (tt, tt), 1)
    rank = _dot(a, (r_i < c_i).astype(BF16))
    cnt = jnp.sum(a.astype(F32), axis=1, keepdims=True)
    cnt = jnp.floor((cnt + (SUBLANES - 1)) * (1.0 / SUBLANES)) * SUBLANES
    e_r = lax.broadcasted_iota(I32, (N_EXPERTS, N_EXPERTS), 0)
    e_c = lax.broadcasted_iota(I32, (N_EXPERTS, N_EXPERTS), 1)
    cnt_cols = jnp.broadcast_to(cnt, (N_EXPERTS, LANES)).astype(BF16)
    off = _dot((e_r > e_c).astype(BF16), cnt_cols)[:, 0:1]
    slot = rank + off
    k_sub = lax.broadcasted_iota(I32, (SUBLANES, tt), 0)
    g_out = jnp.zeros((SUBLANES, tt), F32)
    l_out = jnp.full((SUBLANES, tt), -1.0, F32)
    for k in range(TOP_K):
        lk = jnp.sum(jnp.where(hots[k], slot, 0.0), axis=0, keepdims=True)
        g_out = jnp.where(k_sub == k, jnp.where(valid, exps[k] / tot, 0.0), g_out)
        l_out = jnp.where(k_sub == k, jnp.where(valid, lk, -1.0), l_out)
    g_ref[...] = g_out
    loc_ref[...] = l_out.astype(I32)
    cnt_ref[...] = cnt.astype(I32)
    off_ref[...] = off.astype(I32)


def _route(logits_p, logits_s, tt):
    ntp = logits_p.shape[1] // tt
    nt = ntp + 1
    t = nt * tt
    tk = pl.BlockSpec((SUBLANES, tt), lambda i: (0, i))
    per_tile = pl.BlockSpec((None, N_EXPERTS, 1), lambda i: (i, 0, 0))
    return pl.pallas_call(
        _route_body,
        grid=(nt,),
        in_specs=[pl.BlockSpec((N_EXPERTS, tt), lambda i: (0, jnp.minimum(i, ntp - 1))),
                  pl.BlockSpec(logits_s.shape, lambda i: (0, 0))],
        out_specs=[tk, tk, per_tile, per_tile],
        out_shape=[jax.ShapeDtypeStruct((SUBLANES, t), F32), jax.ShapeDtypeStruct((SUBLANES, t), I32),
                   jax.ShapeDtypeStruct((nt, N_EXPERTS, 1), I32), jax.ShapeDtypeStruct((nt, N_EXPERTS, 1), I32)],
        compiler_params=_cparams("arbitrary"),
        name="moe_route",
    )(logits_p, logits_s)


def _sorted_rows(tt):
    return tt * TOP_K + N_EXPERTS * SUBLANES


def _run_copies(tt, tile, cnt_ref, off_ref, base_ref, make_copy, wait):
    if wait:
        total = off_ref[tile, N_EXPERTS - 1] + cnt_ref[tile, N_EXPERTS - 1]

        @pl.when(total > 0)
        def _():
            make_copy(0, 0, pl.multiple_of(total, SUBLANES)).wait()
        return

    def per_expert(e):
        n = cnt_ref[tile, e]

        @pl.when(n > 0)
        def _():
            make_copy(pl.multiple_of(off_ref[tile, e], SUBLANES),
                      pl.multiple_of(base_ref[tile, e], SUBLANES), pl.multiple_of(n, SUBLANES)).start()

    def four_experts(j, carry):
        for u in range(RUN_COPY_UNROLL):
            per_expert(j * RUN_COPY_UNROLL + u)
        return carry

    lax.fori_loop(0, N_EXPERTS // RUN_COPY_UNROLL, four_experts, 0)


def _dispatch_body(tm, tt, cnt_ref, off_ref, base_ref, zstart_ref, zsize_ref, loc_ref, x_ref, xs_ref, o_hbm,
                   zero_ref, srt_ref, zsem, sems):
    i = pl.program_id(0)
    last = pl.num_programs(0) - 1
    r = _sorted_rows(tt)

    def zero_copy(j):
        n = pl.multiple_of(zsize_ref[j], SUBLANES)
        dst = o_hbm.at[pl.ds(pl.multiple_of(zstart_ref[j], SUBLANES), n), :]
        return pltpu.make_async_copy(zero_ref.at[pl.ds(0, n), :], dst, zsem)

    @pl.when(i == 0)
    def _():
        zero_ref[...] = jnp.zeros_like(zero_ref)

        def start(j, carry):
            @pl.when(zsize_ref[j] > 0)
            def _():
                zero_copy(j).start()
            return carry

        def wait(j, carry):
            @pl.when(zsize_ref[j] > 0)
            def _():
                zero_copy(j).wait()
            return carry

        lax.fori_loop(0, zstart_ref.shape[0], start, 0)
        lax.fori_loop(0, zstart_ref.shape[0], wait, 0)

    loc = loc_ref[...]
    slot_i = lax.broadcasted_iota(I32, (r, tt), 0)
    hit = slot_i == loc[0:1, :]
    for k in range(1, TOP_K):
        hit = hit | (slot_i == loc[k:k + 1, :])
    buf = i % 2
    x = jnp.where(i == last, _pad_rows(xs_ref[...], tt), x_ref[...])
    srt_ref[buf] = _pack_bf16_pairs(_dot(hit.astype(BF16), x.astype(BF16)), is_bf16_valued=True)

    def copies(tile, wait):
        b = tile % 2

        def make_copy(lo, go, size):
            return pltpu.make_async_copy(srt_ref.at[b, pl.ds(lo, size), :], o_hbm.at[pl.ds(go, size), :],
                                         sems.at[b])

        _run_copies(tt, tile, cnt_ref, off_ref, base_ref, make_copy, wait)

    copies(i, False)

    @pl.when(i > 0)
    def _():
        copies(i - 1, True)

    @pl.when(i == last)
    def _():
        copies(i, True)


def _dispatch(cnt, off, base, zero_starts, zero_sizes, loc, hn_p, hn_s, n_rows, tm, tt):
    ntp = hn_p.shape[0] // tt
    smem = pl.BlockSpec(memory_space=pltpu.SMEM)
    return pl.pallas_call(
        functools.partial(_dispatch_body, tm, tt),
        grid_spec=pltpu.PrefetchScalarGridSpec(
            num_scalar_prefetch=0, grid=(ntp + 1,),
            in_specs=[smem, smem, smem, smem, smem,
                      pl.BlockSpec((SUBLANES, tt), lambda i: (0, i)),
                      pl.BlockSpec((tt, D_MODEL), lambda i: (jnp.minimum(i, ntp - 1), 0)),
                      pl.BlockSpec(hn_s.shape, lambda i: (0, 0))],
            out_specs=pl.BlockSpec(memory_space=pl.ANY),
            scratch_shapes=[pltpu.VMEM((tm, D_MODEL // 2), U32),
                            pltpu.VMEM((2, _sorted_rows(tt), D_MODEL // 2), U32),
                            pltpu.SemaphoreType.DMA, pltpu.SemaphoreType.DMA((2,))]),
        out_shape=jax.ShapeDtypeStruct((n_rows, D_MODEL // 2), U32),
        compiler_params=_cparams("arbitrary"),
        name="moe_dispatch",
    )(cnt, off, base, zero_starts, zero_sizes, loc, hn_p, hn_s)


def _expert_body(be_ref, nu_ref, bv_ref, slot_ref, nxt_ref, x_ref, wgu_hbm, bgu_ref, wdn_hbm, bdn_ref,
                 y_ref, wgu_f, wdn_f, wgu_b, wdn_b, sems):
    i = pl.program_id(0)
    tm = x_ref.shape[0]
    valid = bv_ref[i]
    expert = be_ref[i]
    slot = slot_ref[i]

    def weight_copies(e, s):
        return (pltpu.make_async_copy(wgu_hbm.at[e], wgu_f.at[s], sems.at[0, s]),
                pltpu.make_async_copy(wdn_hbm.at[e], wdn_f.at[s], sems.at[1, s]))

    @pl.when(i == 0)
    def _():
        for cp in weight_copies(expert, slot):
            cp.start()

    first = jnp.logical_and(i < nu_ref[0], jnp.logical_or(i == 0, expert != be_ref[jnp.maximum(i - 1, 0)]))

    @pl.when(first)
    def _():
        for cp in weight_copies(expert, slot):
            cp.wait()

        @pl.when(nxt_ref[i] >= 0)
        def _():
            for cp in weight_copies(nxt_ref[i], 1 - slot):
                cp.start()

    def ffn(rows, cast):
        xb = _unpack_bf16_pairs(x_ref[0:rows, :])
        gu_parts = []
        for j in range(2 * D_FF // EXPERT_COL_CHUNK):
            cs = slice(j * EXPERT_COL_CHUNK, (j + 1) * EXPERT_COL_CHUNK)
            if cast:
                wgu_b[:, cs] = wgu_f[slot, :, cs].astype(BF16)
            gu_parts.append(_dot(xb, wgu_b[:, cs]) + bgu_ref[:, cs])
        gate = jnp.minimum(jnp.concatenate(gu_parts[:len(gu_parts) // 2], axis=1), SWIGLU_LIMIT)
        up = jnp.clip(jnp.concatenate(gu_parts[len(gu_parts) // 2:], axis=1), -SWIGLU_LIMIT, SWIGLU_LIMIT)
        act = ((up + 1.0) * (gate * _sigmoid(SWIGLU_ALPHA * gate))).astype(BF16)
        y_parts = []
        for j in range(D_MODEL // EXPERT_COL_CHUNK):
            cs = slice(j * EXPERT_COL_CHUNK, (j + 1) * EXPERT_COL_CHUNK)
            if cast:
                wdn_b[:, cs] = wdn_f[slot, :, cs].astype(BF16)
            y_parts.append(_dot(act, wdn_b[:, cs]) + bdn_ref[:, cs])
        y_ref[0:rows, :] = _pack_bf16_pairs(jnp.concatenate(y_parts, axis=1))
        if rows < tm:
            y_ref[rows:tm, :] = jnp.zeros((tm - rows, D_MODEL // 2), U32)

    quarter = tm // EXPERT_ROW_SPLITS
    for q in range(1, EXPERT_ROW_SPLITS + 1):
        in_q = jnp.logical_and(valid > (q - 1) * quarter, valid <= q * quarter)
        for cast in (True, False):
            @pl.when(jnp.logical_and(in_q, first == cast))
            def _(q=q, cast=cast):
                ffn(q * quarter, cast)

    @pl.when(valid == 0)
    def _():
        y_ref[...] = jnp.zeros_like(y_ref)


def _experts(block_e, n_used, block_valid, block_slot, block_next, xs, wgu, bgu, wdn, bdn, tm):
    n_rows = xs.shape[0]
    return pl.pallas_call(
        _expert_body,
        grid_spec=pltpu.PrefetchScalarGridSpec(
            num_scalar_prefetch=5, grid=(n_rows // tm,),
            in_specs=[pl.BlockSpec((tm, D_MODEL // 2), lambda i, be, nu, *_: (jnp.minimum(i, nu[0] - 1), 0)),
                      pl.BlockSpec(memory_space=pl.ANY),
                      pl.BlockSpec((None, 1, 2 * D_FF), lambda i, be, *_: (be[i], 0, 0)),
                      pl.BlockSpec(memory_space=pl.ANY),
                      pl.BlockSpec((None, 1, D_MODEL), lambda i, be, *_: (be[i], 0, 0))],
            out_specs=pl.BlockSpec((tm, D_MODEL // 2), lambda i, *_: (i, 0)),
            scratch_shapes=[pltpu.VMEM((2, D_MODEL, 2 * D_FF), F32), pltpu.VMEM((2, D_FF, D_MODEL), F32),
                            pltpu.VMEM((D_MODEL, 2 * D_FF), BF16), pltpu.VMEM((D_FF, D_MODEL), BF16),
                            pltpu.SemaphoreType.DMA((2, 2))]),
        out_shape=jax.ShapeDtypeStruct((n_rows, D_MODEL // 2), U32),
        compiler_params=_cparams("arbitrary"),
        name="moe_experts",
    )(block_e, n_used, block_valid, block_slot, block_next, xs, wgu, bgu, wdn, bdn)


def _combine_body(tt, cnt_ref, off_ref, base_ref, loc_ref, g_ref, h_ref, hs_ref, gfin_ref, ys_hbm,
                  y_ref, ysmp_ref, buf_ref, sems):
    i = pl.program_id(0)
    last = pl.num_programs(0) - 1
    r = _sorted_rows(tt)

    def copies(tile, wait):
        b = tile % 2

        def make_copy(lo, go, size):
            return pltpu.make_async_copy(ys_hbm.at[pl.ds(go, size), :], buf_ref.at[b, pl.ds(lo, size), :],
                                         sems.at[b])

        _run_copies(tt, tile, cnt_ref, off_ref, base_ref, make_copy, wait)

    @pl.when(i == 0)
    def _():
        buf_ref[...] = jnp.zeros_like(buf_ref)
        copies(0, False)

    @pl.when(i < last)
    def _():
        copies(i + 1, False)

    copies(i, True)
    loc = loc_ref[...]
    gates = g_ref[...]
    slot_i = lax.broadcasted_iota(I32, (r, tt), 0)
    gmat = jnp.where(slot_i == loc[0:1, :], gates[0:1, :], 0.0)
    for k in range(1, TOP_K):
        gmat = gmat + jnp.where(slot_i == loc[k:k + 1, :], gates[k:k + 1, :], 0.0)
    h = jnp.where(i == last, _pad_rows(hs_ref[...], tt), h_ref[...])
    moe = lax.dot_general(gmat.astype(BF16), _unpack_bf16_pairs(buf_ref[i % 2]), TN_DIMS,
                          preferred_element_type=F32)
    y = _rms(h + moe, gfin_ref[...])

    @pl.when(i < last)
    def _():
        y_ref[...] = y

    @pl.when(i == last)
    def _():
        ysmp_ref[...] = y[0:ysmp_ref.shape[0], :]


def _combine(cnt, off, base, loc, gates, h2_p, h2_s, norm_final, ys, tt):
    ntp = h2_p.shape[0] // tt
    tok = pl.BlockSpec((tt, D_MODEL), lambda i: (jnp.minimum(i, ntp - 1), 0))
    smp = pl.BlockSpec(h2_s.shape, lambda i: (0, 0))
    tk = pl.BlockSpec((SUBLANES, tt), lambda i: (0, i))
    smem = pl.BlockSpec(memory_space=pltpu.SMEM)
    return pl.pallas_call(
        functools.partial(_combine_body, tt),
        grid_spec=pltpu.PrefetchScalarGridSpec(
            num_scalar_prefetch=0, grid=(ntp + 1,),
            in_specs=[smem, smem, smem, tk, tk, tok, smp,
                      pl.BlockSpec((1, D_MODEL), lambda i: (0, 0)),
                      pl.BlockSpec(memory_space=pl.ANY)],
            out_specs=[tok, smp],
            scratch_shapes=[pltpu.VMEM((2, _sorted_rows(tt), D_MODEL // 2), U32), pltpu.SemaphoreType.DMA((2,))]),
        out_shape=[jax.ShapeDtypeStruct(h2_p.shape, F32), jax.ShapeDtypeStruct(h2_s.shape, F32)],
        compiler_params=_cparams("arbitrary"),
        name="moe_combine",
    )(cnt, off, base, loc, gates, h2_p, h2_s, norm_final, ys)


def _moe_and_final_norm(hn_p, logits_p, h2_p, hn_s, logits_s, h2_s, w, tt, tm):
    nt = hn_p.shape[0] // tt + 1
    t = hn_p.shape[0] + hn_s.shape[0]
    gates, loc, cnt3, off3 = _route(logits_p, logits_s, tt)
    cnt = cnt3[:, :, 0]
    counts = jnp.sum(cnt, axis=0)
    padded = (counts + tm - 1) // tm * tm
    pad_end = jnp.cumsum(padded)
    start = pad_end - padded
    off = off3[:, :, 0]
    base = (start[None, :] + jnp.cumsum(cnt, axis=0) - cnt).astype(I32)
    n_blocks = (t * TOP_K + nt * N_EXPERTS * (SUBLANES - 1) + N_EXPERTS * (tm - 1)) // tm
    n_rows = n_blocks * tm
    block_start = jnp.arange(n_blocks, dtype=I32) * tm
    block_e = jnp.minimum(jnp.sum(block_start[:, None] >= pad_end[None, :], axis=-1), N_EXPERTS - 1).astype(I32)
    n_used = (pad_end[-1:] // tm).astype(I32)
    zero_starts = jnp.concatenate([start + counts, block_start]).astype(I32)
    zero_sizes = jnp.concatenate([padded - counts,
                                  jnp.where(block_start >= pad_end[-1], tm, 0)]).astype(I32)
    xs = _dispatch(cnt, off, base, zero_starts, zero_sizes, loc, hn_p, hn_s, n_rows, tm, tt)
    e_ids = jnp.arange(N_EXPERTS, dtype=I32)
    block_hot = block_e[:, None] == e_ids[None, :]

    def per_block(table):
        return jnp.sum(jnp.where(block_hot, table[None, :], 0), axis=1).astype(I32)

    block_valid = jnp.clip(per_block(start + counts) - block_start, 0, tm)
    block_valid = jnp.where(block_start < pad_end[-1], block_valid, 0).astype(I32)
    present = padded > 0
    later = present[None, :] & (e_ids[None, :] > e_ids[:, None])
    next_e = jnp.min(jnp.where(later, e_ids[None, :], N_EXPERTS), axis=1)
    next_e = jnp.where(next_e < N_EXPERTS, next_e, -1).astype(I32)
    before = present[None, :] & (e_ids[None, :] < e_ids[:, None])
    run_slot = (jnp.sum(before.astype(I32), axis=1) % 2).astype(I32)
    ys = _experts(block_e, n_used, block_valid, per_block(run_slot), per_block(next_e), xs, w["w_gate_up"], w["b_gate_up"], w["w_down"], w["b_down"], tm)
    return _combine(cnt, off, base, loc, gates, h2_p, h2_s, w["norm_final"], ys, tt)


def kernel(x_prompt, x_sample, mem_prompt, state_ssm, state_mamba_conv, state_short_conv, cache_mem_k, cache_mem_v, norm_mix, w_in, w_mconv, b_mconv, dt_bias, a_log, d_skip, norm_ssm, w_sconv, w_out, norm_xattn, norm_mem, w_xq, w_xk, w_xv, w_xo, norm_moe, w_router, b_router, w_gate_up, b_gate_up, w_down, b_down, norm_final):
    nbp, seq, _ = x_prompt.shape
    nbs = x_sample.shape[0]
    dt_lo = SSM_INNER + SSM_CONV_DIM
    w_in0 = w_in[0]
    w_dt = w_in0[:, dt_lo:dt_lo + SSM_HEADS]
    w = {
        "norm_mix": norm_mix, "norm_ssm": norm_ssm, "norm_xattn": norm_xattn, "norm_moe": norm_moe,
        "norm_final": norm_final.reshape(1, D_MODEL),
        "w_a": w_in0[:, :dt_lo].astype(BF16),
        "w_dt": w_dt.astype(BF16), "w_dt_t": w_dt.T.astype(BF16),
        "w_b": w_in0[:, dt_lo + SSM_HEADS:].astype(BF16),
        "w_mconv": w_mconv[0], "b_mconv": b_mconv,
        "dt_bias": dt_bias, "dt_bias_t": dt_bias.reshape(SSM_HEADS, 1),
        "a_log": a_log, "a_log_t": a_log.reshape(SSM_HEADS, 1),
        "d_skip": jnp.repeat(d_skip, SSM_HEAD_DIM, axis=1),
        "w_sconv": w_sconv[0], "w_out": w_out[0].astype(BF16),
        "w_xq": w_xq[0].astype(BF16), "w_xo": w_xo[0].astype(BF16),
        "w_router": w_router[0].T.astype(BF16), "b_router": b_router.reshape(N_EXPERTS, 1),
        "w_gate_up": w_gate_up[0], "b_gate_up": b_gate_up[0].reshape(N_EXPERTS, 1, 2 * D_FF),
        "w_down": w_down[0], "b_down": b_down[0].reshape(N_EXPERTS, 1, D_MODEL),
    }

    k_p, v_p, kb, vb = _mem_kv(mem_prompt.reshape(nbp * N_MEM, D_MODEL), norm_mem,
                               w_xk[0].astype(BF16), w_xv[0].astype(BF16))
    h1, ssm_p, mconv_p, sconv_p = _prompt_mixer(x_prompt.reshape(nbp * seq, D_MODEL), nbp, w)
    h2, hn, logits = _prompt_attn(h1, kb, vb, nbp, w)

    xs2 = x_sample.reshape(nbs, D_MODEL)
    mstate_t = jnp.transpose(state_mamba_conv[0], (1, 0, 2))
    sstate_t = jnp.transpose(state_short_conv[0], (1, 0, 2))
    z, xs_, dtx, dec, bm, cm, yb, sga, mnew_t, snew_t = _sample_proj(xs2, mstate_t, sstate_t, w)
    ssm_s, y_s = _sample_state(dec, state_ssm[0].reshape(nbs, SSM_INNER, SSM_STATE), dtx, bm, cm)
    h1s, q_s = _sample_fin1(xs2, y_s, xs_, z, yb, sga, w)
    o_s = _sample_attn(q_s.reshape(nbs, 1, D_MODEL),
                       cache_mem_k[0], cache_mem_v[0])
    h2s, hns, logits_s = _sample_fin2(h1s, o_s.reshape(nbs, D_MODEL), w)
    y_prompt, y_sample = _moe_and_final_norm(hn, logits, h2, hns, logits_s, h2s, w, MIX_TILE, MOE_ROW_TILE)

    return (y_prompt.reshape(nbp, seq, D_MODEL),
            y_sample.reshape(nbs, 1, D_MODEL),
            ssm_p.reshape(1, nbp, SSM_HEADS, SSM_HEAD_DIM, SSM_STATE),
            mconv_p[None], sconv_p[None],
            k_p[None], v_p[None],
            ssm_s.reshape(1, nbs, SSM_HEADS, SSM_HEAD_DIM, SSM_STATE),
            jnp.transpose(mnew_t, (1, 0, 2))[None],
            jnp.transpose(snew_t, (1, 0, 2))[None])
```

```python
import functools

import jax
import jax.numpy as jnp
from jax import lax
from jax.experimental import pallas as pl
from jax.experimental.pallas import tpu as pltpu

F32 = jnp.float32
BF16 = jnp.bfloat16
I32 = jnp.int32
U32 = jnp.uint32

D_MODEL = 1024
N_MEM = 256
SSM_HEADS = 16
SSM_HEAD_DIM = 64
SSM_INNER = SSM_HEADS * SSM_HEAD_DIM
SSM_STATE = 128
SSM_GROUPS = 4
HEADS_PER_GROUP = SSM_HEADS // SSM_GROUPS
GROUP_WIDTH = SSM_INNER // SSM_GROUPS
SSM_CONV = 4
SSM_CONV_DIM = SSM_INNER + 2 * SSM_GROUPS * SSM_STATE
SC_CONV = 3
XA_HEADS = 4
XA_HEAD_DIM = D_MODEL // XA_HEADS
N_EXPERTS = 32
TOP_K = 4
D_FF = D_MODEL
SWIGLU_LIMIT = 7.0
SWIGLU_ALPHA = 1.702
EPS = 1e-6

LANES = 128
SUBLANES = 8
V7X_VMEM_BYTES = 64 * 1024 * 1024
VMEM_LIMIT = V7X_VMEM_BYTES * 7 // 8
HIGH_HALF = 0xFFFF0000

MIX_TILE = 256
ATTN_TILE = 1024
MOE_ROW_TILE = 512
RUN_COPY_UNROLL = 4
EXPERT_ROW_SPLITS = 4
EXPERT_COL_CHUNK = 512
STATE_BB = 8
ATTN_BB = 4

NT_DIMS = (((1,), (1,)), ((), ()))
TN_DIMS = (((0,), (0,)), ((), ()))


def _cparams(*sem):
    return pltpu.CompilerParams(dimension_semantics=sem, vmem_limit_bytes=VMEM_LIMIT)


def _const_spec(shape):
    nd = len(shape)
    return pl.BlockSpec(shape, lambda *_: (0,) * nd, pipeline_mode=pl.Buffered(1))


def _sigmoid(x):
    return 0.5 * jnp.tanh(0.5 * x) + 0.5


def _silu(x):
    return x * _sigmoid(x)


def _softplus(x):
    return jnp.maximum(x, 0.0) + jnp.log(1.0 + jnp.exp(-jnp.abs(x)))


def _rms(x, g):
    ms = jnp.mean(x * x, axis=-1, keepdims=True)
    return x * lax.rsqrt(ms + EPS) * g


def _dot(a, b):
    return jnp.dot(a, b, preferred_element_type=F32)


def _dot_nt(a, b):
    return lax.dot_general(a, b, NT_DIMS, preferred_element_type=F32)


def _expand_heads(v):
    assert LANES == 2 * SSM_HEAD_DIM
    rows = v.shape[0]
    lane = lax.broadcasted_iota(I32, (rows, LANES), 1)
    pieces = []
    for j in range(SSM_HEADS // 2):
        a = jnp.broadcast_to(v[:, 2 * j:2 * j + 1], (rows, LANES))
        b = jnp.broadcast_to(v[:, 2 * j + 1:2 * j + 2], (rows, LANES))
        pieces.append(jnp.where(lane < SSM_HEAD_DIM, a, b))
    return jnp.concatenate(pieces, axis=1)


def _cumsum(x, axis):
    idx = lax.broadcasted_iota(I32, x.shape, axis)
    shift = 1
    while shift < x.shape[axis]:
        x = x + jnp.where(idx >= shift, pltpu.roll(x, shift, axis), 0.0)
        shift *= 2
    return x


def _pack_bf16_pairs(x, is_bf16_valued=False):
    w = x.shape[1] // 2
    if not is_bf16_valued:
        x = x.astype(BF16).astype(F32)
    bits = lax.bitcast_convert_type(x, U32)
    return (bits[:, w:] & jnp.uint32(HIGH_HALF)) | (bits[:, :w] >> 16)


def _unpack_bf16_pairs(p):
    lo = lax.bitcast_convert_type(p << 16, F32)
    hi = lax.bitcast_convert_type(p & jnp.uint32(HIGH_HALF), F32)
    return jnp.concatenate([lo, hi], axis=1).astype(BF16)


def _causal_dwconv(u, halo_ref, w):
    taps = w.shape[0]
    halo = halo_ref[...]
    sub = lax.broadcasted_iota(I32, halo.shape, 0)
    out = u * w[taps - 1:taps, :]
    for s in range(1, taps):
        rolled = pltpu.roll(u, s, axis=0)
        head = jnp.where(sub < s, pltpu.roll(halo, s, axis=0), rolled[0:SUBLANES, :])
        shifted = jnp.concatenate([head, rolled[SUBLANES:, :]], axis=0)
        out = out + shifted * w[taps - 1 - s:taps - s, :]
    halo_ref[...] = u[u.shape[0] - SUBLANES:, :]
    return out


def _pad_rows(x, rows):
    return jnp.concatenate([x, jnp.zeros((rows - x.shape[0], x.shape[1]), x.dtype)], axis=0)


def _group_rmsnorm(u, g):
    outs = []
    for k in range(SSM_GROUPS):
        ug = u[:, k * GROUP_WIDTH:(k + 1) * GROUP_WIDTH]
        ms = jnp.mean(ug * ug, axis=-1, keepdims=True)
        outs.append(ug * lax.rsqrt(ms + EPS))
    return jnp.concatenate(outs, axis=1) * g


def _memkv_body(mem_ref, g_ref, wk_ref, wv_ref, k_ref, v_ref, kb_ref, vb_ref):
    mn = _rms(mem_ref[...], g_ref[...]).astype(BF16)
    k = _dot(mn, wk_ref[...])
    v = _dot(mn, wv_ref[...])
    for hd in range(XA_HEADS):
        sl = slice(hd * XA_HEAD_DIM, (hd + 1) * XA_HEAD_DIM)
        k_ref[:, hd, :] = k[:, sl]
        v_ref[:, hd, :] = v[:, sl]
    kb_ref[...] = k.astype(BF16)
    vb_ref[...] = v.astype(BF16)


def _mem_kv(mem2d, norm_mem, wk, wv):
    rows = mem2d.shape[0]
    nb = rows // N_MEM
    blk = pl.BlockSpec((N_MEM, D_MODEL), lambda b: (b, 0))
    head_blk = pl.BlockSpec((None, N_MEM, XA_HEADS, XA_HEAD_DIM), lambda b: (b, 0, 0, 0))
    return pl.pallas_call(
        _memkv_body,
        grid=(nb,),
        in_specs=[blk, _const_spec((1, D_MODEL)), _const_spec((D_MODEL, D_MODEL)),
                  _const_spec((D_MODEL, D_MODEL))],
        out_specs=[head_blk, head_blk, blk, blk],
        out_shape=[jax.ShapeDtypeStruct((nb, N_MEM, XA_HEADS, XA_HEAD_DIM), F32)] * 2
        + [jax.ShapeDtypeStruct((rows, D_MODEL), BF16)] * 2,
        compiler_params=_cparams("arbitrary"),
        name="mem_kv",
    )(mem2d, norm_mem, wk, wv)


def _mix_body(x_ref, gmix_ref, wa_ref, wdtc_ref, wdtr_ref, wb_ref, wmc_ref, bmc_ref,
              dtb_ref, dtbt_ref, alog_ref, alogt_ref, dskip_ref, gssm_ref, wsc_ref, wout_ref,
              h_ref, ssm_ref, mbuf_ref, sbuf_ref,
              st_ref, cbuf_ref, scbuf_ref):
    tq = MIX_TILE
    c = pl.program_id(1)

    @pl.when(c == 0)
    def _():
        st_ref[...] = jnp.zeros_like(st_ref)
        cbuf_ref[...] = jnp.zeros_like(cbuf_ref)
        scbuf_ref[...] = jnp.zeros_like(scbuf_ref)

    x = x_ref[...]
    xn = _rms(x, gmix_ref[...]).astype(BF16)

    u = _dot(xn, wa_ref[:, SSM_INNER:])
    conv = _causal_dwconv(u, cbuf_ref, wmc_ref[...]) + bmc_ref[...]
    mbuf_ref[...] = u[tq - (SSM_CONV - 1):tq, :]
    xbc = _silu(conv)
    xs = xbc[:, :SSM_INNER]
    bm = xbc[:, SSM_INNER:SSM_INNER + SSM_GROUPS * SSM_STATE]
    cm = xbc[:, SSM_INNER + SSM_GROUPS * SSM_STATE:]

    dt = _softplus(_dot(xn, wdtc_ref[...]) + dtb_ref[...])
    dtt = _softplus(_dot_nt(wdtr_ref[...], xn) + dtbt_ref[...])
    a_row = -jnp.exp(alog_ref[...])
    a_col = -jnp.exp(alogt_ref[...])
    row_i = lax.broadcasted_iota(I32, (tq, tq), 0)
    col_i = lax.broadcasted_iota(I32, (tq, tq), 1)
    causal = row_i >= col_i
    a_cum = _cumsum(dt * a_row, 0)
    a_cumt = _cumsum(dtt * a_col, 1)
    a_last = a_cum[tq - 1:tq, :]

    xdt = xs * _expand_heads(dt)
    in_decay = _expand_heads(jnp.exp(a_cum))
    to_end = _expand_heads(jnp.exp(a_last - a_cum))
    chunk_decay = _expand_heads(jnp.exp(a_last))
    xdt_b = xdt.astype(BF16)
    xend_b = (xdt * to_end).astype(BF16)
    lane = lax.broadcasted_iota(I32, (tq, LANES), 1)

    def proj_b(k):
        return _dot(xn, wb_ref[:, k * D_MODEL:(k + 1) * D_MODEL])

    pb = []
    y_groups = []
    for g in range(SSM_GROUPS):
        pb.append(proj_b(g))
        if g == 0:
            z = _dot(xn, wa_ref[:, :SSM_INNER])
        if g == 2:
            g_b = proj_b(SSM_GROUPS)
        cg = cm[:, g * SSM_STATE:(g + 1) * SSM_STATE].astype(BF16)
        bg_f = bm[:, g * SSM_STATE:(g + 1) * SSM_STATE]
        bg = bg_f.astype(BF16)
        scores = _dot_nt(cg, bg)
        gs = slice(g * GROUP_WIDTH, (g + 1) * GROUP_WIDTH)
        st_g = st_ref[:, gs]
        y_off = _dot(cg, st_g.astype(BF16)) * in_decay[:, gs]
        pair_out = []
        for pr in range(HEADS_PER_GROUP // 2):
            h0 = g * HEADS_PER_GROUP + 2 * pr
            xp = xdt_b[:, h0 * SSM_HEAD_DIM:(h0 + 2) * SSM_HEAD_DIM]
            ys = []
            for h in (h0, h0 + 1):
                seg = a_cum[:, h:h + 1] - a_cumt[h:h + 1, :]
                decay = jnp.where(causal, jnp.exp(jnp.minimum(seg, 0.0)), 0.0)
                ys.append(_dot((scores * decay).astype(BF16), xp))
            pair_out.append(jnp.where(lane < SSM_HEAD_DIM, ys[0], ys[1]))
        y_groups.append(jnp.concatenate(pair_out, axis=1) + y_off)
        st_ref[:, gs] = st_g * chunk_decay[:, gs] + _dot(bg_f.T.astype(BF16), xend_b[:, gs])
    y = jnp.concatenate(y_groups, axis=1) + dskip_ref[...] * xs
    y_a = _group_rmsnorm(y * _silu(z), gssm_ref[...])

    sc_b, sc_c, sc_v, g_a = pb
    cv = sc_c * sc_v
    uc = _causal_dwconv(cv, scbuf_ref, wsc_ref[...])
    sbuf_ref[...] = cv[tq - (SC_CONV - 1):tq, :]
    merged = _sigmoid(g_a) * y_a + _sigmoid(g_b) * (sc_b * uc)
    h_ref[...] = x + _dot(merged.astype(BF16), wout_ref[...])

    @pl.when(c == pl.num_programs(1) - 1)
    def _():
        ssm_ref[...] = st_ref[...].T


def _prompt_mixer(x2d, nb, w):
    t = x2d.shape[0]
    assert (t // nb) % MIX_TILE == 0, "prompt length must be a multiple of MIX_TILE"
    nc = t // nb // MIX_TILE
    tok = pl.BlockSpec((MIX_TILE, D_MODEL), lambda b, c: (b * nc + c, 0))
    return pl.pallas_call(
        _mix_body,
        grid=(nb, nc),
        in_specs=[tok, _const_spec((1, D_MODEL)),
                  _const_spec((D_MODEL, SSM_INNER + SSM_CONV_DIM)),
                  _const_spec((D_MODEL, SSM_HEADS)), _const_spec((SSM_HEADS, D_MODEL)),
                  _const_spec((D_MODEL, 5 * D_MODEL)),
                  _const_spec((SSM_CONV, SSM_CONV_DIM)), _const_spec((1, SSM_CONV_DIM)),
                  _const_spec((1, SSM_HEADS)), _const_spec((SSM_HEADS, 1)),
                  _const_spec((1, SSM_HEADS)), _const_spec((SSM_HEADS, 1)),
                  _const_spec((1, SSM_INNER)), _const_spec((1, SSM_INNER)),
                  _const_spec((SC_CONV, D_MODEL)), _const_spec((D_MODEL, D_MODEL))],
        out_specs=[tok,
                   pl.BlockSpec((None, SSM_INNER, SSM_STATE), lambda b, c: (b, 0, 0)),
                   pl.BlockSpec((None, SSM_CONV - 1, SSM_CONV_DIM), lambda b, c: (b, 0, 0)),
                   pl.BlockSpec((None, SC_CONV - 1, D_MODEL), lambda b, c: (b, 0, 0))],
        out_shape=[jax.ShapeDtypeStruct((t, D_MODEL), F32),
                   jax.ShapeDtypeStruct((nb, SSM_INNER, SSM_STATE), F32),
                   jax.ShapeDtypeStruct((nb, SSM_CONV - 1, SSM_CONV_DIM), F32),
                   jax.ShapeDtypeStruct((nb, SC_CONV - 1, D_MODEL), F32)],
        scratch_shapes=[pltpu.VMEM((SSM_STATE, SSM_INNER), F32),
                        pltpu.VMEM((SUBLANES, SSM_CONV_DIM), F32),
                        pltpu.VMEM((SUBLANES, D_MODEL), F32)],
        compiler_params=_cparams("arbitrary", "arbitrary"),
        name="prompt_mixer",
    )(x2d, w["norm_mix"], w["w_a"], w["w_dt"], w["w_dt_t"], w["w_b"], w["w_mconv"], w["b_mconv"],
      w["dt_bias"], w["dt_bias_t"], w["a_log"], w["a_log_t"], w["d_skip"], w["norm_ssm"],
      w["w_sconv"], w["w_out"])


def _router_tail(h2, gmoe_ref, wr_ref, br_ref, h2_ref, hn_ref, lg_ref):
    h2_ref[...] = h2
    hn = _rms(h2, gmoe_ref[...])
    hn_ref[...] = hn
    lg_ref[...] = _dot_nt(wr_ref[...], hn.astype(BF16)) + br_ref[...]


def _attn_body(h_ref, gx_ref, wq_ref, k_ref, v_ref, wo_ref, gmoe_ref, wr_ref, br_ref,
               h2_ref, hn_ref, lg_ref):
    h = h_ref[...]
    hn = _rms(h, gx_ref[...]).astype(BF16)
    q = _dot(hn, wq_ref[...]).astype(BF16)
    outs = []
    for hd in range(XA_HEADS):
        sl = slice(hd * XA_HEAD_DIM, (hd + 1) * XA_HEAD_DIM)
        s = _dot_nt(q[:, sl], k_ref[:, sl]) * (XA_HEAD_DIM ** -0.5)
        e = jnp.exp(s - jnp.max(s, axis=-1, keepdims=True))
        p = e / jnp.sum(e, axis=-1, keepdims=True)
        outs.append(_dot(p.astype(BF16), v_ref[:, sl]))
    o = jnp.concatenate(outs, axis=1).astype(BF16)
    h2 = h + _dot(o, wo_ref[...])
    _router_tail(h2, gmoe_ref, wr_ref, br_ref, h2_ref, hn_ref, lg_ref)


def _prompt_attn(h1, kb, vb, nb, w):
    t = h1.shape[0]
    assert (t // nb) % ATTN_TILE == 0, "prompt length must be a multiple of ATTN_TILE"
    nc = t // nb // ATTN_TILE
    tok = pl.BlockSpec((ATTN_TILE, D_MODEL), lambda b, c: (b * nc + c, 0))
    kv = pl.BlockSpec((N_MEM, D_MODEL), lambda b, c: (b, 0))
    return pl.pallas_call(
        _attn_body,
        grid=(nb, nc),
        in_specs=[tok, _const_spec((1, D_MODEL)), _const_spec((D_MODEL, D_MODEL)), kv, kv,
                  _const_spec((D_MODEL, D_MODEL)), _const_spec((1, D_MODEL)),
                  _const_spec((N_EXPERTS, D_MODEL)), _const_spec((N_EXPERTS, 1))],
        out_specs=[tok, tok, pl.BlockSpec((N_EXPERTS, ATTN_TILE), lambda b, c: (0, b * nc + c))],
        out_shape=[jax.ShapeDtypeStruct((t, D_MODEL), F32),
                   jax.ShapeDtypeStruct((t, D_MODEL), F32),
                   jax.ShapeDtypeStruct((N_EXPERTS, t), F32)],
        compiler_params=_cparams("arbitrary", "arbitrary"),
        name="prompt_attn",
    )(h1, w["norm_xattn"], w["w_xq"], kb, vb, w["w_xo"], w["norm_moe"], w["w_router"], w["b_router"])


def _sproj_body(x_ref, gmix_ref, wa_ref, wdtc_ref, wb_ref, wmc_ref, bmc_ref, dtb_ref, alog_ref,
                wsc_ref, mst_ref, sst_ref,
                z_ref, xs_ref, dtx_ref, dec_ref, bm_ref, cm_ref, yb_ref, sga_ref, mnew_ref, snew_ref):
    x = x_ref[...]
    xn = _rms(x, gmix_ref[...]).astype(BF16)
    pa = _dot(xn, wa_ref[...])
    z_ref[...] = pa[:, :SSM_INNER]
    u = pa[:, SSM_INNER:]
    wm = wmc_ref[...]
    conv = u * wm[SSM_CONV - 1:SSM_CONV, :] + bmc_ref[...]
    for k in range(SSM_CONV - 1):
        conv = conv + mst_ref[k] * wm[k:k + 1, :]
    for k in range(SSM_CONV - 2):
        mnew_ref[k] = mst_ref[k + 1]
    mnew_ref[SSM_CONV - 2] = u
    xbc = _silu(conv)
    xs = xbc[:, :SSM_INNER]
    xs_ref[...] = xs
    bm_ref[...] = xbc[:, SSM_INNER:SSM_INNER + SSM_GROUPS * SSM_STATE]
    cm_ref[...] = xbc[:, SSM_INNER + SSM_GROUPS * SSM_STATE:]
    dt = _softplus(_dot(xn, wdtc_ref[...]) + dtb_ref[...])
    dec_ref[...] = jnp.exp(dt * (-jnp.exp(alog_ref[...])))
    dtx_ref[...] = xs * _expand_heads(dt)
    pb = _dot(xn, wb_ref[...])
    cv = pb[:, D_MODEL:2 * D_MODEL] * pb[:, 2 * D_MODEL:3 * D_MODEL]
    ws = wsc_ref[...]
    uc = cv * ws[SC_CONV - 1:SC_CONV, :]
    for k in range(SC_CONV - 1):
        uc = uc + sst_ref[k] * ws[k:k + 1, :]
    for k in range(SC_CONV - 2):
        snew_ref[k] = sst_ref[k + 1]
    snew_ref[SC_CONV - 2] = cv
    yb_ref[...] = _sigmoid(pb[:, 4 * D_MODEL:5 * D_MODEL]) * (pb[:, 0:D_MODEL] * uc)
    sga_ref[...] = _sigmoid(pb[:, 3 * D_MODEL:4 * D_MODEL])


def _sample_proj(x, mstate_t, sstate_t, w):
    nb = x.shape[0]
    f = lambda *s: jax.ShapeDtypeStruct(s, F32)
    return pl.pallas_call(
        _sproj_body,
        out_shape=[f(nb, SSM_INNER), f(nb, SSM_INNER), f(nb, SSM_INNER), f(nb, SSM_HEADS),
                   f(nb, SSM_GROUPS * SSM_STATE), f(nb, SSM_GROUPS * SSM_STATE),
                   f(nb, D_MODEL), f(nb, D_MODEL),
                   f(SSM_CONV - 1, nb, SSM_CONV_DIM), f(SC_CONV - 1, nb, D_MODEL)],
        compiler_params=pltpu.CompilerParams(vmem_limit_bytes=VMEM_LIMIT),
        name="sample_proj",
    )(x, w["norm_mix"], w["w_a"], w["w_dt"], w["w_b"], w["w_mconv"], w["b_mconv"], w["dt_bias"],
      w["a_log"], w["w_sconv"], mstate_t, sstate_t)


def _sstate_body(dec_ref, s_ref, dtx_ref, bm_ref, cm_ref, snew_ref, y_ref):
    i = pl.program_id(0)
    rows_per_blk = LANES
    for j in range(STATE_BB):
        b = i * STATE_BB + j
        dtx_row = dtx_ref[j:j + 1, :]
        y_parts = []
        for g in range(SSM_GROUPS):
            b_row = bm_ref[j:j + 1, g * SSM_STATE:(g + 1) * SSM_STATE]
            c_row = cm_ref[j:j + 1, g * SSM_STATE:(g + 1) * SSM_STATE].astype(BF16)
            new_blocks = []
            for q in range(GROUP_WIDTH // rows_per_blk):
                r0 = g * GROUP_WIDTH + q * rows_per_blk
                dcol = jnp.broadcast_to(dtx_row[:, r0:r0 + rows_per_blk], (rows_per_blk, LANES)).T
                sub = []
                for hh in range(rows_per_blk // SSM_HEAD_DIM):
                    h = r0 // SSM_HEAD_DIM + hh
                    lo = hh * SSM_HEAD_DIM
                    s_old = s_ref[j, r0 + lo:r0 + lo + SSM_HEAD_DIM, :]
                    sub.append(s_old * dec_ref[b, h] + dcol[lo:lo + SSM_HEAD_DIM, :] * b_row)
                blk = jnp.concatenate(sub, axis=0)
                snew_ref[j, r0:r0 + rows_per_blk, :] = blk
                new_blocks.append(blk.astype(BF16))
            s_g = jnp.concatenate(new_blocks, axis=0)
            y_parts.append(_dot_nt(c_row, s_g))
        y_ref[j:j + 1, :] = jnp.concatenate(y_parts, axis=1)


def _sample_state(dec, state, dtx, bm, cm):
    nb = state.shape[0]
    row = lambda wdt: pl.BlockSpec((STATE_BB, wdt), lambda i, dec: (i, 0))
    st = pl.BlockSpec((STATE_BB, SSM_INNER, SSM_STATE), lambda i, dec: (i, 0, 0))
    return pl.pallas_call(
        _sstate_body,
        grid_spec=pltpu.PrefetchScalarGridSpec(
            num_scalar_prefetch=1, grid=(nb // STATE_BB,),
            in_specs=[st, row(SSM_INNER), row(SSM_GROUPS * SSM_STATE), row(SSM_GROUPS * SSM_STATE)],
            out_specs=[st, row(SSM_INNER)]),
        out_shape=[jax.ShapeDtypeStruct(state.shape, F32), jax.ShapeDtypeStruct((nb, SSM_INNER), F32)],
        compiler_params=_cparams("arbitrary"),
        name="sample_state",
    )(dec, state, dtx, bm, cm)


def _sfin1_body(x_ref, y_ref, xs_ref, z_ref, yb_ref, sga_ref, dskip_ref, gssm_ref, wout_ref,
                gx_ref, wq_ref, h_ref, q_ref):
    y = y_ref[...] + dskip_ref[...] * xs_ref[...]
    y_a = _group_rmsnorm(y * _silu(z_ref[...]), gssm_ref[...])
    merged = sga_ref[...] * y_a + yb_ref[...]
    h = x_ref[...] + _dot(merged.astype(BF16), wout_ref[...])
    h_ref[...] = h
    q_ref[...] = _dot(_rms(h, gx_ref[...]).astype(BF16), wq_ref[...])


def _sample_fin1(x, y, xs, z, yb, sga, w):
    nb = x.shape[0]
    return pl.pallas_call(
        _sfin1_body,
        out_shape=[jax.ShapeDtypeStruct((nb, D_MODEL), F32)] * 2,
        compiler_params=pltpu.CompilerParams(vmem_limit_bytes=VMEM_LIMIT),
        name="sample_fin1",
    )(x, y, xs, z, yb, sga, w["d_skip"], w["norm_ssm"], w["w_out"], w["norm_xattn"], w["w_xq"])


def _sattn_body(q_ref, k_ref, v_ref, o_ref):
    for j in range(ATTN_BB):
        q_row = q_ref[j]
        q4 = jnp.concatenate([q_row[:, h * XA_HEAD_DIM:(h + 1) * XA_HEAD_DIM]
                              for h in range(XA_HEADS)], axis=0)
        s = jnp.sum(k_ref[j] * q4[None], axis=-1, keepdims=True) * (XA_HEAD_DIM ** -0.5)
        e = jnp.exp(s - jnp.max(s, axis=0, keepdims=True))
        p = e / jnp.sum(e, axis=0, keepdims=True)
        o4 = jnp.sum(p * v_ref[j], axis=0)
        o_ref[j] = jnp.concatenate([o4[h:h + 1, :] for h in range(XA_HEADS)], axis=1)


def _sample_attn(q3, k3, v3):
    nb = q3.shape[0]
    qs = pl.BlockSpec((ATTN_BB, 1, D_MODEL), lambda i: (i, 0, 0))
    kv = pl.BlockSpec((ATTN_BB, N_MEM, XA_HEADS, XA_HEAD_DIM), lambda i: (i, 0, 0, 0))
    return pl.pallas_call(
        _sattn_body,
        grid=(nb // ATTN_BB,),
        in_specs=[qs, kv, kv],
        out_specs=qs,
        out_shape=jax.ShapeDtypeStruct((nb, 1, D_MODEL), F32),
        compiler_params=_cparams("arbitrary"),
        name="sample_attn",
    )(q3, k3, v3)


def _sfin2_body(h_ref, o_ref, wo_ref, gmoe_ref, wr_ref, br_ref, h2_ref, hn_ref, lg_ref):
    h2 = h_ref[...] + _dot(o_ref[...].astype(BF16), wo_ref[...])
    _router_tail(h2, gmoe_ref, wr_ref, br_ref, h2_ref, hn_ref, lg_ref)


def _sample_fin2(h1, o, w):
    nb = h1.shape[0]
    return pl.pallas_call(
        _sfin2_body,
        out_shape=[jax.ShapeDtypeStruct((nb, D_MODEL), F32)] * 2
        + [jax.ShapeDtypeStruct((N_EXPERTS, nb), F32)],
        compiler_params=pltpu.CompilerParams(vmem_limit_bytes=VMEM_LIMIT),
        name="sample_fin2",
    )(h1, o, w["w_xo"], w["norm_moe"], w["w_router"], w["b_router"])


def _pad_cols(x, cols):
    return jnp.concatenate([x, jnp.zeros((x.shape[0], cols - x.shape[1]), x.dtype)], axis=1)


def _route_body(lgp_ref, lgs_ref, g_ref, loc_ref, cnt_ref, off_ref):
    tt = lgp_ref.shape[1]
    is_sample = pl.program_id(0) == pl.num_programs(0) - 1
    col = lax.broadcasted_iota(I32, (1, tt), 1)
    valid = jnp.logical_or(jnp.logical_not(is_sample), col < lgs_ref.shape[1])
    work = jnp.where(is_sample, _pad_cols(lgs_ref[...], tt), lgp_ref[...])
    sub = lax.broadcasted_iota(I32, (N_EXPERTS, tt), 0).astype(F32)
    vals, hots = [], []
    for _ in range(TOP_K):
        m = jnp.max(work, axis=0, keepdims=True)
        idx = jnp.min(jnp.where(work == m, sub, float(N_EXPERTS)), axis=0, keepdims=True)
        hot = (sub == idx) & valid
        vals.append(m)
        hots.append(hot)
        work = jnp.where(hot, -jnp.inf, work)
    exps = [jnp.exp(v - vals[0]) for v in vals]
    tot = exps[0]
    for e in exps[1:]:
        tot = tot + e
    assigned = hots[0]
    for hot in hots[1:]:
        assigned = assigned | hot
    a = assigned.astype(BF16)
    r_i = lax.broadcasted_iota(I32, (tt, tt), 0)
    c_i = lax.broadcasted_iota(I32, (tt, tt), 1)
    rank = _dot(a, (r_i < c_i).astype(BF16))
    cnt = jnp.sum(a.astype(F32), axis=1, keepdims=True)
    cnt = jnp.floor((cnt + (SUBLANES - 1)) * (1.0 / SUBLANES)) * SUBLANES
    e_r = lax.broadcasted_iota(I32, (N_EXPERTS, N_EXPERTS), 0)
    e_c = lax.broadcasted_iota(I32, (N_EXPERTS, N_EXPERTS), 1)
    cnt_cols = jnp.broadcast_to(cnt, (N_EXPERTS, LANES)).astype(BF16)
    off = _dot((e_r > e_c).astype(BF16), cnt_cols)[:, 0:1]
    slot = rank + off
    k_sub = lax.broadcasted_iota(I32, (SUBLANES, tt), 0)
    g_out = jnp.zeros((SUBLANES, tt), F32)
    l_out = jnp.full((SUBLANES, tt), -1.0, F32)
    for k in range(TOP_K):
        lk = jnp.sum(jnp.where(hots[k], slot, 0.0), axis=0, keepdims=True)
        g_out = jnp.where(k_sub == k, jnp.where(valid, exps[k] / tot, 0.0), g_out)
        l_out = jnp.where(k_sub == k, jnp.where(valid, lk, -1.0), l_out)
    g_ref[...] = g_out
    loc_ref[...] = l_out.astype(I32)
    cnt_ref[...] = cnt.astype(I32)
    off_ref[...] = off.astype(I32)


def _route(logits_p, logits_s, tt):
    ntp = logits_p.shape[1] // tt
    nt = ntp + 1
    t = nt * tt
    tk = pl.BlockSpec((SUBLANES, tt), lambda i: (0, i))
    per_tile = pl.BlockSpec((None, N_EXPERTS, 1), lambda i: (i, 0, 0))
    return pl.pallas_call(
        _route_body,
        grid=(nt,),
        in_specs=[pl.BlockSpec((N_EXPERTS, tt), lambda i: (0, jnp.minimum(i, ntp - 1))),
                  pl.BlockSpec(logits_s.shape, lambda i: (0, 0))],
        out_specs=[tk, tk, per_tile, per_tile],
        out_shape=[jax.ShapeDtypeStruct((SUBLANES, t), F32), jax.ShapeDtypeStruct((SUBLANES, t), I32),
                   jax.ShapeDtypeStruct((nt, N_EXPERTS, 1), I32), jax.ShapeDtypeStruct((nt, N_EXPERTS, 1), I32)],
        compiler_params=_cparams("arbitrary"),
        name="moe_route",
    )(logits_p, logits_s)


def _sorted_rows(tt):
    return tt * TOP_K + N_EXPERTS * SUBLANES


def _run_copies(tt, tile, cnt_ref, off_ref, base_ref, make_copy, wait):
    if wait:
        total = off_ref[tile, N_EXPERTS - 1] + cnt_ref[tile, N_EXPERTS - 1]

        @pl.when(total > 0)
        def _():
            make_copy(0, 0, pl.multiple_of(total, SUBLANES)).wait()
        return

    def per_expert(e):
        n = cnt_ref[tile, e]

        @pl.when(n > 0)
        def _():
            make_copy(pl.multiple_of(off_ref[tile, e], SUBLANES),
                      pl.multiple_of(base_ref[tile, e], SUBLANES), pl.multiple_of(n, SUBLANES)).start()

    def four_experts(j, carry):
        for u in range(RUN_COPY_UNROLL):
            per_expert(j * RUN_COPY_UNROLL + u)
        return carry

    lax.fori_loop(0, N_EXPERTS // RUN_COPY_UNROLL, four_experts, 0)


def _dispatch_body(tm, tt, cnt_ref, off_ref, base_ref, zstart_ref, zsize_ref, loc_ref, x_ref, xs_ref, o_hbm,
                   zero_ref, srt_ref, zsem, sems):
    i = pl.program_id(0)
    last = pl.num_programs(0) - 1
    r = _sorted_rows(tt)

    def zero_copy(j):
        n = pl.multiple_of(zsize_ref[j], SUBLANES)
        dst = o_hbm.at[pl.ds(pl.multiple_of(zstart_ref[j], SUBLANES), n), :]
        return pltpu.make_async_copy(zero_ref.at[pl.ds(0, n), :], dst, zsem)

    @pl.when(i == 0)
    def _():
        zero_ref[...] = jnp.zeros_like(zero_ref)

        def start(j, carry):
            @pl.when(zsize_ref[j] > 0)
            def _():
                zero_copy(j).start()
            return carry

        def wait(j, carry):
            @pl.when(zsize_ref[j] > 0)
            def _():
                zero_copy(j).wait()
            return carry

        lax.fori_loop(0, zstart_ref.shape[0], start, 0)
        lax.fori_loop(0, zstart_ref.shape[0], wait, 0)

    loc = loc_ref[...]
    slot_i = lax.broadcasted_iota(I32, (r, tt), 0)
    hit = slot_i == loc[0:1, :]
    for k in range(1, TOP_K):
        hit = hit | (slot_i == loc[k:k + 1, :])
    buf = i % 2
    x = jnp.where(i == last, _pad_rows(xs_ref[...], tt), x_ref[...])
    srt_ref[buf] = _pack_bf16_pairs(_dot(hit.astype(BF16), x.astype(BF16)), is_bf16_valued=True)

    def copies(tile, wait):
        b = tile % 2

        def make_copy(lo, go, size):
            return pltpu.make_async_copy(srt_ref.at[b, pl.ds(lo, size), :], o_hbm.at[pl.ds(go, size), :],
                                         sems.at[b])

        _run_copies(tt, tile, cnt_ref, off_ref, base_ref, make_copy, wait)

    copies(i, False)

    @pl.when(i > 0)
    def _():
        copies(i - 1, True)

    @pl.when(i == last)
    def _():
        copies(i, True)


def _dispatch(cnt, off, base, zero_starts, zero_sizes, loc, hn_p, hn_s, n_rows, tm, tt):
    ntp = hn_p.shape[0] // tt
    smem = pl.BlockSpec(memory_space=pltpu.SMEM)
    return pl.pallas_call(
        functools.partial(_dispatch_body, tm, tt),
        grid_spec=pltpu.PrefetchScalarGridSpec(
            num_scalar_prefetch=0, grid=(ntp + 1,),
            in_specs=[smem, smem, smem, smem, smem,
                      pl.BlockSpec((SUBLANES, tt), lambda i: (0, i)),
                      pl.BlockSpec((tt, D_MODEL), lambda i: (jnp.minimum(i, ntp - 1), 0)),
                      pl.BlockSpec(hn_s.shape, lambda i: (0, 0))],
            out_specs=pl.BlockSpec(memory_space=pl.ANY),
            scratch_shapes=[pltpu.VMEM((tm, D_MODEL // 2), U32),
                            pltpu.VMEM((2, _sorted_rows(tt), D_MODEL // 2), U32),
                            pltpu.SemaphoreType.DMA, pltpu.SemaphoreType.DMA((2,))]),
        out_shape=jax.ShapeDtypeStruct((n_rows, D_MODEL // 2), U32),
        compiler_params=_cparams("arbitrary"),
        name="moe_dispatch",
    )(cnt, off, base, zero_starts, zero_sizes, loc, hn_p, hn_s)


def _expert_body(be_ref, nu_ref, bv_ref, slot_ref, nxt_ref, x_ref, wgu_hbm, bgu_ref, wdn_hbm, bdn_ref,
                 y_ref, wgu_f, wdn_f, wgu_b, wdn_b, sems):
    i = pl.program_id(0)
    tm = x_ref.shape[0]
    valid = bv_ref[i]
    expert = be_ref[i]
    slot = slot_ref[i]

    def weight_copies(e, s):
        return (pltpu.make_async_copy(wgu_hbm.at[e], wgu_f.at[s], sems.at[0, s]),
                pltpu.make_async_copy(wdn_hbm.at[e], wdn_f.at[s], sems.at[1, s]))

    @pl.when(i == 0)
    def _():
        for cp in weight_copies(expert, slot):
            cp.start()

    first = jnp.logical_and(i < nu_ref[0], jnp.logical_or(i == 0, expert != be_ref[jnp.maximum(i - 1, 0)]))

    @pl.when(first)
    def _():
        for cp in weight_copies(expert, slot):
            cp.wait()

        @pl.when(nxt_ref[i] >= 0)
        def _():
            for cp in weight_copies(nxt_ref[i], 1 - slot):
                cp.start()

    def ffn(rows, cast):
        xb = _unpack_bf16_pairs(x_ref[0:rows, :])
        gu_parts = []
        for j in range(2 * D_FF // EXPERT_COL_CHUNK):
            cs = slice(j * EXPERT_COL_CHUNK, (j + 1) * EXPERT_COL_CHUNK)
            if cast:
                wgu_b[:, cs] = wgu_f[slot, :, cs].astype(BF16)
            gu_parts.append(_dot(xb, wgu_b[:, cs]) + bgu_ref[:, cs])
        gate = jnp.minimum(jnp.concatenate(gu_parts[:len(gu_parts) // 2], axis=1), SWIGLU_LIMIT)
        up = jnp.clip(jnp.concatenate(gu_parts[len(gu_parts) // 2:], axis=1), -SWIGLU_LIMIT, SWIGLU_LIMIT)
        act = ((up + 1.0) * (gate * _sigmoid(SWIGLU_ALPHA * gate))).astype(BF16)
        y_parts = []
        for j in range(D_MODEL // EXPERT_COL_CHUNK):
            cs = slice(j * EXPERT_COL_CHUNK, (j + 1) * EXPERT_COL_CHUNK)
            if cast:
                wdn_b[:, cs] = wdn_f[slot, :, cs].astype(BF16)
            y_parts.append(_dot(act, wdn_b[:, cs]) + bdn_ref[:, cs])
        y_ref[0:rows, :] = _pack_bf16_pairs(jnp.concatenate(y_parts, axis=1))
        if rows < tm:
            y_ref[rows:tm, :] = jnp.zeros((tm - rows, D_MODEL // 2), U32)

    quarter = tm // EXPERT_ROW_SPLITS
    for q in range(1, EXPERT_ROW_SPLITS + 1):
        in_q = jnp.logical_and(valid > (q - 1) * quarter, valid <= q * quarter)
        for cast in (True, False):
            @pl.when(jnp.logical_and(in_q, first == cast))
            def _(q=q, cast=cast):
                ffn(q * quarter, cast)

    @pl.when(valid == 0)
    def _():
        y_ref[...] = jnp.zeros_like(y_ref)


def _experts(block_e, n_used, block_valid, block_slot, block_next, xs, wgu, bgu, wdn, bdn, tm):
    n_rows = xs.shape[0]
    return pl.pallas_call(
        _expert_body,
        grid_spec=pltpu.PrefetchScalarGridSpec(
            num_scalar_prefetch=5, grid=(n_rows // tm,),
            in_specs=[pl.BlockSpec((tm, D_MODEL // 2), lambda i, be, nu, *_: (jnp.minimum(i, nu[0] - 1), 0)),
                      pl.BlockSpec(memory_space=pl.ANY),
                      pl.BlockSpec((None, 1, 2 * D_FF), lambda i, be, *_: (be[i], 0, 0)),
                      pl.BlockSpec(memory_space=pl.ANY),
                      pl.BlockSpec((None, 1, D_MODEL), lambda i, be, *_: (be[i], 0, 0))],
            out_specs=pl.BlockSpec((tm, D_MODEL // 2), lambda i, *_: (i, 0)),
            scratch_shapes=[pltpu.VMEM((2, D_MODEL, 2 * D_FF), F32), pltpu.VMEM((2, D_FF, D_MODEL), F32),
                            pltpu.VMEM((D_MODEL, 2 * D_FF), BF16), pltpu.VMEM((D_FF, D_MODEL), BF16),
                            pltpu.SemaphoreType.DMA((2, 2))]),
        out_shape=jax.ShapeDtypeStruct((n_rows, D_MODEL // 2), U32),
        compiler_params=_cparams("arbitrary"),
        name="moe_experts",
    )(block_e, n_used, block_valid, block_slot, block_next, xs, wgu, bgu, wdn, bdn)


def _combine_body(tt, cnt_ref, off_ref, base_ref, loc_ref, g_ref, h_ref, hs_ref, gfin_ref, ys_hbm,
                  y_ref, ysmp_ref, buf_ref, sems):
    i = pl.program_id(0)
    last = pl.num_programs(0) - 1
    r = _sorted_rows(tt)

    def copies(tile, wait):
        b = tile % 2

        def make_copy(lo, go, size):
            return pltpu.make_async_copy(ys_hbm.at[pl.ds(go, size), :], buf_ref.at[b, pl.ds(lo, size), :],
                                         sems.at[b])

        _run_copies(tt, tile, cnt_ref, off_ref, base_ref, make_copy, wait)

    @pl.when(i == 0)
    def _():
        buf_ref[...] = jnp.zeros_like(buf_ref)
        copies(0, False)

    @pl.when(i < last)
    def _():
        copies(i + 1, False)

    copies(i, True)
    loc = loc_ref[...]
    gates = g_ref[...]
    slot_i = lax.broadcasted_iota(I32, (r, tt), 0)
    gmat = jnp.where(slot_i == loc[0:1, :], gates[0:1, :], 0.0)
    for k in range(1, TOP_K):
        gmat = gmat + jnp.where(slot_i == loc[k:k + 1, :], gates[k:k + 1, :], 0.0)
    h = jnp.where(i == last, _pad_rows(hs_ref[...], tt), h_ref[...])
    moe = lax.dot_general(gmat.astype(BF16), _unpack_bf16_pairs(buf_ref[i % 2]), TN_DIMS,
                          preferred_element_type=F32)
    y = _rms(h + moe, gfin_ref[...])

    @pl.when(i < last)
    def _():
        y_ref[...] = y

    @pl.when(i == last)
    def _():
        ysmp_ref[...] = y[0:ysmp_ref.shape[0], :]


def _combine(cnt, off, base, loc, gates, h2_p, h2_s, norm_final, ys, tt):
    ntp = h2_p.shape[0] // tt
    tok = pl.BlockSpec((tt, D_MODEL), lambda i: (jnp.minimum(i, ntp - 1), 0))
    smp = pl.BlockSpec(h2_s.shape, lambda i: (0, 0))
    tk = pl.BlockSpec((SUBLANES, tt), lambda i: (0, i))
    smem = pl.BlockSpec(memory_space=pltpu.SMEM)
    return pl.pallas_call(
        functools.partial(_combine_body, tt),
        grid_spec=pltpu.PrefetchScalarGridSpec(
            num_scalar_prefetch=0, grid=(ntp + 1,),
            in_specs=[smem, smem, smem, tk, tk, tok, smp,
                      pl.BlockSpec((1, D_MODEL), lambda i: (0, 0)),
                      pl.BlockSpec(memory_space=pl.ANY)],
            out_specs=[tok, smp],
            scratch_shapes=[pltpu.VMEM((2, _sorted_rows(tt), D_MODEL // 2), U32), pltpu.SemaphoreType.DMA((2,))]),
        out_shape=[jax.ShapeDtypeStruct(h2_p.shape, F32), jax.ShapeDtypeStruct(h2_s.shape, F32)],
        compiler_params=_cparams("arbitrary"),
        name="moe_combine",
    )(cnt, off, base, loc, gates, h2_p, h2_s, norm_final, ys)


def _moe_and_final_norm(hn_p, logits_p, h2_p, hn_s, logits_s, h2_s, w, tt, tm):
    nt = hn_p.shape[0] // tt + 1
    t = hn_p.shape[0] + hn_s.shape[0]
    gates, loc, cnt3, off3 = _route(logits_p, logits_s, tt)
    cnt = cnt3[:, :, 0]
    counts = jnp.sum(cnt, axis=0)
    padded = (counts + tm - 1) // tm * tm
    pad_end = jnp.cumsum(padded)
    start = pad_end - padded
    off = off3[:, :, 0]
    base = (start[None, :] + jnp.cumsum(cnt, axis=0) - cnt).astype(I32)
    n_blocks = (t * TOP_K + nt * N_EXPERTS * (SUBLANES - 1) + N_EXPERTS * (tm - 1)) // tm
    n_rows = n_blocks * tm
    block_start = jnp.arange(n_blocks, dtype=I32) * tm
    block_e = jnp.minimum(jnp.sum(block_start[:, None] >= pad_end[None, :], axis=-1), N_EXPERTS - 1).astype(I32)
    n_used = (pad_end[-1:] // tm).astype(I32)
    zero_starts = jnp.concatenate([start + counts, block_start]).astype(I32)
    zero_sizes = jnp.concatenate([padded - counts,
                                  jnp.where(block_start >= pad_end[-1], tm, 0)]).astype(I32)
    xs = _dispatch(cnt, off, base, zero_starts, zero_sizes, loc, hn_p, hn_s, n_rows, tm, tt)
    e_ids = jnp.arange(N_EXPERTS, dtype=I32)
    block_hot = block_e[:, None] == e_ids[None, :]

    def per_block(table):
        return jnp.sum(jnp.where(block_hot, table[None, :], 0), axis=1).astype(I32)

    block_valid = jnp.clip(per_block(start + counts) - block_start, 0, tm)
    block_valid = jnp.where(block_start < pad_end[-1], block_valid, 0).astype(I32)
    present = padded > 0
    later = present[None, :] & (e_ids[None, :] > e_ids[:, None])
    next_e = jnp.min(jnp.where(later, e_ids[None, :], N_EXPERTS), axis=1)
    next_e = jnp.where(next_e < N_EXPERTS, next_e, -1).astype(I32)
    before = present[None, :] & (e_ids[None, :] < e_ids[:, None])
    run_slot = (jnp.sum(before.astype(I32), axis=1) % 2).astype(I32)
    ys = _experts(block_e, n_used, block_valid, per_block(run_slot), per_block(next_e), xs, w["w_gate_up"], w["b_gate_up"], w["w_down"], w["b_down"], tm)
    return _combine(cnt, off, base, loc, gates, h2_p, h2_s, w["norm_final"], ys, tt)


def kernel(x_prompt, x_sample, mem_prompt, state_ssm, state_mamba_conv, state_short_conv, cache_mem_k, cache_mem_v, norm_mix, w_in, w_mconv, b_mconv, dt_bias, a_log, d_skip, norm_ssm, w_sconv, w_out, norm_xattn, norm_mem, w_xq, w_xk, w_xv, w_xo, norm_moe, w_router, b_router, w_gate_up, b_gate_up, w_down, b_down, norm_final):
    nbp, seq, _ = x_prompt.shape
    nbs = x_sample.shape[0]
    dt_lo = SSM_INNER + SSM_CONV_DIM
    w_in0 = w_in[0]
    w_dt = w_in0[:, dt_lo:dt_lo + SSM_HEADS]
    w = {
        "norm_mix": norm_mix, "norm_ssm": norm_ssm, "norm_xattn": norm_xattn, "norm_moe": norm_moe,
        "norm_final": norm_final.reshape(1, D_MODEL),
        "w_a": w_in0[:, :dt_lo].astype(BF16),
        "w_dt": w_dt.astype(BF16), "w_dt_t": w_dt.T.astype(BF16),
        "w_b": w_in0[:, dt_lo + SSM_HEADS:].astype(BF16),
        "w_mconv": w_mconv[0], "b_mconv": b_mconv,
        "dt_bias": dt_bias, "dt_bias_t": dt_bias.reshape(SSM_HEADS, 1),
        "a_log": a_log, "a_log_t": a_log.reshape(SSM_HEADS, 1),
        "d_skip": jnp.repeat(d_skip, SSM_HEAD_DIM, axis=1),
        "w_sconv": w_sconv[0], "w_out": w_out[0].astype(BF16),
        "w_xq": w_xq[0].astype(BF16), "w_xo": w_xo[0].astype(BF16),
        "w_router": w_router[0].T.astype(BF16), "b_router": b_router.reshape(N_EXPERTS, 1),
        "w_gate_up": w_gate_up[0], "b_gate_up": b_gate_up[0].reshape(N_EXPERTS, 1, 2 * D_FF),
        "w_down": w_down[0], "b_down": b_down[0].reshape(N_EXPERTS, 1, D_MODEL),
    }

    k_p, v_p, kb, vb = _mem_kv(mem_prompt.reshape(nbp * N_MEM, D_MODEL), norm_mem,
                               w_xk[0].astype(BF16), w_xv[0].astype(BF16))
    h1, ssm_p, mconv_p, sconv_p = _prompt_mixer(x_prompt.reshape(nbp * seq, D_MODEL), nbp, w)
    h2, hn, logits = _prompt_attn(h1, kb, vb, nbp, w)

    xs2 = x_sample.reshape(nbs, D_MODEL)
    mstate_t = jnp.transpose(state_mamba_conv[0], (1, 0, 2))
    sstate_t = jnp.transpose(state_short_conv[0], (1, 0, 2))
    z, xs_, dtx, dec, bm, cm, yb, sga, mnew_t, snew_t = _sample_proj(xs2, mstate_t, sstate_t, w)
    ssm_s, y_s = _sample_state(dec, state_ssm[0].reshape(nbs, SSM_INNER, SSM_STATE), dtx, bm, cm)
    h1s, q_s = _sample_fin1(xs2, y_s, xs_, z, yb, sga, w)
    o_s = _sample_attn(q_s.reshape(nbs, 1, D_MODEL),
                       cache_mem_k[0], cache_mem_v[0])
    h2s, hns, logits_s = _sample_fin2(h1s, o_s.reshape(nbs, D_MODEL), w)
    y_prompt, y_sample = _moe_and_final_norm(hn, logits, h2, hns, logits_s, h2s, w, MIX_TILE, MOE_ROW_TILE)

    return (y_prompt.reshape(nbp, seq, D_MODEL),
            y_sample.reshape(nbs, 1, D_MODEL),
            ssm_p.reshape(1, nbp, SSM_HEADS, SSM_HEAD_DIM, SSM_STATE),
            mconv_p[None], sconv_p[None],
            k_p[None], v_p[None],
            ssm_s.reshape(1, nbs, SSM_HEADS, SSM_HEAD_DIM, SSM_STATE),
            jnp.transpose(mnew_t, (1, 0, 2))[None],
            jnp.transpose(snew_t, (1, 0, 2))[None])
```

```python
import functools

import jax
import jax.numpy as jnp
from jax import lax
from jax.experimental import pallas as pl
from jax.experimental.pallas import tpu as pltpu

F32 = jnp.float32
BF16 = jnp.bfloat16
I32 = jnp.int32
U32 = jnp.uint32

D_MODEL = 1024
N_MEM = 256
SSM_HEADS = 16
SSM_HEAD_DIM = 64
SSM_INNER = SSM_HEADS * SSM_HEAD_DIM
SSM_STATE = 128
SSM_GROUPS = 4
HEADS_PER_GROUP = SSM_HEADS // SSM_GROUPS
GROUP_WIDTH = SSM_INNER // SSM_GROUPS
SSM_CONV = 4
SSM_CONV_DIM = SSM_INNER + 2 * SSM_GROUPS * SSM_STATE
SC_CONV = 3
XA_HEADS = 4
XA_HEAD_DIM = D_MODEL // XA_HEADS
N_EXPERTS = 32
TOP_K = 4
D_FF = D_MODEL
SWIGLU_LIMIT = 7.0
SWIGLU_ALPHA = 1.702
EPS = 1e-6

LANES = 128
SUBLANES = 8
V7X_VMEM_BYTES = 64 * 1024 * 1024
VMEM_LIMIT = V7X_VMEM_BYTES * 7 // 8
HIGH_HALF = 0xFFFF0000

MIX_TILE = 256
ATTN_TILE = 1024
MOE_ROW_TILE = 512
RUN_COPY_UNROLL = 4
EXPERT_ROW_SPLITS = 8
EXPERT_COL_CHUNK = 512
STATE_BB = 8
ATTN_BB = 4

NT_DIMS = (((1,), (1,)), ((), ()))
TN_DIMS = (((0,), (0,)), ((), ()))


def _cparams(*sem):
    return pltpu.CompilerParams(dimension_semantics=sem, vmem_limit_bytes=VMEM_LIMIT)


def _const_spec(shape):
    nd = len(shape)
    return pl.BlockSpec(shape, lambda *_: (0,) * nd, pipeline_mode=pl.Buffered(1))


def _sigmoid(x):
    return 0.5 * jnp.tanh(0.5 * x) + 0.5


def _silu(x):
    return x * _sigmoid(x)


def _softplus(x):
    return jnp.maximum(x, 0.0) + jnp.log(1.0 + jnp.exp(-jnp.abs(x)))


def _rms(x, g):
    ms = jnp.mean(x * x, axis=-1, keepdims=True)
    return x * lax.rsqrt(ms + EPS) * g


def _dot(a, b):
    return jnp.dot(a, b, preferred_element_type=F32)


def _dot_nt(a, b):
    return lax.dot_general(a, b, NT_DIMS, preferred_element_type=F32)


def _expand_heads(v):
    assert LANES == 2 * SSM_HEAD_DIM
    rows = v.shape[0]
    lane = lax.broadcasted_iota(I32, (rows, LANES), 1)
    pieces = []
    for j in range(SSM_HEADS // 2):
        a = jnp.broadcast_to(v[:, 2 * j:2 * j + 1], (rows, LANES))
        b = jnp.broadcast_to(v[:, 2 * j + 1:2 * j + 2], (rows, LANES))
        pieces.append(jnp.where(lane < SSM_HEAD_DIM, a, b))
    return jnp.concatenate(pieces, axis=1)


def _cumsum(x, axis):
    idx = lax.broadcasted_iota(I32, x.shape, axis)
    shift = 1
    while shift < x.shape[axis]:
        x = x + jnp.where(idx >= shift, pltpu.roll(x, shift, axis), 0.0)
        shift *= 2
    return x


def _pack_bf16_pairs(x, is_bf16_valued=False):
    w = x.shape[1] // 2
    if not is_bf16_valued:
        x = x.astype(BF16).astype(F32)
    bits = lax.bitcast_convert_type(x, U32)
    return (bits[:, w:] & jnp.uint32(HIGH_HALF)) | (bits[:, :w] >> 16)


def _unpack_bf16_pairs(p):
    lo = lax.bitcast_convert_type(p << 16, F32)
    hi = lax.bitcast_convert_type(p & jnp.uint32(HIGH_HALF), F32)
    return jnp.concatenate([lo, hi], axis=1).astype(BF16)


def _causal_dwconv(u, halo_ref, w):
    taps = w.shape[0]
    halo = halo_ref[...]
    sub = lax.broadcasted_iota(I32, halo.shape, 0)
    out = u * w[taps - 1:taps, :]
    for s in range(1, taps):
        rolled = pltpu.roll(u, s, axis=0)
        head = jnp.where(sub < s, pltpu.roll(halo, s, axis=0), rolled[0:SUBLANES, :])
        shifted = jnp.concatenate([head, rolled[SUBLANES:, :]], axis=0)
        out = out + shifted * w[taps - 1 - s:taps - s, :]
    halo_ref[...] = u[u.shape[0] - SUBLANES:, :]
    return out


def _pad_rows(x, rows):
    return jnp.concatenate([x, jnp.zeros((rows - x.shape[0], x.shape[1]), x.dtype)], axis=0)


def _group_rmsnorm(u, g):
    outs = []
    for k in range(SSM_GROUPS):
        ug = u[:, k * GROUP_WIDTH:(k + 1) * GROUP_WIDTH]
        ms = jnp.mean(ug * ug, axis=-1, keepdims=True)
        outs.append(ug * lax.rsqrt(ms + EPS))
    return jnp.concatenate(outs, axis=1) * g


def _memkv_body(mem_ref, g_ref, wk_ref, wv_ref, k_ref, v_ref, kb_ref, vb_ref):
    mn = _rms(mem_ref[...], g_ref[...]).astype(BF16)
    k = _dot(mn, wk_ref[...])
    v = _dot(mn, wv_ref[...])
    for hd in range(XA_HEADS):
        sl = slice(hd * XA_HEAD_DIM, (hd + 1) * XA_HEAD_DIM)
        k_ref[:, hd, :] = k[:, sl]
        v_ref[:, hd, :] = v[:, sl]
    kb_ref[...] = k.astype(BF16)
    vb_ref[...] = v.astype(BF16)


def _mem_kv(mem2d, norm_mem, wk, wv):
    rows = mem2d.shape[0]
    nb = rows // N_MEM
    blk = pl.BlockSpec((N_MEM, D_MODEL), lambda b: (b, 0))
    head_blk = pl.BlockSpec((None, N_MEM, XA_HEADS, XA_HEAD_DIM), lambda b: (b, 0, 0, 0))
    return pl.pallas_call(
        _memkv_body,
        grid=(nb,),
        in_specs=[blk, _const_spec((1, D_MODEL)), _const_spec((D_MODEL, D_MODEL)),
                  _const_spec((D_MODEL, D_MODEL))],
        out_specs=[head_blk, head_blk, blk, blk],
        out_shape=[jax.ShapeDtypeStruct((nb, N_MEM, XA_HEADS, XA_HEAD_DIM), F32)] * 2
        + [jax.ShapeDtypeStruct((rows, D_MODEL), BF16)] * 2,
        compiler_params=_cparams("arbitrary"),
        name="mem_kv",
    )(mem2d, norm_mem, wk, wv)


def _mix_body(x_ref, gmix_ref, wa_ref, wdtc_ref, wdtr_ref, wb_ref, wmc_ref, bmc_ref,
              dtb_ref, dtbt_ref, alog_ref, alogt_ref, dskip_ref, gssm_ref, wsc_ref, wout_ref,
              h_ref, ssm_ref, mbuf_ref, sbuf_ref,
              st_ref, cbuf_ref, scbuf_ref):
    tq = MIX_TILE
    c = pl.program_id(1)

    @pl.when(c == 0)
    def _():
        st_ref[...] = jnp.zeros_like(st_ref)
        cbuf_ref[...] = jnp.zeros_like(cbuf_ref)
        scbuf_ref[...] = jnp.zeros_like(scbuf_ref)

    x = x_ref[...]
    xn = _rms(x, gmix_ref[...]).astype(BF16)

    u = _dot(xn, wa_ref[:, SSM_INNER:])
    conv = _causal_dwconv(u, cbuf_ref, wmc_ref[...]) + bmc_ref[...]
    mbuf_ref[...] = u[tq - (SSM_CONV - 1):tq, :]
    xbc = _silu(conv)
    xs = xbc[:, :SSM_INNER]
    bm = xbc[:, SSM_INNER:SSM_INNER + SSM_GROUPS * SSM_STATE]
    cm = xbc[:, SSM_INNER + SSM_GROUPS * SSM_STATE:]

    dt = _softplus(_dot(xn, wdtc_ref[...]) + dtb_ref[...])
    dtt = _softplus(_dot_nt(wdtr_ref[...], xn) + dtbt_ref[...])
    a_row = -jnp.exp(alog_ref[...])
    a_col = -jnp.exp(alogt_ref[...])
    row_i = lax.broadcasted_iota(I32, (tq, tq), 0)
    col_i = lax.broadcasted_iota(I32, (tq, tq), 1)
    causal = row_i >= col_i
    a_cum = _cumsum(dt * a_row, 0)
    a_cumt = _cumsum(dtt * a_col, 1)
    a_last = a_cum[tq - 1:tq, :]

    xdt = xs * _expand_heads(dt)
    in_decay = _expand_heads(jnp.exp(a_cum))
    to_end = _expand_heads(jnp.exp(a_last - a_cum))
    chunk_decay = _expand_heads(jnp.exp(a_last))
    xdt_b = xdt.astype(BF16)
    xend_b = (xdt * to_end).astype(BF16)
    lane = lax.broadcasted_iota(I32, (tq, LANES), 1)

    def proj_b(k):
        return _dot(xn, wb_ref[:, k * D_MODEL:(k + 1) * D_MODEL])

    pb = []
    y_groups = []
    for g in range(SSM_GROUPS):
        pb.append(proj_b(g))
        if g == 0:
            z = _dot(xn, wa_ref[:, :SSM_INNER])
        if g == 2:
            g_b = proj_b(SSM_GROUPS)
        cg = cm[:, g * SSM_STATE:(g + 1) * SSM_STATE].astype(BF16)
        bg_f = bm[:, g * SSM_STATE:(g + 1) * SSM_STATE]
        bg = bg_f.astype(BF16)
        scores = _dot_nt(cg, bg)
        gs = slice(g * GROUP_WIDTH, (g + 1) * GROUP_WIDTH)
        st_g = st_ref[:, gs]
        y_off = _dot(cg, st_g.astype(BF16)) * in_decay[:, gs]
        pair_out = []
        for pr in range(HEADS_PER_GROUP // 2):
            h0 = g * HEADS_PER_GROUP + 2 * pr
            xp = xdt_b[:, h0 * SSM_HEAD_DIM:(h0 + 2) * SSM_HEAD_DIM]
            ys = []
            for h in (h0, h0 + 1):
                seg = a_cum[:, h:h + 1] - a_cumt[h:h + 1, :]
                decay = jnp.where(causal, jnp.exp(jnp.minimum(seg, 0.0)), 0.0)
                ys.append(_dot((scores * decay).astype(BF16), xp))
            pair_out.append(jnp.where(lane < SSM_HEAD_DIM, ys[0], ys[1]))
        y_groups.append(jnp.concatenate(pair_out, axis=1) + y_off)
        st_ref[:, gs] = st_g * chunk_decay[:, gs] + _dot(bg_f.T.astype(BF16), xend_b[:, gs])
    y = jnp.concatenate(y_groups, axis=1) + dskip_ref[...] * xs
    y_a = _group_rmsnorm(y * _silu(z), gssm_ref[...])

    sc_b, sc_c, sc_v, g_a = pb
    cv = sc_c * sc_v
    uc = _causal_dwconv(cv, scbuf_ref, wsc_ref[...])
    sbuf_ref[...] = cv[tq - (SC_CONV - 1):tq, :]
    merged = _sigmoid(g_a) * y_a + _sigmoid(g_b) * (sc_b * uc)
    h_ref[...] = x + _dot(merged.astype(BF16), wout_ref[...])

    @pl.when(c == pl.num_programs(1) - 1)
    def _():
        ssm_ref[...] = st_ref[...].T


def _prompt_mixer(x2d, nb, w):
    t = x2d.shape[0]
    assert (t // nb) % MIX_TILE == 0, "prompt length must be a multiple of MIX_TILE"
    nc = t // nb // MIX_TILE
    tok = pl.BlockSpec((MIX_TILE, D_MODEL), lambda b, c: (b * nc + c, 0))
    return pl.pallas_call(
        _mix_body,
        grid=(nb, nc),
        in_specs=[tok, _const_spec((1, D_MODEL)),
                  _const_spec((D_MODEL, SSM_INNER + SSM_CONV_DIM)),
                  _const_spec((D_MODEL, SSM_HEADS)), _const_spec((SSM_HEADS, D_MODEL)),
                  _const_spec((D_MODEL, 5 * D_MODEL)),
                  _const_spec((SSM_CONV, SSM_CONV_DIM)), _const_spec((1, SSM_CONV_DIM)),
                  _const_spec((1, SSM_HEADS)), _const_spec((SSM_HEADS, 1)),
                  _const_spec((1, SSM_HEADS)), _const_spec((SSM_HEADS, 1)),
                  _const_spec((1, SSM_INNER)), _const_spec((1, SSM_INNER)),
                  _const_spec((SC_CONV, D_MODEL)), _const_spec((D_MODEL, D_MODEL))],
        out_specs=[tok,
                   pl.BlockSpec((None, SSM_INNER, SSM_STATE), lambda b, c: (b, 0, 0)),
                   pl.BlockSpec((None, SSM_CONV - 1, SSM_CONV_DIM), lambda b, c: (b, 0, 0)),
                   pl.BlockSpec((None, SC_CONV - 1, D_MODEL), lambda b, c: (b, 0, 0))],
        out_shape=[jax.ShapeDtypeStruct((t, D_MODEL), F32),
                   jax.ShapeDtypeStruct((nb, SSM_INNER, SSM_STATE), F32),
                   jax.ShapeDtypeStruct((nb, SSM_CONV - 1, SSM_CONV_DIM), F32),
                   jax.ShapeDtypeStruct((nb, SC_CONV - 1, D_MODEL), F32)],
        scratch_shapes=[pltpu.VMEM((SSM_STATE, SSM_INNER), F32),
                        pltpu.VMEM((SUBLANES, SSM_CONV_DIM), F32),
                        pltpu.VMEM((SUBLANES, D_MODEL), F32)],
        compiler_params=_cparams("arbitrary", "arbitrary"),
        name="prompt_mixer",
    )(x2d, w["norm_mix"], w["w_a"], w["w_dt"], w["w_dt_t"], w["w_b"], w["w_mconv"], w["b_mconv"],
      w["dt_bias"], w["dt_bias_t"], w["a_log"], w["a_log_t"], w["d_skip"], w["norm_ssm"],
      w["w_sconv"], w["w_out"])


def _router_tail(h2, gmoe_ref, wr_ref, br_ref, h2_ref, hn_ref, lg_ref):
    h2_ref[...] = h2
    hn = _rms(h2, gmoe_ref[...]).astype(BF16)
    hn_ref[...] = hn
    lg_ref[...] = _dot_nt(wr_ref[...], hn) + br_ref[...]


def _attn_body(h_ref, gx_ref, wq_ref, k_ref, v_ref, wo_ref, gmoe_ref, wr_ref, br_ref,
               h2_ref, hn_ref, lg_ref):
    h = h_ref[...]
    hn = _rms(h, gx_ref[...]).astype(BF16)
    q = _dot(hn, wq_ref[...]).astype(BF16)
    outs = []
    for hd in range(XA_HEADS):
        sl = slice(hd * XA_HEAD_DIM, (hd + 1) * XA_HEAD_DIM)
        s = _dot_nt(q[:, sl], k_ref[:, sl]) * (XA_HEAD_DIM ** -0.5)
        e = jnp.exp(s - jnp.max(s, axis=-1, keepdims=True))
        p = e / jnp.sum(e, axis=-1, keepdims=True)
        outs.append(_dot(p.astype(BF16), v_ref[:, sl]))
    o = jnp.concatenate(outs, axis=1).astype(BF16)
    h2 = h + _dot(o, wo_ref[...])
    _router_tail(h2, gmoe_ref, wr_ref, br_ref, h2_ref, hn_ref, lg_ref)


def _prompt_attn(h1, kb, vb, nb, w):
    t = h1.shape[0]
    assert (t // nb) % ATTN_TILE == 0, "prompt length must be a multiple of ATTN_TILE"
    nc = t // nb // ATTN_TILE
    tok = pl.BlockSpec((ATTN_TILE, D_MODEL), lambda b, c: (b * nc + c, 0))
    kv = pl.BlockSpec((N_MEM, D_MODEL), lambda b, c: (b, 0))
    return pl.pallas_call(
        _attn_body,
        grid=(nb, nc),
        in_specs=[tok, _const_spec((1, D_MODEL)), _const_spec((D_MODEL, D_MODEL)), kv, kv,
                  _const_spec((D_MODEL, D_MODEL)), _const_spec((1, D_MODEL)),
                  _const_spec((N_EXPERTS, D_MODEL)), _const_spec((N_EXPERTS, 1))],
        out_specs=[tok, tok, pl.BlockSpec((N_EXPERTS, ATTN_TILE), lambda b, c: (0, b * nc + c))],
        out_shape=[jax.ShapeDtypeStruct((t, D_MODEL), F32),
                   jax.ShapeDtypeStruct((t, D_MODEL), BF16),
                   jax.ShapeDtypeStruct((N_EXPERTS, t), F32)],
        compiler_params=_cparams("arbitrary", "arbitrary"),
        name="prompt_attn",
    )(h1, w["norm_xattn"], w["w_xq"], kb, vb, w["w_xo"], w["norm_moe"], w["w_router"], w["b_router"])


def _sproj_body(x_ref, gmix_ref, wa_ref, wdtc_ref, wb_ref, wmc_ref, bmc_ref, dtb_ref, alog_ref,
                wsc_ref, mst_ref, sst_ref,
                z_ref, xs_ref, dtx_ref, dec_ref, bm_ref, cm_ref, yb_ref, sga_ref, mnew_ref, snew_ref):
    x = x_ref[...]
    xn = _rms(x, gmix_ref[...]).astype(BF16)
    pa = _dot(xn, wa_ref[...])
    z_ref[...] = pa[:, :SSM_INNER]
    u = pa[:, SSM_INNER:]
    wm = wmc_ref[...]
    conv = u * wm[SSM_CONV - 1:SSM_CONV, :] + bmc_ref[...]
    for k in range(SSM_CONV - 1):
        conv = conv + mst_ref[k] * wm[k:k + 1, :]
    for k in range(SSM_CONV - 2):
        mnew_ref[k] = mst_ref[k + 1]
    mnew_ref[SSM_CONV - 2] = u
    xbc = _silu(conv)
    xs = xbc[:, :SSM_INNER]
    xs_ref[...] = xs
    bm_ref[...] = xbc[:, SSM_INNER:SSM_INNER + SSM_GROUPS * SSM_STATE]
    cm_ref[...] = xbc[:, SSM_INNER + SSM_GROUPS * SSM_STATE:]
    dt = _softplus(_dot(xn, wdtc_ref[...]) + dtb_ref[...])
    dec_ref[...] = jnp.exp(dt * (-jnp.exp(alog_ref[...])))
    dtx_ref[...] = xs * _expand_heads(dt)
    pb = _dot(xn, wb_ref[...])
    cv = pb[:, D_MODEL:2 * D_MODEL] * pb[:, 2 * D_MODEL:3 * D_MODEL]
    ws = wsc_ref[...]
    uc = cv * ws[SC_CONV - 1:SC_CONV, :]
    for k in range(SC_CONV - 1):
        uc = uc + sst_ref[k] * ws[k:k + 1, :]
    for k in range(SC_CONV - 2):
        snew_ref[k] = sst_ref[k + 1]
    snew_ref[SC_CONV - 2] = cv
    yb_ref[...] = _sigmoid(pb[:, 4 * D_MODEL:5 * D_MODEL]) * (pb[:, 0:D_MODEL] * uc)
    sga_ref[...] = _sigmoid(pb[:, 3 * D_MODEL:4 * D_MODEL])


def _sample_proj(x, mstate_t, sstate_t, w):
    nb = x.shape[0]
    f = lambda *s: jax.ShapeDtypeStruct(s, F32)
    return pl.pallas_call(
        _sproj_body,
        out_shape=[f(nb, SSM_INNER), f(nb, SSM_INNER), f(nb, SSM_INNER), f(nb, SSM_HEADS),
                   f(nb, SSM_GROUPS * SSM_STATE), f(nb, SSM_GROUPS * SSM_STATE),
                   f(nb, D_MODEL), f(nb, D_MODEL),
                   f(SSM_CONV - 1, nb, SSM_CONV_DIM), f(SC_CONV - 1, nb, D_MODEL)],
        compiler_params=pltpu.CompilerParams(vmem_limit_bytes=VMEM_LIMIT),
        name="sample_proj",
    )(x, w["norm_mix"], w["w_a"], w["w_dt"], w["w_b"], w["w_mconv"], w["b_mconv"], w["dt_bias"],
      w["a_log"], w["w_sconv"], mstate_t, sstate_t)


def _sstate_body(dec_ref, s_ref, dtx_ref, bm_ref, cm_ref, snew_ref, y_ref):
    i = pl.program_id(0)
    rows_per_blk = LANES
    for j in range(STATE_BB):
        b = i * STATE_BB + j
        dtx_row = dtx_ref[j:j + 1, :]
        y_parts = []
        for g in range(SSM_GROUPS):
            b_row = bm_ref[j:j + 1, g * SSM_STATE:(g + 1) * SSM_STATE]
            c_row = cm_ref[j:j + 1, g * SSM_STATE:(g + 1) * SSM_STATE].astype(BF16)
            new_blocks = []
            for q in range(GROUP_WIDTH // rows_per_blk):
                r0 = g * GROUP_WIDTH + q * rows_per_blk
                dcol = jnp.broadcast_to(dtx_row[:, r0:r0 + rows_per_blk], (rows_per_blk, LANES)).T
                sub = []
                for hh in range(rows_per_blk // SSM_HEAD_DIM):
                    h = r0 // SSM_HEAD_DIM + hh
                    lo = hh * SSM_HEAD_DIM
                    s_old = s_ref[j, r0 + lo:r0 + lo + SSM_HEAD_DIM, :]
                    sub.append(s_old * dec_ref[b, h] + dcol[lo:lo + SSM_HEAD_DIM, :] * b_row)
                blk = jnp.concatenate(sub, axis=0)
                snew_ref[j, r0:r0 + rows_per_blk, :] = blk
                new_blocks.append(blk.astype(BF16))
            s_g = jnp.concatenate(new_blocks, axis=0)
            y_parts.append(_dot_nt(c_row, s_g))
        y_ref[j:j + 1, :] = jnp.concatenate(y_parts, axis=1)


def _sample_state(dec, state, dtx, bm, cm):
    nb = state.shape[0]
    row = lambda wdt: pl.BlockSpec((STATE_BB, wdt), lambda i, dec: (i, 0))
    st = pl.BlockSpec((STATE_BB, SSM_INNER, SSM_STATE), lambda i, dec: (i, 0, 0))
    return pl.pallas_call(
        _sstate_body,
        grid_spec=pltpu.PrefetchScalarGridSpec(
            num_scalar_prefetch=1, grid=(nb // STATE_BB,),
            in_specs=[st, row(SSM_INNER), row(SSM_GROUPS * SSM_STATE), row(SSM_GROUPS * SSM_STATE)],
            out_specs=[st, row(SSM_INNER)]),
        out_shape=[jax.ShapeDtypeStruct(state.shape, F32), jax.ShapeDtypeStruct((nb, SSM_INNER), F32)],
        compiler_params=_cparams("arbitrary"),
        name="sample_state",
    )(dec, state, dtx, bm, cm)


def _sfin1_body(x_ref, y_ref, xs_ref, z_ref, yb_ref, sga_ref, dskip_ref, gssm_ref, wout_ref,
                gx_ref, wq_ref, h_ref, q_ref):
    y = y_ref[...] + dskip_ref[...] * xs_ref[...]
    y_a = _group_rmsnorm(y * _silu(z_ref[...]), gssm_ref[...])
    merged = sga_ref[...] * y_a + yb_ref[...]
    h = x_ref[...] + _dot(merged.astype(BF16), wout_ref[...])
    h_ref[...] = h
    q_ref[...] = _dot(_rms(h, gx_ref[...]).astype(BF16), wq_ref[...])


def _sample_fin1(x, y, xs, z, yb, sga, w):
    nb = x.shape[0]
    return pl.pallas_call(
        _sfin1_body,
        out_shape=[jax.ShapeDtypeStruct((nb, D_MODEL), F32)] * 2,
        compiler_params=pltpu.CompilerParams(vmem_limit_bytes=VMEM_LIMIT),
        name="sample_fin1",
    )(x, y, xs, z, yb, sga, w["d_skip"], w["norm_ssm"], w["w_out"], w["norm_xattn"], w["w_xq"])


def _sattn_body(q_ref, k_ref, v_ref, o_ref):
    for j in range(ATTN_BB):
        q_row = q_ref[j]
        q4 = jnp.concatenate([q_row[:, h * XA_HEAD_DIM:(h + 1) * XA_HEAD_DIM]
                              for h in range(XA_HEADS)], axis=0)
        s = jnp.sum(k_ref[j] * q4[None], axis=-1, keepdims=True) * (XA_HEAD_DIM ** -0.5)
        e = jnp.exp(s - jnp.max(s, axis=0, keepdims=True))
        p = e / jnp.sum(e, axis=0, keepdims=True)
        o4 = jnp.sum(p * v_ref[j], axis=0)
        o_ref[j] = jnp.concatenate([o4[h:h + 1, :] for h in range(XA_HEADS)], axis=1)


def _sample_attn(q3, k3, v3):
    nb = q3.shape[0]
    qs = pl.BlockSpec((ATTN_BB, 1, D_MODEL), lambda i: (i, 0, 0))
    kv = pl.BlockSpec((ATTN_BB, N_MEM, XA_HEADS, XA_HEAD_DIM), lambda i: (i, 0, 0, 0))
    return pl.pallas_call(
        _sattn_body,
        grid=(nb // ATTN_BB,),
        in_specs=[qs, kv, kv],
        out_specs=qs,
        out_shape=jax.ShapeDtypeStruct((nb, 1, D_MODEL), F32),
        compiler_params=_cparams("arbitrary"),
        name="sample_attn",
    )(q3, k3, v3)


def _sfin2_body(h_ref, o_ref, wo_ref, gmoe_ref, wr_ref, br_ref, h2_ref, hn_ref, lg_ref):
    h2 = h_ref[...] + _dot(o_ref[...].astype(BF16), wo_ref[...])
    _router_tail(h2, gmoe_ref, wr_ref, br_ref, h2_ref, hn_ref, lg_ref)


def _sample_fin2(h1, o, w):
    nb = h1.shape[0]
    return pl.pallas_call(
        _sfin2_body,
        out_shape=[jax.ShapeDtypeStruct((nb, D_MODEL), F32), jax.ShapeDtypeStruct((nb, D_MODEL), BF16),
                   jax.ShapeDtypeStruct((N_EXPERTS, nb), F32)],
        compiler_params=pltpu.CompilerParams(vmem_limit_bytes=VMEM_LIMIT),
        name="sample_fin2",
    )(h1, o, w["w_xo"], w["norm_moe"], w["w_router"], w["b_router"])


def _pad_cols(x, cols):
    return jnp.concatenate([x, jnp.zeros((x.shape[0], cols - x.shape[1]), x.dtype)], axis=1)


def _route_body(lgp_ref, lgs_ref, g_ref, loc_ref, cnt_ref, off_ref):
    tt = lgp_ref.shape[1]
    is_sample = pl.program_id(0) == pl.num_programs(0) - 1
    col = lax.broadcasted_iota(I32, (1, tt), 1)
    valid = jnp.logical_or(jnp.logical_not(is_sample), col < lgs_ref.shape[1])
    work = jnp.where(is_sample, _pad_cols(lgs_ref[...], tt), lgp_ref[...])
    sub = lax.broadcasted_iota(I32, (N_EXPERTS, tt), 0).astype(F32)
    vals, hots = [], []
    for _ in range(TOP_K):
        m = jnp.max(work, axis=0, keepdims=True)
        idx = jnp.min(jnp.where(work == m, sub, float(N_EXPERTS)), axis=0, keepdims=True)
        hot = (sub == idx) & valid
        vals.append(m)
        hots.append(hot)
        work = jnp.where(hot, -jnp.inf, work)
    exps = [jnp.exp(v - vals[0]) for v in vals]
    tot = exps[0]
    for e in exps[1:]:
        tot = tot + e
    assigned = hots[0]
    for hot in hots[1:]:
        assigned = assigned | hot
    a = assigned.astype(BF16)
    r_i = lax.broadcasted_iota(I32, (tt, tt), 0)
    c_i = lax.broadcasted_iota(I32, (tt, tt), 1)
    rank = _dot(a, (r_i < c_i).astype(BF16))
    cnt = jnp.sum(a.astype(F32), axis=1, keepdims=True)
    cnt = jnp.floor((cnt + (SUBLANES - 1)) * (1.0 / SUBLANES)) * SUBLANES
    e_r = lax.broadcasted_iota(I32, (N_EXPERTS, N_EXPERTS), 0)
    e_c = lax.broadcasted_iota(I32, (N_EXPERTS, N_EXPERTS), 1)
    cnt_cols = jnp.broadcast_to(cnt, (N_EXPERTS, LANES)).astype(BF16)
    off = _dot((e_r > e_c).astype(BF16), cnt_cols)[:, 0:1]
    slot = rank + off
    k_sub = lax.broadcasted_iota(I32, (SUBLANES, tt), 0)
    g_out = jnp.zeros((SUBLANES, tt), F32)
    l_out = jnp.full((SUBLANES, tt), -1.0, F32)
    for k in range(TOP_K):
        lk = jnp.sum(jnp.where(hots[k], slot, 0.0), axis=0, keepdims=True)
        g_out = jnp.where(k_sub == k, jnp.where(valid, exps[k] / tot, 0.0), g_out)
        l_out = jnp.where(k_sub == k, jnp.where(valid, lk, -1.0), l_out)
    g_ref[...] = g_out
    loc_ref[...] = l_out.astype(I32)
    cnt_ref[...] = cnt.astype(I32)
    off_ref[...] = off.astype(I32)


def _route(logits_p, logits_s, tt):
    ntp = logits_p.shape[1] // tt
    nt = ntp + 1
    t = nt * tt
    tk = pl.BlockSpec((SUBLANES, tt), lambda i: (0, i))
    per_tile = pl.BlockSpec((None, N_EXPERTS, 1), lambda i: (i, 0, 0))
    return pl.pallas_call(
        _route_body,
        grid=(nt,),
        in_specs=[pl.BlockSpec((N_EXPERTS, tt), lambda i: (0, jnp.minimum(i, ntp - 1))),
                  pl.BlockSpec(logits_s.shape, lambda i: (0, 0))],
        out_specs=[tk, tk, per_tile, per_tile],
        out_shape=[jax.ShapeDtypeStruct((SUBLANES, t), F32), jax.ShapeDtypeStruct((SUBLANES, t), I32),
                   jax.ShapeDtypeStruct((nt, N_EXPERTS, 1), I32), jax.ShapeDtypeStruct((nt, N_EXPERTS, 1), I32)],
        compiler_params=_cparams("arbitrary"),
        name="moe_route",
    )(logits_p, logits_s)


def _sorted_rows(tt):
    return tt * TOP_K + N_EXPERTS * SUBLANES


def _run_copies(tt, tile, cnt_ref, off_ref, base_ref, make_copy, wait):
    if wait:
        total = off_ref[tile, N_EXPERTS - 1] + cnt_ref[tile, N_EXPERTS - 1]

        @pl.when(total > 0)
        def _():
            make_copy(0, 0, pl.multiple_of(total, SUBLANES)).wait()
        return

    def per_expert(e):
        n = cnt_ref[tile, e]

        @pl.when(n > 0)
        def _():
            make_copy(pl.multiple_of(off_ref[tile, e], SUBLANES),
                      pl.multiple_of(base_ref[tile, e], SUBLANES), pl.multiple_of(n, SUBLANES)).start()

    def four_experts(j, carry):
        for u in range(RUN_COPY_UNROLL):
            per_expert(j * RUN_COPY_UNROLL + u)
        return carry

    lax.fori_loop(0, N_EXPERTS // RUN_COPY_UNROLL, four_experts, 0)


def _dispatch_body(tm, tt, cnt_ref, off_ref, base_ref, zstart_ref, zsize_ref, loc_ref, x_ref, xs_ref, o_hbm,
                   zero_ref, srt_ref, zsem, sems):
    i = pl.program_id(0)
    last = pl.num_programs(0) - 1
    r = _sorted_rows(tt)

    def zero_copy(j):
        n = pl.multiple_of(zsize_ref[j], SUBLANES)
        dst = o_hbm.at[pl.ds(pl.multiple_of(zstart_ref[j], SUBLANES), n), :]
        return pltpu.make_async_copy(zero_ref.at[pl.ds(0, n), :], dst, zsem)

    @pl.when(i == 0)
    def _():
        zero_ref[...] = jnp.zeros_like(zero_ref)

        def start(j, carry):
            @pl.when(zsize_ref[j] > 0)
            def _():
                zero_copy(j).start()
            return carry

        def wait(j, carry):
            @pl.when(zsize_ref[j] > 0)
            def _():
                zero_copy(j).wait()
            return carry

        lax.fori_loop(0, zstart_ref.shape[0], start, 0)
        lax.fori_loop(0, zstart_ref.shape[0], wait, 0)

    loc = loc_ref[...]
    slot_i = lax.broadcasted_iota(I32, (r, tt), 0)
    hit = slot_i == loc[0:1, :]
    for k in range(1, TOP_K):
        hit = hit | (slot_i == loc[k:k + 1, :])
    buf = i % 2
    x = jnp.where(i == last, _pad_rows(xs_ref[...], tt), x_ref[...])
    srt_ref[buf] = _pack_bf16_pairs(_dot(hit.astype(BF16), x.astype(BF16)), is_bf16_valued=True)

    def copies(tile, wait):
        b = tile % 2

        def make_copy(lo, go, size):
            return pltpu.make_async_copy(srt_ref.at[b, pl.ds(lo, size), :], o_hbm.at[pl.ds(go, size), :],
                                         sems.at[b])

        _run_copies(tt, tile, cnt_ref, off_ref, base_ref, make_copy, wait)

    copies(i, False)

    @pl.when(i > 0)
    def _():
        copies(i - 1, True)

    @pl.when(i == last)
    def _():
        copies(i, True)


def _dispatch(cnt, off, base, zero_starts, zero_sizes, loc, hn_p, hn_s, n_rows, tm, tt):
    ntp = hn_p.shape[0] // tt
    smem = pl.BlockSpec(memory_space=pltpu.SMEM)
    return pl.pallas_call(
        functools.partial(_dispatch_body, tm, tt),
        grid_spec=pltpu.PrefetchScalarGridSpec(
            num_scalar_prefetch=0, grid=(ntp + 1,),
            in_specs=[smem, smem, smem, smem, smem,
                      pl.BlockSpec((SUBLANES, tt), lambda i: (0, i)),
                      pl.BlockSpec((tt, D_MODEL), lambda i: (jnp.minimum(i, ntp - 1), 0)),
                      pl.BlockSpec(hn_s.shape, lambda i: (0, 0))],
            out_specs=pl.BlockSpec(memory_space=pl.ANY),
            scratch_shapes=[pltpu.VMEM((tm, D_MODEL // 2), U32),
                            pltpu.VMEM((2, _sorted_rows(tt), D_MODEL // 2), U32),
                            pltpu.SemaphoreType.DMA, pltpu.SemaphoreType.DMA((2,))]),
        out_shape=jax.ShapeDtypeStruct((n_rows, D_MODEL // 2), U32),
        compiler_params=_cparams("arbitrary"),
        name="moe_dispatch",
    )(cnt, off, base, zero_starts, zero_sizes, loc, hn_p, hn_s)


def _expert_body(be_ref, nu_ref, bv_ref, slot_ref, nxt_ref, x_ref, wgu_hbm, bgu_ref, wdn_hbm, bdn_ref,
                 y_ref, wgu_f, wdn_f, wgu_b, wdn_b, sems):
    i = pl.program_id(0)
    tm = x_ref.shape[0]
    valid = bv_ref[i]
    expert = be_ref[i]
    slot = slot_ref[i]

    def weight_copies(e, s):
        return (pltpu.make_async_copy(wgu_hbm.at[e], wgu_f.at[s], sems.at[0, s]),
                pltpu.make_async_copy(wdn_hbm.at[e], wdn_f.at[s], sems.at[1, s]))

    @pl.when(i == 0)
    def _():
        for cp in weight_copies(expert, slot):
            cp.start()

    first = jnp.logical_and(i < nu_ref[0], jnp.logical_or(i == 0, expert != be_ref[jnp.maximum(i - 1, 0)]))

    @pl.when(first)
    def _():
        for cp in weight_copies(expert, slot):
            cp.wait()

        @pl.when(nxt_ref[i] >= 0)
        def _():
            for cp in weight_copies(nxt_ref[i], 1 - slot):
                cp.start()

    def ffn(rows, cast):
        xb = _unpack_bf16_pairs(x_ref[0:rows, :])
        gu_parts = []
        for j in range(2 * D_FF // EXPERT_COL_CHUNK):
            cs = slice(j * EXPERT_COL_CHUNK, (j + 1) * EXPERT_COL_CHUNK)
            if cast:
                wgu_b[:, cs] = wgu_f[slot, :, cs].astype(BF16)
            gu_parts.append(_dot(xb, wgu_b[:, cs]) + bgu_ref[:, cs])
        gate = jnp.minimum(jnp.concatenate(gu_parts[:len(gu_parts) // 2], axis=1), SWIGLU_LIMIT)
        up = jnp.clip(jnp.concatenate(gu_parts[len(gu_parts) // 2:], axis=1), -SWIGLU_LIMIT, SWIGLU_LIMIT)
        act = ((up + 1.0) * (gate * _sigmoid(SWIGLU_ALPHA * gate))).astype(BF16)
        y_parts = []
        for j in range(D_MODEL // EXPERT_COL_CHUNK):
            cs = slice(j * EXPERT_COL_CHUNK, (j + 1) * EXPERT_COL_CHUNK)
            if cast:
                wdn_b[:, cs] = wdn_f[slot, :, cs].astype(BF16)
            y_parts.append(_dot(act, wdn_b[:, cs]) + bdn_ref[:, cs])
        y_ref[0:rows, :] = _pack_bf16_pairs(jnp.concatenate(y_parts, axis=1))
        if rows < tm:
            y_ref[rows:tm, :] = jnp.zeros((tm - rows, D_MODEL // 2), U32)

    piece = tm // EXPERT_ROW_SPLITS
    for q in range(1, EXPERT_ROW_SPLITS + 1):
        in_q = jnp.logical_and(valid > (q - 1) * piece, valid <= q * piece)
        for cast in (True, False):
            @pl.when(jnp.logical_and(in_q, first == cast))
            def _(q=q, cast=cast):
                ffn(q * piece, cast)

    @pl.when(valid == 0)
    def _():
        y_ref[...] = jnp.zeros_like(y_ref)


def _experts(block_e, n_used, block_valid, block_slot, block_next, xs, wgu, bgu, wdn, bdn, tm):
    n_rows = xs.shape[0]
    return pl.pallas_call(
        _expert_body,
        grid_spec=pltpu.PrefetchScalarGridSpec(
            num_scalar_prefetch=5, grid=(n_rows // tm,),
            in_specs=[pl.BlockSpec((tm, D_MODEL // 2), lambda i, be, nu, *_: (jnp.minimum(i, nu[0] - 1), 0)),
                      pl.BlockSpec(memory_space=pl.ANY),
                      pl.BlockSpec((None, 1, 2 * D_FF), lambda i, be, *_: (be[i], 0, 0)),
                      pl.BlockSpec(memory_space=pl.ANY),
                      pl.BlockSpec((None, 1, D_MODEL), lambda i, be, *_: (be[i], 0, 0))],
            out_specs=pl.BlockSpec((tm, D_MODEL // 2), lambda i, *_: (i, 0)),
            scratch_shapes=[pltpu.VMEM((2, D_MODEL, 2 * D_FF), F32), pltpu.VMEM((2, D_FF, D_MODEL), F32),
                            pltpu.VMEM((D_MODEL, 2 * D_FF), BF16), pltpu.VMEM((D_FF, D_MODEL), BF16),
                            pltpu.SemaphoreType.DMA((2, 2))]),
        out_shape=jax.ShapeDtypeStruct((n_rows, D_MODEL // 2), U32),
        compiler_params=_cparams("arbitrary"),
        name="moe_experts",
    )(block_e, n_used, block_valid, block_slot, block_next, xs, wgu, bgu, wdn, bdn)


def _combine_body(tt, cnt_ref, off_ref, base_ref, loc_ref, g_ref, h_ref, hs_ref, gfin_ref, ys_hbm,
                  y_ref, ysmp_ref, buf_ref, sems):
    i = pl.program_id(0)
    last = pl.num_programs(0) - 1
    r = _sorted_rows(tt)

    def copies(tile, wait):
        b = tile % 2

        def make_copy(lo, go, size):
            return pltpu.make_async_copy(ys_hbm.at[pl.ds(go, size), :], buf_ref.at[b, pl.ds(lo, size), :],
                                         sems.at[b])

        _run_copies(tt, tile, cnt_ref, off_ref, base_ref, make_copy, wait)

    @pl.when(i == 0)
    def _():
        buf_ref[...] = jnp.zeros_like(buf_ref)
        copies(0, False)

    @pl.when(i < last)
    def _():
        copies(i + 1, False)

    copies(i, True)
    loc = loc_ref[...]
    gates = g_ref[...]
    slot_i = lax.broadcasted_iota(I32, (r, tt), 0)
    gmat = jnp.where(slot_i == loc[0:1, :], gates[0:1, :], 0.0)
    for k in range(1, TOP_K):
        gmat = gmat + jnp.where(slot_i == loc[k:k + 1, :], gates[k:k + 1, :], 0.0)
    h = jnp.where(i == last, _pad_rows(hs_ref[...], tt), h_ref[...])
    moe = lax.dot_general(gmat.astype(BF16), _unpack_bf16_pairs(buf_ref[i % 2]), TN_DIMS,
                          preferred_element_type=F32)
    y = _rms(h + moe, gfin_ref[...])

    @pl.when(i < last)
    def _():
        y_ref[...] = y

    @pl.when(i == last)
    def _():
        ysmp_ref[...] = y[0:ysmp_ref.shape[0], :]


def _combine(cnt, off, base, loc, gates, h2_p, h2_s, norm_final, ys, tt):
    ntp = h2_p.shape[0] // tt
    tok = pl.BlockSpec((tt, D_MODEL), lambda i: (jnp.minimum(i, ntp - 1), 0))
    smp = pl.BlockSpec(h2_s.shape, lambda i: (0, 0))
    tk = pl.BlockSpec((SUBLANES, tt), lambda i: (0, i))
    smem = pl.BlockSpec(memory_space=pltpu.SMEM)
    return pl.pallas_call(
        functools.partial(_combine_body, tt),
        grid_spec=pltpu.PrefetchScalarGridSpec(
            num_scalar_prefetch=0, grid=(ntp + 1,),
            in_specs=[smem, smem, smem, tk, tk, tok, smp,
                      pl.BlockSpec((1, D_MODEL), lambda i: (0, 0)),
                      pl.BlockSpec(memory_space=pl.ANY)],
            out_specs=[tok, smp],
            scratch_shapes=[pltpu.VMEM((2, _sorted_rows(tt), D_MODEL // 2), U32), pltpu.SemaphoreType.DMA((2,))]),
        out_shape=[jax.ShapeDtypeStruct(h2_p.shape, F32), jax.ShapeDtypeStruct(h2_s.shape, F32)],
        compiler_params=_cparams("arbitrary"),
        name="moe_combine",
    )(cnt, off, base, loc, gates, h2_p, h2_s, norm_final, ys)


def _moe_and_final_norm(hn_p, logits_p, h2_p, hn_s, logits_s, h2_s, w, tt, tm):
    nt = hn_p.shape[0] // tt + 1
    t = hn_p.shape[0] + hn_s.shape[0]
    gates, loc, cnt3, off3 = _route(logits_p, logits_s, tt)
    cnt = cnt3[:, :, 0]
    counts = jnp.sum(cnt, axis=0)
    padded = (counts + tm - 1) // tm * tm
    pad_end = jnp.cumsum(padded)
    start = pad_end - padded
    off = off3[:, :, 0]
    base = (start[None, :] + jnp.cumsum(cnt, axis=0) - cnt).astype(I32)
    n_blocks = (t * TOP_K + nt * N_EXPERTS * (SUBLANES - 1) + N_EXPERTS * (tm - 1)) // tm
    n_rows = n_blocks * tm
    block_start = jnp.arange(n_blocks, dtype=I32) * tm
    block_e = jnp.minimum(jnp.sum(block_start[:, None] >= pad_end[None, :], axis=-1), N_EXPERTS - 1).astype(I32)
    n_used = (pad_end[-1:] // tm).astype(I32)
    zero_starts = jnp.concatenate([start + counts, block_start]).astype(I32)
    zero_sizes = jnp.concatenate([padded - counts,
                                  jnp.where(block_start >= pad_end[-1], tm, 0)]).astype(I32)
    xs = _dispatch(cnt, off, base, zero_starts, zero_sizes, loc, hn_p, hn_s, n_rows, tm, tt)
    e_ids = jnp.arange(N_EXPERTS, dtype=I32)
    block_hot = block_e[:, None] == e_ids[None, :]

    def per_block(table):
        return jnp.sum(jnp.where(block_hot, table[None, :], 0), axis=1).astype(I32)

    block_valid = jnp.clip(per_block(start + counts) - block_start, 0, tm)
    block_valid = jnp.where(block_start < pad_end[-1], block_valid, 0).astype(I32)
    present = padded > 0
    later = present[None, :] & (e_ids[None, :] > e_ids[:, None])
    next_e = jnp.min(jnp.where(later, e_ids[None, :], N_EXPERTS), axis=1)
    next_e = jnp.where(next_e < N_EXPERTS, next_e, -1).astype(I32)
    before = present[None, :] & (e_ids[None, :] < e_ids[:, None])
    run_slot = (jnp.sum(before.astype(I32), axis=1) % 2).astype(I32)
    ys = _experts(block_e, n_used, block_valid, per_block(run_slot), per_block(next_e), xs, w["w_gate_up"], w["b_gate_up"], w["w_down"], w["b_down"], tm)
    return _combine(cnt, off, base, loc, gates, h2_p, h2_s, w["norm_final"], ys, tt)


def kernel(x_prompt, x_sample, mem_prompt, state_ssm, state_mamba_conv, state_short_conv, cache_mem_k, cache_mem_v, norm_mix, w_in, w_mconv, b_mconv, dt_bias, a_log, d_skip, norm_ssm, w_sconv, w_out, norm_xattn, norm_mem, w_xq, w_xk, w_xv, w_xo, norm_moe, w_router, b_router, w_gate_up, b_gate_up, w_down, b_down, norm_final):
    nbp, seq, _ = x_prompt.shape
    nbs = x_sample.shape[0]
    dt_lo = SSM_INNER + SSM_CONV_DIM
    w_in0 = w_in[0]
    w_dt = w_in0[:, dt_lo:dt_lo + SSM_HEADS]
    w = {
        "norm_mix": norm_mix, "norm_ssm": norm_ssm, "norm_xattn": norm_xattn, "norm_moe": norm_moe,
        "norm_final": norm_final.reshape(1, D_MODEL),
        "w_a": w_in0[:, :dt_lo].astype(BF16),
        "w_dt": w_dt.astype(BF16), "w_dt_t": w_dt.T.astype(BF16),
        "w_b": w_in0[:, dt_lo + SSM_HEADS:].astype(BF16),
        "w_mconv": w_mconv[0], "b_mconv": b_mconv,
        "dt_bias": dt_bias, "dt_bias_t": dt_bias.reshape(SSM_HEADS, 1),
        "a_log": a_log, "a_log_t": a_log.reshape(SSM_HEADS, 1),
        "d_skip": jnp.repeat(d_skip, SSM_HEAD_DIM, axis=1),
        "w_sconv": w_sconv[0], "w_out": w_out[0].astype(BF16),
        "w_xq": w_xq[0].astype(BF16), "w_xo": w_xo[0].astype(BF16),
        "w_router": w_router[0].T.astype(BF16), "b_router": b_router.reshape(N_EXPERTS, 1),
        "w_gate_up": w_gate_up[0], "b_gate_up": b_gate_up[0].reshape(N_EXPERTS, 1, 2 * D_FF),
        "w_down": w_down[0], "b_down": b_down[0].reshape(N_EXPERTS, 1, D_MODEL),
    }

    k_p, v_p, kb, vb = _mem_kv(mem_prompt.reshape(nbp * N_MEM, D_MODEL), norm_mem,
                               w_xk[0].astype(BF16), w_xv[0].astype(BF16))
    h1, ssm_p, mconv_p, sconv_p = _prompt_mixer(x_prompt.reshape(nbp * seq, D_MODEL), nbp, w)
    h2, hn, logits = _prompt_attn(h1, kb, vb, nbp, w)

    xs2 = x_sample.reshape(nbs, D_MODEL)
    mstate_t = jnp.transpose(state_mamba_conv[0], (1, 0, 2))
    sstate_t = jnp.transpose(state_short_conv[0], (1, 0, 2))
    z, xs_, dtx, dec, bm, cm, yb, sga, mnew_t, snew_t = _sample_proj(xs2, mstate_t, sstate_t, w)
    ssm_s, y_s = _sample_state(dec, state_ssm[0].reshape(nbs, SSM_INNER, SSM_STATE), dtx, bm, cm)
    h1s, q_s = _sample_fin1(xs2, y_s, xs_, z, yb, sga, w)
    o_s = _sample_attn(q_s.reshape(nbs, 1, D_MODEL),
                       cache_mem_k[0], cache_mem_v[0])
    h2s, hns, logits_s = _sample_fin2(h1s, o_s.reshape(nbs, D_MODEL), w)
    y_prompt, y_sample = _moe_and_final_norm(hn, logits, h2, hns, logits_s, h2s, w, MIX_TILE, MOE_ROW_TILE)

    return (y_prompt.reshape(nbp, seq, D_MODEL),
            y_sample.reshape(nbs, 1, D_MODEL),
            ssm_p.reshape(1, nbp, SSM_HEADS, SSM_HEAD_DIM, SSM_STATE),
            mconv_p[None], sconv_p[None],
            k_p[None], v_p[None],
            ssm_s.reshape(1, nbs, SSM_HEADS, SSM_HEAD_DIM, SSM_STATE),
            jnp.transpose(mnew_t, (1, 0, 2))[None],
            jnp.transpose(snew_t, (1, 0, 2))[None])
```

```python
import functools

import jax
import jax.numpy as jnp
from jax import lax
from jax.experimental import pallas as pl
from jax.experimental.pallas import tpu as pltpu

F32 = jnp.float32
BF16 = jnp.bfloat16
I32 = jnp.int32
U32 = jnp.uint32

D_MODEL = 1024
N_MEM = 256
SSM_HEADS = 16
SSM_HEAD_DIM = 64
SSM_INNER = SSM_HEADS * SSM_HEAD_DIM
SSM_STATE = 128
SSM_GROUPS = 4
HEADS_PER_GROUP = SSM_HEADS // SSM_GROUPS
GROUP_WIDTH = SSM_INNER // SSM_GROUPS
SSM_CONV = 4
SSM_CONV_DIM = SSM_INNER + 2 * SSM_GROUPS * SSM_STATE
SC_CONV = 3
XA_HEADS = 4
XA_HEAD_DIM = D_MODEL // XA_HEADS
N_EXPERTS = 32
TOP_K = 4
D_FF = D_MODEL
SWIGLU_LIMIT = 7.0
SWIGLU_ALPHA = 1.702
EPS = 1e-6

LANES = 128
SUBLANES = 8
V7X_VMEM_BYTES = 64 * 1024 * 1024
VMEM_LIMIT = V7X_VMEM_BYTES * 7 // 8
HIGH_HALF = 0xFFFF0000

MIX_TILE = 256
ATTN_TILE = 1024
MOE_ROW_TILE = 512
RUN_COPY_UNROLL = 4
EXPERT_ROW_SPLITS = 4
EXPERT_COL_CHUNK = 512
STATE_BB = 8
ATTN_BB = 4

NT_DIMS = (((1,), (1,)), ((), ()))
TN_DIMS = (((0,), (0,)), ((), ()))


def _cparams(*sem):
    return pltpu.CompilerParams(dimension_semantics=sem, vmem_limit_bytes=VMEM_LIMIT)


def _const_spec(shape):
    nd = len(shape)
    return pl.BlockSpec(shape, lambda *_: (0,) * nd, pipeline_mode=pl.Buffered(1))


def _sigmoid(x):
    return 0.5 * jnp.tanh(0.5 * x) + 0.5


def _silu(x):
    return x * _sigmoid(x)


def _softplus(x):
    return jnp.maximum(x, 0.0) + jnp.log(1.0 + jnp.exp(-jnp.abs(x)))


def _rms(x, g):
    ms = jnp.mean(x * x, axis=-1, keepdims=True)
    return x * lax.rsqrt(ms + EPS) * g


def _dot(a, b):
    return jnp.dot(a, b, preferred_element_type=F32)


def _dot_nt(a, b):
    return lax.dot_general(a, b, NT_DIMS, preferred_element_type=F32)


def _expand_heads(v):
    assert LANES == 2 * SSM_HEAD_DIM
    rows = v.shape[0]
    lane = lax.broadcasted_iota(I32, (rows, LANES), 1)
    pieces = []
    for j in range(SSM_HEADS // 2):
        a = jnp.broadcast_to(v[:, 2 * j:2 * j + 1], (rows, LANES))
        b = jnp.broadcast_to(v[:, 2 * j + 1:2 * j + 2], (rows, LANES))
        pieces.append(jnp.where(lane < SSM_HEAD_DIM, a, b))
    return jnp.concatenate(pieces, axis=1)


def _cumsum(x, axis):
    idx = lax.broadcasted_iota(I32, x.shape, axis)
    shift = 1
    while shift < x.shape[axis]:
        x = x + jnp.where(idx >= shift, pltpu.roll(x, shift, axis), 0.0)
        shift *= 2
    return x


def _pack_bf16_pairs(x, is_bf16_valued=False):
    w = x.shape[1] // 2
    if not is_bf16_valued:
        x = x.astype(BF16).astype(F32)
    bits = lax.bitcast_convert_type(x, U32)
    return (bits[:, w:] & jnp.uint32(HIGH_HALF)) | (bits[:, :w] >> 16)


def _unpack_bf16_pairs(p):
    lo = lax.bitcast_convert_type(p << 16, F32)
    hi = lax.bitcast_convert_type(p & jnp.uint32(HIGH_HALF), F32)
    return jnp.concatenate([lo, hi], axis=1).astype(BF16)


def _causal_dwconv(u, halo_ref, w):
    taps = w.shape[0]
    halo = halo_ref[...]
    sub = lax.broadcasted_iota(I32, halo.shape, 0)
    out = u * w[taps - 1:taps, :]
    for s in range(1, taps):
        rolled = pltpu.roll(u, s, axis=0)
        head = jnp.where(sub < s, pltpu.roll(halo, s, axis=0), rolled[0:SUBLANES, :])
        shifted = jnp.concatenate([head, rolled[SUBLANES:, :]], axis=0)
        out = out + shifted * w[taps - 1 - s:taps - s, :]
    halo_ref[...] = u[u.shape[0] - SUBLANES:, :]
    return out


def _pad_rows(x, rows):
    return jnp.concatenate([x, jnp.zeros((rows - x.shape[0], x.shape[1]), x.dtype)], axis=0)


def _group_rmsnorm(u, g):
    outs = []
    for k in range(SSM_GROUPS):
        ug = u[:, k * GROUP_WIDTH:(k + 1) * GROUP_WIDTH]
        ms = jnp.mean(ug * ug, axis=-1, keepdims=True)
        outs.append(ug * lax.rsqrt(ms + EPS))
    return jnp.concatenate(outs, axis=1) * g


def _memkv_body(mem_ref, g_ref, wk_ref, wv_ref, k_ref, v_ref, kb_ref, vb_ref):
    mn = _rms(mem_ref[...], g_ref[...]).astype(BF16)
    k = _dot(mn, wk_ref[...])
    v = _dot(mn, wv_ref[...])
    for hd in range(XA_HEADS):
        sl = slice(hd * XA_HEAD_DIM, (hd + 1) * XA_HEAD_DIM)
        k_ref[:, hd, :] = k[:, sl]
        v_ref[:, hd, :] = v[:, sl]
    kb_ref[...] = k.astype(BF16)
    vb_ref[...] = v.astype(BF16)


def _mem_kv(mem2d, norm_mem, wk, wv):
    rows = mem2d.shape[0]
    nb = rows // N_MEM
    blk = pl.BlockSpec((N_MEM, D_MODEL), lambda b: (b, 0))
    head_blk = pl.BlockSpec((None, N_MEM, XA_HEADS, XA_HEAD_DIM), lambda b: (b, 0, 0, 0))
    return pl.pallas_call(
        _memkv_body,
        grid=(nb,),
        in_specs=[blk, _const_spec((1, D_MODEL)), _const_spec((D_MODEL, D_MODEL)),
                  _const_spec((D_MODEL, D_MODEL))],
        out_specs=[head_blk, head_blk, blk, blk],
        out_shape=[jax.ShapeDtypeStruct((nb, N_MEM, XA_HEADS, XA_HEAD_DIM), F32)] * 2
        + [jax.ShapeDtypeStruct((rows, D_MODEL), BF16)] * 2,
        compiler_params=_cparams("arbitrary"),
        name="mem_kv",
    )(mem2d, norm_mem, wk, wv)


def _mix_body(x_ref, gmix_ref, wa_ref, wdtc_ref, wdtr_ref, wb_ref, wmc_ref, bmc_ref,
              dtb_ref, dtbt_ref, alog_ref, alogt_ref, dskip_ref, gssm_ref, wsc_ref, wout_ref,
              h_ref, ssm_ref, mbuf_ref, sbuf_ref,
              st_ref, cbuf_ref, scbuf_ref):
    tq = MIX_TILE
    c = pl.program_id(1)

    @pl.when(c == 0)
    def _():
        st_ref[...] = jnp.zeros_like(st_ref)
        cbuf_ref[...] = jnp.zeros_like(cbuf_ref)
        scbuf_ref[...] = jnp.zeros_like(scbuf_ref)

    x = x_ref[...]
    xn = _rms(x, gmix_ref[...]).astype(BF16)

    u = _dot(xn, wa_ref[:, SSM_INNER:])
    conv = _causal_dwconv(u, cbuf_ref, wmc_ref[...]) + bmc_ref[...]
    mbuf_ref[...] = u[tq - (SSM_CONV - 1):tq, :]
    xbc = _silu(conv)
    xs = xbc[:, :SSM_INNER]
    bm = xbc[:, SSM_INNER:SSM_INNER + SSM_GROUPS * SSM_STATE]
    cm = xbc[:, SSM_INNER + SSM_GROUPS * SSM_STATE:]

    dt = _softplus(_dot(xn, wdtc_ref[...]) + dtb_ref[...])
    dtt = _softplus(_dot_nt(wdtr_ref[...], xn) + dtbt_ref[...])
    a_row = -jnp.exp(alog_ref[...])
    a_col = -jnp.exp(alogt_ref[...])
    row_i = lax.broadcasted_iota(I32, (tq, tq), 0)
    col_i = lax.broadcasted_iota(I32, (tq, tq), 1)
    causal = row_i >= col_i
    a_cum = _cumsum(dt * a_row, 0)
    a_cumt = _cumsum(dtt * a_col, 1)
    a_last = a_cum[tq - 1:tq, :]

    xdt = xs * _expand_heads(dt)
    in_decay = _expand_heads(jnp.exp(a_cum))
    to_end = _expand_heads(jnp.exp(a_last - a_cum))
    chunk_decay = _expand_heads(jnp.exp(a_last))
    xdt_b = xdt.astype(BF16)
    xend_b = (xdt * to_end).astype(BF16)
    lane = lax.broadcasted_iota(I32, (tq, LANES), 1)

    def proj_b(k):
        return _dot(xn, wb_ref[:, k * D_MODEL:(k + 1) * D_MODEL])

    pb = []
    y_groups = []
    for g in range(SSM_GROUPS):
        pb.append(proj_b(g))
        if g == 0:
            z = _dot(xn, wa_ref[:, :SSM_INNER])
        if g == 2:
            g_b = proj_b(SSM_GROUPS)
        cg = cm[:, g * SSM_STATE:(g + 1) * SSM_STATE].astype(BF16)
        bg_f = bm[:, g * SSM_STATE:(g + 1) * SSM_STATE]
        bg = bg_f.astype(BF16)
        scores = _dot_nt(cg, bg)
        gs = slice(g * GROUP_WIDTH, (g + 1) * GROUP_WIDTH)
        st_g = st_ref[:, gs]
        y_off = _dot(cg, st_g.astype(BF16)) * in_decay[:, gs]
        pair_out = []
        for pr in range(HEADS_PER_GROUP // 2):
            h0 = g * HEADS_PER_GROUP + 2 * pr
            xp = xdt_b[:, h0 * SSM_HEAD_DIM:(h0 + 2) * SSM_HEAD_DIM]
            ys = []
            for h in (h0, h0 + 1):
                seg = a_cum[:, h:h + 1] - a_cumt[h:h + 1, :]
                decay = jnp.where(causal, jnp.exp(jnp.minimum(seg, 0.0)), 0.0)
                ys.append(_dot((scores * decay).astype(BF16), xp))
            pair_out.append(jnp.where(lane < SSM_HEAD_DIM, ys[0], ys[1]))
        y_groups.append(jnp.concatenate(pair_out, axis=1) + y_off)
        st_ref[:, gs] = st_g * chunk_decay[:, gs] + _dot(bg_f.T.astype(BF16), xend_b[:, gs])
    y = jnp.concatenate(y_groups, axis=1) + dskip_ref[...] * xs
    y_a = _group_rmsnorm(y * _silu(z), gssm_ref[...])

    sc_b, sc_c, sc_v, g_a = pb
    cv = sc_c * sc_v
    uc = _causal_dwconv(cv, scbuf_ref, wsc_ref[...])
    sbuf_ref[...] = cv[tq - (SC_CONV - 1):tq, :]
    merged = _sigmoid(g_a) * y_a + _sigmoid(g_b) * (sc_b * uc)
    h_ref[...] = x + _dot(merged.astype(BF16), wout_ref[...])

    @pl.when(c == pl.num_programs(1) - 1)
    def _():
        ssm_ref[...] = st_ref[...].T


def _prompt_mixer(x2d, nb, w):
    t = x2d.shape[0]
    assert (t // nb) % MIX_TILE == 0, "prompt length must be a multiple of MIX_TILE"
    nc = t // nb // MIX_TILE
    tok = pl.BlockSpec((MIX_TILE, D_MODEL), lambda b, c: (b * nc + c, 0))
    return pl.pallas_call(
        _mix_body,
        grid=(nb, nc),
        in_specs=[tok, _const_spec((1, D_MODEL)),
                  _const_spec((D_MODEL, SSM_INNER + SSM_CONV_DIM)),
                  _const_spec((D_MODEL, SSM_HEADS)), _const_spec((SSM_HEADS, D_MODEL)),
                  _const_spec((D_MODEL, 5 * D_MODEL)),
                  _const_spec((SSM_CONV, SSM_CONV_DIM)), _const_spec((1, SSM_CONV_DIM)),
                  _const_spec((1, SSM_HEADS)), _const_spec((SSM_HEADS, 1)),
                  _const_spec((1, SSM_HEADS)), _const_spec((SSM_HEADS, 1)),
                  _const_spec((1, SSM_INNER)), _const_spec((1, SSM_INNER)),
                  _const_spec((SC_CONV, D_MODEL)), _const_spec((D_MODEL, D_MODEL))],
        out_specs=[tok,
                   pl.BlockSpec((None, SSM_INNER, SSM_STATE), lambda b, c: (b, 0, 0)),
                   pl.BlockSpec((None, SSM_CONV - 1, SSM_CONV_DIM), lambda b, c: (b, 0, 0)),
                   pl.BlockSpec((None, SC_CONV - 1, D_MODEL), lambda b, c: (b, 0, 0))],
        out_shape=[jax.ShapeDtypeStruct((t, D_MODEL), F32),
                   jax.ShapeDtypeStruct((nb, SSM_INNER, SSM_STATE), F32),
                   jax.ShapeDtypeStruct((nb, SSM_CONV - 1, SSM_CONV_DIM), F32),
                   jax.ShapeDtypeStruct((nb, SC_CONV - 1, D_MODEL), F32)],
        scratch_shapes=[pltpu.VMEM((SSM_STATE, SSM_INNER), F32),
                        pltpu.VMEM((SUBLANES, SSM_CONV_DIM), F32),
                        pltpu.VMEM((SUBLANES, D_MODEL), F32)],
        compiler_params=_cparams("arbitrary", "arbitrary"),
        name="prompt_mixer",
    )(x2d, w["norm_mix"], w["w_a"], w["w_dt"], w["w_dt_t"], w["w_b"], w["w_mconv"], w["b_mconv"],
      w["dt_bias"], w["dt_bias_t"], w["a_log"], w["a_log_t"], w["d_skip"], w["norm_ssm"],
      w["w_sconv"], w["w_out"])


def _router_tail(h2, gmoe_ref, wr_ref, br_ref, h2_ref, hn_ref, lg_ref):
    h2_ref[...] = h2
    hn = _rms(h2, gmoe_ref[...]).astype(BF16)
    hn_ref[...] = hn
    lg_ref[...] = _dot_nt(wr_ref[...], hn) + br_ref[...]


def _attn_body(h_ref, gx_ref, wq_ref, k_ref, v_ref, wo_ref, gmoe_ref, wr_ref, br_ref,
               h2_ref, hn_ref, lg_ref):
    h = h_ref[...]
    hn = _rms(h, gx_ref[...]).astype(BF16)
    q = _dot(hn, wq_ref[...]).astype(BF16)
    outs = []
    for hd in range(XA_HEADS):
        sl = slice(hd * XA_HEAD_DIM, (hd + 1) * XA_HEAD_DIM)
        s = _dot_nt(q[:, sl], k_ref[:, sl]) * (XA_HEAD_DIM ** -0.5)
        e = jnp.exp(s - jnp.max(s, axis=-1, keepdims=True))
        p = e / jnp.sum(e, axis=-1, keepdims=True)
        outs.append(_dot(p.astype(BF16), v_ref[:, sl]))
    o = jnp.concatenate(outs, axis=1).astype(BF16)
    h2 = h + _dot(o, wo_ref[...])
    _router_tail(h2, gmoe_ref, wr_ref, br_ref, h2_ref, hn_ref, lg_ref)


def _prompt_attn(h1, kb, vb, nb, w):
    t = h1.shape[0]
    assert (t // nb) % ATTN_TILE == 0, "prompt length must be a multiple of ATTN_TILE"
    nc = t // nb // ATTN_TILE
    tok = pl.BlockSpec((ATTN_TILE, D_MODEL), lambda b, c: (b * nc + c, 0))
    kv = pl.BlockSpec((N_MEM, D_MODEL), lambda b, c: (b, 0))
    return pl.pallas_call(
        _attn_body,
        grid=(nb, nc),
        in_specs=[tok, _const_spec((1, D_MODEL)), _const_spec((D_MODEL, D_MODEL)), kv, kv,
                  _const_spec((D_MODEL, D_MODEL)), _const_spec((1, D_MODEL)),
                  _const_spec((N_EXPERTS, D_MODEL)), _const_spec((N_EXPERTS, 1))],
        out_specs=[tok, tok, pl.BlockSpec((N_EXPERTS, ATTN_TILE), lambda b, c: (0, b * nc + c))],
        out_shape=[jax.ShapeDtypeStruct((t, D_MODEL), F32),
                   jax.ShapeDtypeStruct((t, D_MODEL), BF16),
                   jax.ShapeDtypeStruct((N_EXPERTS, t), F32)],
        compiler_params=_cparams("arbitrary", "arbitrary"),
        name="prompt_attn",
    )(h1, w["norm_xattn"], w["w_xq"], kb, vb, w["w_xo"], w["norm_moe"], w["w_router"], w["b_router"])


def _sproj_body(x_ref, gmix_ref, wa_ref, wdtc_ref, wb_ref, wmc_ref, bmc_ref, dtb_ref, alog_ref,
                wsc_ref, mst_ref, sst_ref,
                z_ref, xs_ref, dtx_ref, dec_ref, bm_ref, cm_ref, yb_ref, sga_ref, mnew_ref, snew_ref):
    x = x_ref[...]
    xn = _rms(x, gmix_ref[...]).astype(BF16)
    pa = _dot(xn, wa_ref[...])
    z_ref[...] = pa[:, :SSM_INNER]
    u = pa[:, SSM_INNER:]
    wm = wmc_ref[...]
    conv = u * wm[SSM_CONV - 1:SSM_CONV, :] + bmc_ref[...]
    for k in range(SSM_CONV - 1):
        conv = conv + mst_ref[k] * wm[k:k + 1, :]
    for k in range(SSM_CONV - 2):
        mnew_ref[k] = mst_ref[k + 1]
    mnew_ref[SSM_CONV - 2] = u
    xbc = _silu(conv)
    xs = xbc[:, :SSM_INNER]
    xs_ref[...] = xs
    bm_ref[...] = xbc[:, SSM_INNER:SSM_INNER + SSM_GROUPS * SSM_STATE]
    cm_ref[...] = xbc[:, SSM_INNER + SSM_GROUPS * SSM_STATE:]
    dt = _softplus(_dot(xn, wdtc_ref[...]) + dtb_ref[...])
    dec_ref[...] = jnp.exp(dt * (-jnp.exp(alog_ref[...])))
    dtx_ref[...] = xs * _expand_heads(dt)
    pb = _dot(xn, wb_ref[...])
    cv = pb[:, D_MODEL:2 * D_MODEL] * pb[:, 2 * D_MODEL:3 * D_MODEL]
    ws = wsc_ref[...]
    uc = cv * ws[SC_CONV - 1:SC_CONV, :]
    for k in range(SC_CONV - 1):
        uc = uc + sst_ref[k] * ws[k:k + 1, :]
    for k in range(SC_CONV - 2):
        snew_ref[k] = sst_ref[k + 1]
    snew_ref[SC_CONV - 2] = cv
    yb_ref[...] = _sigmoid(pb[:, 4 * D_MODEL:5 * D_MODEL]) * (pb[:, 0:D_MODEL] * uc)
    sga_ref[...] = _sigmoid(pb[:, 3 * D_MODEL:4 * D_MODEL])


def _sample_proj(x, mstate_t, sstate_t, w):
    nb = x.shape[0]
    f = lambda *s: jax.ShapeDtypeStruct(s, F32)
    return pl.pallas_call(
        _sproj_body,
        out_shape=[f(nb, SSM_INNER), f(nb, SSM_INNER), f(nb, SSM_INNER), f(nb, SSM_HEADS),
                   f(nb, SSM_GROUPS * SSM_STATE), f(nb, SSM_GROUPS * SSM_STATE),
                   f(nb, D_MODEL), f(nb, D_MODEL),
                   f(SSM_CONV - 1, nb, SSM_CONV_DIM), f(SC_CONV - 1, nb, D_MODEL)],
        compiler_params=pltpu.CompilerParams(vmem_limit_bytes=VMEM_LIMIT),
        name="sample_proj",
    )(x, w["norm_mix"], w["w_a"], w["w_dt"], w["w_b"], w["w_mconv"], w["b_mconv"], w["dt_bias"],
      w["a_log"], w["w_sconv"], mstate_t, sstate_t)


def _sstate_body(dec_ref, s_ref, dtx_ref, bm_ref, cm_ref, snew_ref, y_ref):
    i = pl.program_id(0)
    rows_per_blk = LANES
    for j in range(STATE_BB):
        b = i * STATE_BB + j
        dtx_row = dtx_ref[j:j + 1, :]
        y_parts = []
        for g in range(SSM_GROUPS):
            b_row = bm_ref[j:j + 1, g * SSM_STATE:(g + 1) * SSM_STATE]
            c_row = cm_ref[j:j + 1, g * SSM_STATE:(g + 1) * SSM_STATE].astype(BF16)
            new_blocks = []
            for q in range(GROUP_WIDTH // rows_per_blk):
                r0 = g * GROUP_WIDTH + q * rows_per_blk
                dcol = jnp.broadcast_to(dtx_row[:, r0:r0 + rows_per_blk], (rows_per_blk, LANES)).T
                sub = []
                for hh in range(rows_per_blk // SSM_HEAD_DIM):
                    h = r0 // SSM_HEAD_DIM + hh
                    lo = hh * SSM_HEAD_DIM
                    s_old = s_ref[j, r0 + lo:r0 + lo + SSM_HEAD_DIM, :]
                    sub.append(s_old * dec_ref[b, h] + dcol[lo:lo + SSM_HEAD_DIM, :] * b_row)
                blk = jnp.concatenate(sub, axis=0)
                snew_ref[j, r0:r0 + rows_per_blk, :] = blk
                new_blocks.append(blk.astype(BF16))
            s_g = jnp.concatenate(new_blocks, axis=0)
            y_parts.append(_dot_nt(c_row, s_g))
        y_ref[j:j + 1, :] = jnp.concatenate(y_parts, axis=1)


def _sample_state(dec, state, dtx, bm, cm):
    nb = state.shape[0]
    row = lambda wdt: pl.BlockSpec((STATE_BB, wdt), lambda i, dec: (i, 0))
    st = pl.BlockSpec((STATE_BB, SSM_INNER, SSM_STATE), lambda i, dec: (i, 0, 0))
    return pl.pallas_call(
        _sstate_body,
        grid_spec=pltpu.PrefetchScalarGridSpec(
            num_scalar_prefetch=1, grid=(nb // STATE_BB,),
            in_specs=[st, row(SSM_INNER), row(SSM_GROUPS * SSM_STATE), row(SSM_GROUPS * SSM_STATE)],
            out_specs=[st, row(SSM_INNER)]),
        out_shape=[jax.ShapeDtypeStruct(state.shape, F32), jax.ShapeDtypeStruct((nb, SSM_INNER), F32)],
        compiler_params=_cparams("arbitrary"),
        name="sample_state",
    )(dec, state, dtx, bm, cm)


def _sfin1_body(x_ref, y_ref, xs_ref, z_ref, yb_ref, sga_ref, dskip_ref, gssm_ref, wout_ref,
                gx_ref, wq_ref, h_ref, q_ref):
    y = y_ref[...] + dskip_ref[...] * xs_ref[...]
    y_a = _group_rmsnorm(y * _silu(z_ref[...]), gssm_ref[...])
    merged = sga_ref[...] * y_a + yb_ref[...]
    h = x_ref[...] + _dot(merged.astype(BF16), wout_ref[...])
    h_ref[...] = h
    q_ref[...] = _dot(_rms(h, gx_ref[...]).astype(BF16), wq_ref[...])


def _sample_fin1(x, y, xs, z, yb, sga, w):
    nb = x.shape[0]
    return pl.pallas_call(
        _sfin1_body,
        out_shape=[jax.ShapeDtypeStruct((nb, D_MODEL), F32)] * 2,
        compiler_params=pltpu.CompilerParams(vmem_limit_bytes=VMEM_LIMIT),
        name="sample_fin1",
    )(x, y, xs, z, yb, sga, w["d_skip"], w["norm_ssm"], w["w_out"], w["norm_xattn"], w["w_xq"])


def _sattn_body(q_ref, k_ref, v_ref, o_ref):
    for j in range(ATTN_BB):
        q_row = q_ref[j]
        q4 = jnp.concatenate([q_row[:, h * XA_HEAD_DIM:(h + 1) * XA_HEAD_DIM]
                              for h in range(XA_HEADS)], axis=0)
        s = jnp.sum(k_ref[j] * q4[None], axis=-1, keepdims=True) * (XA_HEAD_DIM ** -0.5)
        e = jnp.exp(s - jnp.max(s, axis=0, keepdims=True))
        p = e / jnp.sum(e, axis=0, keepdims=True)
        o4 = jnp.sum(p * v_ref[j], axis=0)
        o_ref[j] = jnp.concatenate([o4[h:h + 1, :] for h in range(XA_HEADS)], axis=1)


def _sample_attn(q3, k3, v3):
    nb = q3.shape[0]
    qs = pl.BlockSpec((ATTN_BB, 1, D_MODEL), lambda i: (i, 0, 0))
    kv = pl.BlockSpec((ATTN_BB, N_MEM, XA_HEADS, XA_HEAD_DIM), lambda i: (i, 0, 0, 0))
    return pl.pallas_call(
        _sattn_body,
        grid=(nb // ATTN_BB,),
        in_specs=[qs, kv, kv],
        out_specs=qs,
        out_shape=jax.ShapeDtypeStruct((nb, 1, D_MODEL), F32),
        compiler_params=_cparams("arbitrary"),
        name="sample_attn",
    )(q3, k3, v3)


def _sfin2_body(h_ref, o_ref, wo_ref, gmoe_ref, wr_ref, br_ref, h2_ref, hn_ref, lg_ref):
    h2 = h_ref[...] + _dot(o_ref[...].astype(BF16), wo_ref[...])
    _router_tail(h2, gmoe_ref, wr_ref, br_ref, h2_ref, hn_ref, lg_ref)


def _sample_fin2(h1, o, w):
    nb = h1.shape[0]
    return pl.pallas_call(
        _sfin2_body,
        out_shape=[jax.ShapeDtypeStruct((nb, D_MODEL), F32), jax.ShapeDtypeStruct((nb, D_MODEL), BF16),
                   jax.ShapeDtypeStruct((N_EXPERTS, nb), F32)],
        compiler_params=pltpu.CompilerParams(vmem_limit_bytes=VMEM_LIMIT),
        name="sample_fin2",
    )(h1, o, w["w_xo"], w["norm_moe"], w["w_router"], w["b_router"])


def _pad_cols(x, cols):
    return jnp.concatenate([x, jnp.zeros((x.shape[0], cols - x.shape[1]), x.dtype)], axis=1)


def _route_body(lgp_ref, lgs_ref, g_ref, loc_ref, cnt_ref, off_ref):
    tt = lgp_ref.shape[1]
    is_sample = pl.program_id(0) == pl.num_programs(0) - 1
    col = lax.broadcasted_iota(I32, (1, tt), 1)
    valid = jnp.logical_or(jnp.logical_not(is_sample), col < lgs_ref.shape[1])
    work = jnp.where(is_sample, _pad_cols(lgs_ref[...], tt), lgp_ref[...])
    sub = lax.broadcasted_iota(I32, (N_EXPERTS, tt), 0).astype(F32)
    vals, hots = [], []
    for _ in range(TOP_K):
        m = jnp.max(work, axis=0, keepdims=True)
        idx = jnp.min(jnp.where(work == m, sub, float(N_EXPERTS)), axis=0, keepdims=True)
        hot = (sub == idx) & valid
        vals.append(m)
        hots.append(hot)
        work = jnp.where(hot, -jnp.inf, work)
    exps = [jnp.exp(v - vals[0]) for v in vals]
    tot = exps[0]
    for e in exps[1:]:
        tot = tot + e
    assigned = hots[0]
    for hot in hots[1:]:
        assigned = assigned | hot
    a = assigned.astype(BF16)
    r_i = lax.broadcasted_iota(I32, (tt, tt), 0)
    c_i = lax.broadcasted_iota(I32, (tt, tt), 1)
    rank = _dot(a, (r_i < c_i).astype(BF16))
    cnt = jnp.sum(a.astype(F32), axis=1, keepdims=True)
    cnt = jnp.floor((cnt + (SUBLANES - 1)) * (1.0 / SUBLANES)) * SUBLANES
    e_r = lax.broadcasted_iota(I32, (N_EXPERTS, N_EXPERTS), 0)
    e_c = lax.broadcasted_iota(I32, (N_EXPERTS, N_EXPERTS), 1)
    cnt_cols = jnp.broadcast_to(cnt, (N_EXPERTS, LANES)).astype(BF16)
    off = _dot((e_r > e_c).astype(BF16), cnt_cols)[:, 0:1]
    slot = rank + off
    k_sub = lax.broadcasted_iota(I32, (SUBLANES, tt), 0)
    g_out = jnp.zeros((SUBLANES, tt), F32)
    l_out = jnp.full((SUBLANES, tt), -1.0, F32)
    for k in range(TOP_K):
        lk = jnp.sum(jnp.where(hots[k], slot, 0.0), axis=0, keepdims=True)
        g_out = jnp.where(k_sub == k, jnp.where(valid, exps[k] / tot, 0.0), g_out)
        l_out = jnp.where(k_sub == k, jnp.where(valid, lk, -1.0), l_out)
    g_ref[...] = g_out
    loc_ref[...] = l_out.astype(I32)
    cnt_ref[...] = cnt.astype(I32)
    off_ref[...] = off.astype(I32)


def _route(logits_p, logits_s, tt):
    ntp = logits_p.shape[1] // tt
    nt = ntp + 1
    t = nt * tt
    tk = pl.BlockSpec((SUBLANES, tt), lambda i: (0, i))
    per_tile = pl.BlockSpec((None, N_EXPERTS, 1), lambda i: (i, 0, 0))
    return pl.pallas_call(
        _route_body,
        grid=(nt,),
        in_specs=[pl.BlockSpec((N_EXPERTS, tt), lambda i: (0, jnp.minimum(i, ntp - 1))),
                  pl.BlockSpec(logits_s.shape, lambda i: (0, 0))],
        out_specs=[tk, tk, per_tile, per_tile],
        out_shape=[jax.ShapeDtypeStruct((SUBLANES, t), F32), jax.ShapeDtypeStruct((SUBLANES, t), I32),
                   jax.ShapeDtypeStruct((nt, N_EXPERTS, 1), I32), jax.ShapeDtypeStruct((nt, N_EXPERTS, 1), I32)],
        compiler_params=_cparams("arbitrary"),
        name="moe_route",
    )(logits_p, logits_s)


def _sorted_rows(tt):
    return tt * TOP_K + N_EXPERTS * SUBLANES


def _run_copies(tt, tile, cnt_ref, off_ref, base_ref, make_copy, wait):
    if wait:
        total = off_ref[tile, N_EXPERTS - 1] + cnt_ref[tile, N_EXPERTS - 1]

        @pl.when(total > 0)
        def _():
            make_copy(0, 0, pl.multiple_of(total, SUBLANES)).wait()
        return

    def per_expert(e):
        n = cnt_ref[tile, e]

        @pl.when(n > 0)
        def _():
            make_copy(pl.multiple_of(off_ref[tile, e], SUBLANES),
                      pl.multiple_of(base_ref[tile, e], SUBLANES), pl.multiple_of(n, SUBLANES)).start()

    def four_experts(j, carry):
        for u in range(RUN_COPY_UNROLL):
            per_expert(j * RUN_COPY_UNROLL + u)
        return carry

    lax.fori_loop(0, N_EXPERTS // RUN_COPY_UNROLL, four_experts, 0)


def _dispatch_body(tm, tt, cnt_ref, off_ref, base_ref, zstart_ref, zsize_ref, loc_ref, x_ref, xs_ref, o_hbm,
                   zero_ref, srt_ref, zsem, sems):
    i = pl.program_id(0)
    last = pl.num_programs(0) - 1
    r = _sorted_rows(tt)

    def zero_copy(j):
        n = pl.multiple_of(zsize_ref[j], SUBLANES)
        dst = o_hbm.at[pl.ds(pl.multiple_of(zstart_ref[j], SUBLANES), n), :]
        return pltpu.make_async_copy(zero_ref.at[pl.ds(0, n), :], dst, zsem)

    @pl.when(i == 0)
    def _():
        zero_ref[...] = jnp.zeros_like(zero_ref)

        def start(j, carry):
            @pl.when(zsize_ref[j] > 0)
            def _():
                zero_copy(j).start()
            return carry

        def wait(j, carry):
            @pl.when(zsize_ref[j] > 0)
            def _():
                zero_copy(j).wait()
            return carry

        lax.fori_loop(0, zstart_ref.shape[0], start, 0)
        lax.fori_loop(0, zstart_ref.shape[0], wait, 0)

    loc = loc_ref[...]
    slot_i = lax.broadcasted_iota(I32, (r, tt), 0)
    hit = slot_i == loc[0:1, :]
    for k in range(1, TOP_K):
        hit = hit | (slot_i == loc[k:k + 1, :])
    buf = i % 2
    x = jnp.where(i == last, _pad_rows(xs_ref[...], tt), x_ref[...])
    srt_ref[buf] = _pack_bf16_pairs(_dot(hit.astype(BF16), x.astype(BF16)), is_bf16_valued=True)

    def copies(tile, wait):
        b = tile % 2

        def make_copy(lo, go, size):
            return pltpu.make_async_copy(srt_ref.at[b, pl.ds(lo, size), :], o_hbm.at[pl.ds(go, size), :],
                                         sems.at[b])

        _run_copies(tt, tile, cnt_ref, off_ref, base_ref, make_copy, wait)

    copies(i, False)

    @pl.when(i > 0)
    def _():
        copies(i - 1, True)

    @pl.when(i == last)
    def _():
        copies(i, True)


def _dispatch(cnt, off, base, zero_starts, zero_sizes, loc, hn_p, hn_s, n_rows, tm, tt):
    ntp = hn_p.shape[0] // tt
    smem = pl.BlockSpec(memory_space=pltpu.SMEM)
    return pl.pallas_call(
        functools.partial(_dispatch_body, tm, tt),
        grid_spec=pltpu.PrefetchScalarGridSpec(
            num_scalar_prefetch=0, grid=(ntp + 1,),
            in_specs=[smem, smem, smem, smem, smem,
                      pl.BlockSpec((SUBLANES, tt), lambda i: (0, i)),
                      pl.BlockSpec((tt, D_MODEL), lambda i: (jnp.minimum(i, ntp - 1), 0)),
                      pl.BlockSpec(hn_s.shape, lambda i: (0, 0))],
            out_specs=pl.BlockSpec(memory_space=pl.ANY),
            scratch_shapes=[pltpu.VMEM((tm, D_MODEL // 2), U32),
                            pltpu.VMEM((2, _sorted_rows(tt), D_MODEL // 2), U32),
                            pltpu.SemaphoreType.DMA, pltpu.SemaphoreType.DMA((2,))]),
        out_shape=jax.ShapeDtypeStruct((n_rows, D_MODEL // 2), U32),
        compiler_params=_cparams("arbitrary"),
        name="moe_dispatch",
    )(cnt, off, base, zero_starts, zero_sizes, loc, hn_p, hn_s)


def _expert_body(be_ref, nu_ref, bv_ref, slot_ref, nxt_ref, x_ref, wgu_hbm, bgu_ref, wdn_hbm, bdn_ref,
                 y_ref, wgu_f, wdn_f, wgu_b, wdn_b, sems):
    i = pl.program_id(0)
    tm = x_ref.shape[0]
    valid = bv_ref[i]
    expert = be_ref[i]
    slot = slot_ref[i]

    def weight_copies(e, s):
        return (pltpu.make_async_copy(wgu_hbm.at[e], wgu_f.at[s], sems.at[0, s]),
                pltpu.make_async_copy(wdn_hbm.at[e], wdn_f.at[s], sems.at[1, s]))

    @pl.when(i == 0)
    def _():
        for cp in weight_copies(expert, slot):
            cp.start()

    first = jnp.logical_and(i < nu_ref[0], jnp.logical_or(i == 0, expert != be_ref[jnp.maximum(i - 1, 0)]))

    @pl.when(first)
    def _():
        for cp in weight_copies(expert, slot):
            cp.wait()

        @pl.when(nxt_ref[i] >= 0)
        def _():
            for cp in weight_copies(nxt_ref[i], 1 - slot):
                cp.start()

    def ffn(rows, cast):
        xb = _unpack_bf16_pairs(x_ref[0:rows, :])
        gu_parts = []
        for j in range(2 * D_FF // EXPERT_COL_CHUNK):
            cs = slice(j * EXPERT_COL_CHUNK, (j + 1) * EXPERT_COL_CHUNK)
            if cast:
                wgu_b[:, cs] = wgu_f[slot, :, cs].astype(BF16)
            gu_parts.append(_dot(xb, wgu_b[:, cs]) + bgu_ref[:, cs])
        gate = jnp.minimum(jnp.concatenate(gu_parts[:len(gu_parts) // 2], axis=1), SWIGLU_LIMIT)
        up = jnp.clip(jnp.concatenate(gu_parts[len(gu_parts) // 2:], axis=1), -SWIGLU_LIMIT, SWIGLU_LIMIT)
        act = ((up + 1.0) * (gate * _sigmoid(SWIGLU_ALPHA * gate))).astype(BF16)
        y_parts = []
        for j in range(D_MODEL // EXPERT_COL_CHUNK):
            cs = slice(j * EXPERT_COL_CHUNK, (j + 1) * EXPERT_COL_CHUNK)
            if cast:
                wdn_b[:, cs] = wdn_f[slot, :, cs].astype(BF16)
            y_parts.append(_dot(act, wdn_b[:, cs]) + bdn_ref[:, cs])
        y_ref[0:rows, :] = _pack_bf16_pairs(jnp.concatenate(y_parts, axis=1))
        if rows < tm:
            y_ref[rows:tm, :] = jnp.zeros((tm - rows, D_MODEL // 2), U32)

    piece = tm // EXPERT_ROW_SPLITS
    for q in range(1, EXPERT_ROW_SPLITS + 1):
        in_q = jnp.logical_and(valid > (q - 1) * piece, valid <= q * piece)
        for cast in (True, False):
            @pl.when(jnp.logical_and(in_q, first == cast))
            def _(q=q, cast=cast):
                ffn(q * piece, cast)

    @pl.when(valid == 0)
    def _():
        y_ref[...] = jnp.zeros_like(y_ref)


def _experts(block_e, n_used, block_valid, block_slot, block_next, xs, wgu, bgu, wdn, bdn, tm):
    n_rows = xs.shape[0]
    return pl.pallas_call(
        _expert_body,
        grid_spec=pltpu.PrefetchScalarGridSpec(
            num_scalar_prefetch=5, grid=(n_rows // tm,),
            in_specs=[pl.BlockSpec((tm, D_MODEL // 2), lambda i, be, nu, *_: (jnp.minimum(i, nu[0] - 1), 0)),
                      pl.BlockSpec(memory_space=pl.ANY),
                      pl.BlockSpec((None, 1, 2 * D_FF), lambda i, be, *_: (be[i], 0, 0)),
                      pl.BlockSpec(memory_space=pl.ANY),
                      pl.BlockSpec((None, 1, D_MODEL), lambda i, be, *_: (be[i], 0, 0))],
            out_specs=pl.BlockSpec((tm, D_MODEL // 2), lambda i, *_: (i, 0)),
            scratch_shapes=[pltpu.VMEM((2, D_MODEL, 2 * D_FF), F32), pltpu.VMEM((2, D_FF, D_MODEL), F32),
                            pltpu.VMEM((D_MODEL, 2 * D_FF), BF16), pltpu.VMEM((D_FF, D_MODEL), BF16),
                            pltpu.SemaphoreType.DMA((2, 2))]),
        out_shape=jax.ShapeDtypeStruct((n_rows, D_MODEL // 2), U32),
        compiler_params=_cparams("arbitrary"),
        name="moe_experts",
    )(block_e, n_used, block_valid, block_slot, block_next, xs, wgu, bgu, wdn, bdn)


def _combine_body(tt, cnt_ref, off_ref, base_ref, loc_ref, g_ref, h_ref, hs_ref, gfin_ref, ys_hbm,
                  y_ref, ysmp_ref, buf_ref, sems):
    i = pl.program_id(0)
    last = pl.num_programs(0) - 1
    r = _sorted_rows(tt)

    def copies(tile, wait):
        b = tile % 2

        def make_copy(lo, go, size):
            return pltpu.make_async_copy(ys_hbm.at[pl.ds(go, size), :], buf_ref.at[b, pl.ds(lo, size), :],
                                         sems.at[b])

        _run_copies(tt, tile, cnt_ref, off_ref, base_ref, make_copy, wait)

    @pl.when(i == 0)
    def _():
        buf_ref[...] = jnp.zeros_like(buf_ref)
        copies(0, False)

    @pl.when(i < last)
    def _():
        copies(i + 1, False)

    copies(i, True)
    loc = loc_ref[...]
    gates = g_ref[...]
    slot_i = lax.broadcasted_iota(I32, (r, tt), 0)
    gmat = jnp.where(slot_i == loc[0:1, :], gates[0:1, :], 0.0)
    for k in range(1, TOP_K):
        gmat = gmat + jnp.where(slot_i == loc[k:k + 1, :], gates[k:k + 1, :], 0.0)
    h = jnp.where(i == last, _pad_rows(hs_ref[...], tt), h_ref[...])
    moe = lax.dot_general(gmat.astype(BF16), _unpack_bf16_pairs(buf_ref[i % 2]), TN_DIMS,
                          preferred_element_type=F32)
    y = _rms(h + moe, gfin_ref[...])

    @pl.when(i < last)
    def _():
        y_ref[...] = y

    @pl.when(i == last)
    def _():
        ysmp_ref[...] = y[0:ysmp_ref.shape[0], :]


def _combine(cnt, off, base, loc, gates, h2_p, h2_s, norm_final, ys, tt):
    ntp = h2_p.shape[0] // tt
    tok = pl.BlockSpec((tt, D_MODEL), lambda i: (jnp.minimum(i, ntp - 1), 0))
    smp = pl.BlockSpec(h2_s.shape, lambda i: (0, 0))
    tk = pl.BlockSpec((SUBLANES, tt), lambda i: (0, i))
    smem = pl.BlockSpec(memory_space=pltpu.SMEM)
    return pl.pallas_call(
        functools.partial(_combine_body, tt),
        grid_spec=pltpu.PrefetchScalarGridSpec(
            num_scalar_prefetch=0, grid=(ntp + 1,),
            in_specs=[smem, smem, smem, tk, tk, tok, smp,
                      pl.BlockSpec((1, D_MODEL), lambda i: (0, 0)),
                      pl.BlockSpec(memory_space=pl.ANY)],
            out_specs=[tok, smp],
            scratch_shapes=[pltpu.VMEM((2, _sorted_rows(tt), D_MODEL // 2), U32), pltpu.SemaphoreType.DMA((2,))]),
        out_shape=[jax.ShapeDtypeStruct(h2_p.shape, F32), jax.ShapeDtypeStruct(h2_s.shape, F32)],
        compiler_params=_cparams("arbitrary"),
        name="moe_combine",
    )(cnt, off, base, loc, gates, h2_p, h2_s, norm_final, ys)


def _moe_and_final_norm(hn_p, logits_p, h2_p, hn_s, logits_s, h2_s, w, tt, tm):
    nt = hn_p.shape[0] // tt + 1
    t = hn_p.shape[0] + hn_s.shape[0]
    gates, loc, cnt3, off3 = _route(logits_p, logits_s, tt)
    cnt = cnt3[:, :, 0]
    counts = jnp.sum(cnt, axis=0)
    padded = (counts + tm - 1) // tm * tm
    pad_end = jnp.cumsum(padded)
    start = pad_end - padded
    off = off3[:, :, 0]
    base = (start[None, :] + jnp.cumsum(cnt, axis=0) - cnt).astype(I32)
    n_blocks = (t * TOP_K + nt * N_EXPERTS * (SUBLANES - 1) + N_EXPERTS * (tm - 1)) // tm
    n_rows = n_blocks * tm
    block_start = jnp.arange(n_blocks, dtype=I32) * tm
    block_e = jnp.minimum(jnp.sum(block_start[:, None] >= pad_end[None, :], axis=-1), N_EXPERTS - 1).astype(I32)
    n_used = (pad_end[-1:] // tm).astype(I32)
    zero_starts = jnp.concatenate([start + counts, block_start]).astype(I32)
    zero_sizes = jnp.concatenate([padded - counts,
                                  jnp.where(block_start >= pad_end[-1], tm, 0)]).astype(I32)
    xs = _dispatch(cnt, off, base, zero_starts, zero_sizes, loc, hn_p, hn_s, n_rows, tm, tt)
    e_ids = jnp.arange(N_EXPERTS, dtype=I32)
    block_hot = block_e[:, None] == e_ids[None, :]

    def per_block(table):
        return jnp.sum(jnp.where(block_hot, table[None, :], 0), axis=1).astype(I32)

    block_valid = jnp.clip(per_block(start + counts) - block_start, 0, tm)
    block_valid = jnp.where(block_start < pad_end[-1], block_valid, 0).astype(I32)
    present = padded > 0
    later = present[None, :] & (e_ids[None, :] > e_ids[:, None])
    next_e = jnp.min(jnp.where(later, e_ids[None, :], N_EXPERTS), axis=1)
    next_e = jnp.where(next_e < N_EXPERTS, next_e, -1).astype(I32)
    before = present[None, :] & (e_ids[None, :] < e_ids[:, None])
    run_slot = (jnp.sum(before.astype(I32), axis=1) % 2).astype(I32)
    ys = _experts(block_e, n_used, block_valid, per_block(run_slot), per_block(next_e), xs, w["w_gate_up"], w["b_gate_up"], w["w_down"], w["b_down"], tm)
    return _combine(cnt, off, base, loc, gates, h2_p, h2_s, w["norm_final"], ys, tt)


def kernel(x_prompt, x_sample, mem_prompt, state_ssm, state_mamba_conv, state_short_conv, cache_mem_k, cache_mem_v, norm_mix, w_in, w_mconv, b_mconv, dt_bias, a_log, d_skip, norm_ssm, w_sconv, w_out, norm_xattn, norm_mem, w_xq, w_xk, w_xv, w_xo, norm_moe, w_router, b_router, w_gate_up, b_gate_up, w_down, b_down, norm_final):
    nbp, seq, _ = x_prompt.shape
    nbs = x_sample.shape[0]
    dt_lo = SSM_INNER + SSM_CONV_DIM
    w_in0 = w_in[0]
    w_dt = w_in0[:, dt_lo:dt_lo + SSM_HEADS]
    w = {
        "norm_mix": norm_mix, "norm_ssm": norm_ssm, "norm_xattn": norm_xattn, "norm_moe": norm_moe,
        "norm_final": norm_final.reshape(1, D_MODEL),
        "w_a": w_in0[:, :dt_lo].astype(BF16),
        "w_dt": w_dt.astype(BF16), "w_dt_t": w_dt.T.astype(BF16),
        "w_b": w_in0[:, dt_lo + SSM_HEADS:].astype(BF16),
        "w_mconv": w_mconv[0], "b_mconv": b_mconv,
        "dt_bias": dt_bias, "dt_bias_t": dt_bias.reshape(SSM_HEADS, 1),
        "a_log": a_log, "a_log_t": a_log.reshape(SSM_HEADS, 1),
        "d_skip": jnp.repeat(d_skip, SSM_HEAD_DIM, axis=1),
        "w_sconv": w_sconv[0], "w_out": w_out[0].astype(BF16),
        "w_xq": w_xq[0].astype(BF16), "w_xo": w_xo[0].astype(BF16),
        "w_router": w_router[0].T.astype(BF16), "b_router": b_router.reshape(N_EXPERTS, 1),
        "w_gate_up": w_gate_up[0], "b_gate_up": b_gate_up[0].reshape(N_EXPERTS, 1, 2 * D_FF),
        "w_down": w_down[0], "b_down": b_down[0].reshape(N_EXPERTS, 1, D_MODEL),
    }

    k_p, v_p, kb, vb = _mem_kv(mem_prompt.reshape(nbp * N_MEM, D_MODEL), norm_mem,
                               w_xk[0].astype(BF16), w_xv[0].astype(BF16))
    h1, ssm_p, mconv_p, sconv_p = _prompt_mixer(x_prompt.reshape(nbp * seq, D_MODEL), nbp, w)
    h2, hn, logits = _prompt_attn(h1, kb, vb, nbp, w)

    xs2 = x_sample.reshape(nbs, D_MODEL)
    mstate_t = jnp.transpose(state_mamba_conv[0], (1, 0, 2))
    sstate_t = jnp.transpose(state_short_conv[0], (1, 0, 2))
    z, xs_, dtx, dec, bm, cm, yb, sga, mnew_t, snew_t = _sample_proj(xs2, mstate_t, sstate_t, w)
    ssm_s, y_s = _sample_state(dec, state_ssm[0].reshape(nbs, SSM_INNER, SSM_STATE), dtx, bm, cm)
    h1s, q_s = _sample_fin1(xs2, y_s, xs_, z, yb, sga, w)
    o_s = _sample_attn(q_s.reshape(nbs, 1, D_MODEL),
                       cache_mem_k[0], cache_mem_v[0])
    h2s, hns, logits_s = _sample_fin2(h1s, o_s.reshape(nbs, D_MODEL), w)
    y_prompt, y_sample = _moe_and_final_norm(hn, logits, h2, hns, logits_s, h2s, w, MIX_TILE, MOE_ROW_TILE)

    return (y_prompt.reshape(nbp, seq, D_MODEL),
            y_sample.reshape(nbs, 1, D_MODEL),
            ssm_p.reshape(1, nbp, SSM_HEADS, SSM_HEAD_DIM, SSM_STATE),
            mconv_p[None], sconv_p[None],
            k_p[None], v_p[None],
            ssm_s.reshape(1, nbs, SSM_HEADS, SSM_HEAD_DIM, SSM_STATE),
            jnp.transpose(mnew_t, (1, 0, 2))[None],
            jnp.transpose(snew_t, (1, 0, 2))[None])
```
